```python
import jax, jax.numpy as jnp
from jax import lax
import numpy as np

D_MODEL = 1024
BATCH = 8
SEQ = 2048
DEPTH = 4

HEAD_DIM = 64
N_HEADS = D_MODEL // HEAD_DIM
CONV_HEADS = N_HEADS // 4
SG_HEADS = N_HEADS // 4
SB_HEADS = N_HEADS // 2
CONV_WIDTH = CONV_HEADS * HEAD_DIM
SG_WIDTH = SG_HEADS * HEAD_DIM
SB_WIDTH = SB_HEADS * HEAD_DIM
MIX_WIDTH = CONV_WIDTH + SG_WIDTH + SB_WIDTH
CONV_KERNEL = 31
SG_CHUNK = 128
SB_BLOCK = 128
OFF_CONV = 0
OFF_SG = OFF_CONV + 2 * CONV_WIDTH
OFF_SB = OFF_SG + 2 * SG_WIDTH
IN_WIDTH = OFF_SB + 3 * SB_WIDTH
FFN_HIDDEN = ((8 * D_MODEL + 3 * 256 - 1) // (3 * 256)) * 256
RMS_EPS = 1e-6
LN_EPS = 1e-5

kernel_name = "hybrid_conv_gmlp_stickbreaking_trunk"


def rms_norm(x, g):
    xf = x.astype(jnp.float32)
    y = xf * lax.rsqrt(jnp.mean(xf * xf, axis=-1, keepdims=True) + RMS_EPS)
    return (y * g.astype(jnp.float32)).astype(x.dtype)


def layer_norm(x, g, b):
    xf = x.astype(jnp.float32)
    mu = jnp.mean(xf, axis=-1, keepdims=True)
    xc = xf - mu
    var = jnp.mean(xc * xc, axis=-1, keepdims=True)
    y = xc * lax.rsqrt(var + LN_EPS) * g.astype(jnp.float32) + b.astype(jnp.float32)
    return y.astype(x.dtype)


def conv_module(val, gate, conv_w, conv_b, ln_g, ln_b):
    h = val * jax.nn.sigmoid(gate)
    h = lax.conv_general_dilated(
        h, conv_w[:, None, :].astype(h.dtype), window_strides=(1,),
        padding=[(CONV_KERNEL - 1, 0)],
        dimension_numbers=('NWC', 'WIO', 'NWC'),
        feature_group_count=CONV_WIDTH) + conv_b
    h = layer_norm(h, ln_g, ln_b)
    return jax.nn.silu(h)


def spatial_gating(uv, ln_g, ln_b, sg_w, sg_b):
    uv = jax.nn.gelu(uv, approximate=False)
    u, v = uv[..., :SG_WIDTH], uv[..., SG_WIDTH:]
    v = layer_norm(v, ln_g, ln_b)
    bsz, seq, _ = v.shape
    n_chunks = seq // SG_CHUNK
    v = v.reshape(bsz, n_chunks, SG_CHUNK, SG_HEADS, HEAD_DIM)
    causal = jnp.tril(jnp.ones((SG_CHUNK, SG_CHUNK), dtype=bool))
    w = jnp.where(causal[None], sg_w, 0)
    mixed = jnp.einsum('hts,bnshd->bnthd', w, v) + sg_b.T[None, None, :, :, None]
    return u * mixed.reshape(bsz, seq, SG_WIDTH)


def stick_breaking_attention(q, k, v):
    scale = HEAD_DIM ** -0.5
    seq = q.shape[2]
    outs = []
    for blk in range(seq // SB_BLOCK):
        t0 = blk * SB_BLOCK
        kv_len = t0 + SB_BLOCK
        qb = q[:, :, t0:kv_len].astype(jnp.float32)
        kb = k[:, :, :kv_len].astype(jnp.float32)
        vb = v[:, :, :kv_len].astype(jnp.float32)
        z = jnp.einsum('bhtd,bhsd->bhts', qb, kb) * scale
        t_pos = t0 + jnp.arange(SB_BLOCK)
        s_pos = jnp.arange(kv_len)
        causal = s_pos[None, :] < t_pos[:, None]
        log_not_beta = jnp.where(causal, jax.nn.log_sigmoid(-z), 0.0)
        between = lax.cumsum(log_not_beta, axis=3, reverse=True) - log_not_beta
        att = jnp.where(causal, jnp.exp(jax.nn.log_sigmoid(z) + between), 0.0)
        outs.append(jnp.einsum('bhts,bhsd->bhtd', att, vb))
    return jnp.concatenate(outs, axis=2).astype(v.dtype)


def hybrid_layer(x, mix_norm_g, w_in, conv_w, conv_b, conv_ln_g, conv_ln_b,
                 sg_ln_g, sg_ln_b, sg_w, sg_b, q_norm_g, k_norm_g, out_norm_g,
                 w_out, ffn_norm_g, w_gate_up, w_down):
    bsz, seq, _ = x.shape
    h = rms_norm(x, mix_norm_g)
    proj = jnp.einsum('bsd,de->bse', h, w_in)

    y_conv = conv_module(proj[..., OFF_CONV:OFF_CONV + CONV_WIDTH],
                         proj[..., OFF_CONV + CONV_WIDTH:OFF_SG],
                         conv_w, conv_b, conv_ln_g, conv_ln_b)

    y_sg = spatial_gating(proj[..., OFF_SG:OFF_SB], sg_ln_g, sg_ln_b, sg_w, sg_b)

    qkv = proj[..., OFF_SB:].reshape(bsz, seq, 3, SB_HEADS, HEAD_DIM)
    q = rms_norm(qkv[:, :, 0], q_norm_g).transpose(0, 2, 1, 3)
    k = rms_norm(qkv[:, :, 1], k_norm_g).transpose(0, 2, 1, 3)
    v = qkv[:, :, 2].transpose(0, 2, 1, 3)
    y_sb = stick_breaking_attention(q, k, v).transpose(0, 2, 1, 3).reshape(bsz, seq, SB_WIDTH)

    y = jnp.concatenate([
        rms_norm(y_conv, out_norm_g[:CONV_WIDTH]),
        rms_norm(y_sg, out_norm_g[CONV_WIDTH:CONV_WIDTH + SG_WIDTH]),
        rms_norm(y_sb, out_norm_g[CONV_WIDTH + SG_WIDTH:]),
    ], axis=-1)
    x = x + jnp.einsum('bse,ed->bsd', y, w_out)

    h = rms_norm(x, ffn_norm_g)
    gu = jnp.einsum('bsd,df->bsf', h, w_gate_up)
    act = jax.nn.silu(gu[..., :FFN_HIDDEN]) * gu[..., FFN_HIDDEN:]
    return x + jnp.einsum('bsf,fd->bsd', act, w_down)


def _fwd_setup_inputs(seed: int = 0) -> dict:
    key = jax.random.key(seed)
    ks = jax.random.split(key, 20)
    f32 = jnp.float32
    nrm = lambda k, shape: jax.random.normal(k, shape, dtype=f32)
    L = DEPTH
    return {
        "x": nrm(ks[0], (BATCH, SEQ, D_MODEL)),
        "mix_norm_g": 1.0 + 0.02 * nrm(ks[1], (L, D_MODEL)),
        "w_in": nrm(ks[2], (L, D_MODEL, IN_WIDTH)) * D_MODEL ** -0.5,
        "conv_w": nrm(ks[3], (L, CONV_KERNEL, CONV_WIDTH)) * CONV_KERNEL ** -0.5,
        "conv_b": 0.02 * nrm(ks[4], (L, CONV_WIDTH)),
        "conv_ln_g": 1.0 + 0.02 * nrm(ks[5], (L, CONV_WIDTH)),
        "conv_ln_b": 0.02 * nrm(ks[6], (L, CONV_WIDTH)),
        "sg_ln_g": 1.0 + 0.02 * nrm(ks[7], (L, SG_WIDTH)),
        "sg_ln_b": 0.02 * nrm(ks[8], (L, SG_WIDTH)),
        "sg_w": nrm(ks[9], (L, SG_HEADS, SG_CHUNK, SG_CHUNK)) * SG_CHUNK ** -0.5,
        "sg_b": 1.0 + 0.1 * nrm(ks[10], (L, SG_HEADS, SG_CHUNK)),
        "q_norm_g": 1.0 + 0.02 * nrm(ks[11], (L, HEAD_DIM)),
        "k_norm_g": 1.0 + 0.02 * nrm(ks[12], (L, HEAD_DIM)),
        "out_norm_g": 1.0 + 0.02 * nrm(ks[13], (L, MIX_WIDTH)),
        "w_out": nrm(ks[14], (L, MIX_WIDTH, D_MODEL)) * MIX_WIDTH ** -0.5,
        "ffn_norm_g": 1.0 + 0.02 * nrm(ks[15], (L, D_MODEL)),
        "w_gate_up": nrm(ks[16], (L, D_MODEL, 2 * FFN_HIDDEN)) * D_MODEL ** -0.5,
        "w_down": nrm(ks[17], (L, FFN_HIDDEN, D_MODEL)) * FFN_HIDDEN ** -0.5,
    }


def _fwd_reference(x, mix_norm_g, w_in, conv_w, conv_b, conv_ln_g, conv_ln_b,
              sg_ln_g, sg_ln_b, sg_w, sg_b, q_norm_g, k_norm_g, out_norm_g,
              w_out, ffn_norm_g, w_gate_up, w_down):
    for l in range(DEPTH):
        x = hybrid_layer(x, mix_norm_g[l], w_in[l], conv_w[l], conv_b[l],
                         conv_ln_g[l], conv_ln_b[l], sg_ln_g[l], sg_ln_b[l],
                         sg_w[l], sg_b[l], q_norm_g[l], k_norm_g[l], out_norm_g[l],
                         w_out[l], ffn_norm_g[l], w_gate_up[l], w_down[l])
    return x


import jax as _jax
import jax.numpy as _jnp

TWIN_FORMAT = 'train_step'
FWD_PARAMS = ['x', 'mix_norm_g', 'w_in', 'conv_w', 'conv_b', 'conv_ln_g', 'conv_ln_b', 'sg_ln_g', 'sg_ln_b', 'sg_w', 'sg_b', 'q_norm_g', 'k_norm_g', 'out_norm_g', 'w_out', 'ffn_norm_g', 'w_gate_up', 'w_down']
TWIN_WEIGHTS = ['mix_norm_g', 'w_in', 'conv_w', 'conv_b', 'conv_ln_g', 'conv_ln_b', 'sg_ln_g', 'sg_ln_b', 'sg_w', 'sg_b', 'q_norm_g', 'k_norm_g', 'out_norm_g', 'w_out', 'ffn_norm_g', 'w_gate_up', 'w_down']
TWIN_DIFF_INPUT = 'x'
TWIN_INPUTS = ['x', 'mix_norm_g', 'w_in', 'conv_w', 'conv_b', 'conv_ln_g', 'conv_ln_b', 'sg_ln_g', 'sg_ln_b', 'sg_w', 'sg_b', 'q_norm_g', 'k_norm_g', 'out_norm_g', 'w_out', 'ffn_norm_g', 'w_gate_up', 'w_down', 'loss_target', 'm_mix_norm_g', 'm_w_in', 'm_conv_w', 'm_conv_b', 'm_conv_ln_g', 'm_conv_ln_b', 'm_sg_ln_g', 'm_sg_ln_b', 'm_sg_w', 'm_sg_b', 'm_q_norm_g', 'm_k_norm_g', 'm_out_norm_g', 'm_w_out', 'm_ffn_norm_g', 'm_w_gate_up', 'm_w_down', 'v_mix_norm_g', 'v_w_in', 'v_conv_w', 'v_conv_b', 'v_conv_ln_g', 'v_conv_ln_b', 'v_sg_ln_g', 'v_sg_ln_b', 'v_sg_w', 'v_sg_b', 'v_q_norm_g', 'v_k_norm_g', 'v_out_norm_g', 'v_w_out', 'v_ffn_norm_g', 'v_w_gate_up', 'v_w_down']
TWIN_OUTPUTS = ['loss', 'grad_x', 'grad_mix_norm_g', 'grad_w_in', 'grad_conv_w', 'grad_conv_b', 'grad_conv_ln_g', 'grad_conv_ln_b', 'grad_sg_ln_g', 'grad_sg_ln_b', 'grad_sg_w', 'grad_sg_b', 'grad_q_norm_g', 'grad_k_norm_g', 'grad_out_norm_g', 'grad_w_out', 'grad_ffn_norm_g', 'grad_w_gate_up', 'grad_w_down', 'delta_mix_norm_g', 'delta_w_in', 'delta_conv_w', 'delta_conv_b', 'delta_conv_ln_g', 'delta_conv_ln_b', 'delta_sg_ln_g', 'delta_sg_ln_b', 'delta_sg_w', 'delta_sg_b', 'delta_q_norm_g', 'delta_k_norm_g', 'delta_out_norm_g', 'delta_w_out', 'delta_ffn_norm_g', 'delta_w_gate_up', 'delta_w_down', 'new_m_mix_norm_g', 'new_m_w_in', 'new_m_conv_w', 'new_m_conv_b', 'new_m_conv_ln_g', 'new_m_conv_ln_b', 'new_m_sg_ln_g', 'new_m_sg_ln_b', 'new_m_sg_w', 'new_m_sg_b', 'new_m_q_norm_g', 'new_m_k_norm_g', 'new_m_out_norm_g', 'new_m_w_out', 'new_m_ffn_norm_g', 'new_m_w_gate_up', 'new_m_w_down', 'new_v_mix_norm_g', 'new_v_w_in', 'new_v_conv_w', 'new_v_conv_b', 'new_v_conv_ln_g', 'new_v_conv_ln_b', 'new_v_sg_ln_g', 'new_v_sg_ln_b', 'new_v_sg_w', 'new_v_sg_b', 'new_v_q_norm_g', 'new_v_k_norm_g', 'new_v_out_norm_g', 'new_v_w_out', 'new_v_ffn_norm_g', 'new_v_w_gate_up', 'new_v_w_down']
TWIN_LEAF_KINDS = {'loss': 'loss', 'grad_x': 'grad_x', 'grad_mix_norm_g': 'grad_w', 'grad_w_in': 'grad_w', 'grad_conv_w': 'grad_w', 'grad_conv_b': 'grad_w', 'grad_conv_ln_g': 'grad_w', 'grad_conv_ln_b': 'grad_w', 'grad_sg_ln_g': 'grad_w', 'grad_sg_ln_b': 'grad_w', 'grad_sg_w': 'grad_w', 'grad_sg_b': 'grad_w', 'grad_q_norm_g': 'grad_w', 'grad_k_norm_g': 'grad_w', 'grad_out_norm_g': 'grad_w', 'grad_w_out': 'grad_w', 'grad_ffn_norm_g': 'grad_w', 'grad_w_gate_up': 'grad_w', 'grad_w_down': 'grad_w', 'delta_mix_norm_g': 'delta_w', 'delta_w_in': 'delta_w', 'delta_conv_w': 'delta_w', 'delta_conv_b': 'delta_w', 'delta_conv_ln_g': 'delta_w', 'delta_conv_ln_b': 'delta_w', 'delta_sg_ln_g': 'delta_w', 'delta_sg_ln_b': 'delta_w', 'delta_sg_w': 'delta_w', 'delta_sg_b': 'delta_w', 'delta_q_norm_g': 'delta_w', 'delta_k_norm_g': 'delta_w', 'delta_out_norm_g': 'delta_w', 'delta_w_out': 'delta_w', 'delta_ffn_norm_g': 'delta_w', 'delta_w_gate_up': 'delta_w', 'delta_w_down': 'delta_w', 'new_m_mix_norm_g': 'new_m', 'new_m_w_in': 'new_m', 'new_m_conv_w': 'new_m', 'new_m_conv_b': 'new_m', 'new_m_conv_ln_g': 'new_m', 'new_m_conv_ln_b': 'new_m', 'new_m_sg_ln_g': 'new_m', 'new_m_sg_ln_b': 'new_m', 'new_m_sg_w': 'new_m', 'new_m_sg_b': 'new_m', 'new_m_q_norm_g': 'new_m', 'new_m_k_norm_g': 'new_m', 'new_m_out_norm_g': 'new_m', 'new_m_w_out': 'new_m', 'new_m_ffn_norm_g': 'new_m', 'new_m_w_gate_up': 'new_m', 'new_m_w_down': 'new_m', 'new_v_mix_norm_g': 'new_v', 'new_v_w_in': 'new_v', 'new_v_conv_w': 'new_v', 'new_v_conv_b': 'new_v', 'new_v_conv_ln_g': 'new_v', 'new_v_conv_ln_b': 'new_v', 'new_v_sg_ln_g': 'new_v', 'new_v_sg_ln_b': 'new_v', 'new_v_sg_w': 'new_v', 'new_v_sg_b': 'new_v', 'new_v_q_norm_g': 'new_v', 'new_v_k_norm_g': 'new_v', 'new_v_out_norm_g': 'new_v', 'new_v_w_out': 'new_v', 'new_v_ffn_norm_g': 'new_v', 'new_v_w_gate_up': 'new_v', 'new_v_w_down': 'new_v'}


def _forward(args):
    return _fwd_reference(*[args[k] for k in FWD_PARAMS])


def _output_shape():
    out = _jax.eval_shape(lambda: _forward(_fwd_setup_inputs(0)))
    return out.shape, out.dtype

N_MICROBATCH = 1
ADAM_LR = 0.001
ADAM_B1 = 0.9
ADAM_B2 = 0.999
ADAM_EPS = 1e-08
ADAM_WD = 0.01
ADAM_STEP = 10
PER_EXAMPLE_BATCH_AXIS = {'x': 0, 'loss_target': 0}
SHARED_INPUTS = []
_WEIGHT_DTYPES = {'mix_norm_g': _jnp.float32, 'w_in': _jnp.float32, 'conv_w': _jnp.float32, 'conv_b': _jnp.float32, 'conv_ln_g': _jnp.float32, 'conv_ln_b': _jnp.float32, 'sg_ln_g': _jnp.float32, 'sg_ln_b': _jnp.float32, 'sg_w': _jnp.float32, 'sg_b': _jnp.float32, 'q_norm_g': _jnp.float32, 'k_norm_g': _jnp.float32, 'out_norm_g': _jnp.float32, 'w_out': _jnp.float32, 'ffn_norm_g': _jnp.float32, 'w_gate_up': _jnp.float32, 'w_down': _jnp.float32}
MOMENT_SCALE = {'mix_norm_g': 2.104327e+00, 'w_in': 1.351463e+00, 'conv_w': 1.967070e+00, 'conv_b': 2.127444e+01, 'conv_ln_g': 8.464642e+00, 'conv_ln_b': 1.194781e+01, 'sg_ln_g': 3.997403e-01, 'sg_ln_b': 3.996414e-01, 'sg_w': 2.327279e-01, 'sg_b': 3.699247e-01, 'q_norm_g': 7.424290e-01, 'k_norm_g': 7.409946e-01, 'out_norm_g': 1.724015e+01, 'w_out': 4.014398e+00, 'ffn_norm_g': 1.218398e+01, 'w_gate_up': 4.355842e-01, 'w_down': 7.079007e-01}


def _to_microbatches(a, axis):
    t = _jnp.moveaxis(a, axis, 0)
    t = t.reshape((N_MICROBATCH, t.shape[0] // N_MICROBATCH) + t.shape[1:])
    return _jnp.moveaxis(t, 1, axis + 1)


def setup_inputs(seed: int = 0) -> dict:
    inp = _fwd_setup_inputs(seed)
    key = _jax.random.fold_in(_jax.random.key(seed), 7919)
    shape, _ = _output_shape()
    out = dict(inp)
    out["loss_target"] = _jax.random.normal(_jax.random.fold_in(key, 0), shape, _jnp.float32)
    for i, name in enumerate(TWIN_WEIGHTS):
        w = inp[name].astype(_jnp.float32)
        if MOMENT_SCALE is None:
            s = _jnp.sqrt(_jnp.mean(_jnp.square(w)) + 1e-30)
        else:
            s = MOMENT_SCALE[name]
        km, kv = _jax.random.split(_jax.random.fold_in(key, i + 1))
        out[name] = w
        out["m_" + name] = s * _jax.random.normal(km, w.shape, _jnp.float32)
        out["v_" + name] = (s * s) * _jax.random.uniform(kv, w.shape, _jnp.float32, 0.5, 1.5)
    if N_MICROBATCH > 1:
        for name, axis in PER_EXAMPLE_BATCH_AXIS.items():
            out[name] = _to_microbatches(out[name], axis)
    return {'x': out['x'], 'mix_norm_g': out['mix_norm_g'], 'w_in': out['w_in'], 'conv_w': out['conv_w'], 'conv_b': out['conv_b'], 'conv_ln_g': out['conv_ln_g'], 'conv_ln_b': out['conv_ln_b'], 'sg_ln_g': out['sg_ln_g'], 'sg_ln_b': out['sg_ln_b'], 'sg_w': out['sg_w'], 'sg_b': out['sg_b'], 'q_norm_g': out['q_norm_g'], 'k_norm_g': out['k_norm_g'], 'out_norm_g': out['out_norm_g'], 'w_out': out['w_out'], 'ffn_norm_g': out['ffn_norm_g'], 'w_gate_up': out['w_gate_up'], 'w_down': out['w_down'], 'loss_target': out['loss_target'], 'm_mix_norm_g': out['m_mix_norm_g'], 'm_w_in': out['m_w_in'], 'm_conv_w': out['m_conv_w'], 'm_conv_b': out['m_conv_b'], 'm_conv_ln_g': out['m_conv_ln_g'], 'm_conv_ln_b': out['m_conv_ln_b'], 'm_sg_ln_g': out['m_sg_ln_g'], 'm_sg_ln_b': out['m_sg_ln_b'], 'm_sg_w': out['m_sg_w'], 'm_sg_b': out['m_sg_b'], 'm_q_norm_g': out['m_q_norm_g'], 'm_k_norm_g': out['m_k_norm_g'], 'm_out_norm_g': out['m_out_norm_g'], 'm_w_out': out['m_w_out'], 'm_ffn_norm_g': out['m_ffn_norm_g'], 'm_w_gate_up': out['m_w_gate_up'], 'm_w_down': out['m_w_down'], 'v_mix_norm_g': out['v_mix_norm_g'], 'v_w_in': out['v_w_in'], 'v_conv_w': out['v_conv_w'], 'v_conv_b': out['v_conv_b'], 'v_conv_ln_g': out['v_conv_ln_g'], 'v_conv_ln_b': out['v_conv_ln_b'], 'v_sg_ln_g': out['v_sg_ln_g'], 'v_sg_ln_b': out['v_sg_ln_b'], 'v_sg_w': out['v_sg_w'], 'v_sg_b': out['v_sg_b'], 'v_q_norm_g': out['v_q_norm_g'], 'v_k_norm_g': out['v_k_norm_g'], 'v_out_norm_g': out['v_out_norm_g'], 'v_w_out': out['v_w_out'], 'v_ffn_norm_g': out['v_ffn_norm_g'], 'v_w_gate_up': out['v_w_gate_up'], 'v_w_down': out['v_w_down']}


def _loss(weights, diff, rest, loss_target):
    with _jax.named_scope("forward"):
        args = {**rest, TWIN_DIFF_INPUT: diff, **{k: w.astype(_WEIGHT_DTYPES[k]) for k, w in weights.items()}}
        y = _forward(args)
    with _jax.named_scope("loss_head"):
        err = _jnp.square(y.astype(_jnp.float32) - loss_target)
        return 0.5 * _jnp.sum(_jnp.mean(err, axis=-1)) if err.ndim else 0.5 * err


def _adamw(w, g, m, v):
    m = ADAM_B1 * m + (1.0 - ADAM_B1) * g
    v = ADAM_B2 * v + (1.0 - ADAM_B2) * _jnp.square(g)
    m_hat = m / (1.0 - ADAM_B1 ** ADAM_STEP)
    v_hat = v / (1.0 - ADAM_B2 ** ADAM_STEP)
    delta = -ADAM_LR * (m_hat / (_jnp.sqrt(v_hat) + ADAM_EPS) + ADAM_WD * w)
    return delta, m, v


def reference(x, mix_norm_g, w_in, conv_w, conv_b, conv_ln_g, conv_ln_b, sg_ln_g, sg_ln_b, sg_w, sg_b, q_norm_g, k_norm_g, out_norm_g, w_out, ffn_norm_g, w_gate_up, w_down, loss_target, m_mix_norm_g, m_w_in, m_conv_w, m_conv_b, m_conv_ln_g, m_conv_ln_b, m_sg_ln_g, m_sg_ln_b, m_sg_w, m_sg_b, m_q_norm_g, m_k_norm_g, m_out_norm_g, m_w_out, m_ffn_norm_g, m_w_gate_up, m_w_down, v_mix_norm_g, v_w_in, v_conv_w, v_conv_b, v_conv_ln_g, v_conv_ln_b, v_sg_ln_g, v_sg_ln_b, v_sg_w, v_sg_b, v_q_norm_g, v_k_norm_g, v_out_norm_g, v_w_out, v_ffn_norm_g, v_w_gate_up, v_w_down):
    given = dict(x=x, mix_norm_g=mix_norm_g, w_in=w_in, conv_w=conv_w, conv_b=conv_b, conv_ln_g=conv_ln_g, conv_ln_b=conv_ln_b, sg_ln_g=sg_ln_g, sg_ln_b=sg_ln_b, sg_w=sg_w, sg_b=sg_b, q_norm_g=q_norm_g, k_norm_g=k_norm_g, out_norm_g=out_norm_g, w_out=w_out, ffn_norm_g=ffn_norm_g, w_gate_up=w_gate_up, w_down=w_down, loss_target=loss_target, m_mix_norm_g=m_mix_norm_g, m_w_in=m_w_in, m_conv_w=m_conv_w, m_conv_b=m_conv_b, m_conv_ln_g=m_conv_ln_g, m_conv_ln_b=m_conv_ln_b, m_sg_ln_g=m_sg_ln_g, m_sg_ln_b=m_sg_ln_b, m_sg_w=m_sg_w, m_sg_b=m_sg_b, m_q_norm_g=m_q_norm_g, m_k_norm_g=m_k_norm_g, m_out_norm_g=m_out_norm_g, m_w_out=m_w_out, m_ffn_norm_g=m_ffn_norm_g, m_w_gate_up=m_w_gate_up, m_w_down=m_w_down, v_mix_norm_g=v_mix_norm_g, v_w_in=v_w_in, v_conv_w=v_conv_w, v_conv_b=v_conv_b, v_conv_ln_g=v_conv_ln_g, v_conv_ln_b=v_conv_ln_b, v_sg_ln_g=v_sg_ln_g, v_sg_ln_b=v_sg_ln_b, v_sg_w=v_sg_w, v_sg_b=v_sg_b, v_q_norm_g=v_q_norm_g, v_k_norm_g=v_k_norm_g, v_out_norm_g=v_out_norm_g, v_w_out=v_w_out, v_ffn_norm_g=v_ffn_norm_g, v_w_gate_up=v_w_gate_up, v_w_down=v_w_down)
    weights = {n: given[n] for n in TWIN_WEIGHTS}
    shared = {n: given[n] for n in SHARED_INPUTS}
    per_example = {n: given[n] for n in ['x']}
    grad_fn = _jax.value_and_grad(_loss, argnums=(0, 1))

    def one_microbatch(ex, loss_target):
        ex = dict(ex)
        diff = ex.pop(TWIN_DIFF_INPUT)
        return grad_fn(weights, diff, {**shared, **ex}, loss_target)

    if N_MICROBATCH == 1:
        loss, (grad_w, grad_x) = one_microbatch(per_example, given["loss_target"])
    else:
        def body(carry, xs):
            loss_sum, grad_sum = carry
            l_k, (gw_k, gx_k) = one_microbatch(xs[0], xs[1])
            with _jax.named_scope("update"):
                return (loss_sum + l_k, _jax.tree.map(_jnp.add, grad_sum, gw_k)), gx_k

        init = (_jnp.zeros((), _jnp.float32), _jax.tree.map(_jnp.zeros_like, weights))
        (loss, grad_w), grad_x = _jax.lax.scan(body, init, (per_example, given["loss_target"]))
    with _jax.named_scope("update"):
        delta_w, new_m, new_v = {}, {}, {}
        for n in TWIN_WEIGHTS:
            delta_w[n], new_m[n], new_v[n] = _adamw(weights[n], grad_w[n], given["m_" + n], given["v_" + n])
    return (loss, grad_x, *[grad_w[n] for n in TWIN_WEIGHTS], *[delta_w[n] for n in TWIN_WEIGHTS],
            *[new_m[n] for n in TWIN_WEIGHTS], *[new_v[n] for n in TWIN_WEIGHTS])
```

```python
import functools

import jax
import jax.numpy as jnp
from jax import lax
from jax.experimental import pallas as pl
from jax.experimental.pallas import tpu as pltpu

F32 = jnp.float32
BF16 = jnp.bfloat16

D_MODEL = 1024
DEPTH = 4
HEAD_DIM = 64
CONV_W = 256
SG_W = 256
SB_W = 512
IN_W = 2560
FFN_H = 2816
CONV_K = 31
CHUNK = 128
OFF_SG = 2 * CONV_W
OFF_SB = OFF_SG + 2 * SG_W
RMS_EPS = 1e-6
LN_EPS = 1e-5
N_DEV = 8
MESH = pl.DeviceIdType.MESH

ADAM_LR = 0.001
ADAM_B1 = 0.9
ADAM_B2 = 0.999
ADAM_EPS = 1e-08
ADAM_WD = 0.01
ADAM_STEP = 10

TOKEN_TILE = 256
VMEM_LIMIT = 56 * 1024 * 1024


def _cp(*sem):
    return pltpu.CompilerParams(dimension_semantics=sem or None, vmem_limit_bytes=VMEM_LIMIT)


def _dot(a, b):
    return jnp.dot(a, b, preferred_element_type=F32)


def _dot_nt(a, b):
    return lax.dot_general(a, b, (((1,), (1,)), ((), ())), preferred_element_type=F32)


def _dot_tn(a, b):
    return lax.dot_general(a, b, (((0,), (0,)), ((), ())), preferred_element_type=F32)


def _dot_split(x, m):
    hi = x.astype(BF16)
    lo = (x - hi.astype(F32)).astype(BF16)
    return _dot(hi, m) + _dot(lo, m)


def _group_mean_matrix(width, group):
    r = lax.broadcasted_iota(jnp.int32, (width, width), 0) // group
    c = lax.broadcasted_iota(jnp.int32, (width, width), 1) // group
    return jnp.where(r == c, 1.0 / group, 0.0).astype(BF16)


def _sigmoid(x):
    return 1.0 / (1.0 + jnp.exp(-x))


def _gelu(x):
    return 0.5 * x * (1.0 + lax.erf(x * (2.0 ** -0.5)))


def _gelu_grad(x):
    return 0.5 * (1.0 + lax.erf(x * (2.0 ** -0.5))) + x * jnp.exp(-0.5 * x * x) * (0.5 * (2.0 / jnp.pi) ** 0.5)


def _rms_stats(x):
    r = lax.rsqrt(jnp.mean(x * x, axis=-1, keepdims=True) + RMS_EPS)
    return r, x * r


def _rms_bwd(xh, r, g, dy):
    dxh = dy * g
    dx = r * (dxh - xh * jnp.mean(dxh * xh, axis=-1, keepdims=True))
    return dx, dy * xh


def _ln_stats(x):
    mu = jnp.mean(x, axis=-1, keepdims=True)
    xc = x - mu
    r = lax.rsqrt(jnp.mean(xc * xc, axis=-1, keepdims=True) + LN_EPS)
    return r, xc * r


def _ln_bwd(xh, r, g, dy):
    dxh = dy * g
    return r * (dxh - jnp.mean(dxh, axis=-1, keepdims=True) - xh * jnp.mean(dxh * xh, axis=-1, keepdims=True))


def _colsum(x):
    return jnp.sum(x, axis=0, keepdims=True)


def _rows(tm, n, j=0):
    return pl.BlockSpec((tm, n), lambda i: (i, j))


def _whole(shape):
    return pl.BlockSpec(shape, lambda i: (0,) * len(shape))


def _fwd_in(x, g, w, qg, kg):
    s = x.shape[0]
    tm = TOKEN_TILE

    def body(x_ref, g_ref, w_ref, qg_ref, kg_ref, proj_ref, qn_ref, kn_ref, vb_ref):
        r, xh = _rms_stats(x_ref[...])
        h = (xh * g_ref[...]).astype(BF16)
        proj = _dot(h, w_ref[...])
        proj_ref[...] = proj
        gm = _group_mean_matrix(SB_W, HEAD_DIM)
        q = proj[:, OFF_SB:OFF_SB + SB_W]
        k = proj[:, OFF_SB + SB_W:OFF_SB + 2 * SB_W]
        rq = lax.rsqrt(_dot_split(q * q, gm) + RMS_EPS)
        rk = lax.rsqrt(_dot_split(k * k, gm) + RMS_EPS)
        qn_ref[...] = (q * rq * qg_ref[...] * (HEAD_DIM ** -0.5)).astype(BF16)
        kn_ref[...] = (k * rk * kg_ref[...]).astype(BF16)
        vb_ref[...] = proj[:, OFF_SB + 2 * SB_W:].astype(BF16)

    return pl.pallas_call(
        body, name="fwd_in", grid=(s // tm,),
        in_specs=[_rows(tm, D_MODEL), _whole((1, D_MODEL)), _whole((D_MODEL, IN_W)),
                  _whole((1, SB_W)), _whole((1, SB_W))],
        out_specs=[_rows(tm, IN_W), _rows(tm, SB_W), _rows(tm, SB_W), _rows(tm, SB_W)],
        out_shape=[jax.ShapeDtypeStruct((s, IN_W), F32)] + [jax.ShapeDtypeStruct((s, SB_W), BF16)] * 3,
        compiler_params=_cp("parallel"),
    )(x, g, w, qg, kg)


def _conv_window(abuf, r0, w_ref):
    win = abuf[pl.ds(pl.multiple_of(r0 + CHUNK - 32, 32), CHUNK + 32), :]
    acc = jnp.zeros((CHUNK, CONV_W), F32)
    for k in range(CONV_K):
        sh = win if k == 0 else pltpu.roll(win, k, axis=0)
        acc = acc + sh[32:, :] * w_ref[CONV_K - 1 - k:CONV_K - k, :]
    return acc, win


def _glu_fill(p_ref, abuf, s):
    abuf[0:CHUNK, :] = jnp.zeros((CHUNK, CONV_W), F32)

    def fill(c, carry):
        r0 = pl.multiple_of(c * CHUNK, CHUNK)
        pv = p_ref[pl.ds(r0, CHUNK), :]
        abuf[pl.ds(r0 + CHUNK, CHUNK), :] = pv[:, :CONV_W] * _sigmoid(pv[:, CONV_W:])
        return carry

    lax.fori_loop(0, s // CHUNK, fill, 0)


def _fwd_conv(proj, w, b, lg, lb):
    s = proj.shape[0]

    def body(p_ref, w_ref, b_ref, lg_ref, lb_ref, y_ref, abuf):
        _glu_fill(p_ref, abuf, s)

        def chunk(c, carry):
            r0 = pl.multiple_of(c * CHUNK, CHUNK)
            acc, _ = _conv_window(abuf, r0, w_ref)
            r, xh = _ln_stats(acc + b_ref[...])
            ln = xh * lg_ref[...] + lb_ref[...]
            y_ref[pl.ds(r0, CHUNK), :] = ln * _sigmoid(ln)
            return carry

        lax.fori_loop(0, s // CHUNK, chunk, 0)

    return pl.pallas_call(
        body, name="fwd_conv", grid=(1,),
        in_specs=[pl.BlockSpec((s, 2 * CONV_W), lambda i: (0, 0)), _whole((CONV_K, CONV_W)),
                  _whole((1, CONV_W)), _whole((1, CONV_W)), _whole((1, CONV_W))],
        out_specs=_whole((s, CONV_W)),
        out_shape=jax.ShapeDtypeStruct((s, CONV_W), F32),
        scratch_shapes=[pltpu.VMEM((s + CHUNK, CONV_W), F32)],
        compiler_params=_cp("arbitrary"),
    )(proj, w, b, lg, lb)


def _sg_masks():
    row = lax.broadcasted_iota(jnp.int32, (CHUNK, CHUNK), 0)
    col = lax.broadcasted_iota(jnp.int32, (CHUNK, CHUNK), 1)
    lane_head = lax.broadcasted_iota(jnp.int32, (CHUNK, SG_W), 1) // HEAD_DIM
    return row >= col, lane_head


def _sg_mix(w_ref, bias_ref, vc, tril, lane_head):
    mixed = bias_ref[...]
    for h in range(SG_W // HEAD_DIM):
        wm = jnp.where(tril, w_ref[h], 0.0).astype(BF16)
        mixed = mixed + jnp.where(lane_head == h, _dot(wm, vc), 0.0)
    return mixed


def _fwd_sg(proj, lg, lb, w, bias):
    s = proj.shape[0]
    tm = TOKEN_TILE

    def body(p_ref, lg_ref, lb_ref, w_ref, bias_ref, y_ref):
        ge = _gelu(p_ref[...])
        u = ge[:, :SG_W]
        r, xh = _ln_stats(ge[:, SG_W:])
        vln = (xh * lg_ref[...] + lb_ref[...]).astype(BF16)
        tril, lane_head = _sg_masks()
        for c in range(tm // CHUNK):
            rows = slice(c * CHUNK, (c + 1) * CHUNK)
            y_ref[rows, :] = u[rows] * _sg_mix(w_ref, bias_ref, vln[rows], tril, lane_head)

    return pl.pallas_call(
        body, name="fwd_sg", grid=(s // tm,),
        in_specs=[_rows(tm, 2 * SG_W, 1), _whole((1, SG_W)), _whole((1, SG_W)),
                  _whole((SG_W // HEAD_DIM, CHUNK, CHUNK)), _whole((CHUNK, SG_W))],
        out_specs=_rows(tm, SG_W),
        out_shape=jax.ShapeDtypeStruct((s, SG_W), F32),
        compiler_params=_cp("parallel"),
    )(proj, lg, lb, w, bias)


def _sb_block(qh, kh, qb, kb, carry, mx):
    row = lax.broadcasted_iota(jnp.int32, (CHUNK, CHUNK), 0)
    col = lax.broadcasted_iota(jnp.int32, (CHUNK, CHUNK), 1)
    mask = (kb * CHUNK + col) < (qb * CHUNK + row)
    z = _dot_nt(qh, kh)
    sp = jnp.maximum(z, 0.0) + jnp.log(1.0 + jnp.exp(-jnp.abs(z)))
    lnb = jnp.where(mask, -sp, 0.0)
    between = _dot_split(lnb, mx) + carry
    att = jnp.where(mask, jnp.exp(z - sp + between), 0.0)
    return mask, z, sp, att, jnp.sum(lnb, axis=1, keepdims=True)


def _strict_tri(lower):
    row = lax.broadcasted_iota(jnp.int32, (CHUNK, CHUNK), 0)
    col = lax.broadcasted_iota(jnp.int32, (CHUNK, CHUNK), 1)
    return (row > col if lower else row < col).astype(BF16)


def _fwd_sb(qn, kn, vb):
    s = qn.shape[0]
    nq = s // CHUNK

    def body(q_ref, k_ref, v_ref, o_ref):
        qb = pl.program_id(1)
        mx = _strict_tri(True)
        outs = []
        for h in range(2):
            lanes = slice(h * HEAD_DIM, (h + 1) * HEAD_DIM)
            qh = q_ref[:, lanes]

            def step(i, carry):
                acc, c = carry
                kb = qb - i
                k0 = pl.multiple_of(kb * CHUNK, CHUNK)
                kh = k_ref[pl.ds(k0, CHUNK), lanes]
                vh = v_ref[pl.ds(k0, CHUNK), lanes]
                _, _, _, att, lsum = _sb_block(qh, kh, qb, kb, c, mx)
                return acc + _dot(att.astype(BF16), vh), c + lsum

            acc, _ = lax.fori_loop(0, qb + 1, step,
                                   (jnp.zeros((CHUNK, HEAD_DIM), F32), jnp.zeros((CHUNK, 1), F32)))
            outs.append(acc)
        o_ref[...] = jnp.concatenate(outs, axis=1)

    return pl.pallas_call(
        body, name="fwd_sb", grid=(SB_W // CHUNK, nq),
        in_specs=[pl.BlockSpec((CHUNK, CHUNK), lambda p, i: (i, p)),
                  pl.BlockSpec((s, CHUNK), lambda p, i: (0, p)),
                  pl.BlockSpec((s, CHUNK), lambda p, i: (0, p))],
        out_specs=pl.BlockSpec((CHUNK, CHUNK), lambda p, i: (i, p)),
        out_shape=jax.ShapeDtypeStruct((s, SB_W), F32),
        compiler_params=_cp("parallel", "parallel"),
    )(qn, kn, vb)


def _group_norms(yc, ys, yb):
    return [_rms_stats(yc), _rms_stats(ys), _rms_stats(yb)]


def _fwd_out(yc, ys, yb, g, w, x):
    s = x.shape[0]
    tm = TOKEN_TILE

    def body(yc_ref, ys_ref, yb_ref, g_ref, w_ref, x_ref, o_ref):
        stats = _group_norms(yc_ref[...], ys_ref[...], yb_ref[...])
        cat = jnp.concatenate([xh for _, xh in stats], axis=1) * g_ref[...]
        o_ref[...] = x_ref[...] + _dot(cat.astype(BF16), w_ref[...])

    return pl.pallas_call(
        body, name="fwd_out", grid=(s // tm,),
        in_specs=[_rows(tm, CONV_W), _rows(tm, SG_W), _rows(tm, SB_W), _whole((1, D_MODEL)),
                  _whole((D_MODEL, D_MODEL)), _rows(tm, D_MODEL)],
        out_specs=_rows(tm, D_MODEL),
        out_shape=jax.ShapeDtypeStruct((s, D_MODEL), F32),
        compiler_params=_cp("parallel"),
    )(yc, ys, yb, g, w, x)


def _fwd_ffn(x, g, wgu, wd):
    s = x.shape[0]
    tm = TOKEN_TILE

    def body(x_ref, g_ref, wgu_ref, wd_ref, gu_ref, o_ref):
        x = x_ref[...]
        r, xh = _rms_stats(x)
        gu = _dot((xh * g_ref[...]).astype(BF16), wgu_ref[...])
        gu_ref[...] = gu
        gate = gu[:, :FFN_H]
        act = gate * _sigmoid(gate) * gu[:, FFN_H:]
        o_ref[...] = x + _dot(act.astype(BF16), wd_ref[...])

    return pl.pallas_call(
        body, name="fwd_ffn", grid=(s // tm,),
        in_specs=[_rows(tm, D_MODEL), _whole((1, D_MODEL)),
                  pl.BlockSpec((D_MODEL, 2 * FFN_H), lambda i: (0, 0), pipeline_mode=pl.Buffered(1)),
                  pl.BlockSpec((FFN_H, D_MODEL), lambda i: (0, 0), pipeline_mode=pl.Buffered(1))],
        out_specs=[_rows(tm, 2 * FFN_H), _rows(tm, D_MODEL)],
        out_shape=[jax.ShapeDtypeStruct((s, 2 * FFN_H), F32), jax.ShapeDtypeStruct((s, D_MODEL), F32)],
        compiler_params=_cp("parallel"),
    )(x, g, wgu, wd)


def _loss_head(y, target):
    s = y.shape[0]
    tm = TOKEN_TILE

    def body(y_ref, t_ref, l_ref, d_ref):
        @pl.when(pl.program_id(0) == 0)
        def _():
            l_ref[...] = jnp.zeros_like(l_ref)

        err = y_ref[...] - t_ref[...]
        d_ref[...] = err * (1.0 / D_MODEL)
        l_ref[...] += 0.5 * jnp.sum(jnp.mean(err * err, axis=-1, keepdims=True), axis=0, keepdims=True)

    return pl.pallas_call(
        body, name="loss_head", grid=(s // tm,),
        in_specs=[_rows(tm, D_MODEL), _rows(tm, D_MODEL)],
        out_specs=[_whole((1, 1)), _rows(tm, D_MODEL)],
        out_shape=[jax.ShapeDtypeStruct((1, 1), F32), jax.ShapeDtypeStruct((s, D_MODEL), F32)],
        compiler_params=_cp("arbitrary"),
    )(y, target)


def _accumulate(ref, value):
    @pl.when(pl.program_id(0) == 0)
    def _():
        ref[...] = jnp.zeros_like(ref)

    ref[...] += value


def _bwd_ffn(dxo, gu, xm, g, wgu, wd):
    s = dxo.shape[0]
    tm = TOKEN_TILE

    def body(dxo_ref, gu_ref, xm_ref, g_ref, wgu_ref, wd_ref, dgu_ref, act_ref, h_ref, dxm_ref, dg_ref):
        dxo = dxo_ref[...]
        gu = gu_ref[...]
        gate, up = gu[:, :FFN_H], gu[:, FFN_H:]
        sg = _sigmoid(gate)
        sl = gate * sg
        act_ref[...] = (sl * up).astype(BF16)
        dact = _dot_nt(dxo.astype(BF16), wd_ref[...])
        dgate = dact * up * (sg * (1.0 + gate * (1.0 - sg)))
        dgu = jnp.concatenate([dgate, dact * sl], axis=1).astype(BF16)
        dgu_ref[...] = dgu
        dh = _dot_nt(dgu, wgu_ref[...])
        r, xh = _rms_stats(xm_ref[...])
        h_ref[...] = (xh * g_ref[...]).astype(BF16)
        dx, dgrow = _rms_bwd(xh, r, g_ref[...], dh)
        dxm_ref[...] = dxo + dx
        _accumulate(dg_ref, _colsum(dgrow))

    return pl.pallas_call(
        body, name="bwd_ffn", grid=(s // tm,),
        in_specs=[_rows(tm, D_MODEL), _rows(tm, 2 * FFN_H), _rows(tm, D_MODEL), _whole((1, D_MODEL)),
                  pl.BlockSpec((D_MODEL, 2 * FFN_H), lambda i: (0, 0), pipeline_mode=pl.Buffered(1)),
                  pl.BlockSpec((FFN_H, D_MODEL), lambda i: (0, 0), pipeline_mode=pl.Buffered(1))],
        out_specs=[_rows(tm, 2 * FFN_H), _rows(tm, FFN_H), _rows(tm, D_MODEL), _rows(tm, D_MODEL),
                   _whole((1, D_MODEL))],
        out_shape=[jax.ShapeDtypeStruct((s, 2 * FFN_H), BF16), jax.ShapeDtypeStruct((s, FFN_H), BF16),
                   jax.ShapeDtypeStruct((s, D_MODEL), BF16), jax.ShapeDtypeStruct((s, D_MODEL), F32),
                   jax.ShapeDtypeStruct((1, D_MODEL), F32)],
        compiler_params=_cp("arbitrary"),
    )(dxo, gu, xm, g, wgu, wd)


def _matmul_tn(a, b, tm, tn, out_dtype=BF16):
    s, m = a.shape
    n = b.shape[1]

    def body(a_ref, b_ref, o_ref):
        o_ref[...] = _dot_tn(a_ref[...].astype(BF16), b_ref[...].astype(BF16)).astype(out_dtype)

    return pl.pallas_call(
        body, name="weight_grad", grid=(m // tm, n // tn),
        in_specs=[pl.BlockSpec((s, tm), lambda i, j: (0, i)), pl.BlockSpec((s, tn), lambda i, j: (0, j))],
        out_specs=pl.BlockSpec((tm, tn), lambda i, j: (i, j)),
        out_shape=jax.ShapeDtypeStruct((m, n), out_dtype),
        compiler_params=_cp("parallel", "parallel"),
    )(a, b)


def _bwd_out(dxm, yc, ys, yb, g, w):
    s = dxm.shape[0]
    tm = TOKEN_TILE

    def body(dxm_ref, yc_ref, ys_ref, yb_ref, g_ref, w_ref, dyc_ref, dys_ref, dyb_ref, cat_ref, dg_ref):
        stats = _group_norms(yc_ref[...], ys_ref[...], yb_ref[...])
        g = g_ref[...]
        cat_ref[...] = (jnp.concatenate([xh for _, xh in stats], axis=1) * g).astype(BF16)
        dcat = _dot_nt(dxm_ref[...].astype(BF16), w_ref[...])
        dgs = []
        off = 0
        for (r, xh), out in zip(stats, (dyc_ref, dys_ref, dyb_ref)):
            cols = slice(off, off + xh.shape[1])
            dx, dgrow = _rms_bwd(xh, r, g[:, cols], dcat[:, cols])
            out[...] = dx
            dgs.append(_colsum(dgrow))
            off += xh.shape[1]
        _accumulate(dg_ref, jnp.concatenate(dgs, axis=1))

    return pl.pallas_call(
        body, name="bwd_out", grid=(s // tm,),
        in_specs=[_rows(tm, D_MODEL), _rows(tm, CONV_W), _rows(tm, SG_W), _rows(tm, SB_W),
                  _whole((1, D_MODEL)), _whole((D_MODEL, D_MODEL))],
        out_specs=[_rows(tm, CONV_W), _rows(tm, SG_W), _rows(tm, SB_W), _rows(tm, D_MODEL),
                   _whole((1, D_MODEL))],
        out_shape=[jax.ShapeDtypeStruct((s, CONV_W), F32), jax.ShapeDtypeStruct((s, SG_W), F32),
                   jax.ShapeDtypeStruct((s, SB_W), F32), jax.ShapeDtypeStruct((s, D_MODEL), BF16),
                   jax.ShapeDtypeStruct((1, D_MODEL), F32)],
        compiler_params=_cp("arbitrary"),
    )(dxm, yc, ys, yb, g, w)


def _bwd_sb(qn, kn, vb, dy):
    s = qn.shape[0]
    nq = s // CHUNK

    def body(q_ref, k_ref, v_ref, do_ref, dq_ref, dk_ref, dv_ref, e_buf, sig_buf):
        qb = pl.program_id(1)

        @pl.when(qb == 0)
        def _():
            dk_ref[...] = jnp.zeros_like(dk_ref)
            dv_ref[...] = jnp.zeros_like(dv_ref)

        mx = _strict_tri(True)
        ml = _strict_tri(False)
        dqs = []
        for h in range(2):
            lanes = slice(h * HEAD_DIM, (h + 1) * HEAD_DIM)
            qh = q_ref[:, lanes]
            doh = do_ref[:, lanes].astype(BF16)

            def sweep_back(i, c):
                kb = qb - i
                k0 = pl.multiple_of(kb * CHUNK, CHUNK)
                kh = k_ref[pl.ds(k0, CHUNK), lanes]
                vh = v_ref[pl.ds(k0, CHUNK), lanes]
                _, z, sp, att, lsum = _sb_block(qh, kh, qb, kb, c, mx)
                e_buf[h, kb] = att * _dot_nt(doh, vh)
                sig_buf[h, kb] = jnp.exp(z - sp)
                dv_ref[pl.ds(k0, CHUNK), lanes] += _dot_tn(att.astype(BF16), doh)
                return c + lsum

            lax.fori_loop(0, qb + 1, sweep_back, jnp.zeros((CHUNK, 1), F32))

            def sweep_fwd(kb, carry):
                dq, ce = carry
                k0 = pl.multiple_of(kb * CHUNK, CHUNK)
                kh = k_ref[pl.ds(k0, CHUNK), lanes]
                e = e_buf[h, kb]
                sig = sig_buf[h, kb]
                row = lax.broadcasted_iota(jnp.int32, (CHUNK, CHUNK), 0)
                col = lax.broadcasted_iota(jnp.int32, (CHUNK, CHUNK), 1)
                mask = (kb * CHUNK + col) < (qb * CHUNK + row)
                before = _dot_split(e, ml) + ce
                dz = jnp.where(mask, e * (1.0 - sig) - before * sig, 0.0).astype(BF16)
                dk_ref[pl.ds(k0, CHUNK), lanes] += _dot_tn(dz, qh)
                return dq + _dot(dz, kh), ce + jnp.sum(e, axis=1, keepdims=True)

            dq, _ = lax.fori_loop(0, qb + 1, sweep_fwd,
                                  (jnp.zeros((CHUNK, HEAD_DIM), F32), jnp.zeros((CHUNK, 1), F32)))
            dqs.append(dq)
        dq_ref[...] = jnp.concatenate(dqs, axis=1)

    blk = pl.BlockSpec((CHUNK, CHUNK), lambda p, i: (i, p))
    seq = pl.BlockSpec((s, CHUNK), lambda p, i: (0, p))
    return pl.pallas_call(
        body, name="bwd_sb", grid=(SB_W // CHUNK, nq),
        in_specs=[blk, seq, seq, blk],
        out_specs=[blk, seq, seq],
        out_shape=[jax.ShapeDtypeStruct((s, SB_W), F32)] * 3,
        scratch_shapes=[pltpu.VMEM((2, nq, CHUNK, CHUNK), F32), pltpu.VMEM((2, nq, CHUNK, CHUNK), F32)],
        compiler_params=_cp("parallel", "arbitrary"),
    )(qn, kn, vb, dy)


def _head_sum(row):
    acc = row[:, 0:HEAD_DIM]
    for h in range(1, SB_W // HEAD_DIM):
        acc = acc + row[:, h * HEAD_DIM:(h + 1) * HEAD_DIM]
    return acc


def _bwd_qk(proj, dqs, dkn, dv, qg, kg):
    s = proj.shape[0]
    tm = TOKEN_TILE

    def body(q_ref, k_ref, dqs_ref, dkn_ref, dv_ref, qg_ref, kg_ref, dp_ref, dqg_ref, dkg_ref, qacc, kacc):
        i = pl.program_id(0)
        gm = _group_mean_matrix(SB_W, HEAD_DIM)

        def one(x, dy, g, acc):
            r = lax.rsqrt(_dot_split(x * x, gm) + RMS_EPS)
            xh = x * r
            dxh = dy * g
            _accumulate(acc, _colsum(dy * xh))
            return r * (dxh - xh * _dot_split(dxh * xh, gm))

        dq = one(q_ref[...], dqs_ref[...] * (HEAD_DIM ** -0.5), qg_ref[...], qacc)
        dk = one(k_ref[...], dkn_ref[...], kg_ref[...], kacc)
        dp_ref[...] = jnp.concatenate([dq, dk, dv_ref[...]], axis=1).astype(BF16)

        @pl.when(i == pl.num_programs(0) - 1)
        def _():
            dqg_ref[...] = _head_sum(qacc[...])
            dkg_ref[...] = _head_sum(kacc[...])

    return pl.pallas_call(
        body, name="bwd_qk", grid=(s // tm,),
        in_specs=[_rows(tm, SB_W, OFF_SB // SB_W), _rows(tm, SB_W, OFF_SB // SB_W + 1),
                  _rows(tm, SB_W), _rows(tm, SB_W), _rows(tm, SB_W), _whole((1, SB_W)), _whole((1, SB_W))],
        out_specs=[_rows(tm, 3 * SB_W), _whole((1, HEAD_DIM)), _whole((1, HEAD_DIM))],
        out_shape=[jax.ShapeDtypeStruct((s, 3 * SB_W), BF16), jax.ShapeDtypeStruct((1, HEAD_DIM), F32),
                   jax.ShapeDtypeStruct((1, HEAD_DIM), F32)],
        scratch_shapes=[pltpu.VMEM((1, SB_W), F32), pltpu.VMEM((1, SB_W), F32)],
        compiler_params=_cp("arbitrary"),
    )(proj, proj, dqs, dkn, dv, qg, kg)


def _bwd_sg(proj, dy, lg, lb, w, bias):
    s = proj.shape[0]
    tm = TOKEN_TILE
    nh = SG_W // HEAD_DIM

    def body(p_ref, dy_ref, lg_ref, lb_ref, w_ref, bias_ref, dp_ref, dlg_ref, dlb_ref, dw_ref, db_ref, dbias):
        i = pl.program_id(0)
        uv = p_ref[...]
        ge = _gelu(uv)
        u = ge[:, :SG_W]
        r, xh = _ln_stats(ge[:, SG_W:])
        vln = (xh * lg_ref[...] + lb_ref[...]).astype(BF16)
        dy = dy_ref[...]
        tril, lane_head = _sg_masks()

        @pl.when(i == 0)
        def _():
            dw_ref[...] = jnp.zeros_like(dw_ref)
            dbias[...] = jnp.zeros_like(dbias)

        dus, dvlns = [], []
        for c in range(tm // CHUNK):
            rows = slice(c * CHUNK, (c + 1) * CHUNK)
            vc = vln[rows]
            dus.append(dy[rows] * _sg_mix(w_ref, bias_ref, vc, tril, lane_head))
            dm = dy[rows] * u[rows]
            dbias[...] += dm
            dvc = jnp.zeros((CHUNK, SG_W), F32)
            for h in range(nh):
                dmh = jnp.where(lane_head == h, dm, 0.0).astype(BF16)
                dw_ref[h] += jnp.where(tril, _dot_nt(dmh, vc), 0.0)
                wm = jnp.where(tril, w_ref[h], 0.0).astype(BF16)
                dvc = dvc + _dot_tn(wm, dmh)
            dvlns.append(dvc)
        du = jnp.concatenate(dus, axis=0)
        dvln = jnp.concatenate(dvlns, axis=0)
        _accumulate(dlg_ref, _colsum(dvln * xh))
        _accumulate(dlb_ref, _colsum(dvln))
        dv = _ln_bwd(xh, r, lg_ref[...], dvln)
        dp_ref[...] = (jnp.concatenate([du, dv], axis=1) * _gelu_grad(uv)).astype(BF16)

        @pl.when(i == pl.num_programs(0) - 1)
        def _():
            lane = lax.broadcasted_iota(jnp.int32, (CHUNK, CHUNK), 1)
            acc = dbias[...]
            out = jnp.zeros((CHUNK, CHUNK), F32)
            for h in range(nh):
                hs = jnp.sum(acc[:, h * HEAD_DIM:(h + 1) * HEAD_DIM], axis=1, keepdims=True)
                out = out + jnp.where(lane == h, hs, 0.0)
            db_ref[...] = out

    return pl.pallas_call(
        body, name="bwd_sg", grid=(s // tm,),
        in_specs=[_rows(tm, 2 * SG_W, 1), _rows(tm, SG_W), _whole((1, SG_W)), _whole((1, SG_W)),
                  _whole((nh, CHUNK, CHUNK)), _whole((CHUNK, SG_W))],
        out_specs=[_rows(tm, 2 * SG_W), _whole((1, SG_W)), _whole((1, SG_W)), _whole((nh, CHUNK, CHUNK)),
                   _whole((CHUNK, CHUNK))],
        out_shape=[jax.ShapeDtypeStruct((s, 2 * SG_W), BF16), jax.ShapeDtypeStruct((1, SG_W), F32),
                   jax.ShapeDtypeStruct((1, SG_W), F32), jax.ShapeDtypeStruct((nh, CHUNK, CHUNK), F32),
                   jax.ShapeDtypeStruct((CHUNK, CHUNK), F32)],
        scratch_shapes=[pltpu.VMEM((CHUNK, SG_W), F32)],
        compiler_params=_cp("arbitrary"),
    )(proj, dy, lg, lb, w, bias)


def _bwd_conv(proj, dy, w, b, lg, lb):
    s = proj.shape[0]

    def body(p_ref, dy_ref, w_ref, b_ref, lg_ref, lb_ref, dp_ref, dw_ref, db_ref, dlg_ref, dlb_ref,
             abuf, dcbuf):
        _glu_fill(p_ref, abuf, s)
        dcbuf[pl.ds(s, CHUNK), :] = jnp.zeros((CHUNK, CONV_W), F32)
        dw_ref[...] = jnp.zeros_like(dw_ref)

        def chunk(c, carry):
            db, dlg, dlb = carry
            r0 = pl.multiple_of(c * CHUNK, CHUNK)
            acc, win = _conv_window(abuf, r0, w_ref)
            r, xh = _ln_stats(acc + b_ref[...])
            ln = xh * lg_ref[...] + lb_ref[...]
            sg = _sigmoid(ln)
            dl = dy_ref[pl.ds(r0, CHUNK), :] * (sg * (1.0 + ln * (1.0 - sg)))
            dc = _ln_bwd(xh, r, lg_ref[...], dl)
            dcbuf[pl.ds(r0, CHUNK), :] = dc
            for k in range(CONV_K):
                sh = win if k == 0 else pltpu.roll(win, k, axis=0)
                dw_ref[CONV_K - 1 - k:CONV_K - k, :] += _colsum(dc * sh[32:, :])
            return db + _colsum(dc), dlg + _colsum(dl * xh), dlb + _colsum(dl)

        zero = jnp.zeros((1, CONV_W), F32)
        db, dlg, dlb = lax.fori_loop(0, s // CHUNK, chunk, (zero, zero, zero))
        db_ref[...] = db
        dlg_ref[...] = dlg
        dlb_ref[...] = dlb

        def chunk_back(c, carry):
            r0 = pl.multiple_of(c * CHUNK, CHUNK)
            win = dcbuf[pl.ds(r0, CHUNK + 32), :]
            da = jnp.zeros((CHUNK, CONV_W), F32)
            for k in range(CONV_K):
                sh = win if k == 0 else pltpu.roll(win, CHUNK + 32 - k, axis=0)
                da = da + sh[:CHUNK, :] * w_ref[CONV_K - 1 - k:CONV_K - k, :]
            pv = p_ref[pl.ds(r0, CHUNK), :]
            val, sg = pv[:, :CONV_W], _sigmoid(pv[:, CONV_W:])
            dp_ref[pl.ds(r0, CHUNK), :] = jnp.concatenate([da * sg, da * val * sg * (1.0 - sg)], axis=1).astype(BF16)
            return carry

        lax.fori_loop(0, s // CHUNK, chunk_back, 0)

    row = _whole((1, CONV_W))
    return pl.pallas_call(
        body, name="bwd_conv", grid=(1,),
        in_specs=[pl.BlockSpec((s, 2 * CONV_W), lambda i: (0, 0)), _whole((s, CONV_W)),
                  _whole((CONV_K, CONV_W)), row, row, row],
        out_specs=[_whole((s, 2 * CONV_W)), _whole((CONV_K, CONV_W)), row, row, row],
        out_shape=[jax.ShapeDtypeStruct((s, 2 * CONV_W), BF16), jax.ShapeDtypeStruct((CONV_K, CONV_W), F32)]
        + [jax.ShapeDtypeStruct((1, CONV_W), F32)] * 3,
        scratch_shapes=[pltpu.VMEM((s + CHUNK, CONV_W), F32), pltpu.VMEM((s + CHUNK, CONV_W), F32)],
        compiler_params=_cp("arbitrary"),
    )(proj, dy, w, b, lg, lb)


def _bwd_in(dpc, dps, dpb, x, g, w, dxm):
    s = x.shape[0]
    tm = TOKEN_TILE

    def body(dpc_ref, dps_ref, dpb_ref, x_ref, g_ref, w_ref, dxm_ref, dx_ref, h_ref, dp_ref, dg_ref):
        dp = jnp.concatenate([dpc_ref[...], dps_ref[...], dpb_ref[...]], axis=1)
        dp_ref[...] = dp
        dh = _dot_nt(dp, w_ref[...])
        r, xh = _rms_stats(x_ref[...])
        h_ref[...] = (xh * g_ref[...]).astype(BF16)
        dx, dgrow = _rms_bwd(xh, r, g_ref[...], dh)
        dx_ref[...] = dxm_ref[...] + dx
        _accumulate(dg_ref, _colsum(dgrow))

    return pl.pallas_call(
        body, name="bwd_in", grid=(s // tm,),
        in_specs=[_rows(tm, 2 * CONV_W), _rows(tm, 2 * SG_W), _rows(tm, 3 * SB_W), _rows(tm, D_MODEL),
                  _whole((1, D_MODEL)), _whole((D_MODEL, IN_W)), _rows(tm, D_MODEL)],
        out_specs=[_rows(tm, D_MODEL), _rows(tm, D_MODEL), _rows(tm, IN_W), _whole((1, D_MODEL))],
        out_shape=[jax.ShapeDtypeStruct((s, D_MODEL), F32), jax.ShapeDtypeStruct((s, D_MODEL), BF16),
                   jax.ShapeDtypeStruct((s, IN_W), BF16), jax.ShapeDtypeStruct((1, D_MODEL), F32)],
        compiler_params=_cp("arbitrary"),
    )(dpc, dps, dpb, x, g, w, dxm)


SMALL = ("mix_norm_g", "conv_w", "conv_b", "conv_ln_g", "conv_ln_b", "sg_ln_g", "sg_ln_b", "sg_w", "sg_b",
         "q_norm_g", "k_norm_g", "out_norm_g", "ffn_norm_g")
LARGE = ("w_in", "w_out", "w_gate_up", "w_down")


def _row(v):
    return v.reshape(1, -1)


def _layer_params(p, l):
    q = {k: v[l] for k, v in p.items()}
    return dict(
        q,
        mix_norm_g=_row(q["mix_norm_g"]), conv_b=_row(q["conv_b"]), conv_ln_g=_row(q["conv_ln_g"]),
        conv_ln_b=_row(q["conv_ln_b"]), sg_ln_g=_row(q["sg_ln_g"]), sg_ln_b=_row(q["sg_ln_b"]),
        out_norm_g=_row(q["out_norm_g"]), ffn_norm_g=_row(q["ffn_norm_g"]),
        qg=_row(jnp.tile(q["q_norm_g"], SB_W // HEAD_DIM)), kg=_row(jnp.tile(q["k_norm_g"], SB_W // HEAD_DIM)),
        sg_bias=jnp.repeat(q["sg_b"].T, HEAD_DIM, axis=1),
    )


def _layer_fwd(x, q):
    proj, qn, kn, vb = _fwd_in(x, q["mix_norm_g"], q["w_in"], q["qg"], q["kg"])
    yc = _fwd_conv(proj, q["conv_w"], q["conv_b"], q["conv_ln_g"], q["conv_ln_b"])
    ys = _fwd_sg(proj, q["sg_ln_g"], q["sg_ln_b"], q["sg_w"], q["sg_bias"])
    yb = _fwd_sb(qn, kn, vb)
    xm = _fwd_out(yc, ys, yb, q["out_norm_g"], q["w_out"], x)
    gu, xo = _fwd_ffn(xm, q["ffn_norm_g"], q["w_gate_up"], q["w_down"])
    return xo, dict(x=x, proj=proj, qn=qn, kn=kn, vb=vb, yc=yc, ys=ys, yb=yb, xm=xm, gu=gu)


def _layer_bwd(dxo, q, st):
    dgu, act, h2, dxm, d_ffn_g = _bwd_ffn(dxo, st["gu"], st["xm"], q["ffn_norm_g"], q["w_gate_up"], q["w_down"])
    d_wgu = _matmul_tn(h2, dgu, 512, 512)
    d_wd = _matmul_tn(act, dxo, 256, 512)
    dyc, dys, dyb, cat, d_out_g = _bwd_out(dxm, st["yc"], st["ys"], st["yb"], q["out_norm_g"], q["w_out"])
    d_wo = _matmul_tn(cat, dxm, 512, 512)
    dqs, dkn, dv = _bwd_sb(st["qn"], st["kn"], st["vb"], dyb)
    dpb, d_qg, d_kg = _bwd_qk(st["proj"], dqs, dkn, dv, q["qg"], q["kg"])
    dps, d_sg_lg, d_sg_lb, d_sg_w, d_sg_b = _bwd_sg(st["proj"], dys, q["sg_ln_g"], q["sg_ln_b"], q["sg_w"],
                                                    q["sg_bias"])
    dpc, d_conv_w, d_conv_b, d_conv_lg, d_conv_lb = _bwd_conv(st["proj"], dyc, q["conv_w"], q["conv_b"],
                                                              q["conv_ln_g"], q["conv_ln_b"])
    dx, h1, dp, d_mix_g = _bwd_in(dpc, dps, dpb, st["x"], q["mix_norm_g"], q["w_in"], dxm)
    d_win = _matmul_tn(h1, dp, 512, 512)
    grads = dict(
        mix_norm_g=d_mix_g[0], w_in=d_win, conv_w=d_conv_w, conv_b=d_conv_b[0], conv_ln_g=d_conv_lg[0],
        conv_ln_b=d_conv_lb[0], sg_ln_g=d_sg_lg[0], sg_ln_b=d_sg_lb[0], sg_w=d_sg_w,
        sg_b=d_sg_b[:, :SG_W // HEAD_DIM].T, q_norm_g=d_qg[0], k_norm_g=d_kg[0], out_norm_g=d_out_g[0],
        w_out=d_wo, ffn_norm_g=d_ffn_g[0], w_gate_up=d_wgu, w_down=d_wd)
    return dx, grads


def _local_step(x, target, p):
    qs = [_layer_params(p, l) for l in range(DEPTH)]
    stash = []
    for l in range(DEPTH):
        x, st = _layer_fwd(x, qs[l])
        stash.append(st)
    loss, dx = _loss_head(x, target)
    grads = [None] * DEPTH
    for l in reversed(range(DEPTH)):
        dx, grads[l] = _layer_bwd(dx, qs[l], stash[l])
    return loss, dx, grads


def _position():
    x, y, c = lax.axis_index("x"), lax.axis_index("y"), lax.axis_index("c")
    return x, y, c


def _flat(px, py, pc):
    return 4 * px + 2 * py + pc


HBM = pl.BlockSpec(memory_space=pl.ANY)


def _all_gather(arrs, name):
    n = len(arrs)

    def body(*refs):
        ins, outs = refs[:n], refs[n:2 * n]
        send_sems, recv_sems, local_sems = refs[2 * n:]
        x, y, c = _position()
        me, sibling = (x, y, c), (x, y, 1 - c)
        chips = [(1 - x, y), (x, 1 - y), (1 - x, 1 - y)]

        def copy(a, k, block, to, src=None):
            dst = outs[a].at[_flat(*block)]
            return pltpu.make_async_remote_copy(
                src_ref=dst if src is None else src, dst_ref=dst, send_sem=send_sems.at[a, k],
                recv_sem=recv_sems.at[a, k], device_id=to, device_id_type=MESH)

        mine, first, passed = [], [], []
        for a in range(n):
            cp = pltpu.make_async_copy(ins[a], outs[a].at[_flat(*me)], local_sems.at[a])
            cp.start()
            mine.append(cp)
            first.append(copy(a, 0, me, sibling, src=ins[a]))
            first += [copy(a, 1 + j, me, (*chip, c), src=ins[a]) for j, chip in enumerate(chips)]
        for cp in first:
            cp.start()
        for a in range(n):
            for j, chip in enumerate(chips):
                copy(a, 1 + j, (*chip, c), me).wait_recv()
                fwd = copy(a, 4 + j, (*chip, c), sibling)
                fwd.start()
                passed.append(fwd)
        for a in range(n):
            copy(a, 0, sibling, me).wait_recv()
            for j, chip in enumerate(chips):
                copy(a, 4 + j, (*chip, 1 - c), me).wait_recv()
        for cp in first + passed:
            cp.wait_send()
        for cp in mine:
            cp.wait()

    return pl.pallas_call(
        body, name=name,
        in_specs=[HBM] * n, out_specs=[HBM] * n,
        out_shape=[jax.ShapeDtypeStruct((N_DEV,) + a.shape, a.dtype) for a in arrs],
        scratch_shapes=[pltpu.SemaphoreType.DMA((n, 7)), pltpu.SemaphoreType.DMA((n, 7)),
                        pltpu.SemaphoreType.DMA((n,))],
    )(*arrs)


def _all_to_all(arrs, name):
    n = len(arrs)

    def body(*refs):
        ins, outs = refs[:n], refs[n:2 * n]
        send_sems, recv_sems, local_sems = refs[2 * n:]
        x, y, c = _position()
        me = _flat(x, y, c)
        peers = [(x ^ (k >> 2 & 1), y ^ (k >> 1 & 1), c ^ (k & 1)) for k in range(1, N_DEV)]

        def copy(a, k, peer):
            return pltpu.make_async_remote_copy(
                src_ref=ins[a].at[_flat(*peer)], dst_ref=outs[a].at[me], send_sem=send_sems.at[a, k],
                recv_sem=recv_sems.at[a, k], device_id=peer, device_id_type=MESH)

        def arrival(a, k, peer):
            dst = outs[a].at[_flat(*peer)]
            return pltpu.make_async_remote_copy(
                src_ref=dst, dst_ref=dst, send_sem=send_sems.at[a, k], recv_sem=recv_sems.at[a, k],
                device_id=peer, device_id_type=MESH)

        mine, sent = [], []
        for a in range(n):
            cp = pltpu.make_async_copy(ins[a].at[me], outs[a].at[me], local_sems.at[a])
            cp.start()
            mine.append(cp)
            for k, peer in enumerate(peers):
                cp = copy(a, k, peer)
                cp.start()
                sent.append(cp)
        for a in range(n):
            for k, peer in enumerate(peers):
                arrival(a, k, peer).wait_recv()
        for cp in sent:
            cp.wait_send()
        for cp in mine:
            cp.wait()

    return pl.pallas_call(
        body, name=name,
        in_specs=[HBM] * n, out_specs=[HBM] * n,
        out_shape=[jax.ShapeDtypeStruct(a.shape, a.dtype) for a in arrs],
        scratch_shapes=[pltpu.SemaphoreType.DMA((n, 7)), pltpu.SemaphoreType.DMA((n, 7)),
                        pltpu.SemaphoreType.DMA((n,))],
    )(*arrs)


def _adamw(parts, w, m, v, tr):
    rows, cols = w.shape

    def body(p_ref, w_ref, m_ref, v_ref, g_ref, d_ref, nm_ref, nv_ref):
        g = p_ref[0].astype(F32)
        for j in range(1, N_DEV):
            g = g + p_ref[j].astype(F32)
        g_ref[...] = g
        m = ADAM_B1 * m_ref[...] + (1.0 - ADAM_B1) * g
        v = ADAM_B2 * v_ref[...] + (1.0 - ADAM_B2) * (g * g)
        nm_ref[...] = m
        nv_ref[...] = v
        m_hat = m / (1.0 - ADAM_B1 ** ADAM_STEP)
        v_hat = v / (1.0 - ADAM_B2 ** ADAM_STEP)
        d_ref[...] = -ADAM_LR * (m_hat / (jnp.sqrt(v_hat) + ADAM_EPS) + ADAM_WD * w_ref[...])

    blk = pl.BlockSpec((tr, cols), lambda i: (i, 0))
    return pl.pallas_call(
        body, name="adamw", grid=(rows // tr,),
        in_specs=[pl.BlockSpec((N_DEV, tr, cols), lambda i: (0, i, 0)), blk, blk, blk],
        out_specs=[blk] * 4,
        out_shape=[jax.ShapeDtypeStruct((rows, cols), F32)] * 4,
        compiler_params=_cp("parallel"),
    )(parts, w, m, v)


def _adamw_tiled(parts, w, m, v):
    shape = w.shape
    cols = shape[-1]
    rows = w.size // cols
    tr = rows
    for cand in (512, 256, 128, 64, 32, 16, 8):
        if rows % cand == 0 and rows > cand:
            tr = cand
            break
    outs = _adamw(parts.reshape(N_DEV, rows, cols), w.reshape(rows, cols), m.reshape(rows, cols),
                  v.reshape(rows, cols), tr)
    return [o.reshape(shape) for o in outs]


def _to_columns(g):
    n, l, r, c = g.shape
    return jnp.transpose(g, (1, 2, 0, 3)).reshape(l, r, n * c)


def _to_rows(g):
    n, l, r, c = g.shape
    return jnp.transpose(g, (1, 0, 2, 3)).reshape(l, n * r, c)


def _from_columns(g):
    l, r, c = g.shape
    return jnp.transpose(g.reshape(l, r, N_DEV, c // N_DEV), (2, 0, 1, 3))


def _from_rows(g):
    l, r, c = g.shape
    return jnp.transpose(g.reshape(l, N_DEV, r // N_DEV, c), (1, 0, 2, 3))


PACK_LANES = 128


def _pack(arrs):
    parts = []
    for a in arrs:
        flat = a.reshape(-1)
        pad = -flat.size % (8 * PACK_LANES)
        parts.append(jnp.pad(flat, (0, pad)))
    return jnp.concatenate(parts).reshape(-1, PACK_LANES)


def _unpack(packed, shapes):
    flat = packed.reshape(-1)
    outs, off = [], 0
    for shp in shapes:
        size = 1
        for d in shp:
            size *= d
        outs.append(flat[off:off + size].reshape(shp))
        off += size + (-size % (8 * PACK_LANES))
    return outs


def kernel(x, mix_norm_g, w_in, conv_w, conv_b, conv_ln_g, conv_ln_b, sg_ln_g, sg_ln_b, sg_w, sg_b, q_norm_g, k_norm_g, out_norm_g, w_out, ffn_norm_g, w_gate_up, w_down, loss_target, m_mix_norm_g, m_w_in, m_conv_w, m_conv_b, m_conv_ln_g, m_conv_ln_b, m_sg_ln_g, m_sg_ln_b, m_sg_w, m_sg_b, m_q_norm_g, m_k_norm_g, m_out_norm_g, m_w_out, m_ffn_norm_g, m_w_gate_up, m_w_down, v_mix_norm_g, v_w_in, v_conv_w, v_conv_b, v_conv_ln_g, v_conv_ln_b, v_sg_ln_g, v_sg_ln_b, v_sg_w, v_sg_b, v_q_norm_g, v_k_norm_g, v_out_norm_g, v_w_out, v_ffn_norm_g, v_w_gate_up, v_w_down):
    names = SMALL[:1] + LARGE[:1] + SMALL[1:12] + LARGE[1:2] + SMALL[12:] + LARGE[2:]
    w = dict(mix_norm_g=mix_norm_g, w_in=w_in, conv_w=conv_w, conv_b=conv_b, conv_ln_g=conv_ln_g,
             conv_ln_b=conv_ln_b, sg_ln_g=sg_ln_g, sg_ln_b=sg_ln_b, sg_w=sg_w, sg_b=sg_b, q_norm_g=q_norm_g,
             k_norm_g=k_norm_g, out_norm_g=out_norm_g, w_out=w_out, ffn_norm_g=ffn_norm_g,
             w_gate_up=w_gate_up, w_down=w_down)
    m = dict(mix_norm_g=m_mix_norm_g, w_in=m_w_in, conv_w=m_conv_w, conv_b=m_conv_b, conv_ln_g=m_conv_ln_g,
             conv_ln_b=m_conv_ln_b, sg_ln_g=m_sg_ln_g, sg_ln_b=m_sg_ln_b, sg_w=m_sg_w, sg_b=m_sg_b,
             q_norm_g=m_q_norm_g, k_norm_g=m_k_norm_g, out_norm_g=m_out_norm_g, w_out=m_w_out,
             ffn_norm_g=m_ffn_norm_g, w_gate_up=m_w_gate_up, w_down=m_w_down)
    v = dict(mix_norm_g=v_mix_norm_g, w_in=v_w_in, conv_w=v_conv_w, conv_b=v_conv_b, conv_ln_g=v_conv_ln_g,
             conv_ln_b=v_conv_ln_b, sg_ln_g=v_sg_ln_g, sg_ln_b=v_sg_ln_b, sg_w=v_sg_w, sg_b=v_sg_b,
             q_norm_g=v_q_norm_g, k_norm_g=v_k_norm_g, out_norm_g=v_out_norm_g, w_out=v_w_out,
             ffn_norm_g=v_ffn_norm_g, w_gate_up=v_w_gate_up, w_down=v_w_down)
    xpos, ypos, cpos = _position()
    me = _flat(xpos, ypos, cpos)
    conv_cols = conv_w.shape[-1]

    g_in, g_out, g_gu, g_down, g_conv = _all_gather(
        [w["w_in"].astype(BF16), w["w_out"].astype(BF16), w["w_gate_up"].astype(BF16),
         w["w_down"].astype(BF16), w["conv_w"]], "gather_weights")
    full = dict(w, w_in=_to_columns(g_in), w_out=_to_rows(g_out), w_gate_up=_to_columns(g_gu),
                w_down=_to_rows(g_down), conv_w=_to_columns(g_conv))

    loss, dx, grads = _local_step(x[0], loss_target[0], full)
    loss = lax.psum(loss[0, 0], ("x", "y", "c"))

    stack = lambda k: jnp.stack([grads[l][k] for l in range(DEPTH)])
    r_in, r_out, r_gu, r_down = _all_to_all(
        [_from_columns(stack("w_in")), _from_rows(stack("w_out")), _from_columns(stack("w_gate_up")),
         _from_rows(stack("w_down"))], "exchange_grads")
    small_shapes = [(DEPTH,) + grads[0][k].shape for k in SMALL]
    (small_parts,) = _all_gather([_pack([stack(k) for k in SMALL])], "gather_small_grads")
    per_dev = [_unpack(small_parts[j], small_shapes) for j in range(N_DEV)]
    small = [jnp.stack([per_dev[j][i] for j in range(N_DEV)]) for i in range(len(SMALL))]
    conv_i = SMALL.index("conv_w")
    small[conv_i] = lax.dynamic_slice_in_dim(small[conv_i], me * conv_cols, conv_cols, axis=3)

    res = {}
    for k, parts in zip(LARGE, (r_in, r_out, r_gu, r_down)):
        res[k] = _adamw_tiled(parts, w[k], m[k], v[k])
    shard_shapes = [w[k].shape for k in SMALL]
    packed = _adamw_tiled(
        jnp.stack([_pack([small[i][j] for i in range(len(SMALL))]) for j in range(N_DEV)]),
        _pack([w[k] for k in SMALL]), _pack([m[k] for k in SMALL]), _pack([v[k] for k in SMALL]))
    unpacked = [_unpack(o, shard_shapes) for o in packed]
    for i, k in enumerate(SMALL):
        res[k] = [u[i] for u in unpacked]

    return (loss, dx[None], *[res[k][0] for k in names], *[res[k][1] for k in names],
            *[res[k][2] for k in names], *[res[k][3] for k in names])
```

```python
import functools

import jax
import jax.numpy as jnp
from jax import lax
from jax.experimental import pallas as pl
from jax.experimental.pallas import tpu as pltpu

F32 = jnp.float32
BF16 = jnp.bfloat16

D_MODEL = 1024
DEPTH = 4
HEAD_DIM = 64
CONV_W = 256
SG_W = 256
SB_W = 512
IN_W = 2560
FFN_H = 2816
CONV_K = 31
CHUNK = 128
OFF_SG = 2 * CONV_W
OFF_SB = OFF_SG + 2 * SG_W
RMS_EPS = 1e-6
LN_EPS = 1e-5
N_DEV = 8
MESH = pl.DeviceIdType.MESH

ADAM_LR = 0.001
ADAM_B1 = 0.9
ADAM_B2 = 0.999
ADAM_EPS = 1e-08
ADAM_WD = 0.01
ADAM_STEP = 10

TOKEN_TILE = 256
VMEM_LIMIT = 56 * 1024 * 1024


def _cp(*sem):
    return pltpu.CompilerParams(dimension_semantics=sem or None, vmem_limit_bytes=VMEM_LIMIT)


def _dot(a, b):
    return jnp.dot(a, b, preferred_element_type=F32)


def _dot_nt(a, b):
    return lax.dot_general(a, b, (((1,), (1,)), ((), ())), preferred_element_type=F32)


def _dot_tn(a, b):
    return lax.dot_general(a, b, (((0,), (0,)), ((), ())), preferred_element_type=F32)


def _dot_split(x, m):
    hi = x.astype(BF16)
    lo = (x - hi.astype(F32)).astype(BF16)
    return _dot(hi, m) + _dot(lo, m)


def _group_mean_matrix(width, group):
    r = lax.broadcasted_iota(jnp.int32, (width, width), 0) // group
    c = lax.broadcasted_iota(jnp.int32, (width, width), 1) // group
    return jnp.where(r == c, 1.0 / group, 0.0).astype(BF16)


def _sigmoid(x):
    return 1.0 / (1.0 + jnp.exp(-x))


def _gelu(x):
    return 0.5 * x * (1.0 + lax.erf(x * (2.0 ** -0.5)))


def _gelu_grad(x):
    return 0.5 * (1.0 + lax.erf(x * (2.0 ** -0.5))) + x * jnp.exp(-0.5 * x * x) * (0.5 * (2.0 / jnp.pi) ** 0.5)


def _rms_stats(x):
    r = lax.rsqrt(jnp.mean(x * x, axis=-1, keepdims=True) + RMS_EPS)
    return r, x * r


def _rms_bwd(xh, r, g, dy):
    dxh = dy * g
    dx = r * (dxh - xh * jnp.mean(dxh * xh, axis=-1, keepdims=True))
    return dx, dy * xh


def _ln_stats(x):
    mu = jnp.mean(x, axis=-1, keepdims=True)
    xc = x - mu
    r = lax.rsqrt(jnp.mean(xc * xc, axis=-1, keepdims=True) + LN_EPS)
    return r, xc * r


def _ln_bwd(xh, r, g, dy):
    dxh = dy * g
    return r * (dxh - jnp.mean(dxh, axis=-1, keepdims=True) - xh * jnp.mean(dxh * xh, axis=-1, keepdims=True))


def _colsum(x):
    return jnp.sum(x, axis=0, keepdims=True)


def _rows(tm, n, j=0):
    return pl.BlockSpec((tm, n), lambda i: (i, j))


def _whole(shape):
    return pl.BlockSpec(shape, lambda i: (0,) * len(shape))


def _stack_heads(a):
    even = (lax.broadcasted_iota(jnp.int32, a.shape, 1) % (2 * HEAD_DIM)) < HEAD_DIM
    top = jnp.where(even, a, 0.0).astype(BF16)
    bot = jnp.where(even, 0.0, a).astype(BF16)
    parts = []
    for c in range(a.shape[0] // CHUNK):
        rows = slice(c * CHUNK, (c + 1) * CHUNK)
        parts += [top[rows], bot[rows]]
    return jnp.concatenate(parts, axis=0)


def _unstack_heads(st):
    even = (lax.broadcasted_iota(jnp.int32, (CHUNK, st.shape[1]), 1) % (2 * HEAD_DIM)) < HEAD_DIM
    parts = []
    for c in range(st.shape[0] // (2 * CHUNK)):
        top = st[2 * c * CHUNK:(2 * c + 1) * CHUNK]
        bot = st[(2 * c + 1) * CHUNK:(2 * c + 2) * CHUNK]
        parts.append(jnp.where(even, top, bot))
    return jnp.concatenate(parts, axis=0)


def _fwd_in(x, g, w, qg, kg):
    s = x.shape[0]
    tm = TOKEN_TILE

    def body(x_ref, g_ref, w_ref, qg_ref, kg_ref, proj_ref, qn_ref, kn_ref, vb_ref):
        r, xh = _rms_stats(x_ref[...])
        h = (xh * g_ref[...]).astype(BF16)
        proj = _dot(h, w_ref[...])
        proj_ref[...] = proj
        gm = _group_mean_matrix(SB_W, HEAD_DIM)
        q = proj[:, OFF_SB:OFF_SB + SB_W]
        k = proj[:, OFF_SB + SB_W:OFF_SB + 2 * SB_W]
        rq = lax.rsqrt(_dot_split(q * q, gm) + RMS_EPS)
        rk = lax.rsqrt(_dot_split(k * k, gm) + RMS_EPS)
        qn_ref[...] = (q * rq * qg_ref[...] * (HEAD_DIM ** -0.5)).astype(BF16)
        kn_ref[...] = _stack_heads(k * rk * kg_ref[...])
        vb_ref[...] = _stack_heads(proj[:, OFF_SB + 2 * SB_W:])

    return pl.pallas_call(
        body, name="fwd_in", grid=(s // tm,),
        in_specs=[_rows(tm, D_MODEL), _whole((1, D_MODEL)), _whole((D_MODEL, IN_W)),
                  _whole((1, SB_W)), _whole((1, SB_W))],
        out_specs=[_rows(tm, IN_W), _rows(tm, SB_W), _rows(2 * tm, SB_W), _rows(2 * tm, SB_W)],
        out_shape=[jax.ShapeDtypeStruct((s, IN_W), F32), jax.ShapeDtypeStruct((s, SB_W), BF16),
                   jax.ShapeDtypeStruct((2 * s, SB_W), BF16), jax.ShapeDtypeStruct((2 * s, SB_W), BF16)],
        compiler_params=_cp("parallel"),
    )(x, g, w, qg, kg)


def _conv_window(abuf, r0, w_ref):
    win = abuf[pl.ds(pl.multiple_of(r0 + CHUNK - 32, 32), CHUNK + 32), :]
    acc = jnp.zeros((CHUNK, CONV_W), F32)
    for k in range(CONV_K):
        sh = win if k == 0 else pltpu.roll(win, k, axis=0)
        acc = acc + sh[32:, :] * w_ref[CONV_K - 1 - k:CONV_K - k, :]
    return acc, win


def _glu_fill(p_ref, abuf, s):
    abuf[0:CHUNK, :] = jnp.zeros((CHUNK, CONV_W), F32)

    def fill(c, carry):
        r0 = pl.multiple_of(c * CHUNK, CHUNK)
        pv = p_ref[pl.ds(r0, CHUNK), :]
        abuf[pl.ds(r0 + CHUNK, CHUNK), :] = pv[:, :CONV_W] * _sigmoid(pv[:, CONV_W:])
        return carry

    lax.fori_loop(0, s // CHUNK, fill, 0)


def _fwd_conv(proj, w, b, lg, lb):
    s = proj.shape[0]

    def body(p_ref, w_ref, b_ref, lg_ref, lb_ref, y_ref, abuf):
        _glu_fill(p_ref, abuf, s)

        def chunk(c, carry):
            r0 = pl.multiple_of(c * CHUNK, CHUNK)
            acc, _ = _conv_window(abuf, r0, w_ref)
            r, xh = _ln_stats(acc + b_ref[...])
            ln = xh * lg_ref[...] + lb_ref[...]
            y_ref[pl.ds(r0, CHUNK), :] = ln * _sigmoid(ln)
            return carry

        lax.fori_loop(0, s // CHUNK, chunk, 0)

    return pl.pallas_call(
        body, name="fwd_conv", grid=(1,),
        in_specs=[pl.BlockSpec((s, 2 * CONV_W), lambda i: (0, 0)), _whole((CONV_K, CONV_W)),
                  _whole((1, CONV_W)), _whole((1, CONV_W)), _whole((1, CONV_W))],
        out_specs=_whole((s, CONV_W)),
        out_shape=jax.ShapeDtypeStruct((s, CONV_W), F32),
        scratch_shapes=[pltpu.VMEM((s + CHUNK, CONV_W), F32)],
        compiler_params=_cp("arbitrary"),
    )(proj, w, b, lg, lb)


def _sg_masks():
    row = lax.broadcasted_iota(jnp.int32, (CHUNK, CHUNK), 0)
    col = lax.broadcasted_iota(jnp.int32, (CHUNK, CHUNK), 1)
    lane_head = lax.broadcasted_iota(jnp.int32, (CHUNK, SG_W), 1) // HEAD_DIM
    return row >= col, lane_head


def _sg_mix(w_ref, bias_ref, vc, tril, lane_head):
    mixed = bias_ref[...]
    for h in range(SG_W // HEAD_DIM):
        wm = jnp.where(tril, w_ref[h], 0.0).astype(BF16)
        mixed = mixed + jnp.where(lane_head == h, _dot(wm, vc), 0.0)
    return mixed


def _fwd_sg(proj, lg, lb, w, bias):
    s = proj.shape[0]
    tm = TOKEN_TILE

    def body(p_ref, lg_ref, lb_ref, w_ref, bias_ref, y_ref):
        ge = _gelu(p_ref[...])
        u = ge[:, :SG_W]
        r, xh = _ln_stats(ge[:, SG_W:])
        vln = (xh * lg_ref[...] + lb_ref[...]).astype(BF16)
        tril, lane_head = _sg_masks()
        for c in range(tm // CHUNK):
            rows = slice(c * CHUNK, (c + 1) * CHUNK)
            y_ref[rows, :] = u[rows] * _sg_mix(w_ref, bias_ref, vln[rows], tril, lane_head)

    return pl.pallas_call(
        body, name="fwd_sg", grid=(s // tm,),
        in_specs=[_rows(tm, 2 * SG_W, 1), _whole((1, SG_W)), _whole((1, SG_W)),
                  _whole((SG_W // HEAD_DIM, CHUNK, CHUNK)), _whole((CHUNK, SG_W))],
        out_specs=_rows(tm, SG_W),
        out_shape=jax.ShapeDtypeStruct((s, SG_W), F32),
        compiler_params=_cp("parallel"),
    )(proj, lg, lb, w, bias)


SB_Q = 2 * CHUNK
PAIR = 2 * CHUNK


def _pair_tri(kind):
    row = lax.broadcasted_iota(jnp.int32, (PAIR, PAIR), 0)
    col = lax.broadcasted_iota(jnp.int32, (PAIR, PAIR), 1)
    tri = {"after": row > col, "upto": row <= col, "before": row < col}[kind]
    return jnp.where(((row // CHUNK) == (col // CHUNK)) & tri, 1.0, 0.0).astype(BF16)


def _sb_scores(q, kt, qpos0, kpos0, masked):
    z = _dot_nt(q, kt)
    sp = jnp.maximum(z, 0.0) + jnp.log(1.0 + jnp.exp(-jnp.abs(z)))
    if not masked:
        return z, sp, -sp, None
    row = lax.broadcasted_iota(jnp.int32, z.shape, 0)
    col = lax.broadcasted_iota(jnp.int32, z.shape, 1) % CHUNK
    mask = (kpos0 + col) < (qpos0 + row)
    return z, sp, jnp.where(mask, -sp, 0.0), mask


def _per_head(c0, c1):
    return jnp.concatenate([jnp.broadcast_to(c0, (SB_Q, CHUNK)), jnp.broadcast_to(c1, (SB_Q, CHUNK))], axis=1)


def _fwd_sb(qn, kst, vst):
    s = qn.shape[0]

    def body(q_ref, k_ref, v_ref, o_ref, lt_ref):
        i = pl.program_id(1)
        q = q_ref[...]
        after = _pair_tri("after")

        def block(kb, carry, masked):
            acc, c0, c1 = carry
            r0 = pl.multiple_of(kb * PAIR, PAIR)
            z, sp, lnb, mask = _sb_scores(q, k_ref[pl.ds(r0, PAIR), :], i * SB_Q, kb * CHUNK, masked)
            loc = _dot_split(lnb, after)
            att = jnp.exp(z - sp + loc + _per_head(c0, c1))
            if masked:
                att = jnp.where(mask, att, 0.0)
            acc = acc + _dot(att.astype(BF16), v_ref[pl.ds(r0, PAIR), :])
            return (acc, c0 + loc[:, 0:1] + lnb[:, 0:1], c1 + loc[:, CHUNK:CHUNK + 1] + lnb[:, CHUNK:CHUNK + 1])

        zero = jnp.zeros((SB_Q, 1), F32)
        carry = (jnp.zeros((SB_Q, CHUNK), F32), zero, zero)
        carry = block(2 * i + 1, carry, True)
        carry = block(2 * i, carry, True)
        acc, c0, c1 = lax.fori_loop(0, 2 * i, lambda j, c: block(2 * i - 1 - j, c, False), carry)
        o_ref[...] = acc
        lt_ref[...] = jnp.concatenate([jnp.broadcast_to(c0, (SB_Q, HEAD_DIM)),
                                       jnp.broadcast_to(c1, (SB_Q, HEAD_DIM))], axis=1)

    blk = pl.BlockSpec((SB_Q, CHUNK), lambda p, i: (i, p))
    seq = pl.BlockSpec((2 * s, CHUNK), lambda p, i: (0, p))
    return pl.pallas_call(
        body, name="fwd_sb", grid=(SB_W // CHUNK, s // SB_Q),
        in_specs=[blk, seq, seq],
        out_specs=[blk, blk],
        out_shape=[jax.ShapeDtypeStruct((s, SB_W), F32)] * 2,
        compiler_params=_cp("parallel", "parallel"),
    )(qn, kst, vst)


def _group_norms(yc, ys, yb):
    return [_rms_stats(yc), _rms_stats(ys), _rms_stats(yb)]


def _fwd_out(yc, ys, yb, g, w, x):
    s = x.shape[0]
    tm = TOKEN_TILE

    def body(yc_ref, ys_ref, yb_ref, g_ref, w_ref, x_ref, o_ref):
        stats = _group_norms(yc_ref[...], ys_ref[...], yb_ref[...])
        cat = jnp.concatenate([xh for _, xh in stats], axis=1) * g_ref[...]
        o_ref[...] = x_ref[...] + _dot(cat.astype(BF16), w_ref[...])

    return pl.pallas_call(
        body, name="fwd_out", grid=(s // tm,),
        in_specs=[_rows(tm, CONV_W), _rows(tm, SG_W), _rows(tm, SB_W), _whole((1, D_MODEL)),
                  _whole((D_MODEL, D_MODEL)), _rows(tm, D_MODEL)],
        out_specs=_rows(tm, D_MODEL),
        out_shape=jax.ShapeDtypeStruct((s, D_MODEL), F32),
        compiler_params=_cp("parallel"),
    )(yc, ys, yb, g, w, x)


def _fwd_ffn(x, g, wgu, wd):
    s = x.shape[0]
    tm = TOKEN_TILE

    def body(x_ref, g_ref, wgu_ref, wd_ref, gu_ref, o_ref):
        x = x_ref[...]
        r, xh = _rms_stats(x)
        gu = _dot((xh * g_ref[...]).astype(BF16), wgu_ref[...])
        gu_ref[...] = gu
        gate = gu[:, :FFN_H]
        act = gate * _sigmoid(gate) * gu[:, FFN_H:]
        o_ref[...] = x + _dot(act.astype(BF16), wd_ref[...])

    return pl.pallas_call(
        body, name="fwd_ffn", grid=(s // tm,),
        in_specs=[_rows(tm, D_MODEL), _whole((1, D_MODEL)),
                  pl.BlockSpec((D_MODEL, 2 * FFN_H), lambda i: (0, 0), pipeline_mode=pl.Buffered(1)),
                  pl.BlockSpec((FFN_H, D_MODEL), lambda i: (0, 0), pipeline_mode=pl.Buffered(1))],
        out_specs=[_rows(tm, 2 * FFN_H), _rows(tm, D_MODEL)],
        out_shape=[jax.ShapeDtypeStruct((s, 2 * FFN_H), F32), jax.ShapeDtypeStruct((s, D_MODEL), F32)],
        compiler_params=_cp("parallel"),
    )(x, g, wgu, wd)


def _loss_head(y, target):
    s = y.shape[0]
    tm = TOKEN_TILE

    def body(y_ref, t_ref, l_ref, d_ref):
        @pl.when(pl.program_id(0) == 0)
        def _():
            l_ref[...] = jnp.zeros_like(l_ref)

        err = y_ref[...] - t_ref[...]
        d_ref[...] = err * (1.0 / D_MODEL)
        l_ref[...] += 0.5 * jnp.sum(jnp.mean(err * err, axis=-1, keepdims=True), axis=0, keepdims=True)

    return pl.pallas_call(
        body, name="loss_head", grid=(s // tm,),
        in_specs=[_rows(tm, D_MODEL), _rows(tm, D_MODEL)],
        out_specs=[_whole((1, 1)), _rows(tm, D_MODEL)],
        out_shape=[jax.ShapeDtypeStruct((1, 1), F32), jax.ShapeDtypeStruct((s, D_MODEL), F32)],
        compiler_params=_cp("arbitrary"),
    )(y, target)


def _accumulate(ref, value):
    @pl.when(pl.program_id(0) == 0)
    def _():
        ref[...] = jnp.zeros_like(ref)

    ref[...] += value


def _bwd_ffn(dxo, gu, xm, g, wgu, wd):
    s = dxo.shape[0]
    tm = TOKEN_TILE

    def body(dxo_ref, gu_ref, xm_ref, g_ref, wgu_ref, wd_ref, dgu_ref, act_ref, h_ref, dxm_ref, dg_ref):
        dxo = dxo_ref[...]
        gu = gu_ref[...]
        gate, up = gu[:, :FFN_H], gu[:, FFN_H:]
        sg = _sigmoid(gate)
        sl = gate * sg
        act_ref[...] = (sl * up).astype(BF16)
        dact = _dot_nt(dxo.astype(BF16), wd_ref[...])
        dgate = dact * up * (sg * (1.0 + gate * (1.0 - sg)))
        dgu = jnp.concatenate([dgate, dact * sl], axis=1).astype(BF16)
        dgu_ref[...] = dgu
        dh = _dot_nt(dgu, wgu_ref[...])
        r, xh = _rms_stats(xm_ref[...])
        h_ref[...] = (xh * g_ref[...]).astype(BF16)
        dx, dgrow = _rms_bwd(xh, r, g_ref[...], dh)
        dxm_ref[...] = dxo + dx
        _accumulate(dg_ref, _colsum(dgrow))

    return pl.pallas_call(
        body, name="bwd_ffn", grid=(s // tm,),
        in_specs=[_rows(tm, D_MODEL), _rows(tm, 2 * FFN_H), _rows(tm, D_MODEL), _whole((1, D_MODEL)),
                  pl.BlockSpec((D_MODEL, 2 * FFN_H), lambda i: (0, 0), pipeline_mode=pl.Buffered(1)),
                  pl.BlockSpec((FFN_H, D_MODEL), lambda i: (0, 0), pipeline_mode=pl.Buffered(1))],
        out_specs=[_rows(tm, 2 * FFN_H), _rows(tm, FFN_H), _rows(tm, D_MODEL), _rows(tm, D_MODEL),
                   _whole((1, D_MODEL))],
        out_shape=[jax.ShapeDtypeStruct((s, 2 * FFN_H), BF16), jax.ShapeDtypeStruct((s, FFN_H), BF16),
                   jax.ShapeDtypeStruct((s, D_MODEL), BF16), jax.ShapeDtypeStruct((s, D_MODEL), F32),
                   jax.ShapeDtypeStruct((1, D_MODEL), F32)],
        compiler_params=_cp("arbitrary"),
    )(dxo, gu, xm, g, wgu, wd)


def _matmul_tn(a, b, tm, tn, out_dtype=BF16):
    s, m = a.shape
    n = b.shape[1]

    def body(a_ref, b_ref, o_ref):
        o_ref[...] = _dot_tn(a_ref[...].astype(BF16), b_ref[...].astype(BF16)).astype(out_dtype)

    return pl.pallas_call(
        body, name="weight_grad", grid=(m // tm, n // tn),
        in_specs=[pl.BlockSpec((s, tm), lambda i, j: (0, i)), pl.BlockSpec((s, tn), lambda i, j: (0, j))],
        out_specs=pl.BlockSpec((tm, tn), lambda i, j: (i, j)),
        out_shape=jax.ShapeDtypeStruct((m, n), out_dtype),
        compiler_params=_cp("parallel", "parallel"),
    )(a, b)


def _bwd_out(dxm, yc, ys, yb, g, w):
    s = dxm.shape[0]
    tm = TOKEN_TILE

    def body(dxm_ref, yc_ref, ys_ref, yb_ref, g_ref, w_ref, dyc_ref, dys_ref, dyb_ref, cat_ref, dg_ref):
        stats = _group_norms(yc_ref[...], ys_ref[...], yb_ref[...])
        g = g_ref[...]
        cat_ref[...] = (jnp.concatenate([xh for _, xh in stats], axis=1) * g).astype(BF16)
        dcat = _dot_nt(dxm_ref[...].astype(BF16), w_ref[...])
        dgs = []
        off = 0
        for (r, xh), out in zip(stats, (dyc_ref, dys_ref, dyb_ref)):
            cols = slice(off, off + xh.shape[1])
            dx, dgrow = _rms_bwd(xh, r, g[:, cols], dcat[:, cols])
            out[...] = dx
            dgs.append(_colsum(dgrow))
            off += xh.shape[1]
        _accumulate(dg_ref, jnp.concatenate(dgs, axis=1))

    return pl.pallas_call(
        body, name="bwd_out", grid=(s // tm,),
        in_specs=[_rows(tm, D_MODEL), _rows(tm, CONV_W), _rows(tm, SG_W), _rows(tm, SB_W),
                  _whole((1, D_MODEL)), _whole((D_MODEL, D_MODEL))],
        out_specs=[_rows(tm, CONV_W), _rows(tm, SG_W), _rows(tm, SB_W), _rows(tm, D_MODEL),
                   _whole((1, D_MODEL))],
        out_shape=[jax.ShapeDtypeStruct((s, CONV_W), F32), jax.ShapeDtypeStruct((s, SG_W), F32),
                   jax.ShapeDtypeStruct((s, SB_W), F32), jax.ShapeDtypeStruct((s, D_MODEL), BF16),
                   jax.ShapeDtypeStruct((1, D_MODEL), F32)],
        compiler_params=_cp("arbitrary"),
    )(dxm, yc, ys, yb, g, w)


def _bwd_sb(qn, kst, vst, dy, ltot):
    s = qn.shape[0]

    def body(q_ref, k_ref, v_ref, do_ref, lt_ref, dq_ref, dk_ref, dv_ref):
        i = pl.program_id(1)

        @pl.when(i == 0)
        def _():
            dk_ref[...] = jnp.zeros_like(dk_ref)
            dv_ref[...] = jnp.zeros_like(dv_ref)

        q = q_ref[...]
        dob = do_ref[...].astype(BF16)
        lt = lt_ref[...]
        ltot = _per_head(lt[:, 0:1], lt[:, HEAD_DIM:HEAD_DIM + 1])
        upto = _pair_tri("upto")
        before = _pair_tri("before")
        last0, last1 = slice(CHUNK - 1, CHUNK), slice(PAIR - 1, PAIR)

        def block(kb, carry, masked):
            dq, p0, p1, e0, e1 = carry
            r0 = pl.multiple_of(kb * PAIR, PAIR)
            kt = k_ref[pl.ds(r0, PAIR), :]
            vt = v_ref[pl.ds(r0, PAIR), :]
            z, sp, lnb, mask = _sb_scores(q, kt, i * SB_Q, kb * CHUNK, masked)
            pin = _dot_split(lnb, upto) + _per_head(p0, p1)
            sig = jnp.exp(z - sp)
            att = jnp.exp(z - sp + (ltot - pin))
            if masked:
                att = jnp.where(mask, att, 0.0)
            e = att * _dot_nt(dob, vt)
            ebefore = _dot_split(e, before) + _per_head(e0, e1)
            dz = e - sig * (e + ebefore)
            if masked:
                dz = jnp.where(mask, dz, 0.0)
            dzb = dz.astype(BF16)
            dk_ref[pl.ds(r0, PAIR), :] += _dot_tn(dzb, q)
            dv_ref[pl.ds(r0, PAIR), :] += _dot_tn(att.astype(BF16), dob)
            return (dq + _dot(dzb, kt), pin[:, last0], pin[:, last1],
                    ebefore[:, last0] + e[:, last0], ebefore[:, last1] + e[:, last1])

        zero = jnp.zeros((SB_Q, 1), F32)
        carry = (jnp.zeros((SB_Q, CHUNK), F32), zero, zero, zero, zero)
        carry = lax.fori_loop(0, 2 * i, lambda kb, c: block(kb, c, False), carry)
        carry = block(2 * i, carry, True)
        carry = block(2 * i + 1, carry, True)
        dq_ref[...] = carry[0]

    blk = pl.BlockSpec((SB_Q, CHUNK), lambda p, i: (i, p))
    seq = pl.BlockSpec((2 * s, CHUNK), lambda p, i: (0, p))
    return pl.pallas_call(
        body, name="bwd_sb", grid=(SB_W // CHUNK, s // SB_Q),
        in_specs=[blk, seq, seq, blk, blk],
        out_specs=[blk, seq, seq],
        out_shape=[jax.ShapeDtypeStruct((s, SB_W), F32), jax.ShapeDtypeStruct((2 * s, SB_W), F32),
                   jax.ShapeDtypeStruct((2 * s, SB_W), F32)],
        compiler_params=_cp("parallel", "arbitrary"),
    )(qn, kst, vst, dy, ltot)


def _head_sum(row):
    acc = row[:, 0:HEAD_DIM]
    for h in range(1, SB_W // HEAD_DIM):
        acc = acc + row[:, h * HEAD_DIM:(h + 1) * HEAD_DIM]
    return acc


def _bwd_qk(proj, dqs, dkn, dv, qg, kg):
    s = proj.shape[0]
    tm = TOKEN_TILE

    def body(q_ref, k_ref, dqs_ref, dkn_ref, dv_ref, qg_ref, kg_ref, dp_ref, dqg_ref, dkg_ref, qacc, kacc):
        i = pl.program_id(0)
        gm = _group_mean_matrix(SB_W, HEAD_DIM)

        def one(x, dy, g, acc):
            r = lax.rsqrt(_dot_split(x * x, gm) + RMS_EPS)
            xh = x * r
            dxh = dy * g
            _accumulate(acc, _colsum(dy * xh))
            return r * (dxh - xh * _dot_split(dxh * xh, gm))

        dq = one(q_ref[...], dqs_ref[...] * (HEAD_DIM ** -0.5), qg_ref[...], qacc)
        dk = one(k_ref[...], _unstack_heads(dkn_ref[...]), kg_ref[...], kacc)
        dp_ref[...] = jnp.concatenate([dq, dk, _unstack_heads(dv_ref[...])], axis=1).astype(BF16)

        @pl.when(i == pl.num_programs(0) - 1)
        def _():
            dqg_ref[...] = _head_sum(qacc[...])
            dkg_ref[...] = _head_sum(kacc[...])

    return pl.pallas_call(
        body, name="bwd_qk", grid=(s // tm,),
        in_specs=[_rows(tm, SB_W, OFF_SB // SB_W), _rows(tm, SB_W, OFF_SB // SB_W + 1),
                  _rows(tm, SB_W), _rows(2 * tm, SB_W), _rows(2 * tm, SB_W), _whole((1, SB_W)), _whole((1, SB_W))],
        out_specs=[_rows(tm, 3 * SB_W), _whole((1, HEAD_DIM)), _whole((1, HEAD_DIM))],
        out_shape=[jax.ShapeDtypeStruct((s, 3 * SB_W), BF16), jax.ShapeDtypeStruct((1, HEAD_DIM), F32),
                   jax.ShapeDtypeStruct((1, HEAD_DIM), F32)],
        scratch_shapes=[pltpu.VMEM((1, SB_W), F32), pltpu.VMEM((1, SB_W), F32)],
        compiler_params=_cp("arbitrary"),
    )(proj, proj, dqs, dkn, dv, qg, kg)


def _bwd_sg(proj, dy, lg, lb, w, bias):
    s = proj.shape[0]
    tm = TOKEN_TILE
    nh = SG_W // HEAD_DIM

    def body(p_ref, dy_ref, lg_ref, lb_ref, w_ref, bias_ref, dp_ref, dlg_ref, dlb_ref, dw_ref, db_ref, dbias):
        i = pl.program_id(0)
        uv = p_ref[...]
        ge = _gelu(uv)
        u = ge[:, :SG_W]
        r, xh = _ln_stats(ge[:, SG_W:])
        vln = (xh * lg_ref[...] + lb_ref[...]).astype(BF16)
        dy = dy_ref[...]
        tril, lane_head = _sg_masks()

        @pl.when(i == 0)
        def _():
            dw_ref[...] = jnp.zeros_like(dw_ref)
            dbias[...] = jnp.zeros_like(dbias)

        dus, dvlns = [], []
        for c in range(tm // CHUNK):
            rows = slice(c * CHUNK, (c + 1) * CHUNK)
            vc = vln[rows]
            dus.append(dy[rows] * _sg_mix(w_ref, bias_ref, vc, tril, lane_head))
            dm = dy[rows] * u[rows]
            dbias[...] += dm
            dvc = jnp.zeros((CHUNK, SG_W), F32)
            for h in range(nh):
                dmh = jnp.where(lane_head == h, dm, 0.0).astype(BF16)
                dw_ref[h] += jnp.where(tril, _dot_nt(dmh, vc), 0.0)
                wm = jnp.where(tril, w_ref[h], 0.0).astype(BF16)
                dvc = dvc + _dot_tn(wm, dmh)
            dvlns.append(dvc)
        du = jnp.concatenate(dus, axis=0)
        dvln = jnp.concatenate(dvlns, axis=0)
        _accumulate(dlg_ref, _colsum(dvln * xh))
        _accumulate(dlb_ref, _colsum(dvln))
        dv = _ln_bwd(xh, r, lg_ref[...], dvln)
        dp_ref[...] = (jnp.concatenate([du, dv], axis=1) * _gelu_grad(uv)).astype(BF16)

        @pl.when(i == pl.num_programs(0) - 1)
        def _():
            lane = lax.broadcasted_iota(jnp.int32, (CHUNK, CHUNK), 1)
            acc = dbias[...]
            out = jnp.zeros((CHUNK, CHUNK), F32)
            for h in range(nh):
                hs = jnp.sum(acc[:, h * HEAD_DIM:(h + 1) * HEAD_DIM], axis=1, keepdims=True)
                out = out + jnp.where(lane == h, hs, 0.0)
            db_ref[...] = out

    return pl.pallas_call(
        body, name="bwd_sg", grid=(s // tm,),
        in_specs=[_rows(tm, 2 * SG_W, 1), _rows(tm, SG_W), _whole((1, SG_W)), _whole((1, SG_W)),
                  _whole((nh, CHUNK, CHUNK)), _whole((CHUNK, SG_W))],
        out_specs=[_rows(tm, 2 * SG_W), _whole((1, SG_W)), _whole((1, SG_W)), _whole((nh, CHUNK, CHUNK)),
                   _whole((CHUNK, CHUNK))],
        out_shape=[jax.ShapeDtypeStruct((s, 2 * SG_W), BF16), jax.ShapeDtypeStruct((1, SG_W), F32),
                   jax.ShapeDtypeStruct((1, SG_W), F32), jax.ShapeDtypeStruct((nh, CHUNK, CHUNK), F32),
                   jax.ShapeDtypeStruct((CHUNK, CHUNK), F32)],
        scratch_shapes=[pltpu.VMEM((CHUNK, SG_W), F32)],
        compiler_params=_cp("arbitrary"),
    )(proj, dy, lg, lb, w, bias)


def _bwd_conv(proj, dy, w, b, lg, lb):
    s = proj.shape[0]

    def body(p_ref, dy_ref, w_ref, b_ref, lg_ref, lb_ref, dp_ref, dw_ref, db_ref, dlg_ref, dlb_ref,
             abuf, dcbuf):
        _glu_fill(p_ref, abuf, s)
        dcbuf[pl.ds(s, CHUNK), :] = jnp.zeros((CHUNK, CONV_W), F32)
        dw_ref[...] = jnp.zeros_like(dw_ref)

        def chunk(c, carry):
            db, dlg, dlb = carry
            r0 = pl.multiple_of(c * CHUNK, CHUNK)
            acc, win = _conv_window(abuf, r0, w_ref)
            r, xh = _ln_stats(acc + b_ref[...])
            ln = xh * lg_ref[...] + lb_ref[...]
            sg = _sigmoid(ln)
            dl = dy_ref[pl.ds(r0, CHUNK), :] * (sg * (1.0 + ln * (1.0 - sg)))
            dc = _ln_bwd(xh, r, lg_ref[...], dl)
            dcbuf[pl.ds(r0, CHUNK), :] = dc
            for k in range(CONV_K):
                sh = win if k == 0 else pltpu.roll(win, k, axis=0)
                dw_ref[CONV_K - 1 - k:CONV_K - k, :] += _colsum(dc * sh[32:, :])
            return db + _colsum(dc), dlg + _colsum(dl * xh), dlb + _colsum(dl)

        zero = jnp.zeros((1, CONV_W), F32)
        db, dlg, dlb = lax.fori_loop(0, s // CHUNK, chunk, (zero, zero, zero))
        db_ref[...] = db
        dlg_ref[...] = dlg
        dlb_ref[...] = dlb

        def chunk_back(c, carry):
            r0 = pl.multiple_of(c * CHUNK, CHUNK)
            win = dcbuf[pl.ds(r0, CHUNK + 32), :]
            da = jnp.zeros((CHUNK, CONV_W), F32)
            for k in range(CONV_K):
                sh = win if k == 0 else pltpu.roll(win, CHUNK + 32 - k, axis=0)
                da = da + sh[:CHUNK, :] * w_ref[CONV_K - 1 - k:CONV_K - k, :]
            pv = p_ref[pl.ds(r0, CHUNK), :]
            val, sg = pv[:, :CONV_W], _sigmoid(pv[:, CONV_W:])
            dp_ref[pl.ds(r0, CHUNK), :] = jnp.concatenate([da * sg, da * val * sg * (1.0 - sg)], axis=1).astype(BF16)
            return carry

        lax.fori_loop(0, s // CHUNK, chunk_back, 0)

    row = _whole((1, CONV_W))
    return pl.pallas_call(
        body, name="bwd_conv", grid=(1,),
        in_specs=[pl.BlockSpec((s, 2 * CONV_W), lambda i: (0, 0)), _whole((s, CONV_W)),
                  _whole((CONV_K, CONV_W)), row, row, row],
        out_specs=[_whole((s, 2 * CONV_W)), _whole((CONV_K, CONV_W)), row, row, row],
        out_shape=[jax.ShapeDtypeStruct((s, 2 * CONV_W), BF16), jax.ShapeDtypeStruct((CONV_K, CONV_W), F32)]
        + [jax.ShapeDtypeStruct((1, CONV_W), F32)] * 3,
        scratch_shapes=[pltpu.VMEM((s + CHUNK, CONV_W), F32), pltpu.VMEM((s + CHUNK, CONV_W), F32)],
        compiler_params=_cp("arbitrary"),
    )(proj, dy, w, b, lg, lb)


def _bwd_in(dpc, dps, dpb, x, g, w, dxm):
    s = x.shape[0]
    tm = TOKEN_TILE

    def body(dpc_ref, dps_ref, dpb_ref, x_ref, g_ref, w_ref, dxm_ref, dx_ref, h_ref, dp_ref, dg_ref):
        dp = jnp.concatenate([dpc_ref[...], dps_ref[...], dpb_ref[...]], axis=1)
        dp_ref[...] = dp
        dh = _dot_nt(dp, w_ref[...])
        r, xh = _rms_stats(x_ref[...])
        h_ref[...] = (xh * g_ref[...]).astype(BF16)
        dx, dgrow = _rms_bwd(xh, r, g_ref[...], dh)
        dx_ref[...] = dxm_ref[...] + dx
        _accumulate(dg_ref, _colsum(dgrow))

    return pl.pallas_call(
        body, name="bwd_in", grid=(s // tm,),
        in_specs=[_rows(tm, 2 * CONV_W), _rows(tm, 2 * SG_W), _rows(tm, 3 * SB_W), _rows(tm, D_MODEL),
                  _whole((1, D_MODEL)), _whole((D_MODEL, IN_W)), _rows(tm, D_MODEL)],
        out_specs=[_rows(tm, D_MODEL), _rows(tm, D_MODEL), _rows(tm, IN_W), _whole((1, D_MODEL))],
        out_shape=[jax.ShapeDtypeStruct((s, D_MODEL), F32), jax.ShapeDtypeStruct((s, D_MODEL), BF16),
                   jax.ShapeDtypeStruct((s, IN_W), BF16), jax.ShapeDtypeStruct((1, D_MODEL), F32)],
        compiler_params=_cp("arbitrary"),
    )(dpc, dps, dpb, x, g, w, dxm)


SMALL = ("mix_norm_g", "conv_w", "conv_b", "conv_ln_g", "conv_ln_b", "sg_ln_g", "sg_ln_b", "sg_w", "sg_b",
         "q_norm_g", "k_norm_g", "out_norm_g", "ffn_norm_g")
LARGE = ("w_in", "w_out", "w_gate_up", "w_down")


def _row(v):
    return v.reshape(1, -1)


def _layer_params(p, l):
    q = {k: v[l] for k, v in p.items()}
    return dict(
        q,
        mix_norm_g=_row(q["mix_norm_g"]), conv_b=_row(q["conv_b"]), conv_ln_g=_row(q["conv_ln_g"]),
        conv_ln_b=_row(q["conv_ln_b"]), sg_ln_g=_row(q["sg_ln_g"]), sg_ln_b=_row(q["sg_ln_b"]),
        out_norm_g=_row(q["out_norm_g"]), ffn_norm_g=_row(q["ffn_norm_g"]),
        qg=_row(jnp.tile(q["q_norm_g"], SB_W // HEAD_DIM)), kg=_row(jnp.tile(q["k_norm_g"], SB_W // HEAD_DIM)),
        sg_bias=jnp.repeat(q["sg_b"].T, HEAD_DIM, axis=1),
    )


def _layer_fwd(x, q):
    proj, qn, kn, vb = _fwd_in(x, q["mix_norm_g"], q["w_in"], q["qg"], q["kg"])
    yc = _fwd_conv(proj, q["conv_w"], q["conv_b"], q["conv_ln_g"], q["conv_ln_b"])
    ys = _fwd_sg(proj, q["sg_ln_g"], q["sg_ln_b"], q["sg_w"], q["sg_bias"])
    yb, lt = _fwd_sb(qn, kn, vb)
    xm = _fwd_out(yc, ys, yb, q["out_norm_g"], q["w_out"], x)
    gu, xo = _fwd_ffn(xm, q["ffn_norm_g"], q["w_gate_up"], q["w_down"])
    return xo, dict(x=x, proj=proj, qn=qn, kn=kn, vb=vb, lt=lt, yc=yc, ys=ys, yb=yb, xm=xm, gu=gu)


def _layer_bwd(dxo, q, st):
    dgu, act, h2, dxm, d_ffn_g = _bwd_ffn(dxo, st["gu"], st["xm"], q["ffn_norm_g"], q["w_gate_up"], q["w_down"])
    d_wgu = _matmul_tn(h2, dgu, 512, 512)
    d_wd = _matmul_tn(act, dxo, 256, 512)
    dyc, dys, dyb, cat, d_out_g = _bwd_out(dxm, st["yc"], st["ys"], st["yb"], q["out_norm_g"], q["w_out"])
    d_wo = _matmul_tn(cat, dxm, 512, 512)
    dqs, dkn, dv = _bwd_sb(st["qn"], st["kn"], st["vb"], dyb, st["lt"])
    dpb, d_qg, d_kg = _bwd_qk(st["proj"], dqs, dkn, dv, q["qg"], q["kg"])
    dps, d_sg_lg, d_sg_lb, d_sg_w, d_sg_b = _bwd_sg(st["proj"], dys, q["sg_ln_g"], q["sg_ln_b"], q["sg_w"],
                                                    q["sg_bias"])
    dpc, d_conv_w, d_conv_b, d_conv_lg, d_conv_lb = _bwd_conv(st["proj"], dyc, q["conv_w"], q["conv_b"],
                                                              q["conv_ln_g"], q["conv_ln_b"])
    dx, h1, dp, d_mix_g = _bwd_in(dpc, dps, dpb, st["x"], q["mix_norm_g"], q["w_in"], dxm)
    d_win = _matmul_tn(h1, dp, 512, 512)
    grads = dict(
        mix_norm_g=d_mix_g[0], w_in=d_win, conv_w=d_conv_w, conv_b=d_conv_b[0], conv_ln_g=d_conv_lg[0],
        conv_ln_b=d_conv_lb[0], sg_ln_g=d_sg_lg[0], sg_ln_b=d_sg_lb[0], sg_w=d_sg_w,
        sg_b=d_sg_b[:, :SG_W // HEAD_DIM].T, q_norm_g=d_qg[0], k_norm_g=d_kg[0], out_norm_g=d_out_g[0],
        w_out=d_wo, ffn_norm_g=d_ffn_g[0], w_gate_up=d_wgu, w_down=d_wd)
    return dx, grads


def _local_step(x, target, p):
    qs = [_layer_params(p, l) for l in range(DEPTH)]
    stash = []
    for l in range(DEPTH):
        x, st = _layer_fwd(x, qs[l])
        stash.append(st)
    loss, dx = _loss_head(x, target)
    grads = [None] * DEPTH
    for l in reversed(range(DEPTH)):
        dx, grads[l] = _layer_bwd(dx, qs[l], stash[l])
    return loss, dx, grads


def _position():
    x, y, c = lax.axis_index("x"), lax.axis_index("y"), lax.axis_index("c")
    return x, y, c


def _flat(px, py, pc):
    return 4 * px + 2 * py + pc


HBM = pl.BlockSpec(memory_space=pl.ANY)


def _all_gather(arrs, name):
    n = len(arrs)

    def body(*refs):
        ins, outs = refs[:n], refs[n:2 * n]
        send_sems, recv_sems, local_sems = refs[2 * n:]
        x, y, c = _position()
        me, sibling = (x, y, c), (x, y, 1 - c)
        chips = [(1 - x, y), (x, 1 - y), (1 - x, 1 - y)]

        def copy(a, k, block, to, src=None):
            dst = outs[a].at[_flat(*block)]
            return pltpu.make_async_remote_copy(
                src_ref=dst if src is None else src, dst_ref=dst, send_sem=send_sems.at[a, k],
                recv_sem=recv_sems.at[a, k], device_id=to, device_id_type=MESH)

        mine, first, passed = [], [], []
        for a in range(n):
            cp = pltpu.make_async_copy(ins[a], outs[a].at[_flat(*me)], local_sems.at[a])
            cp.start()
            mine.append(cp)
            first.append(copy(a, 0, me, sibling, src=ins[a]))
            first += [copy(a, 1 + j, me, (*chip, c), src=ins[a]) for j, chip in enumerate(chips)]
        for cp in first:
            cp.start()
        for a in range(n):
            for j, chip in enumerate(chips):
                copy(a, 1 + j, (*chip, c), me).wait_recv()
                fwd = copy(a, 4 + j, (*chip, c), sibling)
                fwd.start()
                passed.append(fwd)
        for a in range(n):
            copy(a, 0, sibling, me).wait_recv()
            for j, chip in enumerate(chips):
                copy(a, 4 + j, (*chip, 1 - c), me).wait_recv()
        for cp in first + passed:
            cp.wait_send()
        for cp in mine:
            cp.wait()

    return pl.pallas_call(
        body, name=name,
        in_specs=[HBM] * n, out_specs=[HBM] * n,
        out_shape=[jax.ShapeDtypeStruct((N_DEV,) + a.shape, a.dtype) for a in arrs],
        scratch_shapes=[pltpu.SemaphoreType.DMA((n, 7)), pltpu.SemaphoreType.DMA((n, 7)),
                        pltpu.SemaphoreType.DMA((n,))],
    )(*arrs)


def _all_to_all(arrs, name):
    n = len(arrs)

    def body(*refs):
        ins, outs = refs[:n], refs[n:2 * n]
        send_sems, recv_sems, local_sems = refs[2 * n:]
        x, y, c = _position()
        me = _flat(x, y, c)
        peers = [(x ^ (k >> 2 & 1), y ^ (k >> 1 & 1), c ^ (k & 1)) for k in range(1, N_DEV)]

        def copy(a, k, peer):
            return pltpu.make_async_remote_copy(
                src_ref=ins[a].at[_flat(*peer)], dst_ref=outs[a].at[me], send_sem=send_sems.at[a, k],
                recv_sem=recv_sems.at[a, k], device_id=peer, device_id_type=MESH)

        def arrival(a, k, peer):
            dst = outs[a].at[_flat(*peer)]
            return pltpu.make_async_remote_copy(
                src_ref=dst, dst_ref=dst, send_sem=send_sems.at[a, k], recv_sem=recv_sems.at[a, k],
                device_id=peer, device_id_type=MESH)

        mine, sent = [], []
        for a in range(n):
            cp = pltpu.make_async_copy(ins[a].at[me], outs[a].at[me], local_sems.at[a])
            cp.start()
            mine.append(cp)
            for k, peer in enumerate(peers):
                cp = copy(a, k, peer)
                cp.start()
                sent.append(cp)
        for a in range(n):
            for k, peer in enumerate(peers):
                arrival(a, k, peer).wait_recv()
        for cp in sent:
            cp.wait_send()
        for cp in mine:
            cp.wait()

    return pl.pallas_call(
        body, name=name,
        in_specs=[HBM] * n, out_specs=[HBM] * n,
        out_shape=[jax.ShapeDtypeStruct(a.shape, a.dtype) for a in arrs],
        scratch_shapes=[pltpu.SemaphoreType.DMA((n, 7)), pltpu.SemaphoreType.DMA((n, 7)),
                        pltpu.SemaphoreType.DMA((n,))],
    )(*arrs)


def _adamw(parts, w, m, v, tr):
    rows, cols = w.shape

    def body(p_ref, w_ref, m_ref, v_ref, g_ref, d_ref, nm_ref, nv_ref):
        g = p_ref[0].astype(F32)
        for j in range(1, N_DEV):
            g = g + p_ref[j].astype(F32)
        g_ref[...] = g
        m = ADAM_B1 * m_ref[...] + (1.0 - ADAM_B1) * g
        v = ADAM_B2 * v_ref[...] + (1.0 - ADAM_B2) * (g * g)
        nm_ref[...] = m
        nv_ref[...] = v
        m_hat = m / (1.0 - ADAM_B1 ** ADAM_STEP)
        v_hat = v / (1.0 - ADAM_B2 ** ADAM_STEP)
        d_ref[...] = -ADAM_LR * (m_hat / (jnp.sqrt(v_hat) + ADAM_EPS) + ADAM_WD * w_ref[...])

    blk = pl.BlockSpec((tr, cols), lambda i: (i, 0))
    return pl.pallas_call(
        body, name="adamw", grid=(rows // tr,),
        in_specs=[pl.BlockSpec((N_DEV, tr, cols), lambda i: (0, i, 0)), blk, blk, blk],
        out_specs=[blk] * 4,
        out_shape=[jax.ShapeDtypeStruct((rows, cols), F32)] * 4,
        compiler_params=_cp("parallel"),
    )(parts, w, m, v)


def _adamw_tiled(parts, w, m, v):
    shape = w.shape
    cols = shape[-1]
    rows = w.size // cols
    tr = rows
    for cand in (512, 256, 128, 64, 32, 16, 8):
        if rows % cand == 0 and rows > cand:
            tr = cand
            break
    outs = _adamw(parts.reshape(N_DEV, rows, cols), w.reshape(rows, cols), m.reshape(rows, cols),
                  v.reshape(rows, cols), tr)
    return [o.reshape(shape) for o in outs]


def _to_columns(g):
    n, l, r, c = g.shape
    return jnp.transpose(g, (1, 2, 0, 3)).reshape(l, r, n * c)


def _to_rows(g):
    n, l, r, c = g.shape
    return jnp.transpose(g, (1, 0, 2, 3)).reshape(l, n * r, c)


def _from_columns(g):
    l, r, c = g.shape
    return jnp.transpose(g.reshape(l, r, N_DEV, c // N_DEV), (2, 0, 1, 3))


def _from_rows(g):
    l, r, c = g.shape
    return jnp.transpose(g.reshape(l, N_DEV, r // N_DEV, c), (1, 0, 2, 3))


PACK_LANES = 128


def _pack(arrs):
    parts = []
    for a in arrs:
        flat = a.reshape(-1)
        pad = -flat.size % (8 * PACK_LANES)
        parts.append(jnp.pad(flat, (0, pad)))
    return jnp.concatenate(parts).reshape(-1, PACK_LANES)


def _unpack(packed, shapes):
    flat = packed.reshape(-1)
    outs, off = [], 0
    for shp in shapes:
        size = 1
        for d in shp:
            size *= d
        outs.append(flat[off:off + size].reshape(shp))
        off += size + (-size % (8 * PACK_LANES))
    return outs


def kernel(x, mix_norm_g, w_in, conv_w, conv_b, conv_ln_g, conv_ln_b, sg_ln_g, sg_ln_b, sg_w, sg_b, q_norm_g, k_norm_g, out_norm_g, w_out, ffn_norm_g, w_gate_up, w_down, loss_target, m_mix_norm_g, m_w_in, m_conv_w, m_conv_b, m_conv_ln_g, m_conv_ln_b, m_sg_ln_g, m_sg_ln_b, m_sg_w, m_sg_b, m_q_norm_g, m_k_norm_g, m_out_norm_g, m_w_out, m_ffn_norm_g, m_w_gate_up, m_w_down, v_mix_norm_g, v_w_in, v_conv_w, v_conv_b, v_conv_ln_g, v_conv_ln_b, v_sg_ln_g, v_sg_ln_b, v_sg_w, v_sg_b, v_q_norm_g, v_k_norm_g, v_out_norm_g, v_w_out, v_ffn_norm_g, v_w_gate_up, v_w_down):
    names = SMALL[:1] + LARGE[:1] + SMALL[1:12] + LARGE[1:2] + SMALL[12:] + LARGE[2:]
    w = dict(mix_norm_g=mix_norm_g, w_in=w_in, conv_w=conv_w, conv_b=conv_b, conv_ln_g=conv_ln_g,
             conv_ln_b=conv_ln_b, sg_ln_g=sg_ln_g, sg_ln_b=sg_ln_b, sg_w=sg_w, sg_b=sg_b, q_norm_g=q_norm_g,
             k_norm_g=k_norm_g, out_norm_g=out_norm_g, w_out=w_out, ffn_norm_g=ffn_norm_g,
             w_gate_up=w_gate_up, w_down=w_down)
    m = dict(mix_norm_g=m_mix_norm_g, w_in=m_w_in, conv_w=m_conv_w, conv_b=m_conv_b, conv_ln_g=m_conv_ln_g,
             conv_ln_b=m_conv_ln_b, sg_ln_g=m_sg_ln_g, sg_ln_b=m_sg_ln_b, sg_w=m_sg_w, sg_b=m_sg_b,
             q_norm_g=m_q_norm_g, k_norm_g=m_k_norm_g, out_norm_g=m_out_norm_g, w_out=m_w_out,
             ffn_norm_g=m_ffn_norm_g, w_gate_up=m_w_gate_up, w_down=m_w_down)
    v = dict(mix_norm_g=v_mix_norm_g, w_in=v_w_in, conv_w=v_conv_w, conv_b=v_conv_b, conv_ln_g=v_conv_ln_g,
             conv_ln_b=v_conv_ln_b, sg_ln_g=v_sg_ln_g, sg_ln_b=v_sg_ln_b, sg_w=v_sg_w, sg_b=v_sg_b,
             q_norm_g=v_q_norm_g, k_norm_g=v_k_norm_g, out_norm_g=v_out_norm_g, w_out=v_w_out,
             ffn_norm_g=v_ffn_norm_g, w_gate_up=v_w_gate_up, w_down=v_w_down)
    xpos, ypos, cpos = _position()
    me = _flat(xpos, ypos, cpos)
    conv_cols = conv_w.shape[-1]

    g_in, g_out, g_gu, g_down, g_conv = _all_gather(
        [w["w_in"].astype(BF16), w["w_out"].astype(BF16), w["w_gate_up"].astype(BF16),
         w["w_down"].astype(BF16), w["conv_w"]], "gather_weights")
    full = dict(w, w_in=_to_columns(g_in), w_out=_to_rows(g_out), w_gate_up=_to_columns(g_gu),
                w_down=_to_rows(g_down), conv_w=_to_columns(g_conv))

    loss, dx, grads = _local_step(x[0], loss_target[0], full)
    loss = lax.psum(loss[0, 0], ("x", "y", "c"))

    stack = lambda k: jnp.stack([grads[l][k] for l in range(DEPTH)])
    r_in, r_out, r_gu, r_down = _all_to_all(
        [_from_columns(stack("w_in")), _from_rows(stack("w_out")), _from_columns(stack("w_gate_up")),
         _from_rows(stack("w_down"))], "exchange_grads")
    small_shapes = [(DEPTH,) + grads[0][k].shape for k in SMALL]
    (small_parts,) = _all_gather([_pack([stack(k) for k in SMALL])], "gather_small_grads")
    per_dev = [_unpack(small_parts[j], small_shapes) for j in range(N_DEV)]
    small = [jnp.stack([per_dev[j][i] for j in range(N_DEV)]) for i in range(len(SMALL))]
    conv_i = SMALL.index("conv_w")
    small[conv_i] = lax.dynamic_slice_in_dim(small[conv_i], me * conv_cols, conv_cols, axis=3)

    res = {}
    for k, parts in zip(LARGE, (r_in, r_out, r_gu, r_down)):
        res[k] = _adamw_tiled(parts, w[k], m[k], v[k])
    shard_shapes = [w[k].shape for k in SMALL]
    packed = _adamw_tiled(
        jnp.stack([_pack([small[i][j] for i in range(len(SMALL))]) for j in range(N_DEV)]),
        _pack([w[k] for k in SMALL]), _pack([m[k] for k in SMALL]), _pack([v[k] for k in SMALL]))
    unpacked = [_unpack(o, shard_shapes) for o in packed]
    for i, k in enumerate(SMALL):
        res[k] = [u[i] for u in unpacked]

    return (loss, dx[None], *[res[k][0] for k in names], *[res[k][1] for k in names],
            *[res[k][2] for k in names], *[res[k][3] for k in names])
```

```python
import functools

import jax
import jax.numpy as jnp
from jax import lax
from jax.experimental import pallas as pl
from jax.experimental.pallas import tpu as pltpu

F32 = jnp.float32
BF16 = jnp.bfloat16

D_MODEL = 1024
DEPTH = 4
HEAD_DIM = 64
CONV_W = 256
SG_W = 256
SB_W = 512
IN_W = 2560
FFN_H = 2816
CONV_K = 31
CHUNK = 128
OFF_SG = 2 * CONV_W
OFF_SB = OFF_SG + 2 * SG_W
RMS_EPS = 1e-6
LN_EPS = 1e-5
N_DEV = 8
MESH = pl.DeviceIdType.MESH

ADAM_LR = 0.001
ADAM_B1 = 0.9
ADAM_B2 = 0.999
ADAM_EPS = 1e-08
ADAM_WD = 0.01
ADAM_STEP = 10

TOKEN_TILE = 256
VMEM_LIMIT = 56 * 1024 * 1024


def _cp(*sem):
    return pltpu.CompilerParams(dimension_semantics=sem or None, vmem_limit_bytes=VMEM_LIMIT)


def _dot(a, b):
    return jnp.dot(a, b, preferred_element_type=F32)


def _dot_nt(a, b):
    return lax.dot_general(a, b, (((1,), (1,)), ((), ())), preferred_element_type=F32)


def _dot_tn(a, b):
    return lax.dot_general(a, b, (((0,), (0,)), ((), ())), preferred_element_type=F32)


def _dot_split(x, m):
    hi = x.astype(BF16)
    lo = (x - hi.astype(F32)).astype(BF16)
    return _dot(hi, m) + _dot(lo, m)


def _group_mean_matrix(width, group):
    r = lax.broadcasted_iota(jnp.int32, (width, width), 0) // group
    c = lax.broadcasted_iota(jnp.int32, (width, width), 1) // group
    return jnp.where(r == c, 1.0 / group, 0.0).astype(BF16)


def _sigmoid(x):
    return 1.0 / (1.0 + jnp.exp(-x))


def _gelu(x):
    return 0.5 * x * (1.0 + lax.erf(x * (2.0 ** -0.5)))


def _gelu_grad(x):
    return 0.5 * (1.0 + lax.erf(x * (2.0 ** -0.5))) + x * jnp.exp(-0.5 * x * x) * (0.5 * (2.0 / jnp.pi) ** 0.5)


def _rms_stats(x):
    r = lax.rsqrt(jnp.mean(x * x, axis=-1, keepdims=True) + RMS_EPS)
    return r, x * r


def _rms_bwd(xh, r, g, dy):
    dxh = dy * g
    dx = r * (dxh - xh * jnp.mean(dxh * xh, axis=-1, keepdims=True))
    return dx, dy * xh


def _ln_stats(x):
    mu = jnp.mean(x, axis=-1, keepdims=True)
    xc = x - mu
    r = lax.rsqrt(jnp.mean(xc * xc, axis=-1, keepdims=True) + LN_EPS)
    return r, xc * r


def _ln_bwd(xh, r, g, dy):
    dxh = dy * g
    return r * (dxh - jnp.mean(dxh, axis=-1, keepdims=True) - xh * jnp.mean(dxh * xh, axis=-1, keepdims=True))


def _colsum(x):
    return jnp.sum(x, axis=0, keepdims=True)


def _rows(tm, n, j=0):
    return pl.BlockSpec((tm, n), lambda i: (i, j))


def _whole(shape):
    return pl.BlockSpec(shape, lambda i: (0,) * len(shape))


ORDER_ONLY = pl.BlockSpec(memory_space=pl.ANY)


def _stack_heads(a):
    even = (lax.broadcasted_iota(jnp.int32, a.shape, 1) % (2 * HEAD_DIM)) < HEAD_DIM
    top = jnp.where(even, a, 0.0).astype(BF16)
    bot = jnp.where(even, 0.0, a).astype(BF16)
    parts = []
    for c in range(a.shape[0] // CHUNK):
        rows = slice(c * CHUNK, (c + 1) * CHUNK)
        parts += [top[rows], bot[rows]]
    return jnp.concatenate(parts, axis=0)


def _unstack_heads(st):
    even = (lax.broadcasted_iota(jnp.int32, (CHUNK, st.shape[1]), 1) % (2 * HEAD_DIM)) < HEAD_DIM
    parts = []
    for c in range(st.shape[0] // (2 * CHUNK)):
        top = st[2 * c * CHUNK:(2 * c + 1) * CHUNK]
        bot = st[(2 * c + 1) * CHUNK:(2 * c + 2) * CHUNK]
        parts.append(jnp.where(even, top, bot))
    return jnp.concatenate(parts, axis=0)


def _fwd_in(x, g, w, qg, kg, tok):
    s = x.shape[0]
    tm = TOKEN_TILE

    def body(x_ref, g_ref, w_ref, qg_ref, kg_ref, tok_ref, proj_ref, qn_ref, kn_ref, vb_ref):
        r, xh = _rms_stats(x_ref[...])
        h = (xh * g_ref[...]).astype(BF16)
        proj = _dot(h, w_ref[...])
        proj_ref[...] = proj
        gm = _group_mean_matrix(SB_W, HEAD_DIM)
        q = proj[:, OFF_SB:OFF_SB + SB_W]
        k = proj[:, OFF_SB + SB_W:OFF_SB + 2 * SB_W]
        rq = lax.rsqrt(_dot_split(q * q, gm) + RMS_EPS)
        rk = lax.rsqrt(_dot_split(k * k, gm) + RMS_EPS)
        qn_ref[...] = (q * rq * qg_ref[...] * (HEAD_DIM ** -0.5)).astype(BF16)
        kn_ref[...] = _stack_heads(k * rk * kg_ref[...])
        vb_ref[...] = _stack_heads(proj[:, OFF_SB + 2 * SB_W:])

    return pl.pallas_call(
        body, name="fwd_in", grid=(s // tm,),
        in_specs=[_rows(tm, D_MODEL), _whole((1, D_MODEL)), _whole((D_MODEL, IN_W)),
                  _whole((1, SB_W)), _whole((1, SB_W)), ORDER_ONLY],
        out_specs=[_rows(tm, IN_W), _rows(tm, SB_W), _rows(2 * tm, SB_W), _rows(2 * tm, SB_W)],
        out_shape=[jax.ShapeDtypeStruct((s, IN_W), F32), jax.ShapeDtypeStruct((s, SB_W), BF16),
                   jax.ShapeDtypeStruct((2 * s, SB_W), BF16), jax.ShapeDtypeStruct((2 * s, SB_W), BF16)],
        compiler_params=_cp("parallel"),
    )(x, g, w, qg, kg, tok)


def _conv_window(abuf, r0, w_ref):
    win = abuf[pl.ds(pl.multiple_of(r0 + CHUNK - 32, 32), CHUNK + 32), :]
    acc = jnp.zeros((CHUNK, CONV_W), F32)
    for k in range(CONV_K):
        sh = win if k == 0 else pltpu.roll(win, k, axis=0)
        acc = acc + sh[32:, :] * w_ref[CONV_K - 1 - k:CONV_K - k, :]
    return acc, win


def _glu_fill(p_ref, abuf, s):
    abuf[0:CHUNK, :] = jnp.zeros((CHUNK, CONV_W), F32)

    def fill(c, carry):
        r0 = pl.multiple_of(c * CHUNK, CHUNK)
        pv = p_ref[pl.ds(r0, CHUNK), :]
        abuf[pl.ds(r0 + CHUNK, CHUNK), :] = pv[:, :CONV_W] * _sigmoid(pv[:, CONV_W:])
        return carry

    lax.fori_loop(0, s // CHUNK, fill, 0)


def _fwd_conv(proj, w, b, lg, lb):
    s = proj.shape[0]

    def body(p_ref, w_ref, b_ref, lg_ref, lb_ref, y_ref, abuf):
        _glu_fill(p_ref, abuf, s)

        def chunk(c, carry):
            r0 = pl.multiple_of(c * CHUNK, CHUNK)
            acc, _ = _conv_window(abuf, r0, w_ref)
            r, xh = _ln_stats(acc + b_ref[...])
            ln = xh * lg_ref[...] + lb_ref[...]
            y_ref[pl.ds(r0, CHUNK), :] = ln * _sigmoid(ln)
            return carry

        lax.fori_loop(0, s // CHUNK, chunk, 0)

    return pl.pallas_call(
        body, name="fwd_conv", grid=(1,),
        in_specs=[pl.BlockSpec((s, 2 * CONV_W), lambda i: (0, 0)), _whole((CONV_K, CONV_W)),
                  _whole((1, CONV_W)), _whole((1, CONV_W)), _whole((1, CONV_W))],
        out_specs=_whole((s, CONV_W)),
        out_shape=jax.ShapeDtypeStruct((s, CONV_W), F32),
        scratch_shapes=[pltpu.VMEM((s + CHUNK, CONV_W), F32)],
        compiler_params=_cp("arbitrary"),
    )(proj, w, b, lg, lb)


def _sg_masks():
    row = lax.broadcasted_iota(jnp.int32, (CHUNK, CHUNK), 0)
    col = lax.broadcasted_iota(jnp.int32, (CHUNK, CHUNK), 1)
    lane_head = lax.broadcasted_iota(jnp.int32, (CHUNK, SG_W), 1) // HEAD_DIM
    return row >= col, lane_head


def _sg_mix(w_ref, bias_ref, vc, tril, lane_head):
    mixed = bias_ref[...]
    for h in range(SG_W // HEAD_DIM):
        wm = jnp.where(tril, w_ref[h], 0.0).astype(BF16)
        mixed = mixed + jnp.where(lane_head == h, _dot(wm, vc), 0.0)
    return mixed


def _fwd_sg(proj, lg, lb, w, bias):
    s = proj.shape[0]
    tm = TOKEN_TILE

    def body(p_ref, lg_ref, lb_ref, w_ref, bias_ref, y_ref):
        ge = _gelu(p_ref[...])
        u = ge[:, :SG_W]
        r, xh = _ln_stats(ge[:, SG_W:])
        vln = (xh * lg_ref[...] + lb_ref[...]).astype(BF16)
        tril, lane_head = _sg_masks()
        for c in range(tm // CHUNK):
            rows = slice(c * CHUNK, (c + 1) * CHUNK)
            y_ref[rows, :] = u[rows] * _sg_mix(w_ref, bias_ref, vln[rows], tril, lane_head)

    return pl.pallas_call(
        body, name="fwd_sg", grid=(s // tm,),
        in_specs=[_rows(tm, 2 * SG_W, 1), _whole((1, SG_W)), _whole((1, SG_W)),
                  _whole((SG_W // HEAD_DIM, CHUNK, CHUNK)), _whole((CHUNK, SG_W))],
        out_specs=_rows(tm, SG_W),
        out_shape=jax.ShapeDtypeStruct((s, SG_W), F32),
        compiler_params=_cp("parallel"),
    )(proj, lg, lb, w, bias)


SB_Q = 2 * CHUNK
PAIR = 2 * CHUNK


def _pair_tri(kind):
    row = lax.broadcasted_iota(jnp.int32, (PAIR, PAIR), 0)
    col = lax.broadcasted_iota(jnp.int32, (PAIR, PAIR), 1)
    tri = {"after": row > col, "upto": row <= col, "before": row < col}[kind]
    return jnp.where(((row // CHUNK) == (col // CHUNK)) & tri, 1.0, 0.0).astype(BF16)


def _sb_scores(q, kt, qpos0, kpos0, masked):
    z = _dot_nt(q, kt)
    sp = jnp.maximum(z, 0.0) + jnp.log(1.0 + jnp.exp(-jnp.abs(z)))
    if not masked:
        return z, sp, -sp, None
    row = lax.broadcasted_iota(jnp.int32, z.shape, 0)
    col = lax.broadcasted_iota(jnp.int32, z.shape, 1) % CHUNK
    mask = (kpos0 + col) < (qpos0 + row)
    return z, sp, jnp.where(mask, -sp, 0.0), mask


def _per_head(c0, c1):
    return jnp.concatenate([jnp.broadcast_to(c0, (SB_Q, CHUNK)), jnp.broadcast_to(c1, (SB_Q, CHUNK))], axis=1)


def _fwd_sb(qn, kst, vst):
    s = qn.shape[0]

    def body(q_ref, k_ref, v_ref, o_ref, lt_ref):
        i = pl.program_id(1)
        q = q_ref[...]
        after = _pair_tri("after")

        def block(kb, carry, masked):
            acc, c0, c1 = carry
            r0 = pl.multiple_of(kb * PAIR, PAIR)
            z, sp, lnb, mask = _sb_scores(q, k_ref[pl.ds(r0, PAIR), :], i * SB_Q, kb * CHUNK, masked)
            loc = _dot_split(lnb, after)
            att = jnp.exp(z - sp + loc + _per_head(c0, c1))
            if masked:
                att = jnp.where(mask, att, 0.0)
            acc = acc + _dot(att.astype(BF16), v_ref[pl.ds(r0, PAIR), :])
            return (acc, c0 + loc[:, 0:1] + lnb[:, 0:1], c1 + loc[:, CHUNK:CHUNK + 1] + lnb[:, CHUNK:CHUNK + 1])

        zero = jnp.zeros((SB_Q, 1), F32)
        carry = (jnp.zeros((SB_Q, CHUNK), F32), zero, zero)
        carry = block(2 * i + 1, carry, True)
        carry = block(2 * i, carry, True)
        acc, c0, c1 = lax.fori_loop(0, 2 * i, lambda j, c: block(2 * i - 1 - j, c, False), carry)
        o_ref[...] = acc
        lt_ref[...] = jnp.concatenate([jnp.broadcast_to(c0, (SB_Q, HEAD_DIM)),
                                       jnp.broadcast_to(c1, (SB_Q, HEAD_DIM))], axis=1)

    blk = pl.BlockSpec((SB_Q, CHUNK), lambda p, i: (i, p))
    seq = pl.BlockSpec((2 * s, CHUNK), lambda p, i: (0, p))
    return pl.pallas_call(
        body, name="fwd_sb", grid=(SB_W // CHUNK, s // SB_Q),
        in_specs=[blk, seq, seq],
        out_specs=[blk, blk],
        out_shape=[jax.ShapeDtypeStruct((s, SB_W), F32)] * 2,
        compiler_params=_cp("parallel", "parallel"),
    )(qn, kst, vst)


def _group_norms(yc, ys, yb):
    return [_rms_stats(yc), _rms_stats(ys), _rms_stats(yb)]


def _fwd_out(yc, ys, yb, g, w, x):
    s = x.shape[0]
    tm = TOKEN_TILE

    def body(yc_ref, ys_ref, yb_ref, g_ref, w_ref, x_ref, o_ref):
        stats = _group_norms(yc_ref[...], ys_ref[...], yb_ref[...])
        cat = jnp.concatenate([xh for _, xh in stats], axis=1) * g_ref[...]
        o_ref[...] = x_ref[...] + _dot(cat.astype(BF16), w_ref[...])

    return pl.pallas_call(
        body, name="fwd_out", grid=(s // tm,),
        in_specs=[_rows(tm, CONV_W), _rows(tm, SG_W), _rows(tm, SB_W), _whole((1, D_MODEL)),
                  _whole((D_MODEL, D_MODEL)), _rows(tm, D_MODEL)],
        out_specs=_rows(tm, D_MODEL),
        out_shape=jax.ShapeDtypeStruct((s, D_MODEL), F32),
        compiler_params=_cp("parallel"),
    )(yc, ys, yb, g, w, x)


def _fwd_ffn(x, g, wgu, wd):
    s = x.shape[0]
    tm = TOKEN_TILE

    def body(x_ref, g_ref, wgu_ref, wd_ref, gu_ref, o_ref):
        x = x_ref[...]
        r, xh = _rms_stats(x)
        gu = _dot((xh * g_ref[...]).astype(BF16), wgu_ref[...])
        gu_ref[...] = gu
        gate = gu[:, :FFN_H]
        act = gate * _sigmoid(gate) * gu[:, FFN_H:]
        o_ref[...] = x + _dot(act.astype(BF16), wd_ref[...])

    return pl.pallas_call(
        body, name="fwd_ffn", grid=(s // tm,),
        in_specs=[_rows(tm, D_MODEL), _whole((1, D_MODEL)),
                  pl.BlockSpec((D_MODEL, 2 * FFN_H), lambda i: (0, 0), pipeline_mode=pl.Buffered(1)),
                  pl.BlockSpec((FFN_H, D_MODEL), lambda i: (0, 0), pipeline_mode=pl.Buffered(1))],
        out_specs=[_rows(tm, 2 * FFN_H), _rows(tm, D_MODEL)],
        out_shape=[jax.ShapeDtypeStruct((s, 2 * FFN_H), F32), jax.ShapeDtypeStruct((s, D_MODEL), F32)],
        compiler_params=_cp("parallel"),
    )(x, g, wgu, wd)


def _loss_head(y, target):
    s = y.shape[0]
    tm = TOKEN_TILE

    def body(y_ref, t_ref, l_ref, d_ref):
        @pl.when(pl.program_id(0) == 0)
        def _():
            l_ref[...] = jnp.zeros_like(l_ref)

        err = y_ref[...] - t_ref[...]
        d_ref[...] = err * (1.0 / D_MODEL)
        l_ref[...] += 0.5 * jnp.sum(jnp.mean(err * err, axis=-1, keepdims=True), axis=0, keepdims=True)

    return pl.pallas_call(
        body, name="loss_head", grid=(s // tm,),
        in_specs=[_rows(tm, D_MODEL), _rows(tm, D_MODEL)],
        out_specs=[_whole((1, 1)), _rows(tm, D_MODEL)],
        out_shape=[jax.ShapeDtypeStruct((1, 1), F32), jax.ShapeDtypeStruct((s, D_MODEL), F32)],
        compiler_params=_cp("arbitrary"),
    )(y, target)


def _accumulate(ref, value):
    @pl.when(pl.program_id(0) == 0)
    def _():
        ref[...] = jnp.zeros_like(ref)

    ref[...] += value


def _bwd_ffn(dxo, gu, xm, g, wgu, wd, tok):
    s = dxo.shape[0]
    tm = TOKEN_TILE

    def body(dxo_ref, gu_ref, xm_ref, g_ref, wgu_ref, wd_ref, tok_ref, dgu_ref, act_ref, h_ref, dxm_ref, dg_ref):
        dxo = dxo_ref[...]
        gu = gu_ref[...]
        gate, up = gu[:, :FFN_H], gu[:, FFN_H:]
        sg = _sigmoid(gate)
        sl = gate * sg
        act_ref[...] = (sl * up).astype(BF16)
        dact = _dot_nt(dxo.astype(BF16), wd_ref[...])
        dgate = dact * up * (sg * (1.0 + gate * (1.0 - sg)))
        dgu = jnp.concatenate([dgate, dact * sl], axis=1).astype(BF16)
        dgu_ref[...] = dgu
        dh = _dot_nt(dgu, wgu_ref[...])
        r, xh = _rms_stats(xm_ref[...])
        h_ref[...] = (xh * g_ref[...]).astype(BF16)
        dx, dgrow = _rms_bwd(xh, r, g_ref[...], dh)
        dxm_ref[...] = dxo + dx
        _accumulate(dg_ref, _colsum(dgrow))

    return pl.pallas_call(
        body, name="bwd_ffn", grid=(s // tm,),
        in_specs=[_rows(tm, D_MODEL), _rows(tm, 2 * FFN_H), _rows(tm, D_MODEL), _whole((1, D_MODEL)),
                  pl.BlockSpec((D_MODEL, 2 * FFN_H), lambda i: (0, 0), pipeline_mode=pl.Buffered(1)),
                  pl.BlockSpec((FFN_H, D_MODEL), lambda i: (0, 0), pipeline_mode=pl.Buffered(1)), ORDER_ONLY],
        out_specs=[_rows(tm, 2 * FFN_H), _rows(tm, FFN_H), _rows(tm, D_MODEL), _rows(tm, D_MODEL),
                   _whole((1, D_MODEL))],
        out_shape=[jax.ShapeDtypeStruct((s, 2 * FFN_H), BF16), jax.ShapeDtypeStruct((s, FFN_H), BF16),
                   jax.ShapeDtypeStruct((s, D_MODEL), BF16), jax.ShapeDtypeStruct((s, D_MODEL), F32),
                   jax.ShapeDtypeStruct((1, D_MODEL), F32)],
        compiler_params=_cp("arbitrary"),
    )(dxo, gu, xm, g, wgu, wd, tok)


def _matmul_tn(a, b, tm, tn, out_dtype=BF16):
    s, m = a.shape
    n = b.shape[1]

    def body(a_ref, b_ref, o_ref):
        o_ref[...] = _dot_tn(a_ref[...].astype(BF16), b_ref[...].astype(BF16)).astype(out_dtype)

    return pl.pallas_call(
        body, name="weight_grad", grid=(m // tm, n // tn),
        in_specs=[pl.BlockSpec((s, tm), lambda i, j: (0, i)), pl.BlockSpec((s, tn), lambda i, j: (0, j))],
        out_specs=pl.BlockSpec((tm, tn), lambda i, j: (i, j)),
        out_shape=jax.ShapeDtypeStruct((m, n), out_dtype),
        compiler_params=_cp("parallel", "parallel"),
    )(a, b)


def _bwd_out(dxm, yc, ys, yb, g, w, tok):
    s = dxm.shape[0]
    tm = TOKEN_TILE

    def body(dxm_ref, yc_ref, ys_ref, yb_ref, g_ref, w_ref, tok_ref, dyc_ref, dys_ref, dyb_ref, cat_ref, dg_ref):
        stats = _group_norms(yc_ref[...], ys_ref[...], yb_ref[...])
        g = g_ref[...]
        cat_ref[...] = (jnp.concatenate([xh for _, xh in stats], axis=1) * g).astype(BF16)
        dcat = _dot_nt(dxm_ref[...].astype(BF16), w_ref[...])
        dgs = []
        off = 0
        for (r, xh), out in zip(stats, (dyc_ref, dys_ref, dyb_ref)):
            cols = slice(off, off + xh.shape[1])
            dx, dgrow = _rms_bwd(xh, r, g[:, cols], dcat[:, cols])
            out[...] = dx
            dgs.append(_colsum(dgrow))
            off += xh.shape[1]
        _accumulate(dg_ref, jnp.concatenate(dgs, axis=1))

    return pl.pallas_call(
        body, name="bwd_out", grid=(s // tm,),
        in_specs=[_rows(tm, D_MODEL), _rows(tm, CONV_W), _rows(tm, SG_W), _rows(tm, SB_W),
                  _whole((1, D_MODEL)), _whole((D_MODEL, D_MODEL)), ORDER_ONLY],
        out_specs=[_rows(tm, CONV_W), _rows(tm, SG_W), _rows(tm, SB_W), _rows(tm, D_MODEL),
                   _whole((1, D_MODEL))],
        out_shape=[jax.ShapeDtypeStruct((s, CONV_W), F32), jax.ShapeDtypeStruct((s, SG_W), F32),
                   jax.ShapeDtypeStruct((s, SB_W), F32), jax.ShapeDtypeStruct((s, D_MODEL), BF16),
                   jax.ShapeDtypeStruct((1, D_MODEL), F32)],
        compiler_params=_cp("arbitrary"),
    )(dxm, yc, ys, yb, g, w, tok)


def _bwd_sb(qn, kst, vst, dy, ltot):
    s = qn.shape[0]

    def body(q_ref, k_ref, v_ref, do_ref, lt_ref, dq_ref, dk_ref, dv_ref):
        i = pl.program_id(1)

        @pl.when(i == 0)
        def _():
            dk_ref[...] = jnp.zeros_like(dk_ref)
            dv_ref[...] = jnp.zeros_like(dv_ref)

        q = q_ref[...]
        dob = do_ref[...].astype(BF16)
        lt = lt_ref[...]
        ltot = _per_head(lt[:, 0:1], lt[:, HEAD_DIM:HEAD_DIM + 1])
        upto = _pair_tri("upto")
        before = _pair_tri("before")
        last0, last1 = slice(CHUNK - 1, CHUNK), slice(PAIR - 1, PAIR)

        def block(kb, carry, masked):
            dq, p0, p1, e0, e1 = carry
            r0 = pl.multiple_of(kb * PAIR, PAIR)
            kt = k_ref[pl.ds(r0, PAIR), :]
            vt = v_ref[pl.ds(r0, PAIR), :]
            z, sp, lnb, mask = _sb_scores(q, kt, i * SB_Q, kb * CHUNK, masked)
            pin = _dot_split(lnb, upto) + _per_head(p0, p1)
            sig = jnp.exp(z - sp)
            att = jnp.exp(z - sp + (ltot - pin))
            if masked:
                att = jnp.where(mask, att, 0.0)
            e = att * _dot_nt(dob, vt)
            ebefore = _dot_split(e, before) + _per_head(e0, e1)
            dz = e - sig * (e + ebefore)
            if masked:
                dz = jnp.where(mask, dz, 0.0)
            dzb = dz.astype(BF16)
            dk_ref[pl.ds(r0, PAIR), :] += _dot_tn(dzb, q)
            dv_ref[pl.ds(r0, PAIR), :] += _dot_tn(att.astype(BF16), dob)
            return (dq + _dot(dzb, kt), pin[:, last0], pin[:, last1],
                    ebefore[:, last0] + e[:, last0], ebefore[:, last1] + e[:, last1])

        zero = jnp.zeros((SB_Q, 1), F32)
        carry = (jnp.zeros((SB_Q, CHUNK), F32), zero, zero, zero, zero)
        carry = lax.fori_loop(0, 2 * i, lambda kb, c: block(kb, c, False), carry)
        carry = block(2 * i, carry, True)
        carry = block(2 * i + 1, carry, True)
        dq_ref[...] = carry[0]

    blk = pl.BlockSpec((SB_Q, CHUNK), lambda p, i: (i, p))
    seq = pl.BlockSpec((2 * s, CHUNK), lambda p, i: (0, p))
    return pl.pallas_call(
        body, name="bwd_sb", grid=(SB_W // CHUNK, s // SB_Q),
        in_specs=[blk, seq, seq, blk, blk],
        out_specs=[blk, seq, seq],
        out_shape=[jax.ShapeDtypeStruct((s, SB_W), F32), jax.ShapeDtypeStruct((2 * s, SB_W), F32),
                   jax.ShapeDtypeStruct((2 * s, SB_W), F32)],
        compiler_params=_cp("parallel", "arbitrary"),
    )(qn, kst, vst, dy, ltot)


def _head_sum(row):
    acc = row[:, 0:HEAD_DIM]
    for h in range(1, SB_W // HEAD_DIM):
        acc = acc + row[:, h * HEAD_DIM:(h + 1) * HEAD_DIM]
    return acc


def _bwd_qk(proj, dqs, dkn, dv, qg, kg):
    s = proj.shape[0]
    tm = TOKEN_TILE

    def body(q_ref, k_ref, dqs_ref, dkn_ref, dv_ref, qg_ref, kg_ref, dp_ref, dqg_ref, dkg_ref, qacc, kacc):
        i = pl.program_id(0)
        gm = _group_mean_matrix(SB_W, HEAD_DIM)

        def one(x, dy, g, acc):
            r = lax.rsqrt(_dot_split(x * x, gm) + RMS_EPS)
            xh = x * r
            dxh = dy * g
            _accumulate(acc, _colsum(dy * xh))
            return r * (dxh - xh * _dot_split(dxh * xh, gm))

        dq = one(q_ref[...], dqs_ref[...] * (HEAD_DIM ** -0.5), qg_ref[...], qacc)
        dk = one(k_ref[...], _unstack_heads(dkn_ref[...]), kg_ref[...], kacc)
        dp_ref[...] = jnp.concatenate([dq, dk, _unstack_heads(dv_ref[...])], axis=1).astype(BF16)

        @pl.when(i == pl.num_programs(0) - 1)
        def _():
            dqg_ref[...] = _head_sum(qacc[...])
            dkg_ref[...] = _head_sum(kacc[...])

    return pl.pallas_call(
        body, name="bwd_qk", grid=(s // tm,),
        in_specs=[_rows(tm, SB_W, OFF_SB // SB_W), _rows(tm, SB_W, OFF_SB // SB_W + 1),
                  _rows(tm, SB_W), _rows(2 * tm, SB_W), _rows(2 * tm, SB_W), _whole((1, SB_W)), _whole((1, SB_W))],
        out_specs=[_rows(tm, 3 * SB_W), _whole((1, HEAD_DIM)), _whole((1, HEAD_DIM))],
        out_shape=[jax.ShapeDtypeStruct((s, 3 * SB_W), BF16), jax.ShapeDtypeStruct((1, HEAD_DIM), F32),
                   jax.ShapeDtypeStruct((1, HEAD_DIM), F32)],
        scratch_shapes=[pltpu.VMEM((1, SB_W), F32), pltpu.VMEM((1, SB_W), F32)],
        compiler_params=_cp("arbitrary"),
    )(proj, proj, dqs, dkn, dv, qg, kg)


def _bwd_sg(proj, dy, lg, lb, w, bias):
    s = proj.shape[0]
    tm = TOKEN_TILE
    nh = SG_W // HEAD_DIM

    def body(p_ref, dy_ref, lg_ref, lb_ref, w_ref, bias_ref, dp_ref, dlg_ref, dlb_ref, dw_ref, db_ref, dbias):
        i = pl.program_id(0)
        uv = p_ref[...]
        ge = _gelu(uv)
        u = ge[:, :SG_W]
        r, xh = _ln_stats(ge[:, SG_W:])
        vln = (xh * lg_ref[...] + lb_ref[...]).astype(BF16)
        dy = dy_ref[...]
        tril, lane_head = _sg_masks()

        @pl.when(i == 0)
        def _():
            dw_ref[...] = jnp.zeros_like(dw_ref)
            dbias[...] = jnp.zeros_like(dbias)

        dus, dvlns = [], []
        for c in range(tm // CHUNK):
            rows = slice(c * CHUNK, (c + 1) * CHUNK)
            vc = vln[rows]
            dus.append(dy[rows] * _sg_mix(w_ref, bias_ref, vc, tril, lane_head))
            dm = dy[rows] * u[rows]
            dbias[...] += dm
            dvc = jnp.zeros((CHUNK, SG_W), F32)
            for h in range(nh):
                dmh = jnp.where(lane_head == h, dm, 0.0).astype(BF16)
                dw_ref[h] += jnp.where(tril, _dot_nt(dmh, vc), 0.0)
                wm = jnp.where(tril, w_ref[h], 0.0).astype(BF16)
                dvc = dvc + _dot_tn(wm, dmh)
            dvlns.append(dvc)
        du = jnp.concatenate(dus, axis=0)
        dvln = jnp.concatenate(dvlns, axis=0)
        _accumulate(dlg_ref, _colsum(dvln * xh))
        _accumulate(dlb_ref, _colsum(dvln))
        dv = _ln_bwd(xh, r, lg_ref[...], dvln)
        dp_ref[...] = (jnp.concatenate([du, dv], axis=1) * _gelu_grad(uv)).astype(BF16)

        @pl.when(i == pl.num_programs(0) - 1)
        def _():
            lane = lax.broadcasted_iota(jnp.int32, (CHUNK, CHUNK), 1)
            acc = dbias[...]
            out = jnp.zeros((CHUNK, CHUNK), F32)
            for h in range(nh):
                hs = jnp.sum(acc[:, h * HEAD_DIM:(h + 1) * HEAD_DIM], axis=1, keepdims=True)
                out = out + jnp.where(lane == h, hs, 0.0)
            db_ref[...] = out

    return pl.pallas_call(
        body, name="bwd_sg", grid=(s // tm,),
        in_specs=[_rows(tm, 2 * SG_W, 1), _rows(tm, SG_W), _whole((1, SG_W)), _whole((1, SG_W)),
                  _whole((nh, CHUNK, CHUNK)), _whole((CHUNK, SG_W))],
        out_specs=[_rows(tm, 2 * SG_W), _whole((1, SG_W)), _whole((1, SG_W)), _whole((nh, CHUNK, CHUNK)),
                   _whole((CHUNK, CHUNK))],
        out_shape=[jax.ShapeDtypeStruct((s, 2 * SG_W), BF16), jax.ShapeDtypeStruct((1, SG_W), F32),
                   jax.ShapeDtypeStruct((1, SG_W), F32), jax.ShapeDtypeStruct((nh, CHUNK, CHUNK), F32),
                   jax.ShapeDtypeStruct((CHUNK, CHUNK), F32)],
        scratch_shapes=[pltpu.VMEM((CHUNK, SG_W), F32)],
        compiler_params=_cp("arbitrary"),
    )(proj, dy, lg, lb, w, bias)


def _bwd_conv(proj, dy, w, b, lg, lb):
    s = proj.shape[0]

    def body(p_ref, dy_ref, w_ref, b_ref, lg_ref, lb_ref, dp_ref, dw_ref, db_ref, dlg_ref, dlb_ref,
             abuf, dcbuf):
        _glu_fill(p_ref, abuf, s)
        dcbuf[pl.ds(s, CHUNK), :] = jnp.zeros((CHUNK, CONV_W), F32)
        dw_ref[...] = jnp.zeros_like(dw_ref)

        def chunk(c, carry):
            db, dlg, dlb = carry
            r0 = pl.multiple_of(c * CHUNK, CHUNK)
            acc, win = _conv_window(abuf, r0, w_ref)
            r, xh = _ln_stats(acc + b_ref[...])
            ln = xh * lg_ref[...] + lb_ref[...]
            sg = _sigmoid(ln)
            dl = dy_ref[pl.ds(r0, CHUNK), :] * (sg * (1.0 + ln * (1.0 - sg)))
            dc = _ln_bwd(xh, r, lg_ref[...], dl)
            dcbuf[pl.ds(r0, CHUNK), :] = dc
            for k in range(CONV_K):
                sh = win if k == 0 else pltpu.roll(win, k, axis=0)
                dw_ref[CONV_K - 1 - k:CONV_K - k, :] += _colsum(dc * sh[32:, :])
            return db + _colsum(dc), dlg + _colsum(dl * xh), dlb + _colsum(dl)

        zero = jnp.zeros((1, CONV_W), F32)
        db, dlg, dlb = lax.fori_loop(0, s // CHUNK, chunk, (zero, zero, zero))
        db_ref[...] = db
        dlg_ref[...] = dlg
        dlb_ref[...] = dlb

        def chunk_back(c, carry):
            r0 = pl.multiple_of(c * CHUNK, CHUNK)
            win = dcbuf[pl.ds(r0, CHUNK + 32), :]
            da = jnp.zeros((CHUNK, CONV_W), F32)
            for k in range(CONV_K):
                sh = win if k == 0 else pltpu.roll(win, CHUNK + 32 - k, axis=0)
                da = da + sh[:CHUNK, :] * w_ref[CONV_K - 1 - k:CONV_K - k, :]
            pv = p_ref[pl.ds(r0, CHUNK), :]
            val, sg = pv[:, :CONV_W], _sigmoid(pv[:, CONV_W:])
            dp_ref[pl.ds(r0, CHUNK), :] = jnp.concatenate([da * sg, da * val * sg * (1.0 - sg)], axis=1).astype(BF16)
            return carry

        lax.fori_loop(0, s // CHUNK, chunk_back, 0)

    row = _whole((1, CONV_W))
    return pl.pallas_call(
        body, name="bwd_conv", grid=(1,),
        in_specs=[pl.BlockSpec((s, 2 * CONV_W), lambda i: (0, 0)), _whole((s, CONV_W)),
                  _whole((CONV_K, CONV_W)), row, row, row],
        out_specs=[_whole((s, 2 * CONV_W)), _whole((CONV_K, CONV_W)), row, row, row],
        out_shape=[jax.ShapeDtypeStruct((s, 2 * CONV_W), BF16), jax.ShapeDtypeStruct((CONV_K, CONV_W), F32)]
        + [jax.ShapeDtypeStruct((1, CONV_W), F32)] * 3,
        scratch_shapes=[pltpu.VMEM((s + CHUNK, CONV_W), F32), pltpu.VMEM((s + CHUNK, CONV_W), F32)],
        compiler_params=_cp("arbitrary"),
    )(proj, dy, w, b, lg, lb)


def _bwd_in(dpc, dps, dpb, x, g, w, dxm):
    s = x.shape[0]
    tm = TOKEN_TILE

    def body(dpc_ref, dps_ref, dpb_ref, x_ref, g_ref, w_ref, dxm_ref, dx_ref, h_ref, dp_ref, dg_ref):
        dp = jnp.concatenate([dpc_ref[...], dps_ref[...], dpb_ref[...]], axis=1)
        dp_ref[...] = dp
        dh = _dot_nt(dp, w_ref[...])
        r, xh = _rms_stats(x_ref[...])
        h_ref[...] = (xh * g_ref[...]).astype(BF16)
        dx, dgrow = _rms_bwd(xh, r, g_ref[...], dh)
        dx_ref[...] = dxm_ref[...] + dx
        _accumulate(dg_ref, _colsum(dgrow))

    return pl.pallas_call(
        body, name="bwd_in", grid=(s // tm,),
        in_specs=[_rows(tm, 2 * CONV_W), _rows(tm, 2 * SG_W), _rows(tm, 3 * SB_W), _rows(tm, D_MODEL),
                  _whole((1, D_MODEL)), _whole((D_MODEL, IN_W)), _rows(tm, D_MODEL)],
        out_specs=[_rows(tm, D_MODEL), _rows(tm, D_MODEL), _rows(tm, IN_W), _whole((1, D_MODEL))],
        out_shape=[jax.ShapeDtypeStruct((s, D_MODEL), F32), jax.ShapeDtypeStruct((s, D_MODEL), BF16),
                   jax.ShapeDtypeStruct((s, IN_W), BF16), jax.ShapeDtypeStruct((1, D_MODEL), F32)],
        compiler_params=_cp("arbitrary"),
    )(dpc, dps, dpb, x, g, w, dxm)


SMALL = ("mix_norm_g", "conv_w", "conv_b", "conv_ln_g", "conv_ln_b", "sg_ln_g", "sg_ln_b", "sg_w", "sg_b",
         "q_norm_g", "k_norm_g", "out_norm_g", "ffn_norm_g")
LARGE = ("w_in", "w_out", "w_gate_up", "w_down")


def _row(v):
    return v.reshape(1, -1)


def _layer_params(p, large, l):
    q = {k: v[l] for k, v in p.items()}
    return dict(
        q, **large,
        mix_norm_g=_row(q["mix_norm_g"]), conv_b=_row(q["conv_b"]), conv_ln_g=_row(q["conv_ln_g"]),
        conv_ln_b=_row(q["conv_ln_b"]), sg_ln_g=_row(q["sg_ln_g"]), sg_ln_b=_row(q["sg_ln_b"]),
        out_norm_g=_row(q["out_norm_g"]), ffn_norm_g=_row(q["ffn_norm_g"]),
        qg=_row(jnp.tile(q["q_norm_g"], SB_W // HEAD_DIM)), kg=_row(jnp.tile(q["k_norm_g"], SB_W // HEAD_DIM)),
        sg_bias=jnp.repeat(q["sg_b"].T, HEAD_DIM, axis=1),
    )


def _layer_fwd(x, q, tok):
    proj, qn, kn, vb = _fwd_in(x, q["mix_norm_g"], q["w_in"], q["qg"], q["kg"], tok)
    yc = _fwd_conv(proj, q["conv_w"], q["conv_b"], q["conv_ln_g"], q["conv_ln_b"])
    ys = _fwd_sg(proj, q["sg_ln_g"], q["sg_ln_b"], q["sg_w"], q["sg_bias"])
    yb, lt = _fwd_sb(qn, kn, vb)
    xm = _fwd_out(yc, ys, yb, q["out_norm_g"], q["w_out"], x)
    gu, xo = _fwd_ffn(xm, q["ffn_norm_g"], q["w_gate_up"], q["w_down"])
    return xo, dict(x=x, proj=proj, qn=qn, kn=kn, vb=vb, lt=lt, yc=yc, ys=ys, yb=yb, xm=xm, gu=gu)


def _layer_bwd_ffn(dxo, q, st, tok):
    dgu, act, h2, dxm, d_ffn_g = _bwd_ffn(dxo, st["gu"], st["xm"], q["ffn_norm_g"], q["w_gate_up"], q["w_down"],
                                          tok)
    return dxm, d_ffn_g, _matmul_tn(h2, dgu, 512, 512), _matmul_tn(act, dxo, 256, 512)


def _layer_bwd_mix(dxm, d_ffn_g, q, st, tok):
    dyc, dys, dyb, cat, d_out_g = _bwd_out(dxm, st["yc"], st["ys"], st["yb"], q["out_norm_g"], q["w_out"], tok)
    d_wo = _matmul_tn(cat, dxm, 512, 512)
    dqs, dkn, dv = _bwd_sb(st["qn"], st["kn"], st["vb"], dyb, st["lt"])
    dpb, d_qg, d_kg = _bwd_qk(st["proj"], dqs, dkn, dv, q["qg"], q["kg"])
    dps, d_sg_lg, d_sg_lb, d_sg_w, d_sg_b = _bwd_sg(st["proj"], dys, q["sg_ln_g"], q["sg_ln_b"], q["sg_w"],
                                                    q["sg_bias"])
    dpc, d_conv_w, d_conv_b, d_conv_lg, d_conv_lb = _bwd_conv(st["proj"], dyc, q["conv_w"], q["conv_b"],
                                                              q["conv_ln_g"], q["conv_ln_b"])
    dx, h1, dp, d_mix_g = _bwd_in(dpc, dps, dpb, st["x"], q["mix_norm_g"], q["w_in"], dxm)
    d_win = _matmul_tn(h1, dp, 512, 512)
    small = dict(
        mix_norm_g=d_mix_g[0], conv_w=d_conv_w, conv_b=d_conv_b[0], conv_ln_g=d_conv_lg[0],
        conv_ln_b=d_conv_lb[0], sg_ln_g=d_sg_lg[0], sg_ln_b=d_sg_lb[0], sg_w=d_sg_w,
        sg_b=d_sg_b[:, :SG_W // HEAD_DIM].T, q_norm_g=d_qg[0], k_norm_g=d_kg[0], out_norm_g=d_out_g[0],
        ffn_norm_g=d_ffn_g[0])
    return dx, d_wo, d_win, small


def _position():
    x, y, c = lax.axis_index("x"), lax.axis_index("y"), lax.axis_index("c")
    return x, y, c


def _flat(px, py, pc):
    return 4 * px + 2 * py + pc


HBM = pl.BlockSpec(memory_space=pl.ANY)


def _all_gather(arrs, name, after):
    n = len(arrs)

    def body(*refs):
        ins, outs = refs[:n], refs[n + 1:2 * n + 1]
        send_sems, recv_sems, local_sems = refs[2 * n + 1:]
        x, y, c = _position()
        me, sibling = (x, y, c), (x, y, 1 - c)
        chips = [(1 - x, y), (x, 1 - y), (1 - x, 1 - y)]

        def copy(a, k, block, to, src=None):
            dst = outs[a].at[_flat(*block)]
            return pltpu.make_async_remote_copy(
                src_ref=dst if src is None else src, dst_ref=dst, send_sem=send_sems.at[a, k],
                recv_sem=recv_sems.at[a, k], device_id=to, device_id_type=MESH)

        mine, first, passed = [], [], []
        for a in range(n):
            cp = pltpu.make_async_copy(ins[a], outs[a].at[_flat(*me)], local_sems.at[a])
            cp.start()
            mine.append(cp)
            first.append(copy(a, 0, me, sibling, src=ins[a]))
            first += [copy(a, 1 + j, me, (*chip, c), src=ins[a]) for j, chip in enumerate(chips)]
        for cp in first:
            cp.start()
        for a in range(n):
            for j, chip in enumerate(chips):
                copy(a, 1 + j, (*chip, c), me).wait_recv()
                fwd = copy(a, 4 + j, (*chip, c), sibling)
                fwd.start()
                passed.append(fwd)
        for a in range(n):
            copy(a, 0, sibling, me).wait_recv()
            for j, chip in enumerate(chips):
                copy(a, 4 + j, (*chip, 1 - c), me).wait_recv()
        for cp in first + passed:
            cp.wait_send()
        for cp in mine:
            cp.wait()

    return pl.pallas_call(
        body, name=name,
        in_specs=[HBM] * n + [ORDER_ONLY], out_specs=[HBM] * n,
        out_shape=[jax.ShapeDtypeStruct((N_DEV,) + a.shape, a.dtype) for a in arrs],
        scratch_shapes=[pltpu.SemaphoreType.DMA((n, 7)), pltpu.SemaphoreType.DMA((n, 7)),
                        pltpu.SemaphoreType.DMA((n,))],
    )(*arrs, after)


IN_HBM = pl.BlockSpec(memory_space=pltpu.HBM)
IN_SEM = pl.BlockSpec(memory_space=pltpu.SEMAPHORE)
EFFECT = pltpu.SideEffectType.DATAFLOW_SIDE_EFFECTING
N_PEER = N_DEV - 1


def _exchange_copies(src_refs, land_refs, send_sems, recv_sems, layer, arrival):
    x, y, c = _position()
    me = _flat(x, y, c)
    out = []
    for a, (src, land) in enumerate(zip(src_refs, land_refs)):
        land = land if layer is None else land.at[layer]
        for k in range(N_PEER):
            peer = (x ^ ((k + 1) >> 2 & 1), y ^ ((k + 1) >> 1 & 1), c ^ ((k + 1) & 1))
            out.append(pltpu.make_async_remote_copy(
                src_ref=src if layer is None else src.at[_flat(*peer)],
                dst_ref=land.at[_flat(*peer) if arrival else me],
                send_sem=send_sems.at[a * N_PEER + k], recv_sem=recv_sems.at[a * N_PEER + k], device_id=peer,
                device_id_type=MESH))
    return out


def _exchange_start(srcs, lands, name, layer=None):
    n = len(srcs)

    def body(*refs):
        send_sems, recv_sems = refs[2 * n], refs[2 * n + 1]
        for cp in _exchange_copies(refs[:n], refs[n:2 * n], send_sems, recv_sems, layer, arrival=False):
            cp.start()
        refs[-1][...] = jnp.zeros_like(refs[-1])

    thru = [pltpu.HBM(a.shape, a.dtype) for a in (*srcs, *lands)]
    outs = pl.pallas_call(
        body, name=name,
        out_shape=(pltpu.SemaphoreType.DMA((n * N_PEER,)), pltpu.SemaphoreType.DMA((n * N_PEER,)), *thru,
                   jax.ShapeDtypeStruct((8, 128), F32)),
        in_specs=[IN_HBM] * (2 * n),
        out_specs=(IN_SEM, IN_SEM, *[IN_HBM] * (2 * n), pl.BlockSpec(memory_space=pltpu.VMEM)),
        input_output_aliases={i: 2 + i for i in range(2 * n)},
        compiler_params=pltpu.CompilerParams(has_side_effects=EFFECT),
    )(*[pltpu.with_memory_space_constraint(a, pltpu.HBM) for a in (*srcs, *lands)])
    return outs[0], outs[1], list(outs[2:2 + n]), list(outs[2 + n:2 + 2 * n]), outs[-1]


def _exchange_wait(pending, after, name, layer=None):
    send_sems, recv_sems, srcs, lands, _ = pending
    n = len(srcs)

    def body(*refs):
        for cp in _exchange_copies(refs[:n], refs[n:2 * n], refs[2 * n], refs[2 * n + 1], layer, arrival=True):
            cp.wait_send()
            cp.wait_recv()

    thru = [pltpu.HBM(a.shape, a.dtype) for a in (*srcs, *lands)]
    outs = pl.pallas_call(
        body, name=name, out_shape=tuple(thru),
        in_specs=[IN_HBM] * (2 * n) + [IN_SEM, IN_SEM, ORDER_ONLY],
        out_specs=tuple([IN_HBM] * (2 * n)),
        input_output_aliases={i: i for i in range(2 * n)},
        compiler_params=pltpu.CompilerParams(has_side_effects=EFFECT),
    )(*srcs, *lands, send_sems, recv_sems, after)
    return list(outs[n:])


def _landing(block, me):
    land = lax.empty((N_DEV,) + block.shape, block.dtype)
    return lax.dynamic_update_index_in_dim(land, block, me, 0)


def _adamw(parts, w, m, v, tr):
    groups, rows, cols = w.shape

    def body(p_ref, w_ref, m_ref, v_ref, g_ref, d_ref, nm_ref, nv_ref):
        g = p_ref[0].astype(F32)
        for j in range(1, N_DEV):
            g = g + p_ref[j].astype(F32)
        g_ref[...] = g
        m = ADAM_B1 * m_ref[...] + (1.0 - ADAM_B1) * g
        v = ADAM_B2 * v_ref[...] + (1.0 - ADAM_B2) * (g * g)
        nm_ref[...] = m
        nv_ref[...] = v
        m_hat = m / (1.0 - ADAM_B1 ** ADAM_STEP)
        v_hat = v / (1.0 - ADAM_B2 ** ADAM_STEP)
        d_ref[...] = -ADAM_LR * (m_hat / (jnp.sqrt(v_hat) + ADAM_EPS) + ADAM_WD * w_ref[...])

    blk = pl.BlockSpec((None, tr, cols), lambda g, i: (g, i, 0))
    return pl.pallas_call(
        body, name="adamw", grid=(groups, rows // tr),
        in_specs=[pl.BlockSpec((None, N_DEV, tr, cols), lambda g, i: (g, 0, i, 0)), blk, blk, blk],
        out_specs=[blk] * 4,
        out_shape=[jax.ShapeDtypeStruct((groups, rows, cols), F32)] * 4,
        compiler_params=_cp("parallel", "parallel"),
    )(parts, w, m, v)


def _row_tile(rows):
    for cand in (512, 256, 128, 64, 32, 16, 8):
        if rows % cand == 0 and rows > cand:
            return cand
    return rows


def _columns_from_blocks(g):
    n, r, c = g.shape
    return jnp.transpose(g, (1, 0, 2)).reshape(r, n * c)


def _blocks_from_columns(g):
    r, c = g.shape
    return jnp.transpose(g.reshape(r, N_DEV, c // N_DEV), (1, 0, 2))


def _with_own_block(land, blocks, layer, me):
    own = lax.dynamic_index_in_dim(blocks, me, 0, keepdims=True)[None]
    return lax.dynamic_update_slice(land, own, (layer, me, 0, 0))


PACK_LANES = 128


def _pack(arrs):
    parts = []
    for a in arrs:
        flat = a.reshape(-1)
        pad = -flat.size % (8 * PACK_LANES)
        parts.append(jnp.pad(flat, (0, pad)))
    return jnp.concatenate(parts).reshape(-1, PACK_LANES)


def _unpack(packed, shapes):
    flat = packed.reshape(-1)
    outs, off = [], 0
    for shp in shapes:
        size = 1
        for d in shp:
            size *= d
        outs.append(flat[off:off + size].reshape(shp))
        off += size + (-size % (8 * PACK_LANES))
    return outs


def kernel(x, mix_norm_g, w_in, conv_w, conv_b, conv_ln_g, conv_ln_b, sg_ln_g, sg_ln_b, sg_w, sg_b, q_norm_g, k_norm_g, out_norm_g, w_out, ffn_norm_g, w_gate_up, w_down, loss_target, m_mix_norm_g, m_w_in, m_conv_w, m_conv_b, m_conv_ln_g, m_conv_ln_b, m_sg_ln_g, m_sg_ln_b, m_sg_w, m_sg_b, m_q_norm_g, m_k_norm_g, m_out_norm_g, m_w_out, m_ffn_norm_g, m_w_gate_up, m_w_down, v_mix_norm_g, v_w_in, v_conv_w, v_conv_b, v_conv_ln_g, v_conv_ln_b, v_sg_ln_g, v_sg_ln_b, v_sg_w, v_sg_b, v_q_norm_g, v_k_norm_g, v_out_norm_g, v_w_out, v_ffn_norm_g, v_w_gate_up, v_w_down):
    names = SMALL[:1] + LARGE[:1] + SMALL[1:12] + LARGE[1:2] + SMALL[12:] + LARGE[2:]
    w = dict(mix_norm_g=mix_norm_g, w_in=w_in, conv_w=conv_w, conv_b=conv_b, conv_ln_g=conv_ln_g,
             conv_ln_b=conv_ln_b, sg_ln_g=sg_ln_g, sg_ln_b=sg_ln_b, sg_w=sg_w, sg_b=sg_b, q_norm_g=q_norm_g,
             k_norm_g=k_norm_g, out_norm_g=out_norm_g, w_out=w_out, ffn_norm_g=ffn_norm_g,
             w_gate_up=w_gate_up, w_down=w_down)
    m = dict(mix_norm_g=m_mix_norm_g, w_in=m_w_in, conv_w=m_conv_w, conv_b=m_conv_b, conv_ln_g=m_conv_ln_g,
             conv_ln_b=m_conv_ln_b, sg_ln_g=m_sg_ln_g, sg_ln_b=m_sg_ln_b, sg_w=m_sg_w, sg_b=m_sg_b,
             q_norm_g=m_q_norm_g, k_norm_g=m_k_norm_g, out_norm_g=m_out_norm_g, w_out=m_w_out,
             ffn_norm_g=m_ffn_norm_g, w_gate_up=m_w_gate_up, w_down=m_w_down)
    v = dict(mix_norm_g=v_mix_norm_g, w_in=v_w_in, conv_w=v_conv_w, conv_b=v_conv_b, conv_ln_g=v_conv_ln_g,
             conv_ln_b=v_conv_ln_b, sg_ln_g=v_sg_ln_g, sg_ln_b=v_sg_ln_b, sg_w=v_sg_w, sg_b=v_sg_b,
             q_norm_g=v_q_norm_g, k_norm_g=v_k_norm_g, out_norm_g=v_out_norm_g, w_out=v_w_out,
             ffn_norm_g=v_ffn_norm_g, w_gate_up=v_w_gate_up, w_down=v_w_down)
    xpos, ypos, cpos = _position()
    me = _flat(xpos, ypos, cpos)
    conv_cols = conv_w.shape[-1]
    no_token = jnp.zeros((8, 128), F32)
    shards = {k: w[k].astype(BF16) for k in LARGE}

    def gather_start(l):
        srcs = [shards[k][l] for k in LARGE] + ([w["conv_w"]] if l == 0 else [])
        return _exchange_start(srcs, [_landing(a, me) for a in srcs], f"gather_start_{l}")

    pending = gather_start(0)
    act = x[0]
    qs, stash = [], []
    for l in range(DEPTH):
        lands = _exchange_wait(pending, act, f"gather_wait_{l}")
        if l == 0:
            conv_full = jnp.transpose(lands[4], (1, 2, 0, 3)).reshape(DEPTH, CONV_K, CONV_W)
            small_w = dict({k: w[k] for k in SMALL}, conv_w=conv_full)
        large = dict(w_in=_columns_from_blocks(lands[0]), w_out=lands[1].reshape(D_MODEL, D_MODEL),
                     w_gate_up=_columns_from_blocks(lands[2]), w_down=lands[3].reshape(FFN_H, D_MODEL))
        token = no_token
        if l + 1 < DEPTH:
            pending = gather_start(l + 1)
            token = pending[4]
        qs.append(_layer_params(small_w, large, l))
        act, st = _layer_fwd(act, qs[l], token)
        stash.append(st)

    loss, dx = _loss_head(act, loss_target[0])
    loss = lax.psum(loss[0, 0], ("x", "y", "c"))

    land_a = [lax.empty((DEPTH, N_DEV) + w[k].shape[1:], BF16) for k in ("w_gate_up", "w_down")]
    land_b = [lax.empty((DEPTH, N_DEV) + w[k].shape[1:], BF16) for k in ("w_out", "w_in")]
    pend_a = pend_b = None
    token = no_token
    small_grads = [None] * DEPTH
    for l in reversed(range(DEPTH)):
        dxm, d_ffn_g, d_wgu, d_wd = _layer_bwd_ffn(dx, qs[l], stash[l], token)
        srcs = [_blocks_from_columns(d_wgu), d_wd.reshape(N_DEV, FFN_H // N_DEV, D_MODEL)]
        if pend_a is not None:
            land_a = _exchange_wait(pend_a, d_wd, f"grads_a_wait_{l + 1}", layer=l + 1)
        land_a = [_with_own_block(ld, a, l, me) for ld, a in zip(land_a, srcs)]
        pend_a = _exchange_start(srcs, land_a, f"grads_a_start_{l}", layer=l)
        dx, d_wo, d_win, small_grads[l] = _layer_bwd_mix(dxm, d_ffn_g, qs[l], stash[l], pend_a[4])
        srcs = [d_wo.reshape(N_DEV, D_MODEL // N_DEV, D_MODEL), _blocks_from_columns(d_win)]
        if pend_b is not None:
            land_b = _exchange_wait(pend_b, d_win, f"grads_b_wait_{l + 1}", layer=l + 1)
        land_b = [_with_own_block(ld, a, l, me) for ld, a in zip(land_b, srcs)]
        pend_b = _exchange_start(srcs, land_b, f"grads_b_start_{l}", layer=l)
        token = pend_b[4]

    stack = lambda k: jnp.stack([small_grads[l][k] for l in range(DEPTH)])
    small_shapes = [(DEPTH,) + small_grads[0][k].shape for k in SMALL]
    (small_parts,) = _all_gather([_pack([stack(k) for k in SMALL])], "gather_small_grads", token)
    per_dev = [_unpack(small_parts[j], small_shapes) for j in range(N_DEV)]
    small = [jnp.stack([per_dev[j][i] for j in range(N_DEV)]) for i in range(len(SMALL))]
    conv_i = SMALL.index("conv_w")
    small[conv_i] = lax.dynamic_slice_in_dim(small[conv_i], me * conv_cols, conv_cols, axis=3)
    res = {}
    shard_shapes = [w[k].shape for k in SMALL]
    packed_w, packed_m, packed_v = (_pack([t[k] for k in SMALL])[None] for t in (w, m, v))
    packed = _adamw(
        jnp.stack([_pack([small[i][j] for i in range(len(SMALL))]) for j in range(N_DEV)])[None],
        packed_w, packed_m, packed_v, packed_w.shape[1])
    unpacked = [_unpack(o[0], shard_shapes) for o in packed]
    for i, k in enumerate(SMALL):
        res[k] = [u[i] for u in unpacked]

    land_a = _exchange_wait(pend_a, packed[0], "grads_a_wait_0", layer=0)
    for k, parts in zip(("w_gate_up", "w_down"), land_a):
        res[k] = _adamw(parts, w[k], m[k], v[k], _row_tile(w[k].shape[1]))
    land_b = _exchange_wait(pend_b, res["w_down"][0], "grads_b_wait_0", layer=0)
    for k, parts in zip(("w_out", "w_in"), land_b):
        res[k] = _adamw(parts, w[k], m[k], v[k], _row_tile(w[k].shape[1]))

    return (loss, dx[None], *[res[k][0] for k in names], *[res[k][1] for k in names],
            *[res[k][2] for k in names], *[res[k][3] for k in names])
```

```python
import functools

import jax
import jax.numpy as jnp
from jax import lax
from jax.experimental import pallas as pl
from jax.experimental.pallas import tpu as pltpu

F32 = jnp.float32
BF16 = jnp.bfloat16

D_MODEL = 1024
DEPTH = 4
HEAD_DIM = 64
CONV_W = 256
SG_W = 256
SB_W = 512
IN_W = 2560
FFN_H = 2816
CONV_K = 31
CHUNK = 128
OFF_SG = 2 * CONV_W
OFF_SB = OFF_SG + 2 * SG_W
RMS_EPS = 1e-6
LN_EPS = 1e-5
N_DEV = 8
MESH = pl.DeviceIdType.MESH

ADAM_LR = 0.001
ADAM_B1 = 0.9
ADAM_B2 = 0.999
ADAM_EPS = 1e-08
ADAM_WD = 0.01
ADAM_STEP = 10

TOKEN_TILE = 256
VMEM_LIMIT = 56 * 1024 * 1024


def _cp(*sem):
    return pltpu.CompilerParams(dimension_semantics=sem or None, vmem_limit_bytes=VMEM_LIMIT)


def _dot(a, b):
    return jnp.dot(a, b, preferred_element_type=F32)


def _dot_nt(a, b):
    return lax.dot_general(a, b, (((1,), (1,)), ((), ())), preferred_element_type=F32)


def _dot_tn(a, b):
    return lax.dot_general(a, b, (((0,), (0,)), ((), ())), preferred_element_type=F32)


def _dot_split(x, m):
    hi = x.astype(BF16)
    lo = (x - hi.astype(F32)).astype(BF16)
    return _dot(hi, m) + _dot(lo, m)


def _group_mean_matrix(width, group):
    r = lax.broadcasted_iota(jnp.int32, (width, width), 0) // group
    c = lax.broadcasted_iota(jnp.int32, (width, width), 1) // group
    return jnp.where(r == c, 1.0 / group, 0.0).astype(BF16)


def _sigmoid(x):
    return 1.0 / (1.0 + jnp.exp(-x))


def _gelu(x):
    return 0.5 * x * (1.0 + lax.erf(x * (2.0 ** -0.5)))


def _gelu_grad(x):
    return 0.5 * (1.0 + lax.erf(x * (2.0 ** -0.5))) + x * jnp.exp(-0.5 * x * x) * (0.5 * (2.0 / jnp.pi) ** 0.5)


def _rms_stats(x):
    r = lax.rsqrt(jnp.mean(x * x, axis=-1, keepdims=True) + RMS_EPS)
    return r, x * r


def _rms_bwd(xh, r, g, dy):
    dxh = dy * g
    dx = r * (dxh - xh * jnp.mean(dxh * xh, axis=-1, keepdims=True))
    return dx, dy * xh


def _ln_stats(x):
    mu = jnp.mean(x, axis=-1, keepdims=True)
    xc = x - mu
    r = lax.rsqrt(jnp.mean(xc * xc, axis=-1, keepdims=True) + LN_EPS)
    return r, xc * r


def _ln_bwd(xh, r, g, dy):
    dxh = dy * g
    return r * (dxh - jnp.mean(dxh, axis=-1, keepdims=True) - xh * jnp.mean(dxh * xh, axis=-1, keepdims=True))


def _colsum(x):
    return jnp.sum(x, axis=0, keepdims=True)


def _rows(tm, n, j=0):
    return pl.BlockSpec((tm, n), lambda i: (i, j))


def _whole(shape):
    return pl.BlockSpec(shape, lambda i: (0,) * len(shape))


ORDER_ONLY = pl.BlockSpec(memory_space=pl.ANY)


def _stack_heads(a):
    even = (lax.broadcasted_iota(jnp.int32, a.shape, 1) % (2 * HEAD_DIM)) < HEAD_DIM
    top = jnp.where(even, a, 0.0).astype(BF16)
    bot = jnp.where(even, 0.0, a).astype(BF16)
    parts = []
    for c in range(a.shape[0] // CHUNK):
        rows = slice(c * CHUNK, (c + 1) * CHUNK)
        parts += [top[rows], bot[rows]]
    return jnp.concatenate(parts, axis=0)


def _unstack_heads(st):
    even = (lax.broadcasted_iota(jnp.int32, (CHUNK, st.shape[1]), 1) % (2 * HEAD_DIM)) < HEAD_DIM
    parts = []
    for c in range(st.shape[0] // (2 * CHUNK)):
        top = st[2 * c * CHUNK:(2 * c + 1) * CHUNK]
        bot = st[(2 * c + 1) * CHUNK:(2 * c + 2) * CHUNK]
        parts.append(jnp.where(even, top, bot))
    return jnp.concatenate(parts, axis=0)


def _fwd_in(x, g, w, qg, kg, tok):
    s = x.shape[0]
    tm = TOKEN_TILE

    def body(x_ref, g_ref, w_ref, qg_ref, kg_ref, tok_ref, proj_ref, qn_ref, kn_ref, vb_ref):
        r, xh = _rms_stats(x_ref[...])
        h = (xh * g_ref[...]).astype(BF16)
        proj = _dot_nt(h, w_ref[...])
        proj_ref[...] = proj
        gm = _group_mean_matrix(SB_W, HEAD_DIM)
        q = proj[:, OFF_SB:OFF_SB + SB_W]
        k = proj[:, OFF_SB + SB_W:OFF_SB + 2 * SB_W]
        rq = lax.rsqrt(_dot_split(q * q, gm) + RMS_EPS)
        rk = lax.rsqrt(_dot_split(k * k, gm) + RMS_EPS)
        qn_ref[...] = (q * rq * qg_ref[...] * (HEAD_DIM ** -0.5)).astype(BF16)
        kn_ref[...] = _stack_heads(k * rk * kg_ref[...])
        vb_ref[...] = _stack_heads(proj[:, OFF_SB + 2 * SB_W:])

    return pl.pallas_call(
        body, name="fwd_in", grid=(s // tm,),
        in_specs=[_rows(tm, D_MODEL), _whole((1, D_MODEL)), _whole((IN_W, D_MODEL)),
                  _whole((1, SB_W)), _whole((1, SB_W)), ORDER_ONLY],
        out_specs=[_rows(tm, IN_W), _rows(tm, SB_W), _rows(2 * tm, SB_W), _rows(2 * tm, SB_W)],
        out_shape=[jax.ShapeDtypeStruct((s, IN_W), F32), jax.ShapeDtypeStruct((s, SB_W), BF16),
                   jax.ShapeDtypeStruct((2 * s, SB_W), BF16), jax.ShapeDtypeStruct((2 * s, SB_W), BF16)],
        compiler_params=_cp("parallel"),
    )(x, g, w, qg, kg, tok)


def _conv_window(abuf, r0, w_ref):
    win = abuf[pl.ds(pl.multiple_of(r0 + CHUNK - 32, 32), CHUNK + 32), :]
    acc = jnp.zeros((CHUNK, CONV_W), F32)
    for k in range(CONV_K):
        sh = win if k == 0 else pltpu.roll(win, k, axis=0)
        acc = acc + sh[32:, :] * w_ref[CONV_K - 1 - k:CONV_K - k, :]
    return acc, win


def _glu_fill(p_ref, abuf, s):
    abuf[0:CHUNK, :] = jnp.zeros((CHUNK, CONV_W), F32)

    def fill(c, carry):
        r0 = pl.multiple_of(c * CHUNK, CHUNK)
        pv = p_ref[pl.ds(r0, CHUNK), :]
        abuf[pl.ds(r0 + CHUNK, CHUNK), :] = pv[:, :CONV_W] * _sigmoid(pv[:, CONV_W:])
        return carry

    lax.fori_loop(0, s // CHUNK, fill, 0)


def _fwd_conv(proj, w, b, lg, lb):
    s = proj.shape[0]

    def body(p_ref, w_ref, b_ref, lg_ref, lb_ref, y_ref, abuf):
        _glu_fill(p_ref, abuf, s)

        def chunk(c, carry):
            r0 = pl.multiple_of(c * CHUNK, CHUNK)
            acc, _ = _conv_window(abuf, r0, w_ref)
            r, xh = _ln_stats(acc + b_ref[...])
            ln = xh * lg_ref[...] + lb_ref[...]
            y_ref[pl.ds(r0, CHUNK), :] = ln * _sigmoid(ln)
            return carry

        lax.fori_loop(0, s // CHUNK, chunk, 0)

    return pl.pallas_call(
        body, name="fwd_conv", grid=(1,),
        in_specs=[pl.BlockSpec((s, 2 * CONV_W), lambda i: (0, 0)), _whole((CONV_K, CONV_W)),
                  _whole((1, CONV_W)), _whole((1, CONV_W)), _whole((1, CONV_W))],
        out_specs=_whole((s, CONV_W)),
        out_shape=jax.ShapeDtypeStruct((s, CONV_W), F32),
        scratch_shapes=[pltpu.VMEM((s + CHUNK, CONV_W), F32)],
        compiler_params=_cp("arbitrary"),
    )(proj, w, b, lg, lb)


def _sg_masks():
    row = lax.broadcasted_iota(jnp.int32, (CHUNK, CHUNK), 0)
    col = lax.broadcasted_iota(jnp.int32, (CHUNK, CHUNK), 1)
    lane_head = lax.broadcasted_iota(jnp.int32, (CHUNK, SG_W), 1) // HEAD_DIM
    return row >= col, lane_head


def _sg_mix(w_ref, bias_ref, vc, tril, lane_head):
    mixed = bias_ref[...]
    for h in range(SG_W // HEAD_DIM):
        wm = jnp.where(tril, w_ref[h], 0.0).astype(BF16)
        mixed = mixed + jnp.where(lane_head == h, _dot(wm, vc), 0.0)
    return mixed


def _fwd_sg(proj, lg, lb, w, bias):
    s = proj.shape[0]
    tm = TOKEN_TILE

    def body(p_ref, lg_ref, lb_ref, w_ref, bias_ref, y_ref):
        ge = _gelu(p_ref[...])
        u = ge[:, :SG_W]
        r, xh = _ln_stats(ge[:, SG_W:])
        vln = (xh * lg_ref[...] + lb_ref[...]).astype(BF16)
        tril, lane_head = _sg_masks()
        for c in range(tm // CHUNK):
            rows = slice(c * CHUNK, (c + 1) * CHUNK)
            y_ref[rows, :] = u[rows] * _sg_mix(w_ref, bias_ref, vln[rows], tril, lane_head)

    return pl.pallas_call(
        body, name="fwd_sg", grid=(s // tm,),
        in_specs=[_rows(tm, 2 * SG_W, 1), _whole((1, SG_W)), _whole((1, SG_W)),
                  _whole((SG_W // HEAD_DIM, CHUNK, CHUNK)), _whole((CHUNK, SG_W))],
        out_specs=_rows(tm, SG_W),
        out_shape=jax.ShapeDtypeStruct((s, SG_W), F32),
        compiler_params=_cp("parallel"),
    )(proj, lg, lb, w, bias)


SB_Q = 2 * CHUNK
PAIR = 2 * CHUNK


def _pair_tri(kind):
    row = lax.broadcasted_iota(jnp.int32, (PAIR, PAIR), 0)
    col = lax.broadcasted_iota(jnp.int32, (PAIR, PAIR), 1)
    tri = {"after": row > col, "upto": row <= col, "before": row < col}[kind]
    return jnp.where(((row // CHUNK) == (col // CHUNK)) & tri, 1.0, 0.0).astype(BF16)


def _sb_scores(z, qpos0, kpos0, masked):
    sp = jnp.maximum(z, 0.0) + jnp.log(1.0 + jnp.exp(-jnp.abs(z)))
    if not masked:
        return z, sp, -sp, None
    row = lax.broadcasted_iota(jnp.int32, z.shape, 0)
    col = lax.broadcasted_iota(jnp.int32, z.shape, 1) % CHUNK
    mask = (kpos0 + col) < (qpos0 + row)
    return z, sp, jnp.where(mask, -sp, 0.0), mask


def _per_head(c0, c1):
    return jnp.concatenate([jnp.broadcast_to(c0, (SB_Q, CHUNK)), jnp.broadcast_to(c1, (SB_Q, CHUNK))], axis=1)


def _fwd_sb(qn, kst, vst):
    s = qn.shape[0]

    def body(q_ref, k_ref, v_ref, o_ref, lt_ref, z_buf, att_buf):
        i = pl.program_id(1)
        last = 2 * i + 1
        q = q_ref[...]
        after = _pair_tri("after")

        def rows(kb):
            return pl.ds(pl.multiple_of(kb * PAIR, PAIR), PAIR)

        def block(kb, carry, masked):
            acc, c0, c1 = carry
            z_next = _dot_nt(q, k_ref[rows(jnp.maximum(kb - 1, 0)), :])
            pv = _dot(att_buf[...], v_ref[rows(jnp.minimum(kb + 1, last)), :])
            z, sp, lnb, mask = _sb_scores(z_buf[...], i * SB_Q, kb * CHUNK, masked)
            loc = _dot_split(lnb, after)
            att = jnp.exp(z - sp + loc + _per_head(c0, c1))
            if masked:
                att = jnp.where(mask, att, 0.0)
            z_buf[...] = z_next
            att_buf[...] = att.astype(BF16)
            return (acc + pv, c0 + loc[:, 0:1] + lnb[:, 0:1], c1 + loc[:, CHUNK:CHUNK + 1] + lnb[:, CHUNK:CHUNK + 1])

        z_buf[...] = _dot_nt(q, k_ref[rows(last), :])
        att_buf[...] = jnp.zeros_like(att_buf)
        zero = jnp.zeros((SB_Q, 1), F32)
        carry = (jnp.zeros((SB_Q, CHUNK), F32), zero, zero)
        carry = block(last, carry, True)
        carry = block(last - 1, carry, True)
        acc, c0, c1 = lax.fori_loop(0, 2 * i, lambda j, c: block(2 * i - 1 - j, c, False), carry)
        o_ref[...] = acc + _dot(att_buf[...], v_ref[rows(0), :])
        lt_ref[...] = jnp.concatenate([jnp.broadcast_to(c0, (SB_Q, HEAD_DIM)),
                                       jnp.broadcast_to(c1, (SB_Q, HEAD_DIM))], axis=1)

    blk = pl.BlockSpec((SB_Q, CHUNK), lambda p, i: (i, p))
    seq = pl.BlockSpec((2 * s, CHUNK), lambda p, i: (0, p))
    return pl.pallas_call(
        body, name="fwd_sb", grid=(SB_W // CHUNK, s // SB_Q),
        in_specs=[blk, seq, seq],
        out_specs=[blk, blk],
        out_shape=[jax.ShapeDtypeStruct((s, SB_W), F32)] * 2,
        scratch_shapes=[pltpu.VMEM((SB_Q, PAIR), F32), pltpu.VMEM((SB_Q, PAIR), BF16)],
        compiler_params=_cp("parallel", "parallel"),
    )(qn, kst, vst)


def _group_norms(yc, ys, yb):
    return [_rms_stats(yc), _rms_stats(ys), _rms_stats(yb)]


def _fwd_out(yc, ys, yb, g, w, x):
    s = x.shape[0]
    tm = TOKEN_TILE

    def body(yc_ref, ys_ref, yb_ref, g_ref, w_ref, x_ref, o_ref):
        stats = _group_norms(yc_ref[...], ys_ref[...], yb_ref[...])
        cat = jnp.concatenate([xh for _, xh in stats], axis=1) * g_ref[...]
        o_ref[...] = x_ref[...] + _dot(cat.astype(BF16), w_ref[...])

    return pl.pallas_call(
        body, name="fwd_out", grid=(s // tm,),
        in_specs=[_rows(tm, CONV_W), _rows(tm, SG_W), _rows(tm, SB_W), _whole((1, D_MODEL)),
                  _whole((D_MODEL, D_MODEL)), _rows(tm, D_MODEL)],
        out_specs=_rows(tm, D_MODEL),
        out_shape=jax.ShapeDtypeStruct((s, D_MODEL), F32),
        compiler_params=_cp("parallel"),
    )(yc, ys, yb, g, w, x)


def _fwd_ffn(x, g, wgu, wd):
    s = x.shape[0]
    tm = TOKEN_TILE

    def body(x_ref, g_ref, wgu_ref, wd_ref, gu_ref, o_ref):
        x = x_ref[...]
        r, xh = _rms_stats(x)
        gu = _dot_nt((xh * g_ref[...]).astype(BF16), wgu_ref[...])
        gu_ref[...] = gu
        gate = gu[:, :FFN_H]
        act = gate * _sigmoid(gate) * gu[:, FFN_H:]
        o_ref[...] = x + _dot(act.astype(BF16), wd_ref[...])

    return pl.pallas_call(
        body, name="fwd_ffn", grid=(s // tm,),
        in_specs=[_rows(tm, D_MODEL), _whole((1, D_MODEL)),
                  pl.BlockSpec((2 * FFN_H, D_MODEL), lambda i: (0, 0), pipeline_mode=pl.Buffered(1)),
                  pl.BlockSpec((FFN_H, D_MODEL), lambda i: (0, 0), pipeline_mode=pl.Buffered(1))],
        out_specs=[_rows(tm, 2 * FFN_H), _rows(tm, D_MODEL)],
        out_shape=[jax.ShapeDtypeStruct((s, 2 * FFN_H), F32), jax.ShapeDtypeStruct((s, D_MODEL), F32)],
        compiler_params=_cp("parallel"),
    )(x, g, wgu, wd)


def _loss_head(y, target):
    s = y.shape[0]
    tm = TOKEN_TILE

    def body(y_ref, t_ref, l_ref, d_ref):
        @pl.when(pl.program_id(0) == 0)
        def _():
            l_ref[...] = jnp.zeros_like(l_ref)

        err = y_ref[...] - t_ref[...]
        d_ref[...] = err * (1.0 / D_MODEL)
        l_ref[...] += 0.5 * jnp.sum(jnp.mean(err * err, axis=-1, keepdims=True), axis=0, keepdims=True)

    return pl.pallas_call(
        body, name="loss_head", grid=(s // tm,),
        in_specs=[_rows(tm, D_MODEL), _rows(tm, D_MODEL)],
        out_specs=[_whole((1, 1)), _rows(tm, D_MODEL)],
        out_shape=[jax.ShapeDtypeStruct((1, 1), F32), jax.ShapeDtypeStruct((s, D_MODEL), F32)],
        compiler_params=_cp("arbitrary"),
    )(y, target)


def _accumulate(ref, value):
    @pl.when(pl.program_id(0) == 0)
    def _():
        ref[...] = jnp.zeros_like(ref)

    ref[...] += value


def _bwd_ffn(dxo, gu, xm, g, wgu, wd, tok):
    s = dxo.shape[0]
    tm = TOKEN_TILE

    def body(dxo_ref, gu_ref, xm_ref, g_ref, wgu_ref, wd_ref, tok_ref, dgu_ref, act_ref, h_ref, dxm_ref, dg_ref):
        dxo = dxo_ref[...]
        gu = gu_ref[...]
        gate, up = gu[:, :FFN_H], gu[:, FFN_H:]
        sg = _sigmoid(gate)
        sl = gate * sg
        act_ref[...] = (sl * up).astype(BF16)
        dact = _dot_nt(dxo.astype(BF16), wd_ref[...])
        dgate = dact * up * (sg * (1.0 + gate * (1.0 - sg)))
        dgu = jnp.concatenate([dgate, dact * sl], axis=1).astype(BF16)
        dgu_ref[...] = dgu
        dh = _dot(dgu, wgu_ref[...])
        r, xh = _rms_stats(xm_ref[...])
        h_ref[...] = (xh * g_ref[...]).astype(BF16)
        dx, dgrow = _rms_bwd(xh, r, g_ref[...], dh)
        dxm_ref[...] = dxo + dx
        _accumulate(dg_ref, _colsum(dgrow))

    return pl.pallas_call(
        body, name="bwd_ffn", grid=(s // tm,),
        in_specs=[_rows(tm, D_MODEL), _rows(tm, 2 * FFN_H), _rows(tm, D_MODEL), _whole((1, D_MODEL)),
                  pl.BlockSpec((2 * FFN_H, D_MODEL), lambda i: (0, 0), pipeline_mode=pl.Buffered(1)),
                  pl.BlockSpec((FFN_H, D_MODEL), lambda i: (0, 0), pipeline_mode=pl.Buffered(1)), ORDER_ONLY],
        out_specs=[_rows(tm, 2 * FFN_H), _rows(tm, FFN_H), _rows(tm, D_MODEL), _rows(tm, D_MODEL),
                   _whole((1, D_MODEL))],
        out_shape=[jax.ShapeDtypeStruct((s, 2 * FFN_H), BF16), jax.ShapeDtypeStruct((s, FFN_H), BF16),
                   jax.ShapeDtypeStruct((s, D_MODEL), BF16), jax.ShapeDtypeStruct((s, D_MODEL), F32),
                   jax.ShapeDtypeStruct((1, D_MODEL), F32)],
        compiler_params=_cp("arbitrary"),
    )(dxo, gu, xm, g, wgu, wd, tok)


def _matmul_tn(a, b, tm, tn, out_dtype=BF16):
    s, m = a.shape
    n = b.shape[1]

    def body(a_ref, b_ref, o_ref):
        o_ref[...] = _dot_tn(a_ref[...].astype(BF16), b_ref[...].astype(BF16)).astype(out_dtype)

    return pl.pallas_call(
        body, name="weight_grad", grid=(m // tm, n // tn),
        in_specs=[pl.BlockSpec((s, tm), lambda i, j: (0, i)), pl.BlockSpec((s, tn), lambda i, j: (0, j))],
        out_specs=pl.BlockSpec((tm, tn), lambda i, j: (i, j)),
        out_shape=jax.ShapeDtypeStruct((m, n), out_dtype),
        compiler_params=_cp("parallel", "parallel"),
    )(a, b)


def _bwd_out(dxm, yc, ys, yb, g, w, tok):
    s = dxm.shape[0]
    tm = TOKEN_TILE

    def body(dxm_ref, yc_ref, ys_ref, yb_ref, g_ref, w_ref, tok_ref, dyc_ref, dys_ref, dyb_ref, cat_ref, dg_ref):
        stats = _group_norms(yc_ref[...], ys_ref[...], yb_ref[...])
        g = g_ref[...]
        cat_ref[...] = (jnp.concatenate([xh for _, xh in stats], axis=1) * g).astype(BF16)
        dcat = _dot_nt(dxm_ref[...].astype(BF16), w_ref[...])
        dgs = []
        off = 0
        for (r, xh), out in zip(stats, (dyc_ref, dys_ref, dyb_ref)):
            cols = slice(off, off + xh.shape[1])
            dx, dgrow = _rms_bwd(xh, r, g[:, cols], dcat[:, cols])
            out[...] = dx
            dgs.append(_colsum(dgrow))
            off += xh.shape[1]
        _accumulate(dg_ref, jnp.concatenate(dgs, axis=1))

    return pl.pallas_call(
        body, name="bwd_out", grid=(s // tm,),
        in_specs=[_rows(tm, D_MODEL), _rows(tm, CONV_W), _rows(tm, SG_W), _rows(tm, SB_W),
                  _whole((1, D_MODEL)), _whole((D_MODEL, D_MODEL)), ORDER_ONLY],
        out_specs=[_rows(tm, CONV_W), _rows(tm, SG_W), _rows(tm, SB_W), _rows(tm, D_MODEL),
                   _whole((1, D_MODEL))],
        out_shape=[jax.ShapeDtypeStruct((s, CONV_W), F32), jax.ShapeDtypeStruct((s, SG_W), F32),
                   jax.ShapeDtypeStruct((s, SB_W), F32), jax.ShapeDtypeStruct((s, D_MODEL), BF16),
                   jax.ShapeDtypeStruct((1, D_MODEL), F32)],
        compiler_params=_cp("arbitrary"),
    )(dxm, yc, ys, yb, g, w, tok)


def _bwd_sb(qn, kst, vst, dy, ltot):
    s = qn.shape[0]

    def body(q_ref, k_ref, v_ref, do_ref, lt_ref, dq_ref, dk_ref, dv_ref, z_buf, da_buf, dz_buf, att_buf):
        i = pl.program_id(1)
        last = 2 * i + 1

        @pl.when(i == 0)
        def _():
            dk_ref[...] = jnp.zeros_like(dk_ref)
            dv_ref[...] = jnp.zeros_like(dv_ref)

        q = q_ref[...]
        dob = do_ref[...].astype(BF16)
        lt = lt_ref[...]
        ltot = _per_head(lt[:, 0:1], lt[:, HEAD_DIM:HEAD_DIM + 1])
        upto = _pair_tri("upto")
        before = _pair_tri("before")
        last0, last1 = slice(CHUNK - 1, CHUNK), slice(PAIR - 1, PAIR)

        def rows(kb):
            return pl.ds(pl.multiple_of(kb * PAIR, PAIR), PAIR)

        def ahead(kb):
            return _dot_nt(q, k_ref[rows(kb), :]), _dot_nt(dob, v_ref[rows(kb), :])

        def behind(kb, dq):
            dzb = dz_buf[...]
            dk_ref[rows(kb), :] += _dot_tn(dzb, q)
            dv_ref[rows(kb), :] += _dot_tn(att_buf[...], dob)
            return dq + _dot(dzb, k_ref[rows(kb), :])

        def block(kb, carry, masked):
            dq, p0, p1, e0, e1 = carry
            z_next, da_next = ahead(jnp.minimum(kb + 1, last))
            dq = behind(jnp.maximum(kb - 1, 0), dq)
            z, sp, lnb, mask = _sb_scores(z_buf[...], i * SB_Q, kb * CHUNK, masked)
            pin = _dot_split(lnb, upto) + _per_head(p0, p1)
            sig = jnp.exp(z - sp)
            att = jnp.exp(z - sp + (ltot - pin))
            if masked:
                att = jnp.where(mask, att, 0.0)
            e = att * da_buf[...]
            ebefore = _dot_split(e, before) + _per_head(e0, e1)
            dz = e - sig * (e + ebefore)
            if masked:
                dz = jnp.where(mask, dz, 0.0)
            z_buf[...] = z_next
            da_buf[...] = da_next
            dz_buf[...] = dz.astype(BF16)
            att_buf[...] = att.astype(BF16)
            return (dq, pin[:, last0], pin[:, last1],
                    ebefore[:, last0] + e[:, last0], ebefore[:, last1] + e[:, last1])

        z_buf[...], da_buf[...] = ahead(0)
        dz_buf[...] = jnp.zeros_like(dz_buf)
        att_buf[...] = jnp.zeros_like(att_buf)
        zero = jnp.zeros((SB_Q, 1), F32)
        carry = (jnp.zeros((SB_Q, CHUNK), F32), zero, zero, zero, zero)
        carry = lax.fori_loop(0, 2 * i, lambda kb, c: block(kb, c, False), carry)
        carry = block(last - 1, carry, True)
        carry = block(last, carry, True)
        dq_ref[...] = behind(last, carry[0])

    blk = pl.BlockSpec((SB_Q, CHUNK), lambda p, i: (i, p))
    seq = pl.BlockSpec((2 * s, CHUNK), lambda p, i: (0, p))
    return pl.pallas_call(
        body, name="bwd_sb", grid=(SB_W // CHUNK, s // SB_Q),
        in_specs=[blk, seq, seq, blk, blk],
        out_specs=[blk, seq, seq],
        out_shape=[jax.ShapeDtypeStruct((s, SB_W), F32), jax.ShapeDtypeStruct((2 * s, SB_W), F32),
                   jax.ShapeDtypeStruct((2 * s, SB_W), F32)],
        scratch_shapes=[pltpu.VMEM((SB_Q, PAIR), F32), pltpu.VMEM((SB_Q, PAIR), F32),
                        pltpu.VMEM((SB_Q, PAIR), BF16), pltpu.VMEM((SB_Q, PAIR), BF16)],
        compiler_params=_cp("parallel", "arbitrary"),
    )(qn, kst, vst, dy, ltot)


def _head_sum(row):
    acc = row[:, 0:HEAD_DIM]
    for h in range(1, SB_W // HEAD_DIM):
        acc = acc + row[:, h * HEAD_DIM:(h + 1) * HEAD_DIM]
    return acc


def _bwd_qk(proj, dqs, dkn, dv, qg, kg):
    s = proj.shape[0]
    tm = TOKEN_TILE

    def body(q_ref, k_ref, dqs_ref, dkn_ref, dv_ref, qg_ref, kg_ref, dp_ref, dqg_ref, dkg_ref, qacc, kacc):
        i = pl.program_id(0)
        gm = _group_mean_matrix(SB_W, HEAD_DIM)

        def one(x, dy, g, acc):
            r = lax.rsqrt(_dot_split(x * x, gm) + RMS_EPS)
            xh = x * r
            dxh = dy * g
            _accumulate(acc, _colsum(dy * xh))
            return r * (dxh - xh * _dot_split(dxh * xh, gm))

        dq = one(q_ref[...], dqs_ref[...] * (HEAD_DIM ** -0.5), qg_ref[...], qacc)
        dk = one(k_ref[...], _unstack_heads(dkn_ref[...]), kg_ref[...], kacc)
        dp_ref[...] = jnp.concatenate([dq, dk, _unstack_heads(dv_ref[...])], axis=1).astype(BF16)

        @pl.when(i == pl.num_programs(0) - 1)
        def _():
            dqg_ref[...] = _head_sum(qacc[...])
            dkg_ref[...] = _head_sum(kacc[...])

    return pl.pallas_call(
        body, name="bwd_qk", grid=(s // tm,),
        in_specs=[_rows(tm, SB_W, OFF_SB // SB_W), _rows(tm, SB_W, OFF_SB // SB_W + 1),
                  _rows(tm, SB_W), _rows(2 * tm, SB_W), _rows(2 * tm, SB_W), _whole((1, SB_W)), _whole((1, SB_W))],
        out_specs=[_rows(tm, 3 * SB_W), _whole((1, HEAD_DIM)), _whole((1, HEAD_DIM))],
        out_shape=[jax.ShapeDtypeStruct((s, 3 * SB_W), BF16), jax.ShapeDtypeStruct((1, HEAD_DIM), F32),
                   jax.ShapeDtypeStruct((1, HEAD_DIM), F32)],
        scratch_shapes=[pltpu.VMEM((1, SB_W), F32), pltpu.VMEM((1, SB_W), F32)],
        compiler_params=_cp("arbitrary"),
    )(proj, proj, dqs, dkn, dv, qg, kg)


def _bwd_sg(proj, dy, lg, lb, w, bias):
    s = proj.shape[0]
    tm = TOKEN_TILE
    nh = SG_W // HEAD_DIM

    def body(p_ref, dy_ref, lg_ref, lb_ref, w_ref, bias_ref, dp_ref, dlg_ref, dlb_ref, dw_ref, db_ref, dbias):
        i = pl.program_id(0)
        uv = p_ref[...]
        ge = _gelu(uv)
        u = ge[:, :SG_W]
        r, xh = _ln_stats(ge[:, SG_W:])
        vln = (xh * lg_ref[...] + lb_ref[...]).astype(BF16)
        dy = dy_ref[...]
        tril, lane_head = _sg_masks()

        @pl.when(i == 0)
        def _():
            dw_ref[...] = jnp.zeros_like(dw_ref)
            dbias[...] = jnp.zeros_like(dbias)

        dus, dvlns = [], []
        for c in range(tm // CHUNK):
            rows = slice(c * CHUNK, (c + 1) * CHUNK)
            vc = vln[rows]
            dus.append(dy[rows] * _sg_mix(w_ref, bias_ref, vc, tril, lane_head))
            dm = dy[rows] * u[rows]
            dbias[...] += dm
            dvc = jnp.zeros((CHUNK, SG_W), F32)
            for h in range(nh):
                dmh = jnp.where(lane_head == h, dm, 0.0).astype(BF16)
                dw_ref[h] += jnp.where(tril, _dot_nt(dmh, vc), 0.0)
                wm = jnp.where(tril, w_ref[h], 0.0).astype(BF16)
                dvc = dvc + _dot_tn(wm, dmh)
            dvlns.append(dvc)
        du = jnp.concatenate(dus, axis=0)
        dvln = jnp.concatenate(dvlns, axis=0)
        _accumulate(dlg_ref, _colsum(dvln * xh))
        _accumulate(dlb_ref, _colsum(dvln))
        dv = _ln_bwd(xh, r, lg_ref[...], dvln)
        dp_ref[...] = (jnp.concatenate([du, dv], axis=1) * _gelu_grad(uv)).astype(BF16)

        @pl.when(i == pl.num_programs(0) - 1)
        def _():
            lane = lax.broadcasted_iota(jnp.int32, (CHUNK, CHUNK), 1)
            acc = dbias[...]
            out = jnp.zeros((CHUNK, CHUNK), F32)
            for h in range(nh):
                hs = jnp.sum(acc[:, h * HEAD_DIM:(h + 1) * HEAD_DIM], axis=1, keepdims=True)
                out = out + jnp.where(lane == h, hs, 0.0)
            db_ref[...] = out

    return pl.pallas_call(
        body, name="bwd_sg", grid=(s // tm,),
        in_specs=[_rows(tm, 2 * SG_W, 1), _rows(tm, SG_W), _whole((1, SG_W)), _whole((1, SG_W)),
                  _whole((nh, CHUNK, CHUNK)), _whole((CHUNK, SG_W))],
        out_specs=[_rows(tm, 2 * SG_W), _whole((1, SG_W)), _whole((1, SG_W)), _whole((nh, CHUNK, CHUNK)),
                   _whole((CHUNK, CHUNK))],
        out_shape=[jax.ShapeDtypeStruct((s, 2 * SG_W), BF16), jax.ShapeDtypeStruct((1, SG_W), F32),
                   jax.ShapeDtypeStruct((1, SG_W), F32), jax.ShapeDtypeStruct((nh, CHUNK, CHUNK), F32),
                   jax.ShapeDtypeStruct((CHUNK, CHUNK), F32)],
        scratch_shapes=[pltpu.VMEM((CHUNK, SG_W), F32)],
        compiler_params=_cp("arbitrary"),
    )(proj, dy, lg, lb, w, bias)


def _bwd_conv(proj, dy, w, b, lg, lb):
    s = proj.shape[0]

    def body(p_ref, dy_ref, w_ref, b_ref, lg_ref, lb_ref, dp_ref, dw_ref, db_ref, dlg_ref, dlb_ref,
             abuf, dcbuf):
        _glu_fill(p_ref, abuf, s)
        dcbuf[pl.ds(s, CHUNK), :] = jnp.zeros((CHUNK, CONV_W), F32)
        dw_ref[...] = jnp.zeros_like(dw_ref)

        def chunk(c, carry):
            db, dlg, dlb = carry
            r0 = pl.multiple_of(c * CHUNK, CHUNK)
            acc, win = _conv_window(abuf, r0, w_ref)
            r, xh = _ln_stats(acc + b_ref[...])
            ln = xh * lg_ref[...] + lb_ref[...]
            sg = _sigmoid(ln)
            dl = dy_ref[pl.ds(r0, CHUNK), :] * (sg * (1.0 + ln * (1.0 - sg)))
            dc = _ln_bwd(xh, r, lg_ref[...], dl)
            dcbuf[pl.ds(r0, CHUNK), :] = dc
            for k in range(CONV_K):
                sh = win if k == 0 else pltpu.roll(win, k, axis=0)
                dw_ref[CONV_K - 1 - k:CONV_K - k, :] += _colsum(dc * sh[32:, :])
            return db + _colsum(dc), dlg + _colsum(dl * xh), dlb + _colsum(dl)

        zero = jnp.zeros((1, CONV_W), F32)
        db, dlg, dlb = lax.fori_loop(0, s // CHUNK, chunk, (zero, zero, zero))
        db_ref[...] = db
        dlg_ref[...] = dlg
        dlb_ref[...] = dlb

        def chunk_back(c, carry):
            r0 = pl.multiple_of(c * CHUNK, CHUNK)
            win = dcbuf[pl.ds(r0, CHUNK + 32), :]
            da = jnp.zeros((CHUNK, CONV_W), F32)
            for k in range(CONV_K):
                sh = win if k == 0 else pltpu.roll(win, CHUNK + 32 - k, axis=0)
                da = da + sh[:CHUNK, :] * w_ref[CONV_K - 1 - k:CONV_K - k, :]
            pv = p_ref[pl.ds(r0, CHUNK), :]
            val, sg = pv[:, :CONV_W], _sigmoid(pv[:, CONV_W:])
            dp_ref[pl.ds(r0, CHUNK), :] = jnp.concatenate([da * sg, da * val * sg * (1.0 - sg)], axis=1).astype(BF16)
            return carry

        lax.fori_loop(0, s // CHUNK, chunk_back, 0)

    row = _whole((1, CONV_W))
    return pl.pallas_call(
        body, name="bwd_conv", grid=(1,),
        in_specs=[pl.BlockSpec((s, 2 * CONV_W), lambda i: (0, 0)), _whole((s, CONV_W)),
                  _whole((CONV_K, CONV_W)), row, row, row],
        out_specs=[_whole((s, 2 * CONV_W)), _whole((CONV_K, CONV_W)), row, row, row],
        out_shape=[jax.ShapeDtypeStruct((s, 2 * CONV_W), BF16), jax.ShapeDtypeStruct((CONV_K, CONV_W), F32)]
        + [jax.ShapeDtypeStruct((1, CONV_W), F32)] * 3,
        scratch_shapes=[pltpu.VMEM((s + CHUNK, CONV_W), F32), pltpu.VMEM((s + CHUNK, CONV_W), F32)],
        compiler_params=_cp("arbitrary"),
    )(proj, dy, w, b, lg, lb)


def _bwd_in(dpc, dps, dpb, x, g, w, dxm):
    s = x.shape[0]
    tm = TOKEN_TILE

    def body(dpc_ref, dps_ref, dpb_ref, x_ref, g_ref, w_ref, dxm_ref, dx_ref, h_ref, dp_ref, dg_ref):
        dp = jnp.concatenate([dpc_ref[...], dps_ref[...], dpb_ref[...]], axis=1)
        dp_ref[...] = dp
        dh = _dot(dp, w_ref[...])
        r, xh = _rms_stats(x_ref[...])
        h_ref[...] = (xh * g_ref[...]).astype(BF16)
        dx, dgrow = _rms_bwd(xh, r, g_ref[...], dh)
        dx_ref[...] = dxm_ref[...] + dx
        _accumulate(dg_ref, _colsum(dgrow))

    return pl.pallas_call(
        body, name="bwd_in", grid=(s // tm,),
        in_specs=[_rows(tm, 2 * CONV_W), _rows(tm, 2 * SG_W), _rows(tm, 3 * SB_W), _rows(tm, D_MODEL),
                  _whole((1, D_MODEL)), _whole((IN_W, D_MODEL)), _rows(tm, D_MODEL)],
        out_specs=[_rows(tm, D_MODEL), _rows(tm, D_MODEL), _rows(tm, IN_W), _whole((1, D_MODEL))],
        out_shape=[jax.ShapeDtypeStruct((s, D_MODEL), F32), jax.ShapeDtypeStruct((s, D_MODEL), BF16),
                   jax.ShapeDtypeStruct((s, IN_W), BF16), jax.ShapeDtypeStruct((1, D_MODEL), F32)],
        compiler_params=_cp("arbitrary"),
    )(dpc, dps, dpb, x, g, w, dxm)


SMALL = ("mix_norm_g", "conv_w", "conv_b", "conv_ln_g", "conv_ln_b", "sg_ln_g", "sg_ln_b", "sg_w", "sg_b",
         "q_norm_g", "k_norm_g", "out_norm_g", "ffn_norm_g")
LARGE = ("w_in", "w_out", "w_gate_up", "w_down")


def _row(v):
    return v.reshape(1, -1)


def _layer_params(p, large, l):
    q = {k: v[l] for k, v in p.items()}
    return dict(
        q, **large,
        mix_norm_g=_row(q["mix_norm_g"]), conv_b=_row(q["conv_b"]), conv_ln_g=_row(q["conv_ln_g"]),
        conv_ln_b=_row(q["conv_ln_b"]), sg_ln_g=_row(q["sg_ln_g"]), sg_ln_b=_row(q["sg_ln_b"]),
        out_norm_g=_row(q["out_norm_g"]), ffn_norm_g=_row(q["ffn_norm_g"]),
        qg=_row(jnp.tile(q["q_norm_g"], SB_W // HEAD_DIM)), kg=_row(jnp.tile(q["k_norm_g"], SB_W // HEAD_DIM)),
        sg_bias=jnp.repeat(q["sg_b"].T, HEAD_DIM, axis=1),
    )


def _layer_fwd(x, q, tok, rest):
    proj, qn, kn, vb = _fwd_in(x, q["mix_norm_g"], q["w_in"], q["qg"], q["kg"], tok)
    yc = _fwd_conv(proj, q["conv_w"], q["conv_b"], q["conv_ln_g"], q["conv_ln_b"])
    ys = _fwd_sg(proj, q["sg_ln_g"], q["sg_ln_b"], q["sg_w"], q["sg_bias"])
    yb, lt = _fwd_sb(qn, kn, vb)
    q = dict(q, **rest(yb))
    xm = _fwd_out(yc, ys, yb, q["out_norm_g"], q["w_out"], x)
    gu, xo = _fwd_ffn(xm, q["ffn_norm_g"], q["w_gate_up"], q["w_down"])
    return xo, q, dict(x=x, proj=proj, qn=qn, kn=kn, vb=vb, lt=lt, yc=yc, ys=ys, yb=yb, xm=xm, gu=gu)


def _layer_bwd_ffn(dxo, q, st, tok):
    dgu, act, h2, dxm, d_ffn_g = _bwd_ffn(dxo, st["gu"], st["xm"], q["ffn_norm_g"], q["w_gate_up"], q["w_down"],
                                          tok)
    return dxm, d_ffn_g, _matmul_tn(dgu, h2, 512, D_MODEL), _matmul_tn(act, dxo, FFN_H // 2, D_MODEL)


def _layer_bwd_mix(dxm, d_ffn_g, q, st, tok):
    dyc, dys, dyb, cat, d_out_g = _bwd_out(dxm, st["yc"], st["ys"], st["yb"], q["out_norm_g"], q["w_out"], tok)
    d_wo = _matmul_tn(cat, dxm, 512, D_MODEL)
    dqs, dkn, dv = _bwd_sb(st["qn"], st["kn"], st["vb"], dyb, st["lt"])
    dpb, d_qg, d_kg = _bwd_qk(st["proj"], dqs, dkn, dv, q["qg"], q["kg"])
    dps, d_sg_lg, d_sg_lb, d_sg_w, d_sg_b = _bwd_sg(st["proj"], dys, q["sg_ln_g"], q["sg_ln_b"], q["sg_w"],
                                                    q["sg_bias"])
    dpc, d_conv_w, d_conv_b, d_conv_lg, d_conv_lb = _bwd_conv(st["proj"], dyc, q["conv_w"], q["conv_b"],
                                                              q["conv_ln_g"], q["conv_ln_b"])
    dx, h1, dp, d_mix_g = _bwd_in(dpc, dps, dpb, st["x"], q["mix_norm_g"], q["w_in"], dxm)
    d_win = _matmul_tn(dp, h1, 512, D_MODEL)
    small = dict(
        mix_norm_g=d_mix_g[0], conv_w=d_conv_w, conv_b=d_conv_b[0], conv_ln_g=d_conv_lg[0],
        conv_ln_b=d_conv_lb[0], sg_ln_g=d_sg_lg[0], sg_ln_b=d_sg_lb[0], sg_w=d_sg_w,
        sg_b=d_sg_b[:, :SG_W // HEAD_DIM].T, q_norm_g=d_qg[0], k_norm_g=d_kg[0], out_norm_g=d_out_g[0],
        ffn_norm_g=d_ffn_g[0])
    return dx, d_wo, d_win, small


def _position():
    x, y, c = lax.axis_index("x"), lax.axis_index("y"), lax.axis_index("c")
    return x, y, c


def _flat(px, py, pc):
    return 4 * px + 2 * py + pc


HBM = pl.BlockSpec(memory_space=pl.ANY)


def _all_gather(arrs, name, after):
    n = len(arrs)

    def body(*refs):
        ins, outs = refs[:n], refs[n + 1:2 * n + 1]
        send_sems, recv_sems, local_sems = refs[2 * n + 1:]
        x, y, c = _position()
        me, sibling = (x, y, c), (x, y, 1 - c)
        chips = [(1 - x, y), (x, 1 - y), (1 - x, 1 - y)]

        def copy(a, k, block, to, src=None):
            dst = outs[a].at[_flat(*block)]
            return pltpu.make_async_remote_copy(
                src_ref=dst if src is None else src, dst_ref=dst, send_sem=send_sems.at[a, k],
                recv_sem=recv_sems.at[a, k], device_id=to, device_id_type=MESH)

        mine, first, passed = [], [], []
        for a in range(n):
            cp = pltpu.make_async_copy(ins[a], outs[a].at[_flat(*me)], local_sems.at[a])
            cp.start()
            mine.append(cp)
            first.append(copy(a, 0, me, sibling, src=ins[a]))
            first += [copy(a, 1 + j, me, (*chip, c), src=ins[a]) for j, chip in enumerate(chips)]
        for cp in first:
            cp.start()
        for a in range(n):
            for j, chip in enumerate(chips):
                copy(a, 1 + j, (*chip, c), me).wait_recv()
                fwd = copy(a, 4 + j, (*chip, c), sibling)
                fwd.start()
                passed.append(fwd)
        for a in range(n):
            copy(a, 0, sibling, me).wait_recv()
            for j, chip in enumerate(chips):
                copy(a, 4 + j, (*chip, 1 - c), me).wait_recv()
        for cp in first + passed:
            cp.wait_send()
        for cp in mine:
            cp.wait()

    return pl.pallas_call(
        body, name=name,
        in_specs=[HBM] * n + [ORDER_ONLY], out_specs=[HBM] * n,
        out_shape=[jax.ShapeDtypeStruct((N_DEV,) + a.shape, a.dtype) for a in arrs],
        scratch_shapes=[pltpu.SemaphoreType.DMA((n, 7)), pltpu.SemaphoreType.DMA((n, 7)),
                        pltpu.SemaphoreType.DMA((n,))],
    )(*arrs, after)


IN_HBM = pl.BlockSpec(memory_space=pltpu.HBM)
IN_SEM = pl.BlockSpec(memory_space=pltpu.SEMAPHORE)
EFFECT = pltpu.SideEffectType.DATAFLOW_SIDE_EFFECTING
N_PEER = N_DEV - 1


def _exchange_copies(src_refs, land_refs, send_sems, recv_sems, layer, arrival):
    x, y, c = _position()
    me = _flat(x, y, c)
    out = []
    for a, (src, land) in enumerate(zip(src_refs, land_refs)):
        land = land if layer is None else land.at[layer]
        for k in range(N_PEER):
            peer = (x ^ ((k + 1) >> 2 & 1), y ^ ((k + 1) >> 1 & 1), c ^ ((k + 1) & 1))
            out.append(pltpu.make_async_remote_copy(
                src_ref=src if layer is None else src.at[_flat(*peer)],
                dst_ref=land.at[_flat(*peer) if arrival else me],
                send_sem=send_sems.at[a * N_PEER + k], recv_sem=recv_sems.at[a * N_PEER + k], device_id=peer,
                device_id_type=MESH))
    return out


def _exchange_start(srcs, lands, name, layer=None):
    n = len(srcs)

    def body(*refs):
        send_sems, recv_sems = refs[2 * n], refs[2 * n + 1]
        for cp in _exchange_copies(refs[:n], refs[n:2 * n], send_sems, recv_sems, layer, arrival=False):
            cp.start()
        refs[-1][...] = jnp.zeros_like(refs[-1])

    thru = [pltpu.HBM(a.shape, a.dtype) for a in (*srcs, *lands)]
    outs = pl.pallas_call(
        body, name=name,
        out_shape=(pltpu.SemaphoreType.DMA((n * N_PEER,)), pltpu.SemaphoreType.DMA((n * N_PEER,)), *thru,
                   jax.ShapeDtypeStruct((8, 128), F32)),
        in_specs=[IN_HBM] * (2 * n),
        out_specs=(IN_SEM, IN_SEM, *[IN_HBM] * (2 * n), pl.BlockSpec(memory_space=pltpu.VMEM)),
        input_output_aliases={i: 2 + i for i in range(2 * n)},
        compiler_params=pltpu.CompilerParams(has_side_effects=EFFECT),
    )(*[pltpu.with_memory_space_constraint(a, pltpu.HBM) for a in (*srcs, *lands)])
    return outs[0], outs[1], list(outs[2:2 + n]), list(outs[2 + n:2 + 2 * n]), outs[-1]


def _exchange_wait(pending, after, name, layer=None):
    send_sems, recv_sems, srcs, lands, _ = pending
    n = len(srcs)

    def body(*refs):
        for cp in _exchange_copies(refs[:n], refs[n:2 * n], refs[2 * n], refs[2 * n + 1], layer, arrival=True):
            cp.wait_send()
            cp.wait_recv()

    thru = [pltpu.HBM(a.shape, a.dtype) for a in (*srcs, *lands)]
    outs = pl.pallas_call(
        body, name=name, out_shape=tuple(thru),
        in_specs=[IN_HBM] * (2 * n) + [IN_SEM, IN_SEM, ORDER_ONLY],
        out_specs=tuple([IN_HBM] * (2 * n)),
        input_output_aliases={i: i for i in range(2 * n)},
        compiler_params=pltpu.CompilerParams(has_side_effects=EFFECT),
    )(*srcs, *lands, send_sems, recv_sems, after)
    return list(outs[n:])


def _landing(block, me):
    land = lax.empty((N_DEV,) + block.shape, block.dtype)
    return lax.dynamic_update_index_in_dim(land, block, me, 0)


def _adamw(parts, w, m, v, tr):
    groups, rows, cols = w.shape

    def body(p_ref, w_ref, m_ref, v_ref, g_ref, d_ref, nm_ref, nv_ref):
        g = p_ref[0].astype(F32)
        for j in range(1, N_DEV):
            g = g + p_ref[j].astype(F32)
        g_ref[...] = g
        m = ADAM_B1 * m_ref[...] + (1.0 - ADAM_B1) * g
        v = ADAM_B2 * v_ref[...] + (1.0 - ADAM_B2) * (g * g)
        nm_ref[...] = m
        nv_ref[...] = v
        m_hat = m / (1.0 - ADAM_B1 ** ADAM_STEP)
        v_hat = v / (1.0 - ADAM_B2 ** ADAM_STEP)
        d_ref[...] = -ADAM_LR * (m_hat / (jnp.sqrt(v_hat) + ADAM_EPS) + ADAM_WD * w_ref[...])

    blk = pl.BlockSpec((None, tr, cols), lambda g, i: (g, i, 0))
    return pl.pallas_call(
        body, name="adamw", grid=(groups, rows // tr),
        in_specs=[pl.BlockSpec((None, N_DEV, tr, cols), lambda g, i: (g, 0, i, 0)), blk, blk, blk],
        out_specs=[blk] * 4,
        out_shape=[jax.ShapeDtypeStruct((groups, rows, cols), F32)] * 4,
        compiler_params=_cp("parallel", "parallel"),
    )(parts, w, m, v)


def _row_tile(rows):
    for cand in (512, 256, 128, 64, 32, 16, 8):
        if rows % cand == 0 and rows > cand:
            return cand
    return rows


def _with_own_block(land, blocks, layer, me):
    own = lax.dynamic_index_in_dim(blocks, me, 0, keepdims=True)[None]
    return lax.dynamic_update_slice(land, own, (layer, me, 0, 0))


PACK_LANES = 128


def _pack(arrs):
    parts = []
    for a in arrs:
        flat = a.reshape(-1)
        pad = -flat.size % (8 * PACK_LANES)
        parts.append(jnp.pad(flat, (0, pad)))
    return jnp.concatenate(parts).reshape(-1, PACK_LANES)


def _unpack(packed, shapes):
    flat = packed.reshape(-1)
    outs, off = [], 0
    for shp in shapes:
        size = 1
        for d in shp:
            size *= d
        outs.append(flat[off:off + size].reshape(shp))
        off += size + (-size % (8 * PACK_LANES))
    return outs


def kernel(x, mix_norm_g, w_in, conv_w, conv_b, conv_ln_g, conv_ln_b, sg_ln_g, sg_ln_b, sg_w, sg_b, q_norm_g, k_norm_g, out_norm_g, w_out, ffn_norm_g, w_gate_up, w_down, loss_target, m_mix_norm_g, m_w_in, m_conv_w, m_conv_b, m_conv_ln_g, m_conv_ln_b, m_sg_ln_g, m_sg_ln_b, m_sg_w, m_sg_b, m_q_norm_g, m_k_norm_g, m_out_norm_g, m_w_out, m_ffn_norm_g, m_w_gate_up, m_w_down, v_mix_norm_g, v_w_in, v_conv_w, v_conv_b, v_conv_ln_g, v_conv_ln_b, v_sg_ln_g, v_sg_ln_b, v_sg_w, v_sg_b, v_q_norm_g, v_k_norm_g, v_out_norm_g, v_w_out, v_ffn_norm_g, v_w_gate_up, v_w_down):
    names = SMALL[:1] + LARGE[:1] + SMALL[1:12] + LARGE[1:2] + SMALL[12:] + LARGE[2:]
    w = dict(mix_norm_g=mix_norm_g, w_in=w_in, conv_w=conv_w, conv_b=conv_b, conv_ln_g=conv_ln_g,
             conv_ln_b=conv_ln_b, sg_ln_g=sg_ln_g, sg_ln_b=sg_ln_b, sg_w=sg_w, sg_b=sg_b, q_norm_g=q_norm_g,
             k_norm_g=k_norm_g, out_norm_g=out_norm_g, w_out=w_out, ffn_norm_g=ffn_norm_g,
             w_gate_up=w_gate_up, w_down=w_down)
    m = dict(mix_norm_g=m_mix_norm_g, w_in=m_w_in, conv_w=m_conv_w, conv_b=m_conv_b, conv_ln_g=m_conv_ln_g,
             conv_ln_b=m_conv_ln_b, sg_ln_g=m_sg_ln_g, sg_ln_b=m_sg_ln_b, sg_w=m_sg_w, sg_b=m_sg_b,
             q_norm_g=m_q_norm_g, k_norm_g=m_k_norm_g, out_norm_g=m_out_norm_g, w_out=m_w_out,
             ffn_norm_g=m_ffn_norm_g, w_gate_up=m_w_gate_up, w_down=m_w_down)
    v = dict(mix_norm_g=v_mix_norm_g, w_in=v_w_in, conv_w=v_conv_w, conv_b=v_conv_b, conv_ln_g=v_conv_ln_g,
             conv_ln_b=v_conv_ln_b, sg_ln_g=v_sg_ln_g, sg_ln_b=v_sg_ln_b, sg_w=v_sg_w, sg_b=v_sg_b,
             q_norm_g=v_q_norm_g, k_norm_g=v_k_norm_g, out_norm_g=v_out_norm_g, w_out=v_w_out,
             ffn_norm_g=v_ffn_norm_g, w_gate_up=v_w_gate_up, w_down=v_w_down)
    xpos, ypos, cpos = _position()
    me = _flat(xpos, ypos, cpos)
    conv_cols = conv_w.shape[-1]
    no_token = jnp.zeros((8, 128), F32)
    w, m, v = (dict(t, w_in=jnp.swapaxes(t["w_in"], 1, 2), w_gate_up=jnp.swapaxes(t["w_gate_up"], 1, 2))
               for t in (w, m, v))
    shards = {k: w[k].astype(BF16) for k in LARGE}
    full_shape = dict(w_in=(IN_W, D_MODEL), w_out=(D_MODEL, D_MODEL), w_gate_up=(2 * FFN_H, D_MODEL),
                      w_down=(FFN_H, D_MODEL))

    def gather_start(l, keys, name):
        srcs = [shards[k][l] for k in keys] + ([w["conv_w"]] if l == 0 and "w_in" in keys else [])
        return _exchange_start(srcs, [_landing(a, me) for a in srcs], name)

    def gathered(pending, keys, after, name):
        lands = _exchange_wait(pending, after, name)
        return {k: a.reshape(full_shape[k]) for k, a in zip(keys, lands)}, lands[len(keys):]

    first, later = ("w_in",), ("w_out", "w_gate_up", "w_down")
    pending = gather_start(0, first, "gather_start_0"), gather_start(0, later, "gather_start_0_later")
    act = x[0]
    qs, stash = [], []
    for l in range(DEPTH):
        if l == 0:
            large, (conv_blocks,) = gathered(pending[0], first, act, "gather_wait_0")
            conv_full = jnp.transpose(conv_blocks, (1, 2, 0, 3)).reshape(DEPTH, CONV_K, CONV_W)
            small_w = dict({k: w[k] for k in SMALL}, conv_w=conv_full)
            held = pending[1]
            rest = lambda after: gathered(held, later, after, "gather_wait_0_later")[0]
        else:
            large, _ = gathered(pending, LARGE, act, f"gather_wait_{l}")
            rest = lambda after: {}
        token = no_token
        if l + 1 < DEPTH:
            pending = gather_start(l + 1, LARGE, f"gather_start_{l + 1}")
            token = pending[4]
        act, q, st = _layer_fwd(act, _layer_params(small_w, large, l), token, rest)
        qs.append(q)
        stash.append(st)

    loss, dx = _loss_head(act, loss_target[0])
    loss = lax.psum(loss[0, 0], ("x", "y", "c"))

    land_a = [lax.empty((DEPTH, N_DEV) + w[k].shape[1:], BF16) for k in ("w_gate_up", "w_down")]
    land_b = [lax.empty((DEPTH, N_DEV) + w[k].shape[1:], BF16) for k in ("w_out", "w_in")]
    pend_a = pend_b = None
    token = no_token
    small_grads = [None] * DEPTH
    for l in reversed(range(DEPTH)):
        dxm, d_ffn_g, d_wgu, d_wd = _layer_bwd_ffn(dx, qs[l], stash[l], token)
        srcs = [d_wgu.reshape((N_DEV,) + w["w_gate_up"].shape[1:]), d_wd.reshape((N_DEV,) + w["w_down"].shape[1:])]
        if pend_a is not None:
            land_a = _exchange_wait(pend_a, d_wd, f"grads_a_wait_{l + 1}", layer=l + 1)
        land_a = [_with_own_block(ld, a, l, me) for ld, a in zip(land_a, srcs)]
        pend_a = _exchange_start(srcs, land_a, f"grads_a_start_{l}", layer=l)
        dx, d_wo, d_win, small_grads[l] = _layer_bwd_mix(dxm, d_ffn_g, qs[l], stash[l], pend_a[4])
        srcs = [d_wo.reshape((N_DEV,) + w["w_out"].shape[1:]), d_win.reshape((N_DEV,) + w["w_in"].shape[1:])]
        if pend_b is not None:
            land_b = _exchange_wait(pend_b, d_win, f"grads_b_wait_{l + 1}", layer=l + 1)
        land_b = [_with_own_block(ld, a, l, me) for ld, a in zip(land_b, srcs)]
        pend_b = _exchange_start(srcs, land_b, f"grads_b_start_{l}", layer=l)
        token = pend_b[4]

    stack = lambda k: jnp.stack([small_grads[l][k] for l in range(DEPTH)])
    replicated = tuple(k for k in SMALL if k != "conv_w")
    (small_parts,) = _all_gather([_pack([stack(k) for k in replicated + ("conv_w",)])], "gather_small_grads",
                                 token)
    packed_w, packed_m, packed_v = (_pack([t[k] for k in replicated])[None] for t in (w, m, v))
    rows = packed_w.shape[1]
    packed = _adamw(small_parts[None], packed_w, packed_m, packed_v, rows)
    unpacked = [_unpack(o[0], [w[k].shape for k in replicated]) for o in packed]
    res = {k: [u[i] for u in unpacked] for i, k in enumerate(replicated)}
    conv_parts = small_parts[:, rows:].reshape(N_DEV, -1)[:, :DEPTH * CONV_K * CONV_W]
    conv_parts = lax.dynamic_slice_in_dim(conv_parts.reshape(N_DEV, DEPTH * CONV_K, CONV_W), me * conv_cols,
                                          conv_cols, axis=2)
    res["conv_w"] = [a.reshape(w["conv_w"].shape) for a in _adamw(
        conv_parts[None], *(t["conv_w"].reshape(1, DEPTH * CONV_K, conv_cols) for t in (w, m, v)),
        DEPTH * CONV_K)]

    land_a = _exchange_wait(pend_a, packed[0], "grads_a_wait_0", layer=0)
    for k, parts in zip(("w_gate_up", "w_down"), land_a):
        res[k] = _adamw(parts, w[k], m[k], v[k], _row_tile(w[k].shape[1]))
    land_b = _exchange_wait(pend_b, res["w_down"][0], "grads_b_wait_0", layer=0)
    for k, parts in zip(("w_out", "w_in"), land_b):
        res[k] = _adamw(parts, w[k], m[k], v[k], _row_tile(w[k].shape[1]))
    for k in ("w_in", "w_gate_up"):
        res[k] = [jnp.swapaxes(a, 1, 2) for a in res[k]]

    return (loss, dx[None], *[res[k][0] for k in names], *[res[k][1] for k in names],
            *[res[k][2] for k in names], *[res[k][3] for k in names])
```

```python
import functools

import jax
import jax.numpy as jnp
from jax import lax
from jax.experimental import pallas as pl
from jax.experimental.pallas import tpu as pltpu

F32 = jnp.float32
BF16 = jnp.bfloat16

D_MODEL = 1024
DEPTH = 4
HEAD_DIM = 64
CONV_W = 256
SG_W = 256
SB_W = 512
IN_W = 2560
FFN_H = 2816
CONV_K = 31
CHUNK = 128
OFF_SG = 2 * CONV_W
OFF_SB = OFF_SG + 2 * SG_W
RMS_EPS = 1e-6
LN_EPS = 1e-5
N_DEV = 8
MESH = pl.DeviceIdType.MESH

ADAM_LR = 0.001
ADAM_B1 = 0.9
ADAM_B2 = 0.999
ADAM_EPS = 1e-08
ADAM_WD = 0.01
ADAM_STEP = 10

TOKEN_TILE = 256
VMEM_LIMIT = 56 * 1024 * 1024


def _cp(*sem):
    return pltpu.CompilerParams(dimension_semantics=sem or None, vmem_limit_bytes=VMEM_LIMIT)


def _dot(a, b):
    return jnp.dot(a, b, preferred_element_type=F32)


def _dot_nt(a, b):
    return lax.dot_general(a, b, (((1,), (1,)), ((), ())), preferred_element_type=F32)


def _dot_tn(a, b):
    return lax.dot_general(a, b, (((0,), (0,)), ((), ())), preferred_element_type=F32)


def _dot_split(x, m):
    hi = x.astype(BF16)
    lo = (x - hi.astype(F32)).astype(BF16)
    return _dot(hi, m) + _dot(lo, m)


def _group_mean_matrix(width, group):
    r = lax.broadcasted_iota(jnp.int32, (width, width), 0) // group
    c = lax.broadcasted_iota(jnp.int32, (width, width), 1) // group
    return jnp.where(r == c, 1.0 / group, 0.0).astype(BF16)


def _sigmoid(x):
    return 1.0 / (1.0 + jnp.exp(-x))


def _gelu(x):
    return 0.5 * x * (1.0 + lax.erf(x * (2.0 ** -0.5)))


def _gelu_grad(x):
    return 0.5 * (1.0 + lax.erf(x * (2.0 ** -0.5))) + x * jnp.exp(-0.5 * x * x) * (0.5 * (2.0 / jnp.pi) ** 0.5)


def _rms_stats(x):
    r = lax.rsqrt(jnp.mean(x * x, axis=-1, keepdims=True) + RMS_EPS)
    return r, x * r


def _rms_bwd(xh, r, g, dy):
    dxh = dy * g
    dx = r * (dxh - xh * jnp.mean(dxh * xh, axis=-1, keepdims=True))
    return dx, dy * xh


def _ln_stats(x):
    mu = jnp.mean(x, axis=-1, keepdims=True)
    xc = x - mu
    r = lax.rsqrt(jnp.mean(xc * xc, axis=-1, keepdims=True) + LN_EPS)
    return r, xc * r


def _ln_bwd(xh, r, g, dy):
    dxh = dy * g
    return r * (dxh - jnp.mean(dxh, axis=-1, keepdims=True) - xh * jnp.mean(dxh * xh, axis=-1, keepdims=True))


def _colsum(x):
    return jnp.sum(x, axis=0, keepdims=True)


def _rows(tm, n, j=0):
    return pl.BlockSpec((tm, n), lambda i: (i, j))


def _whole(shape):
    return pl.BlockSpec(shape, lambda i: (0,) * len(shape))


ORDER_ONLY = pl.BlockSpec(memory_space=pl.ANY)


def _stack_heads(a):
    even = (lax.broadcasted_iota(jnp.int32, a.shape, 1) % (2 * HEAD_DIM)) < HEAD_DIM
    top = jnp.where(even, a, 0.0).astype(BF16)
    bot = jnp.where(even, 0.0, a).astype(BF16)
    parts = []
    for c in range(a.shape[0] // CHUNK):
        rows = slice(c * CHUNK, (c + 1) * CHUNK)
        parts += [top[rows], bot[rows]]
    return jnp.concatenate(parts, axis=0)


def _unstack_heads(st):
    even = (lax.broadcasted_iota(jnp.int32, (CHUNK, st.shape[1]), 1) % (2 * HEAD_DIM)) < HEAD_DIM
    parts = []
    for c in range(st.shape[0] // (2 * CHUNK)):
        top = st[2 * c * CHUNK:(2 * c + 1) * CHUNK]
        bot = st[(2 * c + 1) * CHUNK:(2 * c + 2) * CHUNK]
        parts.append(jnp.where(even, top, bot))
    return jnp.concatenate(parts, axis=0)


def _fwd_in(x, g, w, qg, kg, tok):
    s = x.shape[0]
    tm = TOKEN_TILE

    def body(x_ref, g_ref, w_ref, qg_ref, kg_ref, tok_ref, proj_ref, qn_ref, kn_ref, vb_ref):
        r, xh = _rms_stats(x_ref[...])
        h = (xh * g_ref[...]).astype(BF16)
        proj = _dot_nt(h, w_ref[...])
        proj_ref[...] = proj
        gm = _group_mean_matrix(SB_W, HEAD_DIM)
        q = proj[:, OFF_SB:OFF_SB + SB_W]
        k = proj[:, OFF_SB + SB_W:OFF_SB + 2 * SB_W]
        rq = lax.rsqrt(_dot_split(q * q, gm) + RMS_EPS)
        rk = lax.rsqrt(_dot_split(k * k, gm) + RMS_EPS)
        qn_ref[...] = (q * rq * qg_ref[...] * (HEAD_DIM ** -0.5)).astype(BF16)
        kn_ref[...] = _stack_heads(k * rk * kg_ref[...])
        vb_ref[...] = _stack_heads(proj[:, OFF_SB + 2 * SB_W:])

    return pl.pallas_call(
        body, name="fwd_in", grid=(s // tm,),
        in_specs=[_rows(tm, D_MODEL), _whole((1, D_MODEL)), _whole((IN_W, D_MODEL)),
                  _whole((1, SB_W)), _whole((1, SB_W)), ORDER_ONLY],
        out_specs=[_rows(tm, IN_W), _rows(tm, SB_W), _rows(2 * tm, SB_W), _rows(2 * tm, SB_W)],
        out_shape=[jax.ShapeDtypeStruct((s, IN_W), F32), jax.ShapeDtypeStruct((s, SB_W), BF16),
                   jax.ShapeDtypeStruct((2 * s, SB_W), BF16), jax.ShapeDtypeStruct((2 * s, SB_W), BF16)],
        compiler_params=_cp("parallel"),
    )(x, g, w, qg, kg, tok)


def _conv_window(abuf, r0, w_ref):
    win = abuf[pl.ds(pl.multiple_of(r0 + CHUNK - 32, 32), CHUNK + 32), :]
    acc = jnp.zeros((CHUNK, CONV_W), F32)
    for k in range(CONV_K):
        sh = win if k == 0 else pltpu.roll(win, k, axis=0)
        acc = acc + sh[32:, :] * w_ref[CONV_K - 1 - k:CONV_K - k, :]
    return acc, win


def _glu_fill(p_ref, abuf, s):
    abuf[0:CHUNK, :] = jnp.zeros((CHUNK, CONV_W), F32)

    def fill(c, carry):
        r0 = pl.multiple_of(c * CHUNK, CHUNK)
        pv = p_ref[pl.ds(r0, CHUNK), :]
        abuf[pl.ds(r0 + CHUNK, CHUNK), :] = pv[:, :CONV_W] * _sigmoid(pv[:, CONV_W:])
        return carry

    lax.fori_loop(0, s // CHUNK, fill, 0)


def _fwd_conv(proj, w, b, lg, lb, tok):
    s = proj.shape[0]

    def body(p_ref, w_ref, b_ref, lg_ref, lb_ref, tok_ref, y_ref, abuf):
        _glu_fill(p_ref, abuf, s)

        def chunk(c, carry):
            r0 = pl.multiple_of(c * CHUNK, CHUNK)
            acc, _ = _conv_window(abuf, r0, w_ref)
            r, xh = _ln_stats(acc + b_ref[...])
            ln = xh * lg_ref[...] + lb_ref[...]
            y_ref[pl.ds(r0, CHUNK), :] = ln * _sigmoid(ln)
            return carry

        lax.fori_loop(0, s // CHUNK, chunk, 0)

    return pl.pallas_call(
        body, name="fwd_conv", grid=(1,),
        in_specs=[pl.BlockSpec((s, 2 * CONV_W), lambda i: (0, 0)), _whole((CONV_K, CONV_W)),
                  _whole((1, CONV_W)), _whole((1, CONV_W)), _whole((1, CONV_W)), ORDER_ONLY],
        out_specs=_whole((s, CONV_W)),
        out_shape=jax.ShapeDtypeStruct((s, CONV_W), F32),
        scratch_shapes=[pltpu.VMEM((s + CHUNK, CONV_W), F32)],
        compiler_params=_cp("arbitrary"),
    )(proj, w, b, lg, lb, tok)


def _sg_masks():
    row = lax.broadcasted_iota(jnp.int32, (CHUNK, CHUNK), 0)
    col = lax.broadcasted_iota(jnp.int32, (CHUNK, CHUNK), 1)
    lane_head = lax.broadcasted_iota(jnp.int32, (CHUNK, SG_W), 1) // HEAD_DIM
    return row >= col, lane_head


def _sg_mix(w_ref, bias_ref, vc, tril, lane_head):
    mixed = bias_ref[...]
    for h in range(SG_W // HEAD_DIM):
        wm = jnp.where(tril, w_ref[h], 0.0).astype(BF16)
        mixed = mixed + jnp.where(lane_head == h, _dot(wm, vc), 0.0)
    return mixed


def _fwd_sg(proj, lg, lb, w, bias):
    s = proj.shape[0]
    tm = TOKEN_TILE

    def body(p_ref, lg_ref, lb_ref, w_ref, bias_ref, y_ref):
        ge = _gelu(p_ref[...])
        u = ge[:, :SG_W]
        r, xh = _ln_stats(ge[:, SG_W:])
        vln = (xh * lg_ref[...] + lb_ref[...]).astype(BF16)
        tril, lane_head = _sg_masks()
        for c in range(tm // CHUNK):
            rows = slice(c * CHUNK, (c + 1) * CHUNK)
            y_ref[rows, :] = u[rows] * _sg_mix(w_ref, bias_ref, vln[rows], tril, lane_head)

    return pl.pallas_call(
        body, name="fwd_sg", grid=(s // tm,),
        in_specs=[_rows(tm, 2 * SG_W, 1), _whole((1, SG_W)), _whole((1, SG_W)),
                  _whole((SG_W // HEAD_DIM, CHUNK, CHUNK)), _whole((CHUNK, SG_W))],
        out_specs=_rows(tm, SG_W),
        out_shape=jax.ShapeDtypeStruct((s, SG_W), F32),
        compiler_params=_cp("parallel"),
    )(proj, lg, lb, w, bias)


SB_Q = 2 * CHUNK
PAIR = 2 * CHUNK


def _pair_tri(kind):
    row = lax.broadcasted_iota(jnp.int32, (PAIR, PAIR), 0)
    col = lax.broadcasted_iota(jnp.int32, (PAIR, PAIR), 1)
    tri = {"after": row > col, "upto": row <= col, "before": row < col}[kind]
    return jnp.where(((row // CHUNK) == (col // CHUNK)) & tri, 1.0, 0.0).astype(BF16)


def _sb_scores(z, qpos0, kpos0, masked):
    sp = jnp.maximum(z, 0.0) + jnp.log(1.0 + jnp.exp(-jnp.abs(z)))
    if not masked:
        return z, sp, -sp, None
    row = lax.broadcasted_iota(jnp.int32, z.shape, 0)
    col = lax.broadcasted_iota(jnp.int32, z.shape, 1) % CHUNK
    mask = (kpos0 + col) < (qpos0 + row)
    return z, sp, jnp.where(mask, -sp, 0.0), mask


def _per_head(c0, c1):
    return jnp.concatenate([jnp.broadcast_to(c0, (SB_Q, CHUNK)), jnp.broadcast_to(c1, (SB_Q, CHUNK))], axis=1)


def _fwd_sb(qn, kst, vst):
    s = qn.shape[0]

    def body(q_ref, k_ref, v_ref, o_ref, lt_ref, z_buf, att_buf):
        i = pl.program_id(1)
        last = 2 * i + 1
        q = q_ref[...]
        after = _pair_tri("after")

        def rows(kb):
            return pl.ds(pl.multiple_of(kb * PAIR, PAIR), PAIR)

        def block(kb, carry, masked):
            acc, c0, c1 = carry
            z_next = _dot_nt(q, k_ref[rows(jnp.maximum(kb - 1, 0)), :])
            pv = _dot(att_buf[...], v_ref[rows(jnp.minimum(kb + 1, last)), :])
            z, sp, lnb, mask = _sb_scores(z_buf[...], i * SB_Q, kb * CHUNK, masked)
            loc = _dot_split(lnb, after)
            att = jnp.exp(z - sp + loc + _per_head(c0, c1))
            if masked:
                att = jnp.where(mask, att, 0.0)
            z_buf[...] = z_next
            att_buf[...] = att.astype(BF16)
            return (acc + pv, c0 + loc[:, 0:1] + lnb[:, 0:1], c1 + loc[:, CHUNK:CHUNK + 1] + lnb[:, CHUNK:CHUNK + 1])

        z_buf[...] = _dot_nt(q, k_ref[rows(last), :])
        att_buf[...] = jnp.zeros_like(att_buf)
        zero = jnp.zeros((SB_Q, 1), F32)
        carry = (jnp.zeros((SB_Q, CHUNK), F32), zero, zero)
        carry = block(last, carry, True)
        carry = block(last - 1, carry, True)
        acc, c0, c1 = lax.fori_loop(0, 2 * i, lambda j, c: block(2 * i - 1 - j, c, False), carry)
        o_ref[...] = acc + _dot(att_buf[...], v_ref[rows(0), :])
        lt_ref[...] = jnp.concatenate([jnp.broadcast_to(c0, (SB_Q, HEAD_DIM)),
                                       jnp.broadcast_to(c1, (SB_Q, HEAD_DIM))], axis=1)

    blk = pl.BlockSpec((SB_Q, CHUNK), lambda p, i: (i, p))
    seq = pl.BlockSpec((2 * s, CHUNK), lambda p, i: (0, p))
    return pl.pallas_call(
        body, name="fwd_sb", grid=(SB_W // CHUNK, s // SB_Q),
        in_specs=[blk, seq, seq],
        out_specs=[blk, blk],
        out_shape=[jax.ShapeDtypeStruct((s, SB_W), F32)] * 2,
        scratch_shapes=[pltpu.VMEM((SB_Q, PAIR), F32), pltpu.VMEM((SB_Q, PAIR), BF16)],
        compiler_params=_cp("parallel", "parallel"),
    )(qn, kst, vst)


def _group_norms(yc, ys, yb):
    return [_rms_stats(yc), _rms_stats(ys), _rms_stats(yb)]


def _fwd_out(yc, ys, yb, g, w, x, tok):
    s = x.shape[0]
    tm = TOKEN_TILE

    def body(yc_ref, ys_ref, yb_ref, g_ref, w_ref, x_ref, tok_ref, o_ref):
        stats = _group_norms(yc_ref[...], ys_ref[...], yb_ref[...])
        cat = jnp.concatenate([xh for _, xh in stats], axis=1) * g_ref[...]
        o_ref[...] = x_ref[...] + _dot(cat.astype(BF16), w_ref[...])

    return pl.pallas_call(
        body, name="fwd_out", grid=(s // tm,),
        in_specs=[_rows(tm, CONV_W), _rows(tm, SG_W), _rows(tm, SB_W), _whole((1, D_MODEL)),
                  _whole((D_MODEL, D_MODEL)), _rows(tm, D_MODEL), ORDER_ONLY],
        out_specs=_rows(tm, D_MODEL),
        out_shape=jax.ShapeDtypeStruct((s, D_MODEL), F32),
        compiler_params=_cp("parallel"),
    )(yc, ys, yb, g, w, x, tok)


def _fwd_ffn(x, g, wgu, wd):
    s = x.shape[0]
    tm = TOKEN_TILE

    def body(x_ref, g_ref, wgu_ref, wd_ref, gu_ref, o_ref):
        x = x_ref[...]
        r, xh = _rms_stats(x)
        gu = _dot_nt((xh * g_ref[...]).astype(BF16), wgu_ref[...])
        gu_ref[...] = gu
        gate = gu[:, :FFN_H]
        act = gate * _sigmoid(gate) * gu[:, FFN_H:]
        o_ref[...] = x + _dot(act.astype(BF16), wd_ref[...])

    return pl.pallas_call(
        body, name="fwd_ffn", grid=(s // tm,),
        in_specs=[_rows(tm, D_MODEL), _whole((1, D_MODEL)),
                  pl.BlockSpec((2 * FFN_H, D_MODEL), lambda i: (0, 0), pipeline_mode=pl.Buffered(1)),
                  pl.BlockSpec((FFN_H, D_MODEL), lambda i: (0, 0), pipeline_mode=pl.Buffered(1))],
        out_specs=[_rows(tm, 2 * FFN_H), _rows(tm, D_MODEL)],
        out_shape=[jax.ShapeDtypeStruct((s, 2 * FFN_H), F32), jax.ShapeDtypeStruct((s, D_MODEL), F32)],
        compiler_params=_cp("parallel"),
    )(x, g, wgu, wd)


def _loss_head(y, target):
    s = y.shape[0]
    tm = TOKEN_TILE

    def body(y_ref, t_ref, l_ref, d_ref):
        @pl.when(pl.program_id(0) == 0)
        def _():
            l_ref[...] = jnp.zeros_like(l_ref)

        err = y_ref[...] - t_ref[...]
        d_ref[...] = err * (1.0 / D_MODEL)
        l_ref[...] += 0.5 * jnp.sum(jnp.mean(err * err, axis=-1, keepdims=True), axis=0, keepdims=True)

    return pl.pallas_call(
        body, name="loss_head", grid=(s // tm,),
        in_specs=[_rows(tm, D_MODEL), _rows(tm, D_MODEL)],
        out_specs=[_whole((1, 1)), _rows(tm, D_MODEL)],
        out_shape=[jax.ShapeDtypeStruct((1, 1), F32), jax.ShapeDtypeStruct((s, D_MODEL), F32)],
        compiler_params=_cp("arbitrary"),
    )(y, target)


def _accumulate(ref, value):
    @pl.when(pl.program_id(0) == 0)
    def _():
        ref[...] = jnp.zeros_like(ref)

    ref[...] += value


def _bwd_ffn(dxo, gu, xm, g, wgu, wd, tok):
    s = dxo.shape[0]
    tm = TOKEN_TILE

    def body(dxo_ref, gu_ref, xm_ref, g_ref, wgu_ref, wd_ref, tok_ref, dgu_ref, act_ref, h_ref, dxm_ref, dg_ref):
        dxo = dxo_ref[...]
        gu = gu_ref[...]
        gate, up = gu[:, :FFN_H], gu[:, FFN_H:]
        sg = _sigmoid(gate)
        sl = gate * sg
        act_ref[...] = (sl * up).astype(BF16)
        dact = _dot_nt(dxo.astype(BF16), wd_ref[...])
        dgate = dact * up * (sg * (1.0 + gate * (1.0 - sg)))
        dgu = jnp.concatenate([dgate, dact * sl], axis=1).astype(BF16)
        dgu_ref[...] = dgu
        dh = _dot(dgu, wgu_ref[...])
        r, xh = _rms_stats(xm_ref[...])
        h_ref[...] = (xh * g_ref[...]).astype(BF16)
        dx, dgrow = _rms_bwd(xh, r, g_ref[...], dh)
        dxm_ref[...] = dxo + dx
        _accumulate(dg_ref, _colsum(dgrow))

    return pl.pallas_call(
        body, name="bwd_ffn", grid=(s // tm,),
        in_specs=[_rows(tm, D_MODEL), _rows(tm, 2 * FFN_H), _rows(tm, D_MODEL), _whole((1, D_MODEL)),
                  pl.BlockSpec((2 * FFN_H, D_MODEL), lambda i: (0, 0), pipeline_mode=pl.Buffered(1)),
                  pl.BlockSpec((FFN_H, D_MODEL), lambda i: (0, 0), pipeline_mode=pl.Buffered(1)), ORDER_ONLY],
        out_specs=[_rows(tm, 2 * FFN_H), _rows(tm, FFN_H), _rows(tm, D_MODEL), _rows(tm, D_MODEL),
                   _whole((1, D_MODEL))],
        out_shape=[jax.ShapeDtypeStruct((s, 2 * FFN_H), BF16), jax.ShapeDtypeStruct((s, FFN_H), BF16),
                   jax.ShapeDtypeStruct((s, D_MODEL), BF16), jax.ShapeDtypeStruct((s, D_MODEL), F32),
                   jax.ShapeDtypeStruct((1, D_MODEL), F32)],
        compiler_params=_cp("arbitrary"),
    )(dxo, gu, xm, g, wgu, wd, tok)


def _matmul_tn(a, b, tm, tn, out_dtype=BF16):
    s, m = a.shape
    n = b.shape[1]

    def body(a_ref, b_ref, o_ref):
        o_ref[...] = _dot_tn(a_ref[...].astype(BF16), b_ref[...].astype(BF16)).astype(out_dtype)

    return pl.pallas_call(
        body, name="weight_grad", grid=(m // tm, n // tn),
        in_specs=[pl.BlockSpec((s, tm), lambda i, j: (0, i)), pl.BlockSpec((s, tn), lambda i, j: (0, j))],
        out_specs=pl.BlockSpec((tm, tn), lambda i, j: (i, j)),
        out_shape=jax.ShapeDtypeStruct((m, n), out_dtype),
        compiler_params=_cp("parallel", "parallel"),
    )(a, b)


def _bwd_out(dxm, yc, ys, yb, g, w, tok):
    s = dxm.shape[0]
    tm = TOKEN_TILE

    def body(dxm_ref, yc_ref, ys_ref, yb_ref, g_ref, w_ref, tok_ref, dyc_ref, dys_ref, dyb_ref, cat_ref, dg_ref):
        stats = _group_norms(yc_ref[...], ys_ref[...], yb_ref[...])
        g = g_ref[...]
        cat_ref[...] = (jnp.concatenate([xh for _, xh in stats], axis=1) * g).astype(BF16)
        dcat = _dot_nt(dxm_ref[...].astype(BF16), w_ref[...])
        dgs = []
        off = 0
        for (r, xh), out in zip(stats, (dyc_ref, dys_ref, dyb_ref)):
            cols = slice(off, off + xh.shape[1])
            dx, dgrow = _rms_bwd(xh, r, g[:, cols], dcat[:, cols])
            out[...] = dx
            dgs.append(_colsum(dgrow))
            off += xh.shape[1]
        _accumulate(dg_ref, jnp.concatenate(dgs, axis=1))

    return pl.pallas_call(
        body, name="bwd_out", grid=(s // tm,),
        in_specs=[_rows(tm, D_MODEL), _rows(tm, CONV_W), _rows(tm, SG_W), _rows(tm, SB_W),
                  _whole((1, D_MODEL)), _whole((D_MODEL, D_MODEL)), ORDER_ONLY],
        out_specs=[_rows(tm, CONV_W), _rows(tm, SG_W), _rows(tm, SB_W), _rows(tm, D_MODEL),
                   _whole((1, D_MODEL))],
        out_shape=[jax.ShapeDtypeStruct((s, CONV_W), F32), jax.ShapeDtypeStruct((s, SG_W), F32),
                   jax.ShapeDtypeStruct((s, SB_W), F32), jax.ShapeDtypeStruct((s, D_MODEL), BF16),
                   jax.ShapeDtypeStruct((1, D_MODEL), F32)],
        compiler_params=_cp("arbitrary"),
    )(dxm, yc, ys, yb, g, w, tok)


def _bwd_sb(qn, kst, vst, dy, ltot):
    s = qn.shape[0]

    def body(q_ref, k_ref, v_ref, do_ref, lt_ref, dq_ref, dk_ref, dv_ref, z_buf, da_buf, dz_buf, att_buf):
        i = pl.program_id(1)
        last = 2 * i + 1

        @pl.when(i == 0)
        def _():
            dk_ref[...] = jnp.zeros_like(dk_ref)
            dv_ref[...] = jnp.zeros_like(dv_ref)

        q = q_ref[...]
        dob = do_ref[...].astype(BF16)
        lt = lt_ref[...]
        ltot = _per_head(lt[:, 0:1], lt[:, HEAD_DIM:HEAD_DIM + 1])
        upto = _pair_tri("upto")
        before = _pair_tri("before")
        last0, last1 = slice(CHUNK - 1, CHUNK), slice(PAIR - 1, PAIR)

        def rows(kb):
            return pl.ds(pl.multiple_of(kb * PAIR, PAIR), PAIR)

        def ahead(kb):
            return _dot_nt(q, k_ref[rows(kb), :]), _dot_nt(dob, v_ref[rows(kb), :])

        def behind(kb, dq):
            dzb = dz_buf[...]
            dk_ref[rows(kb), :] += _dot_tn(dzb, q)
            dv_ref[rows(kb), :] += _dot_tn(att_buf[...], dob)
            return dq + _dot(dzb, k_ref[rows(kb), :])

        def block(kb, carry, masked):
            dq, p0, p1, e0, e1 = carry
            z_next, da_next = ahead(jnp.minimum(kb + 1, last))
            dq = behind(jnp.maximum(kb - 1, 0), dq)
            z, sp, lnb, mask = _sb_scores(z_buf[...], i * SB_Q, kb * CHUNK, masked)
            pin = _dot_split(lnb, upto) + _per_head(p0, p1)
            sig = jnp.exp(z - sp)
            att = jnp.exp(z - sp + (ltot - pin))
            if masked:
                att = jnp.where(mask, att, 0.0)
            e = att * da_buf[...]
            ebefore = _dot_split(e, before) + _per_head(e0, e1)
            dz = e - sig * (e + ebefore)
            if masked:
                dz = jnp.where(mask, dz, 0.0)
            z_buf[...] = z_next
            da_buf[...] = da_next
            dz_buf[...] = dz.astype(BF16)
            att_buf[...] = att.astype(BF16)
            return (dq, pin[:, last0], pin[:, last1],
                    ebefore[:, last0] + e[:, last0], ebefore[:, last1] + e[:, last1])

        z_buf[...], da_buf[...] = ahead(0)
        dz_buf[...] = jnp.zeros_like(dz_buf)
        att_buf[...] = jnp.zeros_like(att_buf)
        zero = jnp.zeros((SB_Q, 1), F32)
        carry = (jnp.zeros((SB_Q, CHUNK), F32), zero, zero, zero, zero)
        carry = lax.fori_loop(0, 2 * i, lambda kb, c: block(kb, c, False), carry)
        carry = block(last - 1, carry, True)
        carry = block(last, carry, True)
        dq_ref[...] = behind(last, carry[0])

    blk = pl.BlockSpec((SB_Q, CHUNK), lambda p, i: (i, p))
    seq = pl.BlockSpec((2 * s, CHUNK), lambda p, i: (0, p))
    return pl.pallas_call(
        body, name="bwd_sb", grid=(SB_W // CHUNK, s // SB_Q),
        in_specs=[blk, seq, seq, blk, blk],
        out_specs=[blk, seq, seq],
        out_shape=[jax.ShapeDtypeStruct((s, SB_W), F32), jax.ShapeDtypeStruct((2 * s, SB_W), F32),
                   jax.ShapeDtypeStruct((2 * s, SB_W), F32)],
        scratch_shapes=[pltpu.VMEM((SB_Q, PAIR), F32), pltpu.VMEM((SB_Q, PAIR), F32),
                        pltpu.VMEM((SB_Q, PAIR), BF16), pltpu.VMEM((SB_Q, PAIR), BF16)],
        compiler_params=_cp("parallel", "arbitrary"),
    )(qn, kst, vst, dy, ltot)


def _head_sum(row):
    acc = row[:, 0:HEAD_DIM]
    for h in range(1, SB_W // HEAD_DIM):
        acc = acc + row[:, h * HEAD_DIM:(h + 1) * HEAD_DIM]
    return acc


def _bwd_qk(proj, dqs, dkn, dv, qg, kg):
    s = proj.shape[0]
    tm = TOKEN_TILE

    def body(q_ref, k_ref, dqs_ref, dkn_ref, dv_ref, qg_ref, kg_ref, dp_ref, dqg_ref, dkg_ref, qacc, kacc):
        i = pl.program_id(0)
        gm = _group_mean_matrix(SB_W, HEAD_DIM)

        def one(x, dy, g, acc):
            r = lax.rsqrt(_dot_split(x * x, gm) + RMS_EPS)
            xh = x * r
            dxh = dy * g
            _accumulate(acc, _colsum(dy * xh))
            return r * (dxh - xh * _dot_split(dxh * xh, gm))

        dq = one(q_ref[...], dqs_ref[...] * (HEAD_DIM ** -0.5), qg_ref[...], qacc)
        dk = one(k_ref[...], _unstack_heads(dkn_ref[...]), kg_ref[...], kacc)
        dp_ref[...] = jnp.concatenate([dq, dk, _unstack_heads(dv_ref[...])], axis=1).astype(BF16)

        @pl.when(i == pl.num_programs(0) - 1)
        def _():
            dqg_ref[...] = _head_sum(qacc[...])
            dkg_ref[...] = _head_sum(kacc[...])

    return pl.pallas_call(
        body, name="bwd_qk", grid=(s // tm,),
        in_specs=[_rows(tm, SB_W, OFF_SB // SB_W), _rows(tm, SB_W, OFF_SB // SB_W + 1),
                  _rows(tm, SB_W), _rows(2 * tm, SB_W), _rows(2 * tm, SB_W), _whole((1, SB_W)), _whole((1, SB_W))],
        out_specs=[_rows(tm, 3 * SB_W), _whole((1, HEAD_DIM)), _whole((1, HEAD_DIM))],
        out_shape=[jax.ShapeDtypeStruct((s, 3 * SB_W), BF16), jax.ShapeDtypeStruct((1, HEAD_DIM), F32),
                   jax.ShapeDtypeStruct((1, HEAD_DIM), F32)],
        scratch_shapes=[pltpu.VMEM((1, SB_W), F32), pltpu.VMEM((1, SB_W), F32)],
        compiler_params=_cp("arbitrary"),
    )(proj, proj, dqs, dkn, dv, qg, kg)


def _bwd_sg(proj, dy, lg, lb, w, bias):
    s = proj.shape[0]
    tm = TOKEN_TILE
    nh = SG_W // HEAD_DIM

    def body(p_ref, dy_ref, lg_ref, lb_ref, w_ref, bias_ref, dp_ref, dlg_ref, dlb_ref, dw_ref, db_ref, dbias):
        i = pl.program_id(0)
        uv = p_ref[...]
        ge = _gelu(uv)
        u = ge[:, :SG_W]
        r, xh = _ln_stats(ge[:, SG_W:])
        vln = (xh * lg_ref[...] + lb_ref[...]).astype(BF16)
        dy = dy_ref[...]
        tril, lane_head = _sg_masks()

        @pl.when(i == 0)
        def _():
            dw_ref[...] = jnp.zeros_like(dw_ref)
            dbias[...] = jnp.zeros_like(dbias)

        dus, dvlns = [], []
        for c in range(tm // CHUNK):
            rows = slice(c * CHUNK, (c + 1) * CHUNK)
            vc = vln[rows]
            dus.append(dy[rows] * _sg_mix(w_ref, bias_ref, vc, tril, lane_head))
            dm = dy[rows] * u[rows]
            dbias[...] += dm
            dvc = jnp.zeros((CHUNK, SG_W), F32)
            for h in range(nh):
                dmh = jnp.where(lane_head == h, dm, 0.0).astype(BF16)
                dw_ref[h] += jnp.where(tril, _dot_nt(dmh, vc), 0.0)
                wm = jnp.where(tril, w_ref[h], 0.0).astype(BF16)
                dvc = dvc + _dot_tn(wm, dmh)
            dvlns.append(dvc)
        du = jnp.concatenate(dus, axis=0)
        dvln = jnp.concatenate(dvlns, axis=0)
        _accumulate(dlg_ref, _colsum(dvln * xh))
        _accumulate(dlb_ref, _colsum(dvln))
        dv = _ln_bwd(xh, r, lg_ref[...], dvln)
        dp_ref[...] = (jnp.concatenate([du, dv], axis=1) * _gelu_grad(uv)).astype(BF16)

        @pl.when(i == pl.num_programs(0) - 1)
        def _():
            lane = lax.broadcasted_iota(jnp.int32, (CHUNK, CHUNK), 1)
            acc = dbias[...]
            out = jnp.zeros((CHUNK, CHUNK), F32)
            for h in range(nh):
                hs = jnp.sum(acc[:, h * HEAD_DIM:(h + 1) * HEAD_DIM], axis=1, keepdims=True)
                out = out + jnp.where(lane == h, hs, 0.0)
            db_ref[...] = out

    return pl.pallas_call(
        body, name="bwd_sg", grid=(s // tm,),
        in_specs=[_rows(tm, 2 * SG_W, 1), _rows(tm, SG_W), _whole((1, SG_W)), _whole((1, SG_W)),
                  _whole((nh, CHUNK, CHUNK)), _whole((CHUNK, SG_W))],
        out_specs=[_rows(tm, 2 * SG_W), _whole((1, SG_W)), _whole((1, SG_W)), _whole((nh, CHUNK, CHUNK)),
                   _whole((CHUNK, CHUNK))],
        out_shape=[jax.ShapeDtypeStruct((s, 2 * SG_W), BF16), jax.ShapeDtypeStruct((1, SG_W), F32),
                   jax.ShapeDtypeStruct((1, SG_W), F32), jax.ShapeDtypeStruct((nh, CHUNK, CHUNK), F32),
                   jax.ShapeDtypeStruct((CHUNK, CHUNK), F32)],
        scratch_shapes=[pltpu.VMEM((CHUNK, SG_W), F32)],
        compiler_params=_cp("arbitrary"),
    )(proj, dy, lg, lb, w, bias)


def _bwd_conv(proj, dy, w, b, lg, lb):
    s = proj.shape[0]

    def body(p_ref, dy_ref, w_ref, b_ref, lg_ref, lb_ref, dp_ref, dw_ref, db_ref, dlg_ref, dlb_ref,
             abuf, dcbuf):
        _glu_fill(p_ref, abuf, s)
        dcbuf[pl.ds(s, CHUNK), :] = jnp.zeros((CHUNK, CONV_W), F32)
        dw_ref[...] = jnp.zeros_like(dw_ref)

        def chunk(c, carry):
            db, dlg, dlb = carry
            r0 = pl.multiple_of(c * CHUNK, CHUNK)
            acc, win = _conv_window(abuf, r0, w_ref)
            r, xh = _ln_stats(acc + b_ref[...])
            ln = xh * lg_ref[...] + lb_ref[...]
            sg = _sigmoid(ln)
            dl = dy_ref[pl.ds(r0, CHUNK), :] * (sg * (1.0 + ln * (1.0 - sg)))
            dc = _ln_bwd(xh, r, lg_ref[...], dl)
            dcbuf[pl.ds(r0, CHUNK), :] = dc
            for k in range(CONV_K):
                sh = win if k == 0 else pltpu.roll(win, k, axis=0)
                dw_ref[CONV_K - 1 - k:CONV_K - k, :] += _colsum(dc * sh[32:, :])
            return db + _colsum(dc), dlg + _colsum(dl * xh), dlb + _colsum(dl)

        zero = jnp.zeros((1, CONV_W), F32)
        db, dlg, dlb = lax.fori_loop(0, s // CHUNK, chunk, (zero, zero, zero))
        db_ref[...] = db
        dlg_ref[...] = dlg
        dlb_ref[...] = dlb

        def chunk_back(c, carry):
            r0 = pl.multiple_of(c * CHUNK, CHUNK)
            win = dcbuf[pl.ds(r0, CHUNK + 32), :]
            da = jnp.zeros((CHUNK, CONV_W), F32)
            for k in range(CONV_K):
                sh = win if k == 0 else pltpu.roll(win, CHUNK + 32 - k, axis=0)
                da = da + sh[:CHUNK, :] * w_ref[CONV_K - 1 - k:CONV_K - k, :]
            pv = p_ref[pl.ds(r0, CHUNK), :]
            val, sg = pv[:, :CONV_W], _sigmoid(pv[:, CONV_W:])
            dp_ref[pl.ds(r0, CHUNK), :] = jnp.concatenate([da * sg, da * val * sg * (1.0 - sg)], axis=1).astype(BF16)
            return carry

        lax.fori_loop(0, s // CHUNK, chunk_back, 0)

    row = _whole((1, CONV_W))
    return pl.pallas_call(
        body, name="bwd_conv", grid=(1,),
        in_specs=[pl.BlockSpec((s, 2 * CONV_W), lambda i: (0, 0)), _whole((s, CONV_W)),
                  _whole((CONV_K, CONV_W)), row, row, row],
        out_specs=[_whole((s, 2 * CONV_W)), _whole((CONV_K, CONV_W)), row, row, row],
        out_shape=[jax.ShapeDtypeStruct((s, 2 * CONV_W), BF16), jax.ShapeDtypeStruct((CONV_K, CONV_W), F32)]
        + [jax.ShapeDtypeStruct((1, CONV_W), F32)] * 3,
        scratch_shapes=[pltpu.VMEM((s + CHUNK, CONV_W), F32), pltpu.VMEM((s + CHUNK, CONV_W), F32)],
        compiler_params=_cp("arbitrary"),
    )(proj, dy, w, b, lg, lb)


def _bwd_in(dpc, dps, dpb, x, g, w, dxm):
    s = x.shape[0]
    tm = TOKEN_TILE

    def body(dpc_ref, dps_ref, dpb_ref, x_ref, g_ref, w_ref, dxm_ref, dx_ref, h_ref, dp_ref, dg_ref):
        dp = jnp.concatenate([dpc_ref[...], dps_ref[...], dpb_ref[...]], axis=1)
        dp_ref[...] = dp
        dh = _dot(dp, w_ref[...])
        r, xh = _rms_stats(x_ref[...])
        h_ref[...] = (xh * g_ref[...]).astype(BF16)
        dx, dgrow = _rms_bwd(xh, r, g_ref[...], dh)
        dx_ref[...] = dxm_ref[...] + dx
        _accumulate(dg_ref, _colsum(dgrow))

    return pl.pallas_call(
        body, name="bwd_in", grid=(s // tm,),
        in_specs=[_rows(tm, 2 * CONV_W), _rows(tm, 2 * SG_W), _rows(tm, 3 * SB_W), _rows(tm, D_MODEL),
                  _whole((1, D_MODEL)), _whole((IN_W, D_MODEL)), _rows(tm, D_MODEL)],
        out_specs=[_rows(tm, D_MODEL), _rows(tm, D_MODEL), _rows(tm, IN_W), _whole((1, D_MODEL))],
        out_shape=[jax.ShapeDtypeStruct((s, D_MODEL), F32), jax.ShapeDtypeStruct((s, D_MODEL), BF16),
                   jax.ShapeDtypeStruct((s, IN_W), BF16), jax.ShapeDtypeStruct((1, D_MODEL), F32)],
        compiler_params=_cp("arbitrary"),
    )(dpc, dps, dpb, x, g, w, dxm)


SMALL = ("mix_norm_g", "conv_w", "conv_b", "conv_ln_g", "conv_ln_b", "sg_ln_g", "sg_ln_b", "sg_w", "sg_b",
         "q_norm_g", "k_norm_g", "out_norm_g", "ffn_norm_g")
LARGE = ("w_in", "w_out", "w_gate_up", "w_down")


def _row(v):
    return v.reshape(1, -1)


def _layer_params(p, large, l):
    q = {k: v[l] for k, v in p.items()}
    return dict(
        q, **large,
        mix_norm_g=_row(q["mix_norm_g"]), conv_b=_row(q["conv_b"]), conv_ln_g=_row(q["conv_ln_g"]),
        conv_ln_b=_row(q["conv_ln_b"]), sg_ln_g=_row(q["sg_ln_g"]), sg_ln_b=_row(q["sg_ln_b"]),
        out_norm_g=_row(q["out_norm_g"]), ffn_norm_g=_row(q["ffn_norm_g"]),
        qg=_row(jnp.tile(q["q_norm_g"], SB_W // HEAD_DIM)), kg=_row(jnp.tile(q["k_norm_g"], SB_W // HEAD_DIM)),
        sg_bias=jnp.repeat(q["sg_b"].T, HEAD_DIM, axis=1),
    )


def _layer_fwd(x, q, tok, after_in, after_mixers):
    proj, qn, kn, vb = _fwd_in(x, q["mix_norm_g"], q["w_in"], q["qg"], q["kg"], tok)
    yc = _fwd_conv(proj, q["conv_w"], q["conv_b"], q["conv_ln_g"], q["conv_ln_b"], after_in(proj))
    ys = _fwd_sg(proj, q["sg_ln_g"], q["sg_ln_b"], q["sg_w"], q["sg_bias"])
    yb, lt = _fwd_sb(qn, kn, vb)
    rest, tok = after_mixers(yb)
    q = dict(q, **rest)
    xm = _fwd_out(yc, ys, yb, q["out_norm_g"], q["w_out"], x, tok)
    gu, xo = _fwd_ffn(xm, q["ffn_norm_g"], q["w_gate_up"], q["w_down"])
    return xo, q, dict(x=x, proj=proj, qn=qn, kn=kn, vb=vb, lt=lt, yc=yc, ys=ys, yb=yb, xm=xm, gu=gu)


def _layer_bwd_ffn(dxo, q, st, tok):
    dgu, act, h2, dxm, d_ffn_g = _bwd_ffn(dxo, st["gu"], st["xm"], q["ffn_norm_g"], q["w_gate_up"], q["w_down"],
                                          tok)
    return dxm, d_ffn_g, _matmul_tn(dgu, h2, 512, D_MODEL), _matmul_tn(act, dxo, FFN_H // 2, D_MODEL)


def _layer_bwd_mix(dxm, d_ffn_g, q, st, tok):
    dyc, dys, dyb, cat, d_out_g = _bwd_out(dxm, st["yc"], st["ys"], st["yb"], q["out_norm_g"], q["w_out"], tok)
    d_wo = _matmul_tn(cat, dxm, 512, D_MODEL)
    dqs, dkn, dv = _bwd_sb(st["qn"], st["kn"], st["vb"], dyb, st["lt"])
    dpb, d_qg, d_kg = _bwd_qk(st["proj"], dqs, dkn, dv, q["qg"], q["kg"])
    dps, d_sg_lg, d_sg_lb, d_sg_w, d_sg_b = _bwd_sg(st["proj"], dys, q["sg_ln_g"], q["sg_ln_b"], q["sg_w"],
                                                    q["sg_bias"])
    dpc, d_conv_w, d_conv_b, d_conv_lg, d_conv_lb = _bwd_conv(st["proj"], dyc, q["conv_w"], q["conv_b"],
                                                              q["conv_ln_g"], q["conv_ln_b"])
    dx, h1, dp, d_mix_g = _bwd_in(dpc, dps, dpb, st["x"], q["mix_norm_g"], q["w_in"], dxm)
    d_win = _matmul_tn(dp, h1, 512, D_MODEL)
    small = dict(
        mix_norm_g=d_mix_g[0], conv_w=d_conv_w, conv_b=d_conv_b[0], conv_ln_g=d_conv_lg[0],
        conv_ln_b=d_conv_lb[0], sg_ln_g=d_sg_lg[0], sg_ln_b=d_sg_lb[0], sg_w=d_sg_w,
        sg_b=d_sg_b[:, :SG_W // HEAD_DIM].T, q_norm_g=d_qg[0], k_norm_g=d_kg[0], out_norm_g=d_out_g[0],
        ffn_norm_g=d_ffn_g[0])
    return dx, d_wo, d_win, small


def _position():
    x, y, c = lax.axis_index("x"), lax.axis_index("y"), lax.axis_index("c")
    return x, y, c


def _flat(px, py, pc):
    return 4 * px + 2 * py + pc


IN_HBM = pl.BlockSpec(memory_space=pltpu.HBM)
IN_SEM = pl.BlockSpec(memory_space=pltpu.SEMAPHORE)
EFFECT = pltpu.SideEffectType.DATAFLOW_SIDE_EFFECTING
COPIES = dict(scatter=7, spread=7, spread_chips=4, **{"pass": 3})


def _exchange_copies(kinds, src_refs, land_refs, send_sems, recv_sems, layer, arrival):
    x, y, c = _position()
    me = _flat(x, y, c)
    everyone = [(x ^ (k >> 2 & 1), y ^ (k >> 1 & 1), c ^ (k & 1)) for k in range(1, N_DEV)]
    sibling = (x, y, 1 - c)
    chips = [(1 - x, y, c), (x, 1 - y, c), (1 - x, 1 - y, c)]
    out = []
    srcs = iter(src_refs)
    for kind, land in zip(kinds, land_refs):
        land = land if layer is None else land.at[layer]
        if kind == "scatter":
            src = next(srcs)
            moves = [(src.at[_flat(*p)], me, _flat(*p), p) for p in everyone]
        elif kind in ("spread", "spread_chips"):
            src = next(srcs)
            moves = [(src, me, _flat(*p), p) for p in (everyone if kind == "spread" else [sibling] + chips)]
        else:
            moves = [(land.at[_flat(*p)], _flat(*p), _flat(p[0], p[1], 1 - c), sibling) for p in chips]
        for src_block, there, here, peer in moves:
            n = len(out)
            out.append(pltpu.make_async_remote_copy(
                src_ref=src_block, dst_ref=land.at[here if arrival else there], send_sem=send_sems.at[n],
                recv_sem=recv_sems.at[n], device_id=peer, device_id_type=MESH))
    return out


def _exchange_start(kinds, srcs, lands, after, name, layer=None):
    ns, n = len(srcs), len(srcs) + len(lands)
    sems = sum(COPIES[k] for k in kinds)

    def body(*refs):
        send_sems, recv_sems = refs[n + 1], refs[n + 2]
        for cp in _exchange_copies(kinds, refs[:ns], refs[ns:n], send_sems, recv_sems, layer, arrival=False):
            cp.start()
        refs[-1][...] = jnp.zeros_like(refs[-1])

    thru = [pltpu.HBM(a.shape, a.dtype) for a in (*srcs, *lands)]
    outs = pl.pallas_call(
        body, name=name,
        out_shape=(pltpu.SemaphoreType.DMA((sems,)), pltpu.SemaphoreType.DMA((sems,)), *thru,
                   jax.ShapeDtypeStruct((8, 128), F32)),
        in_specs=[IN_HBM] * n + [ORDER_ONLY],
        out_specs=(IN_SEM, IN_SEM, *[IN_HBM] * n, pl.BlockSpec(memory_space=pltpu.VMEM)),
        input_output_aliases={i: 2 + i for i in range(n)},
        compiler_params=pltpu.CompilerParams(has_side_effects=EFFECT),
    )(*[pltpu.with_memory_space_constraint(a, pltpu.HBM) for a in (*srcs, *lands)], after)
    return kinds, outs[0], outs[1], list(outs[2:2 + ns]), list(outs[2 + ns:2 + n]), outs[-1]


def _exchange_wait(pending, after, name, layer=None):
    kinds, send_sems, recv_sems, srcs, lands, _ = pending
    ns, n = len(srcs), len(srcs) + len(lands)

    def body(*refs):
        for cp in _exchange_copies(kinds, refs[:ns], refs[ns:n], refs[n], refs[n + 1], layer, arrival=True):
            cp.wait_send()
            cp.wait_recv()

    thru = [pltpu.HBM(a.shape, a.dtype) for a in (*srcs, *lands)]
    outs = pl.pallas_call(
        body, name=name, out_shape=tuple(thru),
        in_specs=[IN_HBM] * n + [IN_SEM, IN_SEM, ORDER_ONLY],
        out_specs=tuple([IN_HBM] * n),
        input_output_aliases={i: i for i in range(n)},
        compiler_params=pltpu.CompilerParams(has_side_effects=EFFECT),
    )(*srcs, *lands, send_sems, recv_sems, after)
    return list(outs[ns:])


def _landing(block, me):
    land = lax.empty((N_DEV,) + block.shape, block.dtype)
    return lax.dynamic_update_index_in_dim(land, block, me, 0)


def _adamw(parts, w, m, v, tr):
    groups, rows, cols = w.shape

    def body(p_ref, w_ref, m_ref, v_ref, g_ref, d_ref, nm_ref, nv_ref):
        g = p_ref[0].astype(F32)
        for j in range(1, N_DEV):
            g = g + p_ref[j].astype(F32)
        g_ref[...] = g
        m = ADAM_B1 * m_ref[...] + (1.0 - ADAM_B1) * g
        v = ADAM_B2 * v_ref[...] + (1.0 - ADAM_B2) * (g * g)
        nm_ref[...] = m
        nv_ref[...] = v
        m_hat = m / (1.0 - ADAM_B1 ** ADAM_STEP)
        v_hat = v / (1.0 - ADAM_B2 ** ADAM_STEP)
        d_ref[...] = -ADAM_LR * (m_hat / (jnp.sqrt(v_hat) + ADAM_EPS) + ADAM_WD * w_ref[...])

    blk = pl.BlockSpec((None, tr, cols), lambda g, i: (g, i, 0))
    return pl.pallas_call(
        body, name="adamw", grid=(groups, rows // tr),
        in_specs=[pl.BlockSpec((None, N_DEV, tr, cols), lambda g, i: (g, 0, i, 0)), blk, blk, blk],
        out_specs=[blk] * 4,
        out_shape=[jax.ShapeDtypeStruct((groups, rows, cols), F32)] * 4,
        compiler_params=_cp("parallel", "parallel"),
    )(parts, w, m, v)


def _row_tile(rows):
    for cand in (512, 256, 128, 64, 32, 16, 8):
        if rows % cand == 0 and rows > cand:
            return cand
    return rows


def _with_own_block(land, blocks, layer, me):
    own = lax.dynamic_index_in_dim(blocks, me, 0, keepdims=True)[None]
    return lax.dynamic_update_slice(land, own, (layer, me, 0, 0))


PACK_LANES = 128


def _pack_layers(arrs):
    parts = []
    for a in arrs:
        flat = a.reshape(a.shape[0], -1)
        parts.append(jnp.pad(flat, ((0, 0), (0, -flat.shape[1] % (8 * PACK_LANES)))))
    return jnp.concatenate(parts, axis=1).reshape(arrs[0].shape[0], -1, PACK_LANES)


def _unpack_layers(packed, shapes):
    flat = packed.reshape(packed.shape[0], -1)
    outs, off = [], 0
    for shp in shapes:
        size = 1
        for d in shp[1:]:
            size *= d
        outs.append(flat[:, off:off + size].reshape(shp))
        off += size + (-size % (8 * PACK_LANES))
    return outs


def kernel(x, mix_norm_g, w_in, conv_w, conv_b, conv_ln_g, conv_ln_b, sg_ln_g, sg_ln_b, sg_w, sg_b, q_norm_g, k_norm_g, out_norm_g, w_out, ffn_norm_g, w_gate_up, w_down, loss_target, m_mix_norm_g, m_w_in, m_conv_w, m_conv_b, m_conv_ln_g, m_conv_ln_b, m_sg_ln_g, m_sg_ln_b, m_sg_w, m_sg_b, m_q_norm_g, m_k_norm_g, m_out_norm_g, m_w_out, m_ffn_norm_g, m_w_gate_up, m_w_down, v_mix_norm_g, v_w_in, v_conv_w, v_conv_b, v_conv_ln_g, v_conv_ln_b, v_sg_ln_g, v_sg_ln_b, v_sg_w, v_sg_b, v_q_norm_g, v_k_norm_g, v_out_norm_g, v_w_out, v_ffn_norm_g, v_w_gate_up, v_w_down):
    names = SMALL[:1] + LARGE[:1] + SMALL[1:12] + LARGE[1:2] + SMALL[12:] + LARGE[2:]
    w = dict(mix_norm_g=mix_norm_g, w_in=w_in, conv_w=conv_w, conv_b=conv_b, conv_ln_g=conv_ln_g,
             conv_ln_b=conv_ln_b, sg_ln_g=sg_ln_g, sg_ln_b=sg_ln_b, sg_w=sg_w, sg_b=sg_b, q_norm_g=q_norm_g,
             k_norm_g=k_norm_g, out_norm_g=out_norm_g, w_out=w_out, ffn_norm_g=ffn_norm_g,
             w_gate_up=w_gate_up, w_down=w_down)
    m = dict(mix_norm_g=m_mix_norm_g, w_in=m_w_in, conv_w=m_conv_w, conv_b=m_conv_b, conv_ln_g=m_conv_ln_g,
             conv_ln_b=m_conv_ln_b, sg_ln_g=m_sg_ln_g, sg_ln_b=m_sg_ln_b, sg_w=m_sg_w, sg_b=m_sg_b,
             q_norm_g=m_q_norm_g, k_norm_g=m_k_norm_g, out_norm_g=m_out_norm_g, w_out=m_w_out,
             ffn_norm_g=m_ffn_norm_g, w_gate_up=m_w_gate_up, w_down=m_w_down)
    v = dict(mix_norm_g=v_mix_norm_g, w_in=v_w_in, conv_w=v_conv_w, conv_b=v_conv_b, conv_ln_g=v_conv_ln_g,
             conv_ln_b=v_conv_ln_b, sg_ln_g=v_sg_ln_g, sg_ln_b=v_sg_ln_b, sg_w=v_sg_w, sg_b=v_sg_b,
             q_norm_g=v_q_norm_g, k_norm_g=v_k_norm_g, out_norm_g=v_out_norm_g, w_out=v_w_out,
             ffn_norm_g=v_ffn_norm_g, w_gate_up=v_w_gate_up, w_down=v_w_down)
    xpos, ypos, cpos = _position()
    me = _flat(xpos, ypos, cpos)
    conv_cols = conv_w.shape[-1]
    no_token = jnp.zeros((8, 128), F32)
    w, m, v = (dict(t, w_in=jnp.swapaxes(t["w_in"], 1, 2), w_gate_up=jnp.swapaxes(t["w_gate_up"], 1, 2))
               for t in (w, m, v))
    shards = {k: w[k].astype(BF16) for k in LARGE}
    full_shape = dict(w_in=(IN_W, D_MODEL), w_out=(D_MODEL, D_MODEL), w_gate_up=(2 * FFN_H, D_MODEL),
                      w_down=(FFN_H, D_MODEL))

    def gather_start(srcs, after, tag):
        return _exchange_start(["spread_chips"] * len(srcs), srcs, [_landing(a, me) for a in srcs], after,
                               f"gather_start_{tag}")

    def gather_pass(pending, after, tag):
        lands = _exchange_wait(pending, after, f"gather_wait_{tag}")
        return _exchange_start(["pass"] * len(lands), [], lands, lands[0], f"gather_pass_{tag}")

    def gathered(pending, keys, after, tag):
        lands = _exchange_wait(pending, after, f"gather_passed_{tag}")
        return {k: a.reshape(full_shape[k]) for k, a in zip(keys, lands)}, lands[len(keys):]

    first, later = ("w_in",), ("w_out", "w_gate_up", "w_down")
    act = x[0]
    head = gather_start([shards["w_in"][0], w["conv_w"]], act, "0")
    tail = gather_start([shards[k][0] for k in later], head[5], "0_later")
    head = gather_pass(head, tail[5], "0")
    large, (conv_blocks,) = gathered(head, first, head[5], "0")
    conv_full = jnp.transpose(conv_blocks, (1, 2, 0, 3)).reshape(DEPTH, CONV_K, CONV_W)
    small_w = dict({k: w[k] for k in SMALL}, conv_w=conv_full)
    qs, stash = [], []
    for l in range(DEPTH):
        coming = {}

        def after_in(proj):
            if l + 1 == DEPTH:
                return no_token
            coming["first"] = gather_start([shards[k][l + 1] for k in LARGE], proj, str(l + 1))
            return coming["first"][5]

        def after_mixers(y_sb):
            rest, token = {}, no_token
            if l == 0:
                passing = gather_pass(tail, y_sb, "0_later")
                rest, _ = gathered(passing, later, passing[5], "0_later")
            if l + 1 < DEPTH:
                coming["second"] = gather_pass(coming["first"], y_sb, str(l + 1))
                token = coming["second"][5]
            return rest, token

        act, q, st = _layer_fwd(act, _layer_params(small_w, large, l), no_token, after_in, after_mixers)
        qs.append(q)
        stash.append(st)
        if l + 1 < DEPTH:
            large, _ = gathered(coming["second"], LARGE, act, str(l + 1))

    loss, dx = _loss_head(act, loss_target[0])
    loss = lax.psum(loss[0, 0], ("x", "y", "c"))

    replicated = tuple(k for k in SMALL if k != "conv_w")
    small_rows = _pack_layers([w[k][:1] for k in replicated]).shape[1]
    conv_rows = _pack_layers([conv_full[:1]]).shape[1]
    land_a = [lax.empty((DEPTH, N_DEV) + w[k].shape[1:], BF16) for k in ("w_gate_up", "w_down")]
    land_b = [lax.empty((DEPTH, N_DEV) + w[k].shape[1:], BF16) for k in ("w_out", "w_in")]
    land_b.append(lax.empty((DEPTH, N_DEV, small_rows + conv_rows, PACK_LANES), F32))
    pend_a = pend_b = None
    token = no_token
    for l in reversed(range(DEPTH)):
        dxm, d_ffn_g, d_wgu, d_wd = _layer_bwd_ffn(dx, qs[l], stash[l], token)
        srcs = [d_wgu.reshape((N_DEV,) + w["w_gate_up"].shape[1:]), d_wd.reshape((N_DEV,) + w["w_down"].shape[1:])]
        if pend_a is not None:
            land_a = _exchange_wait(pend_a, d_wd, f"grads_a_wait_{l + 1}", layer=l + 1)
        land_a = [_with_own_block(ld, a, l, me) for ld, a in zip(land_a, srcs)]
        pend_a = _exchange_start(["scatter"] * 2, srcs, land_a, d_wd, f"grads_a_start_{l}", layer=l)
        dx, d_wo, d_win, small = _layer_bwd_mix(dxm, d_ffn_g, qs[l], stash[l], pend_a[5])
        packed = _pack_layers([small[k][None] for k in replicated + ("conv_w",)])[0]
        srcs = [d_wo.reshape((N_DEV,) + w["w_out"].shape[1:]), d_win.reshape((N_DEV,) + w["w_in"].shape[1:])]
        if pend_b is not None:
            land_b = _exchange_wait(pend_b, d_win, f"grads_b_wait_{l + 1}", layer=l + 1)
        land_b = [_with_own_block(ld, a, l, me) for ld, a in zip(land_b, srcs)] + [
            lax.dynamic_update_slice(land_b[2], packed[None, None], (l, me, 0, 0))]
        pend_b = _exchange_start(["scatter", "scatter", "spread"], srcs + [packed], land_b, d_win,
                                 f"grads_b_start_{l}", layer=l)
        token = pend_b[5]

    land_a = _exchange_wait(pend_a, token, "grads_a_wait_0", layer=0)
    res = {}
    for k, parts in zip(("w_gate_up", "w_down"), land_a):
        res[k] = _adamw(parts, w[k], m[k], v[k], _row_tile(w[k].shape[1]))
    land_b = _exchange_wait(pend_b, res["w_down"][0], "grads_b_wait_0", layer=0)
    for k, parts in zip(("w_out", "w_in"), land_b):
        res[k] = _adamw(parts, w[k], m[k], v[k], _row_tile(w[k].shape[1]))
    small_parts = land_b[2]
    updated = _adamw(small_parts, *(_pack_layers([t[k] for k in replicated]) for t in (w, m, v)), small_rows)
    unpacked = [_unpack_layers(o, [w[k].shape for k in replicated]) for o in updated]
    res.update({k: [u[i] for u in unpacked] for i, k in enumerate(replicated)})
    conv_parts = small_parts[:, :, small_rows:].reshape(DEPTH, N_DEV, -1)[:, :, :CONV_K * CONV_W]
    conv_parts = lax.dynamic_slice_in_dim(conv_parts.reshape(DEPTH, N_DEV, CONV_K, CONV_W), me * conv_cols,
                                          conv_cols, axis=3)
    res["conv_w"] = _adamw(conv_parts, w["conv_w"], m["conv_w"], v["conv_w"], CONV_K)
    for k in ("w_in", "w_gate_up"):
        res[k] = [jnp.swapaxes(a, 1, 2) for a in res[k]]

    return (loss, dx[None], *[res[k][0] for k in names], *[res[k][1] for k in names],
            *[res[k][2] for k in names], *[res[k][3] for k in names])
```

```python
import functools

import jax
import jax.numpy as jnp
from jax import lax
from jax.experimental import pallas as pl
from jax.experimental.pallas import tpu as pltpu

F32 = jnp.float32
BF16 = jnp.bfloat16

D_MODEL = 1024
DEPTH = 4
HEAD_DIM = 64
CONV_W = 256
SG_W = 256
SB_W = 512
IN_W = 2560
FFN_H = 2816
CONV_K = 31
CHUNK = 128
OFF_SG = 2 * CONV_W
OFF_SB = OFF_SG + 2 * SG_W
RMS_EPS = 1e-6
LN_EPS = 1e-5
N_DEV = 8
MESH = pl.DeviceIdType.MESH

ADAM_LR = 0.001
ADAM_B1 = 0.9
ADAM_B2 = 0.999
ADAM_EPS = 1e-08
ADAM_WD = 0.01
ADAM_STEP = 10

TOKEN_TILE = 256
VMEM_LIMIT = 56 * 1024 * 1024


def _cp(*sem):
    return pltpu.CompilerParams(dimension_semantics=sem or None, vmem_limit_bytes=VMEM_LIMIT)


def _dot(a, b):
    return jnp.dot(a, b, preferred_element_type=F32)


def _dot_nt(a, b):
    return lax.dot_general(a, b, (((1,), (1,)), ((), ())), preferred_element_type=F32)


def _dot_tn(a, b):
    return lax.dot_general(a, b, (((0,), (0,)), ((), ())), preferred_element_type=F32)


def _dot_split(x, m):
    hi = x.astype(BF16)
    lo = (x - hi.astype(F32)).astype(BF16)
    return _dot(hi, m) + _dot(lo, m)


def _group_mean_matrix(width, group):
    r = lax.broadcasted_iota(jnp.int32, (width, width), 0) // group
    c = lax.broadcasted_iota(jnp.int32, (width, width), 1) // group
    return jnp.where(r == c, 1.0 / group, 0.0).astype(BF16)


def _sigmoid(x):
    return 1.0 / (1.0 + jnp.exp(-x))


def _gelu(x):
    return 0.5 * x * (1.0 + lax.erf(x * (2.0 ** -0.5)))


def _gelu_grad(x):
    return 0.5 * (1.0 + lax.erf(x * (2.0 ** -0.5))) + x * jnp.exp(-0.5 * x * x) * (0.5 * (2.0 / jnp.pi) ** 0.5)


def _rms_stats(x):
    r = lax.rsqrt(jnp.mean(x * x, axis=-1, keepdims=True) + RMS_EPS)
    return r, x * r


def _rms_bwd(xh, r, g, dy):
    dxh = dy * g
    dx = r * (dxh - xh * jnp.mean(dxh * xh, axis=-1, keepdims=True))
    return dx, dy * xh


def _ln_stats(x):
    mu = jnp.mean(x, axis=-1, keepdims=True)
    xc = x - mu
    r = lax.rsqrt(jnp.mean(xc * xc, axis=-1, keepdims=True) + LN_EPS)
    return r, xc * r


def _ln_bwd(xh, r, g, dy):
    dxh = dy * g
    return r * (dxh - jnp.mean(dxh, axis=-1, keepdims=True) - xh * jnp.mean(dxh * xh, axis=-1, keepdims=True))


def _colsum(x):
    return jnp.sum(x, axis=0, keepdims=True)


def _rows(tm, n, j=0):
    return pl.BlockSpec((tm, n), lambda i: (i, j))


def _whole(shape):
    return pl.BlockSpec(shape, lambda i: (0,) * len(shape))


ORDER_ONLY = pl.BlockSpec(memory_space=pl.ANY)


def _stack_heads(a):
    even = (lax.broadcasted_iota(jnp.int32, a.shape, 1) % (2 * HEAD_DIM)) < HEAD_DIM
    top = jnp.where(even, a, 0.0)
    bot = jnp.where(even, 0.0, a)
    parts = []
    for c in range(a.shape[0] // CHUNK):
        rows = slice(c * CHUNK, (c + 1) * CHUNK)
        parts += [top[rows], bot[rows]]
    return jnp.concatenate(parts, axis=0)


def _store_stacked(st, st_ref, tr_ref):
    st_ref[...] = st.astype(BF16)
    for p in range(SB_W // CHUNK):
        for c in range(st.shape[0] // (2 * CHUNK)):
            tile = st[2 * c * CHUNK:2 * (c + 1) * CHUNK, p * CHUNK:(p + 1) * CHUNK]
            tr_ref[p, c] = tile.T.astype(BF16)


def _load_transposed(tr_ref):
    rows = []
    for c in range(tr_ref.shape[1]):
        tiles = [tr_ref[p, c].T for p in range(SB_W // CHUNK)]
        rows.append(jnp.concatenate(tiles, axis=1))
    return jnp.concatenate(rows, axis=0)


def _unstack_heads(st):
    even = (lax.broadcasted_iota(jnp.int32, (CHUNK, st.shape[1]), 1) % (2 * HEAD_DIM)) < HEAD_DIM
    parts = []
    for c in range(st.shape[0] // (2 * CHUNK)):
        top = st[2 * c * CHUNK:(2 * c + 1) * CHUNK]
        bot = st[(2 * c + 1) * CHUNK:(2 * c + 2) * CHUNK]
        parts.append(jnp.where(even, top, bot))
    return jnp.concatenate(parts, axis=0)


def _fwd_in(x, g, w, qg, kg, tok):
    s = x.shape[0]
    tm = TOKEN_TILE

    def body(x_ref, g_ref, w_ref, qg_ref, kg_ref, tok_ref, proj_ref, qn_ref, kn_ref, vb_ref, kt_ref, vt_ref):
        r, xh = _rms_stats(x_ref[...])
        h = (xh * g_ref[...]).astype(BF16)
        proj = _dot_nt(h, w_ref[...])
        proj_ref[...] = proj
        gm = _group_mean_matrix(SB_W, HEAD_DIM)
        q = proj[:, OFF_SB:OFF_SB + SB_W]
        k = proj[:, OFF_SB + SB_W:OFF_SB + 2 * SB_W]
        rq = lax.rsqrt(_dot_split(q * q, gm) + RMS_EPS)
        rk = lax.rsqrt(_dot_split(k * k, gm) + RMS_EPS)
        qn_ref[...] = (q * rq * qg_ref[...] * (HEAD_DIM ** -0.5)).astype(BF16)
        _store_stacked(_stack_heads(k * rk * kg_ref[...]), kn_ref, kt_ref)
        _store_stacked(_stack_heads(proj[:, OFF_SB + 2 * SB_W:]), vb_ref, vt_ref)

    tiles = pl.BlockSpec((SB_W // CHUNK, tm // CHUNK, CHUNK, PAIR), lambda i: (0, i, 0, 0))
    tiles_shape = jax.ShapeDtypeStruct((SB_W // CHUNK, s // CHUNK, CHUNK, PAIR), BF16)
    return pl.pallas_call(
        body, name="fwd_in", grid=(s // tm,),
        in_specs=[_rows(tm, D_MODEL), _whole((1, D_MODEL)), _whole((IN_W, D_MODEL)),
                  _whole((1, SB_W)), _whole((1, SB_W)), ORDER_ONLY],
        out_specs=[_rows(tm, IN_W), _rows(tm, SB_W), _rows(2 * tm, SB_W), _rows(2 * tm, SB_W), tiles, tiles],
        out_shape=[jax.ShapeDtypeStruct((s, IN_W), F32), jax.ShapeDtypeStruct((s, SB_W), BF16),
                   jax.ShapeDtypeStruct((2 * s, SB_W), BF16), jax.ShapeDtypeStruct((2 * s, SB_W), BF16),
                   tiles_shape, tiles_shape],
        compiler_params=_cp("parallel"),
    )(x, g, w, qg, kg, tok)


def _conv_window(abuf, r0, w_ref):
    win = abuf[pl.ds(pl.multiple_of(r0 + CHUNK - 32, 32), CHUNK + 32), :]
    acc = jnp.zeros((CHUNK, CONV_W), F32)
    for k in range(CONV_K):
        sh = win if k == 0 else pltpu.roll(win, k, axis=0)
        acc = acc + sh[32:, :] * w_ref[CONV_K - 1 - k:CONV_K - k, :]
    return acc, win


def _glu_fill(p_ref, abuf, s):
    abuf[0:CHUNK, :] = jnp.zeros((CHUNK, CONV_W), F32)

    def fill(c, carry):
        r0 = pl.multiple_of(c * CHUNK, CHUNK)
        pv = p_ref[pl.ds(r0, CHUNK), :]
        abuf[pl.ds(r0 + CHUNK, CHUNK), :] = pv[:, :CONV_W] * _sigmoid(pv[:, CONV_W:])
        return carry

    lax.fori_loop(0, s // CHUNK, fill, 0)


def _fwd_conv(proj, w, b, lg, lb, tok):
    s = proj.shape[0]

    def body(p_ref, w_ref, b_ref, lg_ref, lb_ref, tok_ref, y_ref, abuf):
        _glu_fill(p_ref, abuf, s)

        def chunk(c, carry):
            r0 = pl.multiple_of(c * CHUNK, CHUNK)
            acc, _ = _conv_window(abuf, r0, w_ref)
            r, xh = _ln_stats(acc + b_ref[...])
            ln = xh * lg_ref[...] + lb_ref[...]
            y_ref[pl.ds(r0, CHUNK), :] = ln * _sigmoid(ln)
            return carry

        lax.fori_loop(0, s // CHUNK, chunk, 0)

    return pl.pallas_call(
        body, name="fwd_conv", grid=(1,),
        in_specs=[pl.BlockSpec((s, 2 * CONV_W), lambda i: (0, 0)), _whole((CONV_K, CONV_W)),
                  _whole((1, CONV_W)), _whole((1, CONV_W)), _whole((1, CONV_W)), ORDER_ONLY],
        out_specs=_whole((s, CONV_W)),
        out_shape=jax.ShapeDtypeStruct((s, CONV_W), F32),
        scratch_shapes=[pltpu.VMEM((s + CHUNK, CONV_W), F32)],
        compiler_params=_cp("arbitrary"),
    )(proj, w, b, lg, lb, tok)


def _sg_masks():
    row = lax.broadcasted_iota(jnp.int32, (CHUNK, CHUNK), 0)
    col = lax.broadcasted_iota(jnp.int32, (CHUNK, CHUNK), 1)
    lane_head = lax.broadcasted_iota(jnp.int32, (CHUNK, SG_W), 1) // HEAD_DIM
    return row >= col, lane_head


def _sg_mix(w_ref, bias_ref, vc, tril, lane_head):
    mixed = bias_ref[...]
    for h in range(SG_W // HEAD_DIM):
        wm = jnp.where(tril, w_ref[h], 0.0).astype(BF16)
        mixed = mixed + jnp.where(lane_head == h, _dot(wm, vc), 0.0)
    return mixed


def _fwd_sg(proj, lg, lb, w, bias):
    s = proj.shape[0]
    tm = TOKEN_TILE

    def body(p_ref, lg_ref, lb_ref, w_ref, bias_ref, y_ref):
        ge = _gelu(p_ref[...])
        u = ge[:, :SG_W]
        r, xh = _ln_stats(ge[:, SG_W:])
        vln = (xh * lg_ref[...] + lb_ref[...]).astype(BF16)
        tril, lane_head = _sg_masks()
        for c in range(tm // CHUNK):
            rows = slice(c * CHUNK, (c + 1) * CHUNK)
            y_ref[rows, :] = u[rows] * _sg_mix(w_ref, bias_ref, vln[rows], tril, lane_head)

    return pl.pallas_call(
        body, name="fwd_sg", grid=(s // tm,),
        in_specs=[_rows(tm, 2 * SG_W, 1), _whole((1, SG_W)), _whole((1, SG_W)),
                  _whole((SG_W // HEAD_DIM, CHUNK, CHUNK)), _whole((CHUNK, SG_W))],
        out_specs=_rows(tm, SG_W),
        out_shape=jax.ShapeDtypeStruct((s, SG_W), F32),
        compiler_params=_cp("parallel"),
    )(proj, lg, lb, w, bias)


SB_Q = 2 * CHUNK
PAIR = 2 * CHUNK


def _pair_tri(kind):
    row = lax.broadcasted_iota(jnp.int32, (PAIR, PAIR), 0)
    col = lax.broadcasted_iota(jnp.int32, (PAIR, PAIR), 1)
    tri = {"after": row > col, "upto": row <= col, "before": row < col}[kind]
    return jnp.where(((row // CHUNK) == (col // CHUNK)) & tri, 1.0, 0.0).astype(BF16)


def _sb_scores(z, qpos0, kpos0, masked):
    sp = jnp.maximum(z, 0.0) + jnp.log(1.0 + jnp.exp(-jnp.abs(z)))
    if not masked:
        return z, sp, -sp, None
    row = lax.broadcasted_iota(jnp.int32, z.shape, 0)
    col = lax.broadcasted_iota(jnp.int32, z.shape, 1) % CHUNK
    mask = (kpos0 + col) < (qpos0 + row)
    return z, sp, jnp.where(mask, -sp, 0.0), mask


def _per_head(c0, c1):
    return jnp.concatenate([jnp.broadcast_to(c0, (SB_Q, CHUNK)), jnp.broadcast_to(c1, (SB_Q, CHUNK))], axis=1)


def _fwd_sb(qn, ktr, vst):
    s = qn.shape[0]

    def body(q_ref, k_ref, v_ref, after_ref, o_ref, lt_ref, z_buf, att_buf):
        i = pl.program_id(1)
        last = 2 * i + 1
        q = q_ref[...]
        after = after_ref[...]

        def rows(kb):
            return pl.ds(pl.multiple_of(kb * PAIR, PAIR), PAIR)

        def block(kb, carry, masked):
            acc, c0, c1 = carry
            z_next = _dot(q, k_ref[jnp.maximum(kb - 1, 0)])
            pv = _dot(att_buf[...], v_ref[rows(jnp.minimum(kb + 1, last)), :])
            z, sp, lnb, mask = _sb_scores(z_buf[...], i * SB_Q, kb * CHUNK, masked)
            loc = _dot_split(lnb, after)
            att = jnp.exp(z - sp + loc + _per_head(c0, c1))
            if masked:
                att = jnp.where(mask, att, 0.0)
            z_buf[...] = z_next
            att_buf[...] = att.astype(BF16)
            return (acc + pv, c0 + loc[:, 0:1] + lnb[:, 0:1], c1 + loc[:, CHUNK:CHUNK + 1] + lnb[:, CHUNK:CHUNK + 1])

        z_buf[...] = _dot(q, k_ref[last])
        att_buf[...] = jnp.zeros_like(att_buf)
        zero = jnp.zeros((SB_Q, 1), F32)
        carry = (jnp.zeros((SB_Q, CHUNK), F32), zero, zero)
        carry = block(last, carry, True)
        carry = block(last - 1, carry, True)
        acc, c0, c1 = lax.fori_loop(0, 2 * i, lambda j, c: block(2 * i - 1 - j, c, False), carry)
        o_ref[...] = acc + _dot(att_buf[...], v_ref[rows(0), :])
        lt_ref[...] = jnp.concatenate([jnp.broadcast_to(c0, (SB_Q, HEAD_DIM)),
                                       jnp.broadcast_to(c1, (SB_Q, HEAD_DIM))], axis=1)

    blk = pl.BlockSpec((SB_Q, CHUNK), lambda p, i: (i, p))
    seq = pl.BlockSpec((2 * s, CHUNK), lambda p, i: (0, p))
    return pl.pallas_call(
        body, name="fwd_sb", grid=(SB_W // CHUNK, s // SB_Q),
        in_specs=[blk, pl.BlockSpec((None, s // CHUNK, CHUNK, PAIR), lambda p, i: (p, 0, 0, 0)), seq,
                  pl.BlockSpec((PAIR, PAIR), lambda p, i: (0, 0))],
        out_specs=[blk, blk],
        out_shape=[jax.ShapeDtypeStruct((s, SB_W), F32)] * 2,
        scratch_shapes=[pltpu.VMEM((SB_Q, PAIR), F32), pltpu.VMEM((SB_Q, PAIR), BF16)],
        compiler_params=_cp("parallel", "parallel"),
    )(qn, ktr, vst, _pair_tri("after"))


def _group_norms(yc, ys, yb):
    return [_rms_stats(yc), _rms_stats(ys), _rms_stats(yb)]


def _fwd_out(yc, ys, yb, g, w, x, tok):
    s = x.shape[0]
    tm = TOKEN_TILE

    def body(yc_ref, ys_ref, yb_ref, g_ref, w_ref, x_ref, tok_ref, o_ref):
        stats = _group_norms(yc_ref[...], ys_ref[...], yb_ref[...])
        cat = jnp.concatenate([xh for _, xh in stats], axis=1) * g_ref[...]
        o_ref[...] = x_ref[...] + _dot(cat.astype(BF16), w_ref[...])

    return pl.pallas_call(
        body, name="fwd_out", grid=(s // tm,),
        in_specs=[_rows(tm, CONV_W), _rows(tm, SG_W), _rows(tm, SB_W), _whole((1, D_MODEL)),
                  _whole((D_MODEL, D_MODEL)), _rows(tm, D_MODEL), ORDER_ONLY],
        out_specs=_rows(tm, D_MODEL),
        out_shape=jax.ShapeDtypeStruct((s, D_MODEL), F32),
        compiler_params=_cp("parallel"),
    )(yc, ys, yb, g, w, x, tok)


def _fwd_ffn(x, g, wgu, wd):
    s = x.shape[0]
    tm = TOKEN_TILE

    def body(x_ref, g_ref, wgu_ref, wd_ref, gu_ref, o_ref):
        x = x_ref[...]
        r, xh = _rms_stats(x)
        gu = _dot_nt((xh * g_ref[...]).astype(BF16), wgu_ref[...])
        gu_ref[...] = gu
        gate = gu[:, :FFN_H]
        act = gate * _sigmoid(gate) * gu[:, FFN_H:]
        o_ref[...] = x + _dot(act.astype(BF16), wd_ref[...])

    return pl.pallas_call(
        body, name="fwd_ffn", grid=(s // tm,),
        in_specs=[_rows(tm, D_MODEL), _whole((1, D_MODEL)),
                  pl.BlockSpec((2 * FFN_H, D_MODEL), lambda i: (0, 0), pipeline_mode=pl.Buffered(1)),
                  pl.BlockSpec((FFN_H, D_MODEL), lambda i: (0, 0), pipeline_mode=pl.Buffered(1))],
        out_specs=[_rows(tm, 2 * FFN_H), _rows(tm, D_MODEL)],
        out_shape=[jax.ShapeDtypeStruct((s, 2 * FFN_H), F32), jax.ShapeDtypeStruct((s, D_MODEL), F32)],
        compiler_params=_cp("parallel"),
    )(x, g, wgu, wd)


def _loss_head(y, target):
    s = y.shape[0]
    tm = TOKEN_TILE

    def body(y_ref, t_ref, l_ref, d_ref):
        @pl.when(pl.program_id(0) == 0)
        def _():
            l_ref[...] = jnp.zeros_like(l_ref)

        err = y_ref[...] - t_ref[...]
        d_ref[...] = err * (1.0 / D_MODEL)
        l_ref[...] += 0.5 * jnp.sum(jnp.mean(err * err, axis=-1, keepdims=True), axis=0, keepdims=True)

    return pl.pallas_call(
        body, name="loss_head", grid=(s // tm,),
        in_specs=[_rows(tm, D_MODEL), _rows(tm, D_MODEL)],
        out_specs=[_whole((1, 1)), _rows(tm, D_MODEL)],
        out_shape=[jax.ShapeDtypeStruct((1, 1), F32), jax.ShapeDtypeStruct((s, D_MODEL), F32)],
        compiler_params=_cp("arbitrary"),
    )(y, target)


def _accumulate(ref, value):
    @pl.when(pl.program_id(0) == 0)
    def _():
        ref[...] = jnp.zeros_like(ref)

    ref[...] += value


def _bwd_ffn(dxo, gu, xm, g, wgu, wd, tok):
    s = dxo.shape[0]
    tm = TOKEN_TILE

    def body(dxo_ref, gu_ref, xm_ref, g_ref, wgu_ref, wd_ref, tok_ref, dgu_ref, act_ref, h_ref, dxm_ref, dg_ref):
        dxo = dxo_ref[...]
        gu = gu_ref[...]
        gate, up = gu[:, :FFN_H], gu[:, FFN_H:]
        sg = _sigmoid(gate)
        sl = gate * sg
        act_ref[...] = (sl * up).astype(BF16)
        dact = _dot_nt(dxo.astype(BF16), wd_ref[...])
        dgate = dact * up * (sg * (1.0 + gate * (1.0 - sg)))
        dgu = jnp.concatenate([dgate, dact * sl], axis=1).astype(BF16)
        dgu_ref[...] = dgu
        dh = _dot(dgu, wgu_ref[...])
        r, xh = _rms_stats(xm_ref[...])
        h_ref[...] = (xh * g_ref[...]).astype(BF16)
        dx, dgrow = _rms_bwd(xh, r, g_ref[...], dh)
        dxm_ref[...] = dxo + dx
        _accumulate(dg_ref, _colsum(dgrow))

    return pl.pallas_call(
        body, name="bwd_ffn", grid=(s // tm,),
        in_specs=[_rows(tm, D_MODEL), _rows(tm, 2 * FFN_H), _rows(tm, D_MODEL), _whole((1, D_MODEL)),
                  pl.BlockSpec((2 * FFN_H, D_MODEL), lambda i: (0, 0), pipeline_mode=pl.Buffered(1)),
                  pl.BlockSpec((FFN_H, D_MODEL), lambda i: (0, 0), pipeline_mode=pl.Buffered(1)), ORDER_ONLY],
        out_specs=[_rows(tm, 2 * FFN_H), _rows(tm, FFN_H), _rows(tm, D_MODEL), _rows(tm, D_MODEL),
                   _whole((1, D_MODEL))],
        out_shape=[jax.ShapeDtypeStruct((s, 2 * FFN_H), BF16), jax.ShapeDtypeStruct((s, FFN_H), BF16),
                   jax.ShapeDtypeStruct((s, D_MODEL), BF16), jax.ShapeDtypeStruct((s, D_MODEL), F32),
                   jax.ShapeDtypeStruct((1, D_MODEL), F32)],
        compiler_params=_cp("arbitrary"),
    )(dxo, gu, xm, g, wgu, wd, tok)


def _matmul_tn(a, b, tm, tn, out_dtype=BF16):
    s, m = a.shape
    n = b.shape[1]

    def body(a_ref, b_ref, o_ref):
        o_ref[...] = _dot_tn(a_ref[...].astype(BF16), b_ref[...].astype(BF16)).astype(out_dtype)

    return pl.pallas_call(
        body, name="weight_grad", grid=(m // tm, n // tn),
        in_specs=[pl.BlockSpec((s, tm), lambda i, j: (0, i)), pl.BlockSpec((s, tn), lambda i, j: (0, j))],
        out_specs=pl.BlockSpec((tm, tn), lambda i, j: (i, j)),
        out_shape=jax.ShapeDtypeStruct((m, n), out_dtype),
        compiler_params=_cp("parallel", "parallel"),
    )(a, b)


def _bwd_out(dxm, yc, ys, yb, g, w, tok):
    s = dxm.shape[0]
    tm = TOKEN_TILE

    def body(dxm_ref, yc_ref, ys_ref, yb_ref, g_ref, w_ref, tok_ref, dyc_ref, dys_ref, dyb_ref, cat_ref, dg_ref):
        stats = _group_norms(yc_ref[...], ys_ref[...], yb_ref[...])
        g = g_ref[...]
        cat_ref[...] = (jnp.concatenate([xh for _, xh in stats], axis=1) * g).astype(BF16)
        dcat = _dot_nt(dxm_ref[...].astype(BF16), w_ref[...])
        dgs = []
        off = 0
        for (r, xh), out in zip(stats, (dyc_ref, dys_ref, dyb_ref)):
            cols = slice(off, off + xh.shape[1])
            dx, dgrow = _rms_bwd(xh, r, g[:, cols], dcat[:, cols])
            out[...] = dx
            dgs.append(_colsum(dgrow))
            off += xh.shape[1]
        _accumulate(dg_ref, jnp.concatenate(dgs, axis=1))

    return pl.pallas_call(
        body, name="bwd_out", grid=(s // tm,),
        in_specs=[_rows(tm, D_MODEL), _rows(tm, CONV_W), _rows(tm, SG_W), _rows(tm, SB_W),
                  _whole((1, D_MODEL)), _whole((D_MODEL, D_MODEL)), ORDER_ONLY],
        out_specs=[_rows(tm, CONV_W), _rows(tm, SG_W), _rows(tm, SB_W), _rows(tm, D_MODEL),
                   _whole((1, D_MODEL))],
        out_shape=[jax.ShapeDtypeStruct((s, CONV_W), F32), jax.ShapeDtypeStruct((s, SG_W), F32),
                   jax.ShapeDtypeStruct((s, SB_W), F32), jax.ShapeDtypeStruct((s, D_MODEL), BF16),
                   jax.ShapeDtypeStruct((1, D_MODEL), F32)],
        compiler_params=_cp("arbitrary"),
    )(dxm, yc, ys, yb, g, w, tok)


def _bwd_sb(qn, kst, ktr, vtr, dy, ltot, tok):
    s = qn.shape[0]

    def body(q_ref, k_ref, kt_ref, vt_ref, do_ref, lt_ref, upto_ref, before_ref, tok_ref, dq_ref, dk_ref, dv_ref,
             z_buf, da_buf, dz_buf, att_buf):
        i = pl.program_id(1)
        last = 2 * i + 1

        @pl.when(i == 0)
        def _():
            dk_ref[...] = jnp.zeros_like(dk_ref)
            dv_ref[...] = jnp.zeros_like(dv_ref)

        q = q_ref[...]
        do = do_ref[...]
        dob = do.astype(BF16)
        q_t = q.astype(F32).T.astype(BF16)
        do_t = do.T.astype(BF16)
        lt = lt_ref[...]
        ltot = _per_head(lt[:, 0:1], lt[:, HEAD_DIM:HEAD_DIM + 1])
        upto = upto_ref[...]
        before = before_ref[...]
        last0, last1 = slice(CHUNK - 1, CHUNK), slice(PAIR - 1, PAIR)

        def rows(kb):
            return pl.ds(pl.multiple_of(kb * PAIR, PAIR), PAIR)

        def ahead(kb):
            return _dot(q, kt_ref[kb]), _dot(dob, vt_ref[kb])

        def behind(kb, dq):
            dzb = dz_buf[...]
            dk_ref[kb] += _dot(q_t, dzb)
            dv_ref[kb] += _dot(do_t, att_buf[...])
            return dq + _dot(dzb, k_ref[rows(kb), :])

        def block(kb, carry, masked):
            dq, p0, p1, e0, e1 = carry
            z_next, da_next = ahead(jnp.minimum(kb + 1, last))
            dq = behind(jnp.maximum(kb - 1, 0), dq)
            z, sp, lnb, mask = _sb_scores(z_buf[...], i * SB_Q, kb * CHUNK, masked)
            pin = _dot_split(lnb, upto) + _per_head(p0, p1)
            sig = jnp.exp(z - sp)
            att = jnp.exp(z - sp + (ltot - pin))
            if masked:
                att = jnp.where(mask, att, 0.0)
            e = att * da_buf[...]
            ebefore = _dot_split(e, before) + _per_head(e0, e1)
            dz = e - sig * (e + ebefore)
            if masked:
                dz = jnp.where(mask, dz, 0.0)
            z_buf[...] = z_next
            da_buf[...] = da_next
            dz_buf[...] = dz.astype(BF16)
            att_buf[...] = att.astype(BF16)
            return (dq, pin[:, last0], pin[:, last1],
                    ebefore[:, last0] + e[:, last0], ebefore[:, last1] + e[:, last1])

        z_buf[...], da_buf[...] = ahead(0)
        dz_buf[...] = jnp.zeros_like(dz_buf)
        att_buf[...] = jnp.zeros_like(att_buf)
        zero = jnp.zeros((SB_Q, 1), F32)
        carry = (jnp.zeros((SB_Q, CHUNK), F32), zero, zero, zero, zero)
        carry = lax.fori_loop(0, 2 * i, lambda kb, c: block(kb, c, False), carry)
        carry = block(last - 1, carry, True)
        carry = block(last, carry, True)
        dq_ref[...] = behind(last, carry[0])

    blk = pl.BlockSpec((SB_Q, CHUNK), lambda p, i: (i, p))
    seq = pl.BlockSpec((2 * s, CHUNK), lambda p, i: (0, p))
    tiles = pl.BlockSpec((None, s // CHUNK, CHUNK, PAIR), lambda p, i: (p, 0, 0, 0))
    tri = pl.BlockSpec((PAIR, PAIR), lambda p, i: (0, 0))
    return pl.pallas_call(
        body, name="bwd_sb", grid=(SB_W // CHUNK, s // SB_Q),
        in_specs=[blk, seq, tiles, tiles, blk, blk, tri, tri, ORDER_ONLY],
        out_specs=[blk, tiles, tiles],
        out_shape=[jax.ShapeDtypeStruct((s, SB_W), F32)]
        + [jax.ShapeDtypeStruct((SB_W // CHUNK, s // CHUNK, CHUNK, PAIR), F32)] * 2,
        scratch_shapes=[pltpu.VMEM((SB_Q, PAIR), F32), pltpu.VMEM((SB_Q, PAIR), F32),
                        pltpu.VMEM((SB_Q, PAIR), BF16), pltpu.VMEM((SB_Q, PAIR), BF16)],
        compiler_params=_cp("parallel", "arbitrary"),
    )(qn, kst, ktr, vtr, dy, ltot, _pair_tri("upto"), _pair_tri("before"), tok)


def _head_sum(row):
    acc = row[:, 0:HEAD_DIM]
    for h in range(1, SB_W // HEAD_DIM):
        acc = acc + row[:, h * HEAD_DIM:(h + 1) * HEAD_DIM]
    return acc


def _bwd_qk(proj, dqs, dkn, dv, qg, kg):
    s = proj.shape[0]
    tm = TOKEN_TILE
    tiles = pl.BlockSpec((SB_W // CHUNK, tm // CHUNK, CHUNK, PAIR), lambda i: (0, i, 0, 0))

    def body(q_ref, k_ref, dqs_ref, dkn_ref, dv_ref, qg_ref, kg_ref, dp_ref, dqg_ref, dkg_ref, qacc, kacc):
        i = pl.program_id(0)
        gm = _group_mean_matrix(SB_W, HEAD_DIM)

        def one(x, dy, g, acc):
            r = lax.rsqrt(_dot_split(x * x, gm) + RMS_EPS)
            xh = x * r
            dxh = dy * g
            _accumulate(acc, _colsum(dy * xh))
            return r * (dxh - xh * _dot_split(dxh * xh, gm))

        dq = one(q_ref[...], dqs_ref[...] * (HEAD_DIM ** -0.5), qg_ref[...], qacc)
        dk = one(k_ref[...], _unstack_heads(_load_transposed(dkn_ref)), kg_ref[...], kacc)
        dp_ref[...] = jnp.concatenate([dq, dk, _unstack_heads(_load_transposed(dv_ref))], axis=1).astype(BF16)

        @pl.when(i == pl.num_programs(0) - 1)
        def _():
            dqg_ref[...] = _head_sum(qacc[...])
            dkg_ref[...] = _head_sum(kacc[...])

    return pl.pallas_call(
        body, name="bwd_qk", grid=(s // tm,),
        in_specs=[_rows(tm, SB_W, OFF_SB // SB_W), _rows(tm, SB_W, OFF_SB // SB_W + 1),
                  _rows(tm, SB_W), tiles, tiles, _whole((1, SB_W)), _whole((1, SB_W))],
        out_specs=[_rows(tm, 3 * SB_W), _whole((1, HEAD_DIM)), _whole((1, HEAD_DIM))],
        out_shape=[jax.ShapeDtypeStruct((s, 3 * SB_W), BF16), jax.ShapeDtypeStruct((1, HEAD_DIM), F32),
                   jax.ShapeDtypeStruct((1, HEAD_DIM), F32)],
        scratch_shapes=[pltpu.VMEM((1, SB_W), F32), pltpu.VMEM((1, SB_W), F32)],
        compiler_params=_cp("arbitrary"),
    )(proj, proj, dqs, dkn, dv, qg, kg)


def _bwd_sg(proj, dy, lg, lb, w, bias):
    s = proj.shape[0]
    tm = TOKEN_TILE
    nh = SG_W // HEAD_DIM

    def body(p_ref, dy_ref, lg_ref, lb_ref, w_ref, bias_ref, dp_ref, dlg_ref, dlb_ref, dw_ref, db_ref, dbias):
        i = pl.program_id(0)
        uv = p_ref[...]
        ge = _gelu(uv)
        u = ge[:, :SG_W]
        r, xh = _ln_stats(ge[:, SG_W:])
        vln = (xh * lg_ref[...] + lb_ref[...]).astype(BF16)
        dy = dy_ref[...]
        tril, lane_head = _sg_masks()

        @pl.when(i == 0)
        def _():
            dw_ref[...] = jnp.zeros_like(dw_ref)
            dbias[...] = jnp.zeros_like(dbias)

        dus, dvlns = [], []
        for c in range(tm // CHUNK):
            rows = slice(c * CHUNK, (c + 1) * CHUNK)
            vc = vln[rows]
            dus.append(dy[rows] * _sg_mix(w_ref, bias_ref, vc, tril, lane_head))
            dm = dy[rows] * u[rows]
            dbias[...] += dm
            dvc = jnp.zeros((CHUNK, SG_W), F32)
            for h in range(nh):
                dmh = jnp.where(lane_head == h, dm, 0.0).astype(BF16)
                dw_ref[h] += jnp.where(tril, _dot_nt(dmh, vc), 0.0)
                wm = jnp.where(tril, w_ref[h], 0.0).astype(BF16)
                dvc = dvc + _dot_tn(wm, dmh)
            dvlns.append(dvc)
        du = jnp.concatenate(dus, axis=0)
        dvln = jnp.concatenate(dvlns, axis=0)
        _accumulate(dlg_ref, _colsum(dvln * xh))
        _accumulate(dlb_ref, _colsum(dvln))
        dv = _ln_bwd(xh, r, lg_ref[...], dvln)
        dp_ref[...] = (jnp.concatenate([du, dv], axis=1) * _gelu_grad(uv)).astype(BF16)

        @pl.when(i == pl.num_programs(0) - 1)
        def _():
            lane = lax.broadcasted_iota(jnp.int32, (CHUNK, CHUNK), 1)
            acc = dbias[...]
            out = jnp.zeros((CHUNK, CHUNK), F32)
            for h in range(nh):
                hs = jnp.sum(acc[:, h * HEAD_DIM:(h + 1) * HEAD_DIM], axis=1, keepdims=True)
                out = out + jnp.where(lane == h, hs, 0.0)
            db_ref[...] = out

    return pl.pallas_call(
        body, name="bwd_sg", grid=(s // tm,),
        in_specs=[_rows(tm, 2 * SG_W, 1), _rows(tm, SG_W), _whole((1, SG_W)), _whole((1, SG_W)),
                  _whole((nh, CHUNK, CHUNK)), _whole((CHUNK, SG_W))],
        out_specs=[_rows(tm, 2 * SG_W), _whole((1, SG_W)), _whole((1, SG_W)), _whole((nh, CHUNK, CHUNK)),
                   _whole((CHUNK, CHUNK))],
        out_shape=[jax.ShapeDtypeStruct((s, 2 * SG_W), BF16), jax.ShapeDtypeStruct((1, SG_W), F32),
                   jax.ShapeDtypeStruct((1, SG_W), F32), jax.ShapeDtypeStruct((nh, CHUNK, CHUNK), F32),
                   jax.ShapeDtypeStruct((CHUNK, CHUNK), F32)],
        scratch_shapes=[pltpu.VMEM((CHUNK, SG_W), F32)],
        compiler_params=_cp("arbitrary"),
    )(proj, dy, lg, lb, w, bias)


def _bwd_conv(proj, dy, w, b, lg, lb):
    s = proj.shape[0]

    def body(p_ref, dy_ref, w_ref, b_ref, lg_ref, lb_ref, dp_ref, dw_ref, db_ref, dlg_ref, dlb_ref,
             abuf, dcbuf):
        _glu_fill(p_ref, abuf, s)
        dcbuf[pl.ds(s, CHUNK), :] = jnp.zeros((CHUNK, CONV_W), F32)
        dw_ref[...] = jnp.zeros_like(dw_ref)

        def chunk(c, carry):
            db, dlg, dlb = carry
            r0 = pl.multiple_of(c * CHUNK, CHUNK)
            acc, win = _conv_window(abuf, r0, w_ref)
            r, xh = _ln_stats(acc + b_ref[...])
            ln = xh * lg_ref[...] + lb_ref[...]
            sg = _sigmoid(ln)
            dl = dy_ref[pl.ds(r0, CHUNK), :] * (sg * (1.0 + ln * (1.0 - sg)))
            dc = _ln_bwd(xh, r, lg_ref[...], dl)
            dcbuf[pl.ds(r0, CHUNK), :] = dc
            for k in range(CONV_K):
                sh = win if k == 0 else pltpu.roll(win, k, axis=0)
                dw_ref[CONV_K - 1 - k:CONV_K - k, :] += _colsum(dc * sh[32:, :])
            return db + _colsum(dc), dlg + _colsum(dl * xh), dlb + _colsum(dl)

        zero = jnp.zeros((1, CONV_W), F32)
        db, dlg, dlb = lax.fori_loop(0, s // CHUNK, chunk, (zero, zero, zero))
        db_ref[...] = db
        dlg_ref[...] = dlg
        dlb_ref[...] = dlb

        def chunk_back(c, carry):
            r0 = pl.multiple_of(c * CHUNK, CHUNK)
            win = dcbuf[pl.ds(r0, CHUNK + 32), :]
            da = jnp.zeros((CHUNK, CONV_W), F32)
            for k in range(CONV_K):
                sh = win if k == 0 else pltpu.roll(win, CHUNK + 32 - k, axis=0)
                da = da + sh[:CHUNK, :] * w_ref[CONV_K - 1 - k:CONV_K - k, :]
            pv = p_ref[pl.ds(r0, CHUNK), :]
            val, sg = pv[:, :CONV_W], _sigmoid(pv[:, CONV_W:])
            dp_ref[pl.ds(r0, CHUNK), :] = jnp.concatenate([da * sg, da * val * sg * (1.0 - sg)], axis=1).astype(BF16)
            return carry

        lax.fori_loop(0, s // CHUNK, chunk_back, 0)

    row = _whole((1, CONV_W))
    return pl.pallas_call(
        body, name="bwd_conv", grid=(1,),
        in_specs=[pl.BlockSpec((s, 2 * CONV_W), lambda i: (0, 0)), _whole((s, CONV_W)),
                  _whole((CONV_K, CONV_W)), row, row, row],
        out_specs=[_whole((s, 2 * CONV_W)), _whole((CONV_K, CONV_W)), row, row, row],
        out_shape=[jax.ShapeDtypeStruct((s, 2 * CONV_W), BF16), jax.ShapeDtypeStruct((CONV_K, CONV_W), F32)]
        + [jax.ShapeDtypeStruct((1, CONV_W), F32)] * 3,
        scratch_shapes=[pltpu.VMEM((s + CHUNK, CONV_W), F32), pltpu.VMEM((s + CHUNK, CONV_W), F32)],
        compiler_params=_cp("arbitrary"),
    )(proj, dy, w, b, lg, lb)


def _bwd_in(dpc, dps, dpb, x, g, w, dxm):
    s = x.shape[0]
    tm = TOKEN_TILE

    def body(dpc_ref, dps_ref, dpb_ref, x_ref, g_ref, w_ref, dxm_ref, dx_ref, h_ref, dp_ref, dg_ref):
        dp = jnp.concatenate([dpc_ref[...], dps_ref[...], dpb_ref[...]], axis=1)
        dp_ref[...] = dp
        dh = _dot(dp, w_ref[...])
        r, xh = _rms_stats(x_ref[...])
        h_ref[...] = (xh * g_ref[...]).astype(BF16)
        dx, dgrow = _rms_bwd(xh, r, g_ref[...], dh)
        dx_ref[...] = dxm_ref[...] + dx
        _accumulate(dg_ref, _colsum(dgrow))

    return pl.pallas_call(
        body, name="bwd_in", grid=(s // tm,),
        in_specs=[_rows(tm, 2 * CONV_W), _rows(tm, 2 * SG_W), _rows(tm, 3 * SB_W), _rows(tm, D_MODEL),
                  _whole((1, D_MODEL)), _whole((IN_W, D_MODEL)), _rows(tm, D_MODEL)],
        out_specs=[_rows(tm, D_MODEL), _rows(tm, D_MODEL), _rows(tm, IN_W), _whole((1, D_MODEL))],
        out_shape=[jax.ShapeDtypeStruct((s, D_MODEL), F32), jax.ShapeDtypeStruct((s, D_MODEL), BF16),
                   jax.ShapeDtypeStruct((s, IN_W), BF16), jax.ShapeDtypeStruct((1, D_MODEL), F32)],
        compiler_params=_cp("arbitrary"),
    )(dpc, dps, dpb, x, g, w, dxm)


SMALL = ("mix_norm_g", "conv_w", "conv_b", "conv_ln_g", "conv_ln_b", "sg_ln_g", "sg_ln_b", "sg_w", "sg_b",
         "q_norm_g", "k_norm_g", "out_norm_g", "ffn_norm_g")
LARGE = ("w_in", "w_out", "w_gate_up", "w_down")


def _row(v):
    return v.reshape(1, -1)


def _layer_params(p, large, l):
    q = {k: v[l] for k, v in p.items()}
    return dict(
        q, **large,
        mix_norm_g=_row(q["mix_norm_g"]), conv_b=_row(q["conv_b"]), conv_ln_g=_row(q["conv_ln_g"]),
        conv_ln_b=_row(q["conv_ln_b"]), sg_ln_g=_row(q["sg_ln_g"]), sg_ln_b=_row(q["sg_ln_b"]),
        out_norm_g=_row(q["out_norm_g"]), ffn_norm_g=_row(q["ffn_norm_g"]),
        qg=_row(jnp.tile(q["q_norm_g"], SB_W // HEAD_DIM)), kg=_row(jnp.tile(q["k_norm_g"], SB_W // HEAD_DIM)),
        sg_bias=jnp.repeat(q["sg_b"].T, HEAD_DIM, axis=1),
    )


def _layer_fwd(x, q, tok, after_in, after_mixers):
    proj, qn, kn, vb, kt, vt = _fwd_in(x, q["mix_norm_g"], q["w_in"], q["qg"], q["kg"], tok)
    yc = _fwd_conv(proj, q["conv_w"], q["conv_b"], q["conv_ln_g"], q["conv_ln_b"], after_in(proj))
    ys = _fwd_sg(proj, q["sg_ln_g"], q["sg_ln_b"], q["sg_w"], q["sg_bias"])
    yb, lt = _fwd_sb(qn, kt, vb)
    rest, tok = after_mixers(yb)
    q = dict(q, **rest)
    xm = _fwd_out(yc, ys, yb, q["out_norm_g"], q["w_out"], x, tok)
    gu, xo = _fwd_ffn(xm, q["ffn_norm_g"], q["w_gate_up"], q["w_down"])
    return xo, q, dict(x=x, proj=proj, qn=qn, kn=kn, kt=kt, vt=vt, lt=lt, yc=yc, ys=ys, yb=yb, xm=xm, gu=gu)


def _layer_bwd_ffn(dxo, q, st, tok):
    dgu, act, h2, dxm, d_ffn_g = _bwd_ffn(dxo, st["gu"], st["xm"], q["ffn_norm_g"], q["w_gate_up"], q["w_down"],
                                          tok)
    d_wgu, d_wd = _matmul_tn(dgu, h2, 512, D_MODEL), _matmul_tn(act, dxo, FFN_H // 2, D_MODEL)
    dyc, dys, dyb, cat, d_out_g = _bwd_out(dxm, st["yc"], st["ys"], st["yb"], q["out_norm_g"], q["w_out"], tok)
    return dxm, (dyc, dys, dyb, d_ffn_g, d_out_g), d_wgu, d_wd, _matmul_tn(cat, dxm, 512, D_MODEL)


def _layer_bwd_mix(dxm, carried, q, st, tok):
    dyc, dys, dyb, d_ffn_g, d_out_g = carried
    dqs, dkn, dv = _bwd_sb(st["qn"], st["kn"], st["kt"], st["vt"], dyb, st["lt"], tok)
    dpb, d_qg, d_kg = _bwd_qk(st["proj"], dqs, dkn, dv, q["qg"], q["kg"])
    dps, d_sg_lg, d_sg_lb, d_sg_w, d_sg_b = _bwd_sg(st["proj"], dys, q["sg_ln_g"], q["sg_ln_b"], q["sg_w"],
                                                    q["sg_bias"])
    dpc, d_conv_w, d_conv_b, d_conv_lg, d_conv_lb = _bwd_conv(st["proj"], dyc, q["conv_w"], q["conv_b"],
                                                              q["conv_ln_g"], q["conv_ln_b"])
    dx, h1, dp, d_mix_g = _bwd_in(dpc, dps, dpb, st["x"], q["mix_norm_g"], q["w_in"], dxm)
    d_win = _matmul_tn(dp, h1, 512, D_MODEL)
    small = dict(
        mix_norm_g=d_mix_g[0], conv_w=d_conv_w, conv_b=d_conv_b[0], conv_ln_g=d_conv_lg[0],
        conv_ln_b=d_conv_lb[0], sg_ln_g=d_sg_lg[0], sg_ln_b=d_sg_lb[0], sg_w=d_sg_w,
        sg_b=d_sg_b[:, :SG_W // HEAD_DIM].T, q_norm_g=d_qg[0], k_norm_g=d_kg[0], out_norm_g=d_out_g[0],
        ffn_norm_g=d_ffn_g[0])
    return dx, d_win, small


def _position():
    x, y, c = lax.axis_index("x"), lax.axis_index("y"), lax.axis_index("c")
    return x, y, c


def _flat(px, py, pc):
    return 4 * px + 2 * py + pc


IN_HBM = pl.BlockSpec(memory_space=pltpu.HBM)
IN_SEM = pl.BlockSpec(memory_space=pltpu.SEMAPHORE)
EFFECT = pltpu.SideEffectType.DATAFLOW_SIDE_EFFECTING
COPIES = dict(scatter=7, spread=7, spread_chips=4, **{"pass": 3})


def _exchange_copies(kinds, src_refs, land_refs, send_sems, recv_sems, layer, arrival):
    x, y, c = _position()
    me = _flat(x, y, c)
    everyone = [(x ^ (k >> 2 & 1), y ^ (k >> 1 & 1), c ^ (k & 1)) for k in range(1, N_DEV)]
    sibling = (x, y, 1 - c)
    chips = [(1 - x, y, c), (x, 1 - y, c), (1 - x, 1 - y, c)]
    out = []
    srcs = iter(src_refs)
    for kind, land in zip(kinds, land_refs):
        land = land if layer is None else land.at[layer]
        if kind == "scatter":
            src = next(srcs)
            moves = [(src.at[_flat(*p)], me, _flat(*p), p) for p in everyone]
        elif kind in ("spread", "spread_chips"):
            src = next(srcs)
            moves = [(src, me, _flat(*p), p) for p in (everyone if kind == "spread" else [sibling] + chips)]
        else:
            moves = [(land.at[_flat(*p)], _flat(*p), _flat(p[0], p[1], 1 - c), sibling) for p in chips]
        for src_block, there, here, peer in moves:
            n = len(out)
            out.append(pltpu.make_async_remote_copy(
                src_ref=src_block, dst_ref=land.at[here if arrival else there], send_sem=send_sems.at[n],
                recv_sem=recv_sems.at[n], device_id=peer, device_id_type=MESH))
    return out


def _exchange_start(kinds, srcs, lands, after, name, layer=None):
    ns, n = len(srcs), len(srcs) + len(lands)
    sems = sum(COPIES[k] for k in kinds)

    def body(*refs):
        send_sems, recv_sems = refs[n + 1], refs[n + 2]
        for cp in _exchange_copies(kinds, refs[:ns], refs[ns:n], send_sems, recv_sems, layer, arrival=False):
            cp.start()
        refs[-1][...] = jnp.zeros_like(refs[-1])

    thru = [pltpu.HBM(a.shape, a.dtype) for a in (*srcs, *lands)]
    outs = pl.pallas_call(
        body, name=name,
        out_shape=(pltpu.SemaphoreType.DMA((sems,)), pltpu.SemaphoreType.DMA((sems,)), *thru,
                   jax.ShapeDtypeStruct((8, 128), F32)),
        in_specs=[IN_HBM] * n + [ORDER_ONLY],
        out_specs=(IN_SEM, IN_SEM, *[IN_HBM] * n, pl.BlockSpec(memory_space=pltpu.VMEM)),
        input_output_aliases={i: 2 + i for i in range(n)},
        compiler_params=pltpu.CompilerParams(has_side_effects=EFFECT),
    )(*[pltpu.with_memory_space_constraint(a, pltpu.HBM) for a in (*srcs, *lands)], after)
    return kinds, outs[0], outs[1], list(outs[2:2 + ns]), list(outs[2 + ns:2 + n]), outs[-1]


def _exchange_wait(pending, after, name, layer=None):
    kinds, send_sems, recv_sems, srcs, lands, _ = pending
    ns, n = len(srcs), len(srcs) + len(lands)

    def body(*refs):
        for cp in _exchange_copies(kinds, refs[:ns], refs[ns:n], refs[n], refs[n + 1], layer, arrival=True):
            cp.wait_send()
            cp.wait_recv()

    thru = [pltpu.HBM(a.shape, a.dtype) for a in (*srcs, *lands)]
    outs = pl.pallas_call(
        body, name=name, out_shape=tuple(thru),
        in_specs=[IN_HBM] * n + [IN_SEM, IN_SEM, ORDER_ONLY],
        out_specs=tuple([IN_HBM] * n),
        input_output_aliases={i: i for i in range(n)},
        compiler_params=pltpu.CompilerParams(has_side_effects=EFFECT),
    )(*srcs, *lands, send_sems, recv_sems, after)
    return list(outs[ns:])


def _landing(block, me):
    land = lax.empty((N_DEV,) + block.shape, block.dtype)
    return lax.dynamic_update_index_in_dim(land, block, me, 0)


def _adamw(parts, w, m, v, tr):
    groups, rows, cols = w.shape

    def body(p_ref, w_ref, m_ref, v_ref, g_ref, d_ref, nm_ref, nv_ref):
        g = p_ref[0].astype(F32)
        for j in range(1, N_DEV):
            g = g + p_ref[j].astype(F32)
        g_ref[...] = g
        m = ADAM_B1 * m_ref[...] + (1.0 - ADAM_B1) * g
        v = ADAM_B2 * v_ref[...] + (1.0 - ADAM_B2) * (g * g)
        nm_ref[...] = m
        nv_ref[...] = v
        m_hat = m / (1.0 - ADAM_B1 ** ADAM_STEP)
        v_hat = v / (1.0 - ADAM_B2 ** ADAM_STEP)
        d_ref[...] = -ADAM_LR * (m_hat / (jnp.sqrt(v_hat) + ADAM_EPS) + ADAM_WD * w_ref[...])

    blk = pl.BlockSpec((None, tr, cols), lambda g, i: (g, i, 0))
    return pl.pallas_call(
        body, name="adamw", grid=(groups, rows // tr),
        in_specs=[pl.BlockSpec((None, N_DEV, tr, cols), lambda g, i: (g, 0, i, 0)), blk, blk, blk],
        out_specs=[blk] * 4,
        out_shape=[jax.ShapeDtypeStruct((groups, rows, cols), F32)] * 4,
        compiler_params=_cp("parallel", "parallel"),
    )(parts, w, m, v)


def _row_tile(rows):
    for cand in range(min(rows, 512) // 8 * 8, 7, -8):
        if rows % cand == 0:
            return cand
    return rows


def _with_own_block(land, blocks, layer, me):
    own = lax.dynamic_index_in_dim(blocks, me, 0, keepdims=True)[None]
    return lax.dynamic_update_slice(land, own, (layer, me, 0, 0))


PACK_LANES = 128


def _pack_layers(arrs):
    parts = []
    for a in arrs:
        flat = a.reshape(a.shape[0], -1)
        parts.append(jnp.pad(flat, ((0, 0), (0, -flat.shape[1] % (8 * PACK_LANES)))))
    return jnp.concatenate(parts, axis=1).reshape(arrs[0].shape[0], -1, PACK_LANES)


def _unpack_layers(packed, shapes):
    flat = packed.reshape(packed.shape[0], -1)
    outs, off = [], 0
    for shp in shapes:
        size = 1
        for d in shp[1:]:
            size *= d
        outs.append(flat[:, off:off + size].reshape(shp))
        off += size + (-size % (8 * PACK_LANES))
    return outs


def kernel(x, mix_norm_g, w_in, conv_w, conv_b, conv_ln_g, conv_ln_b, sg_ln_g, sg_ln_b, sg_w, sg_b, q_norm_g, k_norm_g, out_norm_g, w_out, ffn_norm_g, w_gate_up, w_down, loss_target, m_mix_norm_g, m_w_in, m_conv_w, m_conv_b, m_conv_ln_g, m_conv_ln_b, m_sg_ln_g, m_sg_ln_b, m_sg_w, m_sg_b, m_q_norm_g, m_k_norm_g, m_out_norm_g, m_w_out, m_ffn_norm_g, m_w_gate_up, m_w_down, v_mix_norm_g, v_w_in, v_conv_w, v_conv_b, v_conv_ln_g, v_conv_ln_b, v_sg_ln_g, v_sg_ln_b, v_sg_w, v_sg_b, v_q_norm_g, v_k_norm_g, v_out_norm_g, v_w_out, v_ffn_norm_g, v_w_gate_up, v_w_down):
    names = SMALL[:1] + LARGE[:1] + SMALL[1:12] + LARGE[1:2] + SMALL[12:] + LARGE[2:]
    w = dict(mix_norm_g=mix_norm_g, w_in=w_in, conv_w=conv_w, conv_b=conv_b, conv_ln_g=conv_ln_g,
             conv_ln_b=conv_ln_b, sg_ln_g=sg_ln_g, sg_ln_b=sg_ln_b, sg_w=sg_w, sg_b=sg_b, q_norm_g=q_norm_g,
             k_norm_g=k_norm_g, out_norm_g=out_norm_g, w_out=w_out, ffn_norm_g=ffn_norm_g,
             w_gate_up=w_gate_up, w_down=w_down)
    m = dict(mix_norm_g=m_mix_norm_g, w_in=m_w_in, conv_w=m_conv_w, conv_b=m_conv_b, conv_ln_g=m_conv_ln_g,
             conv_ln_b=m_conv_ln_b, sg_ln_g=m_sg_ln_g, sg_ln_b=m_sg_ln_b, sg_w=m_sg_w, sg_b=m_sg_b,
             q_norm_g=m_q_norm_g, k_norm_g=m_k_norm_g, out_norm_g=m_out_norm_g, w_out=m_w_out,
             ffn_norm_g=m_ffn_norm_g, w_gate_up=m_w_gate_up, w_down=m_w_down)
    v = dict(mix_norm_g=v_mix_norm_g, w_in=v_w_in, conv_w=v_conv_w, conv_b=v_conv_b, conv_ln_g=v_conv_ln_g,
             conv_ln_b=v_conv_ln_b, sg_ln_g=v_sg_ln_g, sg_ln_b=v_sg_ln_b, sg_w=v_sg_w, sg_b=v_sg_b,
             q_norm_g=v_q_norm_g, k_norm_g=v_k_norm_g, out_norm_g=v_out_norm_g, w_out=v_w_out,
             ffn_norm_g=v_ffn_norm_g, w_gate_up=v_w_gate_up, w_down=v_w_down)
    xpos, ypos, cpos = _position()
    me = _flat(xpos, ypos, cpos)
    conv_cols = conv_w.shape[-1]
    no_token = jnp.zeros((8, 128), F32)
    w, m, v = (dict(t, w_in=jnp.swapaxes(t["w_in"], 1, 2), w_gate_up=jnp.swapaxes(t["w_gate_up"], 1, 2))
               for t in (w, m, v))
    shards = {k: w[k].astype(BF16) for k in LARGE}
    full_shape = dict(w_in=(IN_W, D_MODEL), w_out=(D_MODEL, D_MODEL), w_gate_up=(2 * FFN_H, D_MODEL),
                      w_down=(FFN_H, D_MODEL))

    def gather_start(srcs, after, tag):
        return _exchange_start(["spread_chips"] * len(srcs), srcs, [_landing(a, me) for a in srcs], after,
                               f"gather_start_{tag}")

    def gather_pass(pending, after, tag):
        lands = _exchange_wait(pending, after, f"gather_wait_{tag}")
        return _exchange_start(["pass"] * len(lands), [], lands, lands[0], f"gather_pass_{tag}")

    def gathered(pending, keys, after, tag):
        lands = _exchange_wait(pending, after, f"gather_passed_{tag}")
        return {k: a.reshape(full_shape[k]) for k, a in zip(keys, lands)}, lands[len(keys):]

    first, later = ("w_in",), ("w_out", "w_gate_up", "w_down")
    act = x[0]
    head = gather_start([shards["w_in"][0], w["conv_w"]], act, "0")
    tail = gather_start([shards[k][0] for k in later], head[5], "0_later")
    head = gather_pass(head, tail[5], "0")
    large, (conv_blocks,) = gathered(head, first, head[5], "0")
    conv_full = jnp.transpose(conv_blocks, (1, 2, 0, 3)).reshape(DEPTH, CONV_K, CONV_W)
    small_w = dict({k: w[k] for k in SMALL}, conv_w=conv_full)
    qs, stash = [], []
    for l in range(DEPTH):
        coming = {}

        def after_in(proj):
            if l + 1 == DEPTH:
                return no_token
            coming["first"] = gather_start([shards[k][l + 1] for k in LARGE], proj, str(l + 1))
            return coming["first"][5]

        def after_mixers(y_sb):
            rest, token = {}, no_token
            if l == 0:
                passing = gather_pass(tail, y_sb, "0_later")
                rest, _ = gathered(passing, later, passing[5], "0_later")
            if l + 1 < DEPTH:
                coming["second"] = gather_pass(coming["first"], y_sb, str(l + 1))
                token = coming["second"][5]
            return rest, token

        act, q, st = _layer_fwd(act, _layer_params(small_w, large, l), no_token, after_in, after_mixers)
        qs.append(q)
        stash.append(st)
        if l + 1 < DEPTH:
            large, _ = gathered(coming["second"], LARGE, act, str(l + 1))

    loss, dx = _loss_head(act, loss_target[0])
    loss = lax.psum(loss[0, 0], ("x", "y", "c"))

    replicated = tuple(k for k in SMALL if k != "conv_w")
    small_rows = _pack_layers([w[k][:1] for k in replicated]).shape[1]
    conv_rows = _pack_layers([conv_full[:1]]).shape[1]
    group_a, group_b = ("w_gate_up", "w_down", "w_out"), ("w_in",)
    blocks = lambda k, a: a.reshape((N_DEV,) + w[k].shape[1:])
    land_a = [lax.empty((DEPTH, N_DEV) + w[k].shape[1:], BF16) for k in group_a]
    land_b = [lax.empty((DEPTH, N_DEV) + w[k].shape[1:], BF16) for k in group_b]
    land_b.append(lax.empty((DEPTH, N_DEV, small_rows + conv_rows, PACK_LANES), F32))
    pend_a = pend_b = None
    token = no_token
    for l in reversed(range(DEPTH)):
        dxm, carried, d_wgu, d_wd, d_wo = _layer_bwd_ffn(dx, qs[l], stash[l], token)
        srcs = [blocks(k, a) for k, a in zip(group_a, (d_wgu, d_wd, d_wo))]
        if pend_a is not None:
            land_a = _exchange_wait(pend_a, d_wo, f"grads_a_wait_{l + 1}", layer=l + 1)
        land_a = [_with_own_block(ld, a, l, me) for ld, a in zip(land_a, srcs)]
        pend_a = _exchange_start(["scatter"] * 3, srcs, land_a, d_wo, f"grads_a_start_{l}", layer=l)
        dx, d_win, small = _layer_bwd_mix(dxm, carried, qs[l], stash[l], pend_a[5])
        packed = _pack_layers([small[k][None] for k in replicated + ("conv_w",)])[0]
        srcs = [blocks("w_in", d_win)]
        if pend_b is not None:
            land_b = _exchange_wait(pend_b, d_win, f"grads_b_wait_{l + 1}", layer=l + 1)
        land_b = [_with_own_block(land_b[0], srcs[0], l, me),
                  lax.dynamic_update_slice(land_b[1], packed[None, None], (l, me, 0, 0))]
        pend_b = _exchange_start(["scatter", "spread"], srcs + [packed], land_b, d_win, f"grads_b_start_{l}",
                                 layer=l)
        token = pend_b[5]

    land_a = _exchange_wait(pend_a, token, "grads_a_wait_0", layer=0)
    res = {}
    for k, parts in zip(group_a, land_a):
        res[k] = _adamw(parts, w[k], m[k], v[k], _row_tile(w[k].shape[1]))
    land_b = _exchange_wait(pend_b, res["w_out"][0], "grads_b_wait_0", layer=0)
    res["w_in"] = _adamw(land_b[0], w["w_in"], m["w_in"], v["w_in"], _row_tile(w["w_in"].shape[1]))
    small_parts = land_b[1]
    updated = _adamw(small_parts, *(_pack_layers([t[k] for k in replicated]) for t in (w, m, v)), small_rows)
    unpacked = [_unpack_layers(o, [w[k].shape for k in replicated]) for o in updated]
    res.update({k: [u[i] for u in unpacked] for i, k in enumerate(replicated)})
    conv_parts = small_parts[:, :, small_rows:].reshape(DEPTH, N_DEV, -1)[:, :, :CONV_K * CONV_W]
    conv_parts = lax.dynamic_slice_in_dim(conv_parts.reshape(DEPTH, N_DEV, CONV_K, CONV_W), me * conv_cols,
                                          conv_cols, axis=3)
    res["conv_w"] = _adamw(conv_parts, w["conv_w"], m["conv_w"], v["conv_w"], CONV_K)
    for k in ("w_in", "w_gate_up"):
        res[k] = [jnp.swapaxes(a, 1, 2) for a in res[k]]

    return (loss, dx[None], *[res[k][0] for k in names], *[res[k][1] for k in names],
            *[res[k][2] for k in names], *[res[k][3] for k in names])
```

```python
import functools

import jax
import jax.numpy as jnp
from jax import lax
from jax.experimental import pallas as pl
from jax.experimental.pallas import tpu as pltpu

F32 = jnp.float32
BF16 = jnp.bfloat16

D_MODEL = 1024
DEPTH = 4
HEAD_DIM = 64
CONV_W = 256
SG_W = 256
SB_W = 512
IN_W = 2560
FFN_H = 2816
CONV_K = 31
CHUNK = 128
OFF_SG = 2 * CONV_W
OFF_SB = OFF_SG + 2 * SG_W
RMS_EPS = 1e-6
LN_EPS = 1e-5
N_DEV = 8
MESH = pl.DeviceIdType.MESH

ADAM_LR = 0.001
ADAM_B1 = 0.9
ADAM_B2 = 0.999
ADAM_EPS = 1e-08
ADAM_WD = 0.01
ADAM_STEP = 10

TOKEN_TILE = 256
VMEM_LIMIT = 56 * 1024 * 1024


def _cp(*sem):
    return pltpu.CompilerParams(dimension_semantics=sem or None, vmem_limit_bytes=VMEM_LIMIT)


def _dot(a, b):
    return jnp.dot(a, b, preferred_element_type=F32)


def _dot_nt(a, b):
    return lax.dot_general(a, b, (((1,), (1,)), ((), ())), preferred_element_type=F32)


def _dot_tn(a, b):
    return lax.dot_general(a, b, (((0,), (0,)), ((), ())), preferred_element_type=F32)


def _dot_split(x, m):
    hi = x.astype(BF16)
    lo = (x - hi.astype(F32)).astype(BF16)
    return _dot(hi, m) + _dot(lo, m)


def _group_mean_matrix(width, group):
    r = lax.broadcasted_iota(jnp.int32, (width, width), 0) // group
    c = lax.broadcasted_iota(jnp.int32, (width, width), 1) // group
    return jnp.where(r == c, 1.0 / group, 0.0).astype(BF16)


def _sigmoid(x):
    return 1.0 / (1.0 + jnp.exp(-x))


def _gelu(x):
    return 0.5 * x * (1.0 + lax.erf(x * (2.0 ** -0.5)))


def _gelu_grad(x):
    return 0.5 * (1.0 + lax.erf(x * (2.0 ** -0.5))) + x * jnp.exp(-0.5 * x * x) * (0.5 * (2.0 / jnp.pi) ** 0.5)


def _rms_stats(x):
    r = lax.rsqrt(jnp.mean(x * x, axis=-1, keepdims=True) + RMS_EPS)
    return r, x * r


def _rms_bwd(xh, r, g, dy):
    dxh = dy * g
    dx = r * (dxh - xh * jnp.mean(dxh * xh, axis=-1, keepdims=True))
    return dx, dy * xh


def _ln_stats(x):
    mu = jnp.mean(x, axis=-1, keepdims=True)
    xc = x - mu
    r = lax.rsqrt(jnp.mean(xc * xc, axis=-1, keepdims=True) + LN_EPS)
    return r, xc * r


def _ln_bwd(xh, r, g, dy):
    dxh = dy * g
    return r * (dxh - jnp.mean(dxh, axis=-1, keepdims=True) - xh * jnp.mean(dxh * xh, axis=-1, keepdims=True))


def _colsum(x):
    return jnp.sum(x, axis=0, keepdims=True)


def _rows(tm, n, j=0):
    return pl.BlockSpec((tm, n), lambda i: (i, j))


def _whole(shape):
    return pl.BlockSpec(shape, lambda i: (0,) * len(shape))


ORDER_ONLY = pl.BlockSpec(memory_space=pl.ANY)


def _stack_heads(a):
    even = (lax.broadcasted_iota(jnp.int32, a.shape, 1) % (2 * HEAD_DIM)) < HEAD_DIM
    top = jnp.where(even, a, 0.0)
    bot = jnp.where(even, 0.0, a)
    parts = []
    for c in range(a.shape[0] // CHUNK):
        rows = slice(c * CHUNK, (c + 1) * CHUNK)
        parts += [top[rows], bot[rows]]
    return jnp.concatenate(parts, axis=0)


def _store_stacked(st, st_ref, tr_ref):
    st_ref[...] = st.astype(BF16)
    for p in range(SB_W // CHUNK):
        for c in range(st.shape[0] // (2 * CHUNK)):
            tile = st[2 * c * CHUNK:2 * (c + 1) * CHUNK, p * CHUNK:(p + 1) * CHUNK]
            tr_ref[p, c] = tile.T.astype(BF16)


def _load_transposed(tr_ref):
    rows = []
    for c in range(tr_ref.shape[1]):
        tiles = [tr_ref[p, c].T for p in range(SB_W // CHUNK)]
        rows.append(jnp.concatenate(tiles, axis=1))
    return jnp.concatenate(rows, axis=0)


def _unstack_heads(st):
    even = (lax.broadcasted_iota(jnp.int32, (CHUNK, st.shape[1]), 1) % (2 * HEAD_DIM)) < HEAD_DIM
    parts = []
    for c in range(st.shape[0] // (2 * CHUNK)):
        top = st[2 * c * CHUNK:(2 * c + 1) * CHUNK]
        bot = st[(2 * c + 1) * CHUNK:(2 * c + 2) * CHUNK]
        parts.append(jnp.where(even, top, bot))
    return jnp.concatenate(parts, axis=0)


def _fwd_in(x, g, w, qg, kg, tok):
    s = x.shape[0]
    tm = TOKEN_TILE

    def body(x_ref, g_ref, w_ref, qg_ref, kg_ref, tok_ref, proj_ref, qn_ref, kn_ref, vb_ref, kt_ref, vt_ref):
        r, xh = _rms_stats(x_ref[...])
        h = (xh * g_ref[...]).astype(BF16)
        proj = _dot_nt(h, w_ref[...])
        proj_ref[...] = proj
        gm = _group_mean_matrix(SB_W, HEAD_DIM)
        q = proj[:, OFF_SB:OFF_SB + SB_W]
        k = proj[:, OFF_SB + SB_W:OFF_SB + 2 * SB_W]
        rq = lax.rsqrt(_dot_split(q * q, gm) + RMS_EPS)
        rk = lax.rsqrt(_dot_split(k * k, gm) + RMS_EPS)
        qn_ref[...] = (q * rq * qg_ref[...] * (HEAD_DIM ** -0.5)).astype(BF16)
        _store_stacked(_stack_heads(k * rk * kg_ref[...]), kn_ref, kt_ref)
        _store_stacked(_stack_heads(proj[:, OFF_SB + 2 * SB_W:]), vb_ref, vt_ref)

    tiles = pl.BlockSpec((SB_W // CHUNK, tm // CHUNK, CHUNK, PAIR), lambda i: (0, i, 0, 0))
    tiles_shape = jax.ShapeDtypeStruct((SB_W // CHUNK, s // CHUNK, CHUNK, PAIR), BF16)
    return pl.pallas_call(
        body, name="fwd_in", grid=(s // tm,),
        in_specs=[_rows(tm, D_MODEL), _whole((1, D_MODEL)), _whole((IN_W, D_MODEL)),
                  _whole((1, SB_W)), _whole((1, SB_W)), ORDER_ONLY],
        out_specs=[_rows(tm, IN_W), _rows(tm, SB_W), _rows(2 * tm, SB_W), _rows(2 * tm, SB_W), tiles, tiles],
        out_shape=[jax.ShapeDtypeStruct((s, IN_W), F32), jax.ShapeDtypeStruct((s, SB_W), BF16),
                   jax.ShapeDtypeStruct((2 * s, SB_W), BF16), jax.ShapeDtypeStruct((2 * s, SB_W), BF16),
                   tiles_shape, tiles_shape],
        compiler_params=_cp("parallel"),
    )(x, g, w, qg, kg, tok)


def _conv_window(abuf, r0, w_ref):
    win = abuf[pl.ds(pl.multiple_of(r0 + CHUNK - 32, 32), CHUNK + 32), :]
    acc = jnp.zeros((CHUNK, CONV_W), F32)
    for k in range(CONV_K):
        sh = win if k == 0 else pltpu.roll(win, k, axis=0)
        acc = acc + sh[32:, :] * w_ref[CONV_K - 1 - k:CONV_K - k, :]
    return acc, win


def _glu_fill(p_ref, abuf, s):
    abuf[0:CHUNK, :] = jnp.zeros((CHUNK, CONV_W), F32)

    def fill(c, carry):
        r0 = pl.multiple_of(c * CHUNK, CHUNK)
        pv = p_ref[pl.ds(r0, CHUNK), :]
        abuf[pl.ds(r0 + CHUNK, CHUNK), :] = pv[:, :CONV_W] * _sigmoid(pv[:, CONV_W:])
        return carry

    lax.fori_loop(0, s // CHUNK, fill, 0)


def _fwd_conv(proj, w, b, lg, lb, tok):
    s = proj.shape[0]

    def body(p_ref, w_ref, b_ref, lg_ref, lb_ref, tok_ref, y_ref, abuf):
        _glu_fill(p_ref, abuf, s)

        def chunk(c, carry):
            r0 = pl.multiple_of(c * CHUNK, CHUNK)
            acc, _ = _conv_window(abuf, r0, w_ref)
            r, xh = _ln_stats(acc + b_ref[...])
            ln = xh * lg_ref[...] + lb_ref[...]
            y_ref[pl.ds(r0, CHUNK), :] = ln * _sigmoid(ln)
            return carry

        lax.fori_loop(0, s // CHUNK, chunk, 0)

    return pl.pallas_call(
        body, name="fwd_conv", grid=(1,),
        in_specs=[pl.BlockSpec((s, 2 * CONV_W), lambda i: (0, 0)), _whole((CONV_K, CONV_W)),
                  _whole((1, CONV_W)), _whole((1, CONV_W)), _whole((1, CONV_W)), ORDER_ONLY],
        out_specs=_whole((s, CONV_W)),
        out_shape=jax.ShapeDtypeStruct((s, CONV_W), F32),
        scratch_shapes=[pltpu.VMEM((s + CHUNK, CONV_W), F32)],
        compiler_params=_cp("arbitrary"),
    )(proj, w, b, lg, lb, tok)


def _sg_masks():
    row = lax.broadcasted_iota(jnp.int32, (CHUNK, CHUNK), 0)
    col = lax.broadcasted_iota(jnp.int32, (CHUNK, CHUNK), 1)
    lane_head = lax.broadcasted_iota(jnp.int32, (CHUNK, SG_W), 1) // HEAD_DIM
    return row >= col, lane_head


def _sg_mix(w_ref, bias_ref, vc, tril, lane_head):
    mixed = bias_ref[...]
    for h in range(SG_W // HEAD_DIM):
        wm = jnp.where(tril, w_ref[h], 0.0).astype(BF16)
        mixed = mixed + jnp.where(lane_head == h, _dot(wm, vc), 0.0)
    return mixed


def _fwd_sg(proj, lg, lb, w, bias):
    s = proj.shape[0]
    tm = TOKEN_TILE

    def body(p_ref, lg_ref, lb_ref, w_ref, bias_ref, y_ref):
        ge = _gelu(p_ref[...])
        u = ge[:, :SG_W]
        r, xh = _ln_stats(ge[:, SG_W:])
        vln = (xh * lg_ref[...] + lb_ref[...]).astype(BF16)
        tril, lane_head = _sg_masks()
        for c in range(tm // CHUNK):
            rows = slice(c * CHUNK, (c + 1) * CHUNK)
            y_ref[rows, :] = u[rows] * _sg_mix(w_ref, bias_ref, vln[rows], tril, lane_head)

    return pl.pallas_call(
        body, name="fwd_sg", grid=(s // tm,),
        in_specs=[_rows(tm, 2 * SG_W, 1), _whole((1, SG_W)), _whole((1, SG_W)),
                  _whole((SG_W // HEAD_DIM, CHUNK, CHUNK)), _whole((CHUNK, SG_W))],
        out_specs=_rows(tm, SG_W),
        out_shape=jax.ShapeDtypeStruct((s, SG_W), F32),
        compiler_params=_cp("parallel"),
    )(proj, lg, lb, w, bias)


SB_Q = 2 * CHUNK
PAIR = 2 * CHUNK
SB_PAIRS = 2


def _pair_tri(kind):
    row = lax.broadcasted_iota(jnp.int32, (PAIR, PAIR), 0)
    col = lax.broadcasted_iota(jnp.int32, (PAIR, PAIR), 1)
    tri = {"after": row > col, "upto": row <= col, "before": row < col}[kind]
    return jnp.where(((row // CHUNK) == (col // CHUNK)) & tri, 1.0, 0.0).astype(BF16)


def _sb_scores(z, qpos0, kpos0, masked):
    sp = jnp.maximum(z, 0.0) + jnp.log(1.0 + jnp.exp(-jnp.abs(z)))
    if not masked:
        return z, sp, sp.astype(BF16), None
    row = lax.broadcasted_iota(jnp.int32, z.shape, 0)
    col = lax.broadcasted_iota(jnp.int32, z.shape, 1) % CHUNK
    mask = (kpos0 + col) < (qpos0 + row)
    return z, sp, jnp.where(mask, sp, 0.0).astype(BF16), mask


def _per_head(c0, c1):
    return jnp.concatenate([jnp.broadcast_to(c0, (SB_Q, CHUNK)), jnp.broadcast_to(c1, (SB_Q, CHUNK))], axis=1)


def _fwd_sb(qn, ktr, vst):
    s = qn.shape[0]
    np_ = SB_PAIRS

    def body(q_ref, k_ref, v_ref, after_ref, o_ref, lt_ref, z_buf, att_buf):
        i = pl.program_id(1)
        last = 2 * i + 1
        after = after_ref[...]
        lanes = [slice(pr * CHUNK, (pr + 1) * CHUNK) for pr in range(np_)]
        qs = [q_ref[:, lanes[pr]] for pr in range(np_)]

        def rows(kb):
            return pl.ds(pl.multiple_of(kb * PAIR, PAIR), PAIR)

        def block(kb, carry, masked):
            out = []
            for pr in range(np_):
                acc, c0, c1 = carry[pr]
                z_next = _dot(qs[pr], k_ref[pr, jnp.maximum(kb - 1, 0)])
                pv = _dot(att_buf[pr], v_ref[rows(jnp.minimum(kb + 1, last)), lanes[pr]])
                z, sp, nlb, mask = _sb_scores(z_buf[pr], i * SB_Q, kb * CHUNK, masked)
                loc = _dot(nlb, after)
                att = jnp.exp(z - sp - loc - _per_head(c0, c1))
                if masked:
                    att = jnp.where(mask, att, 0.0)
                z_buf[pr] = z_next
                att_buf[pr] = att.astype(BF16)
                out.append((acc + pv, c0 + loc[:, 0:1] + nlb[:, 0:1].astype(F32),
                            c1 + loc[:, CHUNK:CHUNK + 1] + nlb[:, CHUNK:CHUNK + 1].astype(F32)))
            return tuple(out)

        for pr in range(np_):
            z_buf[pr] = _dot(qs[pr], k_ref[pr, last])
        att_buf[...] = jnp.zeros_like(att_buf)
        zero = jnp.zeros((SB_Q, 1), F32)
        carry = ((jnp.zeros((SB_Q, CHUNK), F32), zero, zero),) * np_
        carry = block(last, carry, True)
        carry = block(last - 1, carry, True)
        carry = lax.fori_loop(0, 2 * i, lambda j, c: block(2 * i - 1 - j, c, False), carry)
        for pr, (acc, c0, c1) in enumerate(carry):
            o_ref[:, lanes[pr]] = acc + _dot(att_buf[pr], v_ref[rows(0), lanes[pr]])
            lt_ref[:, lanes[pr]] = jnp.concatenate([jnp.broadcast_to(c0, (SB_Q, HEAD_DIM)),
                                                    jnp.broadcast_to(c1, (SB_Q, HEAD_DIM))], axis=1)

    blk = pl.BlockSpec((SB_Q, np_ * CHUNK), lambda p, i: (i, p))
    seq = pl.BlockSpec((2 * s, np_ * CHUNK), lambda p, i: (0, p))
    return pl.pallas_call(
        body, name="fwd_sb", grid=(SB_W // CHUNK // np_, s // SB_Q),
        in_specs=[blk, pl.BlockSpec((np_, s // CHUNK, CHUNK, PAIR), lambda p, i: (p, 0, 0, 0)), seq,
                  pl.BlockSpec((PAIR, PAIR), lambda p, i: (0, 0))],
        out_specs=[blk, blk],
        out_shape=[jax.ShapeDtypeStruct((s, SB_W), F32)] * 2,
        scratch_shapes=[pltpu.VMEM((np_, SB_Q, PAIR), F32), pltpu.VMEM((np_, SB_Q, PAIR), BF16)],
        compiler_params=_cp("parallel", "parallel"),
    )(qn, ktr, vst, _pair_tri("after"))


def _group_norms(yc, ys, yb):
    return [_rms_stats(yc), _rms_stats(ys), _rms_stats(yb)]


def _fwd_out(yc, ys, yb, g, w, x, tok):
    s = x.shape[0]
    tm = TOKEN_TILE

    def body(yc_ref, ys_ref, yb_ref, g_ref, w_ref, x_ref, tok_ref, o_ref):
        stats = _group_norms(yc_ref[...], ys_ref[...], yb_ref[...])
        cat = jnp.concatenate([xh for _, xh in stats], axis=1) * g_ref[...]
        o_ref[...] = x_ref[...] + _dot(cat.astype(BF16), w_ref[...])

    return pl.pallas_call(
        body, name="fwd_out", grid=(s // tm,),
        in_specs=[_rows(tm, CONV_W), _rows(tm, SG_W), _rows(tm, SB_W), _whole((1, D_MODEL)),
                  _whole((D_MODEL, D_MODEL)), _rows(tm, D_MODEL), ORDER_ONLY],
        out_specs=_rows(tm, D_MODEL),
        out_shape=jax.ShapeDtypeStruct((s, D_MODEL), F32),
        compiler_params=_cp("parallel"),
    )(yc, ys, yb, g, w, x, tok)


def _fwd_ffn(x, g, wgu, wd):
    s = x.shape[0]
    tm = TOKEN_TILE

    def body(x_ref, g_ref, wgu_ref, wd_ref, gu_ref, o_ref):
        x = x_ref[...]
        r, xh = _rms_stats(x)
        gu = _dot_nt((xh * g_ref[...]).astype(BF16), wgu_ref[...])
        gu_ref[...] = gu
        gate = gu[:, :FFN_H]
        act = gate * _sigmoid(gate) * gu[:, FFN_H:]
        o_ref[...] = x + _dot(act.astype(BF16), wd_ref[...])

    return pl.pallas_call(
        body, name="fwd_ffn", grid=(s // tm,),
        in_specs=[_rows(tm, D_MODEL), _whole((1, D_MODEL)),
                  pl.BlockSpec((2 * FFN_H, D_MODEL), lambda i: (0, 0), pipeline_mode=pl.Buffered(1)),
                  pl.BlockSpec((FFN_H, D_MODEL), lambda i: (0, 0), pipeline_mode=pl.Buffered(1))],
        out_specs=[_rows(tm, 2 * FFN_H), _rows(tm, D_MODEL)],
        out_shape=[jax.ShapeDtypeStruct((s, 2 * FFN_H), F32), jax.ShapeDtypeStruct((s, D_MODEL), F32)],
        compiler_params=_cp("parallel"),
    )(x, g, wgu, wd)


def _loss_head(y, target):
    s = y.shape[0]
    tm = TOKEN_TILE

    def body(y_ref, t_ref, l_ref, d_ref):
        @pl.when(pl.program_id(0) == 0)
        def _():
            l_ref[...] = jnp.zeros_like(l_ref)

        err = y_ref[...] - t_ref[...]
        d_ref[...] = err * (1.0 / D_MODEL)
        l_ref[...] += 0.5 * jnp.sum(jnp.mean(err * err, axis=-1, keepdims=True), axis=0, keepdims=True)

    return pl.pallas_call(
        body, name="loss_head", grid=(s // tm,),
        in_specs=[_rows(tm, D_MODEL), _rows(tm, D_MODEL)],
        out_specs=[_whole((1, 1)), _rows(tm, D_MODEL)],
        out_shape=[jax.ShapeDtypeStruct((1, 1), F32), jax.ShapeDtypeStruct((s, D_MODEL), F32)],
        compiler_params=_cp("arbitrary"),
    )(y, target)


def _accumulate(ref, value):
    @pl.when(pl.program_id(0) == 0)
    def _():
        ref[...] = jnp.zeros_like(ref)

    ref[...] += value


def _bwd_ffn(dxo, gu, xm, g, wgu, wd, tok):
    s = dxo.shape[0]
    tm = TOKEN_TILE

    def body(dxo_ref, gu_ref, xm_ref, g_ref, wgu_ref, wd_ref, tok_ref, dgu_ref, act_ref, h_ref, dxm_ref, dg_ref):
        dxo = dxo_ref[...]
        gu = gu_ref[...]
        gate, up = gu[:, :FFN_H], gu[:, FFN_H:]
        sg = _sigmoid(gate)
        sl = gate * sg
        act_ref[...] = (sl * up).astype(BF16)
        dact = _dot_nt(dxo.astype(BF16), wd_ref[...])
        dgate = dact * up * (sg * (1.0 + gate * (1.0 - sg)))
        dgu = jnp.concatenate([dgate, dact * sl], axis=1).astype(BF16)
        dgu_ref[...] = dgu
        dh = _dot(dgu, wgu_ref[...])
        r, xh = _rms_stats(xm_ref[...])
        h_ref[...] = (xh * g_ref[...]).astype(BF16)
        dx, dgrow = _rms_bwd(xh, r, g_ref[...], dh)
        dxm_ref[...] = dxo + dx
        _accumulate(dg_ref, _colsum(dgrow))

    return pl.pallas_call(
        body, name="bwd_ffn", grid=(s // tm,),
        in_specs=[_rows(tm, D_MODEL), _rows(tm, 2 * FFN_H), _rows(tm, D_MODEL), _whole((1, D_MODEL)),
                  pl.BlockSpec((2 * FFN_H, D_MODEL), lambda i: (0, 0), pipeline_mode=pl.Buffered(1)),
                  pl.BlockSpec((FFN_H, D_MODEL), lambda i: (0, 0), pipeline_mode=pl.Buffered(1)), ORDER_ONLY],
        out_specs=[_rows(tm, 2 * FFN_H), _rows(tm, FFN_H), _rows(tm, D_MODEL), _rows(tm, D_MODEL),
                   _whole((1, D_MODEL))],
        out_shape=[jax.ShapeDtypeStruct((s, 2 * FFN_H), BF16), jax.ShapeDtypeStruct((s, FFN_H), BF16),
                   jax.ShapeDtypeStruct((s, D_MODEL), BF16), jax.ShapeDtypeStruct((s, D_MODEL), F32),
                   jax.ShapeDtypeStruct((1, D_MODEL), F32)],
        compiler_params=_cp("arbitrary"),
    )(dxo, gu, xm, g, wgu, wd, tok)


def _matmul_tn(a, b, tm, tn, out_dtype=BF16):
    s, m = a.shape
    n = b.shape[1]

    def body(a_ref, b_ref, o_ref):
        o_ref[...] = _dot_tn(a_ref[...].astype(BF16), b_ref[...].astype(BF16)).astype(out_dtype)

    return pl.pallas_call(
        body, name="weight_grad", grid=(m // tm, n // tn),
        in_specs=[pl.BlockSpec((s, tm), lambda i, j: (0, i)), pl.BlockSpec((s, tn), lambda i, j: (0, j))],
        out_specs=pl.BlockSpec((tm, tn), lambda i, j: (i, j)),
        out_shape=jax.ShapeDtypeStruct((m, n), out_dtype),
        compiler_params=_cp("parallel", "parallel"),
    )(a, b)


def _bwd_out(dxm, yc, ys, yb, g, w, tok):
    s = dxm.shape[0]
    tm = TOKEN_TILE

    def body(dxm_ref, yc_ref, ys_ref, yb_ref, g_ref, w_ref, tok_ref, dyc_ref, dys_ref, dyb_ref, cat_ref, dg_ref):
        stats = _group_norms(yc_ref[...], ys_ref[...], yb_ref[...])
        g = g_ref[...]
        cat_ref[...] = (jnp.concatenate([xh for _, xh in stats], axis=1) * g).astype(BF16)
        dcat = _dot_nt(dxm_ref[...].astype(BF16), w_ref[...])
        dgs = []
        off = 0
        for (r, xh), out in zip(stats, (dyc_ref, dys_ref, dyb_ref)):
            cols = slice(off, off + xh.shape[1])
            dx, dgrow = _rms_bwd(xh, r, g[:, cols], dcat[:, cols])
            out[...] = dx
            dgs.append(_colsum(dgrow))
            off += xh.shape[1]
        _accumulate(dg_ref, jnp.concatenate(dgs, axis=1))

    return pl.pallas_call(
        body, name="bwd_out", grid=(s // tm,),
        in_specs=[_rows(tm, D_MODEL), _rows(tm, CONV_W), _rows(tm, SG_W), _rows(tm, SB_W),
                  _whole((1, D_MODEL)), _whole((D_MODEL, D_MODEL)), ORDER_ONLY],
        out_specs=[_rows(tm, CONV_W), _rows(tm, SG_W), _rows(tm, SB_W), _rows(tm, D_MODEL),
                   _whole((1, D_MODEL))],
        out_shape=[jax.ShapeDtypeStruct((s, CONV_W), F32), jax.ShapeDtypeStruct((s, SG_W), F32),
                   jax.ShapeDtypeStruct((s, SB_W), F32), jax.ShapeDtypeStruct((s, D_MODEL), BF16),
                   jax.ShapeDtypeStruct((1, D_MODEL), F32)],
        compiler_params=_cp("arbitrary"),
    )(dxm, yc, ys, yb, g, w, tok)


def _bwd_sb(qn, kst, ktr, vtr, dy, ltot, tok):
    s = qn.shape[0]

    def body(q_ref, k_ref, kt_ref, vt_ref, do_ref, lt_ref, upto_ref, before_ref, tok_ref, dq_ref, dk_ref, dv_ref,
             z_buf, da_buf, dz_buf, att_buf):
        i = pl.program_id(1)
        last = 2 * i + 1

        @pl.when(i == 0)
        def _():
            dk_ref[...] = jnp.zeros_like(dk_ref)
            dv_ref[...] = jnp.zeros_like(dv_ref)

        q = q_ref[...]
        do = do_ref[...]
        dob = do.astype(BF16)
        q_t = q.astype(F32).T.astype(BF16)
        do_t = do.T.astype(BF16)
        lt = lt_ref[...]
        ltot = _per_head(lt[:, 0:1], lt[:, HEAD_DIM:HEAD_DIM + 1])
        upto = upto_ref[...]
        before = before_ref[...]
        last0, last1 = slice(CHUNK - 1, CHUNK), slice(PAIR - 1, PAIR)

        def rows(kb):
            return pl.ds(pl.multiple_of(kb * PAIR, PAIR), PAIR)

        def ahead(kb):
            return _dot(q, kt_ref[kb]), _dot(dob, vt_ref[kb])

        def behind(kb, dq):
            dzb = dz_buf[...]
            dk_ref[kb] += _dot(q_t, dzb)
            dv_ref[kb] += _dot(do_t, att_buf[...])
            return dq + _dot(dzb, k_ref[rows(kb), :])

        def block(kb, carry, masked):
            dq, p0, p1, e0, e1 = carry
            z_next, da_next = ahead(jnp.minimum(kb + 1, last))
            dq = behind(jnp.maximum(kb - 1, 0), dq)
            z, sp, nlb, mask = _sb_scores(z_buf[...], i * SB_Q, kb * CHUNK, masked)
            pin = _dot(nlb, upto) + _per_head(p0, p1)
            sig = jnp.exp(z - sp)
            att = jnp.exp(z - sp - (ltot - pin))
            if masked:
                att = jnp.where(mask, att, 0.0)
            e = att * da_buf[...]
            ebefore = _dot(e.astype(BF16), before) + _per_head(e0, e1)
            dz = e - sig * (e + ebefore)
            if masked:
                dz = jnp.where(mask, dz, 0.0)
            z_buf[...] = z_next
            da_buf[...] = da_next
            dz_buf[...] = dz.astype(BF16)
            att_buf[...] = att.astype(BF16)
            return (dq, pin[:, last0], pin[:, last1],
                    ebefore[:, last0] + e[:, last0], ebefore[:, last1] + e[:, last1])

        z_buf[...], da_buf[...] = ahead(0)
        dz_buf[...] = jnp.zeros_like(dz_buf)
        att_buf[...] = jnp.zeros_like(att_buf)
        zero = jnp.zeros((SB_Q, 1), F32)
        carry = (jnp.zeros((SB_Q, CHUNK), F32), zero, zero, zero, zero)
        carry = lax.fori_loop(0, 2 * i, lambda kb, c: block(kb, c, False), carry)
        carry = block(last - 1, carry, True)
        carry = block(last, carry, True)
        dq_ref[...] = behind(last, carry[0])

    blk = pl.BlockSpec((SB_Q, CHUNK), lambda p, i: (i, p))
    seq = pl.BlockSpec((2 * s, CHUNK), lambda p, i: (0, p))
    tiles = pl.BlockSpec((None, s // CHUNK, CHUNK, PAIR), lambda p, i: (p, 0, 0, 0))
    tri = pl.BlockSpec((PAIR, PAIR), lambda p, i: (0, 0))
    return pl.pallas_call(
        body, name="bwd_sb", grid=(SB_W // CHUNK, s // SB_Q),
        in_specs=[blk, seq, tiles, tiles, blk, blk, tri, tri, ORDER_ONLY],
        out_specs=[blk, tiles, tiles],
        out_shape=[jax.ShapeDtypeStruct((s, SB_W), F32)]
        + [jax.ShapeDtypeStruct((SB_W // CHUNK, s // CHUNK, CHUNK, PAIR), F32)] * 2,
        scratch_shapes=[pltpu.VMEM((SB_Q, PAIR), F32), pltpu.VMEM((SB_Q, PAIR), F32),
                        pltpu.VMEM((SB_Q, PAIR), BF16), pltpu.VMEM((SB_Q, PAIR), BF16)],
        compiler_params=_cp("parallel", "arbitrary"),
    )(qn, kst, ktr, vtr, dy, ltot, _pair_tri("upto"), _pair_tri("before"), tok)


def _head_sum(row):
    acc = row[:, 0:HEAD_DIM]
    for h in range(1, SB_W // HEAD_DIM):
        acc = acc + row[:, h * HEAD_DIM:(h + 1) * HEAD_DIM]
    return acc


def _bwd_qk(proj, dqs, dkn, dv, qg, kg):
    s = proj.shape[0]
    tm = TOKEN_TILE
    tiles = pl.BlockSpec((SB_W // CHUNK, tm // CHUNK, CHUNK, PAIR), lambda i: (0, i, 0, 0))

    def body(q_ref, k_ref, dqs_ref, dkn_ref, dv_ref, qg_ref, kg_ref, dp_ref, dqg_ref, dkg_ref, qacc, kacc):
        i = pl.program_id(0)
        gm = _group_mean_matrix(SB_W, HEAD_DIM)

        def one(x, dy, g, acc):
            r = lax.rsqrt(_dot_split(x * x, gm) + RMS_EPS)
            xh = x * r
            dxh = dy * g
            _accumulate(acc, _colsum(dy * xh))
            return r * (dxh - xh * _dot_split(dxh * xh, gm))

        dq = one(q_ref[...], dqs_ref[...] * (HEAD_DIM ** -0.5), qg_ref[...], qacc)
        dk = one(k_ref[...], _unstack_heads(_load_transposed(dkn_ref)), kg_ref[...], kacc)
        dp_ref[...] = jnp.concatenate([dq, dk, _unstack_heads(_load_transposed(dv_ref))], axis=1).astype(BF16)

        @pl.when(i == pl.num_programs(0) - 1)
        def _():
            dqg_ref[...] = _head_sum(qacc[...])
            dkg_ref[...] = _head_sum(kacc[...])

    return pl.pallas_call(
        body, name="bwd_qk", grid=(s // tm,),
        in_specs=[_rows(tm, SB_W, OFF_SB // SB_W), _rows(tm, SB_W, OFF_SB // SB_W + 1),
                  _rows(tm, SB_W), tiles, tiles, _whole((1, SB_W)), _whole((1, SB_W))],
        out_specs=[_rows(tm, 3 * SB_W), _whole((1, HEAD_DIM)), _whole((1, HEAD_DIM))],
        out_shape=[jax.ShapeDtypeStruct((s, 3 * SB_W), BF16), jax.ShapeDtypeStruct((1, HEAD_DIM), F32),
                   jax.ShapeDtypeStruct((1, HEAD_DIM), F32)],
        scratch_shapes=[pltpu.VMEM((1, SB_W), F32), pltpu.VMEM((1, SB_W), F32)],
        compiler_params=_cp("arbitrary"),
    )(proj, proj, dqs, dkn, dv, qg, kg)


def _bwd_sg(proj, dy, lg, lb, w, bias):
    s = proj.shape[0]
    tm = TOKEN_TILE
    nh = SG_W // HEAD_DIM

    def body(p_ref, dy_ref, lg_ref, lb_ref, w_ref, bias_ref, dp_ref, dlg_ref, dlb_ref, dw_ref, db_ref, dbias):
        i = pl.program_id(0)
        uv = p_ref[...]
        ge = _gelu(uv)
        u = ge[:, :SG_W]
        r, xh = _ln_stats(ge[:, SG_W:])
        vln = (xh * lg_ref[...] + lb_ref[...]).astype(BF16)
        dy = dy_ref[...]
        tril, lane_head = _sg_masks()

        @pl.when(i == 0)
        def _():
            dw_ref[...] = jnp.zeros_like(dw_ref)
            dbias[...] = jnp.zeros_like(dbias)

        dus, dvlns = [], []
        for c in range(tm // CHUNK):
            rows = slice(c * CHUNK, (c + 1) * CHUNK)
            vc = vln[rows]
            dus.append(dy[rows] * _sg_mix(w_ref, bias_ref, vc, tril, lane_head))
            dm = dy[rows] * u[rows]
            dbias[...] += dm
            dvc = jnp.zeros((CHUNK, SG_W), F32)
            for h in range(nh):
                dmh = jnp.where(lane_head == h, dm, 0.0).astype(BF16)
                dw_ref[h] += jnp.where(tril, _dot_nt(dmh, vc), 0.0)
                wm = jnp.where(tril, w_ref[h], 0.0).astype(BF16)
                dvc = dvc + _dot_tn(wm, dmh)
            dvlns.append(dvc)
        du = jnp.concatenate(dus, axis=0)
        dvln = jnp.concatenate(dvlns, axis=0)
        _accumulate(dlg_ref, _colsum(dvln * xh))
        _accumulate(dlb_ref, _colsum(dvln))
        dv = _ln_bwd(xh, r, lg_ref[...], dvln)
        dp_ref[...] = (jnp.concatenate([du, dv], axis=1) * _gelu_grad(uv)).astype(BF16)

        @pl.when(i == pl.num_programs(0) - 1)
        def _():
            lane = lax.broadcasted_iota(jnp.int32, (CHUNK, CHUNK), 1)
            acc = dbias[...]
            out = jnp.zeros((CHUNK, CHUNK), F32)
            for h in range(nh):
                hs = jnp.sum(acc[:, h * HEAD_DIM:(h + 1) * HEAD_DIM], axis=1, keepdims=True)
                out = out + jnp.where(lane == h, hs, 0.0)
            db_ref[...] = out

    return pl.pallas_call(
        body, name="bwd_sg", grid=(s // tm,),
        in_specs=[_rows(tm, 2 * SG_W, 1), _rows(tm, SG_W), _whole((1, SG_W)), _whole((1, SG_W)),
                  _whole((nh, CHUNK, CHUNK)), _whole((CHUNK, SG_W))],
        out_specs=[_rows(tm, 2 * SG_W), _whole((1, SG_W)), _whole((1, SG_W)), _whole((nh, CHUNK, CHUNK)),
                   _whole((CHUNK, CHUNK))],
        out_shape=[jax.ShapeDtypeStruct((s, 2 * SG_W), BF16), jax.ShapeDtypeStruct((1, SG_W), F32),
                   jax.ShapeDtypeStruct((1, SG_W), F32), jax.ShapeDtypeStruct((nh, CHUNK, CHUNK), F32),
                   jax.ShapeDtypeStruct((CHUNK, CHUNK), F32)],
        scratch_shapes=[pltpu.VMEM((CHUNK, SG_W), F32)],
        compiler_params=_cp("arbitrary"),
    )(proj, dy, lg, lb, w, bias)


def _bwd_conv(proj, dy, w, b, lg, lb):
    s = proj.shape[0]

    def body(p_ref, dy_ref, w_ref, b_ref, lg_ref, lb_ref, dp_ref, dw_ref, db_ref, dlg_ref, dlb_ref,
             abuf, dcbuf):
        _glu_fill(p_ref, abuf, s)
        dcbuf[pl.ds(s, CHUNK), :] = jnp.zeros((CHUNK, CONV_W), F32)
        dw_ref[...] = jnp.zeros_like(dw_ref)

        def chunk(c, carry):
            db, dlg, dlb = carry
            r0 = pl.multiple_of(c * CHUNK, CHUNK)
            acc, win = _conv_window(abuf, r0, w_ref)
            r, xh = _ln_stats(acc + b_ref[...])
            ln = xh * lg_ref[...] + lb_ref[...]
            sg = _sigmoid(ln)
            dl = dy_ref[pl.ds(r0, CHUNK), :] * (sg * (1.0 + ln * (1.0 - sg)))
            dc = _ln_bwd(xh, r, lg_ref[...], dl)
            dcbuf[pl.ds(r0, CHUNK), :] = dc
            for k in range(CONV_K):
                sh = win if k == 0 else pltpu.roll(win, k, axis=0)
                dw_ref[CONV_K - 1 - k:CONV_K - k, :] += _colsum(dc * sh[32:, :])
            return db + _colsum(dc), dlg + _colsum(dl * xh), dlb + _colsum(dl)

        zero = jnp.zeros((1, CONV_W), F32)
        db, dlg, dlb = lax.fori_loop(0, s // CHUNK, chunk, (zero, zero, zero))
        db_ref[...] = db
        dlg_ref[...] = dlg
        dlb_ref[...] = dlb

        def chunk_back(c, carry):
            r0 = pl.multiple_of(c * CHUNK, CHUNK)
            win = dcbuf[pl.ds(r0, CHUNK + 32), :]
            da = jnp.zeros((CHUNK, CONV_W), F32)
            for k in range(CONV_K):
                sh = win if k == 0 else pltpu.roll(win, CHUNK + 32 - k, axis=0)
                da = da + sh[:CHUNK, :] * w_ref[CONV_K - 1 - k:CONV_K - k, :]
            pv = p_ref[pl.ds(r0, CHUNK), :]
            val, sg = pv[:, :CONV_W], _sigmoid(pv[:, CONV_W:])
            dp_ref[pl.ds(r0, CHUNK), :] = jnp.concatenate([da * sg, da * val * sg * (1.0 - sg)], axis=1).astype(BF16)
            return carry

        lax.fori_loop(0, s // CHUNK, chunk_back, 0)

    row = _whole((1, CONV_W))
    return pl.pallas_call(
        body, name="bwd_conv", grid=(1,),
        in_specs=[pl.BlockSpec((s, 2 * CONV_W), lambda i: (0, 0)), _whole((s, CONV_W)),
                  _whole((CONV_K, CONV_W)), row, row, row],
        out_specs=[_whole((s, 2 * CONV_W)), _whole((CONV_K, CONV_W)), row, row, row],
        out_shape=[jax.ShapeDtypeStruct((s, 2 * CONV_W), BF16), jax.ShapeDtypeStruct((CONV_K, CONV_W), F32)]
        + [jax.ShapeDtypeStruct((1, CONV_W), F32)] * 3,
        scratch_shapes=[pltpu.VMEM((s + CHUNK, CONV_W), F32), pltpu.VMEM((s + CHUNK, CONV_W), F32)],
        compiler_params=_cp("arbitrary"),
    )(proj, dy, w, b, lg, lb)


def _bwd_in(dpc, dps, dpb, x, g, w, dxm):
    s = x.shape[0]
    tm = TOKEN_TILE

    def body(dpc_ref, dps_ref, dpb_ref, x_ref, g_ref, w_ref, dxm_ref, dx_ref, h_ref, dp_ref, dg_ref):
        dp = jnp.concatenate([dpc_ref[...], dps_ref[...], dpb_ref[...]], axis=1)
        dp_ref[...] = dp
        dh = _dot(dp, w_ref[...])
        r, xh = _rms_stats(x_ref[...])
        h_ref[...] = (xh * g_ref[...]).astype(BF16)
        dx, dgrow = _rms_bwd(xh, r, g_ref[...], dh)
        dx_ref[...] = dxm_ref[...] + dx
        _accumulate(dg_ref, _colsum(dgrow))

    return pl.pallas_call(
        body, name="bwd_in", grid=(s // tm,),
        in_specs=[_rows(tm, 2 * CONV_W), _rows(tm, 2 * SG_W), _rows(tm, 3 * SB_W), _rows(tm, D_MODEL),
                  _whole((1, D_MODEL)), _whole((IN_W, D_MODEL)), _rows(tm, D_MODEL)],
        out_specs=[_rows(tm, D_MODEL), _rows(tm, D_MODEL), _rows(tm, IN_W), _whole((1, D_MODEL))],
        out_shape=[jax.ShapeDtypeStruct((s, D_MODEL), F32), jax.ShapeDtypeStruct((s, D_MODEL), BF16),
                   jax.ShapeDtypeStruct((s, IN_W), BF16), jax.ShapeDtypeStruct((1, D_MODEL), F32)],
        compiler_params=_cp("arbitrary"),
    )(dpc, dps, dpb, x, g, w, dxm)


SMALL = ("mix_norm_g", "conv_w", "conv_b", "conv_ln_g", "conv_ln_b", "sg_ln_g", "sg_ln_b", "sg_w", "sg_b",
         "q_norm_g", "k_norm_g", "out_norm_g", "ffn_norm_g")
LARGE = ("w_in", "w_out", "w_gate_up", "w_down")


def _row(v):
    return v.reshape(1, -1)


def _layer_params(p, large, l):
    q = {k: v[l] for k, v in p.items()}
    return dict(
        q, **large,
        mix_norm_g=_row(q["mix_norm_g"]), conv_b=_row(q["conv_b"]), conv_ln_g=_row(q["conv_ln_g"]),
        conv_ln_b=_row(q["conv_ln_b"]), sg_ln_g=_row(q["sg_ln_g"]), sg_ln_b=_row(q["sg_ln_b"]),
        out_norm_g=_row(q["out_norm_g"]), ffn_norm_g=_row(q["ffn_norm_g"]),
        qg=_row(jnp.tile(q["q_norm_g"], SB_W // HEAD_DIM)), kg=_row(jnp.tile(q["k_norm_g"], SB_W // HEAD_DIM)),
        sg_bias=jnp.repeat(q["sg_b"].T, HEAD_DIM, axis=1),
    )


def _layer_fwd(x, q, tok, after_in, after_mixers):
    proj, qn, kn, vb, kt, vt = _fwd_in(x, q["mix_norm_g"], q["w_in"], q["qg"], q["kg"], tok)
    yc = _fwd_conv(proj, q["conv_w"], q["conv_b"], q["conv_ln_g"], q["conv_ln_b"], after_in(proj))
    ys = _fwd_sg(proj, q["sg_ln_g"], q["sg_ln_b"], q["sg_w"], q["sg_bias"])
    yb, lt = _fwd_sb(qn, kt, vb)
    rest, tok = after_mixers(yb)
    q = dict(q, **rest)
    xm = _fwd_out(yc, ys, yb, q["out_norm_g"], q["w_out"], x, tok)
    gu, xo = _fwd_ffn(xm, q["ffn_norm_g"], q["w_gate_up"], q["w_down"])
    return xo, q, dict(x=x, proj=proj, qn=qn, kn=kn, kt=kt, vt=vt, lt=lt, yc=yc, ys=ys, yb=yb, xm=xm, gu=gu)


def _layer_bwd_ffn(dxo, q, st, tok):
    dgu, act, h2, dxm, d_ffn_g = _bwd_ffn(dxo, st["gu"], st["xm"], q["ffn_norm_g"], q["w_gate_up"], q["w_down"],
                                          tok)
    d_wgu, d_wd = _matmul_tn(dgu, h2, 512, D_MODEL), _matmul_tn(act, dxo, FFN_H // 2, D_MODEL)
    dyc, dys, dyb, cat, d_out_g = _bwd_out(dxm, st["yc"], st["ys"], st["yb"], q["out_norm_g"], q["w_out"], tok)
    return dxm, (dyc, dys, dyb, d_ffn_g, d_out_g), d_wgu, d_wd, _matmul_tn(cat, dxm, 512, D_MODEL)


def _layer_bwd_mix(dxm, carried, q, st, tok):
    dyc, dys, dyb, d_ffn_g, d_out_g = carried
    dqs, dkn, dv = _bwd_sb(st["qn"], st["kn"], st["kt"], st["vt"], dyb, st["lt"], tok)
    dpb, d_qg, d_kg = _bwd_qk(st["proj"], dqs, dkn, dv, q["qg"], q["kg"])
    dps, d_sg_lg, d_sg_lb, d_sg_w, d_sg_b = _bwd_sg(st["proj"], dys, q["sg_ln_g"], q["sg_ln_b"], q["sg_w"],
                                                    q["sg_bias"])
    dpc, d_conv_w, d_conv_b, d_conv_lg, d_conv_lb = _bwd_conv(st["proj"], dyc, q["conv_w"], q["conv_b"],
                                                              q["conv_ln_g"], q["conv_ln_b"])
    dx, h1, dp, d_mix_g = _bwd_in(dpc, dps, dpb, st["x"], q["mix_norm_g"], q["w_in"], dxm)
    d_win = _matmul_tn(dp, h1, 512, D_MODEL)
    small = dict(
        mix_norm_g=d_mix_g[0], conv_w=d_conv_w, conv_b=d_conv_b[0], conv_ln_g=d_conv_lg[0],
        conv_ln_b=d_conv_lb[0], sg_ln_g=d_sg_lg[0], sg_ln_b=d_sg_lb[0], sg_w=d_sg_w,
        sg_b=d_sg_b[:, :SG_W // HEAD_DIM].T, q_norm_g=d_qg[0], k_norm_g=d_kg[0], out_norm_g=d_out_g[0],
        ffn_norm_g=d_ffn_g[0])
    return dx, d_win, small


def _position():
    x, y, c = lax.axis_index("x"), lax.axis_index("y"), lax.axis_index("c")
    return x, y, c


def _flat(px, py, pc):
    return 4 * px + 2 * py + pc


IN_HBM = pl.BlockSpec(memory_space=pltpu.HBM)
IN_SEM = pl.BlockSpec(memory_space=pltpu.SEMAPHORE)
EFFECT = pltpu.SideEffectType.DATAFLOW_SIDE_EFFECTING
COPIES = dict(scatter=7, spread=7, spread_chips=4, **{"pass": 3})


def _exchange_copies(kinds, src_refs, land_refs, send_sems, recv_sems, layer, arrival):
    x, y, c = _position()
    me = _flat(x, y, c)
    everyone = [(x ^ (k >> 2 & 1), y ^ (k >> 1 & 1), c ^ (k & 1)) for k in range(1, N_DEV)]
    sibling = (x, y, 1 - c)
    chips = [(1 - x, y, c), (x, 1 - y, c), (1 - x, 1 - y, c)]
    out = []
    srcs = iter(src_refs)
    for kind, land in zip(kinds, land_refs):
        land = land if layer is None else land.at[layer]
        if kind == "scatter":
            src = next(srcs)
            moves = [(src.at[_flat(*p)], me, _flat(*p), p) for p in everyone]
        elif kind in ("spread", "spread_chips"):
            src = next(srcs)
            moves = [(src, me, _flat(*p), p) for p in (everyone if kind == "spread" else [sibling] + chips)]
        else:
            moves = [(land.at[_flat(*p)], _flat(*p), _flat(p[0], p[1], 1 - c), sibling) for p in chips]
        for src_block, there, here, peer in moves:
            n = len(out)
            out.append(pltpu.make_async_remote_copy(
                src_ref=src_block, dst_ref=land.at[here if arrival else there], send_sem=send_sems.at[n],
                recv_sem=recv_sems.at[n], device_id=peer, device_id_type=MESH))
    return out


def _exchange_start(kinds, srcs, lands, after, name, layer=None):
    ns, n = len(srcs), len(srcs) + len(lands)
    sems = sum(COPIES[k] for k in kinds)

    def body(*refs):
        send_sems, recv_sems = refs[n + 1], refs[n + 2]
        for cp in _exchange_copies(kinds, refs[:ns], refs[ns:n], send_sems, recv_sems, layer, arrival=False):
            cp.start()
        refs[-1][...] = jnp.zeros_like(refs[-1])

    thru = [pltpu.HBM(a.shape, a.dtype) for a in (*srcs, *lands)]
    outs = pl.pallas_call(
        body, name=name,
        out_shape=(pltpu.SemaphoreType.DMA((sems,)), pltpu.SemaphoreType.DMA((sems,)), *thru,
                   jax.ShapeDtypeStruct((8, 128), F32)),
        in_specs=[IN_HBM] * n + [ORDER_ONLY],
        out_specs=(IN_SEM, IN_SEM, *[IN_HBM] * n, pl.BlockSpec(memory_space=pltpu.VMEM)),
        input_output_aliases={i: 2 + i for i in range(n)},
        compiler_params=pltpu.CompilerParams(has_side_effects=EFFECT),
    )(*[pltpu.with_memory_space_constraint(a, pltpu.HBM) for a in (*srcs, *lands)], after)
    return kinds, outs[0], outs[1], list(outs[2:2 + ns]), list(outs[2 + ns:2 + n]), outs[-1]


def _exchange_wait(pending, after, name, layer=None):
    kinds, send_sems, recv_sems, srcs, lands, _ = pending
    ns, n = len(srcs), len(srcs) + len(lands)

    def body(*refs):
        for cp in _exchange_copies(kinds, refs[:ns], refs[ns:n], refs[n], refs[n + 1], layer, arrival=True):
            cp.wait_send()
            cp.wait_recv()

    thru = [pltpu.HBM(a.shape, a.dtype) for a in (*srcs, *lands)]
    outs = pl.pallas_call(
        body, name=name, out_shape=tuple(thru),
        in_specs=[IN_HBM] * n + [IN_SEM, IN_SEM, ORDER_ONLY],
        out_specs=tuple([IN_HBM] * n),
        input_output_aliases={i: i for i in range(n)},
        compiler_params=pltpu.CompilerParams(has_side_effects=EFFECT),
    )(*srcs, *lands, send_sems, recv_sems, after)
    return list(outs[ns:])


def _landing(block, me):
    land = lax.empty((N_DEV,) + block.shape, block.dtype)
    return lax.dynamic_update_index_in_dim(land, block, me, 0)


def _adamw(parts, w, m, v, tr):
    groups, rows, cols = w.shape

    def body(p_ref, w_ref, m_ref, v_ref, g_ref, d_ref, nm_ref, nv_ref):
        g = p_ref[0].astype(F32)
        for j in range(1, N_DEV):
            g = g + p_ref[j].astype(F32)
        g_ref[...] = g
        m = ADAM_B1 * m_ref[...] + (1.0 - ADAM_B1) * g
        v = ADAM_B2 * v_ref[...] + (1.0 - ADAM_B2) * (g * g)
        nm_ref[...] = m
        nv_ref[...] = v
        m_hat = m / (1.0 - ADAM_B1 ** ADAM_STEP)
        v_hat = v / (1.0 - ADAM_B2 ** ADAM_STEP)
        d_ref[...] = -ADAM_LR * (m_hat / (jnp.sqrt(v_hat) + ADAM_EPS) + ADAM_WD * w_ref[...])

    blk = pl.BlockSpec((None, tr, cols), lambda g, i: (g, i, 0))
    return pl.pallas_call(
        body, name="adamw", grid=(groups, rows // tr),
        in_specs=[pl.BlockSpec((None, N_DEV, tr, cols), lambda g, i: (g, 0, i, 0)), blk, blk, blk],
        out_specs=[blk] * 4,
        out_shape=[jax.ShapeDtypeStruct((groups, rows, cols), F32)] * 4,
        compiler_params=_cp("parallel", "parallel"),
    )(parts, w, m, v)


def _row_tile(rows):
    for cand in range(min(rows, 512) // 8 * 8, 7, -8):
        if rows % cand == 0:
            return cand
    return rows


def _with_own_block(land, blocks, layer, me):
    own = lax.dynamic_index_in_dim(blocks, me, 0, keepdims=True)[None]
    return lax.dynamic_update_slice(land, own, (layer, me, 0, 0))


PACK_LANES = 128


def _pack_layers(arrs):
    parts = []
    for a in arrs:
        flat = a.reshape(a.shape[0], -1)
        parts.append(jnp.pad(flat, ((0, 0), (0, -flat.shape[1] % (8 * PACK_LANES)))))
    return jnp.concatenate(parts, axis=1).reshape(arrs[0].shape[0], -1, PACK_LANES)


def _unpack_layers(packed, shapes):
    flat = packed.reshape(packed.shape[0], -1)
    outs, off = [], 0
    for shp in shapes:
        size = 1
        for d in shp[1:]:
            size *= d
        outs.append(flat[:, off:off + size].reshape(shp))
        off += size + (-size % (8 * PACK_LANES))
    return outs


def kernel(x, mix_norm_g, w_in, conv_w, conv_b, conv_ln_g, conv_ln_b, sg_ln_g, sg_ln_b, sg_w, sg_b, q_norm_g, k_norm_g, out_norm_g, w_out, ffn_norm_g, w_gate_up, w_down, loss_target, m_mix_norm_g, m_w_in, m_conv_w, m_conv_b, m_conv_ln_g, m_conv_ln_b, m_sg_ln_g, m_sg_ln_b, m_sg_w, m_sg_b, m_q_norm_g, m_k_norm_g, m_out_norm_g, m_w_out, m_ffn_norm_g, m_w_gate_up, m_w_down, v_mix_norm_g, v_w_in, v_conv_w, v_conv_b, v_conv_ln_g, v_conv_ln_b, v_sg_ln_g, v_sg_ln_b, v_sg_w, v_sg_b, v_q_norm_g, v_k_norm_g, v_out_norm_g, v_w_out, v_ffn_norm_g, v_w_gate_up, v_w_down):
    names = SMALL[:1] + LARGE[:1] + SMALL[1:12] + LARGE[1:2] + SMALL[12:] + LARGE[2:]
    w = dict(mix_norm_g=mix_norm_g, w_in=w_in, conv_w=conv_w, conv_b=conv_b, conv_ln_g=conv_ln_g,
             conv_ln_b=conv_ln_b, sg_ln_g=sg_ln_g, sg_ln_b=sg_ln_b, sg_w=sg_w, sg_b=sg_b, q_norm_g=q_norm_g,
             k_norm_g=k_norm_g, out_norm_g=out_norm_g, w_out=w_out, ffn_norm_g=ffn_norm_g,
             w_gate_up=w_gate_up, w_down=w_down)
    m = dict(mix_norm_g=m_mix_norm_g, w_in=m_w_in, conv_w=m_conv_w, conv_b=m_conv_b, conv_ln_g=m_conv_ln_g,
             conv_ln_b=m_conv_ln_b, sg_ln_g=m_sg_ln_g, sg_ln_b=m_sg_ln_b, sg_w=m_sg_w, sg_b=m_sg_b,
             q_norm_g=m_q_norm_g, k_norm_g=m_k_norm_g, out_norm_g=m_out_norm_g, w_out=m_w_out,
             ffn_norm_g=m_ffn_norm_g, w_gate_up=m_w_gate_up, w_down=m_w_down)
    v = dict(mix_norm_g=v_mix_norm_g, w_in=v_w_in, conv_w=v_conv_w, conv_b=v_conv_b, conv_ln_g=v_conv_ln_g,
             conv_ln_b=v_conv_ln_b, sg_ln_g=v_sg_ln_g, sg_ln_b=v_sg_ln_b, sg_w=v_sg_w, sg_b=v_sg_b,
             q_norm_g=v_q_norm_g, k_norm_g=v_k_norm_g, out_norm_g=v_out_norm_g, w_out=v_w_out,
             ffn_norm_g=v_ffn_norm_g, w_gate_up=v_w_gate_up, w_down=v_w_down)
    xpos, ypos, cpos = _position()
    me = _flat(xpos, ypos, cpos)
    conv_cols = conv_w.shape[-1]
    no_token = jnp.zeros((8, 128), F32)
    w, m, v = (dict(t, w_in=jnp.swapaxes(t["w_in"], 1, 2), w_gate_up=jnp.swapaxes(t["w_gate_up"], 1, 2))
               for t in (w, m, v))
    shards = {k: w[k].astype(BF16) for k in LARGE}
    full_shape = dict(w_in=(IN_W, D_MODEL), w_out=(D_MODEL, D_MODEL), w_gate_up=(2 * FFN_H, D_MODEL),
                      w_down=(FFN_H, D_MODEL))

    def gather_start(srcs, after, tag):
        return _exchange_start(["spread_chips"] * len(srcs), srcs, [_landing(a, me) for a in srcs], after,
                               f"gather_start_{tag}")

    def gather_pass(pending, after, tag):
        lands = _exchange_wait(pending, after, f"gather_wait_{tag}")
        return _exchange_start(["pass"] * len(lands), [], lands, lands[0], f"gather_pass_{tag}")

    def gathered(pending, keys, after, tag):
        lands = _exchange_wait(pending, after, f"gather_passed_{tag}")
        return {k: a.reshape(full_shape[k]) for k, a in zip(keys, lands)}, lands[len(keys):]

    first, later = ("w_in",), ("w_out", "w_gate_up", "w_down")
    act = x[0]
    head = gather_start([shards["w_in"][0], w["conv_w"]], act, "0")
    tail = gather_start([shards[k][0] for k in later], head[5], "0_later")
    head = gather_pass(head, tail[5], "0")
    large, (conv_blocks,) = gathered(head, first, head[5], "0")
    conv_full = jnp.transpose(conv_blocks, (1, 2, 0, 3)).reshape(DEPTH, CONV_K, CONV_W)
    small_w = dict({k: w[k] for k in SMALL}, conv_w=conv_full)
    qs, stash = [], []
    for l in range(DEPTH):
        coming = {}

        def after_in(proj):
            if l + 1 == DEPTH:
                return no_token
            coming["first"] = gather_start([shards[k][l + 1] for k in LARGE], proj, str(l + 1))
            return coming["first"][5]

        def after_mixers(y_sb):
            rest, token = {}, no_token
            if l == 0:
                passing = gather_pass(tail, y_sb, "0_later")
                rest, _ = gathered(passing, later, passing[5], "0_later")
            if l + 1 < DEPTH:
                coming["second"] = gather_pass(coming["first"], y_sb, str(l + 1))
                token = coming["second"][5]
            return rest, token

        act, q, st = _layer_fwd(act, _layer_params(small_w, large, l), no_token, after_in, after_mixers)
        qs.append(q)
        stash.append(st)
        if l + 1 < DEPTH:
            large, _ = gathered(coming["second"], LARGE, act, str(l + 1))

    loss, dx = _loss_head(act, loss_target[0])
    loss = lax.psum(loss[0, 0], ("x", "y", "c"))

    replicated = tuple(k for k in SMALL if k != "conv_w")
    small_rows = _pack_layers([w[k][:1] for k in replicated]).shape[1]
    conv_rows = _pack_layers([conv_full[:1]]).shape[1]
    group_a, group_b = ("w_gate_up", "w_down", "w_out"), ("w_in",)
    blocks = lambda k, a: a.reshape((N_DEV,) + w[k].shape[1:])
    land_a = [lax.empty((DEPTH, N_DEV) + w[k].shape[1:], BF16) for k in group_a]
    land_b = [lax.empty((DEPTH, N_DEV) + w[k].shape[1:], BF16) for k in group_b]
    land_b.append(lax.empty((DEPTH, N_DEV, small_rows + conv_rows, PACK_LANES), F32))
    pend_a = pend_b = None
    token = no_token
    for l in reversed(range(DEPTH)):
        dxm, carried, d_wgu, d_wd, d_wo = _layer_bwd_ffn(dx, qs[l], stash[l], token)
        srcs = [blocks(k, a) for k, a in zip(group_a, (d_wgu, d_wd, d_wo))]
        if pend_a is not None:
            land_a = _exchange_wait(pend_a, d_wo, f"grads_a_wait_{l + 1}", layer=l + 1)
        land_a = [_with_own_block(ld, a, l, me) for ld, a in zip(land_a, srcs)]
        pend_a = _exchange_start(["scatter"] * 3, srcs, land_a, d_wo, f"grads_a_start_{l}", layer=l)
        dx, d_win, small = _layer_bwd_mix(dxm, carried, qs[l], stash[l], pend_a[5])
        packed = _pack_layers([small[k][None] for k in replicated + ("conv_w",)])[0]
        srcs = [blocks("w_in", d_win)]
        if pend_b is not None:
            land_b = _exchange_wait(pend_b, d_win, f"grads_b_wait_{l + 1}", layer=l + 1)
        land_b = [_with_own_block(land_b[0], srcs[0], l, me),
                  lax.dynamic_update_slice(land_b[1], packed[None, None], (l, me, 0, 0))]
        pend_b = _exchange_start(["scatter", "spread"], srcs + [packed], land_b, d_win, f"grads_b_start_{l}",
                                 layer=l)
        token = pend_b[5]

    land_a = _exchange_wait(pend_a, token, "grads_a_wait_0", layer=0)
    res = {}
    for k, parts in zip(group_a, land_a):
        res[k] = _adamw(parts, w[k], m[k], v[k], _row_tile(w[k].shape[1]))
    land_b = _exchange_wait(pend_b, res["w_out"][0], "grads_b_wait_0", layer=0)
    res["w_in"] = _adamw(land_b[0], w["w_in"], m["w_in"], v["w_in"], _row_tile(w["w_in"].shape[1]))
    small_parts = land_b[1]
    updated = _adamw(small_parts, *(_pack_layers([t[k] for k in replicated]) for t in (w, m, v)), small_rows)
    unpacked = [_unpack_layers(o, [w[k].shape for k in replicated]) for o in updated]
    res.update({k: [u[i] for u in unpacked] for i, k in enumerate(replicated)})
    conv_parts = small_parts[:, :, small_rows:].reshape(DEPTH, N_DEV, -1)[:, :, :CONV_K * CONV_W]
    conv_parts = lax.dynamic_slice_in_dim(conv_parts.reshape(DEPTH, N_DEV, CONV_K, CONV_W), me * conv_cols,
                                          conv_cols, axis=3)
    res["conv_w"] = _adamw(conv_parts, w["conv_w"], m["conv_w"], v["conv_w"], CONV_K)
    for k in ("w_in", "w_gate_up"):
        res[k] = [jnp.swapaxes(a, 1, 2) for a in res[k]]

    return (loss, dx[None], *[res[k][0] for k in names], *[res[k][1] for k in names],
            *[res[k][2] for k in names], *[res[k][3] for k in names])
```

```python
import functools

import jax
import jax.numpy as jnp
from jax import lax
from jax.experimental import pallas as pl
from jax.experimental.pallas import tpu as pltpu

F32 = jnp.float32
BF16 = jnp.bfloat16

D_MODEL = 1024
DEPTH = 4
HEAD_DIM = 64
CONV_W = 256
SG_W = 256
SB_W = 512
IN_W = 2560
FFN_H = 2816
CONV_K = 31
CHUNK = 128
OFF_SG = 2 * CONV_W
OFF_SB = OFF_SG + 2 * SG_W
RMS_EPS = 1e-6
LN_EPS = 1e-5
N_DEV = 8
MESH = pl.DeviceIdType.MESH

ADAM_LR = 0.001
ADAM_B1 = 0.9
ADAM_B2 = 0.999
ADAM_EPS = 1e-08
ADAM_WD = 0.01
ADAM_STEP = 10

TOKEN_TILE = 256
VMEM_LIMIT = 56 * 1024 * 1024


def _cp(*sem):
    return pltpu.CompilerParams(dimension_semantics=sem or None, vmem_limit_bytes=VMEM_LIMIT)


def _dot(a, b):
    return jnp.dot(a, b, preferred_element_type=F32)


def _dot_nt(a, b):
    return lax.dot_general(a, b, (((1,), (1,)), ((), ())), preferred_element_type=F32)


def _dot_tn(a, b):
    return lax.dot_general(a, b, (((0,), (0,)), ((), ())), preferred_element_type=F32)


def _dot_split(x, m):
    hi = x.astype(BF16)
    lo = (x - hi.astype(F32)).astype(BF16)
    return _dot(hi, m) + _dot(lo, m)


def _group_mean_matrix(width, group):
    r = lax.broadcasted_iota(jnp.int32, (width, width), 0) // group
    c = lax.broadcasted_iota(jnp.int32, (width, width), 1) // group
    return jnp.where(r == c, 1.0 / group, 0.0).astype(BF16)


def _sigmoid(x):
    return 1.0 / (1.0 + jnp.exp(-x))


def _gelu(x):
    return 0.5 * x * (1.0 + lax.erf(x * (2.0 ** -0.5)))


def _gelu_grad(x):
    return 0.5 * (1.0 + lax.erf(x * (2.0 ** -0.5))) + x * jnp.exp(-0.5 * x * x) * (0.5 * (2.0 / jnp.pi) ** 0.5)


def _rms_stats(x):
    r = lax.rsqrt(jnp.mean(x * x, axis=-1, keepdims=True) + RMS_EPS)
    return r, x * r


def _rms_bwd(xh, r, g, dy):
    dxh = dy * g
    dx = r * (dxh - xh * jnp.mean(dxh * xh, axis=-1, keepdims=True))
    return dx, dy * xh


def _ln_stats(x):
    mu = jnp.mean(x, axis=-1, keepdims=True)
    xc = x - mu
    r = lax.rsqrt(jnp.mean(xc * xc, axis=-1, keepdims=True) + LN_EPS)
    return r, xc * r


def _ln_bwd(xh, r, g, dy):
    dxh = dy * g
    return r * (dxh - jnp.mean(dxh, axis=-1, keepdims=True) - xh * jnp.mean(dxh * xh, axis=-1, keepdims=True))


def _colsum(x):
    return jnp.sum(x, axis=0, keepdims=True)


def _rows(tm, n, j=0):
    return pl.BlockSpec((tm, n), lambda i: (i, j))


def _whole(shape):
    return pl.BlockSpec(shape, lambda i: (0,) * len(shape))


ORDER_ONLY = pl.BlockSpec(memory_space=pl.ANY)


def _stack_heads(a):
    even = (lax.broadcasted_iota(jnp.int32, a.shape, 1) % (2 * HEAD_DIM)) < HEAD_DIM
    top = jnp.where(even, a, 0.0)
    bot = jnp.where(even, 0.0, a)
    parts = []
    for c in range(a.shape[0] // CHUNK):
        rows = slice(c * CHUNK, (c + 1) * CHUNK)
        parts += [top[rows], bot[rows]]
    return jnp.concatenate(parts, axis=0)


def _store_stacked(st, st_ref, tr_ref):
    st_ref[...] = st.astype(BF16)
    for p in range(SB_W // CHUNK):
        for c in range(st.shape[0] // (2 * CHUNK)):
            tile = st[2 * c * CHUNK:2 * (c + 1) * CHUNK, p * CHUNK:(p + 1) * CHUNK]
            tr_ref[p, c] = tile.T.astype(BF16)


def _load_transposed(tr_ref):
    rows = []
    for c in range(tr_ref.shape[1]):
        tiles = [tr_ref[p, c].T for p in range(SB_W // CHUNK)]
        rows.append(jnp.concatenate(tiles, axis=1))
    return jnp.concatenate(rows, axis=0)


def _unstack_heads(st):
    even = (lax.broadcasted_iota(jnp.int32, (CHUNK, st.shape[1]), 1) % (2 * HEAD_DIM)) < HEAD_DIM
    parts = []
    for c in range(st.shape[0] // (2 * CHUNK)):
        top = st[2 * c * CHUNK:(2 * c + 1) * CHUNK]
        bot = st[(2 * c + 1) * CHUNK:(2 * c + 2) * CHUNK]
        parts.append(jnp.where(even, top, bot))
    return jnp.concatenate(parts, axis=0)


def _fwd_in(x, g, w, qg, kg, tok):
    s = x.shape[0]
    tm = TOKEN_TILE

    def body(x_ref, g_ref, w_ref, qg_ref, kg_ref, tok_ref, proj_ref, qn_ref, kn_ref, vb_ref, kt_ref, vt_ref):
        r, xh = _rms_stats(x_ref[...])
        h = (xh * g_ref[...]).astype(BF16)
        proj = _dot_nt(h, w_ref[...])
        proj_ref[...] = proj
        gm = _group_mean_matrix(SB_W, HEAD_DIM)
        q = proj[:, OFF_SB:OFF_SB + SB_W]
        k = proj[:, OFF_SB + SB_W:OFF_SB + 2 * SB_W]
        rq = lax.rsqrt(_dot_split(q * q, gm) + RMS_EPS)
        rk = lax.rsqrt(_dot_split(k * k, gm) + RMS_EPS)
        qn_ref[...] = (q * rq * qg_ref[...] * (HEAD_DIM ** -0.5)).astype(BF16)
        _store_stacked(_stack_heads(k * rk * kg_ref[...]), kn_ref, kt_ref)
        _store_stacked(_stack_heads(proj[:, OFF_SB + 2 * SB_W:]), vb_ref, vt_ref)

    tiles = pl.BlockSpec((SB_W // CHUNK, tm // CHUNK, CHUNK, PAIR), lambda i: (0, i, 0, 0))
    tiles_shape = jax.ShapeDtypeStruct((SB_W // CHUNK, s // CHUNK, CHUNK, PAIR), BF16)
    return pl.pallas_call(
        body, name="fwd_in", grid=(s // tm,),
        in_specs=[_rows(tm, D_MODEL), _whole((1, D_MODEL)), _whole((IN_W, D_MODEL)),
                  _whole((1, SB_W)), _whole((1, SB_W)), ORDER_ONLY],
        out_specs=[_rows(tm, IN_W), _rows(tm, SB_W), _rows(2 * tm, SB_W), _rows(2 * tm, SB_W), tiles, tiles],
        out_shape=[jax.ShapeDtypeStruct((s, IN_W), F32), jax.ShapeDtypeStruct((s, SB_W), BF16),
                   jax.ShapeDtypeStruct((2 * s, SB_W), BF16), jax.ShapeDtypeStruct((2 * s, SB_W), BF16),
                   tiles_shape, tiles_shape],
        compiler_params=_cp("parallel"),
    )(x, g, w, qg, kg, tok)


def _conv_window(abuf, r0, w_ref):
    win = abuf[pl.ds(pl.multiple_of(r0 + CHUNK - 32, 32), CHUNK + 32), :]
    acc = jnp.zeros((CHUNK, CONV_W), F32)
    for k in range(CONV_K):
        sh = win if k == 0 else pltpu.roll(win, k, axis=0)
        acc = acc + sh[32:, :] * w_ref[CONV_K - 1 - k:CONV_K - k, :]
    return acc, win


def _glu_fill(p_ref, abuf, s):
    abuf[0:CHUNK, :] = jnp.zeros((CHUNK, CONV_W), F32)

    def fill(c, carry):
        r0 = pl.multiple_of(c * CHUNK, CHUNK)
        pv = p_ref[pl.ds(r0, CHUNK), :]
        abuf[pl.ds(r0 + CHUNK, CHUNK), :] = pv[:, :CONV_W] * _sigmoid(pv[:, CONV_W:])
        return carry

    lax.fori_loop(0, s // CHUNK, fill, 0)


def _fwd_conv(proj, w, b, lg, lb, tok):
    s = proj.shape[0]

    def body(p_ref, w_ref, b_ref, lg_ref, lb_ref, tok_ref, y_ref, abuf):
        _glu_fill(p_ref, abuf, s)

        def chunk(c, carry):
            r0 = pl.multiple_of(c * CHUNK, CHUNK)
            acc, _ = _conv_window(abuf, r0, w_ref)
            r, xh = _ln_stats(acc + b_ref[...])
            ln = xh * lg_ref[...] + lb_ref[...]
            y_ref[pl.ds(r0, CHUNK), :] = ln * _sigmoid(ln)
            return carry

        lax.fori_loop(0, s // CHUNK, chunk, 0)

    return pl.pallas_call(
        body, name="fwd_conv", grid=(1,),
        in_specs=[pl.BlockSpec((s, 2 * CONV_W), lambda i: (0, 0)), _whole((CONV_K, CONV_W)),
                  _whole((1, CONV_W)), _whole((1, CONV_W)), _whole((1, CONV_W)), ORDER_ONLY],
        out_specs=_whole((s, CONV_W)),
        out_shape=jax.ShapeDtypeStruct((s, CONV_W), F32),
        scratch_shapes=[pltpu.VMEM((s + CHUNK, CONV_W), F32)],
        compiler_params=_cp("arbitrary"),
    )(proj, w, b, lg, lb, tok)


def _sg_masks():
    row = lax.broadcasted_iota(jnp.int32, (CHUNK, CHUNK), 0)
    col = lax.broadcasted_iota(jnp.int32, (CHUNK, CHUNK), 1)
    lane_head = lax.broadcasted_iota(jnp.int32, (CHUNK, SG_W), 1) // HEAD_DIM
    return row >= col, lane_head


def _sg_mix(w_ref, bias_ref, vc, tril, lane_head):
    mixed = bias_ref[...]
    for h in range(SG_W // HEAD_DIM):
        wm = jnp.where(tril, w_ref[h], 0.0).astype(BF16)
        mixed = mixed + jnp.where(lane_head == h, _dot(wm, vc), 0.0)
    return mixed


def _fwd_sg(proj, lg, lb, w, bias):
    s = proj.shape[0]
    tm = TOKEN_TILE

    def body(p_ref, lg_ref, lb_ref, w_ref, bias_ref, y_ref):
        ge = _gelu(p_ref[...])
        u = ge[:, :SG_W]
        r, xh = _ln_stats(ge[:, SG_W:])
        vln = (xh * lg_ref[...] + lb_ref[...]).astype(BF16)
        tril, lane_head = _sg_masks()
        for c in range(tm // CHUNK):
            rows = slice(c * CHUNK, (c + 1) * CHUNK)
            y_ref[rows, :] = u[rows] * _sg_mix(w_ref, bias_ref, vln[rows], tril, lane_head)

    return pl.pallas_call(
        body, name="fwd_sg", grid=(s // tm,),
        in_specs=[_rows(tm, 2 * SG_W, 1), _whole((1, SG_W)), _whole((1, SG_W)),
                  _whole((SG_W // HEAD_DIM, CHUNK, CHUNK)), _whole((CHUNK, SG_W))],
        out_specs=_rows(tm, SG_W),
        out_shape=jax.ShapeDtypeStruct((s, SG_W), F32),
        compiler_params=_cp("parallel"),
    )(proj, lg, lb, w, bias)


SB_Q = 2 * CHUNK
PAIR = 2 * CHUNK
SB_PAIRS = 2


def _pair_tri(kind):
    row = lax.broadcasted_iota(jnp.int32, (PAIR, PAIR), 0)
    col = lax.broadcasted_iota(jnp.int32, (PAIR, PAIR), 1)
    tri = {"after": row > col, "upto": row <= col, "before": row < col}[kind]
    return jnp.where(((row // CHUNK) == (col // CHUNK)) & tri, 1.0, 0.0).astype(BF16)


def _sb_scores(z, qpos0, kpos0, masked):
    sp = jnp.maximum(z, 0.0) + jnp.log(1.0 + jnp.exp(-jnp.abs(z)))
    if not masked:
        return z, sp, sp.astype(BF16), None
    row = lax.broadcasted_iota(jnp.int32, z.shape, 0)
    col = lax.broadcasted_iota(jnp.int32, z.shape, 1) % CHUNK
    mask = (kpos0 + col) < (qpos0 + row)
    return z, sp, jnp.where(mask, sp, 0.0).astype(BF16), mask


def _per_head(c0, c1):
    return jnp.concatenate([jnp.broadcast_to(c0, (SB_Q, CHUNK)), jnp.broadcast_to(c1, (SB_Q, CHUNK))], axis=1)


def _fwd_sb(qn, ktr, vst):
    s = qn.shape[0]
    np_ = SB_PAIRS

    def body(q_ref, k_ref, v_ref, after_ref, o_ref, lt_ref, z_buf, att_buf):
        i = pl.program_id(1)
        last = 2 * i + 1
        after = after_ref[...]
        lanes = [slice(pr * CHUNK, (pr + 1) * CHUNK) for pr in range(np_)]
        qs = [q_ref[:, lanes[pr]] for pr in range(np_)]

        def rows(kb):
            return pl.ds(pl.multiple_of(kb * PAIR, PAIR), PAIR)

        def block(kb, carry, masked):
            out = []
            for pr in range(np_):
                acc, c0, c1 = carry[pr]
                z_next = _dot(qs[pr], k_ref[pr, jnp.maximum(kb - 1, 0)])
                pv = _dot(att_buf[pr], v_ref[rows(jnp.minimum(kb + 1, last)), lanes[pr]])
                z, sp, nlb, mask = _sb_scores(z_buf[pr], i * SB_Q, kb * CHUNK, masked)
                loc = _dot(nlb, after)
                att = jnp.exp(z - sp - loc - _per_head(c0, c1))
                if masked:
                    att = jnp.where(mask, att, 0.0)
                z_buf[pr] = z_next
                att_buf[pr] = att.astype(BF16)
                out.append((acc + pv, c0 + loc[:, 0:1] + nlb[:, 0:1].astype(F32),
                            c1 + loc[:, CHUNK:CHUNK + 1] + nlb[:, CHUNK:CHUNK + 1].astype(F32)))
            return tuple(out)

        for pr in range(np_):
            z_buf[pr] = _dot(qs[pr], k_ref[pr, last])
        att_buf[...] = jnp.zeros_like(att_buf)
        zero = jnp.zeros((SB_Q, 1), F32)
        carry = ((jnp.zeros((SB_Q, CHUNK), F32), zero, zero),) * np_
        carry = block(last, carry, True)
        carry = block(last - 1, carry, True)
        carry = lax.fori_loop(0, 2 * i, lambda j, c: block(2 * i - 1 - j, c, False), carry)
        for pr, (acc, c0, c1) in enumerate(carry):
            o_ref[:, lanes[pr]] = acc + _dot(att_buf[pr], v_ref[rows(0), lanes[pr]])
            lt_ref[:, lanes[pr]] = jnp.concatenate([jnp.broadcast_to(c0, (SB_Q, HEAD_DIM)),
                                                    jnp.broadcast_to(c1, (SB_Q, HEAD_DIM))], axis=1)

    blk = pl.BlockSpec((SB_Q, np_ * CHUNK), lambda p, i: (i, p))
    seq = pl.BlockSpec((2 * s, np_ * CHUNK), lambda p, i: (0, p))
    return pl.pallas_call(
        body, name="fwd_sb", grid=(SB_W // CHUNK // np_, s // SB_Q),
        in_specs=[blk, pl.BlockSpec((np_, s // CHUNK, CHUNK, PAIR), lambda p, i: (p, 0, 0, 0)), seq,
                  pl.BlockSpec((PAIR, PAIR), lambda p, i: (0, 0))],
        out_specs=[blk, blk],
        out_shape=[jax.ShapeDtypeStruct((s, SB_W), F32)] * 2,
        scratch_shapes=[pltpu.VMEM((np_, SB_Q, PAIR), F32), pltpu.VMEM((np_, SB_Q, PAIR), BF16)],
        compiler_params=_cp("parallel", "parallel"),
    )(qn, ktr, vst, _pair_tri("after"))


def _group_norms(yc, ys, yb):
    return [_rms_stats(yc), _rms_stats(ys), _rms_stats(yb)]


def _fwd_out(yc, ys, yb, g, w, x, tok):
    s = x.shape[0]
    tm = TOKEN_TILE

    def body(yc_ref, ys_ref, yb_ref, g_ref, w_ref, x_ref, tok_ref, o_ref):
        stats = _group_norms(yc_ref[...], ys_ref[...], yb_ref[...])
        cat = jnp.concatenate([xh for _, xh in stats], axis=1) * g_ref[...]
        o_ref[...] = x_ref[...] + _dot(cat.astype(BF16), w_ref[...])

    return pl.pallas_call(
        body, name="fwd_out", grid=(s // tm,),
        in_specs=[_rows(tm, CONV_W), _rows(tm, SG_W), _rows(tm, SB_W), _whole((1, D_MODEL)),
                  _whole((D_MODEL, D_MODEL)), _rows(tm, D_MODEL), ORDER_ONLY],
        out_specs=_rows(tm, D_MODEL),
        out_shape=jax.ShapeDtypeStruct((s, D_MODEL), F32),
        compiler_params=_cp("parallel"),
    )(yc, ys, yb, g, w, x, tok)


def _fwd_ffn(x, g, wgu, wd, tok):
    s = x.shape[0]
    tm = TOKEN_TILE

    def body(x_ref, g_ref, wgu_ref, wd_ref, tok_ref, gu_ref, o_ref):
        x = x_ref[...]
        r, xh = _rms_stats(x)
        gu = _dot_nt((xh * g_ref[...]).astype(BF16), wgu_ref[...])
        gu_ref[...] = gu
        gate = gu[:, :FFN_H]
        act = gate * _sigmoid(gate) * gu[:, FFN_H:]
        o_ref[...] = x + _dot(act.astype(BF16), wd_ref[...])

    return pl.pallas_call(
        body, name="fwd_ffn", grid=(s // tm,),
        in_specs=[_rows(tm, D_MODEL), _whole((1, D_MODEL)),
                  pl.BlockSpec((2 * FFN_H, D_MODEL), lambda i: (0, 0), pipeline_mode=pl.Buffered(1)),
                  pl.BlockSpec((FFN_H, D_MODEL), lambda i: (0, 0), pipeline_mode=pl.Buffered(1)), ORDER_ONLY],
        out_specs=[_rows(tm, 2 * FFN_H), _rows(tm, D_MODEL)],
        out_shape=[jax.ShapeDtypeStruct((s, 2 * FFN_H), F32), jax.ShapeDtypeStruct((s, D_MODEL), F32)],
        compiler_params=_cp("parallel"),
    )(x, g, wgu, wd, tok)


def _loss_head(y, target):
    s = y.shape[0]
    tm = TOKEN_TILE

    def body(y_ref, t_ref, l_ref, d_ref):
        @pl.when(pl.program_id(0) == 0)
        def _():
            l_ref[...] = jnp.zeros_like(l_ref)

        err = y_ref[...] - t_ref[...]
        d_ref[...] = err * (1.0 / D_MODEL)
        l_ref[...] += 0.5 * jnp.sum(jnp.mean(err * err, axis=-1, keepdims=True), axis=0, keepdims=True)

    return pl.pallas_call(
        body, name="loss_head", grid=(s // tm,),
        in_specs=[_rows(tm, D_MODEL), _rows(tm, D_MODEL)],
        out_specs=[_whole((1, 1)), _rows(tm, D_MODEL)],
        out_shape=[jax.ShapeDtypeStruct((1, 1), F32), jax.ShapeDtypeStruct((s, D_MODEL), F32)],
        compiler_params=_cp("arbitrary"),
    )(y, target)


def _accumulate(ref, value):
    @pl.when(pl.program_id(0) == 0)
    def _():
        ref[...] = jnp.zeros_like(ref)

    ref[...] += value


def _bwd_ffn(dxo, gu, xm, g, wgu, wd, tok):
    s = dxo.shape[0]
    tm = TOKEN_TILE

    def body(dxo_ref, gu_ref, xm_ref, g_ref, wgu_ref, wd_ref, tok_ref, dgu_ref, act_ref, h_ref, dxm_ref, dg_ref):
        dxo = dxo_ref[...]
        gu = gu_ref[...]
        gate, up = gu[:, :FFN_H], gu[:, FFN_H:]
        sg = _sigmoid(gate)
        sl = gate * sg
        act_ref[...] = (sl * up).astype(BF16)
        dact = _dot_nt(dxo.astype(BF16), wd_ref[...])
        dgate = dact * up * (sg * (1.0 + gate * (1.0 - sg)))
        dgu = jnp.concatenate([dgate, dact * sl], axis=1).astype(BF16)
        dgu_ref[...] = dgu
        dh = _dot(dgu, wgu_ref[...])
        r, xh = _rms_stats(xm_ref[...])
        h_ref[...] = (xh * g_ref[...]).astype(BF16)
        dx, dgrow = _rms_bwd(xh, r, g_ref[...], dh)
        dxm_ref[...] = dxo + dx
        _accumulate(dg_ref, _colsum(dgrow))

    return pl.pallas_call(
        body, name="bwd_ffn", grid=(s // tm,),
        in_specs=[_rows(tm, D_MODEL), _rows(tm, 2 * FFN_H), _rows(tm, D_MODEL), _whole((1, D_MODEL)),
                  pl.BlockSpec((2 * FFN_H, D_MODEL), lambda i: (0, 0), pipeline_mode=pl.Buffered(1)),
                  pl.BlockSpec((FFN_H, D_MODEL), lambda i: (0, 0), pipeline_mode=pl.Buffered(1)), ORDER_ONLY],
        out_specs=[_rows(tm, 2 * FFN_H), _rows(tm, FFN_H), _rows(tm, D_MODEL), _rows(tm, D_MODEL),
                   _whole((1, D_MODEL))],
        out_shape=[jax.ShapeDtypeStruct((s, 2 * FFN_H), BF16), jax.ShapeDtypeStruct((s, FFN_H), BF16),
                   jax.ShapeDtypeStruct((s, D_MODEL), BF16), jax.ShapeDtypeStruct((s, D_MODEL), F32),
                   jax.ShapeDtypeStruct((1, D_MODEL), F32)],
        compiler_params=_cp("arbitrary"),
    )(dxo, gu, xm, g, wgu, wd, tok)


def _matmul_tn(a, b, tm, tn, out_dtype=BF16):
    s, m = a.shape
    n = b.shape[1]

    def body(a_ref, b_ref, o_ref):
        o_ref[...] = _dot_tn(a_ref[...].astype(BF16), b_ref[...].astype(BF16)).astype(out_dtype)

    return pl.pallas_call(
        body, name="weight_grad", grid=(m // tm, n // tn),
        in_specs=[pl.BlockSpec((s, tm), lambda i, j: (0, i)), pl.BlockSpec((s, tn), lambda i, j: (0, j))],
        out_specs=pl.BlockSpec((tm, tn), lambda i, j: (i, j)),
        out_shape=jax.ShapeDtypeStruct((m, n), out_dtype),
        compiler_params=_cp("parallel", "parallel"),
    )(a, b)


def _bwd_out(dxm, yc, ys, yb, g, w, tok):
    s = dxm.shape[0]
    tm = TOKEN_TILE

    def body(dxm_ref, yc_ref, ys_ref, yb_ref, g_ref, w_ref, tok_ref, dyc_ref, dys_ref, dyb_ref, cat_ref, dg_ref):
        stats = _group_norms(yc_ref[...], ys_ref[...], yb_ref[...])
        g = g_ref[...]
        cat_ref[...] = (jnp.concatenate([xh for _, xh in stats], axis=1) * g).astype(BF16)
        dcat = _dot_nt(dxm_ref[...].astype(BF16), w_ref[...])
        dgs = []
        off = 0
        for (r, xh), out in zip(stats, (dyc_ref, dys_ref, dyb_ref)):
            cols = slice(off, off + xh.shape[1])
            dx, dgrow = _rms_bwd(xh, r, g[:, cols], dcat[:, cols])
            out[...] = dx
            dgs.append(_colsum(dgrow))
            off += xh.shape[1]
        _accumulate(dg_ref, jnp.concatenate(dgs, axis=1))

    return pl.pallas_call(
        body, name="bwd_out", grid=(s // tm,),
        in_specs=[_rows(tm, D_MODEL), _rows(tm, CONV_W), _rows(tm, SG_W), _rows(tm, SB_W),
                  _whole((1, D_MODEL)), _whole((D_MODEL, D_MODEL)), ORDER_ONLY],
        out_specs=[_rows(tm, CONV_W), _rows(tm, SG_W), _rows(tm, SB_W), _rows(tm, D_MODEL),
                   _whole((1, D_MODEL))],
        out_shape=[jax.ShapeDtypeStruct((s, CONV_W), F32), jax.ShapeDtypeStruct((s, SG_W), F32),
                   jax.ShapeDtypeStruct((s, SB_W), F32), jax.ShapeDtypeStruct((s, D_MODEL), BF16),
                   jax.ShapeDtypeStruct((1, D_MODEL), F32)],
        compiler_params=_cp("arbitrary"),
    )(dxm, yc, ys, yb, g, w, tok)


def _bwd_sb(qn, kst, ktr, vtr, dy, ltot, tok):
    s = qn.shape[0]

    def body(q_ref, k_ref, kt_ref, vt_ref, do_ref, lt_ref, upto_ref, before_ref, tok_ref, dq_ref, dk_ref, dv_ref,
             z_buf, da_buf, dz_buf, att_buf):
        i = pl.program_id(1)
        last = 2 * i + 1

        @pl.when(i == 0)
        def _():
            dk_ref[...] = jnp.zeros_like(dk_ref)
            dv_ref[...] = jnp.zeros_like(dv_ref)

        q = q_ref[...]
        do = do_ref[...]
        dob = do.astype(BF16)
        q_t = q.astype(F32).T.astype(BF16)
        do_t = do.T.astype(BF16)
        lt = lt_ref[...]
        ltot = _per_head(lt[:, 0:1], lt[:, HEAD_DIM:HEAD_DIM + 1])
        upto = upto_ref[...]
        before = before_ref[...]
        last0, last1 = slice(CHUNK - 1, CHUNK), slice(PAIR - 1, PAIR)

        def rows(kb):
            return pl.ds(pl.multiple_of(kb * PAIR, PAIR), PAIR)

        def ahead(kb):
            return _dot(q, kt_ref[kb]), _dot(dob, vt_ref[kb])

        def behind(kb, dq):
            dzb = dz_buf[...]
            dk_ref[kb] += _dot(q_t, dzb)
            dv_ref[kb] += _dot(do_t, att_buf[...])
            return dq + _dot(dzb, k_ref[rows(kb), :])

        def block(kb, carry, masked):
            dq, p0, p1, e0, e1 = carry
            z_next, da_next = ahead(jnp.minimum(kb + 1, last))
            dq = behind(jnp.maximum(kb - 1, 0), dq)
            z, sp, nlb, mask = _sb_scores(z_buf[...], i * SB_Q, kb * CHUNK, masked)
            pin = _dot(nlb, upto) + _per_head(p0, p1)
            sig = jnp.exp(z - sp)
            att = jnp.exp(z - sp - (ltot - pin))
            if masked:
                att = jnp.where(mask, att, 0.0)
            e = att * da_buf[...]
            ebefore = _dot(e.astype(BF16), before) + _per_head(e0, e1)
            dz = e - sig * (e + ebefore)
            if masked:
                dz = jnp.where(mask, dz, 0.0)
            z_buf[...] = z_next
            da_buf[...] = da_next
            dz_buf[...] = dz.astype(BF16)
            att_buf[...] = att.astype(BF16)
            return (dq, pin[:, last0], pin[:, last1],
                    ebefore[:, last0] + e[:, last0], ebefore[:, last1] + e[:, last1])

        z_buf[...], da_buf[...] = ahead(0)
        dz_buf[...] = jnp.zeros_like(dz_buf)
        att_buf[...] = jnp.zeros_like(att_buf)
        zero = jnp.zeros((SB_Q, 1), F32)
        carry = (jnp.zeros((SB_Q, CHUNK), F32), zero, zero, zero, zero)
        carry = lax.fori_loop(0, 2 * i, lambda kb, c: block(kb, c, False), carry)
        carry = block(last - 1, carry, True)
        carry = block(last, carry, True)
        dq_ref[...] = behind(last, carry[0])

    blk = pl.BlockSpec((SB_Q, CHUNK), lambda p, i: (i, p))
    seq = pl.BlockSpec((2 * s, CHUNK), lambda p, i: (0, p))
    tiles = pl.BlockSpec((None, s // CHUNK, CHUNK, PAIR), lambda p, i: (p, 0, 0, 0))
    tri = pl.BlockSpec((PAIR, PAIR), lambda p, i: (0, 0))
    return pl.pallas_call(
        body, name="bwd_sb", grid=(SB_W // CHUNK, s // SB_Q),
        in_specs=[blk, seq, tiles, tiles, blk, blk, tri, tri, ORDER_ONLY],
        out_specs=[blk, tiles, tiles],
        out_shape=[jax.ShapeDtypeStruct((s, SB_W), F32)]
        + [jax.ShapeDtypeStruct((SB_W // CHUNK, s // CHUNK, CHUNK, PAIR), F32)] * 2,
        scratch_shapes=[pltpu.VMEM((SB_Q, PAIR), F32), pltpu.VMEM((SB_Q, PAIR), F32),
                        pltpu.VMEM((SB_Q, PAIR), BF16), pltpu.VMEM((SB_Q, PAIR), BF16)],
        compiler_params=_cp("parallel", "arbitrary"),
    )(qn, kst, ktr, vtr, dy, ltot, _pair_tri("upto"), _pair_tri("before"), tok)


def _head_sum(row):
    acc = row[:, 0:HEAD_DIM]
    for h in range(1, SB_W // HEAD_DIM):
        acc = acc + row[:, h * HEAD_DIM:(h + 1) * HEAD_DIM]
    return acc


def _bwd_qk(proj, dqs, dkn, dv, qg, kg):
    s = proj.shape[0]
    tm = TOKEN_TILE
    tiles = pl.BlockSpec((SB_W // CHUNK, tm // CHUNK, CHUNK, PAIR), lambda i: (0, i, 0, 0))

    def body(q_ref, k_ref, dqs_ref, dkn_ref, dv_ref, qg_ref, kg_ref, dp_ref, dqg_ref, dkg_ref, qacc, kacc):
        i = pl.program_id(0)
        gm = _group_mean_matrix(SB_W, HEAD_DIM)

        def one(x, dy, g, acc):
            r = lax.rsqrt(_dot_split(x * x, gm) + RMS_EPS)
            xh = x * r
            dxh = dy * g
            _accumulate(acc, _colsum(dy * xh))
            return r * (dxh - xh * _dot_split(dxh * xh, gm))

        dq = one(q_ref[...], dqs_ref[...] * (HEAD_DIM ** -0.5), qg_ref[...], qacc)
        dk = one(k_ref[...], _unstack_heads(_load_transposed(dkn_ref)), kg_ref[...], kacc)
        dp_ref[...] = jnp.concatenate([dq, dk, _unstack_heads(_load_transposed(dv_ref))], axis=1).astype(BF16)

        @pl.when(i == pl.num_programs(0) - 1)
        def _():
            dqg_ref[...] = _head_sum(qacc[...])
            dkg_ref[...] = _head_sum(kacc[...])

    return pl.pallas_call(
        body, name="bwd_qk", grid=(s // tm,),
        in_specs=[_rows(tm, SB_W, OFF_SB // SB_W), _rows(tm, SB_W, OFF_SB // SB_W + 1),
                  _rows(tm, SB_W), tiles, tiles, _whole((1, SB_W)), _whole((1, SB_W))],
        out_specs=[_rows(tm, 3 * SB_W), _whole((1, HEAD_DIM)), _whole((1, HEAD_DIM))],
        out_shape=[jax.ShapeDtypeStruct((s, 3 * SB_W), BF16), jax.ShapeDtypeStruct((1, HEAD_DIM), F32),
                   jax.ShapeDtypeStruct((1, HEAD_DIM), F32)],
        scratch_shapes=[pltpu.VMEM((1, SB_W), F32), pltpu.VMEM((1, SB_W), F32)],
        compiler_params=_cp("arbitrary"),
    )(proj, proj, dqs, dkn, dv, qg, kg)


def _bwd_sg(proj, dy, lg, lb, w, bias):
    s = proj.shape[0]
    tm = TOKEN_TILE
    nh = SG_W // HEAD_DIM

    def body(p_ref, dy_ref, lg_ref, lb_ref, w_ref, bias_ref, dp_ref, dlg_ref, dlb_ref, dw_ref, db_ref, dbias):
        i = pl.program_id(0)
        uv = p_ref[...]
        ge = _gelu(uv)
        u = ge[:, :SG_W]
        r, xh = _ln_stats(ge[:, SG_W:])
        vln = (xh * lg_ref[...] + lb_ref[...]).astype(BF16)
        dy = dy_ref[...]
        tril, lane_head = _sg_masks()

        @pl.when(i == 0)
        def _():
            dw_ref[...] = jnp.zeros_like(dw_ref)
            dbias[...] = jnp.zeros_like(dbias)

        dus, dvlns = [], []
        for c in range(tm // CHUNK):
            rows = slice(c * CHUNK, (c + 1) * CHUNK)
            vc = vln[rows]
            dus.append(dy[rows] * _sg_mix(w_ref, bias_ref, vc, tril, lane_head))
            dm = dy[rows] * u[rows]
            dbias[...] += dm
            dvc = jnp.zeros((CHUNK, SG_W), F32)
            for h in range(nh):
                dmh = jnp.where(lane_head == h, dm, 0.0).astype(BF16)
                dw_ref[h] += jnp.where(tril, _dot_nt(dmh, vc), 0.0)
                wm = jnp.where(tril, w_ref[h], 0.0).astype(BF16)
                dvc = dvc + _dot_tn(wm, dmh)
            dvlns.append(dvc)
        du = jnp.concatenate(dus, axis=0)
        dvln = jnp.concatenate(dvlns, axis=0)
        _accumulate(dlg_ref, _colsum(dvln * xh))
        _accumulate(dlb_ref, _colsum(dvln))
        dv = _ln_bwd(xh, r, lg_ref[...], dvln)
        dp_ref[...] = (jnp.concatenate([du, dv], axis=1) * _gelu_grad(uv)).astype(BF16)

        @pl.when(i == pl.num_programs(0) - 1)
        def _():
            lane = lax.broadcasted_iota(jnp.int32, (CHUNK, CHUNK), 1)
            acc = dbias[...]
            out = jnp.zeros((CHUNK, CHUNK), F32)
            for h in range(nh):
                hs = jnp.sum(acc[:, h * HEAD_DIM:(h + 1) * HEAD_DIM], axis=1, keepdims=True)
                out = out + jnp.where(lane == h, hs, 0.0)
            db_ref[...] = out

    return pl.pallas_call(
        body, name="bwd_sg", grid=(s // tm,),
        in_specs=[_rows(tm, 2 * SG_W, 1), _rows(tm, SG_W), _whole((1, SG_W)), _whole((1, SG_W)),
                  _whole((nh, CHUNK, CHUNK)), _whole((CHUNK, SG_W))],
        out_specs=[_rows(tm, 2 * SG_W), _whole((1, SG_W)), _whole((1, SG_W)), _whole((nh, CHUNK, CHUNK)),
                   _whole((CHUNK, CHUNK))],
        out_shape=[jax.ShapeDtypeStruct((s, 2 * SG_W), BF16), jax.ShapeDtypeStruct((1, SG_W), F32),
                   jax.ShapeDtypeStruct((1, SG_W), F32), jax.ShapeDtypeStruct((nh, CHUNK, CHUNK), F32),
                   jax.ShapeDtypeStruct((CHUNK, CHUNK), F32)],
        scratch_shapes=[pltpu.VMEM((CHUNK, SG_W), F32)],
        compiler_params=_cp("arbitrary"),
    )(proj, dy, lg, lb, w, bias)


def _bwd_conv(proj, dy, w, b, lg, lb):
    s = proj.shape[0]

    def body(p_ref, dy_ref, w_ref, b_ref, lg_ref, lb_ref, dp_ref, dw_ref, db_ref, dlg_ref, dlb_ref,
             abuf, dcbuf):
        _glu_fill(p_ref, abuf, s)
        dcbuf[pl.ds(s, CHUNK), :] = jnp.zeros((CHUNK, CONV_W), F32)
        dw_ref[...] = jnp.zeros_like(dw_ref)

        def chunk(c, carry):
            db, dlg, dlb = carry
            r0 = pl.multiple_of(c * CHUNK, CHUNK)
            acc, win = _conv_window(abuf, r0, w_ref)
            r, xh = _ln_stats(acc + b_ref[...])
            ln = xh * lg_ref[...] + lb_ref[...]
            sg = _sigmoid(ln)
            dl = dy_ref[pl.ds(r0, CHUNK), :] * (sg * (1.0 + ln * (1.0 - sg)))
            dc = _ln_bwd(xh, r, lg_ref[...], dl)
            dcbuf[pl.ds(r0, CHUNK), :] = dc
            for k in range(CONV_K):
                sh = win if k == 0 else pltpu.roll(win, k, axis=0)
                dw_ref[CONV_K - 1 - k:CONV_K - k, :] += _colsum(dc * sh[32:, :])
            return db + _colsum(dc), dlg + _colsum(dl * xh), dlb + _colsum(dl)

        zero = jnp.zeros((1, CONV_W), F32)
        db, dlg, dlb = lax.fori_loop(0, s // CHUNK, chunk, (zero, zero, zero))
        db_ref[...] = db
        dlg_ref[...] = dlg
        dlb_ref[...] = dlb

        def chunk_back(c, carry):
            r0 = pl.multiple_of(c * CHUNK, CHUNK)
            win = dcbuf[pl.ds(r0, CHUNK + 32), :]
            da = jnp.zeros((CHUNK, CONV_W), F32)
            for k in range(CONV_K):
                sh = win if k == 0 else pltpu.roll(win, CHUNK + 32 - k, axis=0)
                da = da + sh[:CHUNK, :] * w_ref[CONV_K - 1 - k:CONV_K - k, :]
            pv = p_ref[pl.ds(r0, CHUNK), :]
            val, sg = pv[:, :CONV_W], _sigmoid(pv[:, CONV_W:])
            dp_ref[pl.ds(r0, CHUNK), :] = jnp.concatenate([da * sg, da * val * sg * (1.0 - sg)], axis=1).astype(BF16)
            return carry

        lax.fori_loop(0, s // CHUNK, chunk_back, 0)

    row = _whole((1, CONV_W))
    return pl.pallas_call(
        body, name="bwd_conv", grid=(1,),
        in_specs=[pl.BlockSpec((s, 2 * CONV_W), lambda i: (0, 0)), _whole((s, CONV_W)),
                  _whole((CONV_K, CONV_W)), row, row, row],
        out_specs=[_whole((s, 2 * CONV_W)), _whole((CONV_K, CONV_W)), row, row, row],
        out_shape=[jax.ShapeDtypeStruct((s, 2 * CONV_W), BF16), jax.ShapeDtypeStruct((CONV_K, CONV_W), F32)]
        + [jax.ShapeDtypeStruct((1, CONV_W), F32)] * 3,
        scratch_shapes=[pltpu.VMEM((s + CHUNK, CONV_W), F32), pltpu.VMEM((s + CHUNK, CONV_W), F32)],
        compiler_params=_cp("arbitrary"),
    )(proj, dy, w, b, lg, lb)


def _bwd_in(dpc, dps, dpb, x, g, w, dxm):
    s = x.shape[0]
    tm = TOKEN_TILE

    def body(dpc_ref, dps_ref, dpb_ref, x_ref, g_ref, w_ref, dxm_ref, dx_ref, h_ref, dp_ref, dg_ref):
        dp = jnp.concatenate([dpc_ref[...], dps_ref[...], dpb_ref[...]], axis=1)
        dp_ref[...] = dp
        dh = _dot(dp, w_ref[...])
        r, xh = _rms_stats(x_ref[...])
        h_ref[...] = (xh * g_ref[...]).astype(BF16)
        dx, dgrow = _rms_bwd(xh, r, g_ref[...], dh)
        dx_ref[...] = dxm_ref[...] + dx
        _accumulate(dg_ref, _colsum(dgrow))

    return pl.pallas_call(
        body, name="bwd_in", grid=(s // tm,),
        in_specs=[_rows(tm, 2 * CONV_W), _rows(tm, 2 * SG_W), _rows(tm, 3 * SB_W), _rows(tm, D_MODEL),
                  _whole((1, D_MODEL)), _whole((IN_W, D_MODEL)), _rows(tm, D_MODEL)],
        out_specs=[_rows(tm, D_MODEL), _rows(tm, D_MODEL), _rows(tm, IN_W), _whole((1, D_MODEL))],
        out_shape=[jax.ShapeDtypeStruct((s, D_MODEL), F32), jax.ShapeDtypeStruct((s, D_MODEL), BF16),
                   jax.ShapeDtypeStruct((s, IN_W), BF16), jax.ShapeDtypeStruct((1, D_MODEL), F32)],
        compiler_params=_cp("arbitrary"),
    )(dpc, dps, dpb, x, g, w, dxm)


SMALL = ("mix_norm_g", "conv_w", "conv_b", "conv_ln_g", "conv_ln_b", "sg_ln_g", "sg_ln_b", "sg_w", "sg_b",
         "q_norm_g", "k_norm_g", "out_norm_g", "ffn_norm_g")
LARGE = ("w_in", "w_out", "w_gate_up", "w_down")


def _row(v):
    return v.reshape(1, -1)


def _layer_params(p, large, l):
    q = {k: v[l] for k, v in p.items()}
    return dict(
        q, **large,
        mix_norm_g=_row(q["mix_norm_g"]), conv_b=_row(q["conv_b"]), conv_ln_g=_row(q["conv_ln_g"]),
        conv_ln_b=_row(q["conv_ln_b"]), sg_ln_g=_row(q["sg_ln_g"]), sg_ln_b=_row(q["sg_ln_b"]),
        out_norm_g=_row(q["out_norm_g"]), ffn_norm_g=_row(q["ffn_norm_g"]),
        qg=_row(jnp.tile(q["q_norm_g"], SB_W // HEAD_DIM)), kg=_row(jnp.tile(q["k_norm_g"], SB_W // HEAD_DIM)),
        sg_bias=jnp.repeat(q["sg_b"].T, HEAD_DIM, axis=1),
    )


def _layer_fwd(x, q, tok, after_in, after_mixers, after_out):
    proj, qn, kn, vb, kt, vt = _fwd_in(x, q["mix_norm_g"], q["w_in"], q["qg"], q["kg"], tok)
    yc = _fwd_conv(proj, q["conv_w"], q["conv_b"], q["conv_ln_g"], q["conv_ln_b"], after_in(proj))
    ys = _fwd_sg(proj, q["sg_ln_g"], q["sg_ln_b"], q["sg_w"], q["sg_bias"])
    yb, lt = _fwd_sb(qn, kt, vb)
    rest, tok = after_mixers(yb)
    q = dict(q, **rest)
    xm = _fwd_out(yc, ys, yb, q["out_norm_g"], q["w_out"], x, tok)
    gu, xo = _fwd_ffn(xm, q["ffn_norm_g"], q["w_gate_up"], q["w_down"], after_out(xm))
    return xo, q, dict(x=x, proj=proj, qn=qn, kn=kn, kt=kt, vt=vt, lt=lt, yc=yc, ys=ys, yb=yb, xm=xm, gu=gu)


def _layer_bwd_ffn(dxo, q, st, tok):
    dgu, act, h2, dxm, d_ffn_g = _bwd_ffn(dxo, st["gu"], st["xm"], q["ffn_norm_g"], q["w_gate_up"], q["w_down"],
                                          tok)
    d_wgu, d_wd = _matmul_tn(dgu, h2, 512, D_MODEL), _matmul_tn(act, dxo, FFN_H // 2, D_MODEL)
    dyc, dys, dyb, cat, d_out_g = _bwd_out(dxm, st["yc"], st["ys"], st["yb"], q["out_norm_g"], q["w_out"], tok)
    return dxm, (dyc, dys, dyb, d_ffn_g, d_out_g), d_wgu, d_wd, _matmul_tn(cat, dxm, 512, D_MODEL)


def _layer_bwd_mix(dxm, carried, q, st, tok):
    dyc, dys, dyb, d_ffn_g, d_out_g = carried
    dqs, dkn, dv = _bwd_sb(st["qn"], st["kn"], st["kt"], st["vt"], dyb, st["lt"], tok)
    dpb, d_qg, d_kg = _bwd_qk(st["proj"], dqs, dkn, dv, q["qg"], q["kg"])
    dps, d_sg_lg, d_sg_lb, d_sg_w, d_sg_b = _bwd_sg(st["proj"], dys, q["sg_ln_g"], q["sg_ln_b"], q["sg_w"],
                                                    q["sg_bias"])
    dpc, d_conv_w, d_conv_b, d_conv_lg, d_conv_lb = _bwd_conv(st["proj"], dyc, q["conv_w"], q["conv_b"],
                                                              q["conv_ln_g"], q["conv_ln_b"])
    dx, h1, dp, d_mix_g = _bwd_in(dpc, dps, dpb, st["x"], q["mix_norm_g"], q["w_in"], dxm)
    d_win = _matmul_tn(dp, h1, 512, D_MODEL)
    small = dict(
        mix_norm_g=d_mix_g[0], conv_w=d_conv_w, conv_b=d_conv_b[0], conv_ln_g=d_conv_lg[0],
        conv_ln_b=d_conv_lb[0], sg_ln_g=d_sg_lg[0], sg_ln_b=d_sg_lb[0], sg_w=d_sg_w,
        sg_b=d_sg_b[:, :SG_W // HEAD_DIM].T, q_norm_g=d_qg[0], k_norm_g=d_kg[0], out_norm_g=d_out_g[0],
        ffn_norm_g=d_ffn_g[0])
    return dx, d_win, small


def _position():
    x, y, c = lax.axis_index("x"), lax.axis_index("y"), lax.axis_index("c")
    return x, y, c


def _flat(px, py, pc):
    return 4 * px + 2 * py + pc


IN_HBM = pl.BlockSpec(memory_space=pltpu.HBM)
IN_SEM = pl.BlockSpec(memory_space=pltpu.SEMAPHORE)
EFFECT = pltpu.SideEffectType.DATAFLOW_SIDE_EFFECTING
COPIES = dict(scatter=7, spread=7, spread_chips=4, **{"pass": 3})


def _exchange_copies(kinds, src_refs, land_refs, send_sems, recv_sems, layer, arrival):
    x, y, c = _position()
    me = _flat(x, y, c)
    everyone = [(x ^ (k >> 2 & 1), y ^ (k >> 1 & 1), c ^ (k & 1)) for k in range(1, N_DEV)]
    sibling = (x, y, 1 - c)
    chips = [(1 - x, y, c), (x, 1 - y, c), (1 - x, 1 - y, c)]
    out = []
    srcs = iter(src_refs)
    for kind, land in zip(kinds, land_refs):
        land = land if layer is None else land.at[layer]
        if kind == "scatter":
            src = next(srcs)
            moves = [(src.at[_flat(*p)], me, _flat(*p), p) for p in everyone]
        elif kind in ("spread", "spread_chips"):
            src = next(srcs)
            moves = [(src, me, _flat(*p), p) for p in (everyone if kind == "spread" else [sibling] + chips)]
        else:
            moves = [(land.at[_flat(*p)], _flat(*p), _flat(p[0], p[1], 1 - c), sibling) for p in chips]
        for src_block, there, here, peer in moves:
            n = len(out)
            out.append(pltpu.make_async_remote_copy(
                src_ref=src_block, dst_ref=land.at[here if arrival else there], send_sem=send_sems.at[n],
                recv_sem=recv_sems.at[n], device_id=peer, device_id_type=MESH))
    return out


def _exchange_start(kinds, srcs, lands, after, name, layer=None):
    ns, n = len(srcs), len(srcs) + len(lands)
    sems = sum(COPIES[k] for k in kinds)

    def body(*refs):
        send_sems, recv_sems = refs[n + 1], refs[n + 2]
        for cp in _exchange_copies(kinds, refs[:ns], refs[ns:n], send_sems, recv_sems, layer, arrival=False):
            cp.start()
        refs[-1][...] = jnp.zeros_like(refs[-1])

    thru = [pltpu.HBM(a.shape, a.dtype) for a in (*srcs, *lands)]
    outs = pl.pallas_call(
        body, name=name,
        out_shape=(pltpu.SemaphoreType.DMA((sems,)), pltpu.SemaphoreType.DMA((sems,)), *thru,
                   jax.ShapeDtypeStruct((8, 128), F32)),
        in_specs=[IN_HBM] * n + [ORDER_ONLY],
        out_specs=(IN_SEM, IN_SEM, *[IN_HBM] * n, pl.BlockSpec(memory_space=pltpu.VMEM)),
        input_output_aliases={i: 2 + i for i in range(n)},
        compiler_params=pltpu.CompilerParams(has_side_effects=EFFECT),
    )(*[pltpu.with_memory_space_constraint(a, pltpu.HBM) for a in (*srcs, *lands)], after)
    return kinds, outs[0], outs[1], list(outs[2:2 + ns]), list(outs[2 + ns:2 + n]), outs[-1]


def _exchange_wait(pending, after, name, layer=None):
    kinds, send_sems, recv_sems, srcs, lands, _ = pending
    after = list(after) if isinstance(after, (list, tuple)) else [after]
    ns, n = len(srcs), len(srcs) + len(lands)

    def body(*refs):
        for cp in _exchange_copies(kinds, refs[:ns], refs[ns:n], refs[n], refs[n + 1], layer, arrival=True):
            cp.wait_send()
            cp.wait_recv()

    thru = [pltpu.HBM(a.shape, a.dtype) for a in (*srcs, *lands)]
    outs = pl.pallas_call(
        body, name=name, out_shape=tuple(thru),
        in_specs=[IN_HBM] * n + [IN_SEM, IN_SEM] + [ORDER_ONLY] * len(after),
        out_specs=tuple([IN_HBM] * n),
        input_output_aliases={i: i for i in range(n)},
        compiler_params=pltpu.CompilerParams(has_side_effects=EFFECT),
    )(*srcs, *lands, send_sems, recv_sems, *after)
    return list(outs[ns:])


def _landing(block, me):
    land = lax.empty((N_DEV,) + block.shape, block.dtype)
    return lax.dynamic_update_index_in_dim(land, block, me, 0)


def _adamw(parts, w, m, v, tr):
    groups, rows, cols = w.shape

    def body(p_ref, w_ref, m_ref, v_ref, g_ref, d_ref, nm_ref, nv_ref):
        g = p_ref[0].astype(F32)
        for j in range(1, N_DEV):
            g = g + p_ref[j].astype(F32)
        g_ref[...] = g
        m = ADAM_B1 * m_ref[...] + (1.0 - ADAM_B1) * g
        v = ADAM_B2 * v_ref[...] + (1.0 - ADAM_B2) * (g * g)
        nm_ref[...] = m
        nv_ref[...] = v
        m_hat = m / (1.0 - ADAM_B1 ** ADAM_STEP)
        v_hat = v / (1.0 - ADAM_B2 ** ADAM_STEP)
        d_ref[...] = -ADAM_LR * (m_hat / (jnp.sqrt(v_hat) + ADAM_EPS) + ADAM_WD * w_ref[...])

    blk = pl.BlockSpec((None, tr, cols), lambda g, i: (g, i, 0))
    return pl.pallas_call(
        body, name="adamw", grid=(groups, rows // tr),
        in_specs=[pl.BlockSpec((None, N_DEV, tr, cols), lambda g, i: (g, 0, i, 0)), blk, blk, blk],
        out_specs=[blk] * 4,
        out_shape=[jax.ShapeDtypeStruct((groups, rows, cols), F32)] * 4,
        compiler_params=_cp("parallel", "parallel"),
    )(parts, w, m, v)


def _row_tile(rows):
    for cand in range(min(rows, 512) // 8 * 8, 7, -8):
        if rows % cand == 0:
            return cand
    return rows


def _with_own_block(land, blocks, layer, me):
    own = lax.dynamic_index_in_dim(blocks, me, 0, keepdims=True)[None]
    return lax.dynamic_update_slice(land, own, (layer, me, 0, 0))


PACK_LANES = 128


def _pack_layers(arrs):
    parts = []
    for a in arrs:
        flat = a.reshape(a.shape[0], -1)
        parts.append(jnp.pad(flat, ((0, 0), (0, -flat.shape[1] % (8 * PACK_LANES)))))
    return jnp.concatenate(parts, axis=1).reshape(arrs[0].shape[0], -1, PACK_LANES)


def _unpack_layers(packed, shapes):
    flat = packed.reshape(packed.shape[0], -1)
    outs, off = [], 0
    for shp in shapes:
        size = 1
        for d in shp[1:]:
            size *= d
        outs.append(flat[:, off:off + size].reshape(shp))
        off += size + (-size % (8 * PACK_LANES))
    return outs


def kernel(x, mix_norm_g, w_in, conv_w, conv_b, conv_ln_g, conv_ln_b, sg_ln_g, sg_ln_b, sg_w, sg_b, q_norm_g, k_norm_g, out_norm_g, w_out, ffn_norm_g, w_gate_up, w_down, loss_target, m_mix_norm_g, m_w_in, m_conv_w, m_conv_b, m_conv_ln_g, m_conv_ln_b, m_sg_ln_g, m_sg_ln_b, m_sg_w, m_sg_b, m_q_norm_g, m_k_norm_g, m_out_norm_g, m_w_out, m_ffn_norm_g, m_w_gate_up, m_w_down, v_mix_norm_g, v_w_in, v_conv_w, v_conv_b, v_conv_ln_g, v_conv_ln_b, v_sg_ln_g, v_sg_ln_b, v_sg_w, v_sg_b, v_q_norm_g, v_k_norm_g, v_out_norm_g, v_w_out, v_ffn_norm_g, v_w_gate_up, v_w_down):
    names = SMALL[:1] + LARGE[:1] + SMALL[1:12] + LARGE[1:2] + SMALL[12:] + LARGE[2:]
    w = dict(mix_norm_g=mix_norm_g, w_in=w_in, conv_w=conv_w, conv_b=conv_b, conv_ln_g=conv_ln_g,
             conv_ln_b=conv_ln_b, sg_ln_g=sg_ln_g, sg_ln_b=sg_ln_b, sg_w=sg_w, sg_b=sg_b, q_norm_g=q_norm_g,
             k_norm_g=k_norm_g, out_norm_g=out_norm_g, w_out=w_out, ffn_norm_g=ffn_norm_g,
             w_gate_up=w_gate_up, w_down=w_down)
    m = dict(mix_norm_g=m_mix_norm_g, w_in=m_w_in, conv_w=m_conv_w, conv_b=m_conv_b, conv_ln_g=m_conv_ln_g,
             conv_ln_b=m_conv_ln_b, sg_ln_g=m_sg_ln_g, sg_ln_b=m_sg_ln_b, sg_w=m_sg_w, sg_b=m_sg_b,
             q_norm_g=m_q_norm_g, k_norm_g=m_k_norm_g, out_norm_g=m_out_norm_g, w_out=m_w_out,
             ffn_norm_g=m_ffn_norm_g, w_gate_up=m_w_gate_up, w_down=m_w_down)
    v = dict(mix_norm_g=v_mix_norm_g, w_in=v_w_in, conv_w=v_conv_w, conv_b=v_conv_b, conv_ln_g=v_conv_ln_g,
             conv_ln_b=v_conv_ln_b, sg_ln_g=v_sg_ln_g, sg_ln_b=v_sg_ln_b, sg_w=v_sg_w, sg_b=v_sg_b,
             q_norm_g=v_q_norm_g, k_norm_g=v_k_norm_g, out_norm_g=v_out_norm_g, w_out=v_w_out,
             ffn_norm_g=v_ffn_norm_g, w_gate_up=v_w_gate_up, w_down=v_w_down)
    xpos, ypos, cpos = _position()
    me = _flat(xpos, ypos, cpos)
    conv_cols = conv_w.shape[-1]
    no_token = jnp.zeros((8, 128), F32)
    w, m, v = (dict(t, w_in=jnp.swapaxes(t["w_in"], 1, 2), w_gate_up=jnp.swapaxes(t["w_gate_up"], 1, 2))
               for t in (w, m, v))
    shards = {k: w[k].astype(BF16) for k in LARGE}
    full_shape = dict(w_in=(IN_W, D_MODEL), w_out=(D_MODEL, D_MODEL), w_gate_up=(2 * FFN_H, D_MODEL),
                      w_down=(FFN_H, D_MODEL))

    def gather_start(srcs, after, tag):
        return _exchange_start(["spread_chips"] * len(srcs), srcs, [_landing(a, me) for a in srcs], after,
                               f"gather_start_{tag}")

    def gather_pass(pending, after, tag):
        lands = _exchange_wait(pending, after, f"gather_wait_{tag}")
        return _exchange_start(["pass"] * len(lands), [], lands, lands[0], f"gather_pass_{tag}")

    def gathered(pending, keys, after, tag):
        lands = _exchange_wait(pending, after, f"gather_passed_{tag}")
        return {k: a.reshape(full_shape[k]) for k, a in zip(keys, lands)}, lands[len(keys):]

    first, later = ("w_in",), ("w_out", "w_gate_up", "w_down")
    act = x[0]
    head = gather_start([shards["w_in"][0], w["conv_w"]], act, "0")
    tail = gather_start([shards[k][0] for k in later], head[5], "0_later")
    head = gather_pass(head, tail[5], "0")
    large, (conv_blocks,) = gathered(head, first, head[5], "0")
    conv_full = jnp.transpose(conv_blocks, (1, 2, 0, 3)).reshape(DEPTH, CONV_K, CONV_W)
    small_w = dict({k: w[k] for k in SMALL}, conv_w=conv_full)
    qs, stash = [], []
    for l in range(DEPTH):
        coming = {}
        more = l + 1 < DEPTH

        def next_start(after):
            coming["first"] = gather_start([shards[k][l + 1] for k in LARGE], after, str(l + 1))
            return coming["first"][5]

        def after_in(proj):
            return next_start(proj) if more and l == 0 else no_token

        def after_mixers(y_sb):
            if l > 0:
                return {}, no_token
            passing = gather_pass(tail, y_sb, "0_later")
            return gathered(passing, later, passing[5], "0_later")[0], passing[5]

        def after_out(x_mid):
            if not more:
                return no_token
            coming["second"] = gather_pass(coming["first"], x_mid, str(l + 1))
            return coming["second"][5]

        token = next_start(act) if more and l > 0 else no_token
        act, q, st = _layer_fwd(act, _layer_params(small_w, large, l), token, after_in, after_mixers, after_out)
        qs.append(q)
        stash.append(st)
        if more:
            large, _ = gathered(coming["second"], LARGE, act, str(l + 1))

    loss, dx = _loss_head(act, loss_target[0])
    loss = lax.psum(loss[0, 0], ("x", "y", "c"))

    replicated = tuple(k for k in SMALL if k != "conv_w")
    small_rows = _pack_layers([w[k][:1] for k in replicated]).shape[1]
    conv_rows = _pack_layers([conv_full[:1]]).shape[1]
    group_a, group_b = ("w_gate_up", "w_down", "w_out"), ("w_in",)
    blocks = lambda k, a: a.reshape((N_DEV,) + w[k].shape[1:])
    land_a = [lax.empty((DEPTH, N_DEV) + w[k].shape[1:], BF16) for k in group_a]
    land_b = [lax.empty((DEPTH, N_DEV) + w[k].shape[1:], BF16) for k in group_b]
    land_b.append(lax.empty((DEPTH, N_DEV, small_rows + conv_rows, PACK_LANES), F32))
    pend_a = pend_b = None
    token = no_token
    for l in reversed(range(DEPTH)):
        dxm, carried, d_wgu, d_wd, d_wo = _layer_bwd_ffn(dx, qs[l], stash[l], token)
        srcs = [blocks(k, a) for k, a in zip(group_a, (d_wgu, d_wd, d_wo))]
        if pend_a is not None:
            land_a = _exchange_wait(pend_a, d_wo, f"grads_a_wait_{l + 1}", layer=l + 1)
        land_a = [_with_own_block(ld, a, l, me) for ld, a in zip(land_a, srcs)]
        pend_a = _exchange_start(["scatter"] * 3, srcs, land_a, d_wo, f"grads_a_start_{l}", layer=l)
        dx, d_win, small = _layer_bwd_mix(dxm, carried, qs[l], stash[l], pend_a[5])
        packed = _pack_layers([small[k][None] for k in replicated + ("conv_w",)])[0]
        srcs = [blocks("w_in", d_win)]
        if pend_b is not None:
            land_b = _exchange_wait(pend_b, d_win, f"grads_b_wait_{l + 1}", layer=l + 1)
        land_b = [_with_own_block(land_b[0], srcs[0], l, me),
                  lax.dynamic_update_slice(land_b[1], packed[None, None], (l, me, 0, 0))]
        pend_b = _exchange_start(["scatter", "spread"], srcs + [packed], land_b, d_win, f"grads_b_start_{l}",
                                 layer=l)
        token = pend_b[5]

    land_a = _exchange_wait(pend_a, token, "grads_a_wait_0", layer=0)
    res = {}
    for k, parts in zip(group_a, land_a):
        res[k] = _adamw(parts, w[k], m[k], v[k], _row_tile(w[k].shape[1]))
    land_b = _exchange_wait(pend_b, [res[k][0] for k in group_a], "grads_b_wait_0", layer=0)
    res["w_in"] = _adamw(land_b[0], w["w_in"], m["w_in"], v["w_in"], _row_tile(w["w_in"].shape[1]))
    small_parts = land_b[1]
    updated = _adamw(small_parts, *(_pack_layers([t[k] for k in replicated]) for t in (w, m, v)), small_rows)
    unpacked = [_unpack_layers(o, [w[k].shape for k in replicated]) for o in updated]
    res.update({k: [u[i] for u in unpacked] for i, k in enumerate(replicated)})
    conv_parts = small_parts[:, :, small_rows:].reshape(DEPTH, N_DEV, -1)[:, :, :CONV_K * CONV_W]
    conv_parts = lax.dynamic_slice_in_dim(conv_parts.reshape(DEPTH, N_DEV, CONV_K, CONV_W), me * conv_cols,
                                          conv_cols, axis=3)
    res["conv_w"] = _adamw(conv_parts, w["conv_w"], m["conv_w"], v["conv_w"], CONV_K)
    for k in ("w_in", "w_gate_up"):
        res[k] = [jnp.swapaxes(a, 1, 2) for a in res[k]]

    return (loss, dx[None], *[res[k][0] for k in names], *[res[k][1] for k in names],
            *[res[k][2] for k in names], *[res[k][3] for k in names])
```

```python
import functools

import jax
import jax.numpy as jnp
from jax import lax
from jax.experimental import pallas as pl
from jax.experimental.pallas import tpu as pltpu

F32 = jnp.float32
BF16 = jnp.bfloat16

D_MODEL = 1024
DEPTH = 4
HEAD_DIM = 64
CONV_W = 256
SG_W = 256
SB_W = 512
IN_W = 2560
FFN_H = 2816
CONV_K = 31
CHUNK = 128
OFF_SG = 2 * CONV_W
OFF_SB = OFF_SG + 2 * SG_W
RMS_EPS = 1e-6
LN_EPS = 1e-5
N_DEV = 8
MESH = pl.DeviceIdType.MESH

ADAM_LR = 0.001
ADAM_B1 = 0.9
ADAM_B2 = 0.999
ADAM_EPS = 1e-08
ADAM_WD = 0.01
ADAM_STEP = 10

TOKEN_TILE = 256
VMEM_LIMIT = 56 * 1024 * 1024


def _cp(*sem):
    return pltpu.CompilerParams(dimension_semantics=sem or None, vmem_limit_bytes=VMEM_LIMIT)


def _dot(a, b):
    return jnp.dot(a, b, preferred_element_type=F32)


def _dot_nt(a, b):
    return lax.dot_general(a, b, (((1,), (1,)), ((), ())), preferred_element_type=F32)


def _dot_tn(a, b):
    return lax.dot_general(a, b, (((0,), (0,)), ((), ())), preferred_element_type=F32)


def _dot_split(x, m):
    hi = x.astype(BF16)
    lo = (x - hi.astype(F32)).astype(BF16)
    return _dot(hi, m) + _dot(lo, m)


def _group_mean_matrix(width, group):
    r = lax.broadcasted_iota(jnp.int32, (width, width), 0) // group
    c = lax.broadcasted_iota(jnp.int32, (width, width), 1) // group
    return jnp.where(r == c, 1.0 / group, 0.0).astype(BF16)


def _sigmoid(x):
    return 1.0 / (1.0 + jnp.exp(-x))


def _gelu(x):
    return 0.5 * x * (1.0 + lax.erf(x * (2.0 ** -0.5)))


def _gelu_grad(x):
    return 0.5 * (1.0 + lax.erf(x * (2.0 ** -0.5))) + x * jnp.exp(-0.5 * x * x) * (0.5 * (2.0 / jnp.pi) ** 0.5)


def _rms_stats(x):
    r = lax.rsqrt(jnp.mean(x * x, axis=-1, keepdims=True) + RMS_EPS)
    return r, x * r


def _rms_bwd(xh, r, g, dy):
    dxh = dy * g
    dx = r * (dxh - xh * jnp.mean(dxh * xh, axis=-1, keepdims=True))
    return dx, dy * xh


def _ln_stats(x):
    mu = jnp.mean(x, axis=-1, keepdims=True)
    xc = x - mu
    r = lax.rsqrt(jnp.mean(xc * xc, axis=-1, keepdims=True) + LN_EPS)
    return r, xc * r


def _ln_bwd(xh, r, g, dy):
    dxh = dy * g
    return r * (dxh - jnp.mean(dxh, axis=-1, keepdims=True) - xh * jnp.mean(dxh * xh, axis=-1, keepdims=True))


def _colsum(x):
    return jnp.sum(x, axis=0, keepdims=True)


def _rows(tm, n, j=0):
    return pl.BlockSpec((tm, n), lambda i: (i, j))


def _whole(shape):
    return pl.BlockSpec(shape, lambda i: (0,) * len(shape))


ORDER_ONLY = pl.BlockSpec(memory_space=pl.ANY)


def _stack_heads(a):
    even = (lax.broadcasted_iota(jnp.int32, a.shape, 1) % (2 * HEAD_DIM)) < HEAD_DIM
    top = jnp.where(even, a, 0.0)
    bot = jnp.where(even, 0.0, a)
    parts = []
    for c in range(a.shape[0] // CHUNK):
        rows = slice(c * CHUNK, (c + 1) * CHUNK)
        parts += [top[rows], bot[rows]]
    return jnp.concatenate(parts, axis=0)


def _store_stacked(st, st_ref, tr_ref):
    st_ref[...] = st.astype(BF16)
    for p in range(SB_W // CHUNK):
        for c in range(st.shape[0] // (2 * CHUNK)):
            tile = st[2 * c * CHUNK:2 * (c + 1) * CHUNK, p * CHUNK:(p + 1) * CHUNK]
            tr_ref[p, c] = tile.T.astype(BF16)


def _load_transposed(tr_ref):
    rows = []
    for c in range(tr_ref.shape[1]):
        tiles = [tr_ref[p, c].T for p in range(SB_W // CHUNK)]
        rows.append(jnp.concatenate(tiles, axis=1))
    return jnp.concatenate(rows, axis=0)


def _unstack_heads(st):
    even = (lax.broadcasted_iota(jnp.int32, (CHUNK, st.shape[1]), 1) % (2 * HEAD_DIM)) < HEAD_DIM
    parts = []
    for c in range(st.shape[0] // (2 * CHUNK)):
        top = st[2 * c * CHUNK:(2 * c + 1) * CHUNK]
        bot = st[(2 * c + 1) * CHUNK:(2 * c + 2) * CHUNK]
        parts.append(jnp.where(even, top, bot))
    return jnp.concatenate(parts, axis=0)


def _fwd_in(x, g, w, qg, kg, tok):
    s = x.shape[0]
    tm = TOKEN_TILE

    def body(x_ref, g_ref, w_ref, qg_ref, kg_ref, tok_ref, proj_ref, qn_ref, kn_ref, vb_ref, kt_ref, vt_ref):
        r, xh = _rms_stats(x_ref[...])
        h = (xh * g_ref[...]).astype(BF16)
        proj = _dot_nt(h, w_ref[...])
        proj_ref[...] = proj
        gm = _group_mean_matrix(SB_W, HEAD_DIM)
        q = proj[:, OFF_SB:OFF_SB + SB_W]
        k = proj[:, OFF_SB + SB_W:OFF_SB + 2 * SB_W]
        rq = lax.rsqrt(_dot_split(q * q, gm) + RMS_EPS)
        rk = lax.rsqrt(_dot_split(k * k, gm) + RMS_EPS)
        qn_ref[...] = (q * rq * qg_ref[...] * (HEAD_DIM ** -0.5)).astype(BF16)
        _store_stacked(_stack_heads(k * rk * kg_ref[...]), kn_ref, kt_ref)
        _store_stacked(_stack_heads(proj[:, OFF_SB + 2 * SB_W:]), vb_ref, vt_ref)

    tiles = pl.BlockSpec((SB_W // CHUNK, tm // CHUNK, CHUNK, PAIR), lambda i: (0, i, 0, 0))
    tiles_shape = jax.ShapeDtypeStruct((SB_W // CHUNK, s // CHUNK, CHUNK, PAIR), BF16)
    return pl.pallas_call(
        body, name="fwd_in", grid=(s // tm,),
        in_specs=[_rows(tm, D_MODEL), _whole((1, D_MODEL)), _whole((IN_W, D_MODEL)),
                  _whole((1, SB_W)), _whole((1, SB_W)), ORDER_ONLY],
        out_specs=[_rows(tm, IN_W), _rows(tm, SB_W), _rows(2 * tm, SB_W), _rows(2 * tm, SB_W), tiles, tiles],
        out_shape=[jax.ShapeDtypeStruct((s, IN_W), F32), jax.ShapeDtypeStruct((s, SB_W), BF16),
                   jax.ShapeDtypeStruct((2 * s, SB_W), BF16), jax.ShapeDtypeStruct((2 * s, SB_W), BF16),
                   tiles_shape, tiles_shape],
        compiler_params=_cp("parallel"),
    )(x, g, w, qg, kg, tok)


def _conv_window(abuf, r0, w_ref):
    win = abuf[pl.ds(pl.multiple_of(r0 + CHUNK - 32, 32), CHUNK + 32), :]
    acc = jnp.zeros((CHUNK, CONV_W), F32)
    for k in range(CONV_K):
        sh = win if k == 0 else pltpu.roll(win, k, axis=0)
        acc = acc + sh[32:, :] * w_ref[CONV_K - 1 - k:CONV_K - k, :]
    return acc, win


def _glu_fill(p_ref, abuf, s):
    abuf[0:CHUNK, :] = jnp.zeros((CHUNK, CONV_W), F32)

    def fill(c, carry):
        r0 = pl.multiple_of(c * CHUNK, CHUNK)
        pv = p_ref[pl.ds(r0, CHUNK), :]
        abuf[pl.ds(r0 + CHUNK, CHUNK), :] = pv[:, :CONV_W] * _sigmoid(pv[:, CONV_W:])
        return carry

    lax.fori_loop(0, s // CHUNK, fill, 0)


def _fwd_conv(proj, w, b, lg, lb, tok):
    s = proj.shape[0]

    def body(p_ref, w_ref, b_ref, lg_ref, lb_ref, tok_ref, y_ref, abuf):
        _glu_fill(p_ref, abuf, s)

        def chunk(c, carry):
            r0 = pl.multiple_of(c * CHUNK, CHUNK)
            acc, _ = _conv_window(abuf, r0, w_ref)
            r, xh = _ln_stats(acc + b_ref[...])
            ln = xh * lg_ref[...] + lb_ref[...]
            y_ref[pl.ds(r0, CHUNK), :] = ln * _sigmoid(ln)
            return carry

        lax.fori_loop(0, s // CHUNK, chunk, 0)

    return pl.pallas_call(
        body, name="fwd_conv", grid=(1,),
        in_specs=[pl.BlockSpec((s, 2 * CONV_W), lambda i: (0, 0)), _whole((CONV_K, CONV_W)),
                  _whole((1, CONV_W)), _whole((1, CONV_W)), _whole((1, CONV_W)), ORDER_ONLY],
        out_specs=_whole((s, CONV_W)),
        out_shape=jax.ShapeDtypeStruct((s, CONV_W), F32),
        scratch_shapes=[pltpu.VMEM((s + CHUNK, CONV_W), F32)],
        compiler_params=_cp("arbitrary"),
    )(proj, w, b, lg, lb, tok)


def _sg_masks():
    row = lax.broadcasted_iota(jnp.int32, (CHUNK, CHUNK), 0)
    col = lax.broadcasted_iota(jnp.int32, (CHUNK, CHUNK), 1)
    lane_head = lax.broadcasted_iota(jnp.int32, (CHUNK, SG_W), 1) // HEAD_DIM
    return row >= col, lane_head


def _sg_mix(w_ref, bias_ref, vc, tril, lane_head):
    mixed = bias_ref[...]
    for h in range(SG_W // HEAD_DIM):
        wm = jnp.where(tril, w_ref[h], 0.0).astype(BF16)
        mixed = mixed + jnp.where(lane_head == h, _dot(wm, vc), 0.0)
    return mixed


def _fwd_sg(proj, lg, lb, w, bias):
    s = proj.shape[0]
    tm = TOKEN_TILE

    def body(p_ref, lg_ref, lb_ref, w_ref, bias_ref, y_ref):
        ge = _gelu(p_ref[...])
        u = ge[:, :SG_W]
        r, xh = _ln_stats(ge[:, SG_W:])
        vln = (xh * lg_ref[...] + lb_ref[...]).astype(BF16)
        tril, lane_head = _sg_masks()
        for c in range(tm // CHUNK):
            rows = slice(c * CHUNK, (c + 1) * CHUNK)
            y_ref[rows, :] = u[rows] * _sg_mix(w_ref, bias_ref, vln[rows], tril, lane_head)

    return pl.pallas_call(
        body, name="fwd_sg", grid=(s // tm,),
        in_specs=[_rows(tm, 2 * SG_W, 1), _whole((1, SG_W)), _whole((1, SG_W)),
                  _whole((SG_W // HEAD_DIM, CHUNK, CHUNK)), _whole((CHUNK, SG_W))],
        out_specs=_rows(tm, SG_W),
        out_shape=jax.ShapeDtypeStruct((s, SG_W), F32),
        compiler_params=_cp("parallel"),
    )(proj, lg, lb, w, bias)


SB_Q = 2 * CHUNK
PAIR = 2 * CHUNK
SB_PAIRS = 2


def _pair_tri(kind):
    row = lax.broadcasted_iota(jnp.int32, (PAIR, PAIR), 0)
    col = lax.broadcasted_iota(jnp.int32, (PAIR, PAIR), 1)
    tri = {"after": row > col, "upto": row <= col, "before": row < col}[kind]
    return jnp.where(((row // CHUNK) == (col // CHUNK)) & tri, 1.0, 0.0).astype(BF16)


def _sb_scores(z, qpos0, kpos0, masked):
    sp = jnp.maximum(z, 0.0) + jnp.log(1.0 + jnp.exp(-jnp.abs(z)))
    if not masked:
        return z, sp, sp.astype(BF16), None
    row = lax.broadcasted_iota(jnp.int32, z.shape, 0)
    col = lax.broadcasted_iota(jnp.int32, z.shape, 1) % CHUNK
    mask = (kpos0 + col) < (qpos0 + row)
    return z, sp, jnp.where(mask, sp, 0.0).astype(BF16), mask


def _per_head(c0, c1):
    return jnp.concatenate([jnp.broadcast_to(c0, (SB_Q, CHUNK)), jnp.broadcast_to(c1, (SB_Q, CHUNK))], axis=1)


def _fwd_sb(qn, ktr, vst):
    s = qn.shape[0]
    np_ = SB_PAIRS

    def body(q_ref, k_ref, v_ref, after_ref, o_ref, lt_ref, z_buf, att_buf):
        i = pl.program_id(1)
        last = 2 * i + 1
        after = after_ref[...]
        lanes = [slice(pr * CHUNK, (pr + 1) * CHUNK) for pr in range(np_)]
        qs = [q_ref[:, lanes[pr]] for pr in range(np_)]

        def rows(kb):
            return pl.ds(pl.multiple_of(kb * PAIR, PAIR), PAIR)

        def block(kb, carry, masked):
            out = []
            for pr in range(np_):
                acc, c0, c1 = carry[pr]
                z_next = _dot(qs[pr], k_ref[pr, jnp.maximum(kb - 1, 0)])
                pv = _dot(att_buf[pr], v_ref[rows(jnp.minimum(kb + 1, last)), lanes[pr]])
                z, sp, nlb, mask = _sb_scores(z_buf[pr], i * SB_Q, kb * CHUNK, masked)
                loc = _dot(nlb, after)
                att = jnp.exp(z - sp - loc - _per_head(c0, c1))
                if masked:
                    att = jnp.where(mask, att, 0.0)
                z_buf[pr] = z_next
                att_buf[pr] = att.astype(BF16)
                out.append((acc + pv, c0 + loc[:, 0:1] + nlb[:, 0:1].astype(F32),
                            c1 + loc[:, CHUNK:CHUNK + 1] + nlb[:, CHUNK:CHUNK + 1].astype(F32)))
            return tuple(out)

        for pr in range(np_):
            z_buf[pr] = _dot(qs[pr], k_ref[pr, last])
        att_buf[...] = jnp.zeros_like(att_buf)
        zero = jnp.zeros((SB_Q, 1), F32)
        carry = ((jnp.zeros((SB_Q, CHUNK), F32), zero, zero),) * np_
        carry = block(last, carry, True)
        carry = block(last - 1, carry, True)
        carry = lax.fori_loop(0, 2 * i, lambda j, c: block(2 * i - 1 - j, c, False), carry)
        for pr, (acc, c0, c1) in enumerate(carry):
            o_ref[:, lanes[pr]] = acc + _dot(att_buf[pr], v_ref[rows(0), lanes[pr]])
            lt_ref[:, lanes[pr]] = jnp.concatenate([jnp.broadcast_to(c0, (SB_Q, HEAD_DIM)),
                                                    jnp.broadcast_to(c1, (SB_Q, HEAD_DIM))], axis=1)

    blk = pl.BlockSpec((SB_Q, np_ * CHUNK), lambda p, i: (i, p))
    seq = pl.BlockSpec((2 * s, np_ * CHUNK), lambda p, i: (0, p))
    return pl.pallas_call(
        body, name="fwd_sb", grid=(SB_W // CHUNK // np_, s // SB_Q),
        in_specs=[blk, pl.BlockSpec((np_, s // CHUNK, CHUNK, PAIR), lambda p, i: (p, 0, 0, 0)), seq,
                  pl.BlockSpec((PAIR, PAIR), lambda p, i: (0, 0))],
        out_specs=[blk, blk],
        out_shape=[jax.ShapeDtypeStruct((s, SB_W), F32)] * 2,
        scratch_shapes=[pltpu.VMEM((np_, SB_Q, PAIR), F32), pltpu.VMEM((np_, SB_Q, PAIR), BF16)],
        compiler_params=_cp("parallel", "parallel"),
    )(qn, ktr, vst, _pair_tri("after"))


def _group_norms(yc, ys, yb):
    return [_rms_stats(yc), _rms_stats(ys), _rms_stats(yb)]


def _fwd_out(yc, ys, yb, g, w, x, tok):
    s = x.shape[0]
    tm = TOKEN_TILE

    def body(yc_ref, ys_ref, yb_ref, g_ref, w_ref, x_ref, tok_ref, o_ref):
        stats = _group_norms(yc_ref[...], ys_ref[...], yb_ref[...])
        cat = jnp.concatenate([xh for _, xh in stats], axis=1) * g_ref[...]
        o_ref[...] = x_ref[...] + _dot(cat.astype(BF16), w_ref[...])

    return pl.pallas_call(
        body, name="fwd_out", grid=(s // tm,),
        in_specs=[_rows(tm, CONV_W), _rows(tm, SG_W), _rows(tm, SB_W), _whole((1, D_MODEL)),
                  _whole((D_MODEL, D_MODEL)), _rows(tm, D_MODEL), ORDER_ONLY],
        out_specs=_rows(tm, D_MODEL),
        out_shape=jax.ShapeDtypeStruct((s, D_MODEL), F32),
        compiler_params=_cp("parallel"),
    )(yc, ys, yb, g, w, x, tok)


def _fwd_ffn(x, g, wgu, wd, tok):
    s = x.shape[0]
    tm = TOKEN_TILE

    def body(x_ref, g_ref, wgu_ref, wd_ref, tok_ref, gu_ref, o_ref):
        x = x_ref[...]
        r, xh = _rms_stats(x)
        gu = _dot_nt((xh * g_ref[...]).astype(BF16), wgu_ref[...])
        gu_ref[...] = gu
        gate = gu[:, :FFN_H]
        act = gate * _sigmoid(gate) * gu[:, FFN_H:]
        o_ref[...] = x + _dot(act.astype(BF16), wd_ref[...])

    return pl.pallas_call(
        body, name="fwd_ffn", grid=(s // tm,),
        in_specs=[_rows(tm, D_MODEL), _whole((1, D_MODEL)),
                  pl.BlockSpec((2 * FFN_H, D_MODEL), lambda i: (0, 0), pipeline_mode=pl.Buffered(1)),
                  pl.BlockSpec((FFN_H, D_MODEL), lambda i: (0, 0), pipeline_mode=pl.Buffered(1)), ORDER_ONLY],
        out_specs=[_rows(tm, 2 * FFN_H), _rows(tm, D_MODEL)],
        out_shape=[jax.ShapeDtypeStruct((s, 2 * FFN_H), F32), jax.ShapeDtypeStruct((s, D_MODEL), F32)],
        compiler_params=_cp("parallel"),
    )(x, g, wgu, wd, tok)


def _loss_head(y, target):
    s = y.shape[0]
    tm = TOKEN_TILE

    def body(y_ref, t_ref, l_ref, d_ref):
        @pl.when(pl.program_id(0) == 0)
        def _():
            l_ref[...] = jnp.zeros_like(l_ref)

        err = y_ref[...] - t_ref[...]
        d_ref[...] = err * (1.0 / D_MODEL)
        l_ref[...] += 0.5 * jnp.sum(jnp.mean(err * err, axis=-1, keepdims=True), axis=0, keepdims=True)

    return pl.pallas_call(
        body, name="loss_head", grid=(s // tm,),
        in_specs=[_rows(tm, D_MODEL), _rows(tm, D_MODEL)],
        out_specs=[_whole((1, 1)), _rows(tm, D_MODEL)],
        out_shape=[jax.ShapeDtypeStruct((1, 1), F32), jax.ShapeDtypeStruct((s, D_MODEL), F32)],
        compiler_params=_cp("arbitrary"),
    )(y, target)


def _accumulate(ref, value):
    @pl.when(pl.program_id(0) == 0)
    def _():
        ref[...] = jnp.zeros_like(ref)

    ref[...] += value


def _bwd_ffn(dxo, gu, xm, g, wgu, wd, tok):
    s = dxo.shape[0]
    tm = TOKEN_TILE

    def body(dxo_ref, gu_ref, xm_ref, g_ref, wgu_ref, wd_ref, tok_ref, dgu_ref, act_ref, h_ref, dxm_ref, dg_ref):
        dxo = dxo_ref[...]
        gu = gu_ref[...]
        gate, up = gu[:, :FFN_H], gu[:, FFN_H:]
        sg = _sigmoid(gate)
        sl = gate * sg
        act_ref[...] = (sl * up).astype(BF16)
        dact = _dot_nt(dxo.astype(BF16), wd_ref[...])
        dgate = dact * up * (sg * (1.0 + gate * (1.0 - sg)))
        dgu = jnp.concatenate([dgate, dact * sl], axis=1).astype(BF16)
        dgu_ref[...] = dgu
        dh = _dot(dgu, wgu_ref[...])
        r, xh = _rms_stats(xm_ref[...])
        h_ref[...] = (xh * g_ref[...]).astype(BF16)
        dx, dgrow = _rms_bwd(xh, r, g_ref[...], dh)
        dxm_ref[...] = dxo + dx
        _accumulate(dg_ref, _colsum(dgrow))

    return pl.pallas_call(
        body, name="bwd_ffn", grid=(s // tm,),
        in_specs=[_rows(tm, D_MODEL), _rows(tm, 2 * FFN_H), _rows(tm, D_MODEL), _whole((1, D_MODEL)),
                  pl.BlockSpec((2 * FFN_H, D_MODEL), lambda i: (0, 0), pipeline_mode=pl.Buffered(1)),
                  pl.BlockSpec((FFN_H, D_MODEL), lambda i: (0, 0), pipeline_mode=pl.Buffered(1)), ORDER_ONLY],
        out_specs=[_rows(tm, 2 * FFN_H), _rows(tm, FFN_H), _rows(tm, D_MODEL), _rows(tm, D_MODEL),
                   _whole((1, D_MODEL))],
        out_shape=[jax.ShapeDtypeStruct((s, 2 * FFN_H), BF16), jax.ShapeDtypeStruct((s, FFN_H), BF16),
                   jax.ShapeDtypeStruct((s, D_MODEL), BF16), jax.ShapeDtypeStruct((s, D_MODEL), F32),
                   jax.ShapeDtypeStruct((1, D_MODEL), F32)],
        compiler_params=_cp("arbitrary"),
    )(dxo, gu, xm, g, wgu, wd, tok)


def _matmul_tn(a, b, tm, tn, out_dtype=BF16):
    s, m = a.shape
    n = b.shape[1]

    def body(a_ref, b_ref, o_ref):
        o_ref[...] = _dot_tn(a_ref[...].astype(BF16), b_ref[...].astype(BF16)).astype(out_dtype)

    return pl.pallas_call(
        body, name="weight_grad", grid=(m // tm, n // tn),
        in_specs=[pl.BlockSpec((s, tm), lambda i, j: (0, i)), pl.BlockSpec((s, tn), lambda i, j: (0, j))],
        out_specs=pl.BlockSpec((tm, tn), lambda i, j: (i, j)),
        out_shape=jax.ShapeDtypeStruct((m, n), out_dtype),
        compiler_params=_cp("parallel", "parallel"),
    )(a, b)


def _bwd_out(dxm, yc, ys, yb, g, w, tok):
    s = dxm.shape[0]
    tm = TOKEN_TILE

    def body(dxm_ref, yc_ref, ys_ref, yb_ref, g_ref, w_ref, tok_ref, dyc_ref, dys_ref, dyb_ref, cat_ref, dg_ref):
        stats = _group_norms(yc_ref[...], ys_ref[...], yb_ref[...])
        g = g_ref[...]
        cat_ref[...] = (jnp.concatenate([xh for _, xh in stats], axis=1) * g).astype(BF16)
        dcat = _dot_nt(dxm_ref[...].astype(BF16), w_ref[...])
        dgs = []
        off = 0
        for (r, xh), out in zip(stats, (dyc_ref, dys_ref, dyb_ref)):
            cols = slice(off, off + xh.shape[1])
            dx, dgrow = _rms_bwd(xh, r, g[:, cols], dcat[:, cols])
            out[...] = dx
            dgs.append(_colsum(dgrow))
            off += xh.shape[1]
        _accumulate(dg_ref, jnp.concatenate(dgs, axis=1))

    return pl.pallas_call(
        body, name="bwd_out", grid=(s // tm,),
        in_specs=[_rows(tm, D_MODEL), _rows(tm, CONV_W), _rows(tm, SG_W), _rows(tm, SB_W),
                  _whole((1, D_MODEL)), _whole((D_MODEL, D_MODEL)), ORDER_ONLY],
        out_specs=[_rows(tm, CONV_W), _rows(tm, SG_W), _rows(tm, SB_W), _rows(tm, D_MODEL),
                   _whole((1, D_MODEL))],
        out_shape=[jax.ShapeDtypeStruct((s, CONV_W), F32), jax.ShapeDtypeStruct((s, SG_W), F32),
                   jax.ShapeDtypeStruct((s, SB_W), F32), jax.ShapeDtypeStruct((s, D_MODEL), BF16),
                   jax.ShapeDtypeStruct((1, D_MODEL), F32)],
        compiler_params=_cp("arbitrary"),
    )(dxm, yc, ys, yb, g, w, tok)


def _bwd_sb(qn, kst, ktr, vtr, dy, ltot, tok):
    s = qn.shape[0]
    np_ = SB_PAIRS

    def body(q_ref, k_ref, kt_ref, vt_ref, do_ref, lt_ref, upto_ref, before_ref, tok_ref, dq_ref, dk_ref, dv_ref,
             z_buf, da_buf, dz_buf, att_buf):
        i = pl.program_id(1)
        last = 2 * i + 1

        @pl.when(i == 0)
        def _():
            dk_ref[...] = jnp.zeros_like(dk_ref)
            dv_ref[...] = jnp.zeros_like(dv_ref)

        lanes = [slice(pr * CHUNK, (pr + 1) * CHUNK) for pr in range(np_)]
        qs = [q_ref[:, lanes[pr]] for pr in range(np_)]
        dos = [do_ref[:, lanes[pr]] for pr in range(np_)]
        dobs = [do.astype(BF16) for do in dos]
        q_ts = [q.astype(F32).T.astype(BF16) for q in qs]
        do_ts = [do.T.astype(BF16) for do in dos]
        ltots = [_per_head(lt_ref[:, pr * CHUNK:pr * CHUNK + 1],
                           lt_ref[:, pr * CHUNK + HEAD_DIM:pr * CHUNK + HEAD_DIM + 1]) for pr in range(np_)]
        upto = upto_ref[...]
        before = before_ref[...]
        last0, last1 = slice(CHUNK - 1, CHUNK), slice(PAIR - 1, PAIR)

        def rows(kb):
            return pl.ds(pl.multiple_of(kb * PAIR, PAIR), PAIR)

        def ahead(pr, kb):
            return _dot(qs[pr], kt_ref[pr, kb]), _dot(dobs[pr], vt_ref[pr, kb])

        def behind(pr, kb, dq):
            dzb = dz_buf[pr]
            dk_ref[pr, kb] += _dot(q_ts[pr], dzb)
            dv_ref[pr, kb] += _dot(do_ts[pr], att_buf[pr])
            return dq + _dot(dzb, k_ref[rows(kb), lanes[pr]])

        def block(kb, carry, masked):
            out = []
            for pr in range(np_):
                dq, p0, p1, e0, e1 = carry[pr]
                z_next, da_next = ahead(pr, jnp.minimum(kb + 1, last))
                dq = behind(pr, jnp.maximum(kb - 1, 0), dq)
                z, sp, nlb, mask = _sb_scores(z_buf[pr], i * SB_Q, kb * CHUNK, masked)
                pin = _dot(nlb, upto) + _per_head(p0, p1)
                sig = jnp.exp(z - sp)
                att = jnp.exp(z - sp - (ltots[pr] - pin))
                if masked:
                    att = jnp.where(mask, att, 0.0)
                e = att * da_buf[pr]
                ebefore = _dot(e.astype(BF16), before) + _per_head(e0, e1)
                dz = e - sig * (e + ebefore)
                if masked:
                    dz = jnp.where(mask, dz, 0.0)
                z_buf[pr] = z_next
                da_buf[pr] = da_next
                dz_buf[pr] = dz.astype(BF16)
                att_buf[pr] = att.astype(BF16)
                out.append((dq, pin[:, last0], pin[:, last1],
                            ebefore[:, last0] + e[:, last0], ebefore[:, last1] + e[:, last1]))
            return tuple(out)

        for pr in range(np_):
            z_buf[pr], da_buf[pr] = ahead(pr, 0)
        dz_buf[...] = jnp.zeros_like(dz_buf)
        att_buf[...] = jnp.zeros_like(att_buf)
        zero = jnp.zeros((SB_Q, 1), F32)
        carry = ((jnp.zeros((SB_Q, CHUNK), F32), zero, zero, zero, zero),) * np_
        carry = lax.fori_loop(0, 2 * i, lambda kb, c: block(kb, c, False), carry)
        carry = block(last - 1, carry, True)
        carry = block(last, carry, True)
        for pr in range(np_):
            dq_ref[:, lanes[pr]] = behind(pr, last, carry[pr][0])

    blk = pl.BlockSpec((SB_Q, np_ * CHUNK), lambda p, i: (i, p))
    seq = pl.BlockSpec((2 * s, np_ * CHUNK), lambda p, i: (0, p))
    tiles = pl.BlockSpec((np_, s // CHUNK, CHUNK, PAIR), lambda p, i: (p, 0, 0, 0))
    tri = pl.BlockSpec((PAIR, PAIR), lambda p, i: (0, 0))
    return pl.pallas_call(
        body, name="bwd_sb", grid=(SB_W // CHUNK // np_, s // SB_Q),
        in_specs=[blk, seq, tiles, tiles, blk, blk, tri, tri, ORDER_ONLY],
        out_specs=[blk, tiles, tiles],
        out_shape=[jax.ShapeDtypeStruct((s, SB_W), F32)]
        + [jax.ShapeDtypeStruct((SB_W // CHUNK, s // CHUNK, CHUNK, PAIR), F32)] * 2,
        scratch_shapes=[pltpu.VMEM((np_, SB_Q, PAIR), F32), pltpu.VMEM((np_, SB_Q, PAIR), F32),
                        pltpu.VMEM((np_, SB_Q, PAIR), BF16), pltpu.VMEM((np_, SB_Q, PAIR), BF16)],
        compiler_params=_cp("parallel", "arbitrary"),
    )(qn, kst, ktr, vtr, dy, ltot, _pair_tri("upto"), _pair_tri("before"), tok)


def _head_sum(row):
    acc = row[:, 0:HEAD_DIM]
    for h in range(1, SB_W // HEAD_DIM):
        acc = acc + row[:, h * HEAD_DIM:(h + 1) * HEAD_DIM]
    return acc


def _bwd_qk(proj, dqs, dkn, dv, qg, kg):
    s = proj.shape[0]
    tm = TOKEN_TILE
    tiles = pl.BlockSpec((SB_W // CHUNK, tm // CHUNK, CHUNK, PAIR), lambda i: (0, i, 0, 0))

    def body(q_ref, k_ref, dqs_ref, dkn_ref, dv_ref, qg_ref, kg_ref, dp_ref, dqg_ref, dkg_ref, qacc, kacc):
        i = pl.program_id(0)
        gm = _group_mean_matrix(SB_W, HEAD_DIM)

        def one(x, dy, g, acc):
            r = lax.rsqrt(_dot_split(x * x, gm) + RMS_EPS)
            xh = x * r
            dxh = dy * g
            _accumulate(acc, _colsum(dy * xh))
            return r * (dxh - xh * _dot_split(dxh * xh, gm))

        dq = one(q_ref[...], dqs_ref[...] * (HEAD_DIM ** -0.5), qg_ref[...], qacc)
        dk = one(k_ref[...], _unstack_heads(_load_transposed(dkn_ref)), kg_ref[...], kacc)
        dp_ref[...] = jnp.concatenate([dq, dk, _unstack_heads(_load_transposed(dv_ref))], axis=1).astype(BF16)

        @pl.when(i == pl.num_programs(0) - 1)
        def _():
            dqg_ref[...] = _head_sum(qacc[...])
            dkg_ref[...] = _head_sum(kacc[...])

    return pl.pallas_call(
        body, name="bwd_qk", grid=(s // tm,),
        in_specs=[_rows(tm, SB_W, OFF_SB // SB_W), _rows(tm, SB_W, OFF_SB // SB_W + 1),
                  _rows(tm, SB_W), tiles, tiles, _whole((1, SB_W)), _whole((1, SB_W))],
        out_specs=[_rows(tm, 3 * SB_W), _whole((1, HEAD_DIM)), _whole((1, HEAD_DIM))],
        out_shape=[jax.ShapeDtypeStruct((s, 3 * SB_W), BF16), jax.ShapeDtypeStruct((1, HEAD_DIM), F32),
                   jax.ShapeDtypeStruct((1, HEAD_DIM), F32)],
        scratch_shapes=[pltpu.VMEM((1, SB_W), F32), pltpu.VMEM((1, SB_W), F32)],
        compiler_params=_cp("arbitrary"),
    )(proj, proj, dqs, dkn, dv, qg, kg)


def _bwd_sg(proj, dy, lg, lb, w, bias):
    s = proj.shape[0]
    tm = TOKEN_TILE
    nh = SG_W // HEAD_DIM

    def body(p_ref, dy_ref, lg_ref, lb_ref, w_ref, bias_ref, dp_ref, dlg_ref, dlb_ref, dw_ref, db_ref, dbias):
        i = pl.program_id(0)
        uv = p_ref[...]
        ge = _gelu(uv)
        u = ge[:, :SG_W]
        r, xh = _ln_stats(ge[:, SG_W:])
        vln = (xh * lg_ref[...] + lb_ref[...]).astype(BF16)
        dy = dy_ref[...]
        tril, lane_head = _sg_masks()

        @pl.when(i == 0)
        def _():
            dw_ref[...] = jnp.zeros_like(dw_ref)
            dbias[...] = jnp.zeros_like(dbias)

        dus, dvlns = [], []
        for c in range(tm // CHUNK):
            rows = slice(c * CHUNK, (c + 1) * CHUNK)
            vc = vln[rows]
            dus.append(dy[rows] * _sg_mix(w_ref, bias_ref, vc, tril, lane_head))
            dm = dy[rows] * u[rows]
            dbias[...] += dm
            dvc = jnp.zeros((CHUNK, SG_W), F32)
            for h in range(nh):
                dmh = jnp.where(lane_head == h, dm, 0.0).astype(BF16)
                dw_ref[h] += jnp.where(tril, _dot_nt(dmh, vc), 0.0)
                wm = jnp.where(tril, w_ref[h], 0.0).astype(BF16)
                dvc = dvc + _dot_tn(wm, dmh)
            dvlns.append(dvc)
        du = jnp.concatenate(dus, axis=0)
        dvln = jnp.concatenate(dvlns, axis=0)
        _accumulate(dlg_ref, _colsum(dvln * xh))
        _accumulate(dlb_ref, _colsum(dvln))
        dv = _ln_bwd(xh, r, lg_ref[...], dvln)
        dp_ref[...] = (jnp.concatenate([du, dv], axis=1) * _gelu_grad(uv)).astype(BF16)

        @pl.when(i == pl.num_programs(0) - 1)
        def _():
            lane = lax.broadcasted_iota(jnp.int32, (CHUNK, CHUNK), 1)
            acc = dbias[...]
            out = jnp.zeros((CHUNK, CHUNK), F32)
            for h in range(nh):
                hs = jnp.sum(acc[:, h * HEAD_DIM:(h + 1) * HEAD_DIM], axis=1, keepdims=True)
                out = out + jnp.where(lane == h, hs, 0.0)
            db_ref[...] = out

    return pl.pallas_call(
        body, name="bwd_sg", grid=(s // tm,),
        in_specs=[_rows(tm, 2 * SG_W, 1), _rows(tm, SG_W), _whole((1, SG_W)), _whole((1, SG_W)),
                  _whole((nh, CHUNK, CHUNK)), _whole((CHUNK, SG_W))],
        out_specs=[_rows(tm, 2 * SG_W), _whole((1, SG_W)), _whole((1, SG_W)), _whole((nh, CHUNK, CHUNK)),
                   _whole((CHUNK, CHUNK))],
        out_shape=[jax.ShapeDtypeStruct((s, 2 * SG_W), BF16), jax.ShapeDtypeStruct((1, SG_W), F32),
                   jax.ShapeDtypeStruct((1, SG_W), F32), jax.ShapeDtypeStruct((nh, CHUNK, CHUNK), F32),
                   jax.ShapeDtypeStruct((CHUNK, CHUNK), F32)],
        scratch_shapes=[pltpu.VMEM((CHUNK, SG_W), F32)],
        compiler_params=_cp("arbitrary"),
    )(proj, dy, lg, lb, w, bias)


def _bwd_conv(proj, dy, w, b, lg, lb):
    s = proj.shape[0]

    def body(p_ref, dy_ref, w_ref, b_ref, lg_ref, lb_ref, dp_ref, dw_ref, db_ref, dlg_ref, dlb_ref,
             abuf, dcbuf):
        _glu_fill(p_ref, abuf, s)
        dcbuf[pl.ds(s, CHUNK), :] = jnp.zeros((CHUNK, CONV_W), F32)
        dw_ref[...] = jnp.zeros_like(dw_ref)

        def chunk(c, carry):
            db, dlg, dlb = carry
            r0 = pl.multiple_of(c * CHUNK, CHUNK)
            acc, win = _conv_window(abuf, r0, w_ref)
            r, xh = _ln_stats(acc + b_ref[...])
            ln = xh * lg_ref[...] + lb_ref[...]
            sg = _sigmoid(ln)
            dl = dy_ref[pl.ds(r0, CHUNK), :] * (sg * (1.0 + ln * (1.0 - sg)))
            dc = _ln_bwd(xh, r, lg_ref[...], dl)
            dcbuf[pl.ds(r0, CHUNK), :] = dc
            for k in range(CONV_K):
                sh = win if k == 0 else pltpu.roll(win, k, axis=0)
                dw_ref[CONV_K - 1 - k:CONV_K - k, :] += _colsum(dc * sh[32:, :])
            return db + _colsum(dc), dlg + _colsum(dl * xh), dlb + _colsum(dl)

        zero = jnp.zeros((1, CONV_W), F32)
        db, dlg, dlb = lax.fori_loop(0, s // CHUNK, chunk, (zero, zero, zero))
        db_ref[...] = db
        dlg_ref[...] = dlg
        dlb_ref[...] = dlb

        def chunk_back(c, carry):
            r0 = pl.multiple_of(c * CHUNK, CHUNK)
            win = dcbuf[pl.ds(r0, CHUNK + 32), :]
            da = jnp.zeros((CHUNK, CONV_W), F32)
            for k in range(CONV_K):
                sh = win if k == 0 else pltpu.roll(win, CHUNK + 32 - k, axis=0)
                da = da + sh[:CHUNK, :] * w_ref[CONV_K - 1 - k:CONV_K - k, :]
            pv = p_ref[pl.ds(r0, CHUNK), :]
            val, sg = pv[:, :CONV_W], _sigmoid(pv[:, CONV_W:])
            dp_ref[pl.ds(r0, CHUNK), :] = jnp.concatenate([da * sg, da * val * sg * (1.0 - sg)], axis=1).astype(BF16)
            return carry

        lax.fori_loop(0, s // CHUNK, chunk_back, 0)

    row = _whole((1, CONV_W))
    return pl.pallas_call(
        body, name="bwd_conv", grid=(1,),
        in_specs=[pl.BlockSpec((s, 2 * CONV_W), lambda i: (0, 0)), _whole((s, CONV_W)),
                  _whole((CONV_K, CONV_W)), row, row, row],
        out_specs=[_whole((s, 2 * CONV_W)), _whole((CONV_K, CONV_W)), row, row, row],
        out_shape=[jax.ShapeDtypeStruct((s, 2 * CONV_W), BF16), jax.ShapeDtypeStruct((CONV_K, CONV_W), F32)]
        + [jax.ShapeDtypeStruct((1, CONV_W), F32)] * 3,
        scratch_shapes=[pltpu.VMEM((s + CHUNK, CONV_W), F32), pltpu.VMEM((s + CHUNK, CONV_W), F32)],
        compiler_params=_cp("arbitrary"),
    )(proj, dy, w, b, lg, lb)


def _bwd_in(dpc, dps, dpb, x, g, w, dxm):
    s = x.shape[0]
    tm = TOKEN_TILE

    def body(dpc_ref, dps_ref, dpb_ref, x_ref, g_ref, w_ref, dxm_ref, dx_ref, h_ref, dp_ref, dg_ref):
        dp = jnp.concatenate([dpc_ref[...], dps_ref[...], dpb_ref[...]], axis=1)
        dp_ref[...] = dp
        dh = _dot(dp, w_ref[...])
        r, xh = _rms_stats(x_ref[...])
        h_ref[...] = (xh * g_ref[...]).astype(BF16)
        dx, dgrow = _rms_bwd(xh, r, g_ref[...], dh)
        dx_ref[...] = dxm_ref[...] + dx
        _accumulate(dg_ref, _colsum(dgrow))

    return pl.pallas_call(
        body, name="bwd_in", grid=(s // tm,),
        in_specs=[_rows(tm, 2 * CONV_W), _rows(tm, 2 * SG_W), _rows(tm, 3 * SB_W), _rows(tm, D_MODEL),
                  _whole((1, D_MODEL)), _whole((IN_W, D_MODEL)), _rows(tm, D_MODEL)],
        out_specs=[_rows(tm, D_MODEL), _rows(tm, D_MODEL), _rows(tm, IN_W), _whole((1, D_MODEL))],
        out_shape=[jax.ShapeDtypeStruct((s, D_MODEL), F32), jax.ShapeDtypeStruct((s, D_MODEL), BF16),
                   jax.ShapeDtypeStruct((s, IN_W), BF16), jax.ShapeDtypeStruct((1, D_MODEL), F32)],
        compiler_params=_cp("arbitrary"),
    )(dpc, dps, dpb, x, g, w, dxm)


SMALL = ("mix_norm_g", "conv_w", "conv_b", "conv_ln_g", "conv_ln_b", "sg_ln_g", "sg_ln_b", "sg_w", "sg_b",
         "q_norm_g", "k_norm_g", "out_norm_g", "ffn_norm_g")
LARGE = ("w_in", "w_out", "w_gate_up", "w_down")


def _row(v):
    return v.reshape(1, -1)


def _layer_params(p, large, l):
    q = {k: v[l] for k, v in p.items()}
    return dict(
        q, **large,
        mix_norm_g=_row(q["mix_norm_g"]), conv_b=_row(q["conv_b"]), conv_ln_g=_row(q["conv_ln_g"]),
        conv_ln_b=_row(q["conv_ln_b"]), sg_ln_g=_row(q["sg_ln_g"]), sg_ln_b=_row(q["sg_ln_b"]),
        out_norm_g=_row(q["out_norm_g"]), ffn_norm_g=_row(q["ffn_norm_g"]),
        qg=_row(jnp.tile(q["q_norm_g"], SB_W // HEAD_DIM)), kg=_row(jnp.tile(q["k_norm_g"], SB_W // HEAD_DIM)),
        sg_bias=jnp.repeat(q["sg_b"].T, HEAD_DIM, axis=1),
    )


def _layer_fwd(x, q, tok, after_in, after_mixers, after_out):
    proj, qn, kn, vb, kt, vt = _fwd_in(x, q["mix_norm_g"], q["w_in"], q["qg"], q["kg"], tok)
    yc = _fwd_conv(proj, q["conv_w"], q["conv_b"], q["conv_ln_g"], q["conv_ln_b"], after_in(proj))
    ys = _fwd_sg(proj, q["sg_ln_g"], q["sg_ln_b"], q["sg_w"], q["sg_bias"])
    yb, lt = _fwd_sb(qn, kt, vb)
    rest, tok = after_mixers(yb)
    q = dict(q, **rest)
    xm = _fwd_out(yc, ys, yb, q["out_norm_g"], q["w_out"], x, tok)
    gu, xo = _fwd_ffn(xm, q["ffn_norm_g"], q["w_gate_up"], q["w_down"], after_out(xm))
    return xo, q, dict(x=x, proj=proj, qn=qn, kn=kn, kt=kt, vt=vt, lt=lt, yc=yc, ys=ys, yb=yb, xm=xm, gu=gu)


def _layer_bwd_ffn(dxo, q, st, tok):
    dgu, act, h2, dxm, d_ffn_g = _bwd_ffn(dxo, st["gu"], st["xm"], q["ffn_norm_g"], q["w_gate_up"], q["w_down"],
                                          tok)
    d_wgu, d_wd = _matmul_tn(dgu, h2, 512, D_MODEL), _matmul_tn(act, dxo, FFN_H // 2, D_MODEL)
    dyc, dys, dyb, cat, d_out_g = _bwd_out(dxm, st["yc"], st["ys"], st["yb"], q["out_norm_g"], q["w_out"], tok)
    return dxm, (dyc, dys, dyb, d_ffn_g, d_out_g), d_wgu, d_wd, _matmul_tn(cat, dxm, 512, D_MODEL)


def _layer_bwd_mix(dxm, carried, q, st, tok):
    dyc, dys, dyb, d_ffn_g, d_out_g = carried
    dqs, dkn, dv = _bwd_sb(st["qn"], st["kn"], st["kt"], st["vt"], dyb, st["lt"], tok)
    dpb, d_qg, d_kg = _bwd_qk(st["proj"], dqs, dkn, dv, q["qg"], q["kg"])
    dps, d_sg_lg, d_sg_lb, d_sg_w, d_sg_b = _bwd_sg(st["proj"], dys, q["sg_ln_g"], q["sg_ln_b"], q["sg_w"],
                                                    q["sg_bias"])
    dpc, d_conv_w, d_conv_b, d_conv_lg, d_conv_lb = _bwd_conv(st["proj"], dyc, q["conv_w"], q["conv_b"],
                                                              q["conv_ln_g"], q["conv_ln_b"])
    dx, h1, dp, d_mix_g = _bwd_in(dpc, dps, dpb, st["x"], q["mix_norm_g"], q["w_in"], dxm)
    d_win = _matmul_tn(dp, h1, 512, D_MODEL)
    small = dict(
        mix_norm_g=d_mix_g[0], conv_w=d_conv_w, conv_b=d_conv_b[0], conv_ln_g=d_conv_lg[0],
        conv_ln_b=d_conv_lb[0], sg_ln_g=d_sg_lg[0], sg_ln_b=d_sg_lb[0], sg_w=d_sg_w,
        sg_b=d_sg_b[:, :SG_W // HEAD_DIM].T, q_norm_g=d_qg[0], k_norm_g=d_kg[0], out_norm_g=d_out_g[0],
        ffn_norm_g=d_ffn_g[0])
    return dx, d_win, small


def _position():
    x, y, c = lax.axis_index("x"), lax.axis_index("y"), lax.axis_index("c")
    return x, y, c


def _flat(px, py, pc):
    return 4 * px + 2 * py + pc


IN_HBM = pl.BlockSpec(memory_space=pltpu.HBM)
IN_SEM = pl.BlockSpec(memory_space=pltpu.SEMAPHORE)
EFFECT = pltpu.SideEffectType.DATAFLOW_SIDE_EFFECTING
COPIES = dict(scatter=7, spread=7, spread_chips=4, **{"pass": 3})


def _exchange_copies(kinds, src_refs, land_refs, send_sems, recv_sems, layer, arrival):
    x, y, c = _position()
    me = _flat(x, y, c)
    everyone = [(x ^ (k >> 2 & 1), y ^ (k >> 1 & 1), c ^ (k & 1)) for k in range(1, N_DEV)]
    sibling = (x, y, 1 - c)
    chips = [(1 - x, y, c), (x, 1 - y, c), (1 - x, 1 - y, c)]
    out = []
    srcs = iter(src_refs)
    for kind, land in zip(kinds, land_refs):
        land = land if layer is None else land.at[layer]
        if kind == "scatter":
            src = next(srcs)
            moves = [(src.at[_flat(*p)], me, _flat(*p), p) for p in everyone]
        elif kind in ("spread", "spread_chips"):
            src = next(srcs)
            moves = [(src, me, _flat(*p), p) for p in (everyone if kind == "spread" else [sibling] + chips)]
        else:
            moves = [(land.at[_flat(*p)], _flat(*p), _flat(p[0], p[1], 1 - c), sibling) for p in chips]
        for src_block, there, here, peer in moves:
            n = len(out)
            out.append(pltpu.make_async_remote_copy(
                src_ref=src_block, dst_ref=land.at[here if arrival else there], send_sem=send_sems.at[n],
                recv_sem=recv_sems.at[n], device_id=peer, device_id_type=MESH))
    return out


def _exchange_start(kinds, srcs, lands, after, name, layer=None):
    ns, n = len(srcs), len(srcs) + len(lands)
    sems = sum(COPIES[k] for k in kinds)

    def body(*refs):
        send_sems, recv_sems = refs[n + 1], refs[n + 2]
        for cp in _exchange_copies(kinds, refs[:ns], refs[ns:n], send_sems, recv_sems, layer, arrival=False):
            cp.start()
        refs[-1][...] = jnp.zeros_like(refs[-1])

    thru = [pltpu.HBM(a.shape, a.dtype) for a in (*srcs, *lands)]
    outs = pl.pallas_call(
        body, name=name,
        out_shape=(pltpu.SemaphoreType.DMA((sems,)), pltpu.SemaphoreType.DMA((sems,)), *thru,
                   jax.ShapeDtypeStruct((8, 128), F32)),
        in_specs=[IN_HBM] * n + [ORDER_ONLY],
        out_specs=(IN_SEM, IN_SEM, *[IN_HBM] * n, pl.BlockSpec(memory_space=pltpu.VMEM)),
        input_output_aliases={i: 2 + i for i in range(n)},
        compiler_params=pltpu.CompilerParams(has_side_effects=EFFECT),
    )(*[pltpu.with_memory_space_constraint(a, pltpu.HBM) for a in (*srcs, *lands)], after)
    return kinds, outs[0], outs[1], list(outs[2:2 + ns]), list(outs[2 + ns:2 + n]), outs[-1]


def _exchange_wait(pending, after, name, layer=None):
    kinds, send_sems, recv_sems, srcs, lands, _ = pending
    after = list(after) if isinstance(after, (list, tuple)) else [after]
    ns, n = len(srcs), len(srcs) + len(lands)

    def body(*refs):
        for cp in _exchange_copies(kinds, refs[:ns], refs[ns:n], refs[n], refs[n + 1], layer, arrival=True):
            cp.wait_send()
            cp.wait_recv()

    thru = [pltpu.HBM(a.shape, a.dtype) for a in (*srcs, *lands)]
    outs = pl.pallas_call(
        body, name=name, out_shape=tuple(thru),
        in_specs=[IN_HBM] * n + [IN_SEM, IN_SEM] + [ORDER_ONLY] * len(after),
        out_specs=tuple([IN_HBM] * n),
        input_output_aliases={i: i for i in range(n)},
        compiler_params=pltpu.CompilerParams(has_side_effects=EFFECT),
    )(*srcs, *lands, send_sems, recv_sems, *after)
    return list(outs[ns:])


def _landing(block, me):
    land = lax.empty((N_DEV,) + block.shape, block.dtype)
    return lax.dynamic_update_index_in_dim(land, block, me, 0)


def _adamw(parts, w, m, v, tr):
    groups, rows, cols = w.shape

    def body(p_ref, w_ref, m_ref, v_ref, g_ref, d_ref, nm_ref, nv_ref):
        g = p_ref[0].astype(F32)
        for j in range(1, N_DEV):
            g = g + p_ref[j].astype(F32)
        g_ref[...] = g
        m = ADAM_B1 * m_ref[...] + (1.0 - ADAM_B1) * g
        v = ADAM_B2 * v_ref[...] + (1.0 - ADAM_B2) * (g * g)
        nm_ref[...] = m
        nv_ref[...] = v
        m_hat = m / (1.0 - ADAM_B1 ** ADAM_STEP)
        v_hat = v / (1.0 - ADAM_B2 ** ADAM_STEP)
        d_ref[...] = -ADAM_LR * (m_hat / (jnp.sqrt(v_hat) + ADAM_EPS) + ADAM_WD * w_ref[...])

    blk = pl.BlockSpec((None, tr, cols), lambda g, i: (g, i, 0))
    return pl.pallas_call(
        body, name="adamw", grid=(groups, rows // tr),
        in_specs=[pl.BlockSpec((None, N_DEV, tr, cols), lambda g, i: (g, 0, i, 0)), blk, blk, blk],
        out_specs=[blk] * 4,
        out_shape=[jax.ShapeDtypeStruct((groups, rows, cols), F32)] * 4,
        compiler_params=_cp("parallel", "parallel"),
    )(parts, w, m, v)


def _row_tile(rows):
    for cand in range(min(rows, 512) // 8 * 8, 7, -8):
        if rows % cand == 0:
            return cand
    return rows


def _with_own_block(land, blocks, layer, me):
    own = lax.dynamic_index_in_dim(blocks, me, 0, keepdims=True)[None]
    return lax.dynamic_update_slice(land, own, (layer, me, 0, 0))


PACK_LANES = 128


def _pack_layers(arrs):
    parts = []
    for a in arrs:
        flat = a.reshape(a.shape[0], -1)
        parts.append(jnp.pad(flat, ((0, 0), (0, -flat.shape[1] % (8 * PACK_LANES)))))
    return jnp.concatenate(parts, axis=1).reshape(arrs[0].shape[0], -1, PACK_LANES)


def _unpack_layers(packed, shapes):
    flat = packed.reshape(packed.shape[0], -1)
    outs, off = [], 0
    for shp in shapes:
        size = 1
        for d in shp[1:]:
            size *= d
        outs.append(flat[:, off:off + size].reshape(shp))
        off += size + (-size % (8 * PACK_LANES))
    return outs


def kernel(x, mix_norm_g, w_in, conv_w, conv_b, conv_ln_g, conv_ln_b, sg_ln_g, sg_ln_b, sg_w, sg_b, q_norm_g, k_norm_g, out_norm_g, w_out, ffn_norm_g, w_gate_up, w_down, loss_target, m_mix_norm_g, m_w_in, m_conv_w, m_conv_b, m_conv_ln_g, m_conv_ln_b, m_sg_ln_g, m_sg_ln_b, m_sg_w, m_sg_b, m_q_norm_g, m_k_norm_g, m_out_norm_g, m_w_out, m_ffn_norm_g, m_w_gate_up, m_w_down, v_mix_norm_g, v_w_in, v_conv_w, v_conv_b, v_conv_ln_g, v_conv_ln_b, v_sg_ln_g, v_sg_ln_b, v_sg_w, v_sg_b, v_q_norm_g, v_k_norm_g, v_out_norm_g, v_w_out, v_ffn_norm_g, v_w_gate_up, v_w_down):
    names = SMALL[:1] + LARGE[:1] + SMALL[1:12] + LARGE[1:2] + SMALL[12:] + LARGE[2:]
    w = dict(mix_norm_g=mix_norm_g, w_in=w_in, conv_w=conv_w, conv_b=conv_b, conv_ln_g=conv_ln_g,
             conv_ln_b=conv_ln_b, sg_ln_g=sg_ln_g, sg_ln_b=sg_ln_b, sg_w=sg_w, sg_b=sg_b, q_norm_g=q_norm_g,
             k_norm_g=k_norm_g, out_norm_g=out_norm_g, w_out=w_out, ffn_norm_g=ffn_norm_g,
             w_gate_up=w_gate_up, w_down=w_down)
    m = dict(mix_norm_g=m_mix_norm_g, w_in=m_w_in, conv_w=m_conv_w, conv_b=m_conv_b, conv_ln_g=m_conv_ln_g,
             conv_ln_b=m_conv_ln_b, sg_ln_g=m_sg_ln_g, sg_ln_b=m_sg_ln_b, sg_w=m_sg_w, sg_b=m_sg_b,
             q_norm_g=m_q_norm_g, k_norm_g=m_k_norm_g, out_norm_g=m_out_norm_g, w_out=m_w_out,
             ffn_norm_g=m_ffn_norm_g, w_gate_up=m_w_gate_up, w_down=m_w_down)
    v = dict(mix_norm_g=v_mix_norm_g, w_in=v_w_in, conv_w=v_conv_w, conv_b=v_conv_b, conv_ln_g=v_conv_ln_g,
             conv_ln_b=v_conv_ln_b, sg_ln_g=v_sg_ln_g, sg_ln_b=v_sg_ln_b, sg_w=v_sg_w, sg_b=v_sg_b,
             q_norm_g=v_q_norm_g, k_norm_g=v_k_norm_g, out_norm_g=v_out_norm_g, w_out=v_w_out,
             ffn_norm_g=v_ffn_norm_g, w_gate_up=v_w_gate_up, w_down=v_w_down)
    xpos, ypos, cpos = _position()
    me = _flat(xpos, ypos, cpos)
    conv_cols = conv_w.shape[-1]
    no_token = jnp.zeros((8, 128), F32)
    w, m, v = (dict(t, w_in=jnp.swapaxes(t["w_in"], 1, 2), w_gate_up=jnp.swapaxes(t["w_gate_up"], 1, 2))
               for t in (w, m, v))
    shards = {k: w[k].astype(BF16) for k in LARGE}
    full_shape = dict(w_in=(IN_W, D_MODEL), w_out=(D_MODEL, D_MODEL), w_gate_up=(2 * FFN_H, D_MODEL),
                      w_down=(FFN_H, D_MODEL))

    def gather_start(srcs, after, tag):
        return _exchange_start(["spread_chips"] * len(srcs), srcs, [_landing(a, me) for a in srcs], after,
                               f"gather_start_{tag}")

    def gather_pass(pending, after, tag):
        lands = _exchange_wait(pending, after, f"gather_wait_{tag}")
        return _exchange_start(["pass"] * len(lands), [], lands, after, f"gather_pass_{tag}")

    def gathered(pending, keys, after, tag):
        lands = _exchange_wait(pending, after, f"gather_passed_{tag}")
        return {k: a.reshape(full_shape[k]) for k, a in zip(keys, lands)}, lands[len(keys):]

    first, later = ("w_in",), ("w_out", "w_gate_up", "w_down")
    act = x[0]
    head = gather_start([shards["w_in"][0], w["conv_w"]], act, "0")
    tail = gather_start([shards[k][0] for k in later], head[5], "0_later")
    head = gather_pass(head, tail[5], "0")
    large, (conv_blocks,) = gathered(head, first, head[5], "0")
    conv_full = jnp.transpose(conv_blocks, (1, 2, 0, 3)).reshape(DEPTH, CONV_K, CONV_W)
    small_w = dict({k: w[k] for k in SMALL}, conv_w=conv_full)
    qs, stash = [], []
    for l in range(DEPTH):
        coming = {}
        more = l + 1 < DEPTH

        def next_start(after):
            coming["first"] = gather_start([shards[k][l + 1] for k in LARGE], after, str(l + 1))
            return coming["first"][5]

        def after_in(proj):
            return next_start(proj) if more and l == 0 else no_token

        def after_mixers(y_sb):
            if l > 0:
                return {}, no_token
            passing = gather_pass(tail, y_sb, "0_later")
            return gathered(passing, later, passing[5], "0_later")[0], passing[5]

        def after_out(x_mid):
            if not more:
                return no_token
            coming["second"] = gather_pass(coming["first"], x_mid, str(l + 1))
            return coming["second"][5]

        token = next_start(act) if more and l > 0 else no_token
        act, q, st = _layer_fwd(act, _layer_params(small_w, large, l), token, after_in, after_mixers, after_out)
        qs.append(q)
        stash.append(st)
        if more:
            large, _ = gathered(coming["second"], LARGE, act, str(l + 1))

    loss, dx = _loss_head(act, loss_target[0])
    loss = lax.psum(loss[0, 0], ("x", "y", "c"))

    replicated = tuple(k for k in SMALL if k != "conv_w")
    small_rows = _pack_layers([w[k][:1] for k in replicated]).shape[1]
    conv_rows = _pack_layers([conv_full[:1]]).shape[1]
    group_a, group_b = ("w_gate_up", "w_down", "w_out"), ("w_in",)
    blocks = lambda k, a: a.reshape((N_DEV,) + w[k].shape[1:])
    land_a = [lax.empty((DEPTH, N_DEV) + w[k].shape[1:], BF16) for k in group_a]
    land_b = [lax.empty((DEPTH, N_DEV) + w[k].shape[1:], BF16) for k in group_b]
    land_b.append(lax.empty((DEPTH, N_DEV, small_rows + conv_rows, PACK_LANES), F32))
    pend_a = pend_b = None
    token = no_token
    for l in reversed(range(DEPTH)):
        dxm, carried, d_wgu, d_wd, d_wo = _layer_bwd_ffn(dx, qs[l], stash[l], token)
        srcs = [blocks(k, a) for k, a in zip(group_a, (d_wgu, d_wd, d_wo))]
        if pend_a is not None:
            land_a = _exchange_wait(pend_a, d_wo, f"grads_a_wait_{l + 1}", layer=l + 1)
        land_a = [_with_own_block(ld, a, l, me) for ld, a in zip(land_a, srcs)]
        pend_a = _exchange_start(["scatter"] * 3, srcs, land_a, dxm, f"grads_a_start_{l}", layer=l)
        dx, d_win, small = _layer_bwd_mix(dxm, carried, qs[l], stash[l], pend_a[5])
        packed = _pack_layers([small[k][None] for k in replicated + ("conv_w",)])[0]
        srcs = [blocks("w_in", d_win)]
        if pend_b is not None:
            land_b = _exchange_wait(pend_b, d_win, f"grads_b_wait_{l + 1}", layer=l + 1)
        land_b = [_with_own_block(land_b[0], srcs[0], l, me),
                  lax.dynamic_update_slice(land_b[1], packed[None, None], (l, me, 0, 0))]
        pend_b = _exchange_start(["scatter", "spread"], srcs + [packed], land_b, dx, f"grads_b_start_{l}",
                                 layer=l)
        token = pend_b[5]

    land_a = _exchange_wait(pend_a, token, "grads_a_wait_0", layer=0)
    res = {}
    for k, parts in zip(group_a, land_a):
        res[k] = _adamw(parts, w[k], m[k], v[k], _row_tile(w[k].shape[1]))
    land_b = _exchange_wait(pend_b, [res[k][0] for k in group_a], "grads_b_wait_0", layer=0)
    res["w_in"] = _adamw(land_b[0], w["w_in"], m["w_in"], v["w_in"], _row_tile(w["w_in"].shape[1]))
    small_parts = land_b[1]
    updated = _adamw(small_parts, *(_pack_layers([t[k] for k in replicated]) for t in (w, m, v)), small_rows)
    unpacked = [_unpack_layers(o, [w[k].shape for k in replicated]) for o in updated]
    res.update({k: [u[i] for u in unpacked] for i, k in enumerate(replicated)})
    conv_parts = small_parts[:, :, small_rows:].reshape(DEPTH, N_DEV, -1)[:, :, :CONV_K * CONV_W]
    conv_parts = lax.dynamic_slice_in_dim(conv_parts.reshape(DEPTH, N_DEV, CONV_K, CONV_W), me * conv_cols,
                                          conv_cols, axis=3)
    res["conv_w"] = _adamw(conv_parts, w["conv_w"], m["conv_w"], v["conv_w"], CONV_K)
    for k in ("w_in", "w_gate_up"):
        res[k] = [jnp.swapaxes(a, 1, 2) for a in res[k]]

    return (loss, dx[None], *[res[k][0] for k in names], *[res[k][1] for k in names],
            *[res[k][2] for k in names], *[res[k][3] for k in names])
```

```python
import jax
import jax.numpy as jnp
from jax import lax
from jax.experimental import pallas as pl
from jax.experimental.pallas import tpu as pltpu

F32 = jnp.float32
BF16 = jnp.bfloat16

D_MODEL = 1024
DEPTH = 4
HEAD_DIM = 64
CONV_W = 256
SG_W = 256
SB_W = 512
IN_W = 2560
FFN_H = 2816
CONV_K = 31
CHUNK = 128
OFF_SG = 2 * CONV_W
OFF_SB = OFF_SG + 2 * SG_W
RMS_EPS = 1e-6
LN_EPS = 1e-5
N_DEV = 8
MESH = pl.DeviceIdType.MESH

ADAM_LR = 0.001
ADAM_B1 = 0.9
ADAM_B2 = 0.999
ADAM_EPS = 1e-08
ADAM_WD = 0.01
ADAM_STEP = 10

TOKEN_TILE = 256
VMEM_LIMIT = 56 * 1024 * 1024


def _cp(*sem):
    return pltpu.CompilerParams(dimension_semantics=sem or None, vmem_limit_bytes=VMEM_LIMIT)


def _dot(a, b):
    return jnp.dot(a, b, preferred_element_type=F32)


def _dot_nt(a, b):
    return lax.dot_general(a, b, (((1,), (1,)), ((), ())), preferred_element_type=F32)


def _dot_tn(a, b):
    return lax.dot_general(a, b, (((0,), (0,)), ((), ())), preferred_element_type=F32)


def _dot_split(x, m):
    hi = x.astype(BF16)
    lo = (x - hi.astype(F32)).astype(BF16)
    return _dot(hi, m) + _dot(lo, m)


def _group_mean_matrix(width, group):
    r = lax.broadcasted_iota(jnp.int32, (width, width), 0) // group
    c = lax.broadcasted_iota(jnp.int32, (width, width), 1) // group
    return jnp.where(r == c, 1.0 / group, 0.0).astype(BF16)


def _sigmoid(x):
    return 1.0 / (1.0 + jnp.exp(-x))


def _gelu(x):
    return 0.5 * x * (1.0 + lax.erf(x * (2.0 ** -0.5)))


def _gelu_grad(x):
    return 0.5 * (1.0 + lax.erf(x * (2.0 ** -0.5))) + x * jnp.exp(-0.5 * x * x) * (0.5 * (2.0 / jnp.pi) ** 0.5)


def _rms_stats(x):
    r = lax.rsqrt(jnp.mean(x * x, axis=-1, keepdims=True) + RMS_EPS)
    return r, x * r


def _rms_bwd(xh, r, g, dy):
    dxh = dy * g
    dx = r * (dxh - xh * jnp.mean(dxh * xh, axis=-1, keepdims=True))
    return dx, dy * xh


def _ln_stats(x):
    mu = jnp.mean(x, axis=-1, keepdims=True)
    xc = x - mu
    r = lax.rsqrt(jnp.mean(xc * xc, axis=-1, keepdims=True) + LN_EPS)
    return r, xc * r


def _ln_bwd(xh, r, g, dy):
    dxh = dy * g
    return r * (dxh - jnp.mean(dxh, axis=-1, keepdims=True) - xh * jnp.mean(dxh * xh, axis=-1, keepdims=True))


def _colsum(x):
    return jnp.sum(x, axis=0, keepdims=True)


def _rows(tm, n, j=0):
    return pl.BlockSpec((tm, n), lambda i: (i, j))


def _whole(shape):
    return pl.BlockSpec(shape, lambda i: (0,) * len(shape))


ORDER_ONLY = pl.BlockSpec(memory_space=pl.ANY)


def _stack_heads(a):
    even = (lax.broadcasted_iota(jnp.int32, a.shape, 1) % (2 * HEAD_DIM)) < HEAD_DIM
    top = jnp.where(even, a, 0.0)
    bot = jnp.where(even, 0.0, a)
    parts = []
    for c in range(a.shape[0] // CHUNK):
        rows = slice(c * CHUNK, (c + 1) * CHUNK)
        parts += [top[rows], bot[rows]]
    return jnp.concatenate(parts, axis=0)


def _store_stacked(st, st_ref, tr_ref):
    st_ref[...] = st.astype(BF16)
    for p in range(SB_W // CHUNK):
        for c in range(st.shape[0] // (2 * CHUNK)):
            tile = st[2 * c * CHUNK:2 * (c + 1) * CHUNK, p * CHUNK:(p + 1) * CHUNK]
            tr_ref[p, c] = tile.T.astype(BF16)


def _load_transposed(tr_ref):
    rows = []
    for c in range(tr_ref.shape[1]):
        tiles = [tr_ref[p, c].T for p in range(SB_W // CHUNK)]
        rows.append(jnp.concatenate(tiles, axis=1))
    return jnp.concatenate(rows, axis=0)


def _unstack_heads(st):
    even = (lax.broadcasted_iota(jnp.int32, (CHUNK, st.shape[1]), 1) % (2 * HEAD_DIM)) < HEAD_DIM
    parts = []
    for c in range(st.shape[0] // (2 * CHUNK)):
        top = st[2 * c * CHUNK:(2 * c + 1) * CHUNK]
        bot = st[(2 * c + 1) * CHUNK:(2 * c + 2) * CHUNK]
        parts.append(jnp.where(even, top, bot))
    return jnp.concatenate(parts, axis=0)


def _fwd_in(x, g, w, qg, kg, tok):
    s = x.shape[0]
    tm = TOKEN_TILE

    def body(x_ref, g_ref, w_ref, qg_ref, kg_ref, tok_ref, proj_ref, qn_ref, kn_ref, vb_ref, kt_ref, vt_ref):
        r, xh = _rms_stats(x_ref[...])
        h = (xh * g_ref[...]).astype(BF16)
        proj = _dot_nt(h, w_ref[...])
        proj_ref[...] = proj
        gm = _group_mean_matrix(SB_W, HEAD_DIM)
        q = proj[:, OFF_SB:OFF_SB + SB_W]
        k = proj[:, OFF_SB + SB_W:OFF_SB + 2 * SB_W]
        rq = lax.rsqrt(_dot_split(q * q, gm) + RMS_EPS)
        rk = lax.rsqrt(_dot_split(k * k, gm) + RMS_EPS)
        qn_ref[...] = (q * rq * qg_ref[...] * (HEAD_DIM ** -0.5)).astype(BF16)
        _store_stacked(_stack_heads(k * rk * kg_ref[...]), kn_ref, kt_ref)
        _store_stacked(_stack_heads(proj[:, OFF_SB + 2 * SB_W:]), vb_ref, vt_ref)

    tiles = pl.BlockSpec((SB_W // CHUNK, tm // CHUNK, CHUNK, PAIR), lambda i: (0, i, 0, 0))
    tiles_shape = jax.ShapeDtypeStruct((SB_W // CHUNK, s // CHUNK, CHUNK, PAIR), BF16)
    return pl.pallas_call(
        body, name="fwd_in", grid=(s // tm,),
        in_specs=[_rows(tm, D_MODEL), _whole((1, D_MODEL)), _whole((IN_W, D_MODEL)),
                  _whole((1, SB_W)), _whole((1, SB_W)), ORDER_ONLY],
        out_specs=[_rows(tm, IN_W), _rows(tm, SB_W), _rows(2 * tm, SB_W), _rows(2 * tm, SB_W), tiles, tiles],
        out_shape=[jax.ShapeDtypeStruct((s, IN_W), F32), jax.ShapeDtypeStruct((s, SB_W), BF16),
                   jax.ShapeDtypeStruct((2 * s, SB_W), BF16), jax.ShapeDtypeStruct((2 * s, SB_W), BF16),
                   tiles_shape, tiles_shape],
        compiler_params=_cp("parallel"),
    )(x, g, w, qg, kg, tok)


SUBLANES = 8


def _shifted(win, back):
    n = win.shape[0]
    turned = [win] + [pltpu.roll(win, b if back else n - b, axis=0) for b in range(1, SUBLANES)]

    def shifted(k):
        whole, part = divmod(k, SUBLANES)
        start = 32 - whole * SUBLANES if back else whole * SUBLANES
        return turned[part][start:start + CHUNK, :]

    return shifted


def _conv_window(abuf, r0, w_ref):
    shifted = _shifted(abuf[pl.ds(pl.multiple_of(r0 + CHUNK - 32, 32), CHUNK + 32), :], back=True)
    acc = jnp.zeros((CHUNK, CONV_W), F32)
    for k in range(CONV_K):
        acc = acc + shifted(k) * w_ref[CONV_K - 1 - k:CONV_K - k, :]
    return acc, shifted


def _glu_fill(p_ref, abuf, s):
    abuf[0:CHUNK, :] = jnp.zeros((CHUNK, CONV_W), F32)

    def fill(c, carry):
        r0 = pl.multiple_of(c * CHUNK, CHUNK)
        pv = p_ref[pl.ds(r0, CHUNK), :]
        abuf[pl.ds(r0 + CHUNK, CHUNK), :] = pv[:, :CONV_W] * _sigmoid(pv[:, CONV_W:])
        return carry

    lax.fori_loop(0, s // CHUNK, fill, 0)


def _fwd_conv(proj, w, b, lg, lb, tok):
    s = proj.shape[0]

    def body(p_ref, w_ref, b_ref, lg_ref, lb_ref, tok_ref, y_ref, abuf):
        _glu_fill(p_ref, abuf, s)

        def chunk(c, carry):
            r0 = pl.multiple_of(c * CHUNK, CHUNK)
            acc, _ = _conv_window(abuf, r0, w_ref)
            r, xh = _ln_stats(acc + b_ref[...])
            ln = xh * lg_ref[...] + lb_ref[...]
            y_ref[pl.ds(r0, CHUNK), :] = ln * _sigmoid(ln)
            return carry

        lax.fori_loop(0, s // CHUNK, chunk, 0)

    return pl.pallas_call(
        body, name="fwd_conv", grid=(1,),
        in_specs=[pl.BlockSpec((s, 2 * CONV_W), lambda i: (0, 0)), _whole((CONV_K, CONV_W)),
                  _whole((1, CONV_W)), _whole((1, CONV_W)), _whole((1, CONV_W)), ORDER_ONLY],
        out_specs=_whole((s, CONV_W)),
        out_shape=jax.ShapeDtypeStruct((s, CONV_W), F32),
        scratch_shapes=[pltpu.VMEM((s + CHUNK, CONV_W), F32)],
        compiler_params=_cp("arbitrary"),
    )(proj, w, b, lg, lb, tok)


def _sg_masks():
    row = lax.broadcasted_iota(jnp.int32, (CHUNK, CHUNK), 0)
    col = lax.broadcasted_iota(jnp.int32, (CHUNK, CHUNK), 1)
    lane_head = lax.broadcasted_iota(jnp.int32, (CHUNK, SG_W), 1) // HEAD_DIM
    return row >= col, lane_head


def _sg_mix(w_ref, bias_ref, vc, tril, lane_head):
    mixed = bias_ref[...]
    for h in range(SG_W // HEAD_DIM):
        wm = jnp.where(tril, w_ref[h], 0.0).astype(BF16)
        mixed = mixed + jnp.where(lane_head == h, _dot(wm, vc), 0.0)
    return mixed


def _fwd_sg(proj, lg, lb, w, bias):
    s = proj.shape[0]
    tm = TOKEN_TILE

    def body(p_ref, lg_ref, lb_ref, w_ref, bias_ref, y_ref):
        ge = _gelu(p_ref[...])
        u = ge[:, :SG_W]
        r, xh = _ln_stats(ge[:, SG_W:])
        vln = (xh * lg_ref[...] + lb_ref[...]).astype(BF16)
        tril, lane_head = _sg_masks()
        for c in range(tm // CHUNK):
            rows = slice(c * CHUNK, (c + 1) * CHUNK)
            y_ref[rows, :] = u[rows] * _sg_mix(w_ref, bias_ref, vln[rows], tril, lane_head)

    return pl.pallas_call(
        body, name="fwd_sg", grid=(s // tm,),
        in_specs=[_rows(tm, 2 * SG_W, 1), _whole((1, SG_W)), _whole((1, SG_W)),
                  _whole((SG_W // HEAD_DIM, CHUNK, CHUNK)), _whole((CHUNK, SG_W))],
        out_specs=_rows(tm, SG_W),
        out_shape=jax.ShapeDtypeStruct((s, SG_W), F32),
        compiler_params=_cp("parallel"),
    )(proj, lg, lb, w, bias)


SB_Q = 2 * CHUNK
PAIR = 2 * CHUNK
SB_PAIRS = 2


def _pair_tri(kind):
    row = lax.broadcasted_iota(jnp.int32, (PAIR, PAIR), 0)
    col = lax.broadcasted_iota(jnp.int32, (PAIR, PAIR), 1)
    tri = {"after": row > col, "upto": row <= col, "before": row < col}[kind]
    return jnp.where(((row // CHUNK) == (col // CHUNK)) & tri, 1.0, 0.0).astype(BF16)


def _sb_scores(z, qpos0, kpos0, masked):
    sp = jnp.maximum(z, 0.0) + jnp.log(1.0 + jnp.exp(-jnp.abs(z)))
    if not masked:
        return z, sp, sp.astype(BF16), None
    row = lax.broadcasted_iota(jnp.int32, z.shape, 0)
    col = lax.broadcasted_iota(jnp.int32, z.shape, 1) % CHUNK
    mask = (kpos0 + col) < (qpos0 + row)
    return z, sp, jnp.where(mask, sp, 0.0).astype(BF16), mask


def _per_head(c0, c1):
    return jnp.concatenate([jnp.broadcast_to(c0, (SB_Q, CHUNK)), jnp.broadcast_to(c1, (SB_Q, CHUNK))], axis=1)


def _fwd_sb(qn, ktr, vst):
    s = qn.shape[0]
    np_ = SB_PAIRS

    def body(q_ref, k_ref, v_ref, after_ref, o_ref, lt_ref, z_buf, att_buf):
        i = pl.program_id(1)
        first = i * (SB_Q // CHUNK)
        last = first + SB_Q // CHUNK - 1
        after = after_ref[...]
        lanes = [slice(pr * CHUNK, (pr + 1) * CHUNK) for pr in range(np_)]
        qs = [q_ref[:, lanes[pr]] for pr in range(np_)]

        def rows(kb):
            return pl.ds(pl.multiple_of(kb * PAIR, PAIR), PAIR)

        def block(kb, carry, masked):
            out = []
            for pr in range(np_):
                acc, c0, c1 = carry[pr]
                z_next = _dot(qs[pr], k_ref[pr, jnp.maximum(kb - 1, 0)])
                pv = _dot(att_buf[pr], v_ref[rows(jnp.minimum(kb + 1, last)), lanes[pr]])
                z, sp, nlb, mask = _sb_scores(z_buf[pr], i * SB_Q, kb * CHUNK, masked)
                loc = _dot(nlb, after)
                att = jnp.exp(z - sp - loc - _per_head(c0, c1))
                if masked:
                    att = jnp.where(mask, att, 0.0)
                z_buf[pr] = z_next
                att_buf[pr] = att.astype(BF16)
                out.append((acc + pv, c0 + loc[:, 0:1] + nlb[:, 0:1].astype(F32),
                            c1 + loc[:, CHUNK:CHUNK + 1] + nlb[:, CHUNK:CHUNK + 1].astype(F32)))
            return tuple(out)

        for pr in range(np_):
            z_buf[pr] = _dot(qs[pr], k_ref[pr, last])
        att_buf[...] = jnp.zeros_like(att_buf)
        zero = jnp.zeros((SB_Q, 1), F32)
        carry = ((jnp.zeros((SB_Q, CHUNK), F32), zero, zero),) * np_
        for back in range(SB_Q // CHUNK):
            carry = block(last - back, carry, True)
        carry = lax.fori_loop(0, first, lambda j, c: block(first - 1 - j, c, False), carry)
        for pr, (acc, c0, c1) in enumerate(carry):
            o_ref[:, lanes[pr]] = acc + _dot(att_buf[pr], v_ref[rows(0), lanes[pr]])
            lt_ref[:, lanes[pr]] = jnp.concatenate([jnp.broadcast_to(c0, (SB_Q, HEAD_DIM)),
                                                    jnp.broadcast_to(c1, (SB_Q, HEAD_DIM))], axis=1)

    blk = pl.BlockSpec((SB_Q, np_ * CHUNK), lambda p, i: (i, p))
    seq = pl.BlockSpec((2 * s, np_ * CHUNK), lambda p, i: (0, p))
    return pl.pallas_call(
        body, name="fwd_sb", grid=(SB_W // CHUNK // np_, s // SB_Q),
        in_specs=[blk, pl.BlockSpec((np_, s // CHUNK, CHUNK, PAIR), lambda p, i: (p, 0, 0, 0)), seq,
                  pl.BlockSpec((PAIR, PAIR), lambda p, i: (0, 0))],
        out_specs=[blk, blk],
        out_shape=[jax.ShapeDtypeStruct((s, SB_W), F32)] * 2,
        scratch_shapes=[pltpu.VMEM((np_, SB_Q, PAIR), F32), pltpu.VMEM((np_, SB_Q, PAIR), BF16)],
        compiler_params=_cp("parallel", "parallel"),
    )(qn, ktr, vst, _pair_tri("after"))


def _group_norms(yc, ys, yb):
    return [_rms_stats(yc), _rms_stats(ys), _rms_stats(yb)]


def _fwd_out(yc, ys, yb, g, w, x, tok):
    s = x.shape[0]
    tm = TOKEN_TILE

    def body(yc_ref, ys_ref, yb_ref, g_ref, w_ref, x_ref, tok_ref, o_ref):
        stats = _group_norms(yc_ref[...], ys_ref[...], yb_ref[...])
        cat = jnp.concatenate([xh for _, xh in stats], axis=1) * g_ref[...]
        o_ref[...] = x_ref[...] + _dot(cat.astype(BF16), w_ref[...])

    return pl.pallas_call(
        body, name="fwd_out", grid=(s // tm,),
        in_specs=[_rows(tm, CONV_W), _rows(tm, SG_W), _rows(tm, SB_W), _whole((1, D_MODEL)),
                  _whole((D_MODEL, D_MODEL)), _rows(tm, D_MODEL), ORDER_ONLY],
        out_specs=_rows(tm, D_MODEL),
        out_shape=jax.ShapeDtypeStruct((s, D_MODEL), F32),
        compiler_params=_cp("parallel"),
    )(yc, ys, yb, g, w, x, tok)


def _fwd_ffn(x, g, wgu, wd, tok):
    s = x.shape[0]
    tm = TOKEN_TILE

    def body(x_ref, g_ref, wgu_ref, wd_ref, tok_ref, gu_ref, o_ref):
        x = x_ref[...]
        r, xh = _rms_stats(x)
        gu = _dot_nt((xh * g_ref[...]).astype(BF16), wgu_ref[...])
        gu_ref[...] = gu
        gate = gu[:, :FFN_H]
        act = gate * _sigmoid(gate) * gu[:, FFN_H:]
        o_ref[...] = x + _dot(act.astype(BF16), wd_ref[...])

    return pl.pallas_call(
        body, name="fwd_ffn", grid=(s // tm,),
        in_specs=[_rows(tm, D_MODEL), _whole((1, D_MODEL)),
                  pl.BlockSpec((2 * FFN_H, D_MODEL), lambda i: (0, 0), pipeline_mode=pl.Buffered(1)),
                  pl.BlockSpec((FFN_H, D_MODEL), lambda i: (0, 0), pipeline_mode=pl.Buffered(1)), ORDER_ONLY],
        out_specs=[_rows(tm, 2 * FFN_H), _rows(tm, D_MODEL)],
        out_shape=[jax.ShapeDtypeStruct((s, 2 * FFN_H), F32), jax.ShapeDtypeStruct((s, D_MODEL), F32)],
        compiler_params=_cp("parallel"),
    )(x, g, wgu, wd, tok)


def _loss_head(y, target):
    s = y.shape[0]
    tm = TOKEN_TILE

    def body(y_ref, t_ref, l_ref, d_ref):
        @pl.when(pl.program_id(0) == 0)
        def _():
            l_ref[...] = jnp.zeros_like(l_ref)

        err = y_ref[...] - t_ref[...]
        d_ref[...] = err * (1.0 / D_MODEL)
        l_ref[...] += 0.5 * jnp.sum(jnp.mean(err * err, axis=-1, keepdims=True), axis=0, keepdims=True)

    return pl.pallas_call(
        body, name="loss_head", grid=(s // tm,),
        in_specs=[_rows(tm, D_MODEL), _rows(tm, D_MODEL)],
        out_specs=[_whole((1, 1)), _rows(tm, D_MODEL)],
        out_shape=[jax.ShapeDtypeStruct((1, 1), F32), jax.ShapeDtypeStruct((s, D_MODEL), F32)],
        compiler_params=_cp("arbitrary"),
    )(y, target)


def _accumulate(ref, value):
    @pl.when(pl.program_id(0) == 0)
    def _():
        ref[...] = jnp.zeros_like(ref)

    ref[...] += value


def _bwd_ffn(dxo, gu, xm, g, wgu, wd, tok):
    s = dxo.shape[0]
    tm = TOKEN_TILE

    def body(dxo_ref, gu_ref, xm_ref, g_ref, wgu_ref, wd_ref, tok_ref, dgu_ref, act_ref, h_ref, dxm_ref, dg_ref):
        dxo = dxo_ref[...]
        gu = gu_ref[...]
        gate, up = gu[:, :FFN_H], gu[:, FFN_H:]
        sg = _sigmoid(gate)
        sl = gate * sg
        act_ref[...] = (sl * up).astype(BF16)
        dact = _dot_nt(dxo.astype(BF16), wd_ref[...])
        dgate = dact * up * (sg * (1.0 + gate * (1.0 - sg)))
        dgu = jnp.concatenate([dgate, dact * sl], axis=1).astype(BF16)
        dgu_ref[...] = dgu
        dh = _dot(dgu, wgu_ref[...])
        r, xh = _rms_stats(xm_ref[...])
        h_ref[...] = (xh * g_ref[...]).astype(BF16)
        dx, dgrow = _rms_bwd(xh, r, g_ref[...], dh)
        dxm_ref[...] = dxo + dx
        _accumulate(dg_ref, _colsum(dgrow))

    return pl.pallas_call(
        body, name="bwd_ffn", grid=(s // tm,),
        in_specs=[_rows(tm, D_MODEL), _rows(tm, 2 * FFN_H), _rows(tm, D_MODEL), _whole((1, D_MODEL)),
                  pl.BlockSpec((2 * FFN_H, D_MODEL), lambda i: (0, 0), pipeline_mode=pl.Buffered(1)),
                  pl.BlockSpec((FFN_H, D_MODEL), lambda i: (0, 0), pipeline_mode=pl.Buffered(1)), ORDER_ONLY],
        out_specs=[_rows(tm, 2 * FFN_H), _rows(tm, FFN_H), _rows(tm, D_MODEL), _rows(tm, D_MODEL),
                   _whole((1, D_MODEL))],
        out_shape=[jax.ShapeDtypeStruct((s, 2 * FFN_H), BF16), jax.ShapeDtypeStruct((s, FFN_H), BF16),
                   jax.ShapeDtypeStruct((s, D_MODEL), BF16), jax.ShapeDtypeStruct((s, D_MODEL), F32),
                   jax.ShapeDtypeStruct((1, D_MODEL), F32)],
        compiler_params=_cp("arbitrary"),
    )(dxo, gu, xm, g, wgu, wd, tok)


def _matmul_tn(a, b, tm, tn, out_dtype=BF16):
    s, m = a.shape
    n = b.shape[1]

    def body(a_ref, b_ref, o_ref):
        o_ref[...] = _dot_tn(a_ref[...].astype(BF16), b_ref[...].astype(BF16)).astype(out_dtype)

    return pl.pallas_call(
        body, name="weight_grad", grid=(m // tm, n // tn),
        in_specs=[pl.BlockSpec((s, tm), lambda i, j: (0, i)), pl.BlockSpec((s, tn), lambda i, j: (0, j))],
        out_specs=pl.BlockSpec((tm, tn), lambda i, j: (i, j)),
        out_shape=jax.ShapeDtypeStruct((m, n), out_dtype),
        compiler_params=_cp("parallel", "parallel"),
    )(a, b)


def _bwd_out(dxm, yc, ys, yb, g, w, tok):
    s = dxm.shape[0]
    tm = TOKEN_TILE

    def body(dxm_ref, yc_ref, ys_ref, yb_ref, g_ref, w_ref, tok_ref, dyc_ref, dys_ref, dyb_ref, cat_ref, dg_ref):
        stats = _group_norms(yc_ref[...], ys_ref[...], yb_ref[...])
        g = g_ref[...]
        cat_ref[...] = (jnp.concatenate([xh for _, xh in stats], axis=1) * g).astype(BF16)
        dcat = _dot_nt(dxm_ref[...].astype(BF16), w_ref[...])
        dgs = []
        off = 0
        for (r, xh), out in zip(stats, (dyc_ref, dys_ref, dyb_ref)):
            cols = slice(off, off + xh.shape[1])
            dx, dgrow = _rms_bwd(xh, r, g[:, cols], dcat[:, cols])
            out[...] = dx
            dgs.append(_colsum(dgrow))
            off += xh.shape[1]
        _accumulate(dg_ref, jnp.concatenate(dgs, axis=1))

    return pl.pallas_call(
        body, name="bwd_out", grid=(s // tm,),
        in_specs=[_rows(tm, D_MODEL), _rows(tm, CONV_W), _rows(tm, SG_W), _rows(tm, SB_W),
                  _whole((1, D_MODEL)), _whole((D_MODEL, D_MODEL)), ORDER_ONLY],
        out_specs=[_rows(tm, CONV_W), _rows(tm, SG_W), _rows(tm, SB_W), _rows(tm, D_MODEL),
                   _whole((1, D_MODEL))],
        out_shape=[jax.ShapeDtypeStruct((s, CONV_W), F32), jax.ShapeDtypeStruct((s, SG_W), F32),
                   jax.ShapeDtypeStruct((s, SB_W), F32), jax.ShapeDtypeStruct((s, D_MODEL), BF16),
                   jax.ShapeDtypeStruct((1, D_MODEL), F32)],
        compiler_params=_cp("arbitrary"),
    )(dxm, yc, ys, yb, g, w, tok)


def _bwd_sb(qn, kst, ktr, vtr, dy, ltot, tok):
    s = qn.shape[0]
    np_ = SB_PAIRS

    def body(q_ref, k_ref, kt_ref, vt_ref, do_ref, lt_ref, upto_ref, before_ref, tok_ref, dq_ref, dk_ref, dv_ref,
             z_buf, da_buf, dz_buf, att_buf):
        i = pl.program_id(1)
        first = i * (SB_Q // CHUNK)
        last = first + SB_Q // CHUNK - 1

        @pl.when(i == 0)
        def _():
            dk_ref[...] = jnp.zeros_like(dk_ref)
            dv_ref[...] = jnp.zeros_like(dv_ref)

        lanes = [slice(pr * CHUNK, (pr + 1) * CHUNK) for pr in range(np_)]
        qs = [q_ref[:, lanes[pr]] for pr in range(np_)]
        dos = [do_ref[:, lanes[pr]] for pr in range(np_)]
        dobs = [do.astype(BF16) for do in dos]
        q_ts = [q.astype(F32).T.astype(BF16) for q in qs]
        do_ts = [do.T.astype(BF16) for do in dos]
        ltots = [_per_head(lt_ref[:, pr * CHUNK:pr * CHUNK + 1],
                           lt_ref[:, pr * CHUNK + HEAD_DIM:pr * CHUNK + HEAD_DIM + 1]) for pr in range(np_)]
        upto = upto_ref[...]
        before = before_ref[...]
        last0, last1 = slice(CHUNK - 1, CHUNK), slice(PAIR - 1, PAIR)

        def rows(kb):
            return pl.ds(pl.multiple_of(kb * PAIR, PAIR), PAIR)

        def ahead(pr, kb):
            return _dot(qs[pr], kt_ref[pr, kb]), _dot(dobs[pr], vt_ref[pr, kb])

        def behind(pr, kb, dq):
            dzb = dz_buf[pr]
            dk_ref[pr, kb] += _dot(q_ts[pr], dzb)
            dv_ref[pr, kb] += _dot(do_ts[pr], att_buf[pr])
            return dq + _dot(dzb, k_ref[rows(kb), lanes[pr]])

        def block(kb, carry, masked):
            out = []
            for pr in range(np_):
                dq, p0, p1, e0, e1 = carry[pr]
                z_next, da_next = ahead(pr, jnp.minimum(kb + 1, last))
                dq = behind(pr, jnp.maximum(kb - 1, 0), dq)
                z, sp, nlb, mask = _sb_scores(z_buf[pr], i * SB_Q, kb * CHUNK, masked)
                pin = _dot(nlb, upto) + _per_head(p0, p1)
                sig = jnp.exp(z - sp)
                att = jnp.exp(z - sp - (ltots[pr] - pin))
                if masked:
                    att = jnp.where(mask, att, 0.0)
                e = att * da_buf[pr]
                ebefore = _dot(e.astype(BF16), before) + _per_head(e0, e1)
                dz = e - sig * (e + ebefore)
                if masked:
                    dz = jnp.where(mask, dz, 0.0)
                z_buf[pr] = z_next
                da_buf[pr] = da_next
                dz_buf[pr] = dz.astype(BF16)
                att_buf[pr] = att.astype(BF16)
                out.append((dq, pin[:, last0], pin[:, last1],
                            ebefore[:, last0] + e[:, last0], ebefore[:, last1] + e[:, last1]))
            return tuple(out)

        for pr in range(np_):
            z_buf[pr], da_buf[pr] = ahead(pr, 0)
        dz_buf[...] = jnp.zeros_like(dz_buf)
        att_buf[...] = jnp.zeros_like(att_buf)
        zero = jnp.zeros((SB_Q, 1), F32)
        carry = ((jnp.zeros((SB_Q, CHUNK), F32), zero, zero, zero, zero),) * np_
        carry = lax.fori_loop(0, first, lambda kb, c: block(kb, c, False), carry)
        for ahead_of in range(SB_Q // CHUNK):
            carry = block(first + ahead_of, carry, True)
        for pr in range(np_):
            dq_ref[:, lanes[pr]] = behind(pr, last, carry[pr][0])

    blk = pl.BlockSpec((SB_Q, np_ * CHUNK), lambda p, i: (i, p))
    seq = pl.BlockSpec((2 * s, np_ * CHUNK), lambda p, i: (0, p))
    tiles = pl.BlockSpec((np_, s // CHUNK, CHUNK, PAIR), lambda p, i: (p, 0, 0, 0))
    tri = pl.BlockSpec((PAIR, PAIR), lambda p, i: (0, 0))
    return pl.pallas_call(
        body, name="bwd_sb", grid=(SB_W // CHUNK // np_, s // SB_Q),
        in_specs=[blk, seq, tiles, tiles, blk, blk, tri, tri, ORDER_ONLY],
        out_specs=[blk, tiles, tiles],
        out_shape=[jax.ShapeDtypeStruct((s, SB_W), F32)]
        + [jax.ShapeDtypeStruct((SB_W // CHUNK, s // CHUNK, CHUNK, PAIR), F32)] * 2,
        scratch_shapes=[pltpu.VMEM((np_, SB_Q, PAIR), F32), pltpu.VMEM((np_, SB_Q, PAIR), F32),
                        pltpu.VMEM((np_, SB_Q, PAIR), BF16), pltpu.VMEM((np_, SB_Q, PAIR), BF16)],
        compiler_params=_cp("parallel", "arbitrary"),
    )(qn, kst, ktr, vtr, dy, ltot, _pair_tri("upto"), _pair_tri("before"), tok)


def _head_sum(row):
    acc = row[:, 0:HEAD_DIM]
    for h in range(1, SB_W // HEAD_DIM):
        acc = acc + row[:, h * HEAD_DIM:(h + 1) * HEAD_DIM]
    return acc


def _bwd_qk(proj, dqs, dkn, dv, qg, kg):
    s = proj.shape[0]
    tm = TOKEN_TILE
    tiles = pl.BlockSpec((SB_W // CHUNK, tm // CHUNK, CHUNK, PAIR), lambda i: (0, i, 0, 0))

    def body(q_ref, k_ref, dqs_ref, dkn_ref, dv_ref, qg_ref, kg_ref, dp_ref, dqg_ref, dkg_ref, qacc, kacc):
        i = pl.program_id(0)
        gm = _group_mean_matrix(SB_W, HEAD_DIM)

        def one(x, dy, g, acc):
            r = lax.rsqrt(_dot_split(x * x, gm) + RMS_EPS)
            xh = x * r
            dxh = dy * g
            _accumulate(acc, _colsum(dy * xh))
            return r * (dxh - xh * _dot_split(dxh * xh, gm))

        dq = one(q_ref[...], dqs_ref[...] * (HEAD_DIM ** -0.5), qg_ref[...], qacc)
        dk = one(k_ref[...], _unstack_heads(_load_transposed(dkn_ref)), kg_ref[...], kacc)
        dp_ref[...] = jnp.concatenate([dq, dk, _unstack_heads(_load_transposed(dv_ref))], axis=1).astype(BF16)

        @pl.when(i == pl.num_programs(0) - 1)
        def _():
            dqg_ref[...] = _head_sum(qacc[...])
            dkg_ref[...] = _head_sum(kacc[...])

    return pl.pallas_call(
        body, name="bwd_qk", grid=(s // tm,),
        in_specs=[_rows(tm, SB_W, OFF_SB // SB_W), _rows(tm, SB_W, OFF_SB // SB_W + 1),
                  _rows(tm, SB_W), tiles, tiles, _whole((1, SB_W)), _whole((1, SB_W))],
        out_specs=[_rows(tm, 3 * SB_W), _whole((1, HEAD_DIM)), _whole((1, HEAD_DIM))],
        out_shape=[jax.ShapeDtypeStruct((s, 3 * SB_W), BF16), jax.ShapeDtypeStruct((1, HEAD_DIM), F32),
                   jax.ShapeDtypeStruct((1, HEAD_DIM), F32)],
        scratch_shapes=[pltpu.VMEM((1, SB_W), F32), pltpu.VMEM((1, SB_W), F32)],
        compiler_params=_cp("arbitrary"),
    )(proj, proj, dqs, dkn, dv, qg, kg)


def _bwd_sg(proj, dy, lg, lb, w, bias):
    s = proj.shape[0]
    tm = TOKEN_TILE
    nh = SG_W // HEAD_DIM

    def body(p_ref, dy_ref, lg_ref, lb_ref, w_ref, bias_ref, dp_ref, dlg_ref, dlb_ref, dw_ref, db_ref, dbias):
        i = pl.program_id(0)
        uv = p_ref[...]
        ge = _gelu(uv)
        u = ge[:, :SG_W]
        r, xh = _ln_stats(ge[:, SG_W:])
        vln = (xh * lg_ref[...] + lb_ref[...]).astype(BF16)
        dy = dy_ref[...]
        tril, lane_head = _sg_masks()

        @pl.when(i == 0)
        def _():
            dw_ref[...] = jnp.zeros_like(dw_ref)
            dbias[...] = jnp.zeros_like(dbias)

        dus, dvlns = [], []
        for c in range(tm // CHUNK):
            rows = slice(c * CHUNK, (c + 1) * CHUNK)
            vc = vln[rows]
            dus.append(dy[rows] * _sg_mix(w_ref, bias_ref, vc, tril, lane_head))
            dm = dy[rows] * u[rows]
            dbias[...] += dm
            dvc = jnp.zeros((CHUNK, SG_W), F32)
            for h in range(nh):
                dmh = jnp.where(lane_head == h, dm, 0.0).astype(BF16)
                dw_ref[h] += jnp.where(tril, _dot_nt(dmh, vc), 0.0)
                wm = jnp.where(tril, w_ref[h], 0.0).astype(BF16)
                dvc = dvc + _dot_tn(wm, dmh)
            dvlns.append(dvc)
        du = jnp.concatenate(dus, axis=0)
        dvln = jnp.concatenate(dvlns, axis=0)
        _accumulate(dlg_ref, _colsum(dvln * xh))
        _accumulate(dlb_ref, _colsum(dvln))
        dv = _ln_bwd(xh, r, lg_ref[...], dvln)
        dp_ref[...] = (jnp.concatenate([du, dv], axis=1) * _gelu_grad(uv)).astype(BF16)

        @pl.when(i == pl.num_programs(0) - 1)
        def _():
            lane = lax.broadcasted_iota(jnp.int32, (CHUNK, CHUNK), 1)
            acc = dbias[...]
            out = jnp.zeros((CHUNK, CHUNK), F32)
            for h in range(nh):
                hs = jnp.sum(acc[:, h * HEAD_DIM:(h + 1) * HEAD_DIM], axis=1, keepdims=True)
                out = out + jnp.where(lane == h, hs, 0.0)
            db_ref[...] = out

    return pl.pallas_call(
        body, name="bwd_sg", grid=(s // tm,),
        in_specs=[_rows(tm, 2 * SG_W, 1), _rows(tm, SG_W), _whole((1, SG_W)), _whole((1, SG_W)),
                  _whole((nh, CHUNK, CHUNK)), _whole((CHUNK, SG_W))],
        out_specs=[_rows(tm, 2 * SG_W), _whole((1, SG_W)), _whole((1, SG_W)), _whole((nh, CHUNK, CHUNK)),
                   _whole((CHUNK, CHUNK))],
        out_shape=[jax.ShapeDtypeStruct((s, 2 * SG_W), BF16), jax.ShapeDtypeStruct((1, SG_W), F32),
                   jax.ShapeDtypeStruct((1, SG_W), F32), jax.ShapeDtypeStruct((nh, CHUNK, CHUNK), F32),
                   jax.ShapeDtypeStruct((CHUNK, CHUNK), F32)],
        scratch_shapes=[pltpu.VMEM((CHUNK, SG_W), F32)],
        compiler_params=_cp("arbitrary"),
    )(proj, dy, lg, lb, w, bias)


def _bwd_conv(proj, dy, w, b, lg, lb):
    s = proj.shape[0]

    def body(p_ref, dy_ref, w_ref, b_ref, lg_ref, lb_ref, dp_ref, dw_ref, db_ref, dlg_ref, dlb_ref,
             abuf, dcbuf):
        _glu_fill(p_ref, abuf, s)
        dcbuf[pl.ds(s, CHUNK), :] = jnp.zeros((CHUNK, CONV_W), F32)
        dw_ref[...] = jnp.zeros_like(dw_ref)

        def chunk(c, carry):
            db, dlg, dlb = carry
            r0 = pl.multiple_of(c * CHUNK, CHUNK)
            acc, shifted = _conv_window(abuf, r0, w_ref)
            r, xh = _ln_stats(acc + b_ref[...])
            ln = xh * lg_ref[...] + lb_ref[...]
            sg = _sigmoid(ln)
            dl = dy_ref[pl.ds(r0, CHUNK), :] * (sg * (1.0 + ln * (1.0 - sg)))
            dc = _ln_bwd(xh, r, lg_ref[...], dl)
            dcbuf[pl.ds(r0, CHUNK), :] = dc
            for k in range(CONV_K):
                dw_ref[CONV_K - 1 - k:CONV_K - k, :] += _colsum(dc * shifted(k))
            return db + _colsum(dc), dlg + _colsum(dl * xh), dlb + _colsum(dl)

        zero = jnp.zeros((1, CONV_W), F32)
        db, dlg, dlb = lax.fori_loop(0, s // CHUNK, chunk, (zero, zero, zero))
        db_ref[...] = db
        dlg_ref[...] = dlg
        dlb_ref[...] = dlb

        def chunk_back(c, carry):
            r0 = pl.multiple_of(c * CHUNK, CHUNK)
            shifted = _shifted(dcbuf[pl.ds(r0, CHUNK + 32), :], back=False)
            da = jnp.zeros((CHUNK, CONV_W), F32)
            for k in range(CONV_K):
                da = da + shifted(k) * w_ref[CONV_K - 1 - k:CONV_K - k, :]
            pv = p_ref[pl.ds(r0, CHUNK), :]
            val, sg = pv[:, :CONV_W], _sigmoid(pv[:, CONV_W:])
            dp_ref[pl.ds(r0, CHUNK), :] = jnp.concatenate([da * sg, da * val * sg * (1.0 - sg)], axis=1).astype(BF16)
            return carry

        lax.fori_loop(0, s // CHUNK, chunk_back, 0)

    row = _whole((1, CONV_W))
    return pl.pallas_call(
        body, name="bwd_conv", grid=(1,),
        in_specs=[pl.BlockSpec((s, 2 * CONV_W), lambda i: (0, 0)), _whole((s, CONV_W)),
                  _whole((CONV_K, CONV_W)), row, row, row],
        out_specs=[_whole((s, 2 * CONV_W)), _whole((CONV_K, CONV_W)), row, row, row],
        out_shape=[jax.ShapeDtypeStruct((s, 2 * CONV_W), BF16), jax.ShapeDtypeStruct((CONV_K, CONV_W), F32)]
        + [jax.ShapeDtypeStruct((1, CONV_W), F32)] * 3,
        scratch_shapes=[pltpu.VMEM((s + CHUNK, CONV_W), F32), pltpu.VMEM((s + CHUNK, CONV_W), F32)],
        compiler_params=_cp("arbitrary"),
    )(proj, dy, w, b, lg, lb)


def _bwd_in(dpc, dps, dpb, x, g, w, dxm):
    s = x.shape[0]
    tm = TOKEN_TILE

    def body(dpc_ref, dps_ref, dpb_ref, x_ref, g_ref, w_ref, dxm_ref, dx_ref, h_ref, dp_ref, dg_ref):
        dp = jnp.concatenate([dpc_ref[...], dps_ref[...], dpb_ref[...]], axis=1)
        dp_ref[...] = dp
        dh = _dot(dp, w_ref[...])
        r, xh = _rms_stats(x_ref[...])
        h_ref[...] = (xh * g_ref[...]).astype(BF16)
        dx, dgrow = _rms_bwd(xh, r, g_ref[...], dh)
        dx_ref[...] = dxm_ref[...] + dx
        _accumulate(dg_ref, _colsum(dgrow))

    return pl.pallas_call(
        body, name="bwd_in", grid=(s // tm,),
        in_specs=[_rows(tm, 2 * CONV_W), _rows(tm, 2 * SG_W), _rows(tm, 3 * SB_W), _rows(tm, D_MODEL),
                  _whole((1, D_MODEL)), _whole((IN_W, D_MODEL)), _rows(tm, D_MODEL)],
        out_specs=[_rows(tm, D_MODEL), _rows(tm, D_MODEL), _rows(tm, IN_W), _whole((1, D_MODEL))],
        out_shape=[jax.ShapeDtypeStruct((s, D_MODEL), F32), jax.ShapeDtypeStruct((s, D_MODEL), BF16),
                   jax.ShapeDtypeStruct((s, IN_W), BF16), jax.ShapeDtypeStruct((1, D_MODEL), F32)],
        compiler_params=_cp("arbitrary"),
    )(dpc, dps, dpb, x, g, w, dxm)


SMALL = ("mix_norm_g", "conv_w", "conv_b", "conv_ln_g", "conv_ln_b", "sg_ln_g", "sg_ln_b", "sg_w", "sg_b",
         "q_norm_g", "k_norm_g", "out_norm_g", "ffn_norm_g")
LARGE = ("w_in", "w_out", "w_gate_up", "w_down")


def _row(v):
    return v.reshape(1, -1)


def _layer_params(p, large, l):
    q = {k: v[l] for k, v in p.items()}
    return dict(
        q, **large,
        mix_norm_g=_row(q["mix_norm_g"]), conv_b=_row(q["conv_b"]), conv_ln_g=_row(q["conv_ln_g"]),
        conv_ln_b=_row(q["conv_ln_b"]), sg_ln_g=_row(q["sg_ln_g"]), sg_ln_b=_row(q["sg_ln_b"]),
        out_norm_g=_row(q["out_norm_g"]), ffn_norm_g=_row(q["ffn_norm_g"]),
        qg=_row(jnp.tile(q["q_norm_g"], SB_W // HEAD_DIM)), kg=_row(jnp.tile(q["k_norm_g"], SB_W // HEAD_DIM)),
        sg_bias=jnp.repeat(q["sg_b"].T, HEAD_DIM, axis=1),
    )


def _layer_fwd(x, q, tok, after_in, after_mixers, after_out):
    proj, qn, kn, vb, kt, vt = _fwd_in(x, q["mix_norm_g"], q["w_in"], q["qg"], q["kg"], tok)
    yc = _fwd_conv(proj, q["conv_w"], q["conv_b"], q["conv_ln_g"], q["conv_ln_b"], after_in(proj))
    ys = _fwd_sg(proj, q["sg_ln_g"], q["sg_ln_b"], q["sg_w"], q["sg_bias"])
    yb, lt = _fwd_sb(qn, kt, vb)
    rest, tok = after_mixers(yb)
    q = dict(q, **rest)
    xm = _fwd_out(yc, ys, yb, q["out_norm_g"], q["w_out"], x, tok)
    gu, xo = _fwd_ffn(xm, q["ffn_norm_g"], q["w_gate_up"], q["w_down"], after_out(xm))
    return xo, q, dict(x=x, proj=proj, qn=qn, kn=kn, kt=kt, vt=vt, lt=lt, yc=yc, ys=ys, yb=yb, xm=xm, gu=gu)


def _layer_bwd_ffn(dxo, q, st, tok):
    dgu, act, h2, dxm, d_ffn_g = _bwd_ffn(dxo, st["gu"], st["xm"], q["ffn_norm_g"], q["w_gate_up"], q["w_down"],
                                          tok)
    d_wgu, d_wd = _matmul_tn(dgu, h2, 512, D_MODEL), _matmul_tn(act, dxo, FFN_H // 2, D_MODEL)
    dyc, dys, dyb, cat, d_out_g = _bwd_out(dxm, st["yc"], st["ys"], st["yb"], q["out_norm_g"], q["w_out"], tok)
    return dxm, (dyc, dys, dyb, d_ffn_g, d_out_g), d_wgu, d_wd, _matmul_tn(cat, dxm, 512, D_MODEL)


def _layer_bwd_mix(dxm, carried, q, st, tok):
    dyc, dys, dyb, d_ffn_g, d_out_g = carried
    dqs, dkn, dv = _bwd_sb(st["qn"], st["kn"], st["kt"], st["vt"], dyb, st["lt"], tok)
    dpb, d_qg, d_kg = _bwd_qk(st["proj"], dqs, dkn, dv, q["qg"], q["kg"])
    dps, d_sg_lg, d_sg_lb, d_sg_w, d_sg_b = _bwd_sg(st["proj"], dys, q["sg_ln_g"], q["sg_ln_b"], q["sg_w"],
                                                    q["sg_bias"])
    dpc, d_conv_w, d_conv_b, d_conv_lg, d_conv_lb = _bwd_conv(st["proj"], dyc, q["conv_w"], q["conv_b"],
                                                              q["conv_ln_g"], q["conv_ln_b"])
    dx, h1, dp, d_mix_g = _bwd_in(dpc, dps, dpb, st["x"], q["mix_norm_g"], q["w_in"], dxm)
    d_win = _matmul_tn(dp, h1, 512, D_MODEL)
    small = dict(
        mix_norm_g=d_mix_g[0], conv_w=d_conv_w, conv_b=d_conv_b[0], conv_ln_g=d_conv_lg[0],
        conv_ln_b=d_conv_lb[0], sg_ln_g=d_sg_lg[0], sg_ln_b=d_sg_lb[0], sg_w=d_sg_w,
        sg_b=d_sg_b[:, :SG_W // HEAD_DIM].T, q_norm_g=d_qg[0], k_norm_g=d_kg[0], out_norm_g=d_out_g[0],
        ffn_norm_g=d_ffn_g[0])
    return dx, d_win, small


def _position():
    x, y, c = lax.axis_index("x"), lax.axis_index("y"), lax.axis_index("c")
    return x, y, c


def _flat(px, py, pc):
    return 4 * px + 2 * py + pc


IN_HBM = pl.BlockSpec(memory_space=pltpu.HBM)
IN_SEM = pl.BlockSpec(memory_space=pltpu.SEMAPHORE)
EFFECT = pltpu.SideEffectType.DATAFLOW_SIDE_EFFECTING
COPIES = dict(scatter=7, spread=7, spread_chips=4, **{"pass": 3})


def _exchange_copies(kinds, src_refs, land_refs, send_sems, recv_sems, layer, arrival):
    x, y, c = _position()
    me = _flat(x, y, c)
    everyone = [(x ^ (k >> 2 & 1), y ^ (k >> 1 & 1), c ^ (k & 1)) for k in range(1, N_DEV)]
    sibling = (x, y, 1 - c)
    chips = [(1 - x, y, c), (x, 1 - y, c), (1 - x, 1 - y, c)]
    out = []
    srcs = iter(src_refs)
    for kind, land in zip(kinds, land_refs):
        land = land if layer is None else land.at[layer]
        if kind == "scatter":
            src = next(srcs)
            moves = [(src.at[_flat(*p)], me, _flat(*p), p) for p in everyone]
        elif kind in ("spread", "spread_chips"):
            src = next(srcs)
            moves = [(src, me, _flat(*p), p) for p in (everyone if kind == "spread" else [sibling] + chips)]
        else:
            moves = [(land.at[_flat(*p)], _flat(*p), _flat(p[0], p[1], 1 - c), sibling) for p in chips]
        for src_block, there, here, peer in moves:
            n = len(out)
            out.append(pltpu.make_async_remote_copy(
                src_ref=src_block, dst_ref=land.at[here if arrival else there], send_sem=send_sems.at[n],
                recv_sem=recv_sems.at[n], device_id=peer, device_id_type=MESH))
    return out


def _exchange_start(kinds, srcs, lands, after, name, layer=None):
    ns, n = len(srcs), len(srcs) + len(lands)
    sems = sum(COPIES[k] for k in kinds)

    def body(*refs):
        send_sems, recv_sems = refs[n + 1], refs[n + 2]
        for cp in _exchange_copies(kinds, refs[:ns], refs[ns:n], send_sems, recv_sems, layer, arrival=False):
            cp.start()
        refs[-1][...] = jnp.zeros_like(refs[-1])

    thru = [pltpu.HBM(a.shape, a.dtype) for a in (*srcs, *lands)]
    outs = pl.pallas_call(
        body, name=name,
        out_shape=(pltpu.SemaphoreType.DMA((sems,)), pltpu.SemaphoreType.DMA((sems,)), *thru,
                   jax.ShapeDtypeStruct((8, 128), F32)),
        in_specs=[IN_HBM] * n + [ORDER_ONLY],
        out_specs=(IN_SEM, IN_SEM, *[IN_HBM] * n, pl.BlockSpec(memory_space=pltpu.VMEM)),
        input_output_aliases={i: 2 + i for i in range(n)},
        compiler_params=pltpu.CompilerParams(has_side_effects=EFFECT),
    )(*[pltpu.with_memory_space_constraint(a, pltpu.HBM) for a in (*srcs, *lands)], after)
    return kinds, outs[0], outs[1], list(outs[2:2 + ns]), list(outs[2 + ns:2 + n]), outs[-1]


def _exchange_wait(pending, after, name, layer=None):
    kinds, send_sems, recv_sems, srcs, lands, _ = pending
    after = list(after) if isinstance(after, (list, tuple)) else [after]
    ns, n = len(srcs), len(srcs) + len(lands)

    def body(*refs):
        for cp in _exchange_copies(kinds, refs[:ns], refs[ns:n], refs[n], refs[n + 1], layer, arrival=True):
            cp.wait_send()
            cp.wait_recv()

    thru = [pltpu.HBM(a.shape, a.dtype) for a in (*srcs, *lands)]
    outs = pl.pallas_call(
        body, name=name, out_shape=tuple(thru),
        in_specs=[IN_HBM] * n + [IN_SEM, IN_SEM] + [ORDER_ONLY] * len(after),
        out_specs=tuple([IN_HBM] * n),
        input_output_aliases={i: i for i in range(n)},
        compiler_params=pltpu.CompilerParams(has_side_effects=EFFECT),
    )(*srcs, *lands, send_sems, recv_sems, *after)
    return list(outs[ns:])


def _landing(block, me):
    land = lax.empty((N_DEV,) + block.shape, block.dtype)
    return lax.dynamic_update_index_in_dim(land, block, me, 0)


def _adamw(parts, w, m, v, tr):
    groups, rows, cols = w.shape

    def body(p_ref, w_ref, m_ref, v_ref, g_ref, d_ref, nm_ref, nv_ref):
        g = p_ref[0].astype(F32)
        for j in range(1, N_DEV):
            g = g + p_ref[j].astype(F32)
        g_ref[...] = g
        m = ADAM_B1 * m_ref[...] + (1.0 - ADAM_B1) * g
        v = ADAM_B2 * v_ref[...] + (1.0 - ADAM_B2) * (g * g)
        nm_ref[...] = m
        nv_ref[...] = v
        m_hat = m / (1.0 - ADAM_B1 ** ADAM_STEP)
        v_hat = v / (1.0 - ADAM_B2 ** ADAM_STEP)
        d_ref[...] = -ADAM_LR * (m_hat / (jnp.sqrt(v_hat) + ADAM_EPS) + ADAM_WD * w_ref[...])

    blk = pl.BlockSpec((None, tr, cols), lambda g, i: (g, i, 0))
    return pl.pallas_call(
        body, name="adamw", grid=(groups, rows // tr),
        in_specs=[pl.BlockSpec((None, N_DEV, tr, cols), lambda g, i: (g, 0, i, 0)), blk, blk, blk],
        out_specs=[blk] * 4,
        out_shape=[jax.ShapeDtypeStruct((groups, rows, cols), F32)] * 4,
        compiler_params=_cp("parallel", "parallel"),
    )(parts, w, m, v)


def _row_tile(rows):
    for cand in range(min(rows, 512) // 8 * 8, 7, -8):
        if rows % cand == 0:
            return cand
    return rows


def _with_own_block(land, blocks, layer, me):
    own = lax.dynamic_index_in_dim(blocks, me, 0, keepdims=True)[None]
    return lax.dynamic_update_slice(land, own, (layer, me, 0, 0))


PACK_LANES = 128


def _pack_layers(arrs):
    parts = []
    for a in arrs:
        flat = a.reshape(a.shape[0], -1)
        parts.append(jnp.pad(flat, ((0, 0), (0, -flat.shape[1] % (8 * PACK_LANES)))))
    return jnp.concatenate(parts, axis=1).reshape(arrs[0].shape[0], -1, PACK_LANES)


def _unpack_layers(packed, shapes):
    flat = packed.reshape(packed.shape[0], -1)
    outs, off = [], 0
    for shp in shapes:
        size = 1
        for d in shp[1:]:
            size *= d
        outs.append(flat[:, off:off + size].reshape(shp))
        off += size + (-size % (8 * PACK_LANES))
    return outs


def kernel(x, mix_norm_g, w_in, conv_w, conv_b, conv_ln_g, conv_ln_b, sg_ln_g, sg_ln_b, sg_w, sg_b, q_norm_g, k_norm_g, out_norm_g, w_out, ffn_norm_g, w_gate_up, w_down, loss_target, m_mix_norm_g, m_w_in, m_conv_w, m_conv_b, m_conv_ln_g, m_conv_ln_b, m_sg_ln_g, m_sg_ln_b, m_sg_w, m_sg_b, m_q_norm_g, m_k_norm_g, m_out_norm_g, m_w_out, m_ffn_norm_g, m_w_gate_up, m_w_down, v_mix_norm_g, v_w_in, v_conv_w, v_conv_b, v_conv_ln_g, v_conv_ln_b, v_sg_ln_g, v_sg_ln_b, v_sg_w, v_sg_b, v_q_norm_g, v_k_norm_g, v_out_norm_g, v_w_out, v_ffn_norm_g, v_w_gate_up, v_w_down):
    names = SMALL[:1] + LARGE[:1] + SMALL[1:12] + LARGE[1:2] + SMALL[12:] + LARGE[2:]
    w = dict(mix_norm_g=mix_norm_g, w_in=w_in, conv_w=conv_w, conv_b=conv_b, conv_ln_g=conv_ln_g,
             conv_ln_b=conv_ln_b, sg_ln_g=sg_ln_g, sg_ln_b=sg_ln_b, sg_w=sg_w, sg_b=sg_b, q_norm_g=q_norm_g,
             k_norm_g=k_norm_g, out_norm_g=out_norm_g, w_out=w_out, ffn_norm_g=ffn_norm_g,
             w_gate_up=w_gate_up, w_down=w_down)
    m = dict(mix_norm_g=m_mix_norm_g, w_in=m_w_in, conv_w=m_conv_w, conv_b=m_conv_b, conv_ln_g=m_conv_ln_g,
             conv_ln_b=m_conv_ln_b, sg_ln_g=m_sg_ln_g, sg_ln_b=m_sg_ln_b, sg_w=m_sg_w, sg_b=m_sg_b,
             q_norm_g=m_q_norm_g, k_norm_g=m_k_norm_g, out_norm_g=m_out_norm_g, w_out=m_w_out,
             ffn_norm_g=m_ffn_norm_g, w_gate_up=m_w_gate_up, w_down=m_w_down)
    v = dict(mix_norm_g=v_mix_norm_g, w_in=v_w_in, conv_w=v_conv_w, conv_b=v_conv_b, conv_ln_g=v_conv_ln_g,
             conv_ln_b=v_conv_ln_b, sg_ln_g=v_sg_ln_g, sg_ln_b=v_sg_ln_b, sg_w=v_sg_w, sg_b=v_sg_b,
             q_norm_g=v_q_norm_g, k_norm_g=v_k_norm_g, out_norm_g=v_out_norm_g, w_out=v_w_out,
             ffn_norm_g=v_ffn_norm_g, w_gate_up=v_w_gate_up, w_down=v_w_down)
    xpos, ypos, cpos = _position()
    me = _flat(xpos, ypos, cpos)
    conv_cols = conv_w.shape[-1]
    no_token = jnp.zeros((8, 128), F32)
    w, m, v = (dict(t, w_in=jnp.swapaxes(t["w_in"], 1, 2), w_gate_up=jnp.swapaxes(t["w_gate_up"], 1, 2))
               for t in (w, m, v))
    shards = {k: w[k].astype(BF16) for k in LARGE}
    full_shape = dict(w_in=(IN_W, D_MODEL), w_out=(D_MODEL, D_MODEL), w_gate_up=(2 * FFN_H, D_MODEL),
                      w_down=(FFN_H, D_MODEL))

    def gather_start(srcs, after, tag):
        return _exchange_start(["spread_chips"] * len(srcs), srcs, [_landing(a, me) for a in srcs], after,
                               f"gather_start_{tag}")

    def gather_pass(pending, after, tag):
        lands = _exchange_wait(pending, after, f"gather_wait_{tag}")
        return _exchange_start(["pass"] * len(lands), [], lands, after, f"gather_pass_{tag}")

    def gathered(pending, keys, after, tag):
        lands = _exchange_wait(pending, after, f"gather_passed_{tag}")
        return {k: a.reshape(full_shape[k]) for k, a in zip(keys, lands)}, lands[len(keys):]

    first, later = ("w_in",), ("w_out", "w_gate_up", "w_down")
    act = x[0]
    head = gather_start([shards["w_in"][0], w["conv_w"]], act, "0")
    tail = gather_start([shards[k][0] for k in later], head[5], "0_later")
    head = gather_pass(head, tail[5], "0")
    large, (conv_blocks,) = gathered(head, first, head[5], "0")
    conv_full = jnp.transpose(conv_blocks, (1, 2, 0, 3)).reshape(DEPTH, CONV_K, CONV_W)
    small_w = dict({k: w[k] for k in SMALL}, conv_w=conv_full)
    qs, stash = [], []
    for l in range(DEPTH):
        coming = {}
        more = l + 1 < DEPTH

        def next_start(after):
            coming["first"] = gather_start([shards[k][l + 1] for k in LARGE], after, str(l + 1))
            return coming["first"][5]

        def after_in(proj):
            return next_start(proj) if more and l == 0 else no_token

        def after_mixers(y_sb):
            if l > 0:
                return {}, no_token
            passing = gather_pass(tail, y_sb, "0_later")
            return gathered(passing, later, passing[5], "0_later")[0], passing[5]

        def after_out(x_mid):
            if not more:
                return no_token
            coming["second"] = gather_pass(coming["first"], x_mid, str(l + 1))
            return coming["second"][5]

        token = next_start(act) if more and l > 0 else no_token
        act, q, st = _layer_fwd(act, _layer_params(small_w, large, l), token, after_in, after_mixers, after_out)
        qs.append(q)
        stash.append(st)
        if more:
            large, _ = gathered(coming["second"], LARGE, act, str(l + 1))

    loss, dx = _loss_head(act, loss_target[0])
    loss = lax.psum(loss[0, 0], ("x", "y", "c"))

    replicated = tuple(k for k in SMALL if k != "conv_w")
    small_rows = _pack_layers([w[k][:1] for k in replicated]).shape[1]
    conv_rows = _pack_layers([conv_full[:1]]).shape[1]
    group_a, group_b = ("w_gate_up", "w_down", "w_out"), ("w_in",)
    blocks = lambda k, a: a.reshape((N_DEV,) + w[k].shape[1:])
    land_a = [lax.empty((DEPTH, N_DEV) + w[k].shape[1:], BF16) for k in group_a]
    land_b = [lax.empty((DEPTH, N_DEV) + w[k].shape[1:], BF16) for k in group_b]
    land_b.append(lax.empty((DEPTH, N_DEV, small_rows + conv_rows, PACK_LANES), F32))
    pend_a = pend_b = None
    token = no_token
    for l in reversed(range(DEPTH)):
        dxm, carried, d_wgu, d_wd, d_wo = _layer_bwd_ffn(dx, qs[l], stash[l], token)
        srcs = [blocks(k, a) for k, a in zip(group_a, (d_wgu, d_wd, d_wo))]
        if pend_a is not None:
            land_a = _exchange_wait(pend_a, d_wo, f"grads_a_wait_{l + 1}", layer=l + 1)
        land_a = [_with_own_block(ld, a, l, me) for ld, a in zip(land_a, srcs)]
        pend_a = _exchange_start(["scatter"] * 3, srcs, land_a, dxm, f"grads_a_start_{l}", layer=l)
        dx, d_win, small = _layer_bwd_mix(dxm, carried, qs[l], stash[l], pend_a[5])
        packed = _pack_layers([small[k][None] for k in replicated + ("conv_w",)])[0]
        srcs = [blocks("w_in", d_win)]
        if pend_b is not None:
            land_b = _exchange_wait(pend_b, d_win, f"grads_b_wait_{l + 1}", layer=l + 1)
        land_b = [_with_own_block(land_b[0], srcs[0], l, me),
                  lax.dynamic_update_slice(land_b[1], packed[None, None], (l, me, 0, 0))]
        pend_b = _exchange_start(["scatter", "spread"], srcs + [packed], land_b, dx, f"grads_b_start_{l}",
                                 layer=l)
        token = pend_b[5]

    land_a = _exchange_wait(pend_a, token, "grads_a_wait_0", layer=0)
    res = {}
    for k, parts in zip(group_a, land_a):
        res[k] = _adamw(parts, w[k], m[k], v[k], _row_tile(w[k].shape[1]))
    land_b = _exchange_wait(pend_b, [res[k][0] for k in group_a], "grads_b_wait_0", layer=0)
    res["w_in"] = _adamw(land_b[0], w["w_in"], m["w_in"], v["w_in"], _row_tile(w["w_in"].shape[1]))
    small_parts = land_b[1]
    updated = _adamw(small_parts, *(_pack_layers([t[k] for k in replicated]) for t in (w, m, v)), small_rows)
    unpacked = [_unpack_layers(o, [w[k].shape for k in replicated]) for o in updated]
    res.update({k: [u[i] for u in unpacked] for i, k in enumerate(replicated)})
    conv_parts = small_parts[:, :, small_rows:].reshape(DEPTH, N_DEV, -1)[:, :, :CONV_K * CONV_W]
    conv_parts = lax.dynamic_slice_in_dim(conv_parts.reshape(DEPTH, N_DEV, CONV_K, CONV_W), me * conv_cols,
                                          conv_cols, axis=3)
    res["conv_w"] = _adamw(conv_parts, w["conv_w"], m["conv_w"], v["conv_w"], CONV_K)
    for k in ("w_in", "w_gate_up"):
        res[k] = [jnp.swapaxes(a, 1, 2) for a in res[k]]

    return (loss, dx[None], *[res[k][0] for k in names], *[res[k][1] for k in names],
            *[res[k][2] for k in names], *[res[k][3] for k in names])
```

```python
import jax
import jax.numpy as jnp
from jax import lax
from jax.experimental import pallas as pl
from jax.experimental.pallas import tpu as pltpu

F32 = jnp.float32
BF16 = jnp.bfloat16

D_MODEL = 1024
DEPTH = 4
HEAD_DIM = 64
CONV_W = 256
SG_W = 256
SB_W = 512
IN_W = 2560
FFN_H = 2816
CONV_K = 31
CHUNK = 128
OFF_SG = 2 * CONV_W
OFF_SB = OFF_SG + 2 * SG_W
RMS_EPS = 1e-6
LN_EPS = 1e-5
N_DEV = 8
MESH = pl.DeviceIdType.MESH

ADAM_LR = 0.001
ADAM_B1 = 0.9
ADAM_B2 = 0.999
ADAM_EPS = 1e-08
ADAM_WD = 0.01
ADAM_STEP = 10

TOKEN_TILE = 256
VMEM_LIMIT = 56 * 1024 * 1024


def _cp(*sem):
    return pltpu.CompilerParams(dimension_semantics=sem or None, vmem_limit_bytes=VMEM_LIMIT)


def _dot(a, b):
    return jnp.dot(a, b, preferred_element_type=F32)


def _dot_nt(a, b):
    return lax.dot_general(a, b, (((1,), (1,)), ((), ())), preferred_element_type=F32)


def _dot_tn(a, b):
    return lax.dot_general(a, b, (((0,), (0,)), ((), ())), preferred_element_type=F32)


def _dot_split(x, m):
    hi = x.astype(BF16)
    lo = (x - hi.astype(F32)).astype(BF16)
    return _dot(hi, m) + _dot(lo, m)


def _group_mean_matrix(width, group):
    r = lax.broadcasted_iota(jnp.int32, (width, width), 0) // group
    c = lax.broadcasted_iota(jnp.int32, (width, width), 1) // group
    return jnp.where(r == c, 1.0 / group, 0.0).astype(BF16)


def _sigmoid(x):
    return 1.0 / (1.0 + jnp.exp(-x))


def _gelu(x):
    return 0.5 * x * (1.0 + lax.erf(x * (2.0 ** -0.5)))


def _gelu_grad(x):
    return 0.5 * (1.0 + lax.erf(x * (2.0 ** -0.5))) + x * jnp.exp(-0.5 * x * x) * (0.5 * (2.0 / jnp.pi) ** 0.5)


def _rms_stats(x):
    r = lax.rsqrt(jnp.mean(x * x, axis=-1, keepdims=True) + RMS_EPS)
    return r, x * r


def _rms_bwd(xh, r, g, dy):
    dxh = dy * g
    dx = r * (dxh - xh * jnp.mean(dxh * xh, axis=-1, keepdims=True))
    return dx, dy * xh


def _ln_stats(x):
    mu = jnp.mean(x, axis=-1, keepdims=True)
    xc = x - mu
    r = lax.rsqrt(jnp.mean(xc * xc, axis=-1, keepdims=True) + LN_EPS)
    return r, xc * r


def _ln_bwd(xh, r, g, dy):
    dxh = dy * g
    return r * (dxh - jnp.mean(dxh, axis=-1, keepdims=True) - xh * jnp.mean(dxh * xh, axis=-1, keepdims=True))


def _colsum(x):
    return jnp.sum(x, axis=0, keepdims=True)


def _rows(tm, n, j=0):
    return pl.BlockSpec((tm, n), lambda i: (i, j))


def _whole(shape):
    return pl.BlockSpec(shape, lambda i: (0,) * len(shape))


ORDER_ONLY = pl.BlockSpec(memory_space=pl.ANY)


def _stack_heads(a):
    even = (lax.broadcasted_iota(jnp.int32, a.shape, 1) % (2 * HEAD_DIM)) < HEAD_DIM
    top = jnp.where(even, a, 0.0)
    bot = jnp.where(even, 0.0, a)
    parts = []
    for c in range(a.shape[0] // CHUNK):
        rows = slice(c * CHUNK, (c + 1) * CHUNK)
        parts += [top[rows], bot[rows]]
    return jnp.concatenate(parts, axis=0)


def _store_stacked(st, st_ref, tr_ref):
    st_ref[...] = st.astype(BF16)
    for p in range(SB_W // CHUNK):
        for c in range(st.shape[0] // (2 * CHUNK)):
            tile = st[2 * c * CHUNK:2 * (c + 1) * CHUNK, p * CHUNK:(p + 1) * CHUNK]
            tr_ref[p, c] = tile.T.astype(BF16)


def _load_transposed(tr_ref):
    rows = []
    for c in range(tr_ref.shape[1]):
        tiles = [tr_ref[p, c].T for p in range(SB_W // CHUNK)]
        rows.append(jnp.concatenate(tiles, axis=1))
    return jnp.concatenate(rows, axis=0)


def _unstack_heads(st):
    even = (lax.broadcasted_iota(jnp.int32, (CHUNK, st.shape[1]), 1) % (2 * HEAD_DIM)) < HEAD_DIM
    parts = []
    for c in range(st.shape[0] // (2 * CHUNK)):
        top = st[2 * c * CHUNK:(2 * c + 1) * CHUNK]
        bot = st[(2 * c + 1) * CHUNK:(2 * c + 2) * CHUNK]
        parts.append(jnp.where(even, top, bot))
    return jnp.concatenate(parts, axis=0)


def _fwd_in(x, g, w, qg, kg, tok):
    s = x.shape[0]
    tm = TOKEN_TILE

    def body(x_ref, g_ref, w_ref, qg_ref, kg_ref, tok_ref, proj_ref, qn_ref, kn_ref, vb_ref, kt_ref, vt_ref):
        r, xh = _rms_stats(x_ref[...])
        h = (xh * g_ref[...]).astype(BF16)
        proj = _dot_nt(h, w_ref[...])
        proj_ref[...] = proj
        gm = _group_mean_matrix(SB_W, HEAD_DIM)
        q = proj[:, OFF_SB:OFF_SB + SB_W]
        k = proj[:, OFF_SB + SB_W:OFF_SB + 2 * SB_W]
        rq = lax.rsqrt(_dot_split(q * q, gm) + RMS_EPS)
        rk = lax.rsqrt(_dot_split(k * k, gm) + RMS_EPS)
        qn_ref[...] = (q * rq * qg_ref[...] * (HEAD_DIM ** -0.5)).astype(BF16)
        _store_stacked(_stack_heads(k * rk * kg_ref[...]), kn_ref, kt_ref)
        _store_stacked(_stack_heads(proj[:, OFF_SB + 2 * SB_W:]), vb_ref, vt_ref)

    tiles = pl.BlockSpec((SB_W // CHUNK, tm // CHUNK, CHUNK, PAIR), lambda i: (0, i, 0, 0))
    tiles_shape = jax.ShapeDtypeStruct((SB_W // CHUNK, s // CHUNK, CHUNK, PAIR), BF16)
    return pl.pallas_call(
        body, name="fwd_in", grid=(s // tm,),
        in_specs=[_rows(tm, D_MODEL), _whole((1, D_MODEL)), _whole((IN_W, D_MODEL)),
                  _whole((1, SB_W)), _whole((1, SB_W)), ORDER_ONLY],
        out_specs=[_rows(tm, IN_W), _rows(tm, SB_W), _rows(2 * tm, SB_W), _rows(2 * tm, SB_W), tiles, tiles],
        out_shape=[jax.ShapeDtypeStruct((s, IN_W), F32), jax.ShapeDtypeStruct((s, SB_W), BF16),
                   jax.ShapeDtypeStruct((2 * s, SB_W), BF16), jax.ShapeDtypeStruct((2 * s, SB_W), BF16),
                   tiles_shape, tiles_shape],
        compiler_params=_cp("parallel"),
    )(x, g, w, qg, kg, tok)


SUBLANES = 8


def _shifted(win, back):
    n = win.shape[0]
    turned = [win] + [pltpu.roll(win, b if back else n - b, axis=0) for b in range(1, SUBLANES)]

    def shifted(k):
        whole, part = divmod(k, SUBLANES)
        start = 32 - whole * SUBLANES if back else whole * SUBLANES
        return turned[part][start:start + CHUNK, :]

    return shifted


def _conv_window(abuf, r0, w_ref):
    shifted = _shifted(abuf[pl.ds(pl.multiple_of(r0 + CHUNK - 32, 32), CHUNK + 32), :], back=True)
    acc = jnp.zeros((CHUNK, CONV_W), F32)
    for k in range(CONV_K):
        acc = acc + shifted(k) * w_ref[CONV_K - 1 - k:CONV_K - k, :]
    return acc, shifted


def _glu_fill(p_ref, abuf, s):
    abuf[0:CHUNK, :] = jnp.zeros((CHUNK, CONV_W), F32)

    def fill(c, carry):
        r0 = pl.multiple_of(c * CHUNK, CHUNK)
        pv = p_ref[pl.ds(r0, CHUNK), :]
        abuf[pl.ds(r0 + CHUNK, CHUNK), :] = pv[:, :CONV_W] * _sigmoid(pv[:, CONV_W:])
        return carry

    lax.fori_loop(0, s // CHUNK, fill, 0)


def _fwd_conv(proj, w, b, lg, lb, tok):
    s = proj.shape[0]

    def body(p_ref, w_ref, b_ref, lg_ref, lb_ref, tok_ref, y_ref, abuf):
        _glu_fill(p_ref, abuf, s)

        def chunk(c, carry):
            r0 = pl.multiple_of(c * CHUNK, CHUNK)
            acc, _ = _conv_window(abuf, r0, w_ref)
            r, xh = _ln_stats(acc + b_ref[...])
            ln = xh * lg_ref[...] + lb_ref[...]
            y_ref[pl.ds(r0, CHUNK), :] = ln * _sigmoid(ln)
            return carry

        lax.fori_loop(0, s // CHUNK, chunk, 0)

    return pl.pallas_call(
        body, name="fwd_conv", grid=(1,),
        in_specs=[pl.BlockSpec((s, 2 * CONV_W), lambda i: (0, 0)), _whole((CONV_K, CONV_W)),
                  _whole((1, CONV_W)), _whole((1, CONV_W)), _whole((1, CONV_W)), ORDER_ONLY],
        out_specs=_whole((s, CONV_W)),
        out_shape=jax.ShapeDtypeStruct((s, CONV_W), F32),
        scratch_shapes=[pltpu.VMEM((s + CHUNK, CONV_W), F32)],
        compiler_params=_cp("arbitrary"),
    )(proj, w, b, lg, lb, tok)


def _sg_masks():
    row = lax.broadcasted_iota(jnp.int32, (CHUNK, CHUNK), 0)
    col = lax.broadcasted_iota(jnp.int32, (CHUNK, CHUNK), 1)
    lane_head = lax.broadcasted_iota(jnp.int32, (CHUNK, SG_W), 1) // HEAD_DIM
    return row >= col, lane_head


def _sg_mix(w_ref, bias_ref, vc, tril, lane_head):
    mixed = bias_ref[...]
    for h in range(SG_W // HEAD_DIM):
        wm = jnp.where(tril, w_ref[h], 0.0).astype(BF16)
        mixed = mixed + jnp.where(lane_head == h, _dot(wm, vc), 0.0)
    return mixed


def _fwd_sg(proj, lg, lb, w, bias):
    s = proj.shape[0]
    tm = TOKEN_TILE

    def body(p_ref, lg_ref, lb_ref, w_ref, bias_ref, y_ref):
        ge = _gelu(p_ref[...])
        u = ge[:, :SG_W]
        r, xh = _ln_stats(ge[:, SG_W:])
        vln = (xh * lg_ref[...] + lb_ref[...]).astype(BF16)
        tril, lane_head = _sg_masks()
        for c in range(tm // CHUNK):
            rows = slice(c * CHUNK, (c + 1) * CHUNK)
            y_ref[rows, :] = u[rows] * _sg_mix(w_ref, bias_ref, vln[rows], tril, lane_head)

    return pl.pallas_call(
        body, name="fwd_sg", grid=(s // tm,),
        in_specs=[_rows(tm, 2 * SG_W, 1), _whole((1, SG_W)), _whole((1, SG_W)),
                  _whole((SG_W // HEAD_DIM, CHUNK, CHUNK)), _whole((CHUNK, SG_W))],
        out_specs=_rows(tm, SG_W),
        out_shape=jax.ShapeDtypeStruct((s, SG_W), F32),
        compiler_params=_cp("parallel"),
    )(proj, lg, lb, w, bias)


SB_Q = 2 * CHUNK
PAIR = 2 * CHUNK
SB_PAIRS = 2


def _pair_tri(kind):
    row = lax.broadcasted_iota(jnp.int32, (PAIR, PAIR), 0)
    col = lax.broadcasted_iota(jnp.int32, (PAIR, PAIR), 1)
    tri = {"after": row > col, "upto": row <= col, "before": row < col}[kind]
    return jnp.where(((row // CHUNK) == (col // CHUNK)) & tri, 1.0, 0.0).astype(BF16)


def _sb_scores(z, qpos0, kpos0, masked):
    sp = jnp.maximum(z, 0.0) + jnp.log(1.0 + jnp.exp(-jnp.abs(z)))
    if not masked:
        return z, sp, sp.astype(BF16), None
    row = lax.broadcasted_iota(jnp.int32, z.shape, 0)
    col = lax.broadcasted_iota(jnp.int32, z.shape, 1) % CHUNK
    mask = (kpos0 + col) < (qpos0 + row)
    return z, sp, jnp.where(mask, sp, 0.0).astype(BF16), mask


def _per_head(c0, c1):
    return jnp.concatenate([jnp.broadcast_to(c0, (SB_Q, CHUNK)), jnp.broadcast_to(c1, (SB_Q, CHUNK))], axis=1)


def _fwd_sb(qn, ktr, vst):
    s = qn.shape[0]
    np_ = SB_PAIRS

    def body(q_ref, k_ref, v_ref, after_ref, o_ref, lt_ref, z_buf, att_buf):
        i = pl.program_id(1)
        first = i * (SB_Q // CHUNK)
        last = first + SB_Q // CHUNK - 1
        after = after_ref[...]
        lanes = [slice(pr * CHUNK, (pr + 1) * CHUNK) for pr in range(np_)]
        qs = [q_ref[:, lanes[pr]] for pr in range(np_)]

        def rows(kb):
            return pl.ds(pl.multiple_of(kb * PAIR, PAIR), PAIR)

        def block(kb, carry, masked):
            out = []
            for pr in range(np_):
                acc, c0, c1 = carry[pr]
                z_next = _dot(qs[pr], k_ref[pr, jnp.maximum(kb - 1, 0)])
                pv = _dot(att_buf[pr], v_ref[rows(jnp.minimum(kb + 1, last)), lanes[pr]])
                z, sp, nlb, mask = _sb_scores(z_buf[pr], i * SB_Q, kb * CHUNK, masked)
                loc = _dot(nlb, after)
                att = jnp.exp(z - sp - loc - _per_head(c0, c1))
                if masked:
                    att = jnp.where(mask, att, 0.0)
                z_buf[pr] = z_next
                att_buf[pr] = att.astype(BF16)
                out.append((acc + pv, c0 + loc[:, 0:1] + nlb[:, 0:1].astype(F32),
                            c1 + loc[:, CHUNK:CHUNK + 1] + nlb[:, CHUNK:CHUNK + 1].astype(F32)))
            return tuple(out)

        for pr in range(np_):
            z_buf[pr] = _dot(qs[pr], k_ref[pr, last])
        att_buf[...] = jnp.zeros_like(att_buf)
        zero = jnp.zeros((SB_Q, 1), F32)
        carry = ((jnp.zeros((SB_Q, CHUNK), F32), zero, zero),) * np_
        for back in range(SB_Q // CHUNK):
            carry = block(last - back, carry, True)
        carry = lax.fori_loop(0, first, lambda j, c: block(first - 1 - j, c, False), carry)
        for pr, (acc, c0, c1) in enumerate(carry):
            o_ref[:, lanes[pr]] = acc + _dot(att_buf[pr], v_ref[rows(0), lanes[pr]])
            lt_ref[:, lanes[pr]] = jnp.concatenate([jnp.broadcast_to(c0, (SB_Q, HEAD_DIM)),
                                                    jnp.broadcast_to(c1, (SB_Q, HEAD_DIM))], axis=1)

    blk = pl.BlockSpec((SB_Q, np_ * CHUNK), lambda p, i: (i, p))
    seq = pl.BlockSpec((2 * s, np_ * CHUNK), lambda p, i: (0, p))
    return pl.pallas_call(
        body, name="fwd_sb", grid=(SB_W // CHUNK // np_, s // SB_Q),
        in_specs=[blk, pl.BlockSpec((np_, s // CHUNK, CHUNK, PAIR), lambda p, i: (p, 0, 0, 0)), seq,
                  pl.BlockSpec((PAIR, PAIR), lambda p, i: (0, 0))],
        out_specs=[blk, blk],
        out_shape=[jax.ShapeDtypeStruct((s, SB_W), F32)] * 2,
        scratch_shapes=[pltpu.VMEM((np_, SB_Q, PAIR), F32), pltpu.VMEM((np_, SB_Q, PAIR), BF16)],
        compiler_params=_cp("parallel", "parallel"),
    )(qn, ktr, vst, _pair_tri("after"))


def _group_norms(yc, ys, yb):
    return [_rms_stats(yc), _rms_stats(ys), _rms_stats(yb)]


def _fwd_out(yc, ys, yb, g, w, x, tok):
    s = x.shape[0]
    tm = TOKEN_TILE

    def body(yc_ref, ys_ref, yb_ref, g_ref, w_ref, x_ref, tok_ref, o_ref):
        stats = _group_norms(yc_ref[...], ys_ref[...], yb_ref[...])
        cat = jnp.concatenate([xh for _, xh in stats], axis=1) * g_ref[...]
        o_ref[...] = x_ref[...] + _dot(cat.astype(BF16), w_ref[...])

    return pl.pallas_call(
        body, name="fwd_out", grid=(s // tm,),
        in_specs=[_rows(tm, CONV_W), _rows(tm, SG_W), _rows(tm, SB_W), _whole((1, D_MODEL)),
                  _whole((D_MODEL, D_MODEL)), _rows(tm, D_MODEL), ORDER_ONLY],
        out_specs=_rows(tm, D_MODEL),
        out_shape=jax.ShapeDtypeStruct((s, D_MODEL), F32),
        compiler_params=_cp("parallel"),
    )(yc, ys, yb, g, w, x, tok)


def _fwd_ffn(x, g, wgu, wd, tok):
    s = x.shape[0]
    tm = TOKEN_TILE

    def body(x_ref, g_ref, wgu_ref, wd_ref, tok_ref, gu_ref, o_ref):
        x = x_ref[...]
        r, xh = _rms_stats(x)
        gu = _dot_nt((xh * g_ref[...]).astype(BF16), wgu_ref[...])
        gu_ref[...] = gu.astype(BF16)
        gate = gu[:, :FFN_H]
        act = gate * _sigmoid(gate) * gu[:, FFN_H:]
        o_ref[...] = x + _dot(act.astype(BF16), wd_ref[...])

    return pl.pallas_call(
        body, name="fwd_ffn", grid=(s // tm,),
        in_specs=[_rows(tm, D_MODEL), _whole((1, D_MODEL)),
                  pl.BlockSpec((2 * FFN_H, D_MODEL), lambda i: (0, 0), pipeline_mode=pl.Buffered(1)),
                  pl.BlockSpec((FFN_H, D_MODEL), lambda i: (0, 0), pipeline_mode=pl.Buffered(1)), ORDER_ONLY],
        out_specs=[_rows(tm, 2 * FFN_H), _rows(tm, D_MODEL)],
        out_shape=[jax.ShapeDtypeStruct((s, 2 * FFN_H), BF16), jax.ShapeDtypeStruct((s, D_MODEL), F32)],
        compiler_params=_cp("parallel"),
    )(x, g, wgu, wd, tok)


def _loss_head(y, target):
    s = y.shape[0]
    tm = TOKEN_TILE

    def body(y_ref, t_ref, l_ref, d_ref):
        @pl.when(pl.program_id(0) == 0)
        def _():
            l_ref[...] = jnp.zeros_like(l_ref)

        err = y_ref[...] - t_ref[...]
        d_ref[...] = err * (1.0 / D_MODEL)
        l_ref[...] += 0.5 * jnp.sum(jnp.mean(err * err, axis=-1, keepdims=True), axis=0, keepdims=True)

    return pl.pallas_call(
        body, name="loss_head", grid=(s // tm,),
        in_specs=[_rows(tm, D_MODEL), _rows(tm, D_MODEL)],
        out_specs=[_whole((1, 1)), _rows(tm, D_MODEL)],
        out_shape=[jax.ShapeDtypeStruct((1, 1), F32), jax.ShapeDtypeStruct((s, D_MODEL), F32)],
        compiler_params=_cp("arbitrary"),
    )(y, target)


def _accumulate(ref, value):
    @pl.when(pl.program_id(0) == 0)
    def _():
        ref[...] = jnp.zeros_like(ref)

    ref[...] += value


def _bwd_ffn(dxo, gu, xm, g, wgu, wd, tok):
    s = dxo.shape[0]
    tm = TOKEN_TILE

    def body(dxo_ref, gu_ref, xm_ref, g_ref, wgu_ref, wd_ref, tok_ref, dgu_ref, act_ref, h_ref, dxm_ref, dg_ref):
        dxo = dxo_ref[...]
        gu = gu_ref[...].astype(F32)
        gate, up = gu[:, :FFN_H], gu[:, FFN_H:]
        sg = _sigmoid(gate)
        sl = gate * sg
        act_ref[...] = (sl * up).astype(BF16)
        dact = _dot_nt(dxo.astype(BF16), wd_ref[...])
        dgate = dact * up * (sg * (1.0 + gate * (1.0 - sg)))
        dgu = jnp.concatenate([dgate, dact * sl], axis=1).astype(BF16)
        dgu_ref[...] = dgu
        dh = _dot(dgu, wgu_ref[...])
        r, xh = _rms_stats(xm_ref[...])
        h_ref[...] = (xh * g_ref[...]).astype(BF16)
        dx, dgrow = _rms_bwd(xh, r, g_ref[...], dh)
        dxm_ref[...] = dxo + dx
        _accumulate(dg_ref, _colsum(dgrow))

    return pl.pallas_call(
        body, name="bwd_ffn", grid=(s // tm,),
        in_specs=[_rows(tm, D_MODEL), _rows(tm, 2 * FFN_H), _rows(tm, D_MODEL), _whole((1, D_MODEL)),
                  pl.BlockSpec((2 * FFN_H, D_MODEL), lambda i: (0, 0), pipeline_mode=pl.Buffered(1)),
                  pl.BlockSpec((FFN_H, D_MODEL), lambda i: (0, 0), pipeline_mode=pl.Buffered(1)), ORDER_ONLY],
        out_specs=[_rows(tm, 2 * FFN_H), _rows(tm, FFN_H), _rows(tm, D_MODEL), _rows(tm, D_MODEL),
                   _whole((1, D_MODEL))],
        out_shape=[jax.ShapeDtypeStruct((s, 2 * FFN_H), BF16), jax.ShapeDtypeStruct((s, FFN_H), BF16),
                   jax.ShapeDtypeStruct((s, D_MODEL), BF16), jax.ShapeDtypeStruct((s, D_MODEL), F32),
                   jax.ShapeDtypeStruct((1, D_MODEL), F32)],
        compiler_params=_cp("arbitrary"),
    )(dxo, gu, xm, g, wgu, wd, tok)


def _matmul_tn(a, b, tm, tn, out_dtype=BF16):
    s, m = a.shape
    n = b.shape[1]

    def body(a_ref, b_ref, o_ref):
        o_ref[...] = _dot_tn(a_ref[...].astype(BF16), b_ref[...].astype(BF16)).astype(out_dtype)

    return pl.pallas_call(
        body, name="weight_grad", grid=(m // tm, n // tn),
        in_specs=[pl.BlockSpec((s, tm), lambda i, j: (0, i)), pl.BlockSpec((s, tn), lambda i, j: (0, j))],
        out_specs=pl.BlockSpec((tm, tn), lambda i, j: (i, j)),
        out_shape=jax.ShapeDtypeStruct((m, n), out_dtype),
        compiler_params=_cp("parallel", "parallel"),
    )(a, b)


def _bwd_out(dxm, yc, ys, yb, g, w, tok):
    s = dxm.shape[0]
    tm = TOKEN_TILE

    def body(dxm_ref, yc_ref, ys_ref, yb_ref, g_ref, w_ref, tok_ref, dyc_ref, dys_ref, dyb_ref, cat_ref, dg_ref):
        stats = _group_norms(yc_ref[...], ys_ref[...], yb_ref[...])
        g = g_ref[...]
        cat_ref[...] = (jnp.concatenate([xh for _, xh in stats], axis=1) * g).astype(BF16)
        dcat = _dot_nt(dxm_ref[...].astype(BF16), w_ref[...])
        dgs = []
        off = 0
        for (r, xh), out in zip(stats, (dyc_ref, dys_ref, dyb_ref)):
            cols = slice(off, off + xh.shape[1])
            dx, dgrow = _rms_bwd(xh, r, g[:, cols], dcat[:, cols])
            out[...] = dx
            dgs.append(_colsum(dgrow))
            off += xh.shape[1]
        _accumulate(dg_ref, jnp.concatenate(dgs, axis=1))

    return pl.pallas_call(
        body, name="bwd_out", grid=(s // tm,),
        in_specs=[_rows(tm, D_MODEL), _rows(tm, CONV_W), _rows(tm, SG_W), _rows(tm, SB_W),
                  _whole((1, D_MODEL)), _whole((D_MODEL, D_MODEL)), ORDER_ONLY],
        out_specs=[_rows(tm, CONV_W), _rows(tm, SG_W), _rows(tm, SB_W), _rows(tm, D_MODEL),
                   _whole((1, D_MODEL))],
        out_shape=[jax.ShapeDtypeStruct((s, CONV_W), F32), jax.ShapeDtypeStruct((s, SG_W), F32),
                   jax.ShapeDtypeStruct((s, SB_W), F32), jax.ShapeDtypeStruct((s, D_MODEL), BF16),
                   jax.ShapeDtypeStruct((1, D_MODEL), F32)],
        compiler_params=_cp("arbitrary"),
    )(dxm, yc, ys, yb, g, w, tok)


def _bwd_sb(qn, kst, ktr, vtr, dy, ltot, tok):
    s = qn.shape[0]
    np_ = SB_PAIRS

    def body(q_ref, k_ref, kt_ref, vt_ref, do_ref, lt_ref, upto_ref, before_ref, tok_ref, dq_ref, dk_ref, dv_ref,
             z_buf, da_buf, dz_buf, att_buf):
        i = pl.program_id(1)
        first = i * (SB_Q // CHUNK)
        last = first + SB_Q // CHUNK - 1

        @pl.when(i == 0)
        def _():
            dk_ref[...] = jnp.zeros_like(dk_ref)
            dv_ref[...] = jnp.zeros_like(dv_ref)

        lanes = [slice(pr * CHUNK, (pr + 1) * CHUNK) for pr in range(np_)]
        qs = [q_ref[:, lanes[pr]] for pr in range(np_)]
        dos = [do_ref[:, lanes[pr]] for pr in range(np_)]
        dobs = [do.astype(BF16) for do in dos]
        q_ts = [q.astype(F32).T.astype(BF16) for q in qs]
        do_ts = [do.T.astype(BF16) for do in dos]
        ltots = [_per_head(lt_ref[:, pr * CHUNK:pr * CHUNK + 1],
                           lt_ref[:, pr * CHUNK + HEAD_DIM:pr * CHUNK + HEAD_DIM + 1]) for pr in range(np_)]
        upto = upto_ref[...]
        before = before_ref[...]
        last0, last1 = slice(CHUNK - 1, CHUNK), slice(PAIR - 1, PAIR)

        def rows(kb):
            return pl.ds(pl.multiple_of(kb * PAIR, PAIR), PAIR)

        def ahead(pr, kb):
            return _dot(qs[pr], kt_ref[pr, kb]), _dot(dobs[pr], vt_ref[pr, kb])

        def behind(pr, kb, dq):
            dzb = dz_buf[pr]
            dk_ref[pr, kb] += _dot(q_ts[pr], dzb)
            dv_ref[pr, kb] += _dot(do_ts[pr], att_buf[pr])
            return dq + _dot(dzb, k_ref[rows(kb), lanes[pr]])

        def block(kb, carry, masked):
            out = []
            for pr in range(np_):
                dq, p0, p1, e0, e1 = carry[pr]
                z_next, da_next = ahead(pr, jnp.minimum(kb + 1, last))
                dq = behind(pr, jnp.maximum(kb - 1, 0), dq)
                z, sp, nlb, mask = _sb_scores(z_buf[pr], i * SB_Q, kb * CHUNK, masked)
                pin = _dot(nlb, upto) + _per_head(p0, p1)
                sig = jnp.exp(z - sp)
                att = jnp.exp(z - sp - (ltots[pr] - pin))
                if masked:
                    att = jnp.where(mask, att, 0.0)
                e = att * da_buf[pr]
                ebefore = _dot(e.astype(BF16), before) + _per_head(e0, e1)
                dz = e - sig * (e + ebefore)
                if masked:
                    dz = jnp.where(mask, dz, 0.0)
                z_buf[pr] = z_next
                da_buf[pr] = da_next
                dz_buf[pr] = dz.astype(BF16)
                att_buf[pr] = att.astype(BF16)
                out.append((dq, pin[:, last0], pin[:, last1],
                            ebefore[:, last0] + e[:, last0], ebefore[:, last1] + e[:, last1]))
            return tuple(out)

        for pr in range(np_):
            z_buf[pr], da_buf[pr] = ahead(pr, 0)
        dz_buf[...] = jnp.zeros_like(dz_buf)
        att_buf[...] = jnp.zeros_like(att_buf)
        zero = jnp.zeros((SB_Q, 1), F32)
        carry = ((jnp.zeros((SB_Q, CHUNK), F32), zero, zero, zero, zero),) * np_
        carry = lax.fori_loop(0, first, lambda kb, c: block(kb, c, False), carry)
        for ahead_of in range(SB_Q // CHUNK):
            carry = block(first + ahead_of, carry, True)
        for pr in range(np_):
            dq_ref[:, lanes[pr]] = behind(pr, last, carry[pr][0])

    blk = pl.BlockSpec((SB_Q, np_ * CHUNK), lambda p, i: (i, p))
    seq = pl.BlockSpec((2 * s, np_ * CHUNK), lambda p, i: (0, p))
    tiles = pl.BlockSpec((np_, s // CHUNK, CHUNK, PAIR), lambda p, i: (p, 0, 0, 0))
    tri = pl.BlockSpec((PAIR, PAIR), lambda p, i: (0, 0))
    return pl.pallas_call(
        body, name="bwd_sb", grid=(SB_W // CHUNK // np_, s // SB_Q),
        in_specs=[blk, seq, tiles, tiles, blk, blk, tri, tri, ORDER_ONLY],
        out_specs=[blk, tiles, tiles],
        out_shape=[jax.ShapeDtypeStruct((s, SB_W), F32)]
        + [jax.ShapeDtypeStruct((SB_W // CHUNK, s // CHUNK, CHUNK, PAIR), F32)] * 2,
        scratch_shapes=[pltpu.VMEM((np_, SB_Q, PAIR), F32), pltpu.VMEM((np_, SB_Q, PAIR), F32),
                        pltpu.VMEM((np_, SB_Q, PAIR), BF16), pltpu.VMEM((np_, SB_Q, PAIR), BF16)],
        compiler_params=_cp("parallel", "arbitrary"),
    )(qn, kst, ktr, vtr, dy, ltot, _pair_tri("upto"), _pair_tri("before"), tok)


def _head_sum(row):
    acc = row[:, 0:HEAD_DIM]
    for h in range(1, SB_W // HEAD_DIM):
        acc = acc + row[:, h * HEAD_DIM:(h + 1) * HEAD_DIM]
    return acc


def _bwd_qk(proj, dqs, dkn, dv, qg, kg):
    s = proj.shape[0]
    tm = TOKEN_TILE
    tiles = pl.BlockSpec((SB_W // CHUNK, tm // CHUNK, CHUNK, PAIR), lambda i: (0, i, 0, 0))

    def body(q_ref, k_ref, dqs_ref, dkn_ref, dv_ref, qg_ref, kg_ref, dp_ref, dqg_ref, dkg_ref, qacc, kacc):
        i = pl.program_id(0)
        gm = _group_mean_matrix(SB_W, HEAD_DIM)

        def one(x, dy, g, acc):
            r = lax.rsqrt(_dot_split(x * x, gm) + RMS_EPS)
            xh = x * r
            dxh = dy * g
            _accumulate(acc, _colsum(dy * xh))
            return r * (dxh - xh * _dot_split(dxh * xh, gm))

        dq = one(q_ref[...], dqs_ref[...] * (HEAD_DIM ** -0.5), qg_ref[...], qacc)
        dk = one(k_ref[...], _unstack_heads(_load_transposed(dkn_ref)), kg_ref[...], kacc)
        dp_ref[...] = jnp.concatenate([dq, dk, _unstack_heads(_load_transposed(dv_ref))], axis=1).astype(BF16)

        @pl.when(i == pl.num_programs(0) - 1)
        def _():
            dqg_ref[...] = _head_sum(qacc[...])
            dkg_ref[...] = _head_sum(kacc[...])

    return pl.pallas_call(
        body, name="bwd_qk", grid=(s // tm,),
        in_specs=[_rows(tm, SB_W, OFF_SB // SB_W), _rows(tm, SB_W, OFF_SB // SB_W + 1),
                  _rows(tm, SB_W), tiles, tiles, _whole((1, SB_W)), _whole((1, SB_W))],
        out_specs=[_rows(tm, 3 * SB_W), _whole((1, HEAD_DIM)), _whole((1, HEAD_DIM))],
        out_shape=[jax.ShapeDtypeStruct((s, 3 * SB_W), BF16), jax.ShapeDtypeStruct((1, HEAD_DIM), F32),
                   jax.ShapeDtypeStruct((1, HEAD_DIM), F32)],
        scratch_shapes=[pltpu.VMEM((1, SB_W), F32), pltpu.VMEM((1, SB_W), F32)],
        compiler_params=_cp("arbitrary"),
    )(proj, proj, dqs, dkn, dv, qg, kg)


def _bwd_sg(proj, dy, lg, lb, w, bias):
    s = proj.shape[0]
    tm = TOKEN_TILE
    nh = SG_W // HEAD_DIM

    def body(p_ref, dy_ref, lg_ref, lb_ref, w_ref, bias_ref, dp_ref, dlg_ref, dlb_ref, dw_ref, db_ref, dbias):
        i = pl.program_id(0)
        uv = p_ref[...]
        ge = _gelu(uv)
        u = ge[:, :SG_W]
        r, xh = _ln_stats(ge[:, SG_W:])
        vln = (xh * lg_ref[...] + lb_ref[...]).astype(BF16)
        dy = dy_ref[...]
        tril, lane_head = _sg_masks()

        @pl.when(i == 0)
        def _():
            dw_ref[...] = jnp.zeros_like(dw_ref)
            dbias[...] = jnp.zeros_like(dbias)

        dus, dvlns = [], []
        for c in range(tm // CHUNK):
            rows = slice(c * CHUNK, (c + 1) * CHUNK)
            vc = vln[rows]
            dus.append(dy[rows] * _sg_mix(w_ref, bias_ref, vc, tril, lane_head))
            dm = dy[rows] * u[rows]
            dbias[...] += dm
            dvc = jnp.zeros((CHUNK, SG_W), F32)
            for h in range(nh):
                dmh = jnp.where(lane_head == h, dm, 0.0).astype(BF16)
                dw_ref[h] += jnp.where(tril, _dot_nt(dmh, vc), 0.0)
                wm = jnp.where(tril, w_ref[h], 0.0).astype(BF16)
                dvc = dvc + _dot_tn(wm, dmh)
            dvlns.append(dvc)
        du = jnp.concatenate(dus, axis=0)
        dvln = jnp.concatenate(dvlns, axis=0)
        _accumulate(dlg_ref, _colsum(dvln * xh))
        _accumulate(dlb_ref, _colsum(dvln))
        dv = _ln_bwd(xh, r, lg_ref[...], dvln)
        dp_ref[...] = (jnp.concatenate([du, dv], axis=1) * _gelu_grad(uv)).astype(BF16)

        @pl.when(i == pl.num_programs(0) - 1)
        def _():
            lane = lax.broadcasted_iota(jnp.int32, (CHUNK, CHUNK), 1)
            acc = dbias[...]
            out = jnp.zeros((CHUNK, CHUNK), F32)
            for h in range(nh):
                hs = jnp.sum(acc[:, h * HEAD_DIM:(h + 1) * HEAD_DIM], axis=1, keepdims=True)
                out = out + jnp.where(lane == h, hs, 0.0)
            db_ref[...] = out

    return pl.pallas_call(
        body, name="bwd_sg", grid=(s // tm,),
        in_specs=[_rows(tm, 2 * SG_W, 1), _rows(tm, SG_W), _whole((1, SG_W)), _whole((1, SG_W)),
                  _whole((nh, CHUNK, CHUNK)), _whole((CHUNK, SG_W))],
        out_specs=[_rows(tm, 2 * SG_W), _whole((1, SG_W)), _whole((1, SG_W)), _whole((nh, CHUNK, CHUNK)),
                   _whole((CHUNK, CHUNK))],
        out_shape=[jax.ShapeDtypeStruct((s, 2 * SG_W), BF16), jax.ShapeDtypeStruct((1, SG_W), F32),
                   jax.ShapeDtypeStruct((1, SG_W), F32), jax.ShapeDtypeStruct((nh, CHUNK, CHUNK), F32),
                   jax.ShapeDtypeStruct((CHUNK, CHUNK), F32)],
        scratch_shapes=[pltpu.VMEM((CHUNK, SG_W), F32)],
        compiler_params=_cp("arbitrary"),
    )(proj, dy, lg, lb, w, bias)


def _bwd_conv(proj, dy, w, b, lg, lb):
    s = proj.shape[0]

    def body(p_ref, dy_ref, w_ref, b_ref, lg_ref, lb_ref, dp_ref, dw_ref, db_ref, dlg_ref, dlb_ref,
             abuf, dcbuf):
        _glu_fill(p_ref, abuf, s)
        dcbuf[pl.ds(s, CHUNK), :] = jnp.zeros((CHUNK, CONV_W), F32)
        dw_ref[...] = jnp.zeros_like(dw_ref)

        def chunk(c, carry):
            db, dlg, dlb = carry
            r0 = pl.multiple_of(c * CHUNK, CHUNK)
            acc, shifted = _conv_window(abuf, r0, w_ref)
            r, xh = _ln_stats(acc + b_ref[...])
            ln = xh * lg_ref[...] + lb_ref[...]
            sg = _sigmoid(ln)
            dl = dy_ref[pl.ds(r0, CHUNK), :] * (sg * (1.0 + ln * (1.0 - sg)))
            dc = _ln_bwd(xh, r, lg_ref[...], dl)
            dcbuf[pl.ds(r0, CHUNK), :] = dc
            for k in range(CONV_K):
                dw_ref[CONV_K - 1 - k:CONV_K - k, :] += _colsum(dc * shifted(k))
            return db + _colsum(dc), dlg + _colsum(dl * xh), dlb + _colsum(dl)

        zero = jnp.zeros((1, CONV_W), F32)
        db, dlg, dlb = lax.fori_loop(0, s // CHUNK, chunk, (zero, zero, zero))
        db_ref[...] = db
        dlg_ref[...] = dlg
        dlb_ref[...] = dlb

        def chunk_back(c, carry):
            r0 = pl.multiple_of(c * CHUNK, CHUNK)
            shifted = _shifted(dcbuf[pl.ds(r0, CHUNK + 32), :], back=False)
            da = jnp.zeros((CHUNK, CONV_W), F32)
            for k in range(CONV_K):
                da = da + shifted(k) * w_ref[CONV_K - 1 - k:CONV_K - k, :]
            pv = p_ref[pl.ds(r0, CHUNK), :]
            val, sg = pv[:, :CONV_W], _sigmoid(pv[:, CONV_W:])
            dp_ref[pl.ds(r0, CHUNK), :] = jnp.concatenate([da * sg, da * val * sg * (1.0 - sg)], axis=1).astype(BF16)
            return carry

        lax.fori_loop(0, s // CHUNK, chunk_back, 0)

    row = _whole((1, CONV_W))
    return pl.pallas_call(
        body, name="bwd_conv", grid=(1,),
        in_specs=[pl.BlockSpec((s, 2 * CONV_W), lambda i: (0, 0)), _whole((s, CONV_W)),
                  _whole((CONV_K, CONV_W)), row, row, row],
        out_specs=[_whole((s, 2 * CONV_W)), _whole((CONV_K, CONV_W)), row, row, row],
        out_shape=[jax.ShapeDtypeStruct((s, 2 * CONV_W), BF16), jax.ShapeDtypeStruct((CONV_K, CONV_W), F32)]
        + [jax.ShapeDtypeStruct((1, CONV_W), F32)] * 3,
        scratch_shapes=[pltpu.VMEM((s + CHUNK, CONV_W), F32), pltpu.VMEM((s + CHUNK, CONV_W), F32)],
        compiler_params=_cp("arbitrary"),
    )(proj, dy, w, b, lg, lb)


def _bwd_in(dpc, dps, dpb, x, g, w, dxm):
    s = x.shape[0]
    tm = TOKEN_TILE

    def body(dpc_ref, dps_ref, dpb_ref, x_ref, g_ref, w_ref, dxm_ref, dx_ref, h_ref, dp_ref, dg_ref):
        dp = jnp.concatenate([dpc_ref[...], dps_ref[...], dpb_ref[...]], axis=1)
        dp_ref[...] = dp
        dh = _dot(dp, w_ref[...])
        r, xh = _rms_stats(x_ref[...])
        h_ref[...] = (xh * g_ref[...]).astype(BF16)
        dx, dgrow = _rms_bwd(xh, r, g_ref[...], dh)
        dx_ref[...] = dxm_ref[...] + dx
        _accumulate(dg_ref, _colsum(dgrow))

    return pl.pallas_call(
        body, name="bwd_in", grid=(s // tm,),
        in_specs=[_rows(tm, 2 * CONV_W), _rows(tm, 2 * SG_W), _rows(tm, 3 * SB_W), _rows(tm, D_MODEL),
                  _whole((1, D_MODEL)), _whole((IN_W, D_MODEL)), _rows(tm, D_MODEL)],
        out_specs=[_rows(tm, D_MODEL), _rows(tm, D_MODEL), _rows(tm, IN_W), _whole((1, D_MODEL))],
        out_shape=[jax.ShapeDtypeStruct((s, D_MODEL), F32), jax.ShapeDtypeStruct((s, D_MODEL), BF16),
                   jax.ShapeDtypeStruct((s, IN_W), BF16), jax.ShapeDtypeStruct((1, D_MODEL), F32)],
        compiler_params=_cp("arbitrary"),
    )(dpc, dps, dpb, x, g, w, dxm)


SMALL = ("mix_norm_g", "conv_w", "conv_b", "conv_ln_g", "conv_ln_b", "sg_ln_g", "sg_ln_b", "sg_w", "sg_b",
         "q_norm_g", "k_norm_g", "out_norm_g", "ffn_norm_g")
LARGE = ("w_in", "w_out", "w_gate_up", "w_down")


def _row(v):
    return v.reshape(1, -1)


def _layer_params(p, large, l):
    q = {k: v[l] for k, v in p.items()}
    return dict(
        q, **large,
        mix_norm_g=_row(q["mix_norm_g"]), conv_b=_row(q["conv_b"]), conv_ln_g=_row(q["conv_ln_g"]),
        conv_ln_b=_row(q["conv_ln_b"]), sg_ln_g=_row(q["sg_ln_g"]), sg_ln_b=_row(q["sg_ln_b"]),
        out_norm_g=_row(q["out_norm_g"]), ffn_norm_g=_row(q["ffn_norm_g"]),
        qg=_row(jnp.tile(q["q_norm_g"], SB_W // HEAD_DIM)), kg=_row(jnp.tile(q["k_norm_g"], SB_W // HEAD_DIM)),
        sg_bias=jnp.repeat(q["sg_b"].T, HEAD_DIM, axis=1),
    )


def _layer_fwd(x, q, tok, after_in, after_mixers, after_out):
    proj, qn, kn, vb, kt, vt = _fwd_in(x, q["mix_norm_g"], q["w_in"], q["qg"], q["kg"], tok)
    yc = _fwd_conv(proj, q["conv_w"], q["conv_b"], q["conv_ln_g"], q["conv_ln_b"], after_in(proj))
    ys = _fwd_sg(proj, q["sg_ln_g"], q["sg_ln_b"], q["sg_w"], q["sg_bias"])
    yb, lt = _fwd_sb(qn, kt, vb)
    rest, tok = after_mixers(yb)
    q = dict(q, **rest)
    xm = _fwd_out(yc, ys, yb, q["out_norm_g"], q["w_out"], x, tok)
    gu, xo = _fwd_ffn(xm, q["ffn_norm_g"], q["w_gate_up"], q["w_down"], after_out(xm))
    return xo, q, dict(x=x, proj=proj, qn=qn, kn=kn, kt=kt, vt=vt, lt=lt, yc=yc, ys=ys, yb=yb, xm=xm, gu=gu)


def _layer_bwd_ffn(dxo, q, st, tok):
    dgu, act, h2, dxm, d_ffn_g = _bwd_ffn(dxo, st["gu"], st["xm"], q["ffn_norm_g"], q["w_gate_up"], q["w_down"],
                                          tok)
    d_wgu, d_wd = _matmul_tn(dgu, h2, 512, D_MODEL), _matmul_tn(act, dxo, FFN_H // 2, D_MODEL)
    dyc, dys, dyb, cat, d_out_g = _bwd_out(dxm, st["yc"], st["ys"], st["yb"], q["out_norm_g"], q["w_out"], tok)
    return dxm, (dyc, dys, dyb, d_ffn_g, d_out_g), d_wgu, d_wd, _matmul_tn(cat, dxm, 512, D_MODEL)


def _layer_bwd_mix(dxm, carried, q, st, tok):
    dyc, dys, dyb, d_ffn_g, d_out_g = carried
    dqs, dkn, dv = _bwd_sb(st["qn"], st["kn"], st["kt"], st["vt"], dyb, st["lt"], tok)
    dpb, d_qg, d_kg = _bwd_qk(st["proj"], dqs, dkn, dv, q["qg"], q["kg"])
    dps, d_sg_lg, d_sg_lb, d_sg_w, d_sg_b = _bwd_sg(st["proj"], dys, q["sg_ln_g"], q["sg_ln_b"], q["sg_w"],
                                                    q["sg_bias"])
    dpc, d_conv_w, d_conv_b, d_conv_lg, d_conv_lb = _bwd_conv(st["proj"], dyc, q["conv_w"], q["conv_b"],
                                                              q["conv_ln_g"], q["conv_ln_b"])
    dx, h1, dp, d_mix_g = _bwd_in(dpc, dps, dpb, st["x"], q["mix_norm_g"], q["w_in"], dxm)
    d_win = _matmul_tn(dp, h1, 512, D_MODEL)
    small = dict(
        mix_norm_g=d_mix_g[0], conv_w=d_conv_w, conv_b=d_conv_b[0], conv_ln_g=d_conv_lg[0],
        conv_ln_b=d_conv_lb[0], sg_ln_g=d_sg_lg[0], sg_ln_b=d_sg_lb[0], sg_w=d_sg_w,
        sg_b=d_sg_b[:, :SG_W // HEAD_DIM].T, q_norm_g=d_qg[0], k_norm_g=d_kg[0], out_norm_g=d_out_g[0],
        ffn_norm_g=d_ffn_g[0])
    return dx, d_win, small


def _position():
    x, y, c = lax.axis_index("x"), lax.axis_index("y"), lax.axis_index("c")
    return x, y, c


def _flat(px, py, pc):
    return 4 * px + 2 * py + pc


IN_HBM = pl.BlockSpec(memory_space=pltpu.HBM)
IN_SEM = pl.BlockSpec(memory_space=pltpu.SEMAPHORE)
EFFECT = pltpu.SideEffectType.DATAFLOW_SIDE_EFFECTING
COPIES = dict(scatter=7, spread=7, spread_chips=4, **{"pass": 3})


def _exchange_copies(kinds, src_refs, land_refs, send_sems, recv_sems, layer, arrival):
    x, y, c = _position()
    me = _flat(x, y, c)
    everyone = [(x ^ (k >> 2 & 1), y ^ (k >> 1 & 1), c ^ (k & 1)) for k in range(1, N_DEV)]
    sibling = (x, y, 1 - c)
    chips = [(1 - x, y, c), (x, 1 - y, c), (1 - x, 1 - y, c)]
    out = []
    srcs = iter(src_refs)
    for kind, land in zip(kinds, land_refs):
        land = land if layer is None else land.at[layer]
        if kind == "scatter":
            src = next(srcs)
            moves = [(src.at[_flat(*p)], me, _flat(*p), p) for p in everyone]
        elif kind in ("spread", "spread_chips"):
            src = next(srcs)
            moves = [(src, me, _flat(*p), p) for p in (everyone if kind == "spread" else [sibling] + chips)]
        else:
            moves = [(land.at[_flat(*p)], _flat(*p), _flat(p[0], p[1], 1 - c), sibling) for p in chips]
        for src_block, there, here, peer in moves:
            n = len(out)
            out.append(pltpu.make_async_remote_copy(
                src_ref=src_block, dst_ref=land.at[here if arrival else there], send_sem=send_sems.at[n],
                recv_sem=recv_sems.at[n], device_id=peer, device_id_type=MESH))
    return out


def _exchange_start(kinds, srcs, lands, after, name, layer=None):
    ns, n = len(srcs), len(srcs) + len(lands)
    sems = sum(COPIES[k] for k in kinds)

    def body(*refs):
        send_sems, recv_sems = refs[n + 1], refs[n + 2]
        for cp in _exchange_copies(kinds, refs[:ns], refs[ns:n], send_sems, recv_sems, layer, arrival=False):
            cp.start()
        refs[-1][...] = jnp.zeros_like(refs[-1])

    thru = [pltpu.HBM(a.shape, a.dtype) for a in (*srcs, *lands)]
    outs = pl.pallas_call(
        body, name=name,
        out_shape=(pltpu.SemaphoreType.DMA((sems,)), pltpu.SemaphoreType.DMA((sems,)), *thru,
                   jax.ShapeDtypeStruct((8, 128), F32)),
        in_specs=[IN_HBM] * n + [ORDER_ONLY],
        out_specs=(IN_SEM, IN_SEM, *[IN_HBM] * n, pl.BlockSpec(memory_space=pltpu.VMEM)),
        input_output_aliases={i: 2 + i for i in range(n)},
        compiler_params=pltpu.CompilerParams(has_side_effects=EFFECT),
    )(*[pltpu.with_memory_space_constraint(a, pltpu.HBM) for a in (*srcs, *lands)], after)
    return kinds, outs[0], outs[1], list(outs[2:2 + ns]), list(outs[2 + ns:2 + n]), outs[-1]


def _exchange_wait(pending, after, name, layer=None):
    kinds, send_sems, recv_sems, srcs, lands, _ = pending
    after = list(after) if isinstance(after, (list, tuple)) else [after]
    ns, n = len(srcs), len(srcs) + len(lands)

    def body(*refs):
        for cp in _exchange_copies(kinds, refs[:ns], refs[ns:n], refs[n], refs[n + 1], layer, arrival=True):
            cp.wait_send()
            cp.wait_recv()

    thru = [pltpu.HBM(a.shape, a.dtype) for a in (*srcs, *lands)]
    outs = pl.pallas_call(
        body, name=name, out_shape=tuple(thru),
        in_specs=[IN_HBM] * n + [IN_SEM, IN_SEM] + [ORDER_ONLY] * len(after),
        out_specs=tuple([IN_HBM] * n),
        input_output_aliases={i: i for i in range(n)},
        compiler_params=pltpu.CompilerParams(has_side_effects=EFFECT),
    )(*srcs, *lands, send_sems, recv_sems, *after)
    return list(outs[ns:])


def _landing(block, me):
    land = lax.empty((N_DEV,) + block.shape, block.dtype)
    return lax.dynamic_update_index_in_dim(land, block, me, 0)


def _adamw(parts, w, m, v, tr):
    groups, rows, cols = w.shape

    def body(p_ref, w_ref, m_ref, v_ref, g_ref, d_ref, nm_ref, nv_ref):
        g = p_ref[0].astype(F32)
        for j in range(1, N_DEV):
            g = g + p_ref[j].astype(F32)
        g_ref[...] = g
        m = ADAM_B1 * m_ref[...] + (1.0 - ADAM_B1) * g
        v = ADAM_B2 * v_ref[...] + (1.0 - ADAM_B2) * (g * g)
        nm_ref[...] = m
        nv_ref[...] = v
        m_hat = m / (1.0 - ADAM_B1 ** ADAM_STEP)
        v_hat = v / (1.0 - ADAM_B2 ** ADAM_STEP)
        d_ref[...] = -ADAM_LR * (m_hat / (jnp.sqrt(v_hat) + ADAM_EPS) + ADAM_WD * w_ref[...])

    blk = pl.BlockSpec((None, tr, cols), lambda g, i: (g, i, 0))
    return pl.pallas_call(
        body, name="adamw", grid=(groups, rows // tr),
        in_specs=[pl.BlockSpec((None, N_DEV, tr, cols), lambda g, i: (g, 0, i, 0)), blk, blk, blk],
        out_specs=[blk] * 4,
        out_shape=[jax.ShapeDtypeStruct((groups, rows, cols), F32)] * 4,
        compiler_params=_cp("parallel", "parallel"),
    )(parts, w, m, v)


def _row_tile(rows):
    for cand in range(min(rows, 512) // 8 * 8, 7, -8):
        if rows % cand == 0:
            return cand
    return rows


def _with_own_block(land, blocks, layer, me):
    own = lax.dynamic_index_in_dim(blocks, me, 0, keepdims=True)[None]
    return lax.dynamic_update_slice(land, own, (layer, me, 0, 0))


PACK_LANES = 128


def _pack_layers(arrs):
    parts = []
    for a in arrs:
        flat = a.reshape(a.shape[0], -1)
        parts.append(jnp.pad(flat, ((0, 0), (0, -flat.shape[1] % (8 * PACK_LANES)))))
    return jnp.concatenate(parts, axis=1).reshape(arrs[0].shape[0], -1, PACK_LANES)


def _unpack_layers(packed, shapes):
    flat = packed.reshape(packed.shape[0], -1)
    outs, off = [], 0
    for shp in shapes:
        size = 1
        for d in shp[1:]:
            size *= d
        outs.append(flat[:, off:off + size].reshape(shp))
        off += size + (-size % (8 * PACK_LANES))
    return outs


def kernel(x, mix_norm_g, w_in, conv_w, conv_b, conv_ln_g, conv_ln_b, sg_ln_g, sg_ln_b, sg_w, sg_b, q_norm_g, k_norm_g, out_norm_g, w_out, ffn_norm_g, w_gate_up, w_down, loss_target, m_mix_norm_g, m_w_in, m_conv_w, m_conv_b, m_conv_ln_g, m_conv_ln_b, m_sg_ln_g, m_sg_ln_b, m_sg_w, m_sg_b, m_q_norm_g, m_k_norm_g, m_out_norm_g, m_w_out, m_ffn_norm_g, m_w_gate_up, m_w_down, v_mix_norm_g, v_w_in, v_conv_w, v_conv_b, v_conv_ln_g, v_conv_ln_b, v_sg_ln_g, v_sg_ln_b, v_sg_w, v_sg_b, v_q_norm_g, v_k_norm_g, v_out_norm_g, v_w_out, v_ffn_norm_g, v_w_gate_up, v_w_down):
    names = SMALL[:1] + LARGE[:1] + SMALL[1:12] + LARGE[1:2] + SMALL[12:] + LARGE[2:]
    w = dict(mix_norm_g=mix_norm_g, w_in=w_in, conv_w=conv_w, conv_b=conv_b, conv_ln_g=conv_ln_g,
             conv_ln_b=conv_ln_b, sg_ln_g=sg_ln_g, sg_ln_b=sg_ln_b, sg_w=sg_w, sg_b=sg_b, q_norm_g=q_norm_g,
             k_norm_g=k_norm_g, out_norm_g=out_norm_g, w_out=w_out, ffn_norm_g=ffn_norm_g,
             w_gate_up=w_gate_up, w_down=w_down)
    m = dict(mix_norm_g=m_mix_norm_g, w_in=m_w_in, conv_w=m_conv_w, conv_b=m_conv_b, conv_ln_g=m_conv_ln_g,
             conv_ln_b=m_conv_ln_b, sg_ln_g=m_sg_ln_g, sg_ln_b=m_sg_ln_b, sg_w=m_sg_w, sg_b=m_sg_b,
             q_norm_g=m_q_norm_g, k_norm_g=m_k_norm_g, out_norm_g=m_out_norm_g, w_out=m_w_out,
             ffn_norm_g=m_ffn_norm_g, w_gate_up=m_w_gate_up, w_down=m_w_down)
    v = dict(mix_norm_g=v_mix_norm_g, w_in=v_w_in, conv_w=v_conv_w, conv_b=v_conv_b, conv_ln_g=v_conv_ln_g,
             conv_ln_b=v_conv_ln_b, sg_ln_g=v_sg_ln_g, sg_ln_b=v_sg_ln_b, sg_w=v_sg_w, sg_b=v_sg_b,
             q_norm_g=v_q_norm_g, k_norm_g=v_k_norm_g, out_norm_g=v_out_norm_g, w_out=v_w_out,
             ffn_norm_g=v_ffn_norm_g, w_gate_up=v_w_gate_up, w_down=v_w_down)
    xpos, ypos, cpos = _position()
    me = _flat(xpos, ypos, cpos)
    conv_cols = conv_w.shape[-1]
    no_token = jnp.zeros((8, 128), F32)
    w, m, v = (dict(t, w_in=jnp.swapaxes(t["w_in"], 1, 2), w_gate_up=jnp.swapaxes(t["w_gate_up"], 1, 2))
               for t in (w, m, v))
    shards = {k: w[k].astype(BF16) for k in LARGE}
    full_shape = dict(w_in=(IN_W, D_MODEL), w_out=(D_MODEL, D_MODEL), w_gate_up=(2 * FFN_H, D_MODEL),
                      w_down=(FFN_H, D_MODEL))

    def gather_start(srcs, after, tag):
        return _exchange_start(["spread_chips"] * len(srcs), srcs, [_landing(a, me) for a in srcs], after,
                               f"gather_start_{tag}")

    def gather_pass(pending, after, tag):
        lands = _exchange_wait(pending, after, f"gather_wait_{tag}")
        return _exchange_start(["pass"] * len(lands), [], lands, after, f"gather_pass_{tag}")

    def gathered(pending, keys, after, tag):
        lands = _exchange_wait(pending, after, f"gather_passed_{tag}")
        return {k: a.reshape(full_shape[k]) for k, a in zip(keys, lands)}, lands[len(keys):]

    first, later = ("w_in",), ("w_out", "w_gate_up", "w_down")
    act = x[0]
    head = gather_start([shards["w_in"][0], w["conv_w"]], act, "0")
    tail = gather_start([shards[k][0] for k in later], head[5], "0_later")
    head = gather_pass(head, tail[5], "0")
    large, (conv_blocks,) = gathered(head, first, head[5], "0")
    conv_full = jnp.transpose(conv_blocks, (1, 2, 0, 3)).reshape(DEPTH, CONV_K, CONV_W)
    small_w = dict({k: w[k] for k in SMALL}, conv_w=conv_full)
    qs, stash = [], []
    for l in range(DEPTH):
        coming = {}
        more = l + 1 < DEPTH

        def next_start(after):
            coming["first"] = gather_start([shards[k][l + 1] for k in LARGE], after, str(l + 1))
            return coming["first"][5]

        def after_in(proj):
            return next_start(proj) if more and l == 0 else no_token

        def after_mixers(y_sb):
            if l > 0:
                return {}, no_token
            passing = gather_pass(tail, y_sb, "0_later")
            return gathered(passing, later, passing[5], "0_later")[0], passing[5]

        def after_out(x_mid):
            if not more or l == 0:
                return no_token
            coming["second"] = gather_pass(coming["first"], x_mid, str(l + 1))
            return coming["second"][5]

        token = next_start(act) if more and l > 0 else no_token
        act, q, st = _layer_fwd(act, _layer_params(small_w, large, l), token, after_in, after_mixers, after_out)
        qs.append(q)
        stash.append(st)
        if more:
            if l == 0:
                coming["second"] = gather_pass(coming["first"], act, str(l + 1))
            large, _ = gathered(coming["second"], LARGE, act, str(l + 1))

    loss, dx = _loss_head(act, loss_target[0])
    loss = lax.psum(loss[0, 0], ("x", "y", "c"))

    replicated = tuple(k for k in SMALL if k != "conv_w")
    small_rows = _pack_layers([w[k][:1] for k in replicated]).shape[1]
    conv_rows = _pack_layers([conv_full[:1]]).shape[1]
    group_a, group_b = ("w_gate_up", "w_down", "w_out"), ("w_in",)
    blocks = lambda k, a: a.reshape((N_DEV,) + w[k].shape[1:])
    land_a = [lax.empty((DEPTH, N_DEV) + w[k].shape[1:], BF16) for k in group_a]
    land_b = [lax.empty((DEPTH, N_DEV) + w[k].shape[1:], BF16) for k in group_b]
    land_b.append(lax.empty((DEPTH, N_DEV, small_rows + conv_rows, PACK_LANES), F32))
    pend_a = pend_b = None
    token = no_token
    for l in reversed(range(DEPTH)):
        dxm, carried, d_wgu, d_wd, d_wo = _layer_bwd_ffn(dx, qs[l], stash[l], token)
        srcs = [blocks(k, a) for k, a in zip(group_a, (d_wgu, d_wd, d_wo))]
        if pend_a is not None:
            land_a = _exchange_wait(pend_a, d_wo, f"grads_a_wait_{l + 1}", layer=l + 1)
        land_a = [_with_own_block(ld, a, l, me) for ld, a in zip(land_a, srcs)]
        pend_a = _exchange_start(["scatter"] * 3, srcs, land_a, dxm, f"grads_a_start_{l}", layer=l)
        dx, d_win, small = _layer_bwd_mix(dxm, carried, qs[l], stash[l], pend_a[5])
        packed = _pack_layers([small[k][None] for k in replicated + ("conv_w",)])[0]
        srcs = [blocks("w_in", d_win)]
        if pend_b is not None:
            land_b = _exchange_wait(pend_b, d_win, f"grads_b_wait_{l + 1}", layer=l + 1)
        land_b = [_with_own_block(land_b[0], srcs[0], l, me),
                  lax.dynamic_update_slice(land_b[1], packed[None, None], (l, me, 0, 0))]
        pend_b = _exchange_start(["scatter", "spread"], srcs + [packed], land_b, dx, f"grads_b_start_{l}",
                                 layer=l)
        token = pend_b[5]

    land_a = _exchange_wait(pend_a, token, "grads_a_wait_0", layer=0)
    res = {}
    for k, parts in zip(group_a, land_a):
        res[k] = _adamw(parts, w[k], m[k], v[k], _row_tile(w[k].shape[1]))
    land_b = _exchange_wait(pend_b, [res[k][0] for k in group_a], "grads_b_wait_0", layer=0)
    res["w_in"] = _adamw(land_b[0], w["w_in"], m["w_in"], v["w_in"], _row_tile(w["w_in"].shape[1]))
    small_parts = land_b[1]
    updated = _adamw(small_parts, *(_pack_layers([t[k] for k in replicated]) for t in (w, m, v)), small_rows)
    unpacked = [_unpack_layers(o, [w[k].shape for k in replicated]) for o in updated]
    res.update({k: [u[i] for u in unpacked] for i, k in enumerate(replicated)})
    conv_parts = small_parts[:, :, small_rows:].reshape(DEPTH, N_DEV, -1)[:, :, :CONV_K * CONV_W]
    conv_parts = lax.dynamic_slice_in_dim(conv_parts.reshape(DEPTH, N_DEV, CONV_K, CONV_W), me * conv_cols,
                                          conv_cols, axis=3)
    res["conv_w"] = _adamw(conv_parts, w["conv_w"], m["conv_w"], v["conv_w"], CONV_K)
    for k in ("w_in", "w_gate_up"):
        res[k] = [jnp.swapaxes(a, 1, 2) for a in res[k]]

    return (loss, dx[None], *[res[k][0] for k in names], *[res[k][1] for k in names],
            *[res[k][2] for k in names], *[res[k][3] for k in names])
```

```python
import jax
import jax.numpy as jnp
from jax import lax
from jax.experimental import pallas as pl
from jax.experimental.pallas import tpu as pltpu

F32 = jnp.float32
BF16 = jnp.bfloat16

D_MODEL = 1024
DEPTH = 4
HEAD_DIM = 64
CONV_W = 256
SG_W = 256
SB_W = 512
IN_W = 2560
FFN_H = 2816
CONV_K = 31
CHUNK = 128
OFF_SG = 2 * CONV_W
OFF_SB = OFF_SG + 2 * SG_W
RMS_EPS = 1e-6
LN_EPS = 1e-5
N_DEV = 8
MESH = pl.DeviceIdType.MESH

ADAM_LR = 0.001
ADAM_B1 = 0.9
ADAM_B2 = 0.999
ADAM_EPS = 1e-08
ADAM_WD = 0.01
ADAM_STEP = 10

TOKEN_TILE = 512
FFN_BWD_TILE = 256
LIGHT_TILE = 1024
VMEM_LIMIT = 56 * 1024 * 1024


def _cp(*sem):
    return pltpu.CompilerParams(dimension_semantics=sem or None, vmem_limit_bytes=VMEM_LIMIT)


def _dot(a, b):
    return jnp.dot(a, b, preferred_element_type=F32)


def _dot_nt(a, b):
    return lax.dot_general(a, b, (((1,), (1,)), ((), ())), preferred_element_type=F32)


def _dot_tn(a, b):
    return lax.dot_general(a, b, (((0,), (0,)), ((), ())), preferred_element_type=F32)


def _dot_split(x, m):
    hi = x.astype(BF16)
    lo = (x - hi.astype(F32)).astype(BF16)
    return _dot(hi, m) + _dot(lo, m)


def _group_mean_matrix(width, group):
    r = lax.broadcasted_iota(jnp.int32, (width, width), 0) // group
    c = lax.broadcasted_iota(jnp.int32, (width, width), 1) // group
    return jnp.where(r == c, 1.0 / group, 0.0).astype(BF16)


def _sigmoid(x):
    return 1.0 / (1.0 + jnp.exp(-x))


def _gelu(x):
    return 0.5 * x * (1.0 + lax.erf(x * (2.0 ** -0.5)))


def _gelu_grad(x):
    return 0.5 * (1.0 + lax.erf(x * (2.0 ** -0.5))) + x * jnp.exp(-0.5 * x * x) * (0.5 * (2.0 / jnp.pi) ** 0.5)


def _rms_stats(x):
    r = lax.rsqrt(jnp.mean(x * x, axis=-1, keepdims=True) + RMS_EPS)
    return r, x * r


def _rms_bwd(xh, r, g, dy):
    dxh = dy * g
    dx = r * (dxh - xh * jnp.mean(dxh * xh, axis=-1, keepdims=True))
    return dx, dy * xh


def _ln_stats(x):
    mu = jnp.mean(x, axis=-1, keepdims=True)
    xc = x - mu
    r = lax.rsqrt(jnp.mean(xc * xc, axis=-1, keepdims=True) + LN_EPS)
    return r, xc * r


def _ln_bwd(xh, r, g, dy):
    dxh = dy * g
    return r * (dxh - jnp.mean(dxh, axis=-1, keepdims=True) - xh * jnp.mean(dxh * xh, axis=-1, keepdims=True))


def _colsum(x):
    return jnp.sum(x, axis=0, keepdims=True)


def _rows(tm, n, j=0):
    return pl.BlockSpec((tm, n), lambda i: (i, j))


def _whole(shape):
    return pl.BlockSpec(shape, lambda i: (0,) * len(shape))


ORDER_ONLY = pl.BlockSpec(memory_space=pl.ANY)


def _stack_heads(a):
    even = (lax.broadcasted_iota(jnp.int32, a.shape, 1) % (2 * HEAD_DIM)) < HEAD_DIM
    top = jnp.where(even, a, 0.0)
    bot = jnp.where(even, 0.0, a)
    parts = []
    for c in range(a.shape[0] // CHUNK):
        rows = slice(c * CHUNK, (c + 1) * CHUNK)
        parts += [top[rows], bot[rows]]
    return jnp.concatenate(parts, axis=0)


def _store_stacked(st, st_ref, tr_ref):
    st_ref[...] = st.astype(BF16)
    for p in range(SB_W // CHUNK):
        for c in range(st.shape[0] // (2 * CHUNK)):
            tile = st[2 * c * CHUNK:2 * (c + 1) * CHUNK, p * CHUNK:(p + 1) * CHUNK]
            tr_ref[p, c] = tile.T.astype(BF16)


def _load_transposed(tr_ref):
    rows = []
    for c in range(tr_ref.shape[1]):
        tiles = [tr_ref[p, c].T for p in range(SB_W // CHUNK)]
        rows.append(jnp.concatenate(tiles, axis=1))
    return jnp.concatenate(rows, axis=0)


def _unstack_heads(st):
    even = (lax.broadcasted_iota(jnp.int32, (CHUNK, st.shape[1]), 1) % (2 * HEAD_DIM)) < HEAD_DIM
    parts = []
    for c in range(st.shape[0] // (2 * CHUNK)):
        top = st[2 * c * CHUNK:(2 * c + 1) * CHUNK]
        bot = st[(2 * c + 1) * CHUNK:(2 * c + 2) * CHUNK]
        parts.append(jnp.where(even, top, bot))
    return jnp.concatenate(parts, axis=0)


def _fwd_in(x, g, w, qg, kg, tok):
    s = x.shape[0]
    tm = TOKEN_TILE

    def body(x_ref, g_ref, w_ref, qg_ref, kg_ref, tok_ref, proj_ref, qn_ref, kn_ref, vb_ref, kt_ref, vt_ref):
        r, xh = _rms_stats(x_ref[...])
        h = (xh * g_ref[...]).astype(BF16)
        proj = _dot_nt(h, w_ref[...])
        proj_ref[...] = proj
        gm = _group_mean_matrix(SB_W, HEAD_DIM)
        q = proj[:, OFF_SB:OFF_SB + SB_W]
        k = proj[:, OFF_SB + SB_W:OFF_SB + 2 * SB_W]
        rq = lax.rsqrt(_dot_split(q * q, gm) + RMS_EPS)
        rk = lax.rsqrt(_dot_split(k * k, gm) + RMS_EPS)
        qn_ref[...] = (q * rq * qg_ref[...] * (HEAD_DIM ** -0.5)).astype(BF16)
        _store_stacked(_stack_heads(k * rk * kg_ref[...]), kn_ref, kt_ref)
        _store_stacked(_stack_heads(proj[:, OFF_SB + 2 * SB_W:]), vb_ref, vt_ref)

    tiles = pl.BlockSpec((SB_W // CHUNK, tm // CHUNK, CHUNK, PAIR), lambda i: (0, i, 0, 0))
    tiles_shape = jax.ShapeDtypeStruct((SB_W // CHUNK, s // CHUNK, CHUNK, PAIR), BF16)
    return pl.pallas_call(
        body, name="fwd_in", grid=(s // tm,),
        in_specs=[_rows(tm, D_MODEL), _whole((1, D_MODEL)), _whole((IN_W, D_MODEL)),
                  _whole((1, SB_W)), _whole((1, SB_W)), ORDER_ONLY],
        out_specs=[_rows(tm, IN_W), _rows(tm, SB_W), _rows(2 * tm, SB_W), _rows(2 * tm, SB_W), tiles, tiles],
        out_shape=[jax.ShapeDtypeStruct((s, IN_W), F32), jax.ShapeDtypeStruct((s, SB_W), BF16),
                   jax.ShapeDtypeStruct((2 * s, SB_W), BF16), jax.ShapeDtypeStruct((2 * s, SB_W), BF16),
                   tiles_shape, tiles_shape],
        compiler_params=_cp("parallel"),
    )(x, g, w, qg, kg, tok)


SUBLANES = 8


def _shifted(win, back):
    n = win.shape[0]
    turned = [win] + [pltpu.roll(win, b if back else n - b, axis=0) for b in range(1, SUBLANES)]

    def shifted(k):
        whole, part = divmod(k, SUBLANES)
        start = 32 - whole * SUBLANES if back else whole * SUBLANES
        return turned[part][start:start + CHUNK, :]

    return shifted


def _conv_window(abuf, r0, w_ref):
    shifted = _shifted(abuf[pl.ds(pl.multiple_of(r0 + CHUNK - 32, 32), CHUNK + 32), :], back=True)
    acc = jnp.zeros((CHUNK, CONV_W), F32)
    for k in range(CONV_K):
        acc = acc + shifted(k) * w_ref[CONV_K - 1 - k:CONV_K - k, :]
    return acc, shifted


def _glu_fill(p_ref, abuf, s):
    abuf[0:CHUNK, :] = jnp.zeros((CHUNK, CONV_W), F32)

    def fill(c, carry):
        r0 = pl.multiple_of(c * CHUNK, CHUNK)
        pv = p_ref[pl.ds(r0, CHUNK), :]
        abuf[pl.ds(r0 + CHUNK, CHUNK), :] = pv[:, :CONV_W] * _sigmoid(pv[:, CONV_W:])
        return carry

    lax.fori_loop(0, s // CHUNK, fill, 0)


def _fwd_conv(proj, w, b, lg, lb, tok):
    s = proj.shape[0]

    def body(p_ref, w_ref, b_ref, lg_ref, lb_ref, tok_ref, y_ref, abuf):
        _glu_fill(p_ref, abuf, s)

        def chunk(c, carry):
            r0 = pl.multiple_of(c * CHUNK, CHUNK)
            acc, _ = _conv_window(abuf, r0, w_ref)
            r, xh = _ln_stats(acc + b_ref[...])
            ln = xh * lg_ref[...] + lb_ref[...]
            y_ref[pl.ds(r0, CHUNK), :] = ln * _sigmoid(ln)
            return carry

        lax.fori_loop(0, s // CHUNK, chunk, 0)

    return pl.pallas_call(
        body, name="fwd_conv", grid=(1,),
        in_specs=[pl.BlockSpec((s, 2 * CONV_W), lambda i: (0, 0)), _whole((CONV_K, CONV_W)),
                  _whole((1, CONV_W)), _whole((1, CONV_W)), _whole((1, CONV_W)), ORDER_ONLY],
        out_specs=_whole((s, CONV_W)),
        out_shape=jax.ShapeDtypeStruct((s, CONV_W), F32),
        scratch_shapes=[pltpu.VMEM((s + CHUNK, CONV_W), F32)],
        compiler_params=_cp("arbitrary"),
    )(proj, w, b, lg, lb, tok)


def _sg_masks():
    row = lax.broadcasted_iota(jnp.int32, (CHUNK, CHUNK), 0)
    col = lax.broadcasted_iota(jnp.int32, (CHUNK, CHUNK), 1)
    lane_head = lax.broadcasted_iota(jnp.int32, (CHUNK, SG_W), 1) // HEAD_DIM
    return row >= col, lane_head


def _sg_mix(w_ref, bias_ref, vc, tril, lane_head):
    mixed = bias_ref[...]
    for h in range(SG_W // HEAD_DIM):
        wm = jnp.where(tril, w_ref[h], 0.0).astype(BF16)
        mixed = mixed + jnp.where(lane_head == h, _dot(wm, vc), 0.0)
    return mixed


def _fwd_sg(proj, lg, lb, w, bias):
    s = proj.shape[0]
    tm = LIGHT_TILE

    def body(p_ref, lg_ref, lb_ref, w_ref, bias_ref, y_ref):
        ge = _gelu(p_ref[...])
        u = ge[:, :SG_W]
        r, xh = _ln_stats(ge[:, SG_W:])
        vln = (xh * lg_ref[...] + lb_ref[...]).astype(BF16)
        tril, lane_head = _sg_masks()
        for c in range(tm // CHUNK):
            rows = slice(c * CHUNK, (c + 1) * CHUNK)
            y_ref[rows, :] = u[rows] * _sg_mix(w_ref, bias_ref, vln[rows], tril, lane_head)

    return pl.pallas_call(
        body, name="fwd_sg", grid=(s // tm,),
        in_specs=[_rows(tm, 2 * SG_W, 1), _whole((1, SG_W)), _whole((1, SG_W)),
                  _whole((SG_W // HEAD_DIM, CHUNK, CHUNK)), _whole((CHUNK, SG_W))],
        out_specs=_rows(tm, SG_W),
        out_shape=jax.ShapeDtypeStruct((s, SG_W), F32),
        compiler_params=_cp("parallel"),
    )(proj, lg, lb, w, bias)


SB_Q = 2 * CHUNK
PAIR = 2 * CHUNK
SB_PAIRS = 2


def _pair_tri(kind):
    row = lax.broadcasted_iota(jnp.int32, (PAIR, PAIR), 0)
    col = lax.broadcasted_iota(jnp.int32, (PAIR, PAIR), 1)
    tri = {"after": row > col, "upto": row <= col, "before": row < col}[kind]
    return jnp.where(((row // CHUNK) == (col // CHUNK)) & tri, 1.0, 0.0).astype(BF16)


def _sb_scores(z, qpos0, kpos0, masked):
    sp = jnp.maximum(z, 0.0) + jnp.log(1.0 + jnp.exp(-jnp.abs(z)))
    if not masked:
        return z, sp, sp.astype(BF16), None
    row = lax.broadcasted_iota(jnp.int32, z.shape, 0)
    col = lax.broadcasted_iota(jnp.int32, z.shape, 1) % CHUNK
    mask = (kpos0 + col) < (qpos0 + row)
    return z, sp, jnp.where(mask, sp, 0.0).astype(BF16), mask


def _per_head(c0, c1):
    return jnp.concatenate([jnp.broadcast_to(c0, (SB_Q, CHUNK)), jnp.broadcast_to(c1, (SB_Q, CHUNK))], axis=1)


def _fwd_sb(qn, ktr, vst):
    s = qn.shape[0]
    np_ = SB_PAIRS

    def body(q_ref, k_ref, v_ref, after_ref, o_ref, lt_ref, z_buf, att_buf):
        i = pl.program_id(1)
        first = i * (SB_Q // CHUNK)
        last = first + SB_Q // CHUNK - 1
        after = after_ref[...]
        lanes = [slice(pr * CHUNK, (pr + 1) * CHUNK) for pr in range(np_)]
        qs = [q_ref[:, lanes[pr]] for pr in range(np_)]

        def rows(kb):
            return pl.ds(pl.multiple_of(kb * PAIR, PAIR), PAIR)

        def block(kb, carry, masked):
            out = []
            for pr in range(np_):
                acc, c0, c1 = carry[pr]
                z_next = _dot(qs[pr], k_ref[pr, jnp.maximum(kb - 1, 0)])
                pv = _dot(att_buf[pr], v_ref[rows(jnp.minimum(kb + 1, last)), lanes[pr]])
                z, sp, nlb, mask = _sb_scores(z_buf[pr], i * SB_Q, kb * CHUNK, masked)
                loc = _dot(nlb, after)
                att = jnp.exp(z - sp - loc - _per_head(c0, c1))
                if masked:
                    att = jnp.where(mask, att, 0.0)
                z_buf[pr] = z_next
                att_buf[pr] = att.astype(BF16)
                out.append((acc + pv, c0 + loc[:, 0:1] + nlb[:, 0:1].astype(F32),
                            c1 + loc[:, CHUNK:CHUNK + 1] + nlb[:, CHUNK:CHUNK + 1].astype(F32)))
            return tuple(out)

        for pr in range(np_):
            z_buf[pr] = _dot(qs[pr], k_ref[pr, last])
        att_buf[...] = jnp.zeros_like(att_buf)
        zero = jnp.zeros((SB_Q, 1), F32)
        carry = ((jnp.zeros((SB_Q, CHUNK), F32), zero, zero),) * np_
        for back in range(SB_Q // CHUNK):
            carry = block(last - back, carry, True)
        carry = lax.fori_loop(0, first, lambda j, c: block(first - 1 - j, c, False), carry)
        for pr, (acc, c0, c1) in enumerate(carry):
            o_ref[:, lanes[pr]] = acc + _dot(att_buf[pr], v_ref[rows(0), lanes[pr]])
            lt_ref[:, lanes[pr]] = jnp.concatenate([jnp.broadcast_to(c0, (SB_Q, HEAD_DIM)),
                                                    jnp.broadcast_to(c1, (SB_Q, HEAD_DIM))], axis=1)

    blk = pl.BlockSpec((SB_Q, np_ * CHUNK), lambda p, i: (i, p))
    seq = pl.BlockSpec((2 * s, np_ * CHUNK), lambda p, i: (0, p))
    return pl.pallas_call(
        body, name="fwd_sb", grid=(SB_W // CHUNK // np_, s // SB_Q),
        in_specs=[blk, pl.BlockSpec((np_, s // CHUNK, CHUNK, PAIR), lambda p, i: (p, 0, 0, 0)), seq,
                  pl.BlockSpec((PAIR, PAIR), lambda p, i: (0, 0))],
        out_specs=[blk, blk],
        out_shape=[jax.ShapeDtypeStruct((s, SB_W), F32)] * 2,
        scratch_shapes=[pltpu.VMEM((np_, SB_Q, PAIR), F32), pltpu.VMEM((np_, SB_Q, PAIR), BF16)],
        compiler_params=_cp("parallel", "parallel"),
    )(qn, ktr, vst, _pair_tri("after"))


def _group_norms(yc, ys, yb):
    return [_rms_stats(yc), _rms_stats(ys), _rms_stats(yb)]


def _fwd_out(yc, ys, yb, g, w, x, tok):
    s = x.shape[0]
    tm = LIGHT_TILE

    def body(yc_ref, ys_ref, yb_ref, g_ref, w_ref, x_ref, tok_ref, o_ref):
        stats = _group_norms(yc_ref[...], ys_ref[...], yb_ref[...])
        cat = jnp.concatenate([xh for _, xh in stats], axis=1) * g_ref[...]
        o_ref[...] = x_ref[...] + _dot(cat.astype(BF16), w_ref[...])

    return pl.pallas_call(
        body, name="fwd_out", grid=(s // tm,),
        in_specs=[_rows(tm, CONV_W), _rows(tm, SG_W), _rows(tm, SB_W), _whole((1, D_MODEL)),
                  _whole((D_MODEL, D_MODEL)), _rows(tm, D_MODEL), ORDER_ONLY],
        out_specs=_rows(tm, D_MODEL),
        out_shape=jax.ShapeDtypeStruct((s, D_MODEL), F32),
        compiler_params=_cp("parallel"),
    )(yc, ys, yb, g, w, x, tok)


def _fwd_ffn(x, g, wgu, wd, tok):
    s = x.shape[0]
    tm = TOKEN_TILE

    def body(x_ref, g_ref, wgu_ref, wd_ref, tok_ref, gu_ref, o_ref):
        x = x_ref[...]
        r, xh = _rms_stats(x)
        gu = _dot_nt((xh * g_ref[...]).astype(BF16), wgu_ref[...])
        gu_ref[...] = gu
        gate = gu[:, :FFN_H]
        act = gate * _sigmoid(gate) * gu[:, FFN_H:]
        o_ref[...] = x + _dot(act.astype(BF16), wd_ref[...])

    return pl.pallas_call(
        body, name="fwd_ffn", grid=(s // tm,),
        in_specs=[_rows(tm, D_MODEL), _whole((1, D_MODEL)),
                  pl.BlockSpec((2 * FFN_H, D_MODEL), lambda i: (0, 0), pipeline_mode=pl.Buffered(1)),
                  pl.BlockSpec((FFN_H, D_MODEL), lambda i: (0, 0), pipeline_mode=pl.Buffered(1)), ORDER_ONLY],
        out_specs=[_rows(tm, 2 * FFN_H), _rows(tm, D_MODEL)],
        out_shape=[jax.ShapeDtypeStruct((s, 2 * FFN_H), F32), jax.ShapeDtypeStruct((s, D_MODEL), F32)],
        compiler_params=_cp("parallel"),
    )(x, g, wgu, wd, tok)


def _loss_head(y, target):
    s = y.shape[0]
    tm = LIGHT_TILE

    def body(y_ref, t_ref, l_ref, d_ref):
        @pl.when(pl.program_id(0) == 0)
        def _():
            l_ref[...] = jnp.zeros_like(l_ref)

        err = y_ref[...] - t_ref[...]
        d_ref[...] = err * (1.0 / D_MODEL)
        l_ref[...] += 0.5 * jnp.sum(jnp.mean(err * err, axis=-1, keepdims=True), axis=0, keepdims=True)

    return pl.pallas_call(
        body, name="loss_head", grid=(s // tm,),
        in_specs=[_rows(tm, D_MODEL), _rows(tm, D_MODEL)],
        out_specs=[_whole((1, 1)), _rows(tm, D_MODEL)],
        out_shape=[jax.ShapeDtypeStruct((1, 1), F32), jax.ShapeDtypeStruct((s, D_MODEL), F32)],
        compiler_params=_cp("arbitrary"),
    )(y, target)


def _accumulate(ref, value):
    @pl.when(pl.program_id(0) == 0)
    def _():
        ref[...] = jnp.zeros_like(ref)

    ref[...] += value


def _bwd_ffn(dxo, gu, xm, g, wgu, wd, tok):
    s = dxo.shape[0]
    tm = FFN_BWD_TILE

    def body(dxo_ref, gu_ref, xm_ref, g_ref, wgu_ref, wd_ref, tok_ref, dgu_ref, act_ref, h_ref, dxm_ref, dg_ref):
        dxo = dxo_ref[...]
        gu = gu_ref[...]
        gate, up = gu[:, :FFN_H], gu[:, FFN_H:]
        sg = _sigmoid(gate)
        sl = gate * sg
        act_ref[...] = (sl * up).astype(BF16)
        dact = _dot_nt(dxo.astype(BF16), wd_ref[...])
        dgate = dact * up * (sg * (1.0 + gate * (1.0 - sg)))
        dgu = jnp.concatenate([dgate, dact * sl], axis=1).astype(BF16)
        dgu_ref[...] = dgu
        dh = _dot(dgu, wgu_ref[...])
        r, xh = _rms_stats(xm_ref[...])
        h_ref[...] = (xh * g_ref[...]).astype(BF16)
        dx, dgrow = _rms_bwd(xh, r, g_ref[...], dh)
        dxm_ref[...] = dxo + dx
        _accumulate(dg_ref, _colsum(dgrow))

    return pl.pallas_call(
        body, name="bwd_ffn", grid=(s // tm,),
        in_specs=[_rows(tm, D_MODEL), _rows(tm, 2 * FFN_H), _rows(tm, D_MODEL), _whole((1, D_MODEL)),
                  pl.BlockSpec((2 * FFN_H, D_MODEL), lambda i: (0, 0), pipeline_mode=pl.Buffered(1)),
                  pl.BlockSpec((FFN_H, D_MODEL), lambda i: (0, 0), pipeline_mode=pl.Buffered(1)), ORDER_ONLY],
        out_specs=[_rows(tm, 2 * FFN_H), _rows(tm, FFN_H), _rows(tm, D_MODEL), _rows(tm, D_MODEL),
                   _whole((1, D_MODEL))],
        out_shape=[jax.ShapeDtypeStruct((s, 2 * FFN_H), BF16), jax.ShapeDtypeStruct((s, FFN_H), BF16),
                   jax.ShapeDtypeStruct((s, D_MODEL), BF16), jax.ShapeDtypeStruct((s, D_MODEL), F32),
                   jax.ShapeDtypeStruct((1, D_MODEL), F32)],
        compiler_params=_cp("arbitrary"),
    )(dxo, gu, xm, g, wgu, wd, tok)


def _matmul_tn(a, b, tm, tn, out_dtype=BF16):
    s, m = a.shape
    n = b.shape[1]

    def body(a_ref, b_ref, o_ref):
        o_ref[...] = _dot_tn(a_ref[...].astype(BF16), b_ref[...].astype(BF16)).astype(out_dtype)

    return pl.pallas_call(
        body, name="weight_grad", grid=(m // tm, n // tn),
        in_specs=[pl.BlockSpec((s, tm), lambda i, j: (0, i)), pl.BlockSpec((s, tn), lambda i, j: (0, j))],
        out_specs=pl.BlockSpec((tm, tn), lambda i, j: (i, j)),
        out_shape=jax.ShapeDtypeStruct((m, n), out_dtype),
        compiler_params=_cp("parallel", "parallel"),
    )(a, b)


def _bwd_out(dxm, yc, ys, yb, g, w, tok):
    s = dxm.shape[0]
    tm = LIGHT_TILE

    def body(dxm_ref, yc_ref, ys_ref, yb_ref, g_ref, w_ref, tok_ref, dyc_ref, dys_ref, dyb_ref, cat_ref, dg_ref):
        stats = _group_norms(yc_ref[...], ys_ref[...], yb_ref[...])
        g = g_ref[...]
        cat_ref[...] = (jnp.concatenate([xh for _, xh in stats], axis=1) * g).astype(BF16)
        dcat = _dot_nt(dxm_ref[...].astype(BF16), w_ref[...])
        dgs = []
        off = 0
        for (r, xh), out in zip(stats, (dyc_ref, dys_ref, dyb_ref)):
            cols = slice(off, off + xh.shape[1])
            dx, dgrow = _rms_bwd(xh, r, g[:, cols], dcat[:, cols])
            out[...] = dx
            dgs.append(_colsum(dgrow))
            off += xh.shape[1]
        _accumulate(dg_ref, jnp.concatenate(dgs, axis=1))

    return pl.pallas_call(
        body, name="bwd_out", grid=(s // tm,),
        in_specs=[_rows(tm, D_MODEL), _rows(tm, CONV_W), _rows(tm, SG_W), _rows(tm, SB_W),
                  _whole((1, D_MODEL)), _whole((D_MODEL, D_MODEL)), ORDER_ONLY],
        out_specs=[_rows(tm, CONV_W), _rows(tm, SG_W), _rows(tm, SB_W), _rows(tm, D_MODEL),
                   _whole((1, D_MODEL))],
        out_shape=[jax.ShapeDtypeStruct((s, CONV_W), F32), jax.ShapeDtypeStruct((s, SG_W), F32),
                   jax.ShapeDtypeStruct((s, SB_W), F32), jax.ShapeDtypeStruct((s, D_MODEL), BF16),
                   jax.ShapeDtypeStruct((1, D_MODEL), F32)],
        compiler_params=_cp("arbitrary"),
    )(dxm, yc, ys, yb, g, w, tok)


def _bwd_sb(qn, kst, ktr, vtr, dy, ltot, tok):
    s = qn.shape[0]
    np_ = SB_PAIRS

    def body(q_ref, k_ref, kt_ref, vt_ref, do_ref, lt_ref, upto_ref, before_ref, tok_ref, dq_ref, dk_ref, dv_ref,
             z_buf, da_buf, dz_buf, att_buf):
        i = pl.program_id(1)
        first = i * (SB_Q // CHUNK)
        last = first + SB_Q // CHUNK - 1

        @pl.when(i == 0)
        def _():
            dk_ref[...] = jnp.zeros_like(dk_ref)
            dv_ref[...] = jnp.zeros_like(dv_ref)

        lanes = [slice(pr * CHUNK, (pr + 1) * CHUNK) for pr in range(np_)]
        qs = [q_ref[:, lanes[pr]] for pr in range(np_)]
        dos = [do_ref[:, lanes[pr]] for pr in range(np_)]
        dobs = [do.astype(BF16) for do in dos]
        q_ts = [q.astype(F32).T.astype(BF16) for q in qs]
        do_ts = [do.T.astype(BF16) for do in dos]
        ltots = [_per_head(lt_ref[:, pr * CHUNK:pr * CHUNK + 1],
                           lt_ref[:, pr * CHUNK + HEAD_DIM:pr * CHUNK + HEAD_DIM + 1]) for pr in range(np_)]
        upto = upto_ref[...]
        before = before_ref[...]
        last0, last1 = slice(CHUNK - 1, CHUNK), slice(PAIR - 1, PAIR)

        def rows(kb):
            return pl.ds(pl.multiple_of(kb * PAIR, PAIR), PAIR)

        def ahead(pr, kb):
            return _dot(qs[pr], kt_ref[pr, kb]), _dot(dobs[pr], vt_ref[pr, kb])

        def behind(pr, kb, dq):
            dzb = dz_buf[pr]
            dk_ref[pr, kb] += _dot(q_ts[pr], dzb)
            dv_ref[pr, kb] += _dot(do_ts[pr], att_buf[pr])
            return dq + _dot(dzb, k_ref[rows(kb), lanes[pr]])

        def block(kb, carry, masked):
            out = []
            for pr in range(np_):
                dq, p0, p1, e0, e1 = carry[pr]
                z_next, da_next = ahead(pr, jnp.minimum(kb + 1, last))
                dq = behind(pr, jnp.maximum(kb - 1, 0), dq)
                z, sp, nlb, mask = _sb_scores(z_buf[pr], i * SB_Q, kb * CHUNK, masked)
                pin = _dot(nlb, upto) + _per_head(p0, p1)
                sig = jnp.exp(z - sp)
                att = jnp.exp(z - sp - (ltots[pr] - pin))
                if masked:
                    att = jnp.where(mask, att, 0.0)
                e = att * da_buf[pr]
                ebefore = _dot(e.astype(BF16), before) + _per_head(e0, e1)
                dz = e - sig * (e + ebefore)
                if masked:
                    dz = jnp.where(mask, dz, 0.0)
                z_buf[pr] = z_next
                da_buf[pr] = da_next
                dz_buf[pr] = dz.astype(BF16)
                att_buf[pr] = att.astype(BF16)
                out.append((dq, pin[:, last0], pin[:, last1],
                            ebefore[:, last0] + e[:, last0], ebefore[:, last1] + e[:, last1]))
            return tuple(out)

        for pr in range(np_):
            z_buf[pr], da_buf[pr] = ahead(pr, 0)
        dz_buf[...] = jnp.zeros_like(dz_buf)
        att_buf[...] = jnp.zeros_like(att_buf)
        zero = jnp.zeros((SB_Q, 1), F32)
        carry = ((jnp.zeros((SB_Q, CHUNK), F32), zero, zero, zero, zero),) * np_
        carry = lax.fori_loop(0, first, lambda kb, c: block(kb, c, False), carry)
        for ahead_of in range(SB_Q // CHUNK):
            carry = block(first + ahead_of, carry, True)
        for pr in range(np_):
            dq_ref[:, lanes[pr]] = behind(pr, last, carry[pr][0])

    blk = pl.BlockSpec((SB_Q, np_ * CHUNK), lambda p, i: (i, p))
    seq = pl.BlockSpec((2 * s, np_ * CHUNK), lambda p, i: (0, p))
    tiles = pl.BlockSpec((np_, s // CHUNK, CHUNK, PAIR), lambda p, i: (p, 0, 0, 0))
    tri = pl.BlockSpec((PAIR, PAIR), lambda p, i: (0, 0))
    return pl.pallas_call(
        body, name="bwd_sb", grid=(SB_W // CHUNK // np_, s // SB_Q),
        in_specs=[blk, seq, tiles, tiles, blk, blk, tri, tri, ORDER_ONLY],
        out_specs=[blk, tiles, tiles],
        out_shape=[jax.ShapeDtypeStruct((s, SB_W), F32)]
        + [jax.ShapeDtypeStruct((SB_W // CHUNK, s // CHUNK, CHUNK, PAIR), F32)] * 2,
        scratch_shapes=[pltpu.VMEM((np_, SB_Q, PAIR), F32), pltpu.VMEM((np_, SB_Q, PAIR), F32),
                        pltpu.VMEM((np_, SB_Q, PAIR), BF16), pltpu.VMEM((np_, SB_Q, PAIR), BF16)],
        compiler_params=_cp("parallel", "arbitrary"),
    )(qn, kst, ktr, vtr, dy, ltot, _pair_tri("upto"), _pair_tri("before"), tok)


def _head_sum(row):
    acc = row[:, 0:HEAD_DIM]
    for h in range(1, SB_W // HEAD_DIM):
        acc = acc + row[:, h * HEAD_DIM:(h + 1) * HEAD_DIM]
    return acc


def _bwd_qk(proj, dqs, dkn, dv, qg, kg):
    s = proj.shape[0]
    tm = LIGHT_TILE
    tiles = pl.BlockSpec((SB_W // CHUNK, tm // CHUNK, CHUNK, PAIR), lambda i: (0, i, 0, 0))

    def body(q_ref, k_ref, dqs_ref, dkn_ref, dv_ref, qg_ref, kg_ref, dp_ref, dqg_ref, dkg_ref, qacc, kacc):
        i = pl.program_id(0)
        gm = _group_mean_matrix(SB_W, HEAD_DIM)

        def one(x, dy, g, acc):
            r = lax.rsqrt(_dot_split(x * x, gm) + RMS_EPS)
            xh = x * r
            dxh = dy * g
            _accumulate(acc, _colsum(dy * xh))
            return r * (dxh - xh * _dot_split(dxh * xh, gm))

        dq = one(q_ref[...], dqs_ref[...] * (HEAD_DIM ** -0.5), qg_ref[...], qacc)
        dk = one(k_ref[...], _unstack_heads(_load_transposed(dkn_ref)), kg_ref[...], kacc)
        dp_ref[...] = jnp.concatenate([dq, dk, _unstack_heads(_load_transposed(dv_ref))], axis=1).astype(BF16)

        @pl.when(i == pl.num_programs(0) - 1)
        def _():
            dqg_ref[...] = _head_sum(qacc[...])
            dkg_ref[...] = _head_sum(kacc[...])

    return pl.pallas_call(
        body, name="bwd_qk", grid=(s // tm,),
        in_specs=[_rows(tm, SB_W, OFF_SB // SB_W), _rows(tm, SB_W, OFF_SB // SB_W + 1),
                  _rows(tm, SB_W), tiles, tiles, _whole((1, SB_W)), _whole((1, SB_W))],
        out_specs=[_rows(tm, 3 * SB_W), _whole((1, HEAD_DIM)), _whole((1, HEAD_DIM))],
        out_shape=[jax.ShapeDtypeStruct((s, 3 * SB_W), BF16), jax.ShapeDtypeStruct((1, HEAD_DIM), F32),
                   jax.ShapeDtypeStruct((1, HEAD_DIM), F32)],
        scratch_shapes=[pltpu.VMEM((1, SB_W), F32), pltpu.VMEM((1, SB_W), F32)],
        compiler_params=_cp("arbitrary"),
    )(proj, proj, dqs, dkn, dv, qg, kg)


def _bwd_sg(proj, dy, lg, lb, w, bias):
    s = proj.shape[0]
    tm = LIGHT_TILE
    nh = SG_W // HEAD_DIM

    def body(p_ref, dy_ref, lg_ref, lb_ref, w_ref, bias_ref, dp_ref, dlg_ref, dlb_ref, dw_ref, db_ref, dbias):
        i = pl.program_id(0)
        uv = p_ref[...]
        ge = _gelu(uv)
        u = ge[:, :SG_W]
        r, xh = _ln_stats(ge[:, SG_W:])
        vln = (xh * lg_ref[...] + lb_ref[...]).astype(BF16)
        dy = dy_ref[...]
        tril, lane_head = _sg_masks()

        @pl.when(i == 0)
        def _():
            dw_ref[...] = jnp.zeros_like(dw_ref)
            dbias[...] = jnp.zeros_like(dbias)

        dus, dvlns = [], []
        for c in range(tm // CHUNK):
            rows = slice(c * CHUNK, (c + 1) * CHUNK)
            vc = vln[rows]
            dus.append(dy[rows] * _sg_mix(w_ref, bias_ref, vc, tril, lane_head))
            dm = dy[rows] * u[rows]
            dbias[...] += dm
            dvc = jnp.zeros((CHUNK, SG_W), F32)
            for h in range(nh):
                dmh = jnp.where(lane_head == h, dm, 0.0).astype(BF16)
                dw_ref[h] += jnp.where(tril, _dot_nt(dmh, vc), 0.0)
                wm = jnp.where(tril, w_ref[h], 0.0).astype(BF16)
                dvc = dvc + _dot_tn(wm, dmh)
            dvlns.append(dvc)
        du = jnp.concatenate(dus, axis=0)
        dvln = jnp.concatenate(dvlns, axis=0)
        _accumulate(dlg_ref, _colsum(dvln * xh))
        _accumulate(dlb_ref, _colsum(dvln))
        dv = _ln_bwd(xh, r, lg_ref[...], dvln)
        dp_ref[...] = (jnp.concatenate([du, dv], axis=1) * _gelu_grad(uv)).astype(BF16)

        @pl.when(i == pl.num_programs(0) - 1)
        def _():
            lane = lax.broadcasted_iota(jnp.int32, (CHUNK, CHUNK), 1)
            acc = dbias[...]
            out = jnp.zeros((CHUNK, CHUNK), F32)
            for h in range(nh):
                hs = jnp.sum(acc[:, h * HEAD_DIM:(h + 1) * HEAD_DIM], axis=1, keepdims=True)
                out = out + jnp.where(lane == h, hs, 0.0)
            db_ref[...] = out

    return pl.pallas_call(
        body, name="bwd_sg", grid=(s // tm,),
        in_specs=[_rows(tm, 2 * SG_W, 1), _rows(tm, SG_W), _whole((1, SG_W)), _whole((1, SG_W)),
                  _whole((nh, CHUNK, CHUNK)), _whole((CHUNK, SG_W))],
        out_specs=[_rows(tm, 2 * SG_W), _whole((1, SG_W)), _whole((1, SG_W)), _whole((nh, CHUNK, CHUNK)),
                   _whole((CHUNK, CHUNK))],
        out_shape=[jax.ShapeDtypeStruct((s, 2 * SG_W), BF16), jax.ShapeDtypeStruct((1, SG_W), F32),
                   jax.ShapeDtypeStruct((1, SG_W), F32), jax.ShapeDtypeStruct((nh, CHUNK, CHUNK), F32),
                   jax.ShapeDtypeStruct((CHUNK, CHUNK), F32)],
        scratch_shapes=[pltpu.VMEM((CHUNK, SG_W), F32)],
        compiler_params=_cp("arbitrary"),
    )(proj, dy, lg, lb, w, bias)


def _bwd_conv(proj, dy, w, b, lg, lb):
    s = proj.shape[0]

    def body(p_ref, dy_ref, w_ref, b_ref, lg_ref, lb_ref, dp_ref, dw_ref, db_ref, dlg_ref, dlb_ref,
             abuf, dcbuf):
        _glu_fill(p_ref, abuf, s)
        dcbuf[pl.ds(s, CHUNK), :] = jnp.zeros((CHUNK, CONV_W), F32)
        dw_ref[...] = jnp.zeros_like(dw_ref)

        def chunk(c, carry):
            db, dlg, dlb = carry
            r0 = pl.multiple_of(c * CHUNK, CHUNK)
            acc, shifted = _conv_window(abuf, r0, w_ref)
            r, xh = _ln_stats(acc + b_ref[...])
            ln = xh * lg_ref[...] + lb_ref[...]
            sg = _sigmoid(ln)
            dl = dy_ref[pl.ds(r0, CHUNK), :] * (sg * (1.0 + ln * (1.0 - sg)))
            dc = _ln_bwd(xh, r, lg_ref[...], dl)
            dcbuf[pl.ds(r0, CHUNK), :] = dc
            for k in range(CONV_K):
                dw_ref[CONV_K - 1 - k:CONV_K - k, :] += _colsum(dc * shifted(k))
            return db + _colsum(dc), dlg + _colsum(dl * xh), dlb + _colsum(dl)

        zero = jnp.zeros((1, CONV_W), F32)
        db, dlg, dlb = lax.fori_loop(0, s // CHUNK, chunk, (zero, zero, zero))
        db_ref[...] = db
        dlg_ref[...] = dlg
        dlb_ref[...] = dlb

        def chunk_back(c, carry):
            r0 = pl.multiple_of(c * CHUNK, CHUNK)
            shifted = _shifted(dcbuf[pl.ds(r0, CHUNK + 32), :], back=False)
            da = jnp.zeros((CHUNK, CONV_W), F32)
            for k in range(CONV_K):
                da = da + shifted(k) * w_ref[CONV_K - 1 - k:CONV_K - k, :]
            pv = p_ref[pl.ds(r0, CHUNK), :]
            val, sg = pv[:, :CONV_W], _sigmoid(pv[:, CONV_W:])
            dp_ref[pl.ds(r0, CHUNK), :] = jnp.concatenate([da * sg, da * val * sg * (1.0 - sg)], axis=1).astype(BF16)
            return carry

        lax.fori_loop(0, s // CHUNK, chunk_back, 0)

    row = _whole((1, CONV_W))
    return pl.pallas_call(
        body, name="bwd_conv", grid=(1,),
        in_specs=[pl.BlockSpec((s, 2 * CONV_W), lambda i: (0, 0)), _whole((s, CONV_W)),
                  _whole((CONV_K, CONV_W)), row, row, row],
        out_specs=[_whole((s, 2 * CONV_W)), _whole((CONV_K, CONV_W)), row, row, row],
        out_shape=[jax.ShapeDtypeStruct((s, 2 * CONV_W), BF16), jax.ShapeDtypeStruct((CONV_K, CONV_W), F32)]
        + [jax.ShapeDtypeStruct((1, CONV_W), F32)] * 3,
        scratch_shapes=[pltpu.VMEM((s + CHUNK, CONV_W), F32), pltpu.VMEM((s + CHUNK, CONV_W), F32)],
        compiler_params=_cp("arbitrary"),
    )(proj, dy, w, b, lg, lb)


def _bwd_in(dpc, dps, dpb, x, g, w, dxm):
    s = x.shape[0]
    tm = TOKEN_TILE

    def body(dpc_ref, dps_ref, dpb_ref, x_ref, g_ref, w_ref, dxm_ref, dx_ref, h_ref, dp_ref, dg_ref):
        dp = jnp.concatenate([dpc_ref[...], dps_ref[...], dpb_ref[...]], axis=1)
        dp_ref[...] = dp
        dh = _dot(dp, w_ref[...])
        r, xh = _rms_stats(x_ref[...])
        h_ref[...] = (xh * g_ref[...]).astype(BF16)
        dx, dgrow = _rms_bwd(xh, r, g_ref[...], dh)
        dx_ref[...] = dxm_ref[...] + dx
        _accumulate(dg_ref, _colsum(dgrow))

    return pl.pallas_call(
        body, name="bwd_in", grid=(s // tm,),
        in_specs=[_rows(tm, 2 * CONV_W), _rows(tm, 2 * SG_W), _rows(tm, 3 * SB_W), _rows(tm, D_MODEL),
                  _whole((1, D_MODEL)), _whole((IN_W, D_MODEL)), _rows(tm, D_MODEL)],
        out_specs=[_rows(tm, D_MODEL), _rows(tm, D_MODEL), _rows(tm, IN_W), _whole((1, D_MODEL))],
        out_shape=[jax.ShapeDtypeStruct((s, D_MODEL), F32), jax.ShapeDtypeStruct((s, D_MODEL), BF16),
                   jax.ShapeDtypeStruct((s, IN_W), BF16), jax.ShapeDtypeStruct((1, D_MODEL), F32)],
        compiler_params=_cp("arbitrary"),
    )(dpc, dps, dpb, x, g, w, dxm)


SMALL = ("mix_norm_g", "conv_w", "conv_b", "conv_ln_g", "conv_ln_b", "sg_ln_g", "sg_ln_b", "sg_w", "sg_b",
         "q_norm_g", "k_norm_g", "out_norm_g", "ffn_norm_g")
LARGE = ("w_in", "w_out", "w_gate_up", "w_down")


def _row(v):
    return v.reshape(1, -1)


def _layer_params(p, large, l):
    q = {k: v[l] for k, v in p.items()}
    return dict(
        q, **large,
        mix_norm_g=_row(q["mix_norm_g"]), conv_b=_row(q["conv_b"]), conv_ln_g=_row(q["conv_ln_g"]),
        conv_ln_b=_row(q["conv_ln_b"]), sg_ln_g=_row(q["sg_ln_g"]), sg_ln_b=_row(q["sg_ln_b"]),
        out_norm_g=_row(q["out_norm_g"]), ffn_norm_g=_row(q["ffn_norm_g"]),
        qg=_row(jnp.tile(q["q_norm_g"], SB_W // HEAD_DIM)), kg=_row(jnp.tile(q["k_norm_g"], SB_W // HEAD_DIM)),
        sg_bias=jnp.repeat(q["sg_b"].T, HEAD_DIM, axis=1),
    )


def _layer_fwd(x, q, tok, after_in, after_mixers, after_out):
    proj, qn, kn, vb, kt, vt = _fwd_in(x, q["mix_norm_g"], q["w_in"], q["qg"], q["kg"], tok)
    yc = _fwd_conv(proj, q["conv_w"], q["conv_b"], q["conv_ln_g"], q["conv_ln_b"], after_in(proj))
    ys = _fwd_sg(proj, q["sg_ln_g"], q["sg_ln_b"], q["sg_w"], q["sg_bias"])
    yb, lt = _fwd_sb(qn, kt, vb)
    rest, tok = after_mixers(yb)
    q = dict(q, **rest)
    xm = _fwd_out(yc, ys, yb, q["out_norm_g"], q["w_out"], x, tok)
    gu, xo = _fwd_ffn(xm, q["ffn_norm_g"], q["w_gate_up"], q["w_down"], after_out(xm))
    return xo, q, dict(x=x, proj=proj, qn=qn, kn=kn, kt=kt, vt=vt, lt=lt, yc=yc, ys=ys, yb=yb, xm=xm, gu=gu)


def _layer_bwd_ffn(dxo, q, st, tok):
    dgu, act, h2, dxm, d_ffn_g = _bwd_ffn(dxo, st["gu"], st["xm"], q["ffn_norm_g"], q["w_gate_up"], q["w_down"],
                                          tok)
    d_wgu, d_wd = _matmul_tn(dgu, h2, 512, D_MODEL), _matmul_tn(act, dxo, FFN_H // 2, D_MODEL)
    dyc, dys, dyb, cat, d_out_g = _bwd_out(dxm, st["yc"], st["ys"], st["yb"], q["out_norm_g"], q["w_out"], tok)
    return dxm, (dyc, dys, dyb, d_ffn_g, d_out_g), d_wgu, d_wd, _matmul_tn(cat, dxm, 512, D_MODEL)


def _layer_bwd_mix(dxm, carried, q, st, tok):
    dyc, dys, dyb, d_ffn_g, d_out_g = carried
    dqs, dkn, dv = _bwd_sb(st["qn"], st["kn"], st["kt"], st["vt"], dyb, st["lt"], tok)
    dpb, d_qg, d_kg = _bwd_qk(st["proj"], dqs, dkn, dv, q["qg"], q["kg"])
    dps, d_sg_lg, d_sg_lb, d_sg_w, d_sg_b = _bwd_sg(st["proj"], dys, q["sg_ln_g"], q["sg_ln_b"], q["sg_w"],
                                                    q["sg_bias"])
    dpc, d_conv_w, d_conv_b, d_conv_lg, d_conv_lb = _bwd_conv(st["proj"], dyc, q["conv_w"], q["conv_b"],
                                                              q["conv_ln_g"], q["conv_ln_b"])
    dx, h1, dp, d_mix_g = _bwd_in(dpc, dps, dpb, st["x"], q["mix_norm_g"], q["w_in"], dxm)
    d_win = _matmul_tn(dp, h1, 512, D_MODEL)
    small = dict(
        mix_norm_g=d_mix_g[0], conv_w=d_conv_w, conv_b=d_conv_b[0], conv_ln_g=d_conv_lg[0],
        conv_ln_b=d_conv_lb[0], sg_ln_g=d_sg_lg[0], sg_ln_b=d_sg_lb[0], sg_w=d_sg_w,
        sg_b=d_sg_b[:, :SG_W // HEAD_DIM].T, q_norm_g=d_qg[0], k_norm_g=d_kg[0], out_norm_g=d_out_g[0],
        ffn_norm_g=d_ffn_g[0])
    return dx, d_win, small


def _position():
    x, y, c = lax.axis_index("x"), lax.axis_index("y"), lax.axis_index("c")
    return x, y, c


def _flat(px, py, pc):
    return 4 * px + 2 * py + pc


IN_HBM = pl.BlockSpec(memory_space=pltpu.HBM)
IN_SEM = pl.BlockSpec(memory_space=pltpu.SEMAPHORE)
EFFECT = pltpu.SideEffectType.DATAFLOW_SIDE_EFFECTING
COPIES = dict(scatter=7, spread=7, spread_chips=4, **{"pass": 3})


def _exchange_copies(kinds, src_refs, land_refs, send_sems, recv_sems, layer, arrival):
    x, y, c = _position()
    me = _flat(x, y, c)
    everyone = [(x ^ (k >> 2 & 1), y ^ (k >> 1 & 1), c ^ (k & 1)) for k in range(1, N_DEV)]
    sibling = (x, y, 1 - c)
    chips = [(1 - x, y, c), (x, 1 - y, c), (1 - x, 1 - y, c)]
    out = []
    srcs = iter(src_refs)
    for kind, land in zip(kinds, land_refs):
        land = land if layer is None else land.at[layer]
        if kind == "scatter":
            src = next(srcs)
            moves = [(src.at[_flat(*p)], me, _flat(*p), p) for p in everyone]
        elif kind in ("spread", "spread_chips"):
            src = next(srcs)
            moves = [(src, me, _flat(*p), p) for p in (everyone if kind == "spread" else [sibling] + chips)]
        else:
            moves = [(land.at[_flat(*p)], _flat(*p), _flat(p[0], p[1], 1 - c), sibling) for p in chips]
        for src_block, there, here, peer in moves:
            n = len(out)
            out.append(pltpu.make_async_remote_copy(
                src_ref=src_block, dst_ref=land.at[here if arrival else there], send_sem=send_sems.at[n],
                recv_sem=recv_sems.at[n], device_id=peer, device_id_type=MESH))
    return out


def _exchange_start(kinds, srcs, lands, after, name, layer=None):
    ns, n = len(srcs), len(srcs) + len(lands)
    sems = sum(COPIES[k] for k in kinds)

    def body(*refs):
        send_sems, recv_sems = refs[n + 1], refs[n + 2]
        for cp in _exchange_copies(kinds, refs[:ns], refs[ns:n], send_sems, recv_sems, layer, arrival=False):
            cp.start()
        refs[-1][...] = jnp.zeros_like(refs[-1])

    thru = [pltpu.HBM(a.shape, a.dtype) for a in (*srcs, *lands)]
    outs = pl.pallas_call(
        body, name=name,
        out_shape=(pltpu.SemaphoreType.DMA((sems,)), pltpu.SemaphoreType.DMA((sems,)), *thru,
                   jax.ShapeDtypeStruct((8, 128), F32)),
        in_specs=[IN_HBM] * n + [ORDER_ONLY],
        out_specs=(IN_SEM, IN_SEM, *[IN_HBM] * n, pl.BlockSpec(memory_space=pltpu.VMEM)),
        input_output_aliases={i: 2 + i for i in range(n)},
        compiler_params=pltpu.CompilerParams(has_side_effects=EFFECT),
    )(*[pltpu.with_memory_space_constraint(a, pltpu.HBM) for a in (*srcs, *lands)], after)
    return kinds, outs[0], outs[1], list(outs[2:2 + ns]), list(outs[2 + ns:2 + n]), outs[-1]


def _exchange_wait(pending, after, name, layer=None):
    kinds, send_sems, recv_sems, srcs, lands, _ = pending
    after = list(after) if isinstance(after, (list, tuple)) else [after]
    ns, n = len(srcs), len(srcs) + len(lands)

    def body(*refs):
        for cp in _exchange_copies(kinds, refs[:ns], refs[ns:n], refs[n], refs[n + 1], layer, arrival=True):
            cp.wait_send()
            cp.wait_recv()

    thru = [pltpu.HBM(a.shape, a.dtype) for a in (*srcs, *lands)]
    outs = pl.pallas_call(
        body, name=name, out_shape=tuple(thru),
        in_specs=[IN_HBM] * n + [IN_SEM, IN_SEM] + [ORDER_ONLY] * len(after),
        out_specs=tuple([IN_HBM] * n),
        input_output_aliases={i: i for i in range(n)},
        compiler_params=pltpu.CompilerParams(has_side_effects=EFFECT),
    )(*srcs, *lands, send_sems, recv_sems, *after)
    return list(outs[ns:])


def _landing(block, me):
    land = lax.empty((N_DEV,) + block.shape, block.dtype)
    return lax.dynamic_update_index_in_dim(land, block, me, 0)


def _adamw(parts, w, m, v, tr):
    groups, rows, cols = w.shape

    def body(p_ref, w_ref, m_ref, v_ref, g_ref, d_ref, nm_ref, nv_ref):
        g = p_ref[0].astype(F32)
        for j in range(1, N_DEV):
            g = g + p_ref[j].astype(F32)
        g_ref[...] = g
        m = ADAM_B1 * m_ref[...] + (1.0 - ADAM_B1) * g
        v = ADAM_B2 * v_ref[...] + (1.0 - ADAM_B2) * (g * g)
        nm_ref[...] = m
        nv_ref[...] = v
        m_hat = m / (1.0 - ADAM_B1 ** ADAM_STEP)
        v_hat = v / (1.0 - ADAM_B2 ** ADAM_STEP)
        d_ref[...] = -ADAM_LR * (m_hat / (jnp.sqrt(v_hat) + ADAM_EPS) + ADAM_WD * w_ref[...])

    blk = pl.BlockSpec((None, tr, cols), lambda g, i: (g, i, 0))
    return pl.pallas_call(
        body, name="adamw", grid=(groups, rows // tr),
        in_specs=[pl.BlockSpec((None, N_DEV, tr, cols), lambda g, i: (g, 0, i, 0)), blk, blk, blk],
        out_specs=[blk] * 4,
        out_shape=[jax.ShapeDtypeStruct((groups, rows, cols), F32)] * 4,
        compiler_params=_cp("parallel", "parallel"),
    )(parts, w, m, v)


def _row_tile(rows):
    for cand in range(min(rows, 512) // 8 * 8, 7, -8):
        if rows % cand == 0:
            return cand
    return rows


def _with_own_block(land, blocks, layer, me):
    own = lax.dynamic_index_in_dim(blocks, me, 0, keepdims=True)[None]
    return lax.dynamic_update_slice(land, own, (layer, me, 0, 0))


PACK_LANES = 128


def _pack_layers(arrs):
    parts = []
    for a in arrs:
        flat = a.reshape(a.shape[0], -1)
        parts.append(jnp.pad(flat, ((0, 0), (0, -flat.shape[1] % (8 * PACK_LANES)))))
    return jnp.concatenate(parts, axis=1).reshape(arrs[0].shape[0], -1, PACK_LANES)


def _unpack_layers(packed, shapes):
    flat = packed.reshape(packed.shape[0], -1)
    outs, off = [], 0
    for shp in shapes:
        size = 1
        for d in shp[1:]:
            size *= d
        outs.append(flat[:, off:off + size].reshape(shp))
        off += size + (-size % (8 * PACK_LANES))
    return outs


def kernel(x, mix_norm_g, w_in, conv_w, conv_b, conv_ln_g, conv_ln_b, sg_ln_g, sg_ln_b, sg_w, sg_b, q_norm_g, k_norm_g, out_norm_g, w_out, ffn_norm_g, w_gate_up, w_down, loss_target, m_mix_norm_g, m_w_in, m_conv_w, m_conv_b, m_conv_ln_g, m_conv_ln_b, m_sg_ln_g, m_sg_ln_b, m_sg_w, m_sg_b, m_q_norm_g, m_k_norm_g, m_out_norm_g, m_w_out, m_ffn_norm_g, m_w_gate_up, m_w_down, v_mix_norm_g, v_w_in, v_conv_w, v_conv_b, v_conv_ln_g, v_conv_ln_b, v_sg_ln_g, v_sg_ln_b, v_sg_w, v_sg_b, v_q_norm_g, v_k_norm_g, v_out_norm_g, v_w_out, v_ffn_norm_g, v_w_gate_up, v_w_down):
    names = SMALL[:1] + LARGE[:1] + SMALL[1:12] + LARGE[1:2] + SMALL[12:] + LARGE[2:]
    w = dict(mix_norm_g=mix_norm_g, w_in=w_in, conv_w=conv_w, conv_b=conv_b, conv_ln_g=conv_ln_g,
             conv_ln_b=conv_ln_b, sg_ln_g=sg_ln_g, sg_ln_b=sg_ln_b, sg_w=sg_w, sg_b=sg_b, q_norm_g=q_norm_g,
             k_norm_g=k_norm_g, out_norm_g=out_norm_g, w_out=w_out, ffn_norm_g=ffn_norm_g,
             w_gate_up=w_gate_up, w_down=w_down)
    m = dict(mix_norm_g=m_mix_norm_g, w_in=m_w_in, conv_w=m_conv_w, conv_b=m_conv_b, conv_ln_g=m_conv_ln_g,
             conv_ln_b=m_conv_ln_b, sg_ln_g=m_sg_ln_g, sg_ln_b=m_sg_ln_b, sg_w=m_sg_w, sg_b=m_sg_b,
             q_norm_g=m_q_norm_g, k_norm_g=m_k_norm_g, out_norm_g=m_out_norm_g, w_out=m_w_out,
             ffn_norm_g=m_ffn_norm_g, w_gate_up=m_w_gate_up, w_down=m_w_down)
    v = dict(mix_norm_g=v_mix_norm_g, w_in=v_w_in, conv_w=v_conv_w, conv_b=v_conv_b, conv_ln_g=v_conv_ln_g,
             conv_ln_b=v_conv_ln_b, sg_ln_g=v_sg_ln_g, sg_ln_b=v_sg_ln_b, sg_w=v_sg_w, sg_b=v_sg_b,
             q_norm_g=v_q_norm_g, k_norm_g=v_k_norm_g, out_norm_g=v_out_norm_g, w_out=v_w_out,
             ffn_norm_g=v_ffn_norm_g, w_gate_up=v_w_gate_up, w_down=v_w_down)
    xpos, ypos, cpos = _position()
    me = _flat(xpos, ypos, cpos)
    conv_cols = conv_w.shape[-1]
    no_token = jnp.zeros((8, 128), F32)
    w, m, v = (dict(t, w_in=jnp.swapaxes(t["w_in"], 1, 2), w_gate_up=jnp.swapaxes(t["w_gate_up"], 1, 2))
               for t in (w, m, v))
    shards = {k: w[k].astype(BF16) for k in LARGE}
    full_shape = dict(w_in=(IN_W, D_MODEL), w_out=(D_MODEL, D_MODEL), w_gate_up=(2 * FFN_H, D_MODEL),
                      w_down=(FFN_H, D_MODEL))

    def gather_start(srcs, after, tag):
        return _exchange_start(["spread_chips"] * len(srcs), srcs, [_landing(a, me) for a in srcs], after,
                               f"gather_start_{tag}")

    def gather_pass(pending, after, tag):
        lands = _exchange_wait(pending, after, f"gather_wait_{tag}")
        return _exchange_start(["pass"] * len(lands), [], lands, after, f"gather_pass_{tag}")

    def gathered(pending, keys, after, tag):
        lands = _exchange_wait(pending, after, f"gather_passed_{tag}")
        return {k: a.reshape(full_shape[k]) for k, a in zip(keys, lands)}, lands[len(keys):]

    first, later = ("w_in",), ("w_out", "w_gate_up", "w_down")
    act = x[0]
    head = gather_start([shards["w_in"][0], w["conv_w"]], act, "0")
    tail = gather_start([shards[k][0] for k in later], head[5], "0_later")
    head = gather_pass(head, tail[5], "0")
    large, (conv_blocks,) = gathered(head, first, head[5], "0")
    conv_full = jnp.transpose(conv_blocks, (1, 2, 0, 3)).reshape(DEPTH, CONV_K, CONV_W)
    small_w = dict({k: w[k] for k in SMALL}, conv_w=conv_full)
    qs, stash = [], []
    for l in range(DEPTH):
        coming = {}
        more = l + 1 < DEPTH

        def next_start(after):
            coming["first"] = gather_start([shards[k][l + 1] for k in LARGE], after, str(l + 1))
            return coming["first"][5]

        def after_in(proj):
            return next_start(proj) if more and l == 0 else no_token

        def after_mixers(y_sb):
            if l > 0:
                return {}, no_token
            passing = gather_pass(tail, y_sb, "0_later")
            return gathered(passing, later, passing[5], "0_later")[0], passing[5]

        def after_out(x_mid):
            if not more or l == 0:
                return no_token
            coming["second"] = gather_pass(coming["first"], x_mid, str(l + 1))
            return coming["second"][5]

        token = next_start(act) if more and l > 0 else no_token
        act, q, st = _layer_fwd(act, _layer_params(small_w, large, l), token, after_in, after_mixers, after_out)
        qs.append(q)
        stash.append(st)
        if more:
            if l == 0:
                coming["second"] = gather_pass(coming["first"], act, str(l + 1))
            large, _ = gathered(coming["second"], LARGE, act, str(l + 1))

    loss, dx = _loss_head(act, loss_target[0])
    loss = lax.psum(loss[0, 0], ("x", "y", "c"))

    replicated = tuple(k for k in SMALL if k != "conv_w")
    small_rows = _pack_layers([w[k][:1] for k in replicated]).shape[1]
    conv_rows = _pack_layers([conv_full[:1]]).shape[1]
    group_a, group_b = ("w_gate_up", "w_down", "w_out"), ("w_in",)
    blocks = lambda k, a: a.reshape((N_DEV,) + w[k].shape[1:])
    land_a = [lax.empty((DEPTH, N_DEV) + w[k].shape[1:], BF16) for k in group_a]
    land_b = [lax.empty((DEPTH, N_DEV) + w[k].shape[1:], BF16) for k in group_b]
    land_b.append(lax.empty((DEPTH, N_DEV, small_rows + conv_rows, PACK_LANES), F32))
    pend_a = pend_b = None
    token = no_token
    for l in reversed(range(DEPTH)):
        dxm, carried, d_wgu, d_wd, d_wo = _layer_bwd_ffn(dx, qs[l], stash[l], token)
        srcs = [blocks(k, a) for k, a in zip(group_a, (d_wgu, d_wd, d_wo))]
        if pend_a is not None:
            land_a = _exchange_wait(pend_a, d_wo, f"grads_a_wait_{l + 1}", layer=l + 1)
        land_a = [_with_own_block(ld, a, l, me) for ld, a in zip(land_a, srcs)]
        pend_a = _exchange_start(["scatter"] * 3, srcs, land_a, dxm, f"grads_a_start_{l}", layer=l)
        dx, d_win, small = _layer_bwd_mix(dxm, carried, qs[l], stash[l], pend_a[5])
        packed = _pack_layers([small[k][None] for k in replicated + ("conv_w",)])[0]
        srcs = [blocks("w_in", d_win)]
        if pend_b is not None:
            land_b = _exchange_wait(pend_b, d_win, f"grads_b_wait_{l + 1}", layer=l + 1)
        land_b = [_with_own_block(land_b[0], srcs[0], l, me),
                  lax.dynamic_update_slice(land_b[1], packed[None, None], (l, me, 0, 0))]
        pend_b = _exchange_start(["scatter", "spread"], srcs + [packed], land_b, dx, f"grads_b_start_{l}",
                                 layer=l)
        token = pend_b[5]

    land_a = _exchange_wait(pend_a, token, "grads_a_wait_0", layer=0)
    res = {}
    for k, parts in zip(group_a, land_a):
        res[k] = _adamw(parts, w[k], m[k], v[k], _row_tile(w[k].shape[1]))
    land_b = _exchange_wait(pend_b, [res[k][0] for k in group_a], "grads_b_wait_0", layer=0)
    res["w_in"] = _adamw(land_b[0], w["w_in"], m["w_in"], v["w_in"], _row_tile(w["w_in"].shape[1]))
    small_parts = land_b[1]
    updated = _adamw(small_parts, *(_pack_layers([t[k] for k in replicated]) for t in (w, m, v)), small_rows)
    unpacked = [_unpack_layers(o, [w[k].shape for k in replicated]) for o in updated]
    res.update({k: [u[i] for u in unpacked] for i, k in enumerate(replicated)})
    conv_parts = small_parts[:, :, small_rows:].reshape(DEPTH, N_DEV, -1)[:, :, :CONV_K * CONV_W]
    conv_parts = lax.dynamic_slice_in_dim(conv_parts.reshape(DEPTH, N_DEV, CONV_K, CONV_W), me * conv_cols,
                                          conv_cols, axis=3)
    res["conv_w"] = _adamw(conv_parts, w["conv_w"], m["conv_w"], v["conv_w"], CONV_K)
    for k in ("w_in", "w_gate_up"):
        res[k] = [jnp.swapaxes(a, 1, 2) for a in res[k]]

    return (loss, dx[None], *[res[k][0] for k in names], *[res[k][1] for k in names],
            *[res[k][2] for k in names], *[res[k][3] for k in names])
```

```python
import jax
import jax.numpy as jnp
from jax import lax
from jax.experimental import pallas as pl
from jax.experimental.pallas import tpu as pltpu

F32 = jnp.float32
BF16 = jnp.bfloat16

D_MODEL = 1024
DEPTH = 4
HEAD_DIM = 64
CONV_W = 256
SG_W = 256
SB_W = 512
IN_W = 2560
FFN_H = 2816
CONV_K = 31
CHUNK = 128
OFF_SG = 2 * CONV_W
OFF_SB = OFF_SG + 2 * SG_W
RMS_EPS = 1e-6
LN_EPS = 1e-5
N_DEV = 8
MESH = pl.DeviceIdType.MESH

ADAM_LR = 0.001
ADAM_B1 = 0.9
ADAM_B2 = 0.999
ADAM_EPS = 1e-08
ADAM_WD = 0.01
ADAM_STEP = 10

TOKEN_TILE = 512
FFN_BWD_TILE = 256
LIGHT_TILE = 1024
VMEM_LIMIT = 56 * 1024 * 1024


def _cp(*sem):
    return pltpu.CompilerParams(dimension_semantics=sem or None, vmem_limit_bytes=VMEM_LIMIT)


def _dot(a, b):
    return jnp.dot(a, b, preferred_element_type=F32)


def _dot_nt(a, b):
    return lax.dot_general(a, b, (((1,), (1,)), ((), ())), preferred_element_type=F32)


def _dot_tn(a, b):
    return lax.dot_general(a, b, (((0,), (0,)), ((), ())), preferred_element_type=F32)


def _dot_split(x, m):
    hi = x.astype(BF16)
    lo = (x - hi.astype(F32)).astype(BF16)
    return _dot(hi, m) + _dot(lo, m)


def _group_mean_matrix(width, group):
    r = lax.broadcasted_iota(jnp.int32, (width, width), 0) // group
    c = lax.broadcasted_iota(jnp.int32, (width, width), 1) // group
    return jnp.where(r == c, 1.0 / group, 0.0).astype(BF16)


def _sigmoid(x):
    return 1.0 / (1.0 + jnp.exp(-x))


def _gelu(x):
    return 0.5 * x * (1.0 + lax.erf(x * (2.0 ** -0.5)))


def _gelu_grad(x):
    return 0.5 * (1.0 + lax.erf(x * (2.0 ** -0.5))) + x * jnp.exp(-0.5 * x * x) * (0.5 * (2.0 / jnp.pi) ** 0.5)


def _rms_stats(x):
    r = lax.rsqrt(jnp.mean(x * x, axis=-1, keepdims=True) + RMS_EPS)
    return r, x * r


def _rms_bwd(xh, r, g, dy):
    dxh = dy * g
    dx = r * (dxh - xh * jnp.mean(dxh * xh, axis=-1, keepdims=True))
    return dx, dy * xh


def _ln_stats(x):
    mu = jnp.mean(x, axis=-1, keepdims=True)
    xc = x - mu
    r = lax.rsqrt(jnp.mean(xc * xc, axis=-1, keepdims=True) + LN_EPS)
    return r, xc * r


def _ln_bwd(xh, r, g, dy):
    dxh = dy * g
    return r * (dxh - jnp.mean(dxh, axis=-1, keepdims=True) - xh * jnp.mean(dxh * xh, axis=-1, keepdims=True))


def _colsum(x):
    return jnp.sum(x, axis=0, keepdims=True)


def _rows(tm, n, j=0):
    return pl.BlockSpec((tm, n), lambda i: (i, j))


def _whole(shape):
    return pl.BlockSpec(shape, lambda i: (0,) * len(shape))


ORDER_ONLY = pl.BlockSpec(memory_space=pl.ANY)


def _stack_heads(a):
    even = (lax.broadcasted_iota(jnp.int32, a.shape, 1) % (2 * HEAD_DIM)) < HEAD_DIM
    top = jnp.where(even, a, 0.0)
    bot = jnp.where(even, 0.0, a)
    parts = []
    for c in range(a.shape[0] // CHUNK):
        rows = slice(c * CHUNK, (c + 1) * CHUNK)
        parts += [top[rows], bot[rows]]
    return jnp.concatenate(parts, axis=0)


def _store_stacked(st, st_ref, tr_ref):
    st_ref[...] = st.astype(BF16)
    for p in range(SB_W // CHUNK):
        for c in range(st.shape[0] // (2 * CHUNK)):
            tile = st[2 * c * CHUNK:2 * (c + 1) * CHUNK, p * CHUNK:(p + 1) * CHUNK]
            tr_ref[p, c] = tile.T.astype(BF16)


def _load_transposed(tr_ref):
    rows = []
    for c in range(tr_ref.shape[1]):
        tiles = [tr_ref[p, c].T for p in range(SB_W // CHUNK)]
        rows.append(jnp.concatenate(tiles, axis=1))
    return jnp.concatenate(rows, axis=0)


def _unstack_heads(st):
    even = (lax.broadcasted_iota(jnp.int32, (CHUNK, st.shape[1]), 1) % (2 * HEAD_DIM)) < HEAD_DIM
    parts = []
    for c in range(st.shape[0] // (2 * CHUNK)):
        top = st[2 * c * CHUNK:(2 * c + 1) * CHUNK]
        bot = st[(2 * c + 1) * CHUNK:(2 * c + 2) * CHUNK]
        parts.append(jnp.where(even, top, bot))
    return jnp.concatenate(parts, axis=0)


def _fwd_in(x, g, w, qg, kg, tok):
    s = x.shape[0]
    tm = TOKEN_TILE

    def body(x_ref, g_ref, w_ref, qg_ref, kg_ref, tok_ref, proj_ref, qn_ref, kn_ref, vb_ref, kt_ref, vt_ref):
        r, xh = _rms_stats(x_ref[...])
        h = (xh * g_ref[...]).astype(BF16)
        proj = _dot_nt(h, w_ref[...])
        proj_ref[...] = proj
        gm = _group_mean_matrix(SB_W, HEAD_DIM)
        q = proj[:, OFF_SB:OFF_SB + SB_W]
        k = proj[:, OFF_SB + SB_W:OFF_SB + 2 * SB_W]
        rq = lax.rsqrt(_dot_split(q * q, gm) + RMS_EPS)
        rk = lax.rsqrt(_dot_split(k * k, gm) + RMS_EPS)
        qn_ref[...] = (q * rq * qg_ref[...] * (HEAD_DIM ** -0.5)).astype(BF16)
        _store_stacked(_stack_heads(k * rk * kg_ref[...]), kn_ref, kt_ref)
        _store_stacked(_stack_heads(proj[:, OFF_SB + 2 * SB_W:]), vb_ref, vt_ref)

    tiles = pl.BlockSpec((SB_W // CHUNK, tm // CHUNK, CHUNK, PAIR), lambda i: (0, i, 0, 0))
    tiles_shape = jax.ShapeDtypeStruct((SB_W // CHUNK, s // CHUNK, CHUNK, PAIR), BF16)
    return pl.pallas_call(
        body, name="fwd_in", grid=(s // tm,),
        in_specs=[_rows(tm, D_MODEL), _whole((1, D_MODEL)), _whole((IN_W, D_MODEL)),
                  _whole((1, SB_W)), _whole((1, SB_W)), ORDER_ONLY],
        out_specs=[_rows(tm, IN_W), _rows(tm, SB_W), _rows(2 * tm, SB_W), _rows(2 * tm, SB_W), tiles, tiles],
        out_shape=[jax.ShapeDtypeStruct((s, IN_W), F32), jax.ShapeDtypeStruct((s, SB_W), BF16),
                   jax.ShapeDtypeStruct((2 * s, SB_W), BF16), jax.ShapeDtypeStruct((2 * s, SB_W), BF16),
                   tiles_shape, tiles_shape],
        compiler_params=_cp("parallel"),
    )(x, g, w, qg, kg, tok)


SUBLANES = 8


def _shifted(win, back):
    n = win.shape[0]
    turned = [win] + [pltpu.roll(win, b if back else n - b, axis=0) for b in range(1, SUBLANES)]

    def shifted(k):
        whole, part = divmod(k, SUBLANES)
        start = 32 - whole * SUBLANES if back else whole * SUBLANES
        return turned[part][start:start + CHUNK, :]

    return shifted


def _conv_window(abuf, r0, w_ref):
    shifted = _shifted(abuf[pl.ds(pl.multiple_of(r0 + CHUNK - 32, 32), CHUNK + 32), :], back=True)
    acc = jnp.zeros((CHUNK, CONV_W), F32)
    for k in range(CONV_K):
        acc = acc + shifted(k) * w_ref[CONV_K - 1 - k:CONV_K - k, :]
    return acc, shifted


def _glu_fill(p_ref, abuf, s):
    abuf[0:CHUNK, :] = jnp.zeros((CHUNK, CONV_W), F32)

    def fill(c, carry):
        r0 = pl.multiple_of(c * CHUNK, CHUNK)
        pv = p_ref[pl.ds(r0, CHUNK), :]
        abuf[pl.ds(r0 + CHUNK, CHUNK), :] = pv[:, :CONV_W] * _sigmoid(pv[:, CONV_W:])
        return carry

    lax.fori_loop(0, s // CHUNK, fill, 0)


def _fwd_conv(proj, w, b, lg, lb, tok):
    s = proj.shape[0]

    def body(p_ref, w_ref, b_ref, lg_ref, lb_ref, tok_ref, y_ref, abuf):
        _glu_fill(p_ref, abuf, s)

        def chunk(c, carry):
            r0 = pl.multiple_of(c * CHUNK, CHUNK)
            acc, _ = _conv_window(abuf, r0, w_ref)
            r, xh = _ln_stats(acc + b_ref[...])
            ln = xh * lg_ref[...] + lb_ref[...]
            y_ref[pl.ds(r0, CHUNK), :] = ln * _sigmoid(ln)
            return carry

        lax.fori_loop(0, s // CHUNK, chunk, 0)

    return pl.pallas_call(
        body, name="fwd_conv", grid=(1,),
        in_specs=[pl.BlockSpec((s, 2 * CONV_W), lambda i: (0, 0)), _whole((CONV_K, CONV_W)),
                  _whole((1, CONV_W)), _whole((1, CONV_W)), _whole((1, CONV_W)), ORDER_ONLY],
        out_specs=_whole((s, CONV_W)),
        out_shape=jax.ShapeDtypeStruct((s, CONV_W), F32),
        scratch_shapes=[pltpu.VMEM((s + CHUNK, CONV_W), F32)],
        compiler_params=_cp("arbitrary"),
    )(proj, w, b, lg, lb, tok)


def _sg_masks():
    row = lax.broadcasted_iota(jnp.int32, (CHUNK, CHUNK), 0)
    col = lax.broadcasted_iota(jnp.int32, (CHUNK, CHUNK), 1)
    lane_head = lax.broadcasted_iota(jnp.int32, (CHUNK, SG_W), 1) // HEAD_DIM
    return row >= col, lane_head


def _sg_mix(w_ref, bias_ref, vc, tril, lane_head):
    mixed = bias_ref[...]
    for h in range(SG_W // HEAD_DIM):
        wm = jnp.where(tril, w_ref[h], 0.0).astype(BF16)
        mixed = mixed + jnp.where(lane_head == h, _dot(wm, vc), 0.0)
    return mixed


def _fwd_sg(proj, lg, lb, w, bias):
    s = proj.shape[0]
    tm = LIGHT_TILE

    def body(p_ref, lg_ref, lb_ref, w_ref, bias_ref, y_ref):
        ge = _gelu(p_ref[...])
        u = ge[:, :SG_W]
        r, xh = _ln_stats(ge[:, SG_W:])
        vln = (xh * lg_ref[...] + lb_ref[...]).astype(BF16)
        tril, lane_head = _sg_masks()
        for c in range(tm // CHUNK):
            rows = slice(c * CHUNK, (c + 1) * CHUNK)
            y_ref[rows, :] = u[rows] * _sg_mix(w_ref, bias_ref, vln[rows], tril, lane_head)

    return pl.pallas_call(
        body, name="fwd_sg", grid=(s // tm,),
        in_specs=[_rows(tm, 2 * SG_W, 1), _whole((1, SG_W)), _whole((1, SG_W)),
                  _whole((SG_W // HEAD_DIM, CHUNK, CHUNK)), _whole((CHUNK, SG_W))],
        out_specs=_rows(tm, SG_W),
        out_shape=jax.ShapeDtypeStruct((s, SG_W), F32),
        compiler_params=_cp("parallel"),
    )(proj, lg, lb, w, bias)


SB_Q = 2 * CHUNK
PAIR = 2 * CHUNK
SB_PAIRS = 4


def _pair_tri(kind):
    row = lax.broadcasted_iota(jnp.int32, (PAIR, PAIR), 0)
    col = lax.broadcasted_iota(jnp.int32, (PAIR, PAIR), 1)
    tri = {"after": row > col, "upto": row <= col, "before": row < col}[kind]
    return jnp.where(((row // CHUNK) == (col // CHUNK)) & tri, 1.0, 0.0).astype(BF16)


def _sb_scores(z, qpos0, kpos0, masked):
    sp = jnp.maximum(z, 0.0) + jnp.log(1.0 + jnp.exp(-jnp.abs(z)))
    if not masked:
        return z, sp, sp.astype(BF16), None
    row = lax.broadcasted_iota(jnp.int32, z.shape, 0)
    col = lax.broadcasted_iota(jnp.int32, z.shape, 1) % CHUNK
    mask = (kpos0 + col) < (qpos0 + row)
    return z, sp, jnp.where(mask, sp, 0.0).astype(BF16), mask


def _per_head(c0, c1):
    return jnp.concatenate([jnp.broadcast_to(c0, (SB_Q, CHUNK)), jnp.broadcast_to(c1, (SB_Q, CHUNK))], axis=1)


def _fwd_sb(qn, ktr, vst):
    s = qn.shape[0]
    np_ = SB_PAIRS

    def body(q_ref, k_ref, v_ref, after_ref, o_ref, lt_ref, z_buf, att_buf):
        i = pl.program_id(1)
        first = i * (SB_Q // CHUNK)
        last = first + SB_Q // CHUNK - 1
        after = after_ref[...]
        lanes = [slice(pr * CHUNK, (pr + 1) * CHUNK) for pr in range(np_)]
        qs = [q_ref[:, lanes[pr]] for pr in range(np_)]

        def rows(kb):
            return pl.ds(pl.multiple_of(kb * PAIR, PAIR), PAIR)

        def block(kb, carry, masked):
            out = []
            for pr in range(np_):
                acc, c0, c1 = carry[pr]
                z_next = _dot(qs[pr], k_ref[pr, jnp.maximum(kb - 1, 0)])
                pv = _dot(att_buf[pr], v_ref[rows(jnp.minimum(kb + 1, last)), lanes[pr]])
                z, sp, nlb, mask = _sb_scores(z_buf[pr], i * SB_Q, kb * CHUNK, masked)
                loc = _dot(nlb, after)
                att = jnp.exp(z - sp - loc - _per_head(c0, c1))
                if masked:
                    att = jnp.where(mask, att, 0.0)
                z_buf[pr] = z_next
                att_buf[pr] = att.astype(BF16)
                out.append((acc + pv, c0 + loc[:, 0:1] + nlb[:, 0:1].astype(F32),
                            c1 + loc[:, CHUNK:CHUNK + 1] + nlb[:, CHUNK:CHUNK + 1].astype(F32)))
            return tuple(out)

        for pr in range(np_):
            z_buf[pr] = _dot(qs[pr], k_ref[pr, last])
        att_buf[...] = jnp.zeros_like(att_buf)
        zero = jnp.zeros((SB_Q, 1), F32)
        carry = ((jnp.zeros((SB_Q, CHUNK), F32), zero, zero),) * np_
        for back in range(SB_Q // CHUNK):
            carry = block(last - back, carry, True)
        carry = lax.fori_loop(0, first, lambda j, c: block(first - 1 - j, c, False), carry)
        for pr, (acc, c0, c1) in enumerate(carry):
            o_ref[:, lanes[pr]] = acc + _dot(att_buf[pr], v_ref[rows(0), lanes[pr]])
            lt_ref[:, lanes[pr]] = jnp.concatenate([jnp.broadcast_to(c0, (SB_Q, HEAD_DIM)),
                                                    jnp.broadcast_to(c1, (SB_Q, HEAD_DIM))], axis=1)

    blk = pl.BlockSpec((SB_Q, np_ * CHUNK), lambda p, i: (i, p))
    seq = pl.BlockSpec((2 * s, np_ * CHUNK), lambda p, i: (0, p))
    return pl.pallas_call(
        body, name="fwd_sb", grid=(SB_W // CHUNK // np_, s // SB_Q),
        in_specs=[blk, pl.BlockSpec((np_, s // CHUNK, CHUNK, PAIR), lambda p, i: (p, 0, 0, 0)), seq,
                  pl.BlockSpec((PAIR, PAIR), lambda p, i: (0, 0))],
        out_specs=[blk, blk],
        out_shape=[jax.ShapeDtypeStruct((s, SB_W), F32)] * 2,
        scratch_shapes=[pltpu.VMEM((np_, SB_Q, PAIR), F32), pltpu.VMEM((np_, SB_Q, PAIR), BF16)],
        compiler_params=_cp("parallel", "parallel"),
    )(qn, ktr, vst, _pair_tri("after"))


def _group_norms(yc, ys, yb):
    return [_rms_stats(yc), _rms_stats(ys), _rms_stats(yb)]


def _fwd_out(yc, ys, yb, g, w, x, tok):
    s = x.shape[0]
    tm = LIGHT_TILE

    def body(yc_ref, ys_ref, yb_ref, g_ref, w_ref, x_ref, tok_ref, o_ref):
        stats = _group_norms(yc_ref[...], ys_ref[...], yb_ref[...])
        cat = jnp.concatenate([xh for _, xh in stats], axis=1) * g_ref[...]
        o_ref[...] = x_ref[...] + _dot(cat.astype(BF16), w_ref[...])

    return pl.pallas_call(
        body, name="fwd_out", grid=(s // tm,),
        in_specs=[_rows(tm, CONV_W), _rows(tm, SG_W), _rows(tm, SB_W), _whole((1, D_MODEL)),
                  _whole((D_MODEL, D_MODEL)), _rows(tm, D_MODEL), ORDER_ONLY],
        out_specs=_rows(tm, D_MODEL),
        out_shape=jax.ShapeDtypeStruct((s, D_MODEL), F32),
        compiler_params=_cp("parallel"),
    )(yc, ys, yb, g, w, x, tok)


def _fwd_ffn(x, g, wgu, wd, tok):
    s = x.shape[0]
    tm = TOKEN_TILE

    def body(x_ref, g_ref, wgu_ref, wd_ref, tok_ref, gu_ref, o_ref):
        x = x_ref[...]
        r, xh = _rms_stats(x)
        gu = _dot_nt((xh * g_ref[...]).astype(BF16), wgu_ref[...])
        gu_ref[...] = gu
        gate = gu[:, :FFN_H]
        act = gate * _sigmoid(gate) * gu[:, FFN_H:]
        o_ref[...] = x + _dot(act.astype(BF16), wd_ref[...])

    return pl.pallas_call(
        body, name="fwd_ffn", grid=(s // tm,),
        in_specs=[_rows(tm, D_MODEL), _whole((1, D_MODEL)),
                  pl.BlockSpec((2 * FFN_H, D_MODEL), lambda i: (0, 0), pipeline_mode=pl.Buffered(1)),
                  pl.BlockSpec((FFN_H, D_MODEL), lambda i: (0, 0), pipeline_mode=pl.Buffered(1)), ORDER_ONLY],
        out_specs=[_rows(tm, 2 * FFN_H), _rows(tm, D_MODEL)],
        out_shape=[jax.ShapeDtypeStruct((s, 2 * FFN_H), F32), jax.ShapeDtypeStruct((s, D_MODEL), F32)],
        compiler_params=_cp("parallel"),
    )(x, g, wgu, wd, tok)


def _loss_head(y, target):
    s = y.shape[0]
    tm = LIGHT_TILE

    def body(y_ref, t_ref, l_ref, d_ref):
        @pl.when(pl.program_id(0) == 0)
        def _():
            l_ref[...] = jnp.zeros_like(l_ref)

        err = y_ref[...] - t_ref[...]
        d_ref[...] = err * (1.0 / D_MODEL)
        l_ref[...] += 0.5 * jnp.sum(jnp.mean(err * err, axis=-1, keepdims=True), axis=0, keepdims=True)

    return pl.pallas_call(
        body, name="loss_head", grid=(s // tm,),
        in_specs=[_rows(tm, D_MODEL), _rows(tm, D_MODEL)],
        out_specs=[_whole((1, 1)), _rows(tm, D_MODEL)],
        out_shape=[jax.ShapeDtypeStruct((1, 1), F32), jax.ShapeDtypeStruct((s, D_MODEL), F32)],
        compiler_params=_cp("arbitrary"),
    )(y, target)


def _accumulate(ref, value):
    @pl.when(pl.program_id(0) == 0)
    def _():
        ref[...] = jnp.zeros_like(ref)

    ref[...] += value


def _bwd_ffn(dxo, gu, xm, g, wgu, wd, tok):
    s = dxo.shape[0]
    tm = FFN_BWD_TILE

    def body(dxo_ref, gu_ref, xm_ref, g_ref, wgu_ref, wd_ref, tok_ref, dgu_ref, act_ref, h_ref, dxm_ref, dg_ref):
        dxo = dxo_ref[...]
        gu = gu_ref[...]
        gate, up = gu[:, :FFN_H], gu[:, FFN_H:]
        sg = _sigmoid(gate)
        sl = gate * sg
        act_ref[...] = (sl * up).astype(BF16)
        dact = _dot_nt(dxo.astype(BF16), wd_ref[...])
        dgate = dact * up * (sg * (1.0 + gate * (1.0 - sg)))
        dgu = jnp.concatenate([dgate, dact * sl], axis=1).astype(BF16)
        dgu_ref[...] = dgu
        dh = _dot(dgu, wgu_ref[...])
        r, xh = _rms_stats(xm_ref[...])
        h_ref[...] = (xh * g_ref[...]).astype(BF16)
        dx, dgrow = _rms_bwd(xh, r, g_ref[...], dh)
        dxm_ref[...] = dxo + dx
        _accumulate(dg_ref, _colsum(dgrow))

    return pl.pallas_call(
        body, name="bwd_ffn", grid=(s // tm,),
        in_specs=[_rows(tm, D_MODEL), _rows(tm, 2 * FFN_H), _rows(tm, D_MODEL), _whole((1, D_MODEL)),
                  pl.BlockSpec((2 * FFN_H, D_MODEL), lambda i: (0, 0), pipeline_mode=pl.Buffered(1)),
                  pl.BlockSpec((FFN_H, D_MODEL), lambda i: (0, 0), pipeline_mode=pl.Buffered(1)), ORDER_ONLY],
        out_specs=[_rows(tm, 2 * FFN_H), _rows(tm, FFN_H), _rows(tm, D_MODEL), _rows(tm, D_MODEL),
                   _whole((1, D_MODEL))],
        out_shape=[jax.ShapeDtypeStruct((s, 2 * FFN_H), BF16), jax.ShapeDtypeStruct((s, FFN_H), BF16),
                   jax.ShapeDtypeStruct((s, D_MODEL), BF16), jax.ShapeDtypeStruct((s, D_MODEL), F32),
                   jax.ShapeDtypeStruct((1, D_MODEL), F32)],
        compiler_params=_cp("arbitrary"),
    )(dxo, gu, xm, g, wgu, wd, tok)


def _matmul_tn(a, b, tm, tn, out_dtype=BF16):
    s, m = a.shape
    n = b.shape[1]

    def body(a_ref, b_ref, o_ref):
        o_ref[...] = _dot_tn(a_ref[...].astype(BF16), b_ref[...].astype(BF16)).astype(out_dtype)

    return pl.pallas_call(
        body, name="weight_grad", grid=(m // tm, n // tn),
        in_specs=[pl.BlockSpec((s, tm), lambda i, j: (0, i)), pl.BlockSpec((s, tn), lambda i, j: (0, j))],
        out_specs=pl.BlockSpec((tm, tn), lambda i, j: (i, j)),
        out_shape=jax.ShapeDtypeStruct((m, n), out_dtype),
        compiler_params=_cp("parallel", "parallel"),
    )(a, b)


def _bwd_out(dxm, yc, ys, yb, g, w, tok):
    s = dxm.shape[0]
    tm = LIGHT_TILE

    def body(dxm_ref, yc_ref, ys_ref, yb_ref, g_ref, w_ref, tok_ref, dyc_ref, dys_ref, dyb_ref, cat_ref, dg_ref):
        stats = _group_norms(yc_ref[...], ys_ref[...], yb_ref[...])
        g = g_ref[...]
        cat_ref[...] = (jnp.concatenate([xh for _, xh in stats], axis=1) * g).astype(BF16)
        dcat = _dot_nt(dxm_ref[...].astype(BF16), w_ref[...])
        dgs = []
        off = 0
        for (r, xh), out in zip(stats, (dyc_ref, dys_ref, dyb_ref)):
            cols = slice(off, off + xh.shape[1])
            dx, dgrow = _rms_bwd(xh, r, g[:, cols], dcat[:, cols])
            out[...] = dx
            dgs.append(_colsum(dgrow))
            off += xh.shape[1]
        _accumulate(dg_ref, jnp.concatenate(dgs, axis=1))

    return pl.pallas_call(
        body, name="bwd_out", grid=(s // tm,),
        in_specs=[_rows(tm, D_MODEL), _rows(tm, CONV_W), _rows(tm, SG_W), _rows(tm, SB_W),
                  _whole((1, D_MODEL)), _whole((D_MODEL, D_MODEL)), ORDER_ONLY],
        out_specs=[_rows(tm, CONV_W), _rows(tm, SG_W), _rows(tm, SB_W), _rows(tm, D_MODEL),
                   _whole((1, D_MODEL))],
        out_shape=[jax.ShapeDtypeStruct((s, CONV_W), F32), jax.ShapeDtypeStruct((s, SG_W), F32),
                   jax.ShapeDtypeStruct((s, SB_W), F32), jax.ShapeDtypeStruct((s, D_MODEL), BF16),
                   jax.ShapeDtypeStruct((1, D_MODEL), F32)],
        compiler_params=_cp("arbitrary"),
    )(dxm, yc, ys, yb, g, w, tok)


def _bwd_sb(qn, kst, ktr, vtr, dy, ltot, tok):
    s = qn.shape[0]
    np_ = SB_PAIRS

    def body(q_ref, k_ref, kt_ref, vt_ref, do_ref, lt_ref, upto_ref, before_ref, tok_ref, dq_ref, dk_ref, dv_ref,
             z_buf, da_buf, dz_buf, att_buf):
        i = pl.program_id(1)
        first = i * (SB_Q // CHUNK)
        last = first + SB_Q // CHUNK - 1

        @pl.when(i == 0)
        def _():
            dk_ref[...] = jnp.zeros_like(dk_ref)
            dv_ref[...] = jnp.zeros_like(dv_ref)

        lanes = [slice(pr * CHUNK, (pr + 1) * CHUNK) for pr in range(np_)]
        qs = [q_ref[:, lanes[pr]] for pr in range(np_)]
        dos = [do_ref[:, lanes[pr]] for pr in range(np_)]
        dobs = [do.astype(BF16) for do in dos]
        q_ts = [q.astype(F32).T.astype(BF16) for q in qs]
        do_ts = [do.T.astype(BF16) for do in dos]
        ltots = [_per_head(lt_ref[:, pr * CHUNK:pr * CHUNK + 1],
                           lt_ref[:, pr * CHUNK + HEAD_DIM:pr * CHUNK + HEAD_DIM + 1]) for pr in range(np_)]
        upto = upto_ref[...]
        before = before_ref[...]
        last0, last1 = slice(CHUNK - 1, CHUNK), slice(PAIR - 1, PAIR)

        def rows(kb):
            return pl.ds(pl.multiple_of(kb * PAIR, PAIR), PAIR)

        def ahead(pr, kb):
            return _dot(qs[pr], kt_ref[pr, kb]), _dot(dobs[pr], vt_ref[pr, kb])

        def behind(pr, kb, dq):
            dzb = dz_buf[pr]
            dk_ref[pr, kb] += _dot(q_ts[pr], dzb)
            dv_ref[pr, kb] += _dot(do_ts[pr], att_buf[pr])
            return dq + _dot(dzb, k_ref[rows(kb), lanes[pr]])

        def block(kb, carry, masked):
            out = []
            for pr in range(np_):
                dq, p0, p1, e0, e1 = carry[pr]
                z_next, da_next = ahead(pr, jnp.minimum(kb + 1, last))
                dq = behind(pr, jnp.maximum(kb - 1, 0), dq)
                z, sp, nlb, mask = _sb_scores(z_buf[pr], i * SB_Q, kb * CHUNK, masked)
                pin = _dot(nlb, upto) + _per_head(p0, p1)
                sig = jnp.exp(z - sp)
                att = jnp.exp(z - sp - (ltots[pr] - pin))
                if masked:
                    att = jnp.where(mask, att, 0.0)
                e = att * da_buf[pr]
                ebefore = _dot(e.astype(BF16), before) + _per_head(e0, e1)
                dz = e - sig * (e + ebefore)
                if masked:
                    dz = jnp.where(mask, dz, 0.0)
                z_buf[pr] = z_next
                da_buf[pr] = da_next
                dz_buf[pr] = dz.astype(BF16)
                att_buf[pr] = att.astype(BF16)
                out.append((dq, pin[:, last0], pin[:, last1],
                            ebefore[:, last0] + e[:, last0], ebefore[:, last1] + e[:, last1]))
            return tuple(out)

        for pr in range(np_):
            z_buf[pr], da_buf[pr] = ahead(pr, 0)
        dz_buf[...] = jnp.zeros_like(dz_buf)
        att_buf[...] = jnp.zeros_like(att_buf)
        zero = jnp.zeros((SB_Q, 1), F32)
        carry = ((jnp.zeros((SB_Q, CHUNK), F32), zero, zero, zero, zero),) * np_
        carry = lax.fori_loop(0, first, lambda kb, c: block(kb, c, False), carry)
        for ahead_of in range(SB_Q // CHUNK):
            carry = block(first + ahead_of, carry, True)
        for pr in range(np_):
            dq_ref[:, lanes[pr]] = behind(pr, last, carry[pr][0])

    blk = pl.BlockSpec((SB_Q, np_ * CHUNK), lambda p, i: (i, p))
    seq = pl.BlockSpec((2 * s, np_ * CHUNK), lambda p, i: (0, p))
    tiles = pl.BlockSpec((np_, s // CHUNK, CHUNK, PAIR), lambda p, i: (p, 0, 0, 0))
    tri = pl.BlockSpec((PAIR, PAIR), lambda p, i: (0, 0))
    return pl.pallas_call(
        body, name="bwd_sb", grid=(SB_W // CHUNK // np_, s // SB_Q),
        in_specs=[blk, seq, tiles, tiles, blk, blk, tri, tri, ORDER_ONLY],
        out_specs=[blk, tiles, tiles],
        out_shape=[jax.ShapeDtypeStruct((s, SB_W), F32)]
        + [jax.ShapeDtypeStruct((SB_W // CHUNK, s // CHUNK, CHUNK, PAIR), F32)] * 2,
        scratch_shapes=[pltpu.VMEM((np_, SB_Q, PAIR), F32), pltpu.VMEM((np_, SB_Q, PAIR), F32),
                        pltpu.VMEM((np_, SB_Q, PAIR), BF16), pltpu.VMEM((np_, SB_Q, PAIR), BF16)],
        compiler_params=_cp("parallel", "arbitrary"),
    )(qn, kst, ktr, vtr, dy, ltot, _pair_tri("upto"), _pair_tri("before"), tok)


def _head_sum(row):
    acc = row[:, 0:HEAD_DIM]
    for h in range(1, SB_W // HEAD_DIM):
        acc = acc + row[:, h * HEAD_DIM:(h + 1) * HEAD_DIM]
    return acc


def _bwd_qk(proj, dqs, dkn, dv, qg, kg):
    s = proj.shape[0]
    tm = LIGHT_TILE
    tiles = pl.BlockSpec((SB_W // CHUNK, tm // CHUNK, CHUNK, PAIR), lambda i: (0, i, 0, 0))

    def body(q_ref, k_ref, dqs_ref, dkn_ref, dv_ref, qg_ref, kg_ref, dp_ref, dqg_ref, dkg_ref, qacc, kacc):
        i = pl.program_id(0)
        gm = _group_mean_matrix(SB_W, HEAD_DIM)

        def one(x, dy, g, acc):
            r = lax.rsqrt(_dot_split(x * x, gm) + RMS_EPS)
            xh = x * r
            dxh = dy * g
            _accumulate(acc, _colsum(dy * xh))
            return r * (dxh - xh * _dot_split(dxh * xh, gm))

        dq = one(q_ref[...], dqs_ref[...] * (HEAD_DIM ** -0.5), qg_ref[...], qacc)
        dk = one(k_ref[...], _unstack_heads(_load_transposed(dkn_ref)), kg_ref[...], kacc)
        dp_ref[...] = jnp.concatenate([dq, dk, _unstack_heads(_load_transposed(dv_ref))], axis=1).astype(BF16)

        @pl.when(i == pl.num_programs(0) - 1)
        def _():
            dqg_ref[...] = _head_sum(qacc[...])
            dkg_ref[...] = _head_sum(kacc[...])

    return pl.pallas_call(
        body, name="bwd_qk", grid=(s // tm,),
        in_specs=[_rows(tm, SB_W, OFF_SB // SB_W), _rows(tm, SB_W, OFF_SB // SB_W + 1),
                  _rows(tm, SB_W), tiles, tiles, _whole((1, SB_W)), _whole((1, SB_W))],
        out_specs=[_rows(tm, 3 * SB_W), _whole((1, HEAD_DIM)), _whole((1, HEAD_DIM))],
        out_shape=[jax.ShapeDtypeStruct((s, 3 * SB_W), BF16), jax.ShapeDtypeStruct((1, HEAD_DIM), F32),
                   jax.ShapeDtypeStruct((1, HEAD_DIM), F32)],
        scratch_shapes=[pltpu.VMEM((1, SB_W), F32), pltpu.VMEM((1, SB_W), F32)],
        compiler_params=_cp("arbitrary"),
    )(proj, proj, dqs, dkn, dv, qg, kg)


def _bwd_sg(proj, dy, lg, lb, w, bias):
    s = proj.shape[0]
    tm = LIGHT_TILE
    nh = SG_W // HEAD_DIM

    def body(p_ref, dy_ref, lg_ref, lb_ref, w_ref, bias_ref, dp_ref, dlg_ref, dlb_ref, dw_ref, db_ref, dbias):
        i = pl.program_id(0)
        uv = p_ref[...]
        ge = _gelu(uv)
        u = ge[:, :SG_W]
        r, xh = _ln_stats(ge[:, SG_W:])
        vln = (xh * lg_ref[...] + lb_ref[...]).astype(BF16)
        dy = dy_ref[...]
        tril, lane_head = _sg_masks()

        @pl.when(i == 0)
        def _():
            dw_ref[...] = jnp.zeros_like(dw_ref)
            dbias[...] = jnp.zeros_like(dbias)

        dus, dvlns = [], []
        for c in range(tm // CHUNK):
            rows = slice(c * CHUNK, (c + 1) * CHUNK)
            vc = vln[rows]
            dus.append(dy[rows] * _sg_mix(w_ref, bias_ref, vc, tril, lane_head))
            dm = dy[rows] * u[rows]
            dbias[...] += dm
            dvc = jnp.zeros((CHUNK, SG_W), F32)
            for h in range(nh):
                dmh = jnp.where(lane_head == h, dm, 0.0).astype(BF16)
                dw_ref[h] += jnp.where(tril, _dot_nt(dmh, vc), 0.0)
                wm = jnp.where(tril, w_ref[h], 0.0).astype(BF16)
                dvc = dvc + _dot_tn(wm, dmh)
            dvlns.append(dvc)
        du = jnp.concatenate(dus, axis=0)
        dvln = jnp.concatenate(dvlns, axis=0)
        _accumulate(dlg_ref, _colsum(dvln * xh))
        _accumulate(dlb_ref, _colsum(dvln))
        dv = _ln_bwd(xh, r, lg_ref[...], dvln)
        dp_ref[...] = (jnp.concatenate([du, dv], axis=1) * _gelu_grad(uv)).astype(BF16)

        @pl.when(i == pl.num_programs(0) - 1)
        def _():
            lane = lax.broadcasted_iota(jnp.int32, (CHUNK, CHUNK), 1)
            acc = dbias[...]
            out = jnp.zeros((CHUNK, CHUNK), F32)
            for h in range(nh):
                hs = jnp.sum(acc[:, h * HEAD_DIM:(h + 1) * HEAD_DIM], axis=1, keepdims=True)
                out = out + jnp.where(lane == h, hs, 0.0)
            db_ref[...] = out

    return pl.pallas_call(
        body, name="bwd_sg", grid=(s // tm,),
        in_specs=[_rows(tm, 2 * SG_W, 1), _rows(tm, SG_W), _whole((1, SG_W)), _whole((1, SG_W)),
                  _whole((nh, CHUNK, CHUNK)), _whole((CHUNK, SG_W))],
        out_specs=[_rows(tm, 2 * SG_W), _whole((1, SG_W)), _whole((1, SG_W)), _whole((nh, CHUNK, CHUNK)),
                   _whole((CHUNK, CHUNK))],
        out_shape=[jax.ShapeDtypeStruct((s, 2 * SG_W), BF16), jax.ShapeDtypeStruct((1, SG_W), F32),
                   jax.ShapeDtypeStruct((1, SG_W), F32), jax.ShapeDtypeStruct((nh, CHUNK, CHUNK), F32),
                   jax.ShapeDtypeStruct((CHUNK, CHUNK), F32)],
        scratch_shapes=[pltpu.VMEM((CHUNK, SG_W), F32)],
        compiler_params=_cp("arbitrary"),
    )(proj, dy, lg, lb, w, bias)


def _bwd_conv(proj, dy, w, b, lg, lb):
    s = proj.shape[0]

    def body(p_ref, dy_ref, w_ref, b_ref, lg_ref, lb_ref, dp_ref, dw_ref, db_ref, dlg_ref, dlb_ref,
             abuf, dcbuf):
        _glu_fill(p_ref, abuf, s)
        dcbuf[pl.ds(s, CHUNK), :] = jnp.zeros((CHUNK, CONV_W), F32)
        dw_ref[...] = jnp.zeros_like(dw_ref)

        def chunk(c, carry):
            db, dlg, dlb = carry
            r0 = pl.multiple_of(c * CHUNK, CHUNK)
            acc, shifted = _conv_window(abuf, r0, w_ref)
            r, xh = _ln_stats(acc + b_ref[...])
            ln = xh * lg_ref[...] + lb_ref[...]
            sg = _sigmoid(ln)
            dl = dy_ref[pl.ds(r0, CHUNK), :] * (sg * (1.0 + ln * (1.0 - sg)))
            dc = _ln_bwd(xh, r, lg_ref[...], dl)
            dcbuf[pl.ds(r0, CHUNK), :] = dc
            for k in range(CONV_K):
                dw_ref[CONV_K - 1 - k:CONV_K - k, :] += _colsum(dc * shifted(k))
            return db + _colsum(dc), dlg + _colsum(dl * xh), dlb + _colsum(dl)

        zero = jnp.zeros((1, CONV_W), F32)
        db, dlg, dlb = lax.fori_loop(0, s // CHUNK, chunk, (zero, zero, zero))
        db_ref[...] = db
        dlg_ref[...] = dlg
        dlb_ref[...] = dlb

        def chunk_back(c, carry):
            r0 = pl.multiple_of(c * CHUNK, CHUNK)
            shifted = _shifted(dcbuf[pl.ds(r0, CHUNK + 32), :], back=False)
            da = jnp.zeros((CHUNK, CONV_W), F32)
            for k in range(CONV_K):
                da = da + shifted(k) * w_ref[CONV_K - 1 - k:CONV_K - k, :]
            pv = p_ref[pl.ds(r0, CHUNK), :]
            val, sg = pv[:, :CONV_W], _sigmoid(pv[:, CONV_W:])
            dp_ref[pl.ds(r0, CHUNK), :] = jnp.concatenate([da * sg, da * val * sg * (1.0 - sg)], axis=1).astype(BF16)
            return carry

        lax.fori_loop(0, s // CHUNK, chunk_back, 0)

    row = _whole((1, CONV_W))
    return pl.pallas_call(
        body, name="bwd_conv", grid=(1,),
        in_specs=[pl.BlockSpec((s, 2 * CONV_W), lambda i: (0, 0)), _whole((s, CONV_W)),
                  _whole((CONV_K, CONV_W)), row, row, row],
        out_specs=[_whole((s, 2 * CONV_W)), _whole((CONV_K, CONV_W)), row, row, row],
        out_shape=[jax.ShapeDtypeStruct((s, 2 * CONV_W), BF16), jax.ShapeDtypeStruct((CONV_K, CONV_W), F32)]
        + [jax.ShapeDtypeStruct((1, CONV_W), F32)] * 3,
        scratch_shapes=[pltpu.VMEM((s + CHUNK, CONV_W), F32), pltpu.VMEM((s + CHUNK, CONV_W), F32)],
        compiler_params=_cp("arbitrary"),
    )(proj, dy, w, b, lg, lb)


def _bwd_in(dpc, dps, dpb, x, g, w, dxm):
    s = x.shape[0]
    tm = TOKEN_TILE

    def body(dpc_ref, dps_ref, dpb_ref, x_ref, g_ref, w_ref, dxm_ref, dx_ref, h_ref, dp_ref, dg_ref):
        dp = jnp.concatenate([dpc_ref[...], dps_ref[...], dpb_ref[...]], axis=1)
        dp_ref[...] = dp
        dh = _dot(dp, w_ref[...])
        r, xh = _rms_stats(x_ref[...])
        h_ref[...] = (xh * g_ref[...]).astype(BF16)
        dx, dgrow = _rms_bwd(xh, r, g_ref[...], dh)
        dx_ref[...] = dxm_ref[...] + dx
        _accumulate(dg_ref, _colsum(dgrow))

    return pl.pallas_call(
        body, name="bwd_in", grid=(s // tm,),
        in_specs=[_rows(tm, 2 * CONV_W), _rows(tm, 2 * SG_W), _rows(tm, 3 * SB_W), _rows(tm, D_MODEL),
                  _whole((1, D_MODEL)), _whole((IN_W, D_MODEL)), _rows(tm, D_MODEL)],
        out_specs=[_rows(tm, D_MODEL), _rows(tm, D_MODEL), _rows(tm, IN_W), _whole((1, D_MODEL))],
        out_shape=[jax.ShapeDtypeStruct((s, D_MODEL), F32), jax.ShapeDtypeStruct((s, D_MODEL), BF16),
                   jax.ShapeDtypeStruct((s, IN_W), BF16), jax.ShapeDtypeStruct((1, D_MODEL), F32)],
        compiler_params=_cp("arbitrary"),
    )(dpc, dps, dpb, x, g, w, dxm)


SMALL = ("mix_norm_g", "conv_w", "conv_b", "conv_ln_g", "conv_ln_b", "sg_ln_g", "sg_ln_b", "sg_w", "sg_b",
         "q_norm_g", "k_norm_g", "out_norm_g", "ffn_norm_g")
LARGE = ("w_in", "w_out", "w_gate_up", "w_down")


def _row(v):
    return v.reshape(1, -1)


def _layer_params(p, large, l):
    q = {k: v[l] for k, v in p.items()}
    return dict(
        q, **large,
        mix_norm_g=_row(q["mix_norm_g"]), conv_b=_row(q["conv_b"]), conv_ln_g=_row(q["conv_ln_g"]),
        conv_ln_b=_row(q["conv_ln_b"]), sg_ln_g=_row(q["sg_ln_g"]), sg_ln_b=_row(q["sg_ln_b"]),
        out_norm_g=_row(q["out_norm_g"]), ffn_norm_g=_row(q["ffn_norm_g"]),
        qg=_row(jnp.tile(q["q_norm_g"], SB_W // HEAD_DIM)), kg=_row(jnp.tile(q["k_norm_g"], SB_W // HEAD_DIM)),
        sg_bias=jnp.repeat(q["sg_b"].T, HEAD_DIM, axis=1),
    )


def _layer_fwd(x, q, tok, after_in, after_mixers, after_out):
    proj, qn, kn, vb, kt, vt = _fwd_in(x, q["mix_norm_g"], q["w_in"], q["qg"], q["kg"], tok)
    yc = _fwd_conv(proj, q["conv_w"], q["conv_b"], q["conv_ln_g"], q["conv_ln_b"], after_in(proj))
    ys = _fwd_sg(proj, q["sg_ln_g"], q["sg_ln_b"], q["sg_w"], q["sg_bias"])
    yb, lt = _fwd_sb(qn, kt, vb)
    rest, tok = after_mixers(yb)
    q = dict(q, **rest)
    xm = _fwd_out(yc, ys, yb, q["out_norm_g"], q["w_out"], x, tok)
    gu, xo = _fwd_ffn(xm, q["ffn_norm_g"], q["w_gate_up"], q["w_down"], after_out(xm))
    return xo, q, dict(x=x, proj=proj, qn=qn, kn=kn, kt=kt, vt=vt, lt=lt, yc=yc, ys=ys, yb=yb, xm=xm, gu=gu)


def _layer_bwd_ffn(dxo, q, st, tok):
    dgu, act, h2, dxm, d_ffn_g = _bwd_ffn(dxo, st["gu"], st["xm"], q["ffn_norm_g"], q["w_gate_up"], q["w_down"],
                                          tok)
    d_wgu, d_wd = _matmul_tn(dgu, h2, 512, D_MODEL), _matmul_tn(act, dxo, FFN_H // 2, D_MODEL)
    dyc, dys, dyb, cat, d_out_g = _bwd_out(dxm, st["yc"], st["ys"], st["yb"], q["out_norm_g"], q["w_out"], tok)
    return dxm, (dyc, dys, dyb, d_ffn_g, d_out_g), d_wgu, d_wd, _matmul_tn(cat, dxm, 512, D_MODEL)


def _layer_bwd_mix(dxm, carried, q, st, tok):
    dyc, dys, dyb, d_ffn_g, d_out_g = carried
    dqs, dkn, dv = _bwd_sb(st["qn"], st["kn"], st["kt"], st["vt"], dyb, st["lt"], tok)
    dpb, d_qg, d_kg = _bwd_qk(st["proj"], dqs, dkn, dv, q["qg"], q["kg"])
    dps, d_sg_lg, d_sg_lb, d_sg_w, d_sg_b = _bwd_sg(st["proj"], dys, q["sg_ln_g"], q["sg_ln_b"], q["sg_w"],
                                                    q["sg_bias"])
    dpc, d_conv_w, d_conv_b, d_conv_lg, d_conv_lb = _bwd_conv(st["proj"], dyc, q["conv_w"], q["conv_b"],
                                                              q["conv_ln_g"], q["conv_ln_b"])
    dx, h1, dp, d_mix_g = _bwd_in(dpc, dps, dpb, st["x"], q["mix_norm_g"], q["w_in"], dxm)
    d_win = _matmul_tn(dp, h1, 512, D_MODEL)
    small = dict(
        mix_norm_g=d_mix_g[0], conv_w=d_conv_w, conv_b=d_conv_b[0], conv_ln_g=d_conv_lg[0],
        conv_ln_b=d_conv_lb[0], sg_ln_g=d_sg_lg[0], sg_ln_b=d_sg_lb[0], sg_w=d_sg_w,
        sg_b=d_sg_b[:, :SG_W // HEAD_DIM].T, q_norm_g=d_qg[0], k_norm_g=d_kg[0], out_norm_g=d_out_g[0],
        ffn_norm_g=d_ffn_g[0])
    return dx, d_win, small


def _position():
    x, y, c = lax.axis_index("x"), lax.axis_index("y"), lax.axis_index("c")
    return x, y, c


def _flat(px, py, pc):
    return 4 * px + 2 * py + pc


IN_HBM = pl.BlockSpec(memory_space=pltpu.HBM)
IN_SEM = pl.BlockSpec(memory_space=pltpu.SEMAPHORE)
EFFECT = pltpu.SideEffectType.DATAFLOW_SIDE_EFFECTING
COPIES = dict(scatter=7, spread=7, spread_chips=4, **{"pass": 3})


def _exchange_copies(kinds, src_refs, land_refs, send_sems, recv_sems, layer, arrival):
    x, y, c = _position()
    me = _flat(x, y, c)
    everyone = [(x ^ (k >> 2 & 1), y ^ (k >> 1 & 1), c ^ (k & 1)) for k in range(1, N_DEV)]
    sibling = (x, y, 1 - c)
    chips = [(1 - x, y, c), (x, 1 - y, c), (1 - x, 1 - y, c)]
    out = []
    srcs = iter(src_refs)
    for kind, land in zip(kinds, land_refs):
        land = land if layer is None else land.at[layer]
        if kind == "scatter":
            src = next(srcs)
            moves = [(src.at[_flat(*p)], me, _flat(*p), p) for p in everyone]
        elif kind in ("spread", "spread_chips"):
            src = next(srcs)
            moves = [(src, me, _flat(*p), p) for p in (everyone if kind == "spread" else [sibling] + chips)]
        else:
            moves = [(land.at[_flat(*p)], _flat(*p), _flat(p[0], p[1], 1 - c), sibling) for p in chips]
        for src_block, there, here, peer in moves:
            n = len(out)
            out.append(pltpu.make_async_remote_copy(
                src_ref=src_block, dst_ref=land.at[here if arrival else there], send_sem=send_sems.at[n],
                recv_sem=recv_sems.at[n], device_id=peer, device_id_type=MESH))
    return out


def _exchange_start(kinds, srcs, lands, after, name, layer=None):
    ns, n = len(srcs), len(srcs) + len(lands)
    sems = sum(COPIES[k] for k in kinds)

    def body(*refs):
        send_sems, recv_sems = refs[n + 1], refs[n + 2]
        for cp in _exchange_copies(kinds, refs[:ns], refs[ns:n], send_sems, recv_sems, layer, arrival=False):
            cp.start()
        refs[-1][...] = jnp.zeros_like(refs[-1])

    thru = [pltpu.HBM(a.shape, a.dtype) for a in (*srcs, *lands)]
    outs = pl.pallas_call(
        body, name=name,
        out_shape=(pltpu.SemaphoreType.DMA((sems,)), pltpu.SemaphoreType.DMA((sems,)), *thru,
                   jax.ShapeDtypeStruct((8, 128), F32)),
        in_specs=[IN_HBM] * n + [ORDER_ONLY],
        out_specs=(IN_SEM, IN_SEM, *[IN_HBM] * n, pl.BlockSpec(memory_space=pltpu.VMEM)),
        input_output_aliases={i: 2 + i for i in range(n)},
        compiler_params=pltpu.CompilerParams(has_side_effects=EFFECT),
    )(*[pltpu.with_memory_space_constraint(a, pltpu.HBM) for a in (*srcs, *lands)], after)
    return kinds, outs[0], outs[1], list(outs[2:2 + ns]), list(outs[2 + ns:2 + n]), outs[-1]


def _exchange_wait(pending, after, name, layer=None):
    kinds, send_sems, recv_sems, srcs, lands, _ = pending
    after = list(after) if isinstance(after, (list, tuple)) else [after]
    ns, n = len(srcs), len(srcs) + len(lands)

    def body(*refs):
        for cp in _exchange_copies(kinds, refs[:ns], refs[ns:n], refs[n], refs[n + 1], layer, arrival=True):
            cp.wait_send()
            cp.wait_recv()

    thru = [pltpu.HBM(a.shape, a.dtype) for a in (*srcs, *lands)]
    outs = pl.pallas_call(
        body, name=name, out_shape=tuple(thru),
        in_specs=[IN_HBM] * n + [IN_SEM, IN_SEM] + [ORDER_ONLY] * len(after),
        out_specs=tuple([IN_HBM] * n),
        input_output_aliases={i: i for i in range(n)},
        compiler_params=pltpu.CompilerParams(has_side_effects=EFFECT),
    )(*srcs, *lands, send_sems, recv_sems, *after)
    return list(outs[ns:])


def _landing(block, me):
    land = lax.empty((N_DEV,) + block.shape, block.dtype)
    return lax.dynamic_update_index_in_dim(land, block, me, 0)


def _adamw(parts, w, m, v, tr):
    groups, rows, cols = w.shape

    def body(p_ref, w_ref, m_ref, v_ref, g_ref, d_ref, nm_ref, nv_ref):
        g = p_ref[0].astype(F32)
        for j in range(1, N_DEV):
            g = g + p_ref[j].astype(F32)
        g_ref[...] = g
        m = ADAM_B1 * m_ref[...] + (1.0 - ADAM_B1) * g
        v = ADAM_B2 * v_ref[...] + (1.0 - ADAM_B2) * (g * g)
        nm_ref[...] = m
        nv_ref[...] = v
        m_hat = m / (1.0 - ADAM_B1 ** ADAM_STEP)
        v_hat = v / (1.0 - ADAM_B2 ** ADAM_STEP)
        d_ref[...] = -ADAM_LR * (m_hat / (jnp.sqrt(v_hat) + ADAM_EPS) + ADAM_WD * w_ref[...])

    blk = pl.BlockSpec((None, tr, cols), lambda g, i: (g, i, 0))
    return pl.pallas_call(
        body, name="adamw", grid=(groups, rows // tr),
        in_specs=[pl.BlockSpec((None, N_DEV, tr, cols), lambda g, i: (g, 0, i, 0)), blk, blk, blk],
        out_specs=[blk] * 4,
        out_shape=[jax.ShapeDtypeStruct((groups, rows, cols), F32)] * 4,
        compiler_params=_cp("parallel", "parallel"),
    )(parts, w, m, v)


def _row_tile(rows):
    for cand in range(min(rows, 512) // 8 * 8, 7, -8):
        if rows % cand == 0:
            return cand
    return rows


def _with_own_block(land, blocks, layer, me):
    own = lax.dynamic_index_in_dim(blocks, me, 0, keepdims=True)[None]
    return lax.dynamic_update_slice(land, own, (layer, me, 0, 0))


PACK_LANES = 128


def _pack_layers(arrs):
    parts = []
    for a in arrs:
        flat = a.reshape(a.shape[0], -1)
        parts.append(jnp.pad(flat, ((0, 0), (0, -flat.shape[1] % (8 * PACK_LANES)))))
    return jnp.concatenate(parts, axis=1).reshape(arrs[0].shape[0], -1, PACK_LANES)


def _unpack_layers(packed, shapes):
    flat = packed.reshape(packed.shape[0], -1)
    outs, off = [], 0
    for shp in shapes:
        size = 1
        for d in shp[1:]:
            size *= d
        outs.append(flat[:, off:off + size].reshape(shp))
        off += size + (-size % (8 * PACK_LANES))
    return outs


def kernel(x, mix_norm_g, w_in, conv_w, conv_b, conv_ln_g, conv_ln_b, sg_ln_g, sg_ln_b, sg_w, sg_b, q_norm_g, k_norm_g, out_norm_g, w_out, ffn_norm_g, w_gate_up, w_down, loss_target, m_mix_norm_g, m_w_in, m_conv_w, m_conv_b, m_conv_ln_g, m_conv_ln_b, m_sg_ln_g, m_sg_ln_b, m_sg_w, m_sg_b, m_q_norm_g, m_k_norm_g, m_out_norm_g, m_w_out, m_ffn_norm_g, m_w_gate_up, m_w_down, v_mix_norm_g, v_w_in, v_conv_w, v_conv_b, v_conv_ln_g, v_conv_ln_b, v_sg_ln_g, v_sg_ln_b, v_sg_w, v_sg_b, v_q_norm_g, v_k_norm_g, v_out_norm_g, v_w_out, v_ffn_norm_g, v_w_gate_up, v_w_down):
    names = SMALL[:1] + LARGE[:1] + SMALL[1:12] + LARGE[1:2] + SMALL[12:] + LARGE[2:]
    w = dict(mix_norm_g=mix_norm_g, w_in=w_in, conv_w=conv_w, conv_b=conv_b, conv_ln_g=conv_ln_g,
             conv_ln_b=conv_ln_b, sg_ln_g=sg_ln_g, sg_ln_b=sg_ln_b, sg_w=sg_w, sg_b=sg_b, q_norm_g=q_norm_g,
             k_norm_g=k_norm_g, out_norm_g=out_norm_g, w_out=w_out, ffn_norm_g=ffn_norm_g,
             w_gate_up=w_gate_up, w_down=w_down)
    m = dict(mix_norm_g=m_mix_norm_g, w_in=m_w_in, conv_w=m_conv_w, conv_b=m_conv_b, conv_ln_g=m_conv_ln_g,
             conv_ln_b=m_conv_ln_b, sg_ln_g=m_sg_ln_g, sg_ln_b=m_sg_ln_b, sg_w=m_sg_w, sg_b=m_sg_b,
             q_norm_g=m_q_norm_g, k_norm_g=m_k_norm_g, out_norm_g=m_out_norm_g, w_out=m_w_out,
             ffn_norm_g=m_ffn_norm_g, w_gate_up=m_w_gate_up, w_down=m_w_down)
    v = dict(mix_norm_g=v_mix_norm_g, w_in=v_w_in, conv_w=v_conv_w, conv_b=v_conv_b, conv_ln_g=v_conv_ln_g,
             conv_ln_b=v_conv_ln_b, sg_ln_g=v_sg_ln_g, sg_ln_b=v_sg_ln_b, sg_w=v_sg_w, sg_b=v_sg_b,
             q_norm_g=v_q_norm_g, k_norm_g=v_k_norm_g, out_norm_g=v_out_norm_g, w_out=v_w_out,
             ffn_norm_g=v_ffn_norm_g, w_gate_up=v_w_gate_up, w_down=v_w_down)
    xpos, ypos, cpos = _position()
    me = _flat(xpos, ypos, cpos)
    conv_cols = conv_w.shape[-1]
    no_token = jnp.zeros((8, 128), F32)
    w, m, v = (dict(t, w_in=jnp.swapaxes(t["w_in"], 1, 2), w_gate_up=jnp.swapaxes(t["w_gate_up"], 1, 2))
               for t in (w, m, v))
    shards = {k: w[k].astype(BF16) for k in LARGE}
    full_shape = dict(w_in=(IN_W, D_MODEL), w_out=(D_MODEL, D_MODEL), w_gate_up=(2 * FFN_H, D_MODEL),
                      w_down=(FFN_H, D_MODEL))

    def gather_start(srcs, after, tag):
        return _exchange_start(["spread_chips"] * len(srcs), srcs, [_landing(a, me) for a in srcs], after,
                               f"gather_start_{tag}")

    def gather_pass(pending, after, tag):
        lands = _exchange_wait(pending, after, f"gather_wait_{tag}")
        return _exchange_start(["pass"] * len(lands), [], lands, after, f"gather_pass_{tag}")

    def gathered(pending, keys, after, tag):
        lands = _exchange_wait(pending, after, f"gather_passed_{tag}")
        return {k: a.reshape(full_shape[k]) for k, a in zip(keys, lands)}, lands[len(keys):]

    first, later = ("w_in",), ("w_out", "w_gate_up", "w_down")
    act = x[0]
    head = gather_start([shards["w_in"][0], w["conv_w"]], act, "0")
    tail = gather_start([shards[k][0] for k in later], head[5], "0_later")
    head = gather_pass(head, tail[5], "0")
    large, (conv_blocks,) = gathered(head, first, head[5], "0")
    conv_full = jnp.transpose(conv_blocks, (1, 2, 0, 3)).reshape(DEPTH, CONV_K, CONV_W)
    small_w = dict({k: w[k] for k in SMALL}, conv_w=conv_full)
    qs, stash = [], []
    for l in range(DEPTH):
        coming = {}
        more = l + 1 < DEPTH

        def next_start(after):
            coming["first"] = gather_start([shards[k][l + 1] for k in LARGE], after, str(l + 1))
            return coming["first"][5]

        def after_in(proj):
            return next_start(proj) if more and l == 0 else no_token

        def after_mixers(y_sb):
            if l > 0:
                return {}, no_token
            passing = gather_pass(tail, y_sb, "0_later")
            return gathered(passing, later, passing[5], "0_later")[0], passing[5]

        def after_out(x_mid):
            if not more or l == 0:
                return no_token
            coming["second"] = gather_pass(coming["first"], x_mid, str(l + 1))
            return coming["second"][5]

        token = next_start(act) if more and l > 0 else no_token
        act, q, st = _layer_fwd(act, _layer_params(small_w, large, l), token, after_in, after_mixers, after_out)
        qs.append(q)
        stash.append(st)
        if more:
            if l == 0:
                coming["second"] = gather_pass(coming["first"], act, str(l + 1))
            large, _ = gathered(coming["second"], LARGE, act, str(l + 1))

    loss, dx = _loss_head(act, loss_target[0])
    loss = lax.psum(loss[0, 0], ("x", "y", "c"))

    replicated = tuple(k for k in SMALL if k != "conv_w")
    small_rows = _pack_layers([w[k][:1] for k in replicated]).shape[1]
    conv_rows = _pack_layers([conv_full[:1]]).shape[1]
    group_a, group_b = ("w_gate_up", "w_down", "w_out"), ("w_in",)
    blocks = lambda k, a: a.reshape((N_DEV,) + w[k].shape[1:])
    land_a = [lax.empty((DEPTH, N_DEV) + w[k].shape[1:], BF16) for k in group_a]
    land_b = [lax.empty((DEPTH, N_DEV) + w[k].shape[1:], BF16) for k in group_b]
    land_b.append(lax.empty((DEPTH, N_DEV, small_rows + conv_rows, PACK_LANES), F32))
    pend_a = pend_b = None
    token = no_token
    for l in reversed(range(DEPTH)):
        dxm, carried, d_wgu, d_wd, d_wo = _layer_bwd_ffn(dx, qs[l], stash[l], token)
        srcs = [blocks(k, a) for k, a in zip(group_a, (d_wgu, d_wd, d_wo))]
        if pend_a is not None:
            land_a = _exchange_wait(pend_a, d_wo, f"grads_a_wait_{l + 1}", layer=l + 1)
        land_a = [_with_own_block(ld, a, l, me) for ld, a in zip(land_a, srcs)]
        pend_a = _exchange_start(["scatter"] * 3, srcs, land_a, dxm, f"grads_a_start_{l}", layer=l)
        dx, d_win, small = _layer_bwd_mix(dxm, carried, qs[l], stash[l], pend_a[5])
        packed = _pack_layers([small[k][None] for k in replicated + ("conv_w",)])[0]
        srcs = [blocks("w_in", d_win)]
        if pend_b is not None:
            land_b = _exchange_wait(pend_b, d_win, f"grads_b_wait_{l + 1}", layer=l + 1)
        land_b = [_with_own_block(land_b[0], srcs[0], l, me),
                  lax.dynamic_update_slice(land_b[1], packed[None, None], (l, me, 0, 0))]
        pend_b = _exchange_start(["scatter", "spread"], srcs + [packed], land_b, dx, f"grads_b_start_{l}",
                                 layer=l)
        token = pend_b[5]

    land_a = _exchange_wait(pend_a, token, "grads_a_wait_0", layer=0)
    res = {}
    for k, parts in zip(group_a, land_a):
        res[k] = _adamw(parts, w[k], m[k], v[k], _row_tile(w[k].shape[1]))
    land_b = _exchange_wait(pend_b, [res[k][0] for k in group_a], "grads_b_wait_0", layer=0)
    res["w_in"] = _adamw(land_b[0], w["w_in"], m["w_in"], v["w_in"], _row_tile(w["w_in"].shape[1]))
    small_parts = land_b[1]
    updated = _adamw(small_parts, *(_pack_layers([t[k] for k in replicated]) for t in (w, m, v)), small_rows)
    unpacked = [_unpack_layers(o, [w[k].shape for k in replicated]) for o in updated]
    res.update({k: [u[i] for u in unpacked] for i, k in enumerate(replicated)})
    conv_parts = small_parts[:, :, small_rows:].reshape(DEPTH, N_DEV, -1)[:, :, :CONV_K * CONV_W]
    conv_parts = lax.dynamic_slice_in_dim(conv_parts.reshape(DEPTH, N_DEV, CONV_K, CONV_W), me * conv_cols,
                                          conv_cols, axis=3)
    res["conv_w"] = _adamw(conv_parts, w["conv_w"], m["conv_w"], v["conv_w"], CONV_K)
    for k in ("w_in", "w_gate_up"):
        res[k] = [jnp.swapaxes(a, 1, 2) for a in res[k]]

    return (loss, dx[None], *[res[k][0] for k in names], *[res[k][1] for k in names],
            *[res[k][2] for k in names], *[res[k][3] for k in names])
```

```python
import jax
import jax.numpy as jnp
from jax import lax
from jax.experimental import pallas as pl
from jax.experimental.pallas import tpu as pltpu

F32 = jnp.float32
BF16 = jnp.bfloat16

D_MODEL = 1024
DEPTH = 4
HEAD_DIM = 64
CONV_W = 256
SG_W = 256
SB_W = 512
IN_W = 2560
FFN_H = 2816
CONV_K = 31
CHUNK = 128
OFF_SG = 2 * CONV_W
OFF_SB = OFF_SG + 2 * SG_W
RMS_EPS = 1e-6
LN_EPS = 1e-5
N_DEV = 8
MESH = pl.DeviceIdType.MESH

ADAM_LR = 0.001
ADAM_B1 = 0.9
ADAM_B2 = 0.999
ADAM_EPS = 1e-08
ADAM_WD = 0.01
ADAM_STEP = 10

TOKEN_TILE = 512
FFN_BWD_TILE = 256
LIGHT_TILE = 1024
VMEM_LIMIT = 56 * 1024 * 1024


def _cp(*sem):
    return pltpu.CompilerParams(dimension_semantics=sem or None, vmem_limit_bytes=VMEM_LIMIT)


def _dot(a, b):
    return jnp.dot(a, b, preferred_element_type=F32)


def _dot_nt(a, b):
    return lax.dot_general(a, b, (((1,), (1,)), ((), ())), preferred_element_type=F32)


def _dot_tn(a, b):
    return lax.dot_general(a, b, (((0,), (0,)), ((), ())), preferred_element_type=F32)


def _dot_split(x, m):
    hi = x.astype(BF16)
    lo = (x - hi.astype(F32)).astype(BF16)
    return _dot(hi, m) + _dot(lo, m)


def _group_mean_matrix(width, group):
    r = lax.broadcasted_iota(jnp.int32, (width, width), 0) // group
    c = lax.broadcasted_iota(jnp.int32, (width, width), 1) // group
    return jnp.where(r == c, 1.0 / group, 0.0).astype(BF16)


def _sigmoid(x):
    return 1.0 / (1.0 + jnp.exp(-x))


def _gelu(x):
    return 0.5 * x * (1.0 + lax.erf(x * (2.0 ** -0.5)))


def _gelu_grad(x):
    return 0.5 * (1.0 + lax.erf(x * (2.0 ** -0.5))) + x * jnp.exp(-0.5 * x * x) * (0.5 * (2.0 / jnp.pi) ** 0.5)


def _rms_stats(x):
    r = lax.rsqrt(jnp.mean(x * x, axis=-1, keepdims=True) + RMS_EPS)
    return r, x * r


def _rms_bwd(xh, r, g, dy):
    dxh = dy * g
    dx = r * (dxh - xh * jnp.mean(dxh * xh, axis=-1, keepdims=True))
    return dx, dy * xh


def _ln_stats(x):
    mu = jnp.mean(x, axis=-1, keepdims=True)
    xc = x - mu
    r = lax.rsqrt(jnp.mean(xc * xc, axis=-1, keepdims=True) + LN_EPS)
    return r, xc * r


def _ln_bwd(xh, r, g, dy):
    dxh = dy * g
    return r * (dxh - jnp.mean(dxh, axis=-1, keepdims=True) - xh * jnp.mean(dxh * xh, axis=-1, keepdims=True))


def _colsum(x):
    return jnp.sum(x, axis=0, keepdims=True)


def _rows(tm, n, j=0):
    return pl.BlockSpec((tm, n), lambda i: (i, j))


def _whole(shape):
    return pl.BlockSpec(shape, lambda i: (0,) * len(shape))


ORDER_ONLY = pl.BlockSpec(memory_space=pl.ANY)


def _stack_heads(a):
    even = (lax.broadcasted_iota(jnp.int32, a.shape, 1) % (2 * HEAD_DIM)) < HEAD_DIM
    top = jnp.where(even, a, 0.0)
    bot = jnp.where(even, 0.0, a)
    parts = []
    for c in range(a.shape[0] // CHUNK):
        rows = slice(c * CHUNK, (c + 1) * CHUNK)
        parts += [top[rows], bot[rows]]
    return jnp.concatenate(parts, axis=0)


def _store_stacked(st, st_ref, tr_ref):
    st_ref[...] = st.astype(BF16)
    for p in range(SB_W // CHUNK):
        for c in range(st.shape[0] // (2 * CHUNK)):
            tile = st[2 * c * CHUNK:2 * (c + 1) * CHUNK, p * CHUNK:(p + 1) * CHUNK]
            tr_ref[p, c] = tile.T.astype(BF16)


def _load_transposed(tr_ref):
    rows = []
    for c in range(tr_ref.shape[1]):
        tiles = [tr_ref[p, c].T for p in range(SB_W // CHUNK)]
        rows.append(jnp.concatenate(tiles, axis=1))
    return jnp.concatenate(rows, axis=0)


def _unstack_heads(st):
    even = (lax.broadcasted_iota(jnp.int32, (CHUNK, st.shape[1]), 1) % (2 * HEAD_DIM)) < HEAD_DIM
    parts = []
    for c in range(st.shape[0] // (2 * CHUNK)):
        top = st[2 * c * CHUNK:(2 * c + 1) * CHUNK]
        bot = st[(2 * c + 1) * CHUNK:(2 * c + 2) * CHUNK]
        parts.append(jnp.where(even, top, bot))
    return jnp.concatenate(parts, axis=0)


def _fwd_in(x, g, w, qg, kg, tok):
    s = x.shape[0]
    tm = TOKEN_TILE

    def body(x_ref, g_ref, w_ref, qg_ref, kg_ref, tok_ref, proj_ref, qn_ref, kn_ref, vb_ref, kt_ref, vt_ref):
        r, xh = _rms_stats(x_ref[...])
        h = (xh * g_ref[...]).astype(BF16)
        proj = _dot_nt(h, w_ref[...])
        proj_ref[...] = proj
        gm = _group_mean_matrix(SB_W, HEAD_DIM)
        q = proj[:, OFF_SB:OFF_SB + SB_W]
        k = proj[:, OFF_SB + SB_W:OFF_SB + 2 * SB_W]
        rq = lax.rsqrt(_dot_split(q * q, gm) + RMS_EPS)
        rk = lax.rsqrt(_dot_split(k * k, gm) + RMS_EPS)
        qn_ref[...] = (q * rq * qg_ref[...] * (HEAD_DIM ** -0.5)).astype(BF16)
        _store_stacked(_stack_heads(k * rk * kg_ref[...]), kn_ref, kt_ref)
        _store_stacked(_stack_heads(proj[:, OFF_SB + 2 * SB_W:]), vb_ref, vt_ref)

    tiles = pl.BlockSpec((SB_W // CHUNK, tm // CHUNK, CHUNK, PAIR), lambda i: (0, i, 0, 0))
    tiles_shape = jax.ShapeDtypeStruct((SB_W // CHUNK, s // CHUNK, CHUNK, PAIR), BF16)
    return pl.pallas_call(
        body, name="fwd_in", grid=(s // tm,),
        in_specs=[_rows(tm, D_MODEL), _whole((1, D_MODEL)), _whole((IN_W, D_MODEL)),
                  _whole((1, SB_W)), _whole((1, SB_W)), ORDER_ONLY],
        out_specs=[_rows(tm, IN_W), _rows(tm, SB_W), _rows(2 * tm, SB_W), _rows(2 * tm, SB_W), tiles, tiles],
        out_shape=[jax.ShapeDtypeStruct((s, IN_W), F32), jax.ShapeDtypeStruct((s, SB_W), BF16),
                   jax.ShapeDtypeStruct((2 * s, SB_W), BF16), jax.ShapeDtypeStruct((2 * s, SB_W), BF16),
                   tiles_shape, tiles_shape],
        compiler_params=_cp("parallel"),
    )(x, g, w, qg, kg, tok)


SUBLANES = 8


def _shifted(win, back):
    n = win.shape[0]
    turned = [win] + [pltpu.roll(win, b if back else n - b, axis=0) for b in range(1, SUBLANES)]

    def shifted(k):
        whole, part = divmod(k, SUBLANES)
        start = 32 - whole * SUBLANES if back else whole * SUBLANES
        return turned[part][start:start + CHUNK, :]

    return shifted


def _conv_window(abuf, r0, w_ref):
    shifted = _shifted(abuf[pl.ds(pl.multiple_of(r0 + CHUNK - 32, 32), CHUNK + 32), :], back=True)
    acc = jnp.zeros((CHUNK, CONV_W), F32)
    for k in range(CONV_K):
        acc = acc + shifted(k) * w_ref[CONV_K - 1 - k:CONV_K - k, :]
    return acc, shifted


def _glu_fill(p_ref, abuf, s):
    abuf[0:CHUNK, :] = jnp.zeros((CHUNK, CONV_W), F32)

    def fill(c, carry):
        r0 = pl.multiple_of(c * CHUNK, CHUNK)
        pv = p_ref[pl.ds(r0, CHUNK), :]
        abuf[pl.ds(r0 + CHUNK, CHUNK), :] = pv[:, :CONV_W] * _sigmoid(pv[:, CONV_W:])
        return carry

    lax.fori_loop(0, s // CHUNK, fill, 0)


def _fwd_conv(proj, w, b, lg, lb, tok):
    s = proj.shape[0]

    def body(p_ref, w_ref, b_ref, lg_ref, lb_ref, tok_ref, y_ref, abuf):
        _glu_fill(p_ref, abuf, s)

        def chunk(c, carry):
            r0 = pl.multiple_of(c * CHUNK, CHUNK)
            acc, _ = _conv_window(abuf, r0, w_ref)
            r, xh = _ln_stats(acc + b_ref[...])
            ln = xh * lg_ref[...] + lb_ref[...]
            y_ref[pl.ds(r0, CHUNK), :] = ln * _sigmoid(ln)
            return carry

        lax.fori_loop(0, s // CHUNK, chunk, 0)

    return pl.pallas_call(
        body, name="fwd_conv", grid=(1,),
        in_specs=[pl.BlockSpec((s, 2 * CONV_W), lambda i: (0, 0)), _whole((CONV_K, CONV_W)),
                  _whole((1, CONV_W)), _whole((1, CONV_W)), _whole((1, CONV_W)), ORDER_ONLY],
        out_specs=_whole((s, CONV_W)),
        out_shape=jax.ShapeDtypeStruct((s, CONV_W), F32),
        scratch_shapes=[pltpu.VMEM((s + CHUNK, CONV_W), F32)],
        compiler_params=_cp("arbitrary"),
    )(proj, w, b, lg, lb, tok)


def _sg_masks():
    row = lax.broadcasted_iota(jnp.int32, (CHUNK, CHUNK), 0)
    col = lax.broadcasted_iota(jnp.int32, (CHUNK, CHUNK), 1)
    lane_head = lax.broadcasted_iota(jnp.int32, (CHUNK, SG_W), 1) // HEAD_DIM
    return row >= col, lane_head


def _sg_mix(w_ref, bias_ref, vc, tril, lane_head):
    mixed = bias_ref[...]
    for h in range(SG_W // HEAD_DIM):
        wm = jnp.where(tril, w_ref[h], 0.0).astype(BF16)
        mixed = mixed + jnp.where(lane_head == h, _dot(wm, vc), 0.0)
    return mixed


def _fwd_sg(proj, lg, lb, w, bias):
    s = proj.shape[0]
    tm = LIGHT_TILE

    def body(p_ref, lg_ref, lb_ref, w_ref, bias_ref, y_ref):
        ge = _gelu(p_ref[...])
        u = ge[:, :SG_W]
        r, xh = _ln_stats(ge[:, SG_W:])
        vln = (xh * lg_ref[...] + lb_ref[...]).astype(BF16)
        tril, lane_head = _sg_masks()
        for c in range(tm // CHUNK):
            rows = slice(c * CHUNK, (c + 1) * CHUNK)
            y_ref[rows, :] = u[rows] * _sg_mix(w_ref, bias_ref, vln[rows], tril, lane_head)

    return pl.pallas_call(
        body, name="fwd_sg", grid=(s // tm,),
        in_specs=[_rows(tm, 2 * SG_W, 1), _whole((1, SG_W)), _whole((1, SG_W)),
                  _whole((SG_W // HEAD_DIM, CHUNK, CHUNK)), _whole((CHUNK, SG_W))],
        out_specs=_rows(tm, SG_W),
        out_shape=jax.ShapeDtypeStruct((s, SG_W), F32),
        compiler_params=_cp("parallel"),
    )(proj, lg, lb, w, bias)


SB_Q = 2 * CHUNK
PAIR = 2 * CHUNK
SB_PAIRS = 4


def _pair_tri(kind):
    row = lax.broadcasted_iota(jnp.int32, (PAIR, PAIR), 0)
    col = lax.broadcasted_iota(jnp.int32, (PAIR, PAIR), 1)
    tri = {"after": row > col, "upto": row <= col, "before": row < col}[kind]
    return jnp.where(((row // CHUNK) == (col // CHUNK)) & tri, 1.0, 0.0).astype(BF16)


def _sb_scores(z, qpos0, kpos0, masked):
    sp = jnp.maximum(z, 0.0) + jnp.log(1.0 + jnp.exp(-jnp.abs(z)))
    if not masked:
        return z, sp, sp.astype(BF16), None
    row = lax.broadcasted_iota(jnp.int32, z.shape, 0)
    col = lax.broadcasted_iota(jnp.int32, z.shape, 1) % CHUNK
    mask = (kpos0 + col) < (qpos0 + row)
    return z, sp, jnp.where(mask, sp, 0.0).astype(BF16), mask


def _per_head(c0, c1):
    rows = c0.shape[0]
    return jnp.concatenate([jnp.broadcast_to(c0, (rows, CHUNK)), jnp.broadcast_to(c1, (rows, CHUNK))], axis=1)


def _fwd_sb(qn, ktr, vst):
    s = qn.shape[0]
    np_ = SB_PAIRS

    def body(q_ref, k_ref, v_ref, after_ref, o_ref, lt_ref, z_buf, att_buf):
        i = pl.program_id(1)
        first = i * (SB_Q // CHUNK)
        last = first + SB_Q // CHUNK - 1
        after = after_ref[...]
        lanes = [slice(pr * CHUNK, (pr + 1) * CHUNK) for pr in range(np_)]
        qs = [q_ref[:, lanes[pr]] for pr in range(np_)]

        def rows(kb):
            return pl.ds(pl.multiple_of(kb * PAIR, PAIR), PAIR)

        def block(kb, carry, masked, top=0):
            out = []
            for pr in range(np_):
                acc, c0, c1 = carry[pr]
                z_next = _dot(qs[pr], k_ref[pr, jnp.maximum(kb - 1, 0)])
                pv = _dot(att_buf[pr], v_ref[rows(jnp.minimum(kb + 1, last)), lanes[pr]])
                z, sp, nlb, mask = _sb_scores(z_buf[pr, top:, :], i * SB_Q + top, kb * CHUNK, masked)
                loc = _dot(nlb, after)
                att = jnp.exp(z - sp - loc - _per_head(c0[top:], c1[top:]))
                if masked:
                    att = jnp.where(mask, att, 0.0)
                z_buf[pr] = z_next
                if top:
                    att_buf[pr, :top, :] = jnp.zeros((top, PAIR), BF16)
                att_buf[pr, top:, :] = att.astype(BF16)
                add0 = loc[:, 0:1] + nlb[:, 0:1].astype(F32)
                add1 = loc[:, CHUNK:CHUNK + 1] + nlb[:, CHUNK:CHUNK + 1].astype(F32)
                if top:
                    add0 = jnp.concatenate([jnp.zeros((top, 1), F32), add0], axis=0)
                    add1 = jnp.concatenate([jnp.zeros((top, 1), F32), add1], axis=0)
                out.append((acc + pv, c0 + add0, c1 + add1))
            return tuple(out)

        for pr in range(np_):
            z_buf[pr] = _dot(qs[pr], k_ref[pr, last])
        att_buf[...] = jnp.zeros_like(att_buf)
        zero = jnp.zeros((SB_Q, 1), F32)
        carry = ((jnp.zeros((SB_Q, CHUNK), F32), zero, zero),) * np_
        for back in range(SB_Q // CHUNK):
            carry = block(last - back, carry, True, top=SB_Q - (back + 1) * CHUNK)
        carry = lax.fori_loop(0, first, lambda j, c: block(first - 1 - j, c, False), carry)
        for pr, (acc, c0, c1) in enumerate(carry):
            o_ref[:, lanes[pr]] = acc + _dot(att_buf[pr], v_ref[rows(0), lanes[pr]])
            lt_ref[:, lanes[pr]] = jnp.concatenate([jnp.broadcast_to(c0, (SB_Q, HEAD_DIM)),
                                                    jnp.broadcast_to(c1, (SB_Q, HEAD_DIM))], axis=1)

    blk = pl.BlockSpec((SB_Q, np_ * CHUNK), lambda p, i: (i, p))
    seq = pl.BlockSpec((2 * s, np_ * CHUNK), lambda p, i: (0, p))
    return pl.pallas_call(
        body, name="fwd_sb", grid=(SB_W // CHUNK // np_, s // SB_Q),
        in_specs=[blk, pl.BlockSpec((np_, s // CHUNK, CHUNK, PAIR), lambda p, i: (p, 0, 0, 0)), seq,
                  pl.BlockSpec((PAIR, PAIR), lambda p, i: (0, 0))],
        out_specs=[blk, blk],
        out_shape=[jax.ShapeDtypeStruct((s, SB_W), F32)] * 2,
        scratch_shapes=[pltpu.VMEM((np_, SB_Q, PAIR), F32), pltpu.VMEM((np_, SB_Q, PAIR), BF16)],
        compiler_params=_cp("parallel", "parallel"),
    )(qn, ktr, vst, _pair_tri("after"))


def _group_norms(yc, ys, yb):
    return [_rms_stats(yc), _rms_stats(ys), _rms_stats(yb)]


def _fwd_out(yc, ys, yb, g, w, x, tok):
    s = x.shape[0]
    tm = LIGHT_TILE

    def body(yc_ref, ys_ref, yb_ref, g_ref, w_ref, x_ref, tok_ref, o_ref):
        stats = _group_norms(yc_ref[...], ys_ref[...], yb_ref[...])
        cat = jnp.concatenate([xh for _, xh in stats], axis=1) * g_ref[...]
        o_ref[...] = x_ref[...] + _dot(cat.astype(BF16), w_ref[...])

    return pl.pallas_call(
        body, name="fwd_out", grid=(s // tm,),
        in_specs=[_rows(tm, CONV_W), _rows(tm, SG_W), _rows(tm, SB_W), _whole((1, D_MODEL)),
                  _whole((D_MODEL, D_MODEL)), _rows(tm, D_MODEL), ORDER_ONLY],
        out_specs=_rows(tm, D_MODEL),
        out_shape=jax.ShapeDtypeStruct((s, D_MODEL), F32),
        compiler_params=_cp("parallel"),
    )(yc, ys, yb, g, w, x, tok)


def _fwd_ffn(x, g, wgu, wd, tok):
    s = x.shape[0]
    tm = TOKEN_TILE

    def body(x_ref, g_ref, wgu_ref, wd_ref, tok_ref, gu_ref, o_ref):
        x = x_ref[...]
        r, xh = _rms_stats(x)
        gu = _dot_nt((xh * g_ref[...]).astype(BF16), wgu_ref[...])
        gu_ref[...] = gu
        gate = gu[:, :FFN_H]
        act = gate * _sigmoid(gate) * gu[:, FFN_H:]
        o_ref[...] = x + _dot(act.astype(BF16), wd_ref[...])

    return pl.pallas_call(
        body, name="fwd_ffn", grid=(s // tm,),
        in_specs=[_rows(tm, D_MODEL), _whole((1, D_MODEL)),
                  pl.BlockSpec((2 * FFN_H, D_MODEL), lambda i: (0, 0), pipeline_mode=pl.Buffered(1)),
                  pl.BlockSpec((FFN_H, D_MODEL), lambda i: (0, 0), pipeline_mode=pl.Buffered(1)), ORDER_ONLY],
        out_specs=[_rows(tm, 2 * FFN_H), _rows(tm, D_MODEL)],
        out_shape=[jax.ShapeDtypeStruct((s, 2 * FFN_H), F32), jax.ShapeDtypeStruct((s, D_MODEL), F32)],
        compiler_params=_cp("parallel"),
    )(x, g, wgu, wd, tok)


def _loss_head(y, target):
    s = y.shape[0]
    tm = LIGHT_TILE

    def body(y_ref, t_ref, l_ref, d_ref):
        @pl.when(pl.program_id(0) == 0)
        def _():
            l_ref[...] = jnp.zeros_like(l_ref)

        err = y_ref[...] - t_ref[...]
        d_ref[...] = err * (1.0 / D_MODEL)
        l_ref[...] += 0.5 * jnp.sum(jnp.mean(err * err, axis=-1, keepdims=True), axis=0, keepdims=True)

    return pl.pallas_call(
        body, name="loss_head", grid=(s // tm,),
        in_specs=[_rows(tm, D_MODEL), _rows(tm, D_MODEL)],
        out_specs=[_whole((1, 1)), _rows(tm, D_MODEL)],
        out_shape=[jax.ShapeDtypeStruct((1, 1), F32), jax.ShapeDtypeStruct((s, D_MODEL), F32)],
        compiler_params=_cp("arbitrary"),
    )(y, target)


def _accumulate(ref, value):
    @pl.when(pl.program_id(0) == 0)
    def _():
        ref[...] = jnp.zeros_like(ref)

    ref[...] += value


def _bwd_ffn(dxo, gu, xm, g, wgu, wd, tok):
    s = dxo.shape[0]
    tm = FFN_BWD_TILE

    def body(dxo_ref, gu_ref, xm_ref, g_ref, wgu_ref, wd_ref, tok_ref, dgu_ref, act_ref, h_ref, dxm_ref, dg_ref):
        dxo = dxo_ref[...]
        gu = gu_ref[...]
        gate, up = gu[:, :FFN_H], gu[:, FFN_H:]
        sg = _sigmoid(gate)
        sl = gate * sg
        act_ref[...] = (sl * up).astype(BF16)
        dact = _dot_nt(dxo.astype(BF16), wd_ref[...])
        dgate = dact * up * (sg * (1.0 + gate * (1.0 - sg)))
        dgu = jnp.concatenate([dgate, dact * sl], axis=1).astype(BF16)
        dgu_ref[...] = dgu
        dh = _dot(dgu, wgu_ref[...])
        r, xh = _rms_stats(xm_ref[...])
        h_ref[...] = (xh * g_ref[...]).astype(BF16)
        dx, dgrow = _rms_bwd(xh, r, g_ref[...], dh)
        dxm_ref[...] = dxo + dx
        _accumulate(dg_ref, _colsum(dgrow))

    return pl.pallas_call(
        body, name="bwd_ffn", grid=(s // tm,),
        in_specs=[_rows(tm, D_MODEL), _rows(tm, 2 * FFN_H), _rows(tm, D_MODEL), _whole((1, D_MODEL)),
                  pl.BlockSpec((2 * FFN_H, D_MODEL), lambda i: (0, 0), pipeline_mode=pl.Buffered(1)),
                  pl.BlockSpec((FFN_H, D_MODEL), lambda i: (0, 0), pipeline_mode=pl.Buffered(1)), ORDER_ONLY],
        out_specs=[_rows(tm, 2 * FFN_H), _rows(tm, FFN_H), _rows(tm, D_MODEL), _rows(tm, D_MODEL),
                   _whole((1, D_MODEL))],
        out_shape=[jax.ShapeDtypeStruct((s, 2 * FFN_H), BF16), jax.ShapeDtypeStruct((s, FFN_H), BF16),
                   jax.ShapeDtypeStruct((s, D_MODEL), BF16), jax.ShapeDtypeStruct((s, D_MODEL), F32),
                   jax.ShapeDtypeStruct((1, D_MODEL), F32)],
        compiler_params=_cp("arbitrary"),
    )(dxo, gu, xm, g, wgu, wd, tok)


def _matmul_tn(a, b, tm, tn, out_dtype=BF16):
    s, m = a.shape
    n = b.shape[1]

    def body(a_ref, b_ref, o_ref):
        o_ref[...] = _dot_tn(a_ref[...].astype(BF16), b_ref[...].astype(BF16)).astype(out_dtype)

    return pl.pallas_call(
        body, name="weight_grad", grid=(m // tm, n // tn),
        in_specs=[pl.BlockSpec((s, tm), lambda i, j: (0, i)), pl.BlockSpec((s, tn), lambda i, j: (0, j))],
        out_specs=pl.BlockSpec((tm, tn), lambda i, j: (i, j)),
        out_shape=jax.ShapeDtypeStruct((m, n), out_dtype),
        compiler_params=_cp("parallel", "parallel"),
    )(a, b)


def _bwd_out(dxm, yc, ys, yb, g, w, tok):
    s = dxm.shape[0]
    tm = LIGHT_TILE

    def body(dxm_ref, yc_ref, ys_ref, yb_ref, g_ref, w_ref, tok_ref, dyc_ref, dys_ref, dyb_ref, cat_ref, dg_ref):
        stats = _group_norms(yc_ref[...], ys_ref[...], yb_ref[...])
        g = g_ref[...]
        cat_ref[...] = (jnp.concatenate([xh for _, xh in stats], axis=1) * g).astype(BF16)
        dcat = _dot_nt(dxm_ref[...].astype(BF16), w_ref[...])
        dgs = []
        off = 0
        for (r, xh), out in zip(stats, (dyc_ref, dys_ref, dyb_ref)):
            cols = slice(off, off + xh.shape[1])
            dx, dgrow = _rms_bwd(xh, r, g[:, cols], dcat[:, cols])
            out[...] = dx
            dgs.append(_colsum(dgrow))
            off += xh.shape[1]
        _accumulate(dg_ref, jnp.concatenate(dgs, axis=1))

    return pl.pallas_call(
        body, name="bwd_out", grid=(s // tm,),
        in_specs=[_rows(tm, D_MODEL), _rows(tm, CONV_W), _rows(tm, SG_W), _rows(tm, SB_W),
                  _whole((1, D_MODEL)), _whole((D_MODEL, D_MODEL)), ORDER_ONLY],
        out_specs=[_rows(tm, CONV_W), _rows(tm, SG_W), _rows(tm, SB_W), _rows(tm, D_MODEL),
                   _whole((1, D_MODEL))],
        out_shape=[jax.ShapeDtypeStruct((s, CONV_W), F32), jax.ShapeDtypeStruct((s, SG_W), F32),
                   jax.ShapeDtypeStruct((s, SB_W), F32), jax.ShapeDtypeStruct((s, D_MODEL), BF16),
                   jax.ShapeDtypeStruct((1, D_MODEL), F32)],
        compiler_params=_cp("arbitrary"),
    )(dxm, yc, ys, yb, g, w, tok)


def _bwd_sb(qn, kst, ktr, vtr, dy, ltot, tok):
    s = qn.shape[0]
    np_ = SB_PAIRS

    def body(q_ref, k_ref, kt_ref, vt_ref, do_ref, lt_ref, upto_ref, before_ref, tok_ref, dq_ref, dk_ref, dv_ref,
             z_buf, da_buf, dz_buf, att_buf):
        i = pl.program_id(1)
        first = i * (SB_Q // CHUNK)
        last = first + SB_Q // CHUNK - 1

        @pl.when(i == 0)
        def _():
            dk_ref[...] = jnp.zeros_like(dk_ref)
            dv_ref[...] = jnp.zeros_like(dv_ref)

        lanes = [slice(pr * CHUNK, (pr + 1) * CHUNK) for pr in range(np_)]
        qs = [q_ref[:, lanes[pr]] for pr in range(np_)]
        dos = [do_ref[:, lanes[pr]] for pr in range(np_)]
        dobs = [do.astype(BF16) for do in dos]
        q_ts = [q.astype(F32).T.astype(BF16) for q in qs]
        do_ts = [do.T.astype(BF16) for do in dos]
        ltots = [_per_head(lt_ref[:, pr * CHUNK:pr * CHUNK + 1],
                           lt_ref[:, pr * CHUNK + HEAD_DIM:pr * CHUNK + HEAD_DIM + 1]) for pr in range(np_)]
        upto = upto_ref[...]
        before = before_ref[...]
        last0, last1 = slice(CHUNK - 1, CHUNK), slice(PAIR - 1, PAIR)

        def rows(kb):
            return pl.ds(pl.multiple_of(kb * PAIR, PAIR), PAIR)

        def ahead(pr, kb):
            return _dot(qs[pr], kt_ref[pr, kb]), _dot(dobs[pr], vt_ref[pr, kb])

        def behind(pr, kb, dq):
            dzb = dz_buf[pr]
            dk_ref[pr, kb] += _dot(q_ts[pr], dzb)
            dv_ref[pr, kb] += _dot(do_ts[pr], att_buf[pr])
            return dq + _dot(dzb, k_ref[rows(kb), lanes[pr]])

        def block(kb, carry, masked, top=0):
            out = []
            for pr in range(np_):
                dq, p0, p1, e0, e1 = carry[pr]
                z_next, da_next = ahead(pr, jnp.minimum(kb + 1, last))
                dq = behind(pr, jnp.maximum(kb - 1, 0), dq)
                z, sp, nlb, mask = _sb_scores(z_buf[pr, top:, :], i * SB_Q + top, kb * CHUNK, masked)
                pin = _dot(nlb, upto) + _per_head(p0[top:], p1[top:])
                sig = jnp.exp(z - sp)
                att = jnp.exp(z - sp - (ltots[pr][top:] - pin))
                if masked:
                    att = jnp.where(mask, att, 0.0)
                e = att * da_buf[pr, top:, :]
                ebefore = _dot(e.astype(BF16), before) + _per_head(e0[top:], e1[top:])
                dz = e - sig * (e + ebefore)
                if masked:
                    dz = jnp.where(mask, dz, 0.0)
                z_buf[pr] = z_next
                da_buf[pr] = da_next
                if top:
                    dz_buf[pr, :top, :] = jnp.zeros((top, PAIR), BF16)
                    att_buf[pr, :top, :] = jnp.zeros((top, PAIR), BF16)
                dz_buf[pr, top:, :] = dz.astype(BF16)
                att_buf[pr, top:, :] = att.astype(BF16)
                new = (pin[:, last0], pin[:, last1], ebefore[:, last0] + e[:, last0], ebefore[:, last1] + e[:, last1])
                if top:
                    new = tuple(jnp.concatenate([old[:top], n], axis=0) for old, n in zip((p0, p1, e0, e1), new))
                out.append((dq,) + new)
            return tuple(out)

        for pr in range(np_):
            z_buf[pr], da_buf[pr] = ahead(pr, 0)
        dz_buf[...] = jnp.zeros_like(dz_buf)
        att_buf[...] = jnp.zeros_like(att_buf)
        zero = jnp.zeros((SB_Q, 1), F32)
        carry = ((jnp.zeros((SB_Q, CHUNK), F32), zero, zero, zero, zero),) * np_
        carry = lax.fori_loop(0, first, lambda kb, c: block(kb, c, False), carry)
        for ahead_of in range(SB_Q // CHUNK):
            carry = block(first + ahead_of, carry, True, top=ahead_of * CHUNK)
        for pr in range(np_):
            dq_ref[:, lanes[pr]] = behind(pr, last, carry[pr][0])

    blk = pl.BlockSpec((SB_Q, np_ * CHUNK), lambda p, i: (i, p))
    seq = pl.BlockSpec((2 * s, np_ * CHUNK), lambda p, i: (0, p))
    tiles = pl.BlockSpec((np_, s // CHUNK, CHUNK, PAIR), lambda p, i: (p, 0, 0, 0))
    tri = pl.BlockSpec((PAIR, PAIR), lambda p, i: (0, 0))
    return pl.pallas_call(
        body, name="bwd_sb", grid=(SB_W // CHUNK // np_, s // SB_Q),
        in_specs=[blk, seq, tiles, tiles, blk, blk, tri, tri, ORDER_ONLY],
        out_specs=[blk, tiles, tiles],
        out_shape=[jax.ShapeDtypeStruct((s, SB_W), F32)]
        + [jax.ShapeDtypeStruct((SB_W // CHUNK, s // CHUNK, CHUNK, PAIR), F32)] * 2,
        scratch_shapes=[pltpu.VMEM((np_, SB_Q, PAIR), F32), pltpu.VMEM((np_, SB_Q, PAIR), F32),
                        pltpu.VMEM((np_, SB_Q, PAIR), BF16), pltpu.VMEM((np_, SB_Q, PAIR), BF16)],
        compiler_params=_cp("parallel", "arbitrary"),
    )(qn, kst, ktr, vtr, dy, ltot, _pair_tri("upto"), _pair_tri("before"), tok)


def _head_sum(row):
    acc = row[:, 0:HEAD_DIM]
    for h in range(1, SB_W // HEAD_DIM):
        acc = acc + row[:, h * HEAD_DIM:(h + 1) * HEAD_DIM]
    return acc


def _bwd_qk(proj, dqs, dkn, dv, qg, kg):
    s = proj.shape[0]
    tm = LIGHT_TILE
    tiles = pl.BlockSpec((SB_W // CHUNK, tm // CHUNK, CHUNK, PAIR), lambda i: (0, i, 0, 0))

    def body(q_ref, k_ref, dqs_ref, dkn_ref, dv_ref, qg_ref, kg_ref, dp_ref, dqg_ref, dkg_ref, qacc, kacc):
        i = pl.program_id(0)
        gm = _group_mean_matrix(SB_W, HEAD_DIM)

        def one(x, dy, g, acc):
            r = lax.rsqrt(_dot_split(x * x, gm) + RMS_EPS)
            xh = x * r
            dxh = dy * g
            _accumulate(acc, _colsum(dy * xh))
            return r * (dxh - xh * _dot_split(dxh * xh, gm))

        dq = one(q_ref[...], dqs_ref[...] * (HEAD_DIM ** -0.5), qg_ref[...], qacc)
        dk = one(k_ref[...], _unstack_heads(_load_transposed(dkn_ref)), kg_ref[...], kacc)
        dp_ref[...] = jnp.concatenate([dq, dk, _unstack_heads(_load_transposed(dv_ref))], axis=1).astype(BF16)

        @pl.when(i == pl.num_programs(0) - 1)
        def _():
            dqg_ref[...] = _head_sum(qacc[...])
            dkg_ref[...] = _head_sum(kacc[...])

    return pl.pallas_call(
        body, name="bwd_qk", grid=(s // tm,),
        in_specs=[_rows(tm, SB_W, OFF_SB // SB_W), _rows(tm, SB_W, OFF_SB // SB_W + 1),
                  _rows(tm, SB_W), tiles, tiles, _whole((1, SB_W)), _whole((1, SB_W))],
        out_specs=[_rows(tm, 3 * SB_W), _whole((1, HEAD_DIM)), _whole((1, HEAD_DIM))],
        out_shape=[jax.ShapeDtypeStruct((s, 3 * SB_W), BF16), jax.ShapeDtypeStruct((1, HEAD_DIM), F32),
                   jax.ShapeDtypeStruct((1, HEAD_DIM), F32)],
        scratch_shapes=[pltpu.VMEM((1, SB_W), F32), pltpu.VMEM((1, SB_W), F32)],
        compiler_params=_cp("arbitrary"),
    )(proj, proj, dqs, dkn, dv, qg, kg)


def _bwd_sg(proj, dy, lg, lb, w, bias):
    s = proj.shape[0]
    tm = LIGHT_TILE
    nh = SG_W // HEAD_DIM

    def body(p_ref, dy_ref, lg_ref, lb_ref, w_ref, bias_ref, dp_ref, dlg_ref, dlb_ref, dw_ref, db_ref, dbias):
        i = pl.program_id(0)
        uv = p_ref[...]
        ge = _gelu(uv)
        u = ge[:, :SG_W]
        r, xh = _ln_stats(ge[:, SG_W:])
        vln = (xh * lg_ref[...] + lb_ref[...]).astype(BF16)
        dy = dy_ref[...]
        tril, lane_head = _sg_masks()

        @pl.when(i == 0)
        def _():
            dw_ref[...] = jnp.zeros_like(dw_ref)
            dbias[...] = jnp.zeros_like(dbias)

        dus, dvlns = [], []
        for c in range(tm // CHUNK):
            rows = slice(c * CHUNK, (c + 1) * CHUNK)
            vc = vln[rows]
            dus.append(dy[rows] * _sg_mix(w_ref, bias_ref, vc, tril, lane_head))
            dm = dy[rows] * u[rows]
            dbias[...] += dm
            dvc = jnp.zeros((CHUNK, SG_W), F32)
            for h in range(nh):
                dmh = jnp.where(lane_head == h, dm, 0.0).astype(BF16)
                dw_ref[h] += jnp.where(tril, _dot_nt(dmh, vc), 0.0)
                wm = jnp.where(tril, w_ref[h], 0.0).astype(BF16)
                dvc = dvc + _dot_tn(wm, dmh)
            dvlns.append(dvc)
        du = jnp.concatenate(dus, axis=0)
        dvln = jnp.concatenate(dvlns, axis=0)
        _accumulate(dlg_ref, _colsum(dvln * xh))
        _accumulate(dlb_ref, _colsum(dvln))
        dv = _ln_bwd(xh, r, lg_ref[...], dvln)
        dp_ref[...] = (jnp.concatenate([du, dv], axis=1) * _gelu_grad(uv)).astype(BF16)

        @pl.when(i == pl.num_programs(0) - 1)
        def _():
            lane = lax.broadcasted_iota(jnp.int32, (CHUNK, CHUNK), 1)
            acc = dbias[...]
            out = jnp.zeros((CHUNK, CHUNK), F32)
            for h in range(nh):
                hs = jnp.sum(acc[:, h * HEAD_DIM:(h + 1) * HEAD_DIM], axis=1, keepdims=True)
                out = out + jnp.where(lane == h, hs, 0.0)
            db_ref[...] = out

    return pl.pallas_call(
        body, name="bwd_sg", grid=(s // tm,),
        in_specs=[_rows(tm, 2 * SG_W, 1), _rows(tm, SG_W), _whole((1, SG_W)), _whole((1, SG_W)),
                  _whole((nh, CHUNK, CHUNK)), _whole((CHUNK, SG_W))],
        out_specs=[_rows(tm, 2 * SG_W), _whole((1, SG_W)), _whole((1, SG_W)), _whole((nh, CHUNK, CHUNK)),
                   _whole((CHUNK, CHUNK))],
        out_shape=[jax.ShapeDtypeStruct((s, 2 * SG_W), BF16), jax.ShapeDtypeStruct((1, SG_W), F32),
                   jax.ShapeDtypeStruct((1, SG_W), F32), jax.ShapeDtypeStruct((nh, CHUNK, CHUNK), F32),
                   jax.ShapeDtypeStruct((CHUNK, CHUNK), F32)],
        scratch_shapes=[pltpu.VMEM((CHUNK, SG_W), F32)],
        compiler_params=_cp("arbitrary"),
    )(proj, dy, lg, lb, w, bias)


def _bwd_conv(proj, dy, w, b, lg, lb):
    s = proj.shape[0]

    def body(p_ref, dy_ref, w_ref, b_ref, lg_ref, lb_ref, dp_ref, dw_ref, db_ref, dlg_ref, dlb_ref,
             abuf, dcbuf):
        _glu_fill(p_ref, abuf, s)
        dcbuf[pl.ds(s, CHUNK), :] = jnp.zeros((CHUNK, CONV_W), F32)
        dw_ref[...] = jnp.zeros_like(dw_ref)

        def chunk(c, carry):
            db, dlg, dlb = carry
            r0 = pl.multiple_of(c * CHUNK, CHUNK)
            acc, shifted = _conv_window(abuf, r0, w_ref)
            r, xh = _ln_stats(acc + b_ref[...])
            ln = xh * lg_ref[...] + lb_ref[...]
            sg = _sigmoid(ln)
            dl = dy_ref[pl.ds(r0, CHUNK), :] * (sg * (1.0 + ln * (1.0 - sg)))
            dc = _ln_bwd(xh, r, lg_ref[...], dl)
            dcbuf[pl.ds(r0, CHUNK), :] = dc
            for k in range(CONV_K):
                dw_ref[CONV_K - 1 - k:CONV_K - k, :] += _colsum(dc * shifted(k))
            return db + _colsum(dc), dlg + _colsum(dl * xh), dlb + _colsum(dl)

        zero = jnp.zeros((1, CONV_W), F32)
        db, dlg, dlb = lax.fori_loop(0, s // CHUNK, chunk, (zero, zero, zero))
        db_ref[...] = db
        dlg_ref[...] = dlg
        dlb_ref[...] = dlb

        def chunk_back(c, carry):
            r0 = pl.multiple_of(c * CHUNK, CHUNK)
            shifted = _shifted(dcbuf[pl.ds(r0, CHUNK + 32), :], back=False)
            da = jnp.zeros((CHUNK, CONV_W), F32)
            for k in range(CONV_K):
                da = da + shifted(k) * w_ref[CONV_K - 1 - k:CONV_K - k, :]
            pv = p_ref[pl.ds(r0, CHUNK), :]
            val, sg = pv[:, :CONV_W], _sigmoid(pv[:, CONV_W:])
            dp_ref[pl.ds(r0, CHUNK), :] = jnp.concatenate([da * sg, da * val * sg * (1.0 - sg)], axis=1).astype(BF16)
            return carry

        lax.fori_loop(0, s // CHUNK, chunk_back, 0)

    row = _whole((1, CONV_W))
    return pl.pallas_call(
        body, name="bwd_conv", grid=(1,),
        in_specs=[pl.BlockSpec((s, 2 * CONV_W), lambda i: (0, 0)), _whole((s, CONV_W)),
                  _whole((CONV_K, CONV_W)), row, row, row],
        out_specs=[_whole((s, 2 * CONV_W)), _whole((CONV_K, CONV_W)), row, row, row],
        out_shape=[jax.ShapeDtypeStruct((s, 2 * CONV_W), BF16), jax.ShapeDtypeStruct((CONV_K, CONV_W), F32)]
        + [jax.ShapeDtypeStruct((1, CONV_W), F32)] * 3,
        scratch_shapes=[pltpu.VMEM((s + CHUNK, CONV_W), F32), pltpu.VMEM((s + CHUNK, CONV_W), F32)],
        compiler_params=_cp("arbitrary"),
    )(proj, dy, w, b, lg, lb)


def _bwd_in(dpc, dps, dpb, x, g, w, dxm):
    s = x.shape[0]
    tm = TOKEN_TILE

    def body(dpc_ref, dps_ref, dpb_ref, x_ref, g_ref, w_ref, dxm_ref, dx_ref, h_ref, dp_ref, dg_ref):
        dp = jnp.concatenate([dpc_ref[...], dps_ref[...], dpb_ref[...]], axis=1)
        dp_ref[...] = dp
        dh = _dot(dp, w_ref[...])
        r, xh = _rms_stats(x_ref[...])
        h_ref[...] = (xh * g_ref[...]).astype(BF16)
        dx, dgrow = _rms_bwd(xh, r, g_ref[...], dh)
        dx_ref[...] = dxm_ref[...] + dx
        _accumulate(dg_ref, _colsum(dgrow))

    return pl.pallas_call(
        body, name="bwd_in", grid=(s // tm,),
        in_specs=[_rows(tm, 2 * CONV_W), _rows(tm, 2 * SG_W), _rows(tm, 3 * SB_W), _rows(tm, D_MODEL),
                  _whole((1, D_MODEL)), _whole((IN_W, D_MODEL)), _rows(tm, D_MODEL)],
        out_specs=[_rows(tm, D_MODEL), _rows(tm, D_MODEL), _rows(tm, IN_W), _whole((1, D_MODEL))],
        out_shape=[jax.ShapeDtypeStruct((s, D_MODEL), F32), jax.ShapeDtypeStruct((s, D_MODEL), BF16),
                   jax.ShapeDtypeStruct((s, IN_W), BF16), jax.ShapeDtypeStruct((1, D_MODEL), F32)],
        compiler_params=_cp("arbitrary"),
    )(dpc, dps, dpb, x, g, w, dxm)


SMALL = ("mix_norm_g", "conv_w", "conv_b", "conv_ln_g", "conv_ln_b", "sg_ln_g", "sg_ln_b", "sg_w", "sg_b",
         "q_norm_g", "k_norm_g", "out_norm_g", "ffn_norm_g")
LARGE = ("w_in", "w_out", "w_gate_up", "w_down")


def _row(v):
    return v.reshape(1, -1)


def _layer_params(p, large, l):
    q = {k: v[l] for k, v in p.items()}
    return dict(
        q, **large,
        mix_norm_g=_row(q["mix_norm_g"]), conv_b=_row(q["conv_b"]), conv_ln_g=_row(q["conv_ln_g"]),
        conv_ln_b=_row(q["conv_ln_b"]), sg_ln_g=_row(q["sg_ln_g"]), sg_ln_b=_row(q["sg_ln_b"]),
        out_norm_g=_row(q["out_norm_g"]), ffn_norm_g=_row(q["ffn_norm_g"]),
        qg=_row(jnp.tile(q["q_norm_g"], SB_W // HEAD_DIM)), kg=_row(jnp.tile(q["k_norm_g"], SB_W // HEAD_DIM)),
        sg_bias=jnp.repeat(q["sg_b"].T, HEAD_DIM, axis=1),
    )


def _layer_fwd(x, q, tok, after_in, after_mixers, after_out):
    proj, qn, kn, vb, kt, vt = _fwd_in(x, q["mix_norm_g"], q["w_in"], q["qg"], q["kg"], tok)
    yc = _fwd_conv(proj, q["conv_w"], q["conv_b"], q["conv_ln_g"], q["conv_ln_b"], after_in(proj))
    ys = _fwd_sg(proj, q["sg_ln_g"], q["sg_ln_b"], q["sg_w"], q["sg_bias"])
    yb, lt = _fwd_sb(qn, kt, vb)
    rest, tok = after_mixers(yb)
    q = dict(q, **rest)
    xm = _fwd_out(yc, ys, yb, q["out_norm_g"], q["w_out"], x, tok)
    gu, xo = _fwd_ffn(xm, q["ffn_norm_g"], q["w_gate_up"], q["w_down"], after_out(xm))
    return xo, q, dict(x=x, proj=proj, qn=qn, kn=kn, kt=kt, vt=vt, lt=lt, yc=yc, ys=ys, yb=yb, xm=xm, gu=gu)


def _layer_bwd_ffn(dxo, q, st, tok):
    dgu, act, h2, dxm, d_ffn_g = _bwd_ffn(dxo, st["gu"], st["xm"], q["ffn_norm_g"], q["w_gate_up"], q["w_down"],
                                          tok)
    d_wgu, d_wd = _matmul_tn(dgu, h2, 512, D_MODEL), _matmul_tn(act, dxo, FFN_H // 2, D_MODEL)
    dyc, dys, dyb, cat, d_out_g = _bwd_out(dxm, st["yc"], st["ys"], st["yb"], q["out_norm_g"], q["w_out"], tok)
    return dxm, (dyc, dys, dyb, d_ffn_g, d_out_g), d_wgu, d_wd, _matmul_tn(cat, dxm, 512, D_MODEL)


def _layer_bwd_mix(dxm, carried, q, st, tok):
    dyc, dys, dyb, d_ffn_g, d_out_g = carried
    dqs, dkn, dv = _bwd_sb(st["qn"], st["kn"], st["kt"], st["vt"], dyb, st["lt"], tok)
    dpb, d_qg, d_kg = _bwd_qk(st["proj"], dqs, dkn, dv, q["qg"], q["kg"])
    dps, d_sg_lg, d_sg_lb, d_sg_w, d_sg_b = _bwd_sg(st["proj"], dys, q["sg_ln_g"], q["sg_ln_b"], q["sg_w"],
                                                    q["sg_bias"])
    dpc, d_conv_w, d_conv_b, d_conv_lg, d_conv_lb = _bwd_conv(st["proj"], dyc, q["conv_w"], q["conv_b"],
                                                              q["conv_ln_g"], q["conv_ln_b"])
    dx, h1, dp, d_mix_g = _bwd_in(dpc, dps, dpb, st["x"], q["mix_norm_g"], q["w_in"], dxm)
    d_win = _matmul_tn(dp, h1, 512, D_MODEL)
    small = dict(
        mix_norm_g=d_mix_g[0], conv_w=d_conv_w, conv_b=d_conv_b[0], conv_ln_g=d_conv_lg[0],
        conv_ln_b=d_conv_lb[0], sg_ln_g=d_sg_lg[0], sg_ln_b=d_sg_lb[0], sg_w=d_sg_w,
        sg_b=d_sg_b[:, :SG_W // HEAD_DIM].T, q_norm_g=d_qg[0], k_norm_g=d_kg[0], out_norm_g=d_out_g[0],
        ffn_norm_g=d_ffn_g[0])
    return dx, d_win, small


def _position():
    x, y, c = lax.axis_index("x"), lax.axis_index("y"), lax.axis_index("c")
    return x, y, c


def _flat(px, py, pc):
    return 4 * px + 2 * py + pc


IN_HBM = pl.BlockSpec(memory_space=pltpu.HBM)
IN_SEM = pl.BlockSpec(memory_space=pltpu.SEMAPHORE)
EFFECT = pltpu.SideEffectType.DATAFLOW_SIDE_EFFECTING
COPIES = dict(scatter=7, spread=7, spread_chips=4, **{"pass": 3})


def _exchange_copies(kinds, src_refs, land_refs, send_sems, recv_sems, layer, arrival):
    x, y, c = _position()
    me = _flat(x, y, c)
    everyone = [(x ^ (k >> 2 & 1), y ^ (k >> 1 & 1), c ^ (k & 1)) for k in range(1, N_DEV)]
    sibling = (x, y, 1 - c)
    chips = [(1 - x, y, c), (x, 1 - y, c), (1 - x, 1 - y, c)]
    out = []
    srcs = iter(src_refs)
    for kind, land in zip(kinds, land_refs):
        land = land if layer is None else land.at[layer]
        if kind == "scatter":
            src = next(srcs)
            moves = [(src.at[_flat(*p)], me, _flat(*p), p) for p in everyone]
        elif kind in ("spread", "spread_chips"):
            src = next(srcs)
            moves = [(src, me, _flat(*p), p) for p in (everyone if kind == "spread" else [sibling] + chips)]
        else:
            moves = [(land.at[_flat(*p)], _flat(*p), _flat(p[0], p[1], 1 - c), sibling) for p in chips]
        for src_block, there, here, peer in moves:
            n = len(out)
            out.append(pltpu.make_async_remote_copy(
                src_ref=src_block, dst_ref=land.at[here if arrival else there], send_sem=send_sems.at[n],
                recv_sem=recv_sems.at[n], device_id=peer, device_id_type=MESH))
    return out


def _exchange_start(kinds, srcs, lands, after, name, layer=None):
    ns, n = len(srcs), len(srcs) + len(lands)
    sems = sum(COPIES[k] for k in kinds)

    def body(*refs):
        send_sems, recv_sems = refs[n + 1], refs[n + 2]
        for cp in _exchange_copies(kinds, refs[:ns], refs[ns:n], send_sems, recv_sems, layer, arrival=False):
            cp.start()
        refs[-1][...] = jnp.zeros_like(refs[-1])

    thru = [pltpu.HBM(a.shape, a.dtype) for a in (*srcs, *lands)]
    outs = pl.pallas_call(
        body, name=name,
        out_shape=(pltpu.SemaphoreType.DMA((sems,)), pltpu.SemaphoreType.DMA((sems,)), *thru,
                   jax.ShapeDtypeStruct((8, 128), F32)),
        in_specs=[IN_HBM] * n + [ORDER_ONLY],
        out_specs=(IN_SEM, IN_SEM, *[IN_HBM] * n, pl.BlockSpec(memory_space=pltpu.VMEM)),
        input_output_aliases={i: 2 + i for i in range(n)},
        compiler_params=pltpu.CompilerParams(has_side_effects=EFFECT),
    )(*[pltpu.with_memory_space_constraint(a, pltpu.HBM) for a in (*srcs, *lands)], after)
    return kinds, outs[0], outs[1], list(outs[2:2 + ns]), list(outs[2 + ns:2 + n]), outs[-1]


def _exchange_wait(pending, after, name, layer=None):
    kinds, send_sems, recv_sems, srcs, lands, _ = pending
    after = list(after) if isinstance(after, (list, tuple)) else [after]
    ns, n = len(srcs), len(srcs) + len(lands)

    def body(*refs):
        for cp in _exchange_copies(kinds, refs[:ns], refs[ns:n], refs[n], refs[n + 1], layer, arrival=True):
            cp.wait_send()
            cp.wait_recv()

    thru = [pltpu.HBM(a.shape, a.dtype) for a in (*srcs, *lands)]
    outs = pl.pallas_call(
        body, name=name, out_shape=tuple(thru),
        in_specs=[IN_HBM] * n + [IN_SEM, IN_SEM] + [ORDER_ONLY] * len(after),
        out_specs=tuple([IN_HBM] * n),
        input_output_aliases={i: i for i in range(n)},
        compiler_params=pltpu.CompilerParams(has_side_effects=EFFECT),
    )(*srcs, *lands, send_sems, recv_sems, *after)
    return list(outs[ns:])


def _landing(block, me):
    land = lax.empty((N_DEV,) + block.shape, block.dtype)
    return lax.dynamic_update_index_in_dim(land, block, me, 0)


def _adamw(parts, w, m, v, tr):
    groups, rows, cols = w.shape

    def body(p_ref, w_ref, m_ref, v_ref, g_ref, d_ref, nm_ref, nv_ref):
        g = p_ref[0].astype(F32)
        for j in range(1, N_DEV):
            g = g + p_ref[j].astype(F32)
        g_ref[...] = g
        m = ADAM_B1 * m_ref[...] + (1.0 - ADAM_B1) * g
        v = ADAM_B2 * v_ref[...] + (1.0 - ADAM_B2) * (g * g)
        nm_ref[...] = m
        nv_ref[...] = v
        m_hat = m / (1.0 - ADAM_B1 ** ADAM_STEP)
        v_hat = v / (1.0 - ADAM_B2 ** ADAM_STEP)
        d_ref[...] = -ADAM_LR * (m_hat / (jnp.sqrt(v_hat) + ADAM_EPS) + ADAM_WD * w_ref[...])

    blk = pl.BlockSpec((None, tr, cols), lambda g, i: (g, i, 0))
    return pl.pallas_call(
        body, name="adamw", grid=(groups, rows // tr),
        in_specs=[pl.BlockSpec((None, N_DEV, tr, cols), lambda g, i: (g, 0, i, 0)), blk, blk, blk],
        out_specs=[blk] * 4,
        out_shape=[jax.ShapeDtypeStruct((groups, rows, cols), F32)] * 4,
        compiler_params=_cp("parallel", "parallel"),
    )(parts, w, m, v)


def _row_tile(rows):
    for cand in range(min(rows, 512) // 8 * 8, 7, -8):
        if rows % cand == 0:
            return cand
    return rows


def _with_own_block(land, blocks, layer, me):
    own = lax.dynamic_index_in_dim(blocks, me, 0, keepdims=True)[None]
    return lax.dynamic_update_slice(land, own, (layer, me, 0, 0))


PACK_LANES = 128


def _pack_layers(arrs):
    parts = []
    for a in arrs:
        flat = a.reshape(a.shape[0], -1)
        parts.append(jnp.pad(flat, ((0, 0), (0, -flat.shape[1] % (8 * PACK_LANES)))))
    return jnp.concatenate(parts, axis=1).reshape(arrs[0].shape[0], -1, PACK_LANES)


def _unpack_layers(packed, shapes):
    flat = packed.reshape(packed.shape[0], -1)
    outs, off = [], 0
    for shp in shapes:
        size = 1
        for d in shp[1:]:
            size *= d
        outs.append(flat[:, off:off + size].reshape(shp))
        off += size + (-size % (8 * PACK_LANES))
    return outs


def kernel(x, mix_norm_g, w_in, conv_w, conv_b, conv_ln_g, conv_ln_b, sg_ln_g, sg_ln_b, sg_w, sg_b, q_norm_g, k_norm_g, out_norm_g, w_out, ffn_norm_g, w_gate_up, w_down, loss_target, m_mix_norm_g, m_w_in, m_conv_w, m_conv_b, m_conv_ln_g, m_conv_ln_b, m_sg_ln_g, m_sg_ln_b, m_sg_w, m_sg_b, m_q_norm_g, m_k_norm_g, m_out_norm_g, m_w_out, m_ffn_norm_g, m_w_gate_up, m_w_down, v_mix_norm_g, v_w_in, v_conv_w, v_conv_b, v_conv_ln_g, v_conv_ln_b, v_sg_ln_g, v_sg_ln_b, v_sg_w, v_sg_b, v_q_norm_g, v_k_norm_g, v_out_norm_g, v_w_out, v_ffn_norm_g, v_w_gate_up, v_w_down):
    names = SMALL[:1] + LARGE[:1] + SMALL[1:12] + LARGE[1:2] + SMALL[12:] + LARGE[2:]
    w = dict(mix_norm_g=mix_norm_g, w_in=w_in, conv_w=conv_w, conv_b=conv_b, conv_ln_g=conv_ln_g,
             conv_ln_b=conv_ln_b, sg_ln_g=sg_ln_g, sg_ln_b=sg_ln_b, sg_w=sg_w, sg_b=sg_b, q_norm_g=q_norm_g,
             k_norm_g=k_norm_g, out_norm_g=out_norm_g, w_out=w_out, ffn_norm_g=ffn_norm_g,
             w_gate_up=w_gate_up, w_down=w_down)
    m = dict(mix_norm_g=m_mix_norm_g, w_in=m_w_in, conv_w=m_conv_w, conv_b=m_conv_b, conv_ln_g=m_conv_ln_g,
             conv_ln_b=m_conv_ln_b, sg_ln_g=m_sg_ln_g, sg_ln_b=m_sg_ln_b, sg_w=m_sg_w, sg_b=m_sg_b,
             q_norm_g=m_q_norm_g, k_norm_g=m_k_norm_g, out_norm_g=m_out_norm_g, w_out=m_w_out,
             ffn_norm_g=m_ffn_norm_g, w_gate_up=m_w_gate_up, w_down=m_w_down)
    v = dict(mix_norm_g=v_mix_norm_g, w_in=v_w_in, conv_w=v_conv_w, conv_b=v_conv_b, conv_ln_g=v_conv_ln_g,
             conv_ln_b=v_conv_ln_b, sg_ln_g=v_sg_ln_g, sg_ln_b=v_sg_ln_b, sg_w=v_sg_w, sg_b=v_sg_b,
             q_norm_g=v_q_norm_g, k_norm_g=v_k_norm_g, out_norm_g=v_out_norm_g, w_out=v_w_out,
             ffn_norm_g=v_ffn_norm_g, w_gate_up=v_w_gate_up, w_down=v_w_down)
    xpos, ypos, cpos = _position()
    me = _flat(xpos, ypos, cpos)
    conv_cols = conv_w.shape[-1]
    no_token = jnp.zeros((8, 128), F32)
    w, m, v = (dict(t, w_in=jnp.swapaxes(t["w_in"], 1, 2), w_gate_up=jnp.swapaxes(t["w_gate_up"], 1, 2))
               for t in (w, m, v))
    shards = {k: w[k].astype(BF16) for k in LARGE}
    full_shape = dict(w_in=(IN_W, D_MODEL), w_out=(D_MODEL, D_MODEL), w_gate_up=(2 * FFN_H, D_MODEL),
                      w_down=(FFN_H, D_MODEL))

    def gather_start(srcs, after, tag):
        return _exchange_start(["spread_chips"] * len(srcs), srcs, [_landing(a, me) for a in srcs], after,
                               f"gather_start_{tag}")

    def gather_pass(pending, after, tag):
        lands = _exchange_wait(pending, after, f"gather_wait_{tag}")
        return _exchange_start(["pass"] * len(lands), [], lands, after, f"gather_pass_{tag}")

    def gathered(pending, keys, after, tag):
        lands = _exchange_wait(pending, after, f"gather_passed_{tag}")
        return {k: a.reshape(full_shape[k]) for k, a in zip(keys, lands)}, lands[len(keys):]

    first, later = ("w_in",), ("w_out", "w_gate_up", "w_down")
    act = x[0]
    head = gather_start([shards["w_in"][0], w["conv_w"]], act, "0")
    tail = gather_start([shards[k][0] for k in later], head[5], "0_later")
    head = gather_pass(head, tail[5], "0")
    large, (conv_blocks,) = gathered(head, first, head[5], "0")
    conv_full = jnp.transpose(conv_blocks, (1, 2, 0, 3)).reshape(DEPTH, CONV_K, CONV_W)
    small_w = dict({k: w[k] for k in SMALL}, conv_w=conv_full)
    qs, stash = [], []
    for l in range(DEPTH):
        coming = {}
        more = l + 1 < DEPTH

        def next_start(after):
            coming["first"] = gather_start([shards[k][l + 1] for k in LARGE], after, str(l + 1))
            return coming["first"][5]

        def after_in(proj):
            return next_start(proj) if more and l == 0 else no_token

        def after_mixers(y_sb):
            if l > 0:
                return {}, no_token
            passing = gather_pass(tail, y_sb, "0_later")
            return gathered(passing, later, passing[5], "0_later")[0], passing[5]

        def after_out(x_mid):
            if not more or l == 0:
                return no_token
            coming["second"] = gather_pass(coming["first"], x_mid, str(l + 1))
            return coming["second"][5]

        token = next_start(act) if more and l > 0 else no_token
        act, q, st = _layer_fwd(act, _layer_params(small_w, large, l), token, after_in, after_mixers, after_out)
        qs.append(q)
        stash.append(st)
        if more:
            if l == 0:
                coming["second"] = gather_pass(coming["first"], act, str(l + 1))
            large, _ = gathered(coming["second"], LARGE, act, str(l + 1))

    loss, dx = _loss_head(act, loss_target[0])
    loss = lax.psum(loss[0, 0], ("x", "y", "c"))

    replicated = tuple(k for k in SMALL if k != "conv_w")
    small_rows = _pack_layers([w[k][:1] for k in replicated]).shape[1]
    conv_rows = _pack_layers([conv_full[:1]]).shape[1]
    group_a, group_b = ("w_gate_up", "w_down", "w_out"), ("w_in",)
    blocks = lambda k, a: a.reshape((N_DEV,) + w[k].shape[1:])
    land_a = [lax.empty((DEPTH, N_DEV) + w[k].shape[1:], BF16) for k in group_a]
    land_b = [lax.empty((DEPTH, N_DEV) + w[k].shape[1:], BF16) for k in group_b]
    land_b.append(lax.empty((DEPTH, N_DEV, small_rows + conv_rows, PACK_LANES), F32))
    pend_a = pend_b = None
    token = no_token
    for l in reversed(range(DEPTH)):
        dxm, carried, d_wgu, d_wd, d_wo = _layer_bwd_ffn(dx, qs[l], stash[l], token)
        srcs = [blocks(k, a) for k, a in zip(group_a, (d_wgu, d_wd, d_wo))]
        if pend_a is not None:
            land_a = _exchange_wait(pend_a, d_wo, f"grads_a_wait_{l + 1}", layer=l + 1)
        land_a = [_with_own_block(ld, a, l, me) for ld, a in zip(land_a, srcs)]
        pend_a = _exchange_start(["scatter"] * 3, srcs, land_a, dxm, f"grads_a_start_{l}", layer=l)
        dx, d_win, small = _layer_bwd_mix(dxm, carried, qs[l], stash[l], pend_a[5])
        packed = _pack_layers([small[k][None] for k in replicated + ("conv_w",)])[0]
        srcs = [blocks("w_in", d_win)]
        if pend_b is not None:
            land_b = _exchange_wait(pend_b, d_win, f"grads_b_wait_{l + 1}", layer=l + 1)
        land_b = [_with_own_block(land_b[0], srcs[0], l, me),
                  lax.dynamic_update_slice(land_b[1], packed[None, None], (l, me, 0, 0))]
        pend_b = _exchange_start(["scatter", "spread"], srcs + [packed], land_b, dx, f"grads_b_start_{l}",
                                 layer=l)
        token = pend_b[5]

    land_a = _exchange_wait(pend_a, token, "grads_a_wait_0", layer=0)
    res = {}
    for k, parts in zip(group_a, land_a):
        res[k] = _adamw(parts, w[k], m[k], v[k], _row_tile(w[k].shape[1]))
    land_b = _exchange_wait(pend_b, [res[k][0] for k in group_a], "grads_b_wait_0", layer=0)
    res["w_in"] = _adamw(land_b[0], w["w_in"], m["w_in"], v["w_in"], _row_tile(w["w_in"].shape[1]))
    small_parts = land_b[1]
    updated = _adamw(small_parts, *(_pack_layers([t[k] for k in replicated]) for t in (w, m, v)), small_rows)
    unpacked = [_unpack_layers(o, [w[k].shape for k in replicated]) for o in updated]
    res.update({k: [u[i] for u in unpacked] for i, k in enumerate(replicated)})
    conv_parts = small_parts[:, :, small_rows:].reshape(DEPTH, N_DEV, -1)[:, :, :CONV_K * CONV_W]
    conv_parts = lax.dynamic_slice_in_dim(conv_parts.reshape(DEPTH, N_DEV, CONV_K, CONV_W), me * conv_cols,
                                          conv_cols, axis=3)
    res["conv_w"] = _adamw(conv_parts, w["conv_w"], m["conv_w"], v["conv_w"], CONV_K)
    for k in ("w_in", "w_gate_up"):
        res[k] = [jnp.swapaxes(a, 1, 2) for a in res[k]]

    return (loss, dx[None], *[res[k][0] for k in names], *[res[k][1] for k in names],
            *[res[k][2] for k in names], *[res[k][3] for k in names])
```

```python
import jax
import jax.numpy as jnp
from jax import lax
from jax.experimental import pallas as pl
from jax.experimental.pallas import tpu as pltpu

F32 = jnp.float32
BF16 = jnp.bfloat16

D_MODEL = 1024
DEPTH = 4
HEAD_DIM = 64
CONV_W = 256
SG_W = 256
SB_W = 512
IN_W = 2560
FFN_H = 2816
CONV_K = 31
CHUNK = 128
OFF_SG = 2 * CONV_W
OFF_SB = OFF_SG + 2 * SG_W
RMS_EPS = 1e-6
LN_EPS = 1e-5
N_DEV = 8
MESH = pl.DeviceIdType.MESH

ADAM_LR = 0.001
ADAM_B1 = 0.9
ADAM_B2 = 0.999
ADAM_EPS = 1e-08
ADAM_WD = 0.01
ADAM_STEP = 10

TOKEN_TILE = 512
FFN_BWD_TILE = 256
LIGHT_TILE = 1024
VMEM_LIMIT = 56 * 1024 * 1024


def _cp(*sem):
    return pltpu.CompilerParams(dimension_semantics=sem or None, vmem_limit_bytes=VMEM_LIMIT)


def _dot(a, b):
    return jnp.dot(a, b, preferred_element_type=F32)


def _dot_nt(a, b):
    return lax.dot_general(a, b, (((1,), (1,)), ((), ())), preferred_element_type=F32)


def _dot_tn(a, b):
    return lax.dot_general(a, b, (((0,), (0,)), ((), ())), preferred_element_type=F32)


def _dot_split(x, m):
    hi = x.astype(BF16)
    lo = (x - hi.astype(F32)).astype(BF16)
    return _dot(hi, m) + _dot(lo, m)


def _group_mean_matrix(width, group):
    r = lax.broadcasted_iota(jnp.int32, (width, width), 0) // group
    c = lax.broadcasted_iota(jnp.int32, (width, width), 1) // group
    return jnp.where(r == c, 1.0 / group, 0.0).astype(BF16)


def _sigmoid(x):
    return 1.0 / (1.0 + jnp.exp(-x))


def _gelu(x):
    return 0.5 * x * (1.0 + lax.erf(x * (2.0 ** -0.5)))


def _gelu_grad(x):
    return 0.5 * (1.0 + lax.erf(x * (2.0 ** -0.5))) + x * jnp.exp(-0.5 * x * x) * (0.5 * (2.0 / jnp.pi) ** 0.5)


def _rms_stats(x):
    r = lax.rsqrt(jnp.mean(x * x, axis=-1, keepdims=True) + RMS_EPS)
    return r, x * r


def _rms_bwd(xh, r, g, dy):
    dxh = dy * g
    dx = r * (dxh - xh * jnp.mean(dxh * xh, axis=-1, keepdims=True))
    return dx, dy * xh


def _ln_stats(x):
    mu = jnp.mean(x, axis=-1, keepdims=True)
    xc = x - mu
    r = lax.rsqrt(jnp.mean(xc * xc, axis=-1, keepdims=True) + LN_EPS)
    return r, xc * r


def _ln_bwd(xh, r, g, dy):
    dxh = dy * g
    return r * (dxh - jnp.mean(dxh, axis=-1, keepdims=True) - xh * jnp.mean(dxh * xh, axis=-1, keepdims=True))


def _colsum(x):
    return jnp.sum(x, axis=0, keepdims=True)


def _rows(tm, n, j=0):
    return pl.BlockSpec((tm, n), lambda i: (i, j))


def _whole(shape):
    return pl.BlockSpec(shape, lambda i: (0,) * len(shape))


ORDER_ONLY = pl.BlockSpec(memory_space=pl.ANY)


def _stack_heads(a):
    even = (lax.broadcasted_iota(jnp.int32, a.shape, 1) % (2 * HEAD_DIM)) < HEAD_DIM
    top = jnp.where(even, a, 0.0)
    bot = jnp.where(even, 0.0, a)
    parts = []
    for c in range(a.shape[0] // CHUNK):
        rows = slice(c * CHUNK, (c + 1) * CHUNK)
        parts += [top[rows], bot[rows]]
    return jnp.concatenate(parts, axis=0)


def _store_stacked(st, st_ref, tr_ref):
    st_ref[...] = st.astype(BF16)
    for p in range(SB_W // CHUNK):
        for c in range(st.shape[0] // (2 * CHUNK)):
            tile = st[2 * c * CHUNK:2 * (c + 1) * CHUNK, p * CHUNK:(p + 1) * CHUNK]
            tr_ref[p, c] = tile.T.astype(BF16)


def _load_transposed(tr_ref):
    rows = []
    for c in range(tr_ref.shape[1]):
        tiles = [tr_ref[p, c].T for p in range(SB_W // CHUNK)]
        rows.append(jnp.concatenate(tiles, axis=1))
    return jnp.concatenate(rows, axis=0)


def _unstack_heads(st):
    even = (lax.broadcasted_iota(jnp.int32, (CHUNK, st.shape[1]), 1) % (2 * HEAD_DIM)) < HEAD_DIM
    parts = []
    for c in range(st.shape[0] // (2 * CHUNK)):
        top = st[2 * c * CHUNK:(2 * c + 1) * CHUNK]
        bot = st[(2 * c + 1) * CHUNK:(2 * c + 2) * CHUNK]
        parts.append(jnp.where(even, top, bot))
    return jnp.concatenate(parts, axis=0)


def _fwd_in(x, g, w, qg, kg, tok):
    s = x.shape[0]
    tm = TOKEN_TILE

    def body(x_ref, g_ref, w_ref, qg_ref, kg_ref, tok_ref, proj_ref, qn_ref, kn_ref, vb_ref, kt_ref, vt_ref):
        r, xh = _rms_stats(x_ref[...])
        h = (xh * g_ref[...]).astype(BF16)
        proj = _dot_nt(h, w_ref[...])
        proj_ref[...] = proj
        gm = _group_mean_matrix(SB_W, HEAD_DIM)
        q = proj[:, OFF_SB:OFF_SB + SB_W]
        k = proj[:, OFF_SB + SB_W:OFF_SB + 2 * SB_W]
        rq = lax.rsqrt(_dot_split(q * q, gm) + RMS_EPS)
        rk = lax.rsqrt(_dot_split(k * k, gm) + RMS_EPS)
        qn_ref[...] = (q * rq * qg_ref[...] * (HEAD_DIM ** -0.5)).astype(BF16)
        _store_stacked(_stack_heads(k * rk * kg_ref[...]), kn_ref, kt_ref)
        _store_stacked(_stack_heads(proj[:, OFF_SB + 2 * SB_W:]), vb_ref, vt_ref)

    tiles = pl.BlockSpec((SB_W // CHUNK, tm // CHUNK, CHUNK, PAIR), lambda i: (0, i, 0, 0))
    tiles_shape = jax.ShapeDtypeStruct((SB_W // CHUNK, s // CHUNK, CHUNK, PAIR), BF16)
    return pl.pallas_call(
        body, name="fwd_in", grid=(s // tm,),
        in_specs=[_rows(tm, D_MODEL), _whole((1, D_MODEL)), _whole((IN_W, D_MODEL)),
                  _whole((1, SB_W)), _whole((1, SB_W)), ORDER_ONLY],
        out_specs=[_rows(tm, IN_W), _rows(tm, SB_W), _rows(2 * tm, SB_W), _rows(2 * tm, SB_W), tiles, tiles],
        out_shape=[jax.ShapeDtypeStruct((s, IN_W), F32), jax.ShapeDtypeStruct((s, SB_W), BF16),
                   jax.ShapeDtypeStruct((2 * s, SB_W), BF16), jax.ShapeDtypeStruct((2 * s, SB_W), BF16),
                   tiles_shape, tiles_shape],
        compiler_params=_cp("parallel"),
    )(x, g, w, qg, kg, tok)


SUBLANES = 8


def _shifted(win, back):
    n = win.shape[0]
    turned = [win] + [pltpu.roll(win, b if back else n - b, axis=0) for b in range(1, SUBLANES)]

    def shifted(k):
        whole, part = divmod(k, SUBLANES)
        start = 32 - whole * SUBLANES if back else whole * SUBLANES
        return turned[part][start:start + CHUNK, :]

    return shifted


def _conv_window(abuf, r0, w_ref):
    shifted = _shifted(abuf[pl.ds(pl.multiple_of(r0 + CHUNK - 32, 32), CHUNK + 32), :], back=True)
    acc = jnp.zeros((CHUNK, CONV_W), F32)
    for k in range(CONV_K):
        acc = acc + shifted(k) * w_ref[CONV_K - 1 - k:CONV_K - k, :]
    return acc, shifted


def _glu_fill(p_ref, abuf, s):
    abuf[0:CHUNK, :] = jnp.zeros((CHUNK, CONV_W), F32)

    def fill(c, carry):
        r0 = pl.multiple_of(c * CHUNK, CHUNK)
        pv = p_ref[pl.ds(r0, CHUNK), :]
        abuf[pl.ds(r0 + CHUNK, CHUNK), :] = pv[:, :CONV_W] * _sigmoid(pv[:, CONV_W:])
        return carry

    lax.fori_loop(0, s // CHUNK, fill, 0)


def _fwd_conv(proj, w, b, lg, lb, tok):
    s = proj.shape[0]

    def body(p_ref, w_ref, b_ref, lg_ref, lb_ref, tok_ref, y_ref, abuf):
        _glu_fill(p_ref, abuf, s)

        def chunk(c, carry):
            r0 = pl.multiple_of(c * CHUNK, CHUNK)
            acc, _ = _conv_window(abuf, r0, w_ref)
            r, xh = _ln_stats(acc + b_ref[...])
            ln = xh * lg_ref[...] + lb_ref[...]
            y_ref[pl.ds(r0, CHUNK), :] = ln * _sigmoid(ln)
            return carry

        lax.fori_loop(0, s // CHUNK, chunk, 0)

    return pl.pallas_call(
        body, name="fwd_conv", grid=(1,),
        in_specs=[pl.BlockSpec((s, 2 * CONV_W), lambda i: (0, 0)), _whole((CONV_K, CONV_W)),
                  _whole((1, CONV_W)), _whole((1, CONV_W)), _whole((1, CONV_W)), ORDER_ONLY],
        out_specs=_whole((s, CONV_W)),
        out_shape=jax.ShapeDtypeStruct((s, CONV_W), F32),
        scratch_shapes=[pltpu.VMEM((s + CHUNK, CONV_W), F32)],
        compiler_params=_cp("arbitrary"),
    )(proj, w, b, lg, lb, tok)


def _sg_masks():
    row = lax.broadcasted_iota(jnp.int32, (CHUNK, CHUNK), 0)
    col = lax.broadcasted_iota(jnp.int32, (CHUNK, CHUNK), 1)
    lane_head = lax.broadcasted_iota(jnp.int32, (CHUNK, SG_W), 1) // HEAD_DIM
    return row >= col, lane_head


def _sg_mix(w_ref, bias_ref, vc, tril, lane_head):
    mixed = bias_ref[...]
    for h in range(SG_W // HEAD_DIM):
        wm = jnp.where(tril, w_ref[h], 0.0).astype(BF16)
        mixed = mixed + jnp.where(lane_head == h, _dot(wm, vc), 0.0)
    return mixed


def _fwd_sg(proj, lg, lb, w, bias):
    s = proj.shape[0]
    tm = LIGHT_TILE

    def body(p_ref, lg_ref, lb_ref, w_ref, bias_ref, y_ref):
        ge = _gelu(p_ref[...])
        u = ge[:, :SG_W]
        r, xh = _ln_stats(ge[:, SG_W:])
        vln = (xh * lg_ref[...] + lb_ref[...]).astype(BF16)
        tril, lane_head = _sg_masks()
        for c in range(tm // CHUNK):
            rows = slice(c * CHUNK, (c + 1) * CHUNK)
            y_ref[rows, :] = u[rows] * _sg_mix(w_ref, bias_ref, vln[rows], tril, lane_head)

    return pl.pallas_call(
        body, name="fwd_sg", grid=(s // tm,),
        in_specs=[_rows(tm, 2 * SG_W, 1), _whole((1, SG_W)), _whole((1, SG_W)),
                  _whole((SG_W // HEAD_DIM, CHUNK, CHUNK)), _whole((CHUNK, SG_W))],
        out_specs=_rows(tm, SG_W),
        out_shape=jax.ShapeDtypeStruct((s, SG_W), F32),
        compiler_params=_cp("parallel"),
    )(proj, lg, lb, w, bias)


SB_Q_FWD = 16 * CHUNK
SB_Q_BWD = 8 * CHUNK
PAIR = 2 * CHUNK
SB_PAIRS = 1


def _pair_tri(kind):
    row = lax.broadcasted_iota(jnp.int32, (PAIR, PAIR), 0)
    col = lax.broadcasted_iota(jnp.int32, (PAIR, PAIR), 1)
    tri = {"after": row > col, "upto": row <= col, "before": row < col}[kind]
    return jnp.where(((row // CHUNK) == (col // CHUNK)) & tri, 1.0, 0.0).astype(BF16)


def _sb_scores(z, qpos0, kpos0, masked):
    sp = jnp.maximum(z, 0.0) + jnp.log(1.0 + jnp.exp(-jnp.abs(z)))
    if not masked:
        return z, sp, sp.astype(BF16), None
    row = lax.broadcasted_iota(jnp.int32, z.shape, 0)
    col = lax.broadcasted_iota(jnp.int32, z.shape, 1) % CHUNK
    mask = (kpos0 + col) < (qpos0 + row)
    return z, sp, jnp.where(mask, sp, 0.0).astype(BF16), mask


def _per_head(c0, c1):
    rows = c0.shape[0]
    return jnp.concatenate([jnp.broadcast_to(c0, (rows, CHUNK)), jnp.broadcast_to(c1, (rows, CHUNK))], axis=1)


def _fwd_sb(qn, ktr, vst):
    s = qn.shape[0]
    SB_Q = min(SB_Q_FWD, s)
    np_ = SB_PAIRS

    def body(q_ref, k_ref, v_ref, after_ref, o_ref, lt_ref, z_buf, att_buf):
        i = pl.program_id(1)
        first = i * (SB_Q // CHUNK)
        last = first + SB_Q // CHUNK - 1
        after = after_ref[...]
        lanes = [slice(pr * CHUNK, (pr + 1) * CHUNK) for pr in range(np_)]
        qs = [q_ref[:, lanes[pr]] for pr in range(np_)]

        def rows(kb):
            return pl.ds(pl.multiple_of(kb * PAIR, PAIR), PAIR)

        def block(kb, carry, masked, top=0):
            out = []
            for pr in range(np_):
                acc, c0, c1 = carry[pr]
                z_next = _dot(qs[pr], k_ref[pr, jnp.maximum(kb - 1, 0)])
                pv = _dot(att_buf[pr], v_ref[rows(jnp.minimum(kb + 1, last)), lanes[pr]])
                z, sp, nlb, mask = _sb_scores(z_buf[pr, top:, :], i * SB_Q + top, kb * CHUNK, masked)
                loc = _dot(nlb, after)
                att = jnp.exp(z - sp - loc - _per_head(c0[top:], c1[top:]))
                if masked:
                    att = jnp.where(mask, att, 0.0)
                z_buf[pr] = z_next
                if top:
                    att_buf[pr, :top, :] = jnp.zeros((top, PAIR), BF16)
                att_buf[pr, top:, :] = att.astype(BF16)
                add0 = loc[:, 0:1] + nlb[:, 0:1].astype(F32)
                add1 = loc[:, CHUNK:CHUNK + 1] + nlb[:, CHUNK:CHUNK + 1].astype(F32)
                if top:
                    add0 = jnp.concatenate([jnp.zeros((top, 1), F32), add0], axis=0)
                    add1 = jnp.concatenate([jnp.zeros((top, 1), F32), add1], axis=0)
                out.append((acc + pv, c0 + add0, c1 + add1))
            return tuple(out)

        for pr in range(np_):
            z_buf[pr] = _dot(qs[pr], k_ref[pr, last])
        att_buf[...] = jnp.zeros_like(att_buf)
        zero = jnp.zeros((SB_Q, 1), F32)
        carry = ((jnp.zeros((SB_Q, CHUNK), F32), zero, zero),) * np_
        for back in range(SB_Q // CHUNK):
            carry = block(last - back, carry, True, top=SB_Q - (back + 1) * CHUNK)
        carry = lax.fori_loop(0, first, lambda j, c: block(first - 1 - j, c, False), carry)
        for pr, (acc, c0, c1) in enumerate(carry):
            o_ref[:, lanes[pr]] = acc + _dot(att_buf[pr], v_ref[rows(0), lanes[pr]])
            lt_ref[:, lanes[pr]] = jnp.concatenate([jnp.broadcast_to(c0, (SB_Q, HEAD_DIM)),
                                                    jnp.broadcast_to(c1, (SB_Q, HEAD_DIM))], axis=1)

    blk = pl.BlockSpec((SB_Q, np_ * CHUNK), lambda p, i: (i, p))
    seq = pl.BlockSpec((2 * s, np_ * CHUNK), lambda p, i: (0, p))
    return pl.pallas_call(
        body, name="fwd_sb", grid=(SB_W // CHUNK // np_, s // SB_Q),
        in_specs=[blk, pl.BlockSpec((np_, s // CHUNK, CHUNK, PAIR), lambda p, i: (p, 0, 0, 0)), seq,
                  pl.BlockSpec((PAIR, PAIR), lambda p, i: (0, 0))],
        out_specs=[blk, blk],
        out_shape=[jax.ShapeDtypeStruct((s, SB_W), F32)] * 2,
        scratch_shapes=[pltpu.VMEM((np_, SB_Q, PAIR), F32), pltpu.VMEM((np_, SB_Q, PAIR), BF16)],
        compiler_params=_cp("parallel", "parallel"),
    )(qn, ktr, vst, _pair_tri("after"))


def _group_norms(yc, ys, yb):
    return [_rms_stats(yc), _rms_stats(ys), _rms_stats(yb)]


def _fwd_out(yc, ys, yb, g, w, x, tok):
    s = x.shape[0]
    tm = LIGHT_TILE

    def body(yc_ref, ys_ref, yb_ref, g_ref, w_ref, x_ref, tok_ref, o_ref):
        stats = _group_norms(yc_ref[...], ys_ref[...], yb_ref[...])
        cat = jnp.concatenate([xh for _, xh in stats], axis=1) * g_ref[...]
        o_ref[...] = x_ref[...] + _dot(cat.astype(BF16), w_ref[...])

    return pl.pallas_call(
        body, name="fwd_out", grid=(s // tm,),
        in_specs=[_rows(tm, CONV_W), _rows(tm, SG_W), _rows(tm, SB_W), _whole((1, D_MODEL)),
                  _whole((D_MODEL, D_MODEL)), _rows(tm, D_MODEL), ORDER_ONLY],
        out_specs=_rows(tm, D_MODEL),
        out_shape=jax.ShapeDtypeStruct((s, D_MODEL), F32),
        compiler_params=_cp("parallel"),
    )(yc, ys, yb, g, w, x, tok)


def _fwd_ffn(x, g, wgu, wd, tok):
    s = x.shape[0]
    tm = TOKEN_TILE

    def body(x_ref, g_ref, wgu_ref, wd_ref, tok_ref, gu_ref, o_ref):
        x = x_ref[...]
        r, xh = _rms_stats(x)
        gu = _dot_nt((xh * g_ref[...]).astype(BF16), wgu_ref[...])
        gu_ref[...] = gu
        gate = gu[:, :FFN_H]
        act = gate * _sigmoid(gate) * gu[:, FFN_H:]
        o_ref[...] = x + _dot(act.astype(BF16), wd_ref[...])

    return pl.pallas_call(
        body, name="fwd_ffn", grid=(s // tm,),
        in_specs=[_rows(tm, D_MODEL), _whole((1, D_MODEL)),
                  pl.BlockSpec((2 * FFN_H, D_MODEL), lambda i: (0, 0), pipeline_mode=pl.Buffered(1)),
                  pl.BlockSpec((FFN_H, D_MODEL), lambda i: (0, 0), pipeline_mode=pl.Buffered(1)), ORDER_ONLY],
        out_specs=[_rows(tm, 2 * FFN_H), _rows(tm, D_MODEL)],
        out_shape=[jax.ShapeDtypeStruct((s, 2 * FFN_H), F32), jax.ShapeDtypeStruct((s, D_MODEL), F32)],
        compiler_params=_cp("parallel"),
    )(x, g, wgu, wd, tok)


def _loss_head(y, target):
    s = y.shape[0]
    tm = LIGHT_TILE

    def body(y_ref, t_ref, l_ref, d_ref):
        @pl.when(pl.program_id(0) == 0)
        def _():
            l_ref[...] = jnp.zeros_like(l_ref)

        err = y_ref[...] - t_ref[...]
        d_ref[...] = err * (1.0 / D_MODEL)
        l_ref[...] += 0.5 * jnp.sum(jnp.mean(err * err, axis=-1, keepdims=True), axis=0, keepdims=True)

    return pl.pallas_call(
        body, name="loss_head", grid=(s // tm,),
        in_specs=[_rows(tm, D_MODEL), _rows(tm, D_MODEL)],
        out_specs=[_whole((1, 1)), _rows(tm, D_MODEL)],
        out_shape=[jax.ShapeDtypeStruct((1, 1), F32), jax.ShapeDtypeStruct((s, D_MODEL), F32)],
        compiler_params=_cp("arbitrary"),
    )(y, target)


def _accumulate(ref, value):
    @pl.when(pl.program_id(0) == 0)
    def _():
        ref[...] = jnp.zeros_like(ref)

    ref[...] += value


def _bwd_ffn(dxo, gu, xm, g, wgu, wd, tok):
    s = dxo.shape[0]
    tm = FFN_BWD_TILE

    def body(dxo_ref, gu_ref, xm_ref, g_ref, wgu_ref, wd_ref, tok_ref, dgu_ref, act_ref, h_ref, dxm_ref, dg_ref):
        dxo = dxo_ref[...]
        gu = gu_ref[...]
        gate, up = gu[:, :FFN_H], gu[:, FFN_H:]
        sg = _sigmoid(gate)
        sl = gate * sg
        act_ref[...] = (sl * up).astype(BF16)
        dact = _dot_nt(dxo.astype(BF16), wd_ref[...])
        dgate = dact * up * (sg * (1.0 + gate * (1.0 - sg)))
        dgu = jnp.concatenate([dgate, dact * sl], axis=1).astype(BF16)
        dgu_ref[...] = dgu
        dh = _dot(dgu, wgu_ref[...])
        r, xh = _rms_stats(xm_ref[...])
        h_ref[...] = (xh * g_ref[...]).astype(BF16)
        dx, dgrow = _rms_bwd(xh, r, g_ref[...], dh)
        dxm_ref[...] = dxo + dx
        _accumulate(dg_ref, _colsum(dgrow))

    return pl.pallas_call(
        body, name="bwd_ffn", grid=(s // tm,),
        in_specs=[_rows(tm, D_MODEL), _rows(tm, 2 * FFN_H), _rows(tm, D_MODEL), _whole((1, D_MODEL)),
                  pl.BlockSpec((2 * FFN_H, D_MODEL), lambda i: (0, 0), pipeline_mode=pl.Buffered(1)),
                  pl.BlockSpec((FFN_H, D_MODEL), lambda i: (0, 0), pipeline_mode=pl.Buffered(1)), ORDER_ONLY],
        out_specs=[_rows(tm, 2 * FFN_H), _rows(tm, FFN_H), _rows(tm, D_MODEL), _rows(tm, D_MODEL),
                   _whole((1, D_MODEL))],
        out_shape=[jax.ShapeDtypeStruct((s, 2 * FFN_H), BF16), jax.ShapeDtypeStruct((s, FFN_H), BF16),
                   jax.ShapeDtypeStruct((s, D_MODEL), BF16), jax.ShapeDtypeStruct((s, D_MODEL), F32),
                   jax.ShapeDtypeStruct((1, D_MODEL), F32)],
        compiler_params=_cp("arbitrary"),
    )(dxo, gu, xm, g, wgu, wd, tok)


def _matmul_tn(a, b, tm, tn, out_dtype=BF16):
    s, m = a.shape
    n = b.shape[1]

    def body(a_ref, b_ref, o_ref):
        o_ref[...] = _dot_tn(a_ref[...].astype(BF16), b_ref[...].astype(BF16)).astype(out_dtype)

    return pl.pallas_call(
        body, name="weight_grad", grid=(m // tm, n // tn),
        in_specs=[pl.BlockSpec((s, tm), lambda i, j: (0, i)), pl.BlockSpec((s, tn), lambda i, j: (0, j))],
        out_specs=pl.BlockSpec((tm, tn), lambda i, j: (i, j)),
        out_shape=jax.ShapeDtypeStruct((m, n), out_dtype),
        compiler_params=_cp("parallel", "parallel"),
    )(a, b)


def _bwd_out(dxm, yc, ys, yb, g, w, tok):
    s = dxm.shape[0]
    tm = LIGHT_TILE

    def body(dxm_ref, yc_ref, ys_ref, yb_ref, g_ref, w_ref, tok_ref, dyc_ref, dys_ref, dyb_ref, cat_ref, dg_ref):
        stats = _group_norms(yc_ref[...], ys_ref[...], yb_ref[...])
        g = g_ref[...]
        cat_ref[...] = (jnp.concatenate([xh for _, xh in stats], axis=1) * g).astype(BF16)
        dcat = _dot_nt(dxm_ref[...].astype(BF16), w_ref[...])
        dgs = []
        off = 0
        for (r, xh), out in zip(stats, (dyc_ref, dys_ref, dyb_ref)):
            cols = slice(off, off + xh.shape[1])
            dx, dgrow = _rms_bwd(xh, r, g[:, cols], dcat[:, cols])
            out[...] = dx
            dgs.append(_colsum(dgrow))
            off += xh.shape[1]
        _accumulate(dg_ref, jnp.concatenate(dgs, axis=1))

    return pl.pallas_call(
        body, name="bwd_out", grid=(s // tm,),
        in_specs=[_rows(tm, D_MODEL), _rows(tm, CONV_W), _rows(tm, SG_W), _rows(tm, SB_W),
                  _whole((1, D_MODEL)), _whole((D_MODEL, D_MODEL)), ORDER_ONLY],
        out_specs=[_rows(tm, CONV_W), _rows(tm, SG_W), _rows(tm, SB_W), _rows(tm, D_MODEL),
                   _whole((1, D_MODEL))],
        out_shape=[jax.ShapeDtypeStruct((s, CONV_W), F32), jax.ShapeDtypeStruct((s, SG_W), F32),
                   jax.ShapeDtypeStruct((s, SB_W), F32), jax.ShapeDtypeStruct((s, D_MODEL), BF16),
                   jax.ShapeDtypeStruct((1, D_MODEL), F32)],
        compiler_params=_cp("arbitrary"),
    )(dxm, yc, ys, yb, g, w, tok)


def _bwd_sb(qn, kst, ktr, vtr, dy, ltot, tok):
    s = qn.shape[0]
    SB_Q = min(SB_Q_BWD, s)
    np_ = SB_PAIRS

    def body(q_ref, k_ref, kt_ref, vt_ref, do_ref, lt_ref, upto_ref, before_ref, tok_ref, dq_ref, dk_ref, dv_ref,
             z_buf, da_buf, dz_buf, att_buf):
        i = pl.program_id(1)
        first = i * (SB_Q // CHUNK)
        last = first + SB_Q // CHUNK - 1

        @pl.when(i == 0)
        def _():
            dk_ref[...] = jnp.zeros_like(dk_ref)
            dv_ref[...] = jnp.zeros_like(dv_ref)

        lanes = [slice(pr * CHUNK, (pr + 1) * CHUNK) for pr in range(np_)]
        qs = [q_ref[:, lanes[pr]] for pr in range(np_)]
        dos = [do_ref[:, lanes[pr]] for pr in range(np_)]
        dobs = [do.astype(BF16) for do in dos]
        q_ts = [q.astype(F32).T.astype(BF16) for q in qs]
        do_ts = [do.T.astype(BF16) for do in dos]
        ltots = [_per_head(lt_ref[:, pr * CHUNK:pr * CHUNK + 1],
                           lt_ref[:, pr * CHUNK + HEAD_DIM:pr * CHUNK + HEAD_DIM + 1]) for pr in range(np_)]
        upto = upto_ref[...]
        before = before_ref[...]
        last0, last1 = slice(CHUNK - 1, CHUNK), slice(PAIR - 1, PAIR)

        def rows(kb):
            return pl.ds(pl.multiple_of(kb * PAIR, PAIR), PAIR)

        def ahead(pr, kb):
            return _dot(qs[pr], kt_ref[pr, kb]), _dot(dobs[pr], vt_ref[pr, kb])

        def behind(pr, kb, dq):
            dzb = dz_buf[pr]
            dk_ref[pr, kb] += _dot(q_ts[pr], dzb)
            dv_ref[pr, kb] += _dot(do_ts[pr], att_buf[pr])
            return dq + _dot(dzb, k_ref[rows(kb), lanes[pr]])

        def block(kb, carry, masked, top=0):
            out = []
            for pr in range(np_):
                dq, p0, p1, e0, e1 = carry[pr]
                z_next, da_next = ahead(pr, jnp.minimum(kb + 1, last))
                dq = behind(pr, jnp.maximum(kb - 1, 0), dq)
                z, sp, nlb, mask = _sb_scores(z_buf[pr, top:, :], i * SB_Q + top, kb * CHUNK, masked)
                pin = _dot(nlb, upto) + _per_head(p0[top:], p1[top:])
                sig = jnp.exp(z - sp)
                att = jnp.exp(z - sp - (ltots[pr][top:] - pin))
                if masked:
                    att = jnp.where(mask, att, 0.0)
                e = att * da_buf[pr, top:, :]
                ebefore = _dot(e.astype(BF16), before) + _per_head(e0[top:], e1[top:])
                dz = e - sig * (e + ebefore)
                if masked:
                    dz = jnp.where(mask, dz, 0.0)
                z_buf[pr] = z_next
                da_buf[pr] = da_next
                if top:
                    dz_buf[pr, :top, :] = jnp.zeros((top, PAIR), BF16)
                    att_buf[pr, :top, :] = jnp.zeros((top, PAIR), BF16)
                dz_buf[pr, top:, :] = dz.astype(BF16)
                att_buf[pr, top:, :] = att.astype(BF16)
                new = (pin[:, last0], pin[:, last1], ebefore[:, last0] + e[:, last0], ebefore[:, last1] + e[:, last1])
                if top:
                    new = tuple(jnp.concatenate([old[:top], n], axis=0) for old, n in zip((p0, p1, e0, e1), new))
                out.append((dq,) + new)
            return tuple(out)

        for pr in range(np_):
            z_buf[pr], da_buf[pr] = ahead(pr, 0)
        dz_buf[...] = jnp.zeros_like(dz_buf)
        att_buf[...] = jnp.zeros_like(att_buf)
        zero = jnp.zeros((SB_Q, 1), F32)
        carry = ((jnp.zeros((SB_Q, CHUNK), F32), zero, zero, zero, zero),) * np_
        carry = lax.fori_loop(0, first, lambda kb, c: block(kb, c, False), carry)
        for ahead_of in range(SB_Q // CHUNK):
            carry = block(first + ahead_of, carry, True, top=ahead_of * CHUNK)
        for pr in range(np_):
            dq_ref[:, lanes[pr]] = behind(pr, last, carry[pr][0])

    blk = pl.BlockSpec((SB_Q, np_ * CHUNK), lambda p, i: (i, p))
    seq = pl.BlockSpec((2 * s, np_ * CHUNK), lambda p, i: (0, p))
    tiles = pl.BlockSpec((np_, s // CHUNK, CHUNK, PAIR), lambda p, i: (p, 0, 0, 0))
    tri = pl.BlockSpec((PAIR, PAIR), lambda p, i: (0, 0))
    return pl.pallas_call(
        body, name="bwd_sb", grid=(SB_W // CHUNK // np_, s // SB_Q),
        in_specs=[blk, seq, tiles, tiles, blk, blk, tri, tri, ORDER_ONLY],
        out_specs=[blk, tiles, tiles],
        out_shape=[jax.ShapeDtypeStruct((s, SB_W), F32)]
        + [jax.ShapeDtypeStruct((SB_W // CHUNK, s // CHUNK, CHUNK, PAIR), F32)] * 2,
        scratch_shapes=[pltpu.VMEM((np_, SB_Q, PAIR), F32), pltpu.VMEM((np_, SB_Q, PAIR), F32),
                        pltpu.VMEM((np_, SB_Q, PAIR), BF16), pltpu.VMEM((np_, SB_Q, PAIR), BF16)],
        compiler_params=_cp("parallel", "arbitrary"),
    )(qn, kst, ktr, vtr, dy, ltot, _pair_tri("upto"), _pair_tri("before"), tok)


def _head_sum(row):
    acc = row[:, 0:HEAD_DIM]
    for h in range(1, SB_W // HEAD_DIM):
        acc = acc + row[:, h * HEAD_DIM:(h + 1) * HEAD_DIM]
    return acc


def _bwd_qk(proj, dqs, dkn, dv, qg, kg):
    s = proj.shape[0]
    tm = LIGHT_TILE
    tiles = pl.BlockSpec((SB_W // CHUNK, tm // CHUNK, CHUNK, PAIR), lambda i: (0, i, 0, 0))

    def body(q_ref, k_ref, dqs_ref, dkn_ref, dv_ref, qg_ref, kg_ref, dp_ref, dqg_ref, dkg_ref, qacc, kacc):
        i = pl.program_id(0)
        gm = _group_mean_matrix(SB_W, HEAD_DIM)

        def one(x, dy, g, acc):
            r = lax.rsqrt(_dot_split(x * x, gm) + RMS_EPS)
            xh = x * r
            dxh = dy * g
            _accumulate(acc, _colsum(dy * xh))
            return r * (dxh - xh * _dot_split(dxh * xh, gm))

        dq = one(q_ref[...], dqs_ref[...] * (HEAD_DIM ** -0.5), qg_ref[...], qacc)
        dk = one(k_ref[...], _unstack_heads(_load_transposed(dkn_ref)), kg_ref[...], kacc)
        dp_ref[...] = jnp.concatenate([dq, dk, _unstack_heads(_load_transposed(dv_ref))], axis=1).astype(BF16)

        @pl.when(i == pl.num_programs(0) - 1)
        def _():
            dqg_ref[...] = _head_sum(qacc[...])
            dkg_ref[...] = _head_sum(kacc[...])

    return pl.pallas_call(
        body, name="bwd_qk", grid=(s // tm,),
        in_specs=[_rows(tm, SB_W, OFF_SB // SB_W), _rows(tm, SB_W, OFF_SB // SB_W + 1),
                  _rows(tm, SB_W), tiles, tiles, _whole((1, SB_W)), _whole((1, SB_W))],
        out_specs=[_rows(tm, 3 * SB_W), _whole((1, HEAD_DIM)), _whole((1, HEAD_DIM))],
        out_shape=[jax.ShapeDtypeStruct((s, 3 * SB_W), BF16), jax.ShapeDtypeStruct((1, HEAD_DIM), F32),
                   jax.ShapeDtypeStruct((1, HEAD_DIM), F32)],
        scratch_shapes=[pltpu.VMEM((1, SB_W), F32), pltpu.VMEM((1, SB_W), F32)],
        compiler_params=_cp("arbitrary"),
    )(proj, proj, dqs, dkn, dv, qg, kg)


def _bwd_sg(proj, dy, lg, lb, w, bias):
    s = proj.shape[0]
    tm = LIGHT_TILE
    nh = SG_W // HEAD_DIM

    def body(p_ref, dy_ref, lg_ref, lb_ref, w_ref, bias_ref, dp_ref, dlg_ref, dlb_ref, dw_ref, db_ref, dbias):
        i = pl.program_id(0)
        uv = p_ref[...]
        ge = _gelu(uv)
        u = ge[:, :SG_W]
        r, xh = _ln_stats(ge[:, SG_W:])
        vln = (xh * lg_ref[...] + lb_ref[...]).astype(BF16)
        dy = dy_ref[...]
        tril, lane_head = _sg_masks()

        @pl.when(i == 0)
        def _():
            dw_ref[...] = jnp.zeros_like(dw_ref)
            dbias[...] = jnp.zeros_like(dbias)

        dus, dvlns = [], []
        for c in range(tm // CHUNK):
            rows = slice(c * CHUNK, (c + 1) * CHUNK)
            vc = vln[rows]
            dus.append(dy[rows] * _sg_mix(w_ref, bias_ref, vc, tril, lane_head))
            dm = dy[rows] * u[rows]
            dbias[...] += dm
            dvc = jnp.zeros((CHUNK, SG_W), F32)
            for h in range(nh):
                dmh = jnp.where(lane_head == h, dm, 0.0).astype(BF16)
                dw_ref[h] += jnp.where(tril, _dot_nt(dmh, vc), 0.0)
                wm = jnp.where(tril, w_ref[h], 0.0).astype(BF16)
                dvc = dvc + _dot_tn(wm, dmh)
            dvlns.append(dvc)
        du = jnp.concatenate(dus, axis=0)
        dvln = jnp.concatenate(dvlns, axis=0)
        _accumulate(dlg_ref, _colsum(dvln * xh))
        _accumulate(dlb_ref, _colsum(dvln))
        dv = _ln_bwd(xh, r, lg_ref[...], dvln)
        dp_ref[...] = (jnp.concatenate([du, dv], axis=1) * _gelu_grad(uv)).astype(BF16)

        @pl.when(i == pl.num_programs(0) - 1)
        def _():
            lane = lax.broadcasted_iota(jnp.int32, (CHUNK, CHUNK), 1)
            acc = dbias[...]
            out = jnp.zeros((CHUNK, CHUNK), F32)
            for h in range(nh):
                hs = jnp.sum(acc[:, h * HEAD_DIM:(h + 1) * HEAD_DIM], axis=1, keepdims=True)
                out = out + jnp.where(lane == h, hs, 0.0)
            db_ref[...] = out

    return pl.pallas_call(
        body, name="bwd_sg", grid=(s // tm,),
        in_specs=[_rows(tm, 2 * SG_W, 1), _rows(tm, SG_W), _whole((1, SG_W)), _whole((1, SG_W)),
                  _whole((nh, CHUNK, CHUNK)), _whole((CHUNK, SG_W))],
        out_specs=[_rows(tm, 2 * SG_W), _whole((1, SG_W)), _whole((1, SG_W)), _whole((nh, CHUNK, CHUNK)),
                   _whole((CHUNK, CHUNK))],
        out_shape=[jax.ShapeDtypeStruct((s, 2 * SG_W), BF16), jax.ShapeDtypeStruct((1, SG_W), F32),
                   jax.ShapeDtypeStruct((1, SG_W), F32), jax.ShapeDtypeStruct((nh, CHUNK, CHUNK), F32),
                   jax.ShapeDtypeStruct((CHUNK, CHUNK), F32)],
        scratch_shapes=[pltpu.VMEM((CHUNK, SG_W), F32)],
        compiler_params=_cp("arbitrary"),
    )(proj, dy, lg, lb, w, bias)


def _bwd_conv(proj, dy, w, b, lg, lb):
    s = proj.shape[0]

    def body(p_ref, dy_ref, w_ref, b_ref, lg_ref, lb_ref, dp_ref, dw_ref, db_ref, dlg_ref, dlb_ref,
             abuf, dcbuf):
        _glu_fill(p_ref, abuf, s)
        dcbuf[pl.ds(s, CHUNK), :] = jnp.zeros((CHUNK, CONV_W), F32)
        dw_ref[...] = jnp.zeros_like(dw_ref)

        def chunk(c, carry):
            db, dlg, dlb = carry
            r0 = pl.multiple_of(c * CHUNK, CHUNK)
            acc, shifted = _conv_window(abuf, r0, w_ref)
            r, xh = _ln_stats(acc + b_ref[...])
            ln = xh * lg_ref[...] + lb_ref[...]
            sg = _sigmoid(ln)
            dl = dy_ref[pl.ds(r0, CHUNK), :] * (sg * (1.0 + ln * (1.0 - sg)))
            dc = _ln_bwd(xh, r, lg_ref[...], dl)
            dcbuf[pl.ds(r0, CHUNK), :] = dc
            for k in range(CONV_K):
                dw_ref[CONV_K - 1 - k:CONV_K - k, :] += _colsum(dc * shifted(k))
            return db + _colsum(dc), dlg + _colsum(dl * xh), dlb + _colsum(dl)

        zero = jnp.zeros((1, CONV_W), F32)
        db, dlg, dlb = lax.fori_loop(0, s // CHUNK, chunk, (zero, zero, zero))
        db_ref[...] = db
        dlg_ref[...] = dlg
        dlb_ref[...] = dlb

        def chunk_back(c, carry):
            r0 = pl.multiple_of(c * CHUNK, CHUNK)
            shifted = _shifted(dcbuf[pl.ds(r0, CHUNK + 32), :], back=False)
            da = jnp.zeros((CHUNK, CONV_W), F32)
            for k in range(CONV_K):
                da = da + shifted(k) * w_ref[CONV_K - 1 - k:CONV_K - k, :]
            pv = p_ref[pl.ds(r0, CHUNK), :]
            val, sg = pv[:, :CONV_W], _sigmoid(pv[:, CONV_W:])
            dp_ref[pl.ds(r0, CHUNK), :] = jnp.concatenate([da * sg, da * val * sg * (1.0 - sg)], axis=1).astype(BF16)
            return carry

        lax.fori_loop(0, s // CHUNK, chunk_back, 0)

    row = _whole((1, CONV_W))
    return pl.pallas_call(
        body, name="bwd_conv", grid=(1,),
        in_specs=[pl.BlockSpec((s, 2 * CONV_W), lambda i: (0, 0)), _whole((s, CONV_W)),
                  _whole((CONV_K, CONV_W)), row, row, row],
        out_specs=[_whole((s, 2 * CONV_W)), _whole((CONV_K, CONV_W)), row, row, row],
        out_shape=[jax.ShapeDtypeStruct((s, 2 * CONV_W), BF16), jax.ShapeDtypeStruct((CONV_K, CONV_W), F32)]
        + [jax.ShapeDtypeStruct((1, CONV_W), F32)] * 3,
        scratch_shapes=[pltpu.VMEM((s + CHUNK, CONV_W), F32), pltpu.VMEM((s + CHUNK, CONV_W), F32)],
        compiler_params=_cp("arbitrary"),
    )(proj, dy, w, b, lg, lb)


def _bwd_in(dpc, dps, dpb, x, g, w, dxm):
    s = x.shape[0]
    tm = TOKEN_TILE

    def body(dpc_ref, dps_ref, dpb_ref, x_ref, g_ref, w_ref, dxm_ref, dx_ref, h_ref, dp_ref, dg_ref):
        dp = jnp.concatenate([dpc_ref[...], dps_ref[...], dpb_ref[...]], axis=1)
        dp_ref[...] = dp
        dh = _dot(dp, w_ref[...])
        r, xh = _rms_stats(x_ref[...])
        h_ref[...] = (xh * g_ref[...]).astype(BF16)
        dx, dgrow = _rms_bwd(xh, r, g_ref[...], dh)
        dx_ref[...] = dxm_ref[...] + dx
        _accumulate(dg_ref, _colsum(dgrow))

    return pl.pallas_call(
        body, name="bwd_in", grid=(s // tm,),
        in_specs=[_rows(tm, 2 * CONV_W), _rows(tm, 2 * SG_W), _rows(tm, 3 * SB_W), _rows(tm, D_MODEL),
                  _whole((1, D_MODEL)), _whole((IN_W, D_MODEL)), _rows(tm, D_MODEL)],
        out_specs=[_rows(tm, D_MODEL), _rows(tm, D_MODEL), _rows(tm, IN_W), _whole((1, D_MODEL))],
        out_shape=[jax.ShapeDtypeStruct((s, D_MODEL), F32), jax.ShapeDtypeStruct((s, D_MODEL), BF16),
                   jax.ShapeDtypeStruct((s, IN_W), BF16), jax.ShapeDtypeStruct((1, D_MODEL), F32)],
        compiler_params=_cp("arbitrary"),
    )(dpc, dps, dpb, x, g, w, dxm)


SMALL = ("mix_norm_g", "conv_w", "conv_b", "conv_ln_g", "conv_ln_b", "sg_ln_g", "sg_ln_b", "sg_w", "sg_b",
         "q_norm_g", "k_norm_g", "out_norm_g", "ffn_norm_g")
LARGE = ("w_in", "w_out", "w_gate_up", "w_down")


def _row(v):
    return v.reshape(1, -1)


def _layer_params(p, large, l):
    q = {k: v[l] for k, v in p.items()}
    return dict(
        q, **large,
        mix_norm_g=_row(q["mix_norm_g"]), conv_b=_row(q["conv_b"]), conv_ln_g=_row(q["conv_ln_g"]),
        conv_ln_b=_row(q["conv_ln_b"]), sg_ln_g=_row(q["sg_ln_g"]), sg_ln_b=_row(q["sg_ln_b"]),
        out_norm_g=_row(q["out_norm_g"]), ffn_norm_g=_row(q["ffn_norm_g"]),
        qg=_row(jnp.tile(q["q_norm_g"], SB_W // HEAD_DIM)), kg=_row(jnp.tile(q["k_norm_g"], SB_W // HEAD_DIM)),
        sg_bias=jnp.repeat(q["sg_b"].T, HEAD_DIM, axis=1),
    )


def _layer_fwd(x, q, tok, after_in, after_mixers, after_out):
    proj, qn, kn, vb, kt, vt = _fwd_in(x, q["mix_norm_g"], q["w_in"], q["qg"], q["kg"], tok)
    yc = _fwd_conv(proj, q["conv_w"], q["conv_b"], q["conv_ln_g"], q["conv_ln_b"], after_in(proj))
    ys = _fwd_sg(proj, q["sg_ln_g"], q["sg_ln_b"], q["sg_w"], q["sg_bias"])
    yb, lt = _fwd_sb(qn, kt, vb)
    rest, tok = after_mixers(yb)
    q = dict(q, **rest)
    xm = _fwd_out(yc, ys, yb, q["out_norm_g"], q["w_out"], x, tok)
    gu, xo = _fwd_ffn(xm, q["ffn_norm_g"], q["w_gate_up"], q["w_down"], after_out(xm))
    return xo, q, dict(x=x, proj=proj, qn=qn, kn=kn, kt=kt, vt=vt, lt=lt, yc=yc, ys=ys, yb=yb, xm=xm, gu=gu)


def _layer_bwd_ffn(dxo, q, st, tok):
    dgu, act, h2, dxm, d_ffn_g = _bwd_ffn(dxo, st["gu"], st["xm"], q["ffn_norm_g"], q["w_gate_up"], q["w_down"],
                                          tok)
    d_wgu, d_wd = _matmul_tn(dgu, h2, 512, D_MODEL), _matmul_tn(act, dxo, FFN_H // 2, D_MODEL)
    dyc, dys, dyb, cat, d_out_g = _bwd_out(dxm, st["yc"], st["ys"], st["yb"], q["out_norm_g"], q["w_out"], tok)
    return dxm, (dyc, dys, dyb, d_ffn_g, d_out_g), d_wgu, d_wd, _matmul_tn(cat, dxm, 512, D_MODEL)


def _layer_bwd_mix(dxm, carried, q, st, tok):
    dyc, dys, dyb, d_ffn_g, d_out_g = carried
    dqs, dkn, dv = _bwd_sb(st["qn"], st["kn"], st["kt"], st["vt"], dyb, st["lt"], tok)
    dpb, d_qg, d_kg = _bwd_qk(st["proj"], dqs, dkn, dv, q["qg"], q["kg"])
    dps, d_sg_lg, d_sg_lb, d_sg_w, d_sg_b = _bwd_sg(st["proj"], dys, q["sg_ln_g"], q["sg_ln_b"], q["sg_w"],
                                                    q["sg_bias"])
    dpc, d_conv_w, d_conv_b, d_conv_lg, d_conv_lb = _bwd_conv(st["proj"], dyc, q["conv_w"], q["conv_b"],
                                                              q["conv_ln_g"], q["conv_ln_b"])
    dx, h1, dp, d_mix_g = _bwd_in(dpc, dps, dpb, st["x"], q["mix_norm_g"], q["w_in"], dxm)
    d_win = _matmul_tn(dp, h1, 512, D_MODEL)
    small = dict(
        mix_norm_g=d_mix_g[0], conv_w=d_conv_w, conv_b=d_conv_b[0], conv_ln_g=d_conv_lg[0],
        conv_ln_b=d_conv_lb[0], sg_ln_g=d_sg_lg[0], sg_ln_b=d_sg_lb[0], sg_w=d_sg_w,
        sg_b=d_sg_b[:, :SG_W // HEAD_DIM].T, q_norm_g=d_qg[0], k_norm_g=d_kg[0], out_norm_g=d_out_g[0],
        ffn_norm_g=d_ffn_g[0])
    return dx, d_win, small


def _position():
    x, y, c = lax.axis_index("x"), lax.axis_index("y"), lax.axis_index("c")
    return x, y, c


def _flat(px, py, pc):
    return 4 * px + 2 * py + pc


IN_HBM = pl.BlockSpec(memory_space=pltpu.HBM)
IN_SEM = pl.BlockSpec(memory_space=pltpu.SEMAPHORE)
EFFECT = pltpu.SideEffectType.DATAFLOW_SIDE_EFFECTING
COPIES = dict(scatter=7, spread=7, spread_chips=4, **{"pass": 3})


def _exchange_copies(kinds, src_refs, land_refs, send_sems, recv_sems, layer, arrival):
    x, y, c = _position()
    me = _flat(x, y, c)
    everyone = [(x ^ (k >> 2 & 1), y ^ (k >> 1 & 1), c ^ (k & 1)) for k in range(1, N_DEV)]
    sibling = (x, y, 1 - c)
    chips = [(1 - x, y, c), (x, 1 - y, c), (1 - x, 1 - y, c)]
    out = []
    srcs = iter(src_refs)
    for kind, land in zip(kinds, land_refs):
        land = land if layer is None else land.at[layer]
        if kind == "scatter":
            src = next(srcs)
            moves = [(src.at[_flat(*p)], me, _flat(*p), p) for p in everyone]
        elif kind in ("spread", "spread_chips"):
            src = next(srcs)
            moves = [(src, me, _flat(*p), p) for p in (everyone if kind == "spread" else [sibling] + chips)]
        else:
            moves = [(land.at[_flat(*p)], _flat(*p), _flat(p[0], p[1], 1 - c), sibling) for p in chips]
        for src_block, there, here, peer in moves:
            n = len(out)
            out.append(pltpu.make_async_remote_copy(
                src_ref=src_block, dst_ref=land.at[here if arrival else there], send_sem=send_sems.at[n],
                recv_sem=recv_sems.at[n], device_id=peer, device_id_type=MESH))
    return out


def _exchange_start(kinds, srcs, lands, after, name, layer=None):
    ns, n = len(srcs), len(srcs) + len(lands)
    sems = sum(COPIES[k] for k in kinds)

    def body(*refs):
        send_sems, recv_sems = refs[n + 1], refs[n + 2]
        for cp in _exchange_copies(kinds, refs[:ns], refs[ns:n], send_sems, recv_sems, layer, arrival=False):
            cp.start()
        refs[-1][...] = jnp.zeros_like(refs[-1])

    thru = [pltpu.HBM(a.shape, a.dtype) for a in (*srcs, *lands)]
    outs = pl.pallas_call(
        body, name=name,
        out_shape=(pltpu.SemaphoreType.DMA((sems,)), pltpu.SemaphoreType.DMA((sems,)), *thru,
                   jax.ShapeDtypeStruct((8, 128), F32)),
        in_specs=[IN_HBM] * n + [ORDER_ONLY],
        out_specs=(IN_SEM, IN_SEM, *[IN_HBM] * n, pl.BlockSpec(memory_space=pltpu.VMEM)),
        input_output_aliases={i: 2 + i for i in range(n)},
        compiler_params=pltpu.CompilerParams(has_side_effects=EFFECT),
    )(*[pltpu.with_memory_space_constraint(a, pltpu.HBM) for a in (*srcs, *lands)], after)
    return kinds, outs[0], outs[1], list(outs[2:2 + ns]), list(outs[2 + ns:2 + n]), outs[-1]


def _exchange_wait(pending, after, name, layer=None):
    kinds, send_sems, recv_sems, srcs, lands, _ = pending
    after = list(after) if isinstance(after, (list, tuple)) else [after]
    ns, n = len(srcs), len(srcs) + len(lands)

    def body(*refs):
        for cp in _exchange_copies(kinds, refs[:ns], refs[ns:n], refs[n], refs[n + 1], layer, arrival=True):
            cp.wait_send()
            cp.wait_recv()

    thru = [pltpu.HBM(a.shape, a.dtype) for a in (*srcs, *lands)]
    outs = pl.pallas_call(
        body, name=name, out_shape=tuple(thru),
        in_specs=[IN_HBM] * n + [IN_SEM, IN_SEM] + [ORDER_ONLY] * len(after),
        out_specs=tuple([IN_HBM] * n),
        input_output_aliases={i: i for i in range(n)},
        compiler_params=pltpu.CompilerParams(has_side_effects=EFFECT),
    )(*srcs, *lands, send_sems, recv_sems, *after)
    return list(outs[ns:])


def _landing(block, me):
    land = lax.empty((N_DEV,) + block.shape, block.dtype)
    return lax.dynamic_update_index_in_dim(land, block, me, 0)


def _adamw(parts, w, m, v, tr):
    groups, rows, cols = w.shape

    def body(p_ref, w_ref, m_ref, v_ref, g_ref, d_ref, nm_ref, nv_ref):
        g = p_ref[0].astype(F32)
        for j in range(1, N_DEV):
            g = g + p_ref[j].astype(F32)
        g_ref[...] = g
        m = ADAM_B1 * m_ref[...] + (1.0 - ADAM_B1) * g
        v = ADAM_B2 * v_ref[...] + (1.0 - ADAM_B2) * (g * g)
        nm_ref[...] = m
        nv_ref[...] = v
        m_hat = m / (1.0 - ADAM_B1 ** ADAM_STEP)
        v_hat = v / (1.0 - ADAM_B2 ** ADAM_STEP)
        d_ref[...] = -ADAM_LR * (m_hat / (jnp.sqrt(v_hat) + ADAM_EPS) + ADAM_WD * w_ref[...])

    blk = pl.BlockSpec((None, tr, cols), lambda g, i: (g, i, 0))
    return pl.pallas_call(
        body, name="adamw", grid=(groups, rows // tr),
        in_specs=[pl.BlockSpec((None, N_DEV, tr, cols), lambda g, i: (g, 0, i, 0)), blk, blk, blk],
        out_specs=[blk] * 4,
        out_shape=[jax.ShapeDtypeStruct((groups, rows, cols), F32)] * 4,
        compiler_params=_cp("parallel", "parallel"),
    )(parts, w, m, v)


def _row_tile(rows):
    for cand in range(min(rows, 512) // 8 * 8, 7, -8):
        if rows % cand == 0:
            return cand
    return rows


def _with_own_block(land, blocks, layer, me):
    own = lax.dynamic_index_in_dim(blocks, me, 0, keepdims=True)[None]
    return lax.dynamic_update_slice(land, own, (layer, me, 0, 0))


PACK_LANES = 128


def _pack_layers(arrs):
    parts = []
    for a in arrs:
        flat = a.reshape(a.shape[0], -1)
        parts.append(jnp.pad(flat, ((0, 0), (0, -flat.shape[1] % (8 * PACK_LANES)))))
    return jnp.concatenate(parts, axis=1).reshape(arrs[0].shape[0], -1, PACK_LANES)


def _unpack_layers(packed, shapes):
    flat = packed.reshape(packed.shape[0], -1)
    outs, off = [], 0
    for shp in shapes:
        size = 1
        for d in shp[1:]:
            size *= d
        outs.append(flat[:, off:off + size].reshape(shp))
        off += size + (-size % (8 * PACK_LANES))
    return outs


def kernel(x, mix_norm_g, w_in, conv_w, conv_b, conv_ln_g, conv_ln_b, sg_ln_g, sg_ln_b, sg_w, sg_b, q_norm_g, k_norm_g, out_norm_g, w_out, ffn_norm_g, w_gate_up, w_down, loss_target, m_mix_norm_g, m_w_in, m_conv_w, m_conv_b, m_conv_ln_g, m_conv_ln_b, m_sg_ln_g, m_sg_ln_b, m_sg_w, m_sg_b, m_q_norm_g, m_k_norm_g, m_out_norm_g, m_w_out, m_ffn_norm_g, m_w_gate_up, m_w_down, v_mix_norm_g, v_w_in, v_conv_w, v_conv_b, v_conv_ln_g, v_conv_ln_b, v_sg_ln_g, v_sg_ln_b, v_sg_w, v_sg_b, v_q_norm_g, v_k_norm_g, v_out_norm_g, v_w_out, v_ffn_norm_g, v_w_gate_up, v_w_down):
    names = SMALL[:1] + LARGE[:1] + SMALL[1:12] + LARGE[1:2] + SMALL[12:] + LARGE[2:]
    w = dict(mix_norm_g=mix_norm_g, w_in=w_in, conv_w=conv_w, conv_b=conv_b, conv_ln_g=conv_ln_g,
             conv_ln_b=conv_ln_b, sg_ln_g=sg_ln_g, sg_ln_b=sg_ln_b, sg_w=sg_w, sg_b=sg_b, q_norm_g=q_norm_g,
             k_norm_g=k_norm_g, out_norm_g=out_norm_g, w_out=w_out, ffn_norm_g=ffn_norm_g,
             w_gate_up=w_gate_up, w_down=w_down)
    m = dict(mix_norm_g=m_mix_norm_g, w_in=m_w_in, conv_w=m_conv_w, conv_b=m_conv_b, conv_ln_g=m_conv_ln_g,
             conv_ln_b=m_conv_ln_b, sg_ln_g=m_sg_ln_g, sg_ln_b=m_sg_ln_b, sg_w=m_sg_w, sg_b=m_sg_b,
             q_norm_g=m_q_norm_g, k_norm_g=m_k_norm_g, out_norm_g=m_out_norm_g, w_out=m_w_out,
             ffn_norm_g=m_ffn_norm_g, w_gate_up=m_w_gate_up, w_down=m_w_down)
    v = dict(mix_norm_g=v_mix_norm_g, w_in=v_w_in, conv_w=v_conv_w, conv_b=v_conv_b, conv_ln_g=v_conv_ln_g,
             conv_ln_b=v_conv_ln_b, sg_ln_g=v_sg_ln_g, sg_ln_b=v_sg_ln_b, sg_w=v_sg_w, sg_b=v_sg_b,
             q_norm_g=v_q_norm_g, k_norm_g=v_k_norm_g, out_norm_g=v_out_norm_g, w_out=v_w_out,
             ffn_norm_g=v_ffn_norm_g, w_gate_up=v_w_gate_up, w_down=v_w_down)
    xpos, ypos, cpos = _position()
    me = _flat(xpos, ypos, cpos)
    conv_cols = conv_w.shape[-1]
    no_token = jnp.zeros((8, 128), F32)
    w, m, v = (dict(t, w_in=jnp.swapaxes(t["w_in"], 1, 2), w_gate_up=jnp.swapaxes(t["w_gate_up"], 1, 2))
               for t in (w, m, v))
    shards = {k: w[k].astype(BF16) for k in LARGE}
    full_shape = dict(w_in=(IN_W, D_MODEL), w_out=(D_MODEL, D_MODEL), w_gate_up=(2 * FFN_H, D_MODEL),
                      w_down=(FFN_H, D_MODEL))

    def gather_start(srcs, after, tag):
        return _exchange_start(["spread_chips"] * len(srcs), srcs, [_landing(a, me) for a in srcs], after,
                               f"gather_start_{tag}")

    def gather_pass(pending, after, tag):
        lands = _exchange_wait(pending, after, f"gather_wait_{tag}")
        return _exchange_start(["pass"] * len(lands), [], lands, after, f"gather_pass_{tag}")

    def gathered(pending, keys, after, tag):
        lands = _exchange_wait(pending, after, f"gather_passed_{tag}")
        return {k: a.reshape(full_shape[k]) for k, a in zip(keys, lands)}, lands[len(keys):]

    first, later = ("w_in",), ("w_out", "w_gate_up", "w_down")
    act = x[0]
    head = gather_start([shards["w_in"][0], w["conv_w"]], act, "0")
    tail = gather_start([shards[k][0] for k in later], head[5], "0_later")
    head = gather_pass(head, tail[5], "0")
    large, (conv_blocks,) = gathered(head, first, head[5], "0")
    conv_full = jnp.transpose(conv_blocks, (1, 2, 0, 3)).reshape(DEPTH, CONV_K, CONV_W)
    small_w = dict({k: w[k] for k in SMALL}, conv_w=conv_full)
    qs, stash = [], []
    for l in range(DEPTH):
        coming = {}
        more = l + 1 < DEPTH

        def next_start(after):
            coming["first"] = gather_start([shards[k][l + 1] for k in LARGE], after, str(l + 1))
            return coming["first"][5]

        def after_in(proj):
            return next_start(proj) if more and l == 0 else no_token

        def after_mixers(y_sb):
            if l > 0:
                return {}, no_token
            passing = gather_pass(tail, y_sb, "0_later")
            return gathered(passing, later, passing[5], "0_later")[0], passing[5]

        def after_out(x_mid):
            if not more or l == 0:
                return no_token
            coming["second"] = gather_pass(coming["first"], x_mid, str(l + 1))
            return coming["second"][5]

        token = next_start(act) if more and l > 0 else no_token
        act, q, st = _layer_fwd(act, _layer_params(small_w, large, l), token, after_in, after_mixers, after_out)
        qs.append(q)
        stash.append(st)
        if more:
            if l == 0:
                coming["second"] = gather_pass(coming["first"], act, str(l + 1))
            large, _ = gathered(coming["second"], LARGE, act, str(l + 1))

    loss, dx = _loss_head(act, loss_target[0])
    loss = lax.psum(loss[0, 0], ("x", "y", "c"))

    replicated = tuple(k for k in SMALL if k != "conv_w")
    small_rows = _pack_layers([w[k][:1] for k in replicated]).shape[1]
    conv_rows = _pack_layers([conv_full[:1]]).shape[1]
    group_a, group_b = ("w_gate_up", "w_down", "w_out"), ("w_in",)
    blocks = lambda k, a: a.reshape((N_DEV,) + w[k].shape[1:])
    land_a = [lax.empty((DEPTH, N_DEV) + w[k].shape[1:], BF16) for k in group_a]
    land_b = [lax.empty((DEPTH, N_DEV) + w[k].shape[1:], BF16) for k in group_b]
    land_b.append(lax.empty((DEPTH, N_DEV, small_rows + conv_rows, PACK_LANES), F32))
    pend_a = pend_b = None
    token = no_token
    for l in reversed(range(DEPTH)):
        dxm, carried, d_wgu, d_wd, d_wo = _layer_bwd_ffn(dx, qs[l], stash[l], token)
        srcs = [blocks(k, a) for k, a in zip(group_a, (d_wgu, d_wd, d_wo))]
        if pend_a is not None:
            land_a = _exchange_wait(pend_a, d_wo, f"grads_a_wait_{l + 1}", layer=l + 1)
        land_a = [_with_own_block(ld, a, l, me) for ld, a in zip(land_a, srcs)]
        pend_a = _exchange_start(["scatter"] * 3, srcs, land_a, dxm, f"grads_a_start_{l}", layer=l)
        dx, d_win, small = _layer_bwd_mix(dxm, carried, qs[l], stash[l], pend_a[5])
        packed = _pack_layers([small[k][None] for k in replicated + ("conv_w",)])[0]
        srcs = [blocks("w_in", d_win)]
        if pend_b is not None:
            land_b = _exchange_wait(pend_b, d_win, f"grads_b_wait_{l + 1}", layer=l + 1)
        land_b = [_with_own_block(land_b[0], srcs[0], l, me),
                  lax.dynamic_update_slice(land_b[1], packed[None, None], (l, me, 0, 0))]
        pend_b = _exchange_start(["scatter", "spread"], srcs + [packed], land_b, dx, f"grads_b_start_{l}",
                                 layer=l)
        token = pend_b[5]

    land_a = _exchange_wait(pend_a, token, "grads_a_wait_0", layer=0)
    res = {}
    for k, parts in zip(group_a, land_a):
        res[k] = _adamw(parts, w[k], m[k], v[k], _row_tile(w[k].shape[1]))
    land_b = _exchange_wait(pend_b, [res[k][0] for k in group_a], "grads_b_wait_0", layer=0)
    res["w_in"] = _adamw(land_b[0], w["w_in"], m["w_in"], v["w_in"], _row_tile(w["w_in"].shape[1]))
    small_parts = land_b[1]
    updated = _adamw(small_parts, *(_pack_layers([t[k] for k in replicated]) for t in (w, m, v)), small_rows)
    unpacked = [_unpack_layers(o, [w[k].shape for k in replicated]) for o in updated]
    res.update({k: [u[i] for u in unpacked] for i, k in enumerate(replicated)})
    conv_parts = small_parts[:, :, small_rows:].reshape(DEPTH, N_DEV, -1)[:, :, :CONV_K * CONV_W]
    conv_parts = lax.dynamic_slice_in_dim(conv_parts.reshape(DEPTH, N_DEV, CONV_K, CONV_W), me * conv_cols,
                                          conv_cols, axis=3)
    res["conv_w"] = _adamw(conv_parts, w["conv_w"], m["conv_w"], v["conv_w"], CONV_K)
    for k in ("w_in", "w_gate_up"):
        res[k] = [jnp.swapaxes(a, 1, 2) for a in res[k]]

    return (loss, dx[None], *[res[k][0] for k in names], *[res[k][1] for k in names],
            *[res[k][2] for k in names], *[res[k][3] for k in names])
```

```python
import jax
import jax.numpy as jnp
from jax import lax
from jax.experimental import pallas as pl
from jax.experimental.pallas import tpu as pltpu

F32 = jnp.float32
BF16 = jnp.bfloat16

D_MODEL = 1024
DEPTH = 4
HEAD_DIM = 64
CONV_W = 256
SG_W = 256
SB_W = 512
IN_W = 2560
FFN_H = 2816
CONV_K = 31
CHUNK = 128
OFF_SG = 2 * CONV_W
OFF_SB = OFF_SG + 2 * SG_W
RMS_EPS = 1e-6
LN_EPS = 1e-5
N_DEV = 8
MESH = pl.DeviceIdType.MESH

ADAM_LR = 0.001
ADAM_B1 = 0.9
ADAM_B2 = 0.999
ADAM_EPS = 1e-08
ADAM_WD = 0.01
ADAM_STEP = 10

TOKEN_TILE = 512
FFN_BWD_TILE = 256
LIGHT_TILE = 1024
VMEM_LIMIT = 56 * 1024 * 1024


def _cp(*sem):
    return pltpu.CompilerParams(dimension_semantics=sem or None, vmem_limit_bytes=VMEM_LIMIT)


def _dot(a, b):
    return jnp.dot(a, b, preferred_element_type=F32)


def _dot_nt(a, b):
    return lax.dot_general(a, b, (((1,), (1,)), ((), ())), preferred_element_type=F32)


def _dot_tn(a, b):
    return lax.dot_general(a, b, (((0,), (0,)), ((), ())), preferred_element_type=F32)


def _dot_split(x, m):
    hi = x.astype(BF16)
    lo = (x - hi.astype(F32)).astype(BF16)
    return _dot(hi, m) + _dot(lo, m)


def _group_mean_matrix(width, group):
    r = lax.broadcasted_iota(jnp.int32, (width, width), 0) // group
    c = lax.broadcasted_iota(jnp.int32, (width, width), 1) // group
    return jnp.where(r == c, 1.0 / group, 0.0).astype(BF16)


def _sigmoid(x):
    return 1.0 / (1.0 + jnp.exp(-x))


def _gelu(x):
    return 0.5 * x * (1.0 + lax.erf(x * (2.0 ** -0.5)))


def _gelu_grad(x):
    return 0.5 * (1.0 + lax.erf(x * (2.0 ** -0.5))) + x * jnp.exp(-0.5 * x * x) * (0.5 * (2.0 / jnp.pi) ** 0.5)


def _rms_stats(x):
    r = lax.rsqrt(jnp.mean(x * x, axis=-1, keepdims=True) + RMS_EPS)
    return r, x * r


def _rms_bwd(xh, r, g, dy):
    dxh = dy * g
    dx = r * (dxh - xh * jnp.mean(dxh * xh, axis=-1, keepdims=True))
    return dx, dy * xh


def _ln_stats(x):
    mu = jnp.mean(x, axis=-1, keepdims=True)
    xc = x - mu
    r = lax.rsqrt(jnp.mean(xc * xc, axis=-1, keepdims=True) + LN_EPS)
    return r, xc * r


def _ln_bwd(xh, r, g, dy):
    dxh = dy * g
    return r * (dxh - jnp.mean(dxh, axis=-1, keepdims=True) - xh * jnp.mean(dxh * xh, axis=-1, keepdims=True))


def _colsum(x):
    return jnp.sum(x, axis=0, keepdims=True)


def _rows(tm, n, j=0):
    return pl.BlockSpec((tm, n), lambda i: (i, j))


def _whole(shape):
    return pl.BlockSpec(shape, lambda i: (0,) * len(shape))


ORDER_ONLY = pl.BlockSpec(memory_space=pl.ANY)


def _stack_heads(a):
    even = (lax.broadcasted_iota(jnp.int32, a.shape, 1) % (2 * HEAD_DIM)) < HEAD_DIM
    top = jnp.where(even, a, 0.0)
    bot = jnp.where(even, 0.0, a)
    parts = []
    for c in range(a.shape[0] // CHUNK):
        rows = slice(c * CHUNK, (c + 1) * CHUNK)
        parts += [top[rows], bot[rows]]
    return jnp.concatenate(parts, axis=0)


def _store_stacked(st, st_ref, tr_ref):
    st_ref[...] = st.astype(BF16)
    for p in range(SB_W // CHUNK):
        for c in range(st.shape[0] // (2 * CHUNK)):
            tile = st[2 * c * CHUNK:2 * (c + 1) * CHUNK, p * CHUNK:(p + 1) * CHUNK]
            tr_ref[p, c] = tile.T.astype(BF16)


def _load_transposed(tr_ref):
    rows = []
    for c in range(tr_ref.shape[1]):
        tiles = [tr_ref[p, c].T for p in range(SB_W // CHUNK)]
        rows.append(jnp.concatenate(tiles, axis=1))
    return jnp.concatenate(rows, axis=0)


def _unstack_heads(st):
    even = (lax.broadcasted_iota(jnp.int32, (CHUNK, st.shape[1]), 1) % (2 * HEAD_DIM)) < HEAD_DIM
    parts = []
    for c in range(st.shape[0] // (2 * CHUNK)):
        top = st[2 * c * CHUNK:(2 * c + 1) * CHUNK]
        bot = st[(2 * c + 1) * CHUNK:(2 * c + 2) * CHUNK]
        parts.append(jnp.where(even, top, bot))
    return jnp.concatenate(parts, axis=0)


def _fwd_in(x, g, w, qg, kg, tok):
    s = x.shape[0]
    tm = TOKEN_TILE

    def body(x_ref, g_ref, w_ref, qg_ref, kg_ref, tok_ref, proj_ref, qn_ref, kn_ref, vb_ref, kt_ref, vt_ref):
        r, xh = _rms_stats(x_ref[...])
        h = (xh * g_ref[...]).astype(BF16)
        proj = _dot_nt(h, w_ref[...])
        proj_ref[...] = proj
        gm = _group_mean_matrix(SB_W, HEAD_DIM)
        q = proj[:, OFF_SB:OFF_SB + SB_W]
        k = proj[:, OFF_SB + SB_W:OFF_SB + 2 * SB_W]
        rq = lax.rsqrt(_dot_split(q * q, gm) + RMS_EPS)
        rk = lax.rsqrt(_dot_split(k * k, gm) + RMS_EPS)
        qn_ref[...] = (q * rq * qg_ref[...] * (HEAD_DIM ** -0.5)).astype(BF16)
        _store_stacked(_stack_heads(k * rk * kg_ref[...]), kn_ref, kt_ref)
        _store_stacked(_stack_heads(proj[:, OFF_SB + 2 * SB_W:]), vb_ref, vt_ref)

    tiles = pl.BlockSpec((SB_W // CHUNK, tm // CHUNK, CHUNK, PAIR), lambda i: (0, i, 0, 0))
    tiles_shape = jax.ShapeDtypeStruct((SB_W // CHUNK, s // CHUNK, CHUNK, PAIR), BF16)
    return pl.pallas_call(
        body, name="fwd_in", grid=(s // tm,),
        in_specs=[_rows(tm, D_MODEL), _whole((1, D_MODEL)), _whole((IN_W, D_MODEL)),
                  _whole((1, SB_W)), _whole((1, SB_W)), ORDER_ONLY],
        out_specs=[_rows(tm, IN_W), _rows(tm, SB_W), _rows(2 * tm, SB_W), _rows(2 * tm, SB_W), tiles, tiles],
        out_shape=[jax.ShapeDtypeStruct((s, IN_W), F32), jax.ShapeDtypeStruct((s, SB_W), BF16),
                   jax.ShapeDtypeStruct((2 * s, SB_W), BF16), jax.ShapeDtypeStruct((2 * s, SB_W), BF16),
                   tiles_shape, tiles_shape],
        compiler_params=_cp("parallel"),
    )(x, g, w, qg, kg, tok)


SUBLANES = 8


def _shifted(win, back):
    n = win.shape[0]
    turned = [win] + [pltpu.roll(win, b if back else n - b, axis=0) for b in range(1, SUBLANES)]

    def shifted(k):
        whole, part = divmod(k, SUBLANES)
        start = 32 - whole * SUBLANES if back else whole * SUBLANES
        return turned[part][start:start + CHUNK, :]

    return shifted


def _conv_window(abuf, r0, w_ref):
    shifted = _shifted(abuf[pl.ds(pl.multiple_of(r0 + CHUNK - 32, 32), CHUNK + 32), :], back=True)
    acc = jnp.zeros((CHUNK, CONV_W), F32)
    for k in range(CONV_K):
        acc = acc + shifted(k) * w_ref[CONV_K - 1 - k:CONV_K - k, :]
    return acc, shifted


def _glu_fill(p_ref, abuf, s):
    abuf[0:CHUNK, :] = jnp.zeros((CHUNK, CONV_W), F32)

    def fill(c, carry):
        r0 = pl.multiple_of(c * CHUNK, CHUNK)
        pv = p_ref[pl.ds(r0, CHUNK), :]
        abuf[pl.ds(r0 + CHUNK, CHUNK), :] = pv[:, :CONV_W] * _sigmoid(pv[:, CONV_W:])
        return carry

    lax.fori_loop(0, s // CHUNK, fill, 0)


def _fwd_conv(proj, w, b, lg, lb, tok):
    s = proj.shape[0]

    def body(p_ref, w_ref, b_ref, lg_ref, lb_ref, tok_ref, y_ref, abuf):
        _glu_fill(p_ref, abuf, s)

        def chunk(c, carry):
            r0 = pl.multiple_of(c * CHUNK, CHUNK)
            acc, _ = _conv_window(abuf, r0, w_ref)
            r, xh = _ln_stats(acc + b_ref[...])
            ln = xh * lg_ref[...] + lb_ref[...]
            y_ref[pl.ds(r0, CHUNK), :] = ln * _sigmoid(ln)
            return carry

        lax.fori_loop(0, s // CHUNK, chunk, 0)

    return pl.pallas_call(
        body, name="fwd_conv", grid=(1,),
        in_specs=[pl.BlockSpec((s, 2 * CONV_W), lambda i: (0, 0)), _whole((CONV_K, CONV_W)),
                  _whole((1, CONV_W)), _whole((1, CONV_W)), _whole((1, CONV_W)), ORDER_ONLY],
        out_specs=_whole((s, CONV_W)),
        out_shape=jax.ShapeDtypeStruct((s, CONV_W), F32),
        scratch_shapes=[pltpu.VMEM((s + CHUNK, CONV_W), F32)],
        compiler_params=_cp("arbitrary"),
    )(proj, w, b, lg, lb, tok)


def _sg_masks():
    row = lax.broadcasted_iota(jnp.int32, (CHUNK, CHUNK), 0)
    col = lax.broadcasted_iota(jnp.int32, (CHUNK, CHUNK), 1)
    lane_head = lax.broadcasted_iota(jnp.int32, (CHUNK, SG_W), 1) // HEAD_DIM
    return row >= col, lane_head


def _sg_mix(w_ref, bias_ref, vc, tril, lane_head):
    mixed = bias_ref[...]
    for h in range(SG_W // HEAD_DIM):
        wm = jnp.where(tril, w_ref[h], 0.0).astype(BF16)
        mixed = mixed + jnp.where(lane_head == h, _dot(wm, vc), 0.0)
    return mixed


def _fwd_sg(proj, lg, lb, w, bias):
    s = proj.shape[0]
    tm = LIGHT_TILE

    def body(p_ref, lg_ref, lb_ref, w_ref, bias_ref, y_ref):
        ge = _gelu(p_ref[...])
        u = ge[:, :SG_W]
        r, xh = _ln_stats(ge[:, SG_W:])
        vln = (xh * lg_ref[...] + lb_ref[...]).astype(BF16)
        tril, lane_head = _sg_masks()
        for c in range(tm // CHUNK):
            rows = slice(c * CHUNK, (c + 1) * CHUNK)
            y_ref[rows, :] = u[rows] * _sg_mix(w_ref, bias_ref, vln[rows], tril, lane_head)

    return pl.pallas_call(
        body, name="fwd_sg", grid=(s // tm,),
        in_specs=[_rows(tm, 2 * SG_W, 1), _whole((1, SG_W)), _whole((1, SG_W)),
                  _whole((SG_W // HEAD_DIM, CHUNK, CHUNK)), _whole((CHUNK, SG_W))],
        out_specs=_rows(tm, SG_W),
        out_shape=jax.ShapeDtypeStruct((s, SG_W), F32),
        compiler_params=_cp("parallel"),
    )(proj, lg, lb, w, bias)


SB_Q_FWD = 16 * CHUNK
SB_Q_BWD = 8 * CHUNK
PAIR = 2 * CHUNK
SB_PAIRS = 1


def _pair_tri(kind):
    row = lax.broadcasted_iota(jnp.int32, (PAIR, PAIR), 0)
    col = lax.broadcasted_iota(jnp.int32, (PAIR, PAIR), 1)
    tri = {"after": row > col, "upto": row <= col, "before": row < col}[kind]
    return jnp.where(((row // CHUNK) == (col // CHUNK)) & tri, 1.0, 0.0).astype(BF16)


def _causal(x):
    row = lax.broadcasted_iota(jnp.int32, (CHUNK, PAIR), 0)
    col = lax.broadcasted_iota(jnp.int32, (CHUNK, PAIR), 1) % CHUNK
    top = jnp.where(col < row, x[:CHUNK], 0.0)
    return top if x.shape[0] == CHUNK else jnp.concatenate([top, x[CHUNK:]], axis=0)


def _sb_scores(z, masked):
    sp = jnp.maximum(z, 0.0) + jnp.log(1.0 + jnp.exp(-jnp.abs(z)))
    return z, sp, (_causal(sp) if masked else sp).astype(BF16)


def _per_head(c0, c1):
    rows = c0.shape[0]
    return jnp.concatenate([jnp.broadcast_to(c0, (rows, CHUNK)), jnp.broadcast_to(c1, (rows, CHUNK))], axis=1)


def _fwd_sb(qn, ktr, vst):
    s = qn.shape[0]
    SB_Q = min(SB_Q_FWD, s)
    np_ = SB_PAIRS

    def body(q_ref, k_ref, v_ref, after_ref, o_ref, lt_ref, z_buf, att_buf):
        i = pl.program_id(1)
        first = i * (SB_Q // CHUNK)
        last = first + SB_Q // CHUNK - 1
        after = after_ref[...]
        lanes = [slice(pr * CHUNK, (pr + 1) * CHUNK) for pr in range(np_)]
        qs = [q_ref[:, lanes[pr]] for pr in range(np_)]

        def rows(kb):
            return pl.ds(pl.multiple_of(kb * PAIR, PAIR), PAIR)

        def block(kb, carry, masked, top=0):
            out = []
            for pr in range(np_):
                acc, c0, c1 = carry[pr]
                z_next = _dot(qs[pr], k_ref[pr, jnp.maximum(kb - 1, 0)])
                pv = _dot(att_buf[pr], v_ref[rows(jnp.minimum(kb + 1, last)), lanes[pr]])
                z, sp, nlb = _sb_scores(z_buf[pr, top:, :], masked)
                loc = _dot(nlb, after)
                att = jnp.exp(z - sp - loc - _per_head(c0[top:], c1[top:]))
                if masked:
                    att = _causal(att)
                z_buf[pr] = z_next
                if top:
                    att_buf[pr, :top, :] = jnp.zeros((top, PAIR), BF16)
                att_buf[pr, top:, :] = att.astype(BF16)
                add0 = loc[:, 0:1] + nlb[:, 0:1].astype(F32)
                add1 = loc[:, CHUNK:CHUNK + 1] + nlb[:, CHUNK:CHUNK + 1].astype(F32)
                if top:
                    add0 = jnp.concatenate([jnp.zeros((top, 1), F32), add0], axis=0)
                    add1 = jnp.concatenate([jnp.zeros((top, 1), F32), add1], axis=0)
                out.append((acc + pv, c0 + add0, c1 + add1))
            return tuple(out)

        for pr in range(np_):
            z_buf[pr] = _dot(qs[pr], k_ref[pr, last])
        att_buf[...] = jnp.zeros_like(att_buf)
        zero = jnp.zeros((SB_Q, 1), F32)
        carry = ((jnp.zeros((SB_Q, CHUNK), F32), zero, zero),) * np_
        for back in range(SB_Q // CHUNK):
            carry = block(last - back, carry, True, top=SB_Q - (back + 1) * CHUNK)
        carry = lax.fori_loop(0, first, lambda j, c: block(first - 1 - j, c, False), carry)
        for pr, (acc, c0, c1) in enumerate(carry):
            o_ref[:, lanes[pr]] = acc + _dot(att_buf[pr], v_ref[rows(0), lanes[pr]])
            lt_ref[:, lanes[pr]] = jnp.concatenate([jnp.broadcast_to(c0, (SB_Q, HEAD_DIM)),
                                                    jnp.broadcast_to(c1, (SB_Q, HEAD_DIM))], axis=1)

    blk = pl.BlockSpec((SB_Q, np_ * CHUNK), lambda p, i: (i, p))
    seq = pl.BlockSpec((2 * s, np_ * CHUNK), lambda p, i: (0, p))
    return pl.pallas_call(
        body, name="fwd_sb", grid=(SB_W // CHUNK // np_, s // SB_Q),
        in_specs=[blk, pl.BlockSpec((np_, s // CHUNK, CHUNK, PAIR), lambda p, i: (p, 0, 0, 0)), seq,
                  pl.BlockSpec((PAIR, PAIR), lambda p, i: (0, 0))],
        out_specs=[blk, blk],
        out_shape=[jax.ShapeDtypeStruct((s, SB_W), F32)] * 2,
        scratch_shapes=[pltpu.VMEM((np_, SB_Q, PAIR), F32), pltpu.VMEM((np_, SB_Q, PAIR), BF16)],
        compiler_params=_cp("parallel", "parallel"),
    )(qn, ktr, vst, _pair_tri("after"))


def _group_norms(yc, ys, yb):
    return [_rms_stats(yc), _rms_stats(ys), _rms_stats(yb)]


def _fwd_out(yc, ys, yb, g, w, x, tok):
    s = x.shape[0]
    tm = LIGHT_TILE

    def body(yc_ref, ys_ref, yb_ref, g_ref, w_ref, x_ref, tok_ref, o_ref):
        stats = _group_norms(yc_ref[...], ys_ref[...], yb_ref[...])
        cat = jnp.concatenate([xh for _, xh in stats], axis=1) * g_ref[...]
        o_ref[...] = x_ref[...] + _dot(cat.astype(BF16), w_ref[...])

    return pl.pallas_call(
        body, name="fwd_out", grid=(s // tm,),
        in_specs=[_rows(tm, CONV_W), _rows(tm, SG_W), _rows(tm, SB_W), _whole((1, D_MODEL)),
                  _whole((D_MODEL, D_MODEL)), _rows(tm, D_MODEL), ORDER_ONLY],
        out_specs=_rows(tm, D_MODEL),
        out_shape=jax.ShapeDtypeStruct((s, D_MODEL), F32),
        compiler_params=_cp("parallel"),
    )(yc, ys, yb, g, w, x, tok)


def _fwd_ffn(x, g, wgu, wd, tok):
    s = x.shape[0]
    tm = TOKEN_TILE

    def body(x_ref, g_ref, wgu_ref, wd_ref, tok_ref, gu_ref, o_ref):
        x = x_ref[...]
        r, xh = _rms_stats(x)
        gu = _dot_nt((xh * g_ref[...]).astype(BF16), wgu_ref[...])
        gu_ref[...] = gu
        gate = gu[:, :FFN_H]
        act = gate * _sigmoid(gate) * gu[:, FFN_H:]
        o_ref[...] = x + _dot(act.astype(BF16), wd_ref[...])

    return pl.pallas_call(
        body, name="fwd_ffn", grid=(s // tm,),
        in_specs=[_rows(tm, D_MODEL), _whole((1, D_MODEL)),
                  pl.BlockSpec((2 * FFN_H, D_MODEL), lambda i: (0, 0), pipeline_mode=pl.Buffered(1)),
                  pl.BlockSpec((FFN_H, D_MODEL), lambda i: (0, 0), pipeline_mode=pl.Buffered(1)), ORDER_ONLY],
        out_specs=[_rows(tm, 2 * FFN_H), _rows(tm, D_MODEL)],
        out_shape=[jax.ShapeDtypeStruct((s, 2 * FFN_H), F32), jax.ShapeDtypeStruct((s, D_MODEL), F32)],
        compiler_params=_cp("parallel"),
    )(x, g, wgu, wd, tok)


def _loss_head(y, target):
    s = y.shape[0]
    tm = LIGHT_TILE

    def body(y_ref, t_ref, l_ref, d_ref):
        @pl.when(pl.program_id(0) == 0)
        def _():
            l_ref[...] = jnp.zeros_like(l_ref)

        err = y_ref[...] - t_ref[...]
        d_ref[...] = err * (1.0 / D_MODEL)
        l_ref[...] += 0.5 * jnp.sum(jnp.mean(err * err, axis=-1, keepdims=True), axis=0, keepdims=True)

    return pl.pallas_call(
        body, name="loss_head", grid=(s // tm,),
        in_specs=[_rows(tm, D_MODEL), _rows(tm, D_MODEL)],
        out_specs=[_whole((1, 1)), _rows(tm, D_MODEL)],
        out_shape=[jax.ShapeDtypeStruct((1, 1), F32), jax.ShapeDtypeStruct((s, D_MODEL), F32)],
        compiler_params=_cp("arbitrary"),
    )(y, target)


def _accumulate(ref, value):
    @pl.when(pl.program_id(0) == 0)
    def _():
        ref[...] = jnp.zeros_like(ref)

    ref[...] += value


def _bwd_ffn(dxo, gu, xm, g, wgu, wd, tok):
    s = dxo.shape[0]
    tm = FFN_BWD_TILE

    def body(dxo_ref, gu_ref, xm_ref, g_ref, wgu_ref, wd_ref, tok_ref, dgu_ref, act_ref, h_ref, dxm_ref, dg_ref):
        dxo = dxo_ref[...]
        gu = gu_ref[...]
        gate, up = gu[:, :FFN_H], gu[:, FFN_H:]
        sg = _sigmoid(gate)
        sl = gate * sg
        act_ref[...] = (sl * up).astype(BF16)
        dact = _dot_nt(dxo.astype(BF16), wd_ref[...])
        dgate = dact * up * (sg * (1.0 + gate * (1.0 - sg)))
        dgu = jnp.concatenate([dgate, dact * sl], axis=1).astype(BF16)
        dgu_ref[...] = dgu
        dh = _dot(dgu, wgu_ref[...])
        r, xh = _rms_stats(xm_ref[...])
        h_ref[...] = (xh * g_ref[...]).astype(BF16)
        dx, dgrow = _rms_bwd(xh, r, g_ref[...], dh)
        dxm_ref[...] = dxo + dx
        _accumulate(dg_ref, _colsum(dgrow))

    return pl.pallas_call(
        body, name="bwd_ffn", grid=(s // tm,),
        in_specs=[_rows(tm, D_MODEL), _rows(tm, 2 * FFN_H), _rows(tm, D_MODEL), _whole((1, D_MODEL)),
                  pl.BlockSpec((2 * FFN_H, D_MODEL), lambda i: (0, 0), pipeline_mode=pl.Buffered(1)),
                  pl.BlockSpec((FFN_H, D_MODEL), lambda i: (0, 0), pipeline_mode=pl.Buffered(1)), ORDER_ONLY],
        out_specs=[_rows(tm, 2 * FFN_H), _rows(tm, FFN_H), _rows(tm, D_MODEL), _rows(tm, D_MODEL),
                   _whole((1, D_MODEL))],
        out_shape=[jax.ShapeDtypeStruct((s, 2 * FFN_H), BF16), jax.ShapeDtypeStruct((s, FFN_H), BF16),
                   jax.ShapeDtypeStruct((s, D_MODEL), BF16), jax.ShapeDtypeStruct((s, D_MODEL), F32),
                   jax.ShapeDtypeStruct((1, D_MODEL), F32)],
        compiler_params=_cp("arbitrary"),
    )(dxo, gu, xm, g, wgu, wd, tok)


def _matmul_tn(a, b, tm, tn, out_dtype=BF16):
    s, m = a.shape
    n = b.shape[1]

    def body(a_ref, b_ref, o_ref):
        o_ref[...] = _dot_tn(a_ref[...].astype(BF16), b_ref[...].astype(BF16)).astype(out_dtype)

    return pl.pallas_call(
        body, name="weight_grad", grid=(m // tm, n // tn),
        in_specs=[pl.BlockSpec((s, tm), lambda i, j: (0, i)), pl.BlockSpec((s, tn), lambda i, j: (0, j))],
        out_specs=pl.BlockSpec((tm, tn), lambda i, j: (i, j)),
        out_shape=jax.ShapeDtypeStruct((m, n), out_dtype),
        compiler_params=_cp("parallel", "parallel"),
    )(a, b)


def _bwd_out(dxm, yc, ys, yb, g, w, tok):
    s = dxm.shape[0]
    tm = LIGHT_TILE

    def body(dxm_ref, yc_ref, ys_ref, yb_ref, g_ref, w_ref, tok_ref, dyc_ref, dys_ref, dyb_ref, cat_ref, dg_ref):
        stats = _group_norms(yc_ref[...], ys_ref[...], yb_ref[...])
        g = g_ref[...]
        cat_ref[...] = (jnp.concatenate([xh for _, xh in stats], axis=1) * g).astype(BF16)
        dcat = _dot_nt(dxm_ref[...].astype(BF16), w_ref[...])
        dgs = []
        off = 0
        for (r, xh), out in zip(stats, (dyc_ref, dys_ref, dyb_ref)):
            cols = slice(off, off + xh.shape[1])
            dx, dgrow = _rms_bwd(xh, r, g[:, cols], dcat[:, cols])
            out[...] = dx
            dgs.append(_colsum(dgrow))
            off += xh.shape[1]
        _accumulate(dg_ref, jnp.concatenate(dgs, axis=1))

    return pl.pallas_call(
        body, name="bwd_out", grid=(s // tm,),
        in_specs=[_rows(tm, D_MODEL), _rows(tm, CONV_W), _rows(tm, SG_W), _rows(tm, SB_W),
                  _whole((1, D_MODEL)), _whole((D_MODEL, D_MODEL)), ORDER_ONLY],
        out_specs=[_rows(tm, CONV_W), _rows(tm, SG_W), _rows(tm, SB_W), _rows(tm, D_MODEL),
                   _whole((1, D_MODEL))],
        out_shape=[jax.ShapeDtypeStruct((s, CONV_W), F32), jax.ShapeDtypeStruct((s, SG_W), F32),
                   jax.ShapeDtypeStruct((s, SB_W), F32), jax.ShapeDtypeStruct((s, D_MODEL), BF16),
                   jax.ShapeDtypeStruct((1, D_MODEL), F32)],
        compiler_params=_cp("arbitrary"),
    )(dxm, yc, ys, yb, g, w, tok)


def _bwd_sb(qn, kst, ktr, vtr, dy, ltot, tok):
    s = qn.shape[0]
    SB_Q = min(SB_Q_BWD, s)
    np_ = SB_PAIRS

    def body(q_ref, k_ref, kt_ref, vt_ref, do_ref, lt_ref, upto_ref, before_ref, tok_ref, dq_ref, dk_ref, dv_ref,
             z_buf, da_buf, dz_buf, att_buf):
        i = pl.program_id(1)
        first = i * (SB_Q // CHUNK)
        last = first + SB_Q // CHUNK - 1

        @pl.when(i == 0)
        def _():
            dk_ref[...] = jnp.zeros_like(dk_ref)
            dv_ref[...] = jnp.zeros_like(dv_ref)

        lanes = [slice(pr * CHUNK, (pr + 1) * CHUNK) for pr in range(np_)]
        qs = [q_ref[:, lanes[pr]] for pr in range(np_)]
        dos = [do_ref[:, lanes[pr]] for pr in range(np_)]
        dobs = [do.astype(BF16) for do in dos]
        q_ts = [q.astype(F32).T.astype(BF16) for q in qs]
        do_ts = [do.T.astype(BF16) for do in dos]
        ltots = [_per_head(lt_ref[:, pr * CHUNK:pr * CHUNK + 1],
                           lt_ref[:, pr * CHUNK + HEAD_DIM:pr * CHUNK + HEAD_DIM + 1]) for pr in range(np_)]
        upto = upto_ref[...]
        before = before_ref[...]
        last0, last1 = slice(CHUNK - 1, CHUNK), slice(PAIR - 1, PAIR)

        def rows(kb):
            return pl.ds(pl.multiple_of(kb * PAIR, PAIR), PAIR)

        def ahead(pr, kb):
            return _dot(qs[pr], kt_ref[pr, kb]), _dot(dobs[pr], vt_ref[pr, kb])

        def behind(pr, kb, dq):
            dzb = dz_buf[pr]
            dk_ref[pr, kb] += _dot(q_ts[pr], dzb)
            dv_ref[pr, kb] += _dot(do_ts[pr], att_buf[pr])
            return dq + _dot(dzb, k_ref[rows(kb), lanes[pr]])

        def block(kb, carry, masked, top=0):
            out = []
            for pr in range(np_):
                dq, p0, p1, e0, e1 = carry[pr]
                z_next, da_next = ahead(pr, jnp.minimum(kb + 1, last))
                dq = behind(pr, jnp.maximum(kb - 1, 0), dq)
                z, sp, nlb = _sb_scores(z_buf[pr, top:, :], masked)
                pin = _dot(nlb, upto) + _per_head(p0[top:], p1[top:])
                sig = jnp.exp(z - sp)
                att = jnp.exp(z - sp - (ltots[pr][top:] - pin))
                if masked:
                    att = _causal(att)
                e = att * da_buf[pr, top:, :]
                ebefore = _dot(e.astype(BF16), before) + _per_head(e0[top:], e1[top:])
                dz = e - sig * (e + ebefore)
                if masked:
                    dz = _causal(dz)
                z_buf[pr] = z_next
                da_buf[pr] = da_next
                if top:
                    dz_buf[pr, :top, :] = jnp.zeros((top, PAIR), BF16)
                    att_buf[pr, :top, :] = jnp.zeros((top, PAIR), BF16)
                dz_buf[pr, top:, :] = dz.astype(BF16)
                att_buf[pr, top:, :] = att.astype(BF16)
                new = (pin[:, last0], pin[:, last1], ebefore[:, last0] + e[:, last0], ebefore[:, last1] + e[:, last1])
                if top:
                    new = tuple(jnp.concatenate([old[:top], n], axis=0) for old, n in zip((p0, p1, e0, e1), new))
                out.append((dq,) + new)
            return tuple(out)

        for pr in range(np_):
            z_buf[pr], da_buf[pr] = ahead(pr, 0)
        dz_buf[...] = jnp.zeros_like(dz_buf)
        att_buf[...] = jnp.zeros_like(att_buf)
        zero = jnp.zeros((SB_Q, 1), F32)
        carry = ((jnp.zeros((SB_Q, CHUNK), F32), zero, zero, zero, zero),) * np_
        carry = lax.fori_loop(0, first, lambda kb, c: block(kb, c, False), carry)
        for ahead_of in range(SB_Q // CHUNK):
            carry = block(first + ahead_of, carry, True, top=ahead_of * CHUNK)
        for pr in range(np_):
            dq_ref[:, lanes[pr]] = behind(pr, last, carry[pr][0])

    blk = pl.BlockSpec((SB_Q, np_ * CHUNK), lambda p, i: (i, p))
    seq = pl.BlockSpec((2 * s, np_ * CHUNK), lambda p, i: (0, p))
    tiles = pl.BlockSpec((np_, s // CHUNK, CHUNK, PAIR), lambda p, i: (p, 0, 0, 0))
    tri = pl.BlockSpec((PAIR, PAIR), lambda p, i: (0, 0))
    return pl.pallas_call(
        body, name="bwd_sb", grid=(SB_W // CHUNK // np_, s // SB_Q),
        in_specs=[blk, seq, tiles, tiles, blk, blk, tri, tri, ORDER_ONLY],
        out_specs=[blk, tiles, tiles],
        out_shape=[jax.ShapeDtypeStruct((s, SB_W), F32)]
        + [jax.ShapeDtypeStruct((SB_W // CHUNK, s // CHUNK, CHUNK, PAIR), F32)] * 2,
        scratch_shapes=[pltpu.VMEM((np_, SB_Q, PAIR), F32), pltpu.VMEM((np_, SB_Q, PAIR), F32),
                        pltpu.VMEM((np_, SB_Q, PAIR), BF16), pltpu.VMEM((np_, SB_Q, PAIR), BF16)],
        compiler_params=_cp("parallel", "arbitrary"),
    )(qn, kst, ktr, vtr, dy, ltot, _pair_tri("upto"), _pair_tri("before"), tok)


def _head_sum(row):
    acc = row[:, 0:HEAD_DIM]
    for h in range(1, SB_W // HEAD_DIM):
        acc = acc + row[:, h * HEAD_DIM:(h + 1) * HEAD_DIM]
    return acc


def _bwd_qk(proj, dqs, dkn, dv, qg, kg):
    s = proj.shape[0]
    tm = LIGHT_TILE
    tiles = pl.BlockSpec((SB_W // CHUNK, tm // CHUNK, CHUNK, PAIR), lambda i: (0, i, 0, 0))

    def body(q_ref, k_ref, dqs_ref, dkn_ref, dv_ref, qg_ref, kg_ref, dp_ref, dqg_ref, dkg_ref, qacc, kacc):
        i = pl.program_id(0)
        gm = _group_mean_matrix(SB_W, HEAD_DIM)

        def one(x, dy, g, acc):
            r = lax.rsqrt(_dot_split(x * x, gm) + RMS_EPS)
            xh = x * r
            dxh = dy * g
            _accumulate(acc, _colsum(dy * xh))
            return r * (dxh - xh * _dot_split(dxh * xh, gm))

        dq = one(q_ref[...], dqs_ref[...] * (HEAD_DIM ** -0.5), qg_ref[...], qacc)
        dk = one(k_ref[...], _unstack_heads(_load_transposed(dkn_ref)), kg_ref[...], kacc)
        dp_ref[...] = jnp.concatenate([dq, dk, _unstack_heads(_load_transposed(dv_ref))], axis=1).astype(BF16)

        @pl.when(i == pl.num_programs(0) - 1)
        def _():
            dqg_ref[...] = _head_sum(qacc[...])
            dkg_ref[...] = _head_sum(kacc[...])

    return pl.pallas_call(
        body, name="bwd_qk", grid=(s // tm,),
        in_specs=[_rows(tm, SB_W, OFF_SB // SB_W), _rows(tm, SB_W, OFF_SB // SB_W + 1),
                  _rows(tm, SB_W), tiles, tiles, _whole((1, SB_W)), _whole((1, SB_W))],
        out_specs=[_rows(tm, 3 * SB_W), _whole((1, HEAD_DIM)), _whole((1, HEAD_DIM))],
        out_shape=[jax.ShapeDtypeStruct((s, 3 * SB_W), BF16), jax.ShapeDtypeStruct((1, HEAD_DIM), F32),
                   jax.ShapeDtypeStruct((1, HEAD_DIM), F32)],
        scratch_shapes=[pltpu.VMEM((1, SB_W), F32), pltpu.VMEM((1, SB_W), F32)],
        compiler_params=_cp("arbitrary"),
    )(proj, proj, dqs, dkn, dv, qg, kg)


def _bwd_sg(proj, dy, lg, lb, w, bias):
    s = proj.shape[0]
    tm = LIGHT_TILE
    nh = SG_W // HEAD_DIM

    def body(p_ref, dy_ref, lg_ref, lb_ref, w_ref, bias_ref, dp_ref, dlg_ref, dlb_ref, dw_ref, db_ref, dbias):
        i = pl.program_id(0)
        uv = p_ref[...]
        ge = _gelu(uv)
        u = ge[:, :SG_W]
        r, xh = _ln_stats(ge[:, SG_W:])
        vln = (xh * lg_ref[...] + lb_ref[...]).astype(BF16)
        dy = dy_ref[...]
        tril, lane_head = _sg_masks()

        @pl.when(i == 0)
        def _():
            dw_ref[...] = jnp.zeros_like(dw_ref)
            dbias[...] = jnp.zeros_like(dbias)

        dus, dvlns = [], []
        for c in range(tm // CHUNK):
            rows = slice(c * CHUNK, (c + 1) * CHUNK)
            vc = vln[rows]
            dus.append(dy[rows] * _sg_mix(w_ref, bias_ref, vc, tril, lane_head))
            dm = dy[rows] * u[rows]
            dbias[...] += dm
            dvc = jnp.zeros((CHUNK, SG_W), F32)
            for h in range(nh):
                dmh = jnp.where(lane_head == h, dm, 0.0).astype(BF16)
                dw_ref[h] += jnp.where(tril, _dot_nt(dmh, vc), 0.0)
                wm = jnp.where(tril, w_ref[h], 0.0).astype(BF16)
                dvc = dvc + _dot_tn(wm, dmh)
            dvlns.append(dvc)
        du = jnp.concatenate(dus, axis=0)
        dvln = jnp.concatenate(dvlns, axis=0)
        _accumulate(dlg_ref, _colsum(dvln * xh))
        _accumulate(dlb_ref, _colsum(dvln))
        dv = _ln_bwd(xh, r, lg_ref[...], dvln)
        dp_ref[...] = (jnp.concatenate([du, dv], axis=1) * _gelu_grad(uv)).astype(BF16)

        @pl.when(i == pl.num_programs(0) - 1)
        def _():
            lane = lax.broadcasted_iota(jnp.int32, (CHUNK, CHUNK), 1)
            acc = dbias[...]
            out = jnp.zeros((CHUNK, CHUNK), F32)
            for h in range(nh):
                hs = jnp.sum(acc[:, h * HEAD_DIM:(h + 1) * HEAD_DIM], axis=1, keepdims=True)
                out = out + jnp.where(lane == h, hs, 0.0)
            db_ref[...] = out

    return pl.pallas_call(
        body, name="bwd_sg", grid=(s // tm,),
        in_specs=[_rows(tm, 2 * SG_W, 1), _rows(tm, SG_W), _whole((1, SG_W)), _whole((1, SG_W)),
                  _whole((nh, CHUNK, CHUNK)), _whole((CHUNK, SG_W))],
        out_specs=[_rows(tm, 2 * SG_W), _whole((1, SG_W)), _whole((1, SG_W)), _whole((nh, CHUNK, CHUNK)),
                   _whole((CHUNK, CHUNK))],
        out_shape=[jax.ShapeDtypeStruct((s, 2 * SG_W), BF16), jax.ShapeDtypeStruct((1, SG_W), F32),
                   jax.ShapeDtypeStruct((1, SG_W), F32), jax.ShapeDtypeStruct((nh, CHUNK, CHUNK), F32),
                   jax.ShapeDtypeStruct((CHUNK, CHUNK), F32)],
        scratch_shapes=[pltpu.VMEM((CHUNK, SG_W), F32)],
        compiler_params=_cp("arbitrary"),
    )(proj, dy, lg, lb, w, bias)


def _bwd_conv(proj, dy, w, b, lg, lb):
    s = proj.shape[0]

    def body(p_ref, dy_ref, w_ref, b_ref, lg_ref, lb_ref, dp_ref, dw_ref, db_ref, dlg_ref, dlb_ref,
             abuf, dcbuf):
        _glu_fill(p_ref, abuf, s)
        dcbuf[pl.ds(s, CHUNK), :] = jnp.zeros((CHUNK, CONV_W), F32)
        dw_ref[...] = jnp.zeros_like(dw_ref)

        def chunk(c, carry):
            db, dlg, dlb = carry
            r0 = pl.multiple_of(c * CHUNK, CHUNK)
            acc, shifted = _conv_window(abuf, r0, w_ref)
            r, xh = _ln_stats(acc + b_ref[...])
            ln = xh * lg_ref[...] + lb_ref[...]
            sg = _sigmoid(ln)
            dl = dy_ref[pl.ds(r0, CHUNK), :] * (sg * (1.0 + ln * (1.0 - sg)))
            dc = _ln_bwd(xh, r, lg_ref[...], dl)
            dcbuf[pl.ds(r0, CHUNK), :] = dc
            for k in range(CONV_K):
                dw_ref[CONV_K - 1 - k:CONV_K - k, :] += _colsum(dc * shifted(k))
            return db + _colsum(dc), dlg + _colsum(dl * xh), dlb + _colsum(dl)

        zero = jnp.zeros((1, CONV_W), F32)
        db, dlg, dlb = lax.fori_loop(0, s // CHUNK, chunk, (zero, zero, zero))
        db_ref[...] = db
        dlg_ref[...] = dlg
        dlb_ref[...] = dlb

        def chunk_back(c, carry):
            r0 = pl.multiple_of(c * CHUNK, CHUNK)
            shifted = _shifted(dcbuf[pl.ds(r0, CHUNK + 32), :], back=False)
            da = jnp.zeros((CHUNK, CONV_W), F32)
            for k in range(CONV_K):
                da = da + shifted(k) * w_ref[CONV_K - 1 - k:CONV_K - k, :]
            pv = p_ref[pl.ds(r0, CHUNK), :]
            val, sg = pv[:, :CONV_W], _sigmoid(pv[:, CONV_W:])
            dp_ref[pl.ds(r0, CHUNK), :] = jnp.concatenate([da * sg, da * val * sg * (1.0 - sg)], axis=1).astype(BF16)
            return carry

        lax.fori_loop(0, s // CHUNK, chunk_back, 0)

    row = _whole((1, CONV_W))
    return pl.pallas_call(
        body, name="bwd_conv", grid=(1,),
        in_specs=[pl.BlockSpec((s, 2 * CONV_W), lambda i: (0, 0)), _whole((s, CONV_W)),
                  _whole((CONV_K, CONV_W)), row, row, row],
        out_specs=[_whole((s, 2 * CONV_W)), _whole((CONV_K, CONV_W)), row, row, row],
        out_shape=[jax.ShapeDtypeStruct((s, 2 * CONV_W), BF16), jax.ShapeDtypeStruct((CONV_K, CONV_W), F32)]
        + [jax.ShapeDtypeStruct((1, CONV_W), F32)] * 3,
        scratch_shapes=[pltpu.VMEM((s + CHUNK, CONV_W), F32), pltpu.VMEM((s + CHUNK, CONV_W), F32)],
        compiler_params=_cp("arbitrary"),
    )(proj, dy, w, b, lg, lb)


def _bwd_in(dpc, dps, dpb, x, g, w, dxm):
    s = x.shape[0]
    tm = TOKEN_TILE

    def body(dpc_ref, dps_ref, dpb_ref, x_ref, g_ref, w_ref, dxm_ref, dx_ref, h_ref, dp_ref, dg_ref):
        dp = jnp.concatenate([dpc_ref[...], dps_ref[...], dpb_ref[...]], axis=1)
        dp_ref[...] = dp
        dh = _dot(dp, w_ref[...])
        r, xh = _rms_stats(x_ref[...])
        h_ref[...] = (xh * g_ref[...]).astype(BF16)
        dx, dgrow = _rms_bwd(xh, r, g_ref[...], dh)
        dx_ref[...] = dxm_ref[...] + dx
        _accumulate(dg_ref, _colsum(dgrow))

    return pl.pallas_call(
        body, name="bwd_in", grid=(s // tm,),
        in_specs=[_rows(tm, 2 * CONV_W), _rows(tm, 2 * SG_W), _rows(tm, 3 * SB_W), _rows(tm, D_MODEL),
                  _whole((1, D_MODEL)), _whole((IN_W, D_MODEL)), _rows(tm, D_MODEL)],
        out_specs=[_rows(tm, D_MODEL), _rows(tm, D_MODEL), _rows(tm, IN_W), _whole((1, D_MODEL))],
        out_shape=[jax.ShapeDtypeStruct((s, D_MODEL), F32), jax.ShapeDtypeStruct((s, D_MODEL), BF16),
                   jax.ShapeDtypeStruct((s, IN_W), BF16), jax.ShapeDtypeStruct((1, D_MODEL), F32)],
        compiler_params=_cp("arbitrary"),
    )(dpc, dps, dpb, x, g, w, dxm)


SMALL = ("mix_norm_g", "conv_w", "conv_b", "conv_ln_g", "conv_ln_b", "sg_ln_g", "sg_ln_b", "sg_w", "sg_b",
         "q_norm_g", "k_norm_g", "out_norm_g", "ffn_norm_g")
LARGE = ("w_in", "w_out", "w_gate_up", "w_down")


def _row(v):
    return v.reshape(1, -1)


def _layer_params(p, large, l):
    q = {k: v[l] for k, v in p.items()}
    return dict(
        q, **large,
        mix_norm_g=_row(q["mix_norm_g"]), conv_b=_row(q["conv_b"]), conv_ln_g=_row(q["conv_ln_g"]),
        conv_ln_b=_row(q["conv_ln_b"]), sg_ln_g=_row(q["sg_ln_g"]), sg_ln_b=_row(q["sg_ln_b"]),
        out_norm_g=_row(q["out_norm_g"]), ffn_norm_g=_row(q["ffn_norm_g"]),
        qg=_row(jnp.tile(q["q_norm_g"], SB_W // HEAD_DIM)), kg=_row(jnp.tile(q["k_norm_g"], SB_W // HEAD_DIM)),
        sg_bias=jnp.repeat(q["sg_b"].T, HEAD_DIM, axis=1),
    )


def _layer_fwd(x, q, tok, after_in, after_mixers, after_out):
    proj, qn, kn, vb, kt, vt = _fwd_in(x, q["mix_norm_g"], q["w_in"], q["qg"], q["kg"], tok)
    yc = _fwd_conv(proj, q["conv_w"], q["conv_b"], q["conv_ln_g"], q["conv_ln_b"], after_in(proj))
    ys = _fwd_sg(proj, q["sg_ln_g"], q["sg_ln_b"], q["sg_w"], q["sg_bias"])
    yb, lt = _fwd_sb(qn, kt, vb)
    rest, tok = after_mixers(yb)
    q = dict(q, **rest)
    xm = _fwd_out(yc, ys, yb, q["out_norm_g"], q["w_out"], x, tok)
    gu, xo = _fwd_ffn(xm, q["ffn_norm_g"], q["w_gate_up"], q["w_down"], after_out(xm))
    return xo, q, dict(x=x, proj=proj, qn=qn, kn=kn, kt=kt, vt=vt, lt=lt, yc=yc, ys=ys, yb=yb, xm=xm, gu=gu)


def _layer_bwd_ffn(dxo, q, st, tok):
    dgu, act, h2, dxm, d_ffn_g = _bwd_ffn(dxo, st["gu"], st["xm"], q["ffn_norm_g"], q["w_gate_up"], q["w_down"],
                                          tok)
    d_wgu, d_wd = _matmul_tn(dgu, h2, 512, D_MODEL), _matmul_tn(act, dxo, FFN_H // 2, D_MODEL)
    dyc, dys, dyb, cat, d_out_g = _bwd_out(dxm, st["yc"], st["ys"], st["yb"], q["out_norm_g"], q["w_out"], tok)
    return dxm, (dyc, dys, dyb, d_ffn_g, d_out_g), d_wgu, d_wd, _matmul_tn(cat, dxm, 512, D_MODEL)


def _layer_bwd_mix(dxm, carried, q, st, tok):
    dyc, dys, dyb, d_ffn_g, d_out_g = carried
    dqs, dkn, dv = _bwd_sb(st["qn"], st["kn"], st["kt"], st["vt"], dyb, st["lt"], tok)
    dpb, d_qg, d_kg = _bwd_qk(st["proj"], dqs, dkn, dv, q["qg"], q["kg"])
    dps, d_sg_lg, d_sg_lb, d_sg_w, d_sg_b = _bwd_sg(st["proj"], dys, q["sg_ln_g"], q["sg_ln_b"], q["sg_w"],
                                                    q["sg_bias"])
    dpc, d_conv_w, d_conv_b, d_conv_lg, d_conv_lb = _bwd_conv(st["proj"], dyc, q["conv_w"], q["conv_b"],
                                                              q["conv_ln_g"], q["conv_ln_b"])
    dx, h1, dp, d_mix_g = _bwd_in(dpc, dps, dpb, st["x"], q["mix_norm_g"], q["w_in"], dxm)
    d_win = _matmul_tn(dp, h1, 512, D_MODEL)
    small = dict(
        mix_norm_g=d_mix_g[0], conv_w=d_conv_w, conv_b=d_conv_b[0], conv_ln_g=d_conv_lg[0],
        conv_ln_b=d_conv_lb[0], sg_ln_g=d_sg_lg[0], sg_ln_b=d_sg_lb[0], sg_w=d_sg_w,
        sg_b=d_sg_b[:, :SG_W // HEAD_DIM].T, q_norm_g=d_qg[0], k_norm_g=d_kg[0], out_norm_g=d_out_g[0],
        ffn_norm_g=d_ffn_g[0])
    return dx, d_win, small


def _position():
    x, y, c = lax.axis_index("x"), lax.axis_index("y"), lax.axis_index("c")
    return x, y, c


def _flat(px, py, pc):
    return 4 * px + 2 * py + pc


IN_HBM = pl.BlockSpec(memory_space=pltpu.HBM)
IN_SEM = pl.BlockSpec(memory_space=pltpu.SEMAPHORE)
EFFECT = pltpu.SideEffectType.DATAFLOW_SIDE_EFFECTING
COPIES = dict(scatter=7, spread=7, spread_chips=4, **{"pass": 3})


def _exchange_copies(kinds, src_refs, land_refs, send_sems, recv_sems, layer, arrival):
    x, y, c = _position()
    me = _flat(x, y, c)
    everyone = [(x ^ (k >> 2 & 1), y ^ (k >> 1 & 1), c ^ (k & 1)) for k in range(1, N_DEV)]
    sibling = (x, y, 1 - c)
    chips = [(1 - x, y, c), (x, 1 - y, c), (1 - x, 1 - y, c)]
    out = []
    srcs = iter(src_refs)
    for kind, land in zip(kinds, land_refs):
        land = land if layer is None else land.at[layer]
        if kind == "scatter":
            src = next(srcs)
            moves = [(src.at[_flat(*p)], me, _flat(*p), p) for p in everyone]
        elif kind in ("spread", "spread_chips"):
            src = next(srcs)
            moves = [(src, me, _flat(*p), p) for p in (everyone if kind == "spread" else [sibling] + chips)]
        else:
            moves = [(land.at[_flat(*p)], _flat(*p), _flat(p[0], p[1], 1 - c), sibling) for p in chips]
        for src_block, there, here, peer in moves:
            n = len(out)
            out.append(pltpu.make_async_remote_copy(
                src_ref=src_block, dst_ref=land.at[here if arrival else there], send_sem=send_sems.at[n],
                recv_sem=recv_sems.at[n], device_id=peer, device_id_type=MESH))
    return out


def _exchange_start(kinds, srcs, lands, after, name, layer=None):
    ns, n = len(srcs), len(srcs) + len(lands)
    sems = sum(COPIES[k] for k in kinds)

    def body(*refs):
        send_sems, recv_sems = refs[n + 1], refs[n + 2]
        for cp in _exchange_copies(kinds, refs[:ns], refs[ns:n], send_sems, recv_sems, layer, arrival=False):
            cp.start()
        refs[-1][...] = jnp.zeros_like(refs[-1])

    thru = [pltpu.HBM(a.shape, a.dtype) for a in (*srcs, *lands)]
    outs = pl.pallas_call(
        body, name=name,
        out_shape=(pltpu.SemaphoreType.DMA((sems,)), pltpu.SemaphoreType.DMA((sems,)), *thru,
                   jax.ShapeDtypeStruct((8, 128), F32)),
        in_specs=[IN_HBM] * n + [ORDER_ONLY],
        out_specs=(IN_SEM, IN_SEM, *[IN_HBM] * n, pl.BlockSpec(memory_space=pltpu.VMEM)),
        input_output_aliases={i: 2 + i for i in range(n)},
        compiler_params=pltpu.CompilerParams(has_side_effects=EFFECT),
    )(*[pltpu.with_memory_space_constraint(a, pltpu.HBM) for a in (*srcs, *lands)], after)
    return kinds, outs[0], outs[1], list(outs[2:2 + ns]), list(outs[2 + ns:2 + n]), outs[-1]


def _exchange_wait(pending, after, name, layer=None):
    kinds, send_sems, recv_sems, srcs, lands, _ = pending
    after = list(after) if isinstance(after, (list, tuple)) else [after]
    ns, n = len(srcs), len(srcs) + len(lands)

    def body(*refs):
        for cp in _exchange_copies(kinds, refs[:ns], refs[ns:n], refs[n], refs[n + 1], layer, arrival=True):
            cp.wait_send()
            cp.wait_recv()

    thru = [pltpu.HBM(a.shape, a.dtype) for a in (*srcs, *lands)]
    outs = pl.pallas_call(
        body, name=name, out_shape=tuple(thru),
        in_specs=[IN_HBM] * n + [IN_SEM, IN_SEM] + [ORDER_ONLY] * len(after),
        out_specs=tuple([IN_HBM] * n),
        input_output_aliases={i: i for i in range(n)},
        compiler_params=pltpu.CompilerParams(has_side_effects=EFFECT),
    )(*srcs, *lands, send_sems, recv_sems, *after)
    return list(outs[ns:])


def _landing(block, me):
    land = lax.empty((N_DEV,) + block.shape, block.dtype)
    return lax.dynamic_update_index_in_dim(land, block, me, 0)


def _adamw(parts, w, m, v, tr):
    groups, rows, cols = w.shape

    def body(p_ref, w_ref, m_ref, v_ref, g_ref, d_ref, nm_ref, nv_ref):
        g = p_ref[0].astype(F32)
        for j in range(1, N_DEV):
            g = g + p_ref[j].astype(F32)
        g_ref[...] = g
        m = ADAM_B1 * m_ref[...] + (1.0 - ADAM_B1) * g
        v = ADAM_B2 * v_ref[...] + (1.0 - ADAM_B2) * (g * g)
        nm_ref[...] = m
        nv_ref[...] = v
        m_hat = m / (1.0 - ADAM_B1 ** ADAM_STEP)
        v_hat = v / (1.0 - ADAM_B2 ** ADAM_STEP)
        d_ref[...] = -ADAM_LR * (m_hat / (jnp.sqrt(v_hat) + ADAM_EPS) + ADAM_WD * w_ref[...])

    blk = pl.BlockSpec((None, tr, cols), lambda g, i: (g, i, 0))
    return pl.pallas_call(
        body, name="adamw", grid=(groups, rows // tr),
        in_specs=[pl.BlockSpec((None, N_DEV, tr, cols), lambda g, i: (g, 0, i, 0)), blk, blk, blk],
        out_specs=[blk] * 4,
        out_shape=[jax.ShapeDtypeStruct((groups, rows, cols), F32)] * 4,
        compiler_params=_cp("parallel", "parallel"),
    )(parts, w, m, v)


def _row_tile(rows):
    for cand in range(min(rows, 512) // 8 * 8, 7, -8):
        if rows % cand == 0:
            return cand
    return rows


def _with_own_block(land, blocks, layer, me):
    own = lax.dynamic_index_in_dim(blocks, me, 0, keepdims=True)[None]
    return lax.dynamic_update_slice(land, own, (layer, me, 0, 0))


PACK_LANES = 128


def _pack_layers(arrs):
    parts = []
    for a in arrs:
        flat = a.reshape(a.shape[0], -1)
        parts.append(jnp.pad(flat, ((0, 0), (0, -flat.shape[1] % (8 * PACK_LANES)))))
    return jnp.concatenate(parts, axis=1).reshape(arrs[0].shape[0], -1, PACK_LANES)


def _unpack_layers(packed, shapes):
    flat = packed.reshape(packed.shape[0], -1)
    outs, off = [], 0
    for shp in shapes:
        size = 1
        for d in shp[1:]:
            size *= d
        outs.append(flat[:, off:off + size].reshape(shp))
        off += size + (-size % (8 * PACK_LANES))
    return outs


def kernel(x, mix_norm_g, w_in, conv_w, conv_b, conv_ln_g, conv_ln_b, sg_ln_g, sg_ln_b, sg_w, sg_b, q_norm_g, k_norm_g, out_norm_g, w_out, ffn_norm_g, w_gate_up, w_down, loss_target, m_mix_norm_g, m_w_in, m_conv_w, m_conv_b, m_conv_ln_g, m_conv_ln_b, m_sg_ln_g, m_sg_ln_b, m_sg_w, m_sg_b, m_q_norm_g, m_k_norm_g, m_out_norm_g, m_w_out, m_ffn_norm_g, m_w_gate_up, m_w_down, v_mix_norm_g, v_w_in, v_conv_w, v_conv_b, v_conv_ln_g, v_conv_ln_b, v_sg_ln_g, v_sg_ln_b, v_sg_w, v_sg_b, v_q_norm_g, v_k_norm_g, v_out_norm_g, v_w_out, v_ffn_norm_g, v_w_gate_up, v_w_down):
    names = SMALL[:1] + LARGE[:1] + SMALL[1:12] + LARGE[1:2] + SMALL[12:] + LARGE[2:]
    w = dict(mix_norm_g=mix_norm_g, w_in=w_in, conv_w=conv_w, conv_b=conv_b, conv_ln_g=conv_ln_g,
             conv_ln_b=conv_ln_b, sg_ln_g=sg_ln_g, sg_ln_b=sg_ln_b, sg_w=sg_w, sg_b=sg_b, q_norm_g=q_norm_g,
             k_norm_g=k_norm_g, out_norm_g=out_norm_g, w_out=w_out, ffn_norm_g=ffn_norm_g,
             w_gate_up=w_gate_up, w_down=w_down)
    m = dict(mix_norm_g=m_mix_norm_g, w_in=m_w_in, conv_w=m_conv_w, conv_b=m_conv_b, conv_ln_g=m_conv_ln_g,
             conv_ln_b=m_conv_ln_b, sg_ln_g=m_sg_ln_g, sg_ln_b=m_sg_ln_b, sg_w=m_sg_w, sg_b=m_sg_b,
             q_norm_g=m_q_norm_g, k_norm_g=m_k_norm_g, out_norm_g=m_out_norm_g, w_out=m_w_out,
             ffn_norm_g=m_ffn_norm_g, w_gate_up=m_w_gate_up, w_down=m_w_down)
    v = dict(mix_norm_g=v_mix_norm_g, w_in=v_w_in, conv_w=v_conv_w, conv_b=v_conv_b, conv_ln_g=v_conv_ln_g,
             conv_ln_b=v_conv_ln_b, sg_ln_g=v_sg_ln_g, sg_ln_b=v_sg_ln_b, sg_w=v_sg_w, sg_b=v_sg_b,
             q_norm_g=v_q_norm_g, k_norm_g=v_k_norm_g, out_norm_g=v_out_norm_g, w_out=v_w_out,
             ffn_norm_g=v_ffn_norm_g, w_gate_up=v_w_gate_up, w_down=v_w_down)
    xpos, ypos, cpos = _position()
    me = _flat(xpos, ypos, cpos)
    conv_cols = conv_w.shape[-1]
    no_token = jnp.zeros((8, 128), F32)
    w, m, v = (dict(t, w_in=jnp.swapaxes(t["w_in"], 1, 2), w_gate_up=jnp.swapaxes(t["w_gate_up"], 1, 2))
               for t in (w, m, v))
    shards = {k: w[k].astype(BF16) for k in LARGE}
    full_shape = dict(w_in=(IN_W, D_MODEL), w_out=(D_MODEL, D_MODEL), w_gate_up=(2 * FFN_H, D_MODEL),
                      w_down=(FFN_H, D_MODEL))

    def gather_start(srcs, after, tag):
        return _exchange_start(["spread_chips"] * len(srcs), srcs, [_landing(a, me) for a in srcs], after,
                               f"gather_start_{tag}")

    def gather_pass(pending, after, tag):
        lands = _exchange_wait(pending, after, f"gather_wait_{tag}")
        return _exchange_start(["pass"] * len(lands), [], lands, after, f"gather_pass_{tag}")

    def gathered(pending, keys, after, tag):
        lands = _exchange_wait(pending, after, f"gather_passed_{tag}")
        return {k: a.reshape(full_shape[k]) for k, a in zip(keys, lands)}, lands[len(keys):]

    first, later = ("w_in",), ("w_out", "w_gate_up", "w_down")
    act = x[0]
    head = gather_start([shards["w_in"][0], w["conv_w"]], act, "0")
    tail = gather_start([shards[k][0] for k in later], head[5], "0_later")
    head = gather_pass(head, tail[5], "0")
    large, (conv_blocks,) = gathered(head, first, head[5], "0")
    conv_full = jnp.transpose(conv_blocks, (1, 2, 0, 3)).reshape(DEPTH, CONV_K, CONV_W)
    small_w = dict({k: w[k] for k in SMALL}, conv_w=conv_full)
    qs, stash = [], []
    for l in range(DEPTH):
        coming = {}
        more = l + 1 < DEPTH

        def next_start(after):
            coming["first"] = gather_start([shards[k][l + 1] for k in LARGE], after, str(l + 1))
            return coming["first"][5]

        def after_in(proj):
            return next_start(proj) if more and l == 0 else no_token

        def after_mixers(y_sb):
            if l > 0:
                return {}, no_token
            passing = gather_pass(tail, y_sb, "0_later")
            return gathered(passing, later, passing[5], "0_later")[0], passing[5]

        def after_out(x_mid):
            if not more or l == 0:
                return no_token
            coming["second"] = gather_pass(coming["first"], x_mid, str(l + 1))
            return coming["second"][5]

        token = next_start(act) if more and l > 0 else no_token
        act, q, st = _layer_fwd(act, _layer_params(small_w, large, l), token, after_in, after_mixers, after_out)
        qs.append(q)
        stash.append(st)
        if more:
            if l == 0:
                coming["second"] = gather_pass(coming["first"], act, str(l + 1))
            large, _ = gathered(coming["second"], LARGE, act, str(l + 1))

    loss, dx = _loss_head(act, loss_target[0])
    loss = lax.psum(loss[0, 0], ("x", "y", "c"))

    replicated = tuple(k for k in SMALL if k != "conv_w")
    small_rows = _pack_layers([w[k][:1] for k in replicated]).shape[1]
    conv_rows = _pack_layers([conv_full[:1]]).shape[1]
    group_a, group_b = ("w_gate_up", "w_down", "w_out"), ("w_in",)
    blocks = lambda k, a: a.reshape((N_DEV,) + w[k].shape[1:])
    land_a = [lax.empty((DEPTH, N_DEV) + w[k].shape[1:], BF16) for k in group_a]
    land_b = [lax.empty((DEPTH, N_DEV) + w[k].shape[1:], BF16) for k in group_b]
    land_b.append(lax.empty((DEPTH, N_DEV, small_rows + conv_rows, PACK_LANES), F32))
    pend_a = pend_b = None
    token = no_token
    for l in reversed(range(DEPTH)):
        dxm, carried, d_wgu, d_wd, d_wo = _layer_bwd_ffn(dx, qs[l], stash[l], token)
        srcs = [blocks(k, a) for k, a in zip(group_a, (d_wgu, d_wd, d_wo))]
        if pend_a is not None:
            land_a = _exchange_wait(pend_a, d_wo, f"grads_a_wait_{l + 1}", layer=l + 1)
        land_a = [_with_own_block(ld, a, l, me) for ld, a in zip(land_a, srcs)]
        pend_a = _exchange_start(["scatter"] * 3, srcs, land_a, dxm, f"grads_a_start_{l}", layer=l)
        dx, d_win, small = _layer_bwd_mix(dxm, carried, qs[l], stash[l], pend_a[5])
        packed = _pack_layers([small[k][None] for k in replicated + ("conv_w",)])[0]
        srcs = [blocks("w_in", d_win)]
        if pend_b is not None:
            land_b = _exchange_wait(pend_b, d_win, f"grads_b_wait_{l + 1}", layer=l + 1)
        land_b = [_with_own_block(land_b[0], srcs[0], l, me),
                  lax.dynamic_update_slice(land_b[1], packed[None, None], (l, me, 0, 0))]
        pend_b = _exchange_start(["scatter", "spread"], srcs + [packed], land_b, dx, f"grads_b_start_{l}",
                                 layer=l)
        token = pend_b[5]

    land_a = _exchange_wait(pend_a, token, "grads_a_wait_0", layer=0)
    res = {}
    for k, parts in zip(group_a, land_a):
        res[k] = _adamw(parts, w[k], m[k], v[k], _row_tile(w[k].shape[1]))
    land_b = _exchange_wait(pend_b, [res[k][0] for k in group_a], "grads_b_wait_0", layer=0)
    res["w_in"] = _adamw(land_b[0], w["w_in"], m["w_in"], v["w_in"], _row_tile(w["w_in"].shape[1]))
    small_parts = land_b[1]
    updated = _adamw(small_parts, *(_pack_layers([t[k] for k in replicated]) for t in (w, m, v)), small_rows)
    unpacked = [_unpack_layers(o, [w[k].shape for k in replicated]) for o in updated]
    res.update({k: [u[i] for u in unpacked] for i, k in enumerate(replicated)})
    conv_parts = small_parts[:, :, small_rows:].reshape(DEPTH, N_DEV, -1)[:, :, :CONV_K * CONV_W]
    conv_parts = lax.dynamic_slice_in_dim(conv_parts.reshape(DEPTH, N_DEV, CONV_K, CONV_W), me * conv_cols,
                                          conv_cols, axis=3)
    res["conv_w"] = _adamw(conv_parts, w["conv_w"], m["conv_w"], v["conv_w"], CONV_K)
    for k in ("w_in", "w_gate_up"):
        res[k] = [jnp.swapaxes(a, 1, 2) for a in res[k]]

    return (loss, dx[None], *[res[k][0] for k in names], *[res[k][1] for k in names],
            *[res[k][2] for k in names], *[res[k][3] for k in names])
```

```python
import jax
import jax.numpy as jnp
from jax import lax
from jax.experimental import pallas as pl
from jax.experimental.pallas import tpu as pltpu

F32 = jnp.float32
BF16 = jnp.bfloat16

D_MODEL = 1024
DEPTH = 4
HEAD_DIM = 64
CONV_W = 256
SG_W = 256
SB_W = 512
IN_W = 2560
FFN_H = 2816
CONV_K = 31
CHUNK = 128
OFF_SG = 2 * CONV_W
OFF_SB = OFF_SG + 2 * SG_W
RMS_EPS = 1e-6
LN_EPS = 1e-5
N_DEV = 8
MESH = pl.DeviceIdType.MESH

ADAM_LR = 0.001
ADAM_B1 = 0.9
ADAM_B2 = 0.999
ADAM_EPS = 1e-08
ADAM_WD = 0.01
ADAM_STEP = 10

TOKEN_TILE = 512
FFN_BWD_TILE = 256
LIGHT_TILE = 1024
VMEM_LIMIT = 56 * 1024 * 1024


def _cp(*sem):
    return pltpu.CompilerParams(dimension_semantics=sem or None, vmem_limit_bytes=VMEM_LIMIT)


def _dot(a, b):
    return jnp.dot(a, b, preferred_element_type=F32)


def _dot_nt(a, b):
    return lax.dot_general(a, b, (((1,), (1,)), ((), ())), preferred_element_type=F32)


def _dot_tn(a, b):
    return lax.dot_general(a, b, (((0,), (0,)), ((), ())), preferred_element_type=F32)


def _dot_split(x, m):
    hi = x.astype(BF16)
    lo = (x - hi.astype(F32)).astype(BF16)
    return _dot(hi, m) + _dot(lo, m)


def _group_mean_matrix(width, group):
    r = lax.broadcasted_iota(jnp.int32, (width, width), 0) // group
    c = lax.broadcasted_iota(jnp.int32, (width, width), 1) // group
    return jnp.where(r == c, 1.0 / group, 0.0).astype(BF16)


def _sigmoid(x):
    return 1.0 / (1.0 + jnp.exp(-x))


def _gelu(x):
    return 0.5 * x * (1.0 + lax.erf(x * (2.0 ** -0.5)))


def _gelu_grad(x):
    return 0.5 * (1.0 + lax.erf(x * (2.0 ** -0.5))) + x * jnp.exp(-0.5 * x * x) * (0.5 * (2.0 / jnp.pi) ** 0.5)


def _rms_stats(x):
    r = lax.rsqrt(jnp.mean(x * x, axis=-1, keepdims=True) + RMS_EPS)
    return r, x * r


def _rms_bwd(xh, r, g, dy):
    dxh = dy * g
    dx = r * (dxh - xh * jnp.mean(dxh * xh, axis=-1, keepdims=True))
    return dx, dy * xh


def _ln_stats(x):
    mu = jnp.mean(x, axis=-1, keepdims=True)
    xc = x - mu
    r = lax.rsqrt(jnp.mean(xc * xc, axis=-1, keepdims=True) + LN_EPS)
    return r, xc * r


def _ln_bwd(xh, r, g, dy):
    dxh = dy * g
    return r * (dxh - jnp.mean(dxh, axis=-1, keepdims=True) - xh * jnp.mean(dxh * xh, axis=-1, keepdims=True))


def _colsum(x):
    return jnp.sum(x, axis=0, keepdims=True)


def _rows(tm, n, j=0):
    return pl.BlockSpec((tm, n), lambda i: (i, j))


def _whole(shape):
    return pl.BlockSpec(shape, lambda i: (0,) * len(shape))


ORDER_ONLY = pl.BlockSpec(memory_space=pl.ANY)


def _stack_heads(a):
    even = (lax.broadcasted_iota(jnp.int32, a.shape, 1) % (2 * HEAD_DIM)) < HEAD_DIM
    top = jnp.where(even, a, 0.0)
    bot = jnp.where(even, 0.0, a)
    parts = []
    for c in range(a.shape[0] // CHUNK):
        rows = slice(c * CHUNK, (c + 1) * CHUNK)
        parts += [top[rows], bot[rows]]
    return jnp.concatenate(parts, axis=0)


def _store_stacked(st, st_ref, tr_ref):
    st_ref[...] = st.astype(BF16)
    for p in range(SB_W // CHUNK):
        for c in range(st.shape[0] // (2 * CHUNK)):
            tile = st[2 * c * CHUNK:2 * (c + 1) * CHUNK, p * CHUNK:(p + 1) * CHUNK]
            tr_ref[p, c] = tile.T.astype(BF16)


def _load_transposed(tr_ref):
    rows = []
    for c in range(tr_ref.shape[1]):
        tiles = [tr_ref[p, c].T for p in range(SB_W // CHUNK)]
        rows.append(jnp.concatenate(tiles, axis=1))
    return jnp.concatenate(rows, axis=0)


def _unstack_heads(st):
    even = (lax.broadcasted_iota(jnp.int32, (CHUNK, st.shape[1]), 1) % (2 * HEAD_DIM)) < HEAD_DIM
    parts = []
    for c in range(st.shape[0] // (2 * CHUNK)):
        top = st[2 * c * CHUNK:(2 * c + 1) * CHUNK]
        bot = st[(2 * c + 1) * CHUNK:(2 * c + 2) * CHUNK]
        parts.append(jnp.where(even, top, bot))
    return jnp.concatenate(parts, axis=0)


def _fwd_in(x, g, w, qg, kg, tok):
    s = x.shape[0]
    tm = TOKEN_TILE

    def body(x_ref, g_ref, w_ref, qg_ref, kg_ref, tok_ref, proj_ref, qn_ref, kn_ref, vb_ref, kt_ref, vt_ref):
        r, xh = _rms_stats(x_ref[...])
        h = (xh * g_ref[...]).astype(BF16)
        proj = _dot_nt(h, w_ref[...])
        proj_ref[...] = proj
        gm = _group_mean_matrix(SB_W, HEAD_DIM)
        q = proj[:, OFF_SB:OFF_SB + SB_W]
        k = proj[:, OFF_SB + SB_W:OFF_SB + 2 * SB_W]
        rq = lax.rsqrt(_dot_split(q * q, gm) + RMS_EPS)
        rk = lax.rsqrt(_dot_split(k * k, gm) + RMS_EPS)
        qn_ref[...] = (q * rq * qg_ref[...] * (HEAD_DIM ** -0.5)).astype(BF16)
        _store_stacked(_stack_heads(k * rk * kg_ref[...]), kn_ref, kt_ref)
        _store_stacked(_stack_heads(proj[:, OFF_SB + 2 * SB_W:]), vb_ref, vt_ref)

    tiles = pl.BlockSpec((SB_W // CHUNK, tm // CHUNK, CHUNK, PAIR), lambda i: (0, i, 0, 0))
    tiles_shape = jax.ShapeDtypeStruct((SB_W // CHUNK, s // CHUNK, CHUNK, PAIR), BF16)
    return pl.pallas_call(
        body, name="fwd_in", grid=(s // tm,),
        in_specs=[_rows(tm, D_MODEL), _whole((1, D_MODEL)), _whole((IN_W, D_MODEL)),
                  _whole((1, SB_W)), _whole((1, SB_W)), ORDER_ONLY],
        out_specs=[_rows(tm, IN_W), _rows(tm, SB_W), _rows(2 * tm, SB_W), _rows(2 * tm, SB_W), tiles, tiles],
        out_shape=[jax.ShapeDtypeStruct((s, IN_W), F32), jax.ShapeDtypeStruct((s, SB_W), BF16),
                   jax.ShapeDtypeStruct((2 * s, SB_W), BF16), jax.ShapeDtypeStruct((2 * s, SB_W), BF16),
                   tiles_shape, tiles_shape],
        compiler_params=_cp("parallel"),
    )(x, g, w, qg, kg, tok)


SUBLANES = 8


def _shifted(win, back):
    n = win.shape[0]
    turned = [win] + [pltpu.roll(win, b if back else n - b, axis=0) for b in range(1, SUBLANES)]

    def shifted(k):
        whole, part = divmod(k, SUBLANES)
        start = 32 - whole * SUBLANES if back else whole * SUBLANES
        return turned[part][start:start + CHUNK, :]

    return shifted


def _conv_window(abuf, r0, w_ref):
    shifted = _shifted(abuf[pl.ds(pl.multiple_of(r0 + CHUNK - 32, 32), CHUNK + 32), :], back=True)
    acc = jnp.zeros((CHUNK, CONV_W), F32)
    for k in range(CONV_K):
        acc = acc + shifted(k) * w_ref[CONV_K - 1 - k:CONV_K - k, :]
    return acc, shifted


def _glu_fill(p_ref, abuf, s):
    abuf[0:CHUNK, :] = jnp.zeros((CHUNK, CONV_W), F32)

    def fill(c, carry):
        r0 = pl.multiple_of(c * CHUNK, CHUNK)
        pv = p_ref[pl.ds(r0, CHUNK), :]
        abuf[pl.ds(r0 + CHUNK, CHUNK), :] = pv[:, :CONV_W] * _sigmoid(pv[:, CONV_W:])
        return carry

    lax.fori_loop(0, s // CHUNK, fill, 0)


def _fwd_conv(proj, w, b, lg, lb, tok):
    s = proj.shape[0]

    def body(p_ref, w_ref, b_ref, lg_ref, lb_ref, tok_ref, y_ref, abuf):
        _glu_fill(p_ref, abuf, s)

        def chunk(c, carry):
            r0 = pl.multiple_of(c * CHUNK, CHUNK)
            acc, _ = _conv_window(abuf, r0, w_ref)
            r, xh = _ln_stats(acc + b_ref[...])
            ln = xh * lg_ref[...] + lb_ref[...]
            y_ref[pl.ds(r0, CHUNK), :] = ln * _sigmoid(ln)
            return carry

        lax.fori_loop(0, s // CHUNK, chunk, 0)

    return pl.pallas_call(
        body, name="fwd_conv", grid=(1,),
        in_specs=[pl.BlockSpec((s, 2 * CONV_W), lambda i: (0, 0)), _whole((CONV_K, CONV_W)),
                  _whole((1, CONV_W)), _whole((1, CONV_W)), _whole((1, CONV_W)), ORDER_ONLY],
        out_specs=_whole((s, CONV_W)),
        out_shape=jax.ShapeDtypeStruct((s, CONV_W), F32),
        scratch_shapes=[pltpu.VMEM((s + CHUNK, CONV_W), F32)],
        compiler_params=_cp("arbitrary"),
    )(proj, w, b, lg, lb, tok)


def _sg_masks():
    row = lax.broadcasted_iota(jnp.int32, (CHUNK, CHUNK), 0)
    col = lax.broadcasted_iota(jnp.int32, (CHUNK, CHUNK), 1)
    lane_head = lax.broadcasted_iota(jnp.int32, (CHUNK, SG_W), 1) // HEAD_DIM
    return row >= col, lane_head


def _sg_mix(w_ref, bias_ref, vc, tril, lane_head):
    mixed = bias_ref[...]
    for h in range(SG_W // HEAD_DIM):
        wm = jnp.where(tril, w_ref[h], 0.0).astype(BF16)
        mixed = mixed + jnp.where(lane_head == h, _dot(wm, vc), 0.0)
    return mixed


def _fwd_sg(proj, lg, lb, w, bias):
    s = proj.shape[0]
    tm = LIGHT_TILE

    def body(p_ref, lg_ref, lb_ref, w_ref, bias_ref, y_ref):
        ge = _gelu(p_ref[...])
        u = ge[:, :SG_W]
        r, xh = _ln_stats(ge[:, SG_W:])
        vln = (xh * lg_ref[...] + lb_ref[...]).astype(BF16)
        tril, lane_head = _sg_masks()
        for c in range(tm // CHUNK):
            rows = slice(c * CHUNK, (c + 1) * CHUNK)
            y_ref[rows, :] = u[rows] * _sg_mix(w_ref, bias_ref, vln[rows], tril, lane_head)

    return pl.pallas_call(
        body, name="fwd_sg", grid=(s // tm,),
        in_specs=[_rows(tm, 2 * SG_W, 1), _whole((1, SG_W)), _whole((1, SG_W)),
                  _whole((SG_W // HEAD_DIM, CHUNK, CHUNK)), _whole((CHUNK, SG_W))],
        out_specs=_rows(tm, SG_W),
        out_shape=jax.ShapeDtypeStruct((s, SG_W), F32),
        compiler_params=_cp("parallel"),
    )(proj, lg, lb, w, bias)


SB_Q_FWD = 16 * CHUNK
SB_Q_BWD = 8 * CHUNK
PAIR = 2 * CHUNK
SB_PAIRS = 1
SB_BAND = 512


def _pair_tri(kind):
    row = lax.broadcasted_iota(jnp.int32, (PAIR, PAIR), 0)
    col = lax.broadcasted_iota(jnp.int32, (PAIR, PAIR), 1)
    tri = {"after": row > col, "upto": row <= col, "before": row < col}[kind]
    return jnp.where(((row // CHUNK) == (col // CHUNK)) & tri, 1.0, 0.0).astype(BF16)


def _causal(x):
    row = lax.broadcasted_iota(jnp.int32, (CHUNK, PAIR), 0)
    col = lax.broadcasted_iota(jnp.int32, (CHUNK, PAIR), 1) % CHUNK
    top = jnp.where(col < row, x[:CHUNK], 0.0)
    return top if x.shape[0] == CHUNK else jnp.concatenate([top, x[CHUNK:]], axis=0)


def _sb_scores(z, masked):
    sp = jnp.maximum(z, 0.0) + jnp.log(1.0 + jnp.exp(-jnp.abs(z)))
    return z, sp, (_causal(sp) if masked else sp).astype(BF16)


def _per_head(c0, c1):
    rows = c0.shape[0]
    return jnp.concatenate([jnp.broadcast_to(c0, (rows, CHUNK)), jnp.broadcast_to(c1, (rows, CHUNK))], axis=1)


def _fwd_sb(qn, ktr, vst):
    s = qn.shape[0]
    SB_Q = min(SB_Q_FWD, s)
    np_ = SB_PAIRS

    def body(q_ref, k_ref, v_ref, after_ref, o_ref, lt_ref, z_buf, att_buf):
        i = pl.program_id(1)
        first = i * (SB_Q // CHUNK)
        last = first + SB_Q // CHUNK - 1
        after = after_ref[...]
        lanes = [slice(pr * CHUNK, (pr + 1) * CHUNK) for pr in range(np_)]
        qs = [q_ref[:, lanes[pr]] for pr in range(np_)]

        def rows(kb):
            return pl.ds(pl.multiple_of(kb * PAIR, PAIR), PAIR)

        def block(kb, carry, masked, top=0):
            out = []
            for pr in range(np_):
                acc, c0, c1 = carry[pr]
                z_next = _dot(qs[pr], k_ref[pr, jnp.maximum(kb - 1, 0)])
                pv = _dot(att_buf[pr], v_ref[rows(jnp.minimum(kb + 1, last)), lanes[pr]])
                z, sp, nlb = _sb_scores(z_buf[pr, top:, :], masked)
                loc = _dot(nlb, after)
                att = jnp.exp(z - sp - loc - _per_head(c0[top:], c1[top:]))
                if masked:
                    att = _causal(att)
                z_buf[pr] = z_next
                if top:
                    att_buf[pr, :top, :] = jnp.zeros((top, PAIR), BF16)
                att_buf[pr, top:, :] = att.astype(BF16)
                add0 = loc[:, 0:1] + nlb[:, 0:1].astype(F32)
                add1 = loc[:, CHUNK:CHUNK + 1] + nlb[:, CHUNK:CHUNK + 1].astype(F32)
                if top:
                    add0 = jnp.concatenate([jnp.zeros((top, 1), F32), add0], axis=0)
                    add1 = jnp.concatenate([jnp.zeros((top, 1), F32), add1], axis=0)
                out.append((acc + pv, c0 + add0, c1 + add1))
            return tuple(out)

        for pr in range(np_):
            z_buf[pr] = _dot(qs[pr], k_ref[pr, last])
        att_buf[...] = jnp.zeros_like(att_buf)
        zero = jnp.zeros((SB_Q, 1), F32)
        carry = ((jnp.zeros((SB_Q, CHUNK), F32), zero, zero),) * np_
        for back in range(SB_Q // CHUNK):
            carry = block(last - back, carry, True, top=SB_Q - (back + 1) * CHUNK)
        carry = lax.fori_loop(0, first, lambda j, c: block(first - 1 - j, c, False), carry)
        for pr, (acc, c0, c1) in enumerate(carry):
            o_ref[:, lanes[pr]] = acc + _dot(att_buf[pr], v_ref[rows(0), lanes[pr]])
            lt_ref[:, lanes[pr]] = jnp.concatenate([jnp.broadcast_to(c0, (SB_Q, HEAD_DIM)),
                                                    jnp.broadcast_to(c1, (SB_Q, HEAD_DIM))], axis=1)

    blk = pl.BlockSpec((SB_Q, np_ * CHUNK), lambda p, i: (i, p))
    seq = pl.BlockSpec((2 * s, np_ * CHUNK), lambda p, i: (0, p))
    return pl.pallas_call(
        body, name="fwd_sb", grid=(SB_W // CHUNK // np_, s // SB_Q),
        in_specs=[blk, pl.BlockSpec((np_, s // CHUNK, CHUNK, PAIR), lambda p, i: (p, 0, 0, 0)), seq,
                  pl.BlockSpec((PAIR, PAIR), lambda p, i: (0, 0))],
        out_specs=[blk, blk],
        out_shape=[jax.ShapeDtypeStruct((s, SB_W), F32)] * 2,
        scratch_shapes=[pltpu.VMEM((np_, SB_Q, PAIR), F32), pltpu.VMEM((np_, SB_Q, PAIR), BF16)],
        compiler_params=_cp("parallel", "parallel"),
    )(qn, ktr, vst, _pair_tri("after"))


def _group_norms(yc, ys, yb):
    return [_rms_stats(yc), _rms_stats(ys), _rms_stats(yb)]


def _fwd_out(yc, ys, yb, g, w, x, tok):
    s = x.shape[0]
    tm = LIGHT_TILE

    def body(yc_ref, ys_ref, yb_ref, g_ref, w_ref, x_ref, tok_ref, o_ref):
        stats = _group_norms(yc_ref[...], ys_ref[...], yb_ref[...])
        cat = jnp.concatenate([xh for _, xh in stats], axis=1) * g_ref[...]
        o_ref[...] = x_ref[...] + _dot(cat.astype(BF16), w_ref[...])

    return pl.pallas_call(
        body, name="fwd_out", grid=(s // tm,),
        in_specs=[_rows(tm, CONV_W), _rows(tm, SG_W), _rows(tm, SB_W), _whole((1, D_MODEL)),
                  _whole((D_MODEL, D_MODEL)), _rows(tm, D_MODEL), ORDER_ONLY],
        out_specs=_rows(tm, D_MODEL),
        out_shape=jax.ShapeDtypeStruct((s, D_MODEL), F32),
        compiler_params=_cp("parallel"),
    )(yc, ys, yb, g, w, x, tok)


def _fwd_ffn(x, g, wgu, wd, tok):
    s = x.shape[0]
    tm = TOKEN_TILE

    def body(x_ref, g_ref, wgu_ref, wd_ref, tok_ref, gu_ref, o_ref):
        x = x_ref[...]
        r, xh = _rms_stats(x)
        gu = _dot_nt((xh * g_ref[...]).astype(BF16), wgu_ref[...])
        gu_ref[...] = gu
        gate = gu[:, :FFN_H]
        act = gate * _sigmoid(gate) * gu[:, FFN_H:]
        o_ref[...] = x + _dot(act.astype(BF16), wd_ref[...])

    return pl.pallas_call(
        body, name="fwd_ffn", grid=(s // tm,),
        in_specs=[_rows(tm, D_MODEL), _whole((1, D_MODEL)),
                  pl.BlockSpec((2 * FFN_H, D_MODEL), lambda i: (0, 0), pipeline_mode=pl.Buffered(1)),
                  pl.BlockSpec((FFN_H, D_MODEL), lambda i: (0, 0), pipeline_mode=pl.Buffered(1)), ORDER_ONLY],
        out_specs=[_rows(tm, 2 * FFN_H), _rows(tm, D_MODEL)],
        out_shape=[jax.ShapeDtypeStruct((s, 2 * FFN_H), F32), jax.ShapeDtypeStruct((s, D_MODEL), F32)],
        compiler_params=_cp("parallel"),
    )(x, g, wgu, wd, tok)


def _loss_head(y, target):
    s = y.shape[0]
    tm = LIGHT_TILE

    def body(y_ref, t_ref, l_ref, d_ref):
        @pl.when(pl.program_id(0) == 0)
        def _():
            l_ref[...] = jnp.zeros_like(l_ref)

        err = y_ref[...] - t_ref[...]
        d_ref[...] = err * (1.0 / D_MODEL)
        l_ref[...] += 0.5 * jnp.sum(jnp.mean(err * err, axis=-1, keepdims=True), axis=0, keepdims=True)

    return pl.pallas_call(
        body, name="loss_head", grid=(s // tm,),
        in_specs=[_rows(tm, D_MODEL), _rows(tm, D_MODEL)],
        out_specs=[_whole((1, 1)), _rows(tm, D_MODEL)],
        out_shape=[jax.ShapeDtypeStruct((1, 1), F32), jax.ShapeDtypeStruct((s, D_MODEL), F32)],
        compiler_params=_cp("arbitrary"),
    )(y, target)


def _accumulate(ref, value):
    @pl.when(pl.program_id(0) == 0)
    def _():
        ref[...] = jnp.zeros_like(ref)

    ref[...] += value


def _bwd_ffn(dxo, gu, xm, g, wgu, wd, tok):
    s = dxo.shape[0]
    tm = FFN_BWD_TILE

    def body(dxo_ref, gu_ref, xm_ref, g_ref, wgu_ref, wd_ref, tok_ref, dgu_ref, act_ref, h_ref, dxm_ref, dg_ref):
        dxo = dxo_ref[...]
        gu = gu_ref[...]
        gate, up = gu[:, :FFN_H], gu[:, FFN_H:]
        sg = _sigmoid(gate)
        sl = gate * sg
        act_ref[...] = (sl * up).astype(BF16)
        dact = _dot_nt(dxo.astype(BF16), wd_ref[...])
        dgate = dact * up * (sg * (1.0 + gate * (1.0 - sg)))
        dgu = jnp.concatenate([dgate, dact * sl], axis=1).astype(BF16)
        dgu_ref[...] = dgu
        dh = _dot(dgu, wgu_ref[...])
        r, xh = _rms_stats(xm_ref[...])
        h_ref[...] = (xh * g_ref[...]).astype(BF16)
        dx, dgrow = _rms_bwd(xh, r, g_ref[...], dh)
        dxm_ref[...] = dxo + dx
        _accumulate(dg_ref, _colsum(dgrow))

    return pl.pallas_call(
        body, name="bwd_ffn", grid=(s // tm,),
        in_specs=[_rows(tm, D_MODEL), _rows(tm, 2 * FFN_H), _rows(tm, D_MODEL), _whole((1, D_MODEL)),
                  pl.BlockSpec((2 * FFN_H, D_MODEL), lambda i: (0, 0), pipeline_mode=pl.Buffered(1)),
                  pl.BlockSpec((FFN_H, D_MODEL), lambda i: (0, 0), pipeline_mode=pl.Buffered(1)), ORDER_ONLY],
        out_specs=[_rows(tm, 2 * FFN_H), _rows(tm, FFN_H), _rows(tm, D_MODEL), _rows(tm, D_MODEL),
                   _whole((1, D_MODEL))],
        out_shape=[jax.ShapeDtypeStruct((s, 2 * FFN_H), BF16), jax.ShapeDtypeStruct((s, FFN_H), BF16),
                   jax.ShapeDtypeStruct((s, D_MODEL), BF16), jax.ShapeDtypeStruct((s, D_MODEL), F32),
                   jax.ShapeDtypeStruct((1, D_MODEL), F32)],
        compiler_params=_cp("arbitrary"),
    )(dxo, gu, xm, g, wgu, wd, tok)


def _matmul_tn(a, b, tm, tn, out_dtype=BF16):
    s, m = a.shape
    n = b.shape[1]

    def body(a_ref, b_ref, o_ref):
        o_ref[...] = _dot_tn(a_ref[...].astype(BF16), b_ref[...].astype(BF16)).astype(out_dtype)

    return pl.pallas_call(
        body, name="weight_grad", grid=(m // tm, n // tn),
        in_specs=[pl.BlockSpec((s, tm), lambda i, j: (0, i)), pl.BlockSpec((s, tn), lambda i, j: (0, j))],
        out_specs=pl.BlockSpec((tm, tn), lambda i, j: (i, j)),
        out_shape=jax.ShapeDtypeStruct((m, n), out_dtype),
        compiler_params=_cp("parallel", "parallel"),
    )(a, b)


def _bwd_out(dxm, yc, ys, yb, g, w, tok):
    s = dxm.shape[0]
    tm = LIGHT_TILE

    def body(dxm_ref, yc_ref, ys_ref, yb_ref, g_ref, w_ref, tok_ref, dyc_ref, dys_ref, dyb_ref, cat_ref, dg_ref):
        stats = _group_norms(yc_ref[...], ys_ref[...], yb_ref[...])
        g = g_ref[...]
        cat_ref[...] = (jnp.concatenate([xh for _, xh in stats], axis=1) * g).astype(BF16)
        dcat = _dot_nt(dxm_ref[...].astype(BF16), w_ref[...])
        dgs = []
        off = 0
        for (r, xh), out in zip(stats, (dyc_ref, dys_ref, dyb_ref)):
            cols = slice(off, off + xh.shape[1])
            dx, dgrow = _rms_bwd(xh, r, g[:, cols], dcat[:, cols])
            out[...] = dx
            dgs.append(_colsum(dgrow))
            off += xh.shape[1]
        _accumulate(dg_ref, jnp.concatenate(dgs, axis=1))

    return pl.pallas_call(
        body, name="bwd_out", grid=(s // tm,),
        in_specs=[_rows(tm, D_MODEL), _rows(tm, CONV_W), _rows(tm, SG_W), _rows(tm, SB_W),
                  _whole((1, D_MODEL)), _whole((D_MODEL, D_MODEL)), ORDER_ONLY],
        out_specs=[_rows(tm, CONV_W), _rows(tm, SG_W), _rows(tm, SB_W), _rows(tm, D_MODEL),
                   _whole((1, D_MODEL))],
        out_shape=[jax.ShapeDtypeStruct((s, CONV_W), F32), jax.ShapeDtypeStruct((s, SG_W), F32),
                   jax.ShapeDtypeStruct((s, SB_W), F32), jax.ShapeDtypeStruct((s, D_MODEL), BF16),
                   jax.ShapeDtypeStruct((1, D_MODEL), F32)],
        compiler_params=_cp("arbitrary"),
    )(dxm, yc, ys, yb, g, w, tok)


def _bwd_sb(qn, kst, ktr, vtr, dy, ltot, tok):
    s = qn.shape[0]
    SB_Q = min(SB_Q_BWD, s)
    np_ = SB_PAIRS

    def body(q_ref, k_ref, kt_ref, vt_ref, do_ref, lt_ref, upto_ref, before_ref, tok_ref, dq_ref, dk_ref, dv_ref,
             z_buf, da_buf, dz_buf, att_buf):
        i = pl.program_id(1)
        first = i * (SB_Q // CHUNK)
        last = first + SB_Q // CHUNK - 1

        @pl.when(i == 0)
        def _():
            dk_ref[...] = jnp.zeros_like(dk_ref)
            dv_ref[...] = jnp.zeros_like(dv_ref)

        lanes = [slice(pr * CHUNK, (pr + 1) * CHUNK) for pr in range(np_)]
        qs = [q_ref[:, lanes[pr]] for pr in range(np_)]
        dos = [do_ref[:, lanes[pr]] for pr in range(np_)]
        dobs = [do.astype(BF16) for do in dos]
        q_ts = [q.astype(F32).T.astype(BF16) for q in qs]
        do_ts = [do.T.astype(BF16) for do in dos]
        ltots = [_per_head(lt_ref[:, pr * CHUNK:pr * CHUNK + 1],
                           lt_ref[:, pr * CHUNK + HEAD_DIM:pr * CHUNK + HEAD_DIM + 1]) for pr in range(np_)]
        upto = upto_ref[...]
        before = before_ref[...]
        last0, last1 = slice(CHUNK - 1, CHUNK), slice(PAIR - 1, PAIR)

        def rows(kb):
            return pl.ds(pl.multiple_of(kb * PAIR, PAIR), PAIR)

        def ahead(pr, kb):
            return _dot(qs[pr], kt_ref[pr, kb]), _dot(dobs[pr], vt_ref[pr, kb])

        def behind(pr, kb, dq):
            dzb = dz_buf[pr]
            dk_ref[pr, kb] += _dot(q_ts[pr], dzb)
            dv_ref[pr, kb] += _dot(do_ts[pr], att_buf[pr])
            return dq + _dot(dzb, k_ref[rows(kb), lanes[pr]])

        def block(kb, carry, masked, top=0):
            out = []
            for pr in range(np_):
                dq, p0, p1, e0, e1 = carry[pr]
                z_next, da_next = ahead(pr, jnp.minimum(kb + 1, last))
                dq = behind(pr, jnp.maximum(kb - 1, 0), dq)
                if top:
                    dz_buf[pr, :top, :] = jnp.zeros((top, PAIR), BF16)
                    att_buf[pr, :top, :] = jnp.zeros((top, PAIR), BF16)
                pieces = [(p0[:top], p1[:top], e0[:top], e1[:top])] if top else []
                for r in range(top, SB_Q, SB_BAND):
                    band = slice(r, min(r + SB_BAND, SB_Q))
                    diag = masked and r == top
                    z, sp, nlb = _sb_scores(z_buf[pr, band, :], diag)
                    pin = _dot(nlb, upto) + _per_head(p0[band], p1[band])
                    sig = jnp.exp(z - sp)
                    att = jnp.exp(z - sp - (ltots[pr][band] - pin))
                    if diag:
                        att = _causal(att)
                    e = att * da_buf[pr, band, :]
                    ebefore = _dot(e.astype(BF16), before) + _per_head(e0[band], e1[band])
                    dz = e - sig * (e + ebefore)
                    if diag:
                        dz = _causal(dz)
                    dz_buf[pr, band, :] = dz.astype(BF16)
                    att_buf[pr, band, :] = att.astype(BF16)
                    pieces.append((pin[:, last0], pin[:, last1],
                                   ebefore[:, last0] + e[:, last0], ebefore[:, last1] + e[:, last1]))
                z_buf[pr] = z_next
                da_buf[pr] = da_next
                new = tuple(jnp.concatenate(cols, axis=0) if len(cols) > 1 else cols[0] for cols in zip(*pieces))
                out.append((dq,) + new)
            return tuple(out)

        for pr in range(np_):
            z_buf[pr], da_buf[pr] = ahead(pr, 0)
        dz_buf[...] = jnp.zeros_like(dz_buf)
        att_buf[...] = jnp.zeros_like(att_buf)
        zero = jnp.zeros((SB_Q, 1), F32)
        carry = ((jnp.zeros((SB_Q, CHUNK), F32), zero, zero, zero, zero),) * np_
        carry = lax.fori_loop(0, first, lambda kb, c: block(kb, c, False), carry)
        for ahead_of in range(SB_Q // CHUNK):
            carry = block(first + ahead_of, carry, True, top=ahead_of * CHUNK)
        for pr in range(np_):
            dq_ref[:, lanes[pr]] = behind(pr, last, carry[pr][0])

    blk = pl.BlockSpec((SB_Q, np_ * CHUNK), lambda p, i: (i, p))
    seq = pl.BlockSpec((2 * s, np_ * CHUNK), lambda p, i: (0, p))
    tiles = pl.BlockSpec((np_, s // CHUNK, CHUNK, PAIR), lambda p, i: (p, 0, 0, 0))
    tri = pl.BlockSpec((PAIR, PAIR), lambda p, i: (0, 0))
    return pl.pallas_call(
        body, name="bwd_sb", grid=(SB_W // CHUNK // np_, s // SB_Q),
        in_specs=[blk, seq, tiles, tiles, blk, blk, tri, tri, ORDER_ONLY],
        out_specs=[blk, tiles, tiles],
        out_shape=[jax.ShapeDtypeStruct((s, SB_W), F32)]
        + [jax.ShapeDtypeStruct((SB_W // CHUNK, s // CHUNK, CHUNK, PAIR), F32)] * 2,
        scratch_shapes=[pltpu.VMEM((np_, SB_Q, PAIR), F32), pltpu.VMEM((np_, SB_Q, PAIR), F32),
                        pltpu.VMEM((np_, SB_Q, PAIR), BF16), pltpu.VMEM((np_, SB_Q, PAIR), BF16)],
        compiler_params=_cp("parallel", "arbitrary"),
    )(qn, kst, ktr, vtr, dy, ltot, _pair_tri("upto"), _pair_tri("before"), tok)


def _head_sum(row):
    acc = row[:, 0:HEAD_DIM]
    for h in range(1, SB_W // HEAD_DIM):
        acc = acc + row[:, h * HEAD_DIM:(h + 1) * HEAD_DIM]
    return acc


def _bwd_qk(proj, dqs, dkn, dv, qg, kg):
    s = proj.shape[0]
    tm = LIGHT_TILE
    tiles = pl.BlockSpec((SB_W // CHUNK, tm // CHUNK, CHUNK, PAIR), lambda i: (0, i, 0, 0))

    def body(q_ref, k_ref, dqs_ref, dkn_ref, dv_ref, qg_ref, kg_ref, dp_ref, dqg_ref, dkg_ref, qacc, kacc):
        i = pl.program_id(0)
        gm = _group_mean_matrix(SB_W, HEAD_DIM)

        def one(x, dy, g, acc):
            r = lax.rsqrt(_dot_split(x * x, gm) + RMS_EPS)
            xh = x * r
            dxh = dy * g
            _accumulate(acc, _colsum(dy * xh))
            return r * (dxh - xh * _dot_split(dxh * xh, gm))

        dq = one(q_ref[...], dqs_ref[...] * (HEAD_DIM ** -0.5), qg_ref[...], qacc)
        dk = one(k_ref[...], _unstack_heads(_load_transposed(dkn_ref)), kg_ref[...], kacc)
        dp_ref[...] = jnp.concatenate([dq, dk, _unstack_heads(_load_transposed(dv_ref))], axis=1).astype(BF16)

        @pl.when(i == pl.num_programs(0) - 1)
        def _():
            dqg_ref[...] = _head_sum(qacc[...])
            dkg_ref[...] = _head_sum(kacc[...])

    return pl.pallas_call(
        body, name="bwd_qk", grid=(s // tm,),
        in_specs=[_rows(tm, SB_W, OFF_SB // SB_W), _rows(tm, SB_W, OFF_SB // SB_W + 1),
                  _rows(tm, SB_W), tiles, tiles, _whole((1, SB_W)), _whole((1, SB_W))],
        out_specs=[_rows(tm, 3 * SB_W), _whole((1, HEAD_DIM)), _whole((1, HEAD_DIM))],
        out_shape=[jax.ShapeDtypeStruct((s, 3 * SB_W), BF16), jax.ShapeDtypeStruct((1, HEAD_DIM), F32),
                   jax.ShapeDtypeStruct((1, HEAD_DIM), F32)],
        scratch_shapes=[pltpu.VMEM((1, SB_W), F32), pltpu.VMEM((1, SB_W), F32)],
        compiler_params=_cp("arbitrary"),
    )(proj, proj, dqs, dkn, dv, qg, kg)


def _bwd_sg(proj, dy, lg, lb, w, bias):
    s = proj.shape[0]
    tm = LIGHT_TILE
    nh = SG_W // HEAD_DIM

    def body(p_ref, dy_ref, lg_ref, lb_ref, w_ref, bias_ref, dp_ref, dlg_ref, dlb_ref, dw_ref, db_ref, dbias):
        i = pl.program_id(0)
        uv = p_ref[...]
        ge = _gelu(uv)
        u = ge[:, :SG_W]
        r, xh = _ln_stats(ge[:, SG_W:])
        vln = (xh * lg_ref[...] + lb_ref[...]).astype(BF16)
        dy = dy_ref[...]
        tril, lane_head = _sg_masks()

        @pl.when(i == 0)
        def _():
            dw_ref[...] = jnp.zeros_like(dw_ref)
            dbias[...] = jnp.zeros_like(dbias)

        dus, dvlns = [], []
        for c in range(tm // CHUNK):
            rows = slice(c * CHUNK, (c + 1) * CHUNK)
            vc = vln[rows]
            dus.append(dy[rows] * _sg_mix(w_ref, bias_ref, vc, tril, lane_head))
            dm = dy[rows] * u[rows]
            dbias[...] += dm
            dvc = jnp.zeros((CHUNK, SG_W), F32)
            for h in range(nh):
                dmh = jnp.where(lane_head == h, dm, 0.0).astype(BF16)
                dw_ref[h] += jnp.where(tril, _dot_nt(dmh, vc), 0.0)
                wm = jnp.where(tril, w_ref[h], 0.0).astype(BF16)
                dvc = dvc + _dot_tn(wm, dmh)
            dvlns.append(dvc)
        du = jnp.concatenate(dus, axis=0)
        dvln = jnp.concatenate(dvlns, axis=0)
        _accumulate(dlg_ref, _colsum(dvln * xh))
        _accumulate(dlb_ref, _colsum(dvln))
        dv = _ln_bwd(xh, r, lg_ref[...], dvln)
        dp_ref[...] = (jnp.concatenate([du, dv], axis=1) * _gelu_grad(uv)).astype(BF16)

        @pl.when(i == pl.num_programs(0) - 1)
        def _():
            lane = lax.broadcasted_iota(jnp.int32, (CHUNK, CHUNK), 1)
            acc = dbias[...]
            out = jnp.zeros((CHUNK, CHUNK), F32)
            for h in range(nh):
                hs = jnp.sum(acc[:, h * HEAD_DIM:(h + 1) * HEAD_DIM], axis=1, keepdims=True)
                out = out + jnp.where(lane == h, hs, 0.0)
            db_ref[...] = out

    return pl.pallas_call(
        body, name="bwd_sg", grid=(s // tm,),
        in_specs=[_rows(tm, 2 * SG_W, 1), _rows(tm, SG_W), _whole((1, SG_W)), _whole((1, SG_W)),
                  _whole((nh, CHUNK, CHUNK)), _whole((CHUNK, SG_W))],
        out_specs=[_rows(tm, 2 * SG_W), _whole((1, SG_W)), _whole((1, SG_W)), _whole((nh, CHUNK, CHUNK)),
                   _whole((CHUNK, CHUNK))],
        out_shape=[jax.ShapeDtypeStruct((s, 2 * SG_W), BF16), jax.ShapeDtypeStruct((1, SG_W), F32),
                   jax.ShapeDtypeStruct((1, SG_W), F32), jax.ShapeDtypeStruct((nh, CHUNK, CHUNK), F32),
                   jax.ShapeDtypeStruct((CHUNK, CHUNK), F32)],
        scratch_shapes=[pltpu.VMEM((CHUNK, SG_W), F32)],
        compiler_params=_cp("arbitrary"),
    )(proj, dy, lg, lb, w, bias)


def _bwd_conv(proj, dy, w, b, lg, lb):
    s = proj.shape[0]

    def body(p_ref, dy_ref, w_ref, b_ref, lg_ref, lb_ref, dp_ref, dw_ref, db_ref, dlg_ref, dlb_ref,
             abuf, dcbuf):
        _glu_fill(p_ref, abuf, s)
        dcbuf[pl.ds(s, CHUNK), :] = jnp.zeros((CHUNK, CONV_W), F32)
        dw_ref[...] = jnp.zeros_like(dw_ref)

        def chunk(c, carry):
            db, dlg, dlb = carry
            r0 = pl.multiple_of(c * CHUNK, CHUNK)
            acc, shifted = _conv_window(abuf, r0, w_ref)
            r, xh = _ln_stats(acc + b_ref[...])
            ln = xh * lg_ref[...] + lb_ref[...]
            sg = _sigmoid(ln)
            dl = dy_ref[pl.ds(r0, CHUNK), :] * (sg * (1.0 + ln * (1.0 - sg)))
            dc = _ln_bwd(xh, r, lg_ref[...], dl)
            dcbuf[pl.ds(r0, CHUNK), :] = dc
            for k in range(CONV_K):
                dw_ref[CONV_K - 1 - k:CONV_K - k, :] += _colsum(dc * shifted(k))
            return db + _colsum(dc), dlg + _colsum(dl * xh), dlb + _colsum(dl)

        zero = jnp.zeros((1, CONV_W), F32)
        db, dlg, dlb = lax.fori_loop(0, s // CHUNK, chunk, (zero, zero, zero))
        db_ref[...] = db
        dlg_ref[...] = dlg
        dlb_ref[...] = dlb

        def chunk_back(c, carry):
            r0 = pl.multiple_of(c * CHUNK, CHUNK)
            shifted = _shifted(dcbuf[pl.ds(r0, CHUNK + 32), :], back=False)
            da = jnp.zeros((CHUNK, CONV_W), F32)
            for k in range(CONV_K):
                da = da + shifted(k) * w_ref[CONV_K - 1 - k:CONV_K - k, :]
            pv = p_ref[pl.ds(r0, CHUNK), :]
            val, sg = pv[:, :CONV_W], _sigmoid(pv[:, CONV_W:])
            dp_ref[pl.ds(r0, CHUNK), :] = jnp.concatenate([da * sg, da * val * sg * (1.0 - sg)], axis=1).astype(BF16)
            return carry

        lax.fori_loop(0, s // CHUNK, chunk_back, 0)

    row = _whole((1, CONV_W))
    return pl.pallas_call(
        body, name="bwd_conv", grid=(1,),
        in_specs=[pl.BlockSpec((s, 2 * CONV_W), lambda i: (0, 0)), _whole((s, CONV_W)),
                  _whole((CONV_K, CONV_W)), row, row, row],
        out_specs=[_whole((s, 2 * CONV_W)), _whole((CONV_K, CONV_W)), row, row, row],
        out_shape=[jax.ShapeDtypeStruct((s, 2 * CONV_W), BF16), jax.ShapeDtypeStruct((CONV_K, CONV_W), F32)]
        + [jax.ShapeDtypeStruct((1, CONV_W), F32)] * 3,
        scratch_shapes=[pltpu.VMEM((s + CHUNK, CONV_W), F32), pltpu.VMEM((s + CHUNK, CONV_W), F32)],
        compiler_params=_cp("arbitrary"),
    )(proj, dy, w, b, lg, lb)


def _bwd_in(dpc, dps, dpb, x, g, w, dxm):
    s = x.shape[0]
    tm = TOKEN_TILE

    def body(dpc_ref, dps_ref, dpb_ref, x_ref, g_ref, w_ref, dxm_ref, dx_ref, h_ref, dp_ref, dg_ref):
        dp = jnp.concatenate([dpc_ref[...], dps_ref[...], dpb_ref[...]], axis=1)
        dp_ref[...] = dp
        dh = _dot(dp, w_ref[...])
        r, xh = _rms_stats(x_ref[...])
        h_ref[...] = (xh * g_ref[...]).astype(BF16)
        dx, dgrow = _rms_bwd(xh, r, g_ref[...], dh)
        dx_ref[...] = dxm_ref[...] + dx
        _accumulate(dg_ref, _colsum(dgrow))

    return pl.pallas_call(
        body, name="bwd_in", grid=(s // tm,),
        in_specs=[_rows(tm, 2 * CONV_W), _rows(tm, 2 * SG_W), _rows(tm, 3 * SB_W), _rows(tm, D_MODEL),
                  _whole((1, D_MODEL)), _whole((IN_W, D_MODEL)), _rows(tm, D_MODEL)],
        out_specs=[_rows(tm, D_MODEL), _rows(tm, D_MODEL), _rows(tm, IN_W), _whole((1, D_MODEL))],
        out_shape=[jax.ShapeDtypeStruct((s, D_MODEL), F32), jax.ShapeDtypeStruct((s, D_MODEL), BF16),
                   jax.ShapeDtypeStruct((s, IN_W), BF16), jax.ShapeDtypeStruct((1, D_MODEL), F32)],
        compiler_params=_cp("arbitrary"),
    )(dpc, dps, dpb, x, g, w, dxm)


SMALL = ("mix_norm_g", "conv_w", "conv_b", "conv_ln_g", "conv_ln_b", "sg_ln_g", "sg_ln_b", "sg_w", "sg_b",
         "q_norm_g", "k_norm_g", "out_norm_g", "ffn_norm_g")
LARGE = ("w_in", "w_out", "w_gate_up", "w_down")


def _row(v):
    return v.reshape(1, -1)


def _layer_params(p, large, l):
    q = {k: v[l] for k, v in p.items()}
    return dict(
        q, **large,
        mix_norm_g=_row(q["mix_norm_g"]), conv_b=_row(q["conv_b"]), conv_ln_g=_row(q["conv_ln_g"]),
        conv_ln_b=_row(q["conv_ln_b"]), sg_ln_g=_row(q["sg_ln_g"]), sg_ln_b=_row(q["sg_ln_b"]),
        out_norm_g=_row(q["out_norm_g"]), ffn_norm_g=_row(q["ffn_norm_g"]),
        qg=_row(jnp.tile(q["q_norm_g"], SB_W // HEAD_DIM)), kg=_row(jnp.tile(q["k_norm_g"], SB_W // HEAD_DIM)),
        sg_bias=jnp.repeat(q["sg_b"].T, HEAD_DIM, axis=1),
    )


def _layer_fwd(x, q, tok, after_in, after_mixers, after_out):
    proj, qn, kn, vb, kt, vt = _fwd_in(x, q["mix_norm_g"], q["w_in"], q["qg"], q["kg"], tok)
    yc = _fwd_conv(proj, q["conv_w"], q["conv_b"], q["conv_ln_g"], q["conv_ln_b"], after_in(proj))
    ys = _fwd_sg(proj, q["sg_ln_g"], q["sg_ln_b"], q["sg_w"], q["sg_bias"])
    yb, lt = _fwd_sb(qn, kt, vb)
    rest, tok = after_mixers(yb)
    q = dict(q, **rest)
    xm = _fwd_out(yc, ys, yb, q["out_norm_g"], q["w_out"], x, tok)
    gu, xo = _fwd_ffn(xm, q["ffn_norm_g"], q["w_gate_up"], q["w_down"], after_out(xm))
    return xo, q, dict(x=x, proj=proj, qn=qn, kn=kn, kt=kt, vt=vt, lt=lt, yc=yc, ys=ys, yb=yb, xm=xm, gu=gu)


def _layer_bwd_ffn(dxo, q, st, tok):
    dgu, act, h2, dxm, d_ffn_g = _bwd_ffn(dxo, st["gu"], st["xm"], q["ffn_norm_g"], q["w_gate_up"], q["w_down"],
                                          tok)
    d_wgu, d_wd = _matmul_tn(dgu, h2, 512, D_MODEL), _matmul_tn(act, dxo, FFN_H // 2, D_MODEL)
    dyc, dys, dyb, cat, d_out_g = _bwd_out(dxm, st["yc"], st["ys"], st["yb"], q["out_norm_g"], q["w_out"], tok)
    return dxm, (dyc, dys, dyb, d_ffn_g, d_out_g), d_wgu, d_wd, _matmul_tn(cat, dxm, 512, D_MODEL)


def _layer_bwd_mix(dxm, carried, q, st, tok):
    dyc, dys, dyb, d_ffn_g, d_out_g = carried
    dqs, dkn, dv = _bwd_sb(st["qn"], st["kn"], st["kt"], st["vt"], dyb, st["lt"], tok)
    dpb, d_qg, d_kg = _bwd_qk(st["proj"], dqs, dkn, dv, q["qg"], q["kg"])
    dps, d_sg_lg, d_sg_lb, d_sg_w, d_sg_b = _bwd_sg(st["proj"], dys, q["sg_ln_g"], q["sg_ln_b"], q["sg_w"],
                                                    q["sg_bias"])
    dpc, d_conv_w, d_conv_b, d_conv_lg, d_conv_lb = _bwd_conv(st["proj"], dyc, q["conv_w"], q["conv_b"],
                                                              q["conv_ln_g"], q["conv_ln_b"])
    dx, h1, dp, d_mix_g = _bwd_in(dpc, dps, dpb, st["x"], q["mix_norm_g"], q["w_in"], dxm)
    d_win = _matmul_tn(dp, h1, 512, D_MODEL)
    small = dict(
        mix_norm_g=d_mix_g[0], conv_w=d_conv_w, conv_b=d_conv_b[0], conv_ln_g=d_conv_lg[0],
        conv_ln_b=d_conv_lb[0], sg_ln_g=d_sg_lg[0], sg_ln_b=d_sg_lb[0], sg_w=d_sg_w,
        sg_b=d_sg_b[:, :SG_W // HEAD_DIM].T, q_norm_g=d_qg[0], k_norm_g=d_kg[0], out_norm_g=d_out_g[0],
        ffn_norm_g=d_ffn_g[0])
    return dx, d_win, small


def _position():
    x, y, c = lax.axis_index("x"), lax.axis_index("y"), lax.axis_index("c")
    return x, y, c


def _flat(px, py, pc):
    return 4 * px + 2 * py + pc


IN_HBM = pl.BlockSpec(memory_space=pltpu.HBM)
IN_SEM = pl.BlockSpec(memory_space=pltpu.SEMAPHORE)
EFFECT = pltpu.SideEffectType.DATAFLOW_SIDE_EFFECTING
COPIES = dict(scatter=7, spread=7, spread_chips=4, **{"pass": 3})


def _exchange_copies(kinds, src_refs, land_refs, send_sems, recv_sems, layer, arrival):
    x, y, c = _position()
    me = _flat(x, y, c)
    everyone = [(x ^ (k >> 2 & 1), y ^ (k >> 1 & 1), c ^ (k & 1)) for k in range(1, N_DEV)]
    sibling = (x, y, 1 - c)
    chips = [(1 - x, y, c), (x, 1 - y, c), (1 - x, 1 - y, c)]
    out = []
    srcs = iter(src_refs)
    for kind, land in zip(kinds, land_refs):
        land = land if layer is None else land.at[layer]
        if kind == "scatter":
            src = next(srcs)
            moves = [(src.at[_flat(*p)], me, _flat(*p), p) for p in everyone]
        elif kind in ("spread", "spread_chips"):
            src = next(srcs)
            moves = [(src, me, _flat(*p), p) for p in (everyone if kind == "spread" else [sibling] + chips)]
        else:
            moves = [(land.at[_flat(*p)], _flat(*p), _flat(p[0], p[1], 1 - c), sibling) for p in chips]
        for src_block, there, here, peer in moves:
            n = len(out)
            out.append(pltpu.make_async_remote_copy(
                src_ref=src_block, dst_ref=land.at[here if arrival else there], send_sem=send_sems.at[n],
                recv_sem=recv_sems.at[n], device_id=peer, device_id_type=MESH))
    return out


def _exchange_start(kinds, srcs, lands, after, name, layer=None):
    ns, n = len(srcs), len(srcs) + len(lands)
    sems = sum(COPIES[k] for k in kinds)

    def body(*refs):
        send_sems, recv_sems = refs[n + 1], refs[n + 2]
        for cp in _exchange_copies(kinds, refs[:ns], refs[ns:n], send_sems, recv_sems, layer, arrival=False):
            cp.start()
        refs[-1][...] = jnp.zeros_like(refs[-1])

    thru = [pltpu.HBM(a.shape, a.dtype) for a in (*srcs, *lands)]
    outs = pl.pallas_call(
        body, name=name,
        out_shape=(pltpu.SemaphoreType.DMA((sems,)), pltpu.SemaphoreType.DMA((sems,)), *thru,
                   jax.ShapeDtypeStruct((8, 128), F32)),
        in_specs=[IN_HBM] * n + [ORDER_ONLY],
        out_specs=(IN_SEM, IN_SEM, *[IN_HBM] * n, pl.BlockSpec(memory_space=pltpu.VMEM)),
        input_output_aliases={i: 2 + i for i in range(n)},
        compiler_params=pltpu.CompilerParams(has_side_effects=EFFECT),
    )(*[pltpu.with_memory_space_constraint(a, pltpu.HBM) for a in (*srcs, *lands)], after)
    return kinds, outs[0], outs[1], list(outs[2:2 + ns]), list(outs[2 + ns:2 + n]), outs[-1]


def _exchange_wait(pending, after, name, layer=None):
    kinds, send_sems, recv_sems, srcs, lands, _ = pending
    after = list(after) if isinstance(after, (list, tuple)) else [after]
    ns, n = len(srcs), len(srcs) + len(lands)

    def body(*refs):
        for cp in _exchange_copies(kinds, refs[:ns], refs[ns:n], refs[n], refs[n + 1], layer, arrival=True):
            cp.wait_send()
            cp.wait_recv()

    thru = [pltpu.HBM(a.shape, a.dtype) for a in (*srcs, *lands)]
    outs = pl.pallas_call(
        body, name=name, out_shape=tuple(thru),
        in_specs=[IN_HBM] * n + [IN_SEM, IN_SEM] + [ORDER_ONLY] * len(after),
        out_specs=tuple([IN_HBM] * n),
        input_output_aliases={i: i for i in range(n)},
        compiler_params=pltpu.CompilerParams(has_side_effects=EFFECT),
    )(*srcs, *lands, send_sems, recv_sems, *after)
    return list(outs[ns:])


def _landing(block, me):
    land = lax.empty((N_DEV,) + block.shape, block.dtype)
    return lax.dynamic_update_index_in_dim(land, block, me, 0)


def _adamw(parts, w, m, v, tr):
    groups, rows, cols = w.shape

    def body(p_ref, w_ref, m_ref, v_ref, g_ref, d_ref, nm_ref, nv_ref):
        g = p_ref[0].astype(F32)
        for j in range(1, N_DEV):
            g = g + p_ref[j].astype(F32)
        g_ref[...] = g
        m = ADAM_B1 * m_ref[...] + (1.0 - ADAM_B1) * g
        v = ADAM_B2 * v_ref[...] + (1.0 - ADAM_B2) * (g * g)
        nm_ref[...] = m
        nv_ref[...] = v
        m_hat = m / (1.0 - ADAM_B1 ** ADAM_STEP)
        v_hat = v / (1.0 - ADAM_B2 ** ADAM_STEP)
        d_ref[...] = -ADAM_LR * (m_hat / (jnp.sqrt(v_hat) + ADAM_EPS) + ADAM_WD * w_ref[...])

    blk = pl.BlockSpec((None, tr, cols), lambda g, i: (g, i, 0))
    return pl.pallas_call(
        body, name="adamw", grid=(groups, rows // tr),
        in_specs=[pl.BlockSpec((None, N_DEV, tr, cols), lambda g, i: (g, 0, i, 0)), blk, blk, blk],
        out_specs=[blk] * 4,
        out_shape=[jax.ShapeDtypeStruct((groups, rows, cols), F32)] * 4,
        compiler_params=_cp("parallel", "parallel"),
    )(parts, w, m, v)


def _row_tile(rows):
    for cand in range(min(rows, 512) // 8 * 8, 7, -8):
        if rows % cand == 0:
            return cand
    return rows


def _with_own_block(land, blocks, layer, me):
    own = lax.dynamic_index_in_dim(blocks, me, 0, keepdims=True)[None]
    return lax.dynamic_update_slice(land, own, (layer, me, 0, 0))


PACK_LANES = 128


def _pack_layers(arrs):
    parts = []
    for a in arrs:
        flat = a.reshape(a.shape[0], -1)
        parts.append(jnp.pad(flat, ((0, 0), (0, -flat.shape[1] % (8 * PACK_LANES)))))
    return jnp.concatenate(parts, axis=1).reshape(arrs[0].shape[0], -1, PACK_LANES)


def _unpack_layers(packed, shapes):
    flat = packed.reshape(packed.shape[0], -1)
    outs, off = [], 0
    for shp in shapes:
        size = 1
        for d in shp[1:]:
            size *= d
        outs.append(flat[:, off:off + size].reshape(shp))
        off += size + (-size % (8 * PACK_LANES))
    return outs


def kernel(x, mix_norm_g, w_in, conv_w, conv_b, conv_ln_g, conv_ln_b, sg_ln_g, sg_ln_b, sg_w, sg_b, q_norm_g, k_norm_g, out_norm_g, w_out, ffn_norm_g, w_gate_up, w_down, loss_target, m_mix_norm_g, m_w_in, m_conv_w, m_conv_b, m_conv_ln_g, m_conv_ln_b, m_sg_ln_g, m_sg_ln_b, m_sg_w, m_sg_b, m_q_norm_g, m_k_norm_g, m_out_norm_g, m_w_out, m_ffn_norm_g, m_w_gate_up, m_w_down, v_mix_norm_g, v_w_in, v_conv_w, v_conv_b, v_conv_ln_g, v_conv_ln_b, v_sg_ln_g, v_sg_ln_b, v_sg_w, v_sg_b, v_q_norm_g, v_k_norm_g, v_out_norm_g, v_w_out, v_ffn_norm_g, v_w_gate_up, v_w_down):
    names = SMALL[:1] + LARGE[:1] + SMALL[1:12] + LARGE[1:2] + SMALL[12:] + LARGE[2:]
    w = dict(mix_norm_g=mix_norm_g, w_in=w_in, conv_w=conv_w, conv_b=conv_b, conv_ln_g=conv_ln_g,
             conv_ln_b=conv_ln_b, sg_ln_g=sg_ln_g, sg_ln_b=sg_ln_b, sg_w=sg_w, sg_b=sg_b, q_norm_g=q_norm_g,
             k_norm_g=k_norm_g, out_norm_g=out_norm_g, w_out=w_out, ffn_norm_g=ffn_norm_g,
             w_gate_up=w_gate_up, w_down=w_down)
    m = dict(mix_norm_g=m_mix_norm_g, w_in=m_w_in, conv_w=m_conv_w, conv_b=m_conv_b, conv_ln_g=m_conv_ln_g,
             conv_ln_b=m_conv_ln_b, sg_ln_g=m_sg_ln_g, sg_ln_b=m_sg_ln_b, sg_w=m_sg_w, sg_b=m_sg_b,
             q_norm_g=m_q_norm_g, k_norm_g=m_k_norm_g, out_norm_g=m_out_norm_g, w_out=m_w_out,
             ffn_norm_g=m_ffn_norm_g, w_gate_up=m_w_gate_up, w_down=m_w_down)
    v = dict(mix_norm_g=v_mix_norm_g, w_in=v_w_in, conv_w=v_conv_w, conv_b=v_conv_b, conv_ln_g=v_conv_ln_g,
             conv_ln_b=v_conv_ln_b, sg_ln_g=v_sg_ln_g, sg_ln_b=v_sg_ln_b, sg_w=v_sg_w, sg_b=v_sg_b,
             q_norm_g=v_q_norm_g, k_norm_g=v_k_norm_g, out_norm_g=v_out_norm_g, w_out=v_w_out,
             ffn_norm_g=v_ffn_norm_g, w_gate_up=v_w_gate_up, w_down=v_w_down)
    xpos, ypos, cpos = _position()
    me = _flat(xpos, ypos, cpos)
    conv_cols = conv_w.shape[-1]
    no_token = jnp.zeros((8, 128), F32)
    w, m, v = (dict(t, w_in=jnp.swapaxes(t["w_in"], 1, 2), w_gate_up=jnp.swapaxes(t["w_gate_up"], 1, 2))
               for t in (w, m, v))
    shards = {k: w[k].astype(BF16) for k in LARGE}
    full_shape = dict(w_in=(IN_W, D_MODEL), w_out=(D_MODEL, D_MODEL), w_gate_up=(2 * FFN_H, D_MODEL),
                      w_down=(FFN_H, D_MODEL))

    def gather_start(srcs, after, tag):
        return _exchange_start(["spread_chips"] * len(srcs), srcs, [_landing(a, me) for a in srcs], after,
                               f"gather_start_{tag}")

    def gather_pass(pending, after, tag):
        lands = _exchange_wait(pending, after, f"gather_wait_{tag}")
        return _exchange_start(["pass"] * len(lands), [], lands, after, f"gather_pass_{tag}")

    def gathered(pending, keys, after, tag):
        lands = _exchange_wait(pending, after, f"gather_passed_{tag}")
        return {k: a.reshape(full_shape[k]) for k, a in zip(keys, lands)}, lands[len(keys):]

    first, later = ("w_in",), ("w_out", "w_gate_up", "w_down")
    act = x[0]
    head = gather_start([shards["w_in"][0], w["conv_w"]], act, "0")
    tail = gather_start([shards[k][0] for k in later], head[5], "0_later")
    head = gather_pass(head, tail[5], "0")
    large, (conv_blocks,) = gathered(head, first, head[5], "0")
    conv_full = jnp.transpose(conv_blocks, (1, 2, 0, 3)).reshape(DEPTH, CONV_K, CONV_W)
    small_w = dict({k: w[k] for k in SMALL}, conv_w=conv_full)
    qs, stash = [], []
    for l in range(DEPTH):
        coming = {}
        more = l + 1 < DEPTH

        def next_start(after):
            coming["first"] = gather_start([shards[k][l + 1] for k in LARGE], after, str(l + 1))
            return coming["first"][5]

        def after_in(proj):
            return next_start(proj) if more and l == 0 else no_token

        def after_mixers(y_sb):
            if l > 0:
                return {}, no_token
            passing = gather_pass(tail, y_sb, "0_later")
            return gathered(passing, later, passing[5], "0_later")[0], passing[5]

        def after_out(x_mid):
            if not more or l == 0:
                return no_token
            coming["second"] = gather_pass(coming["first"], x_mid, str(l + 1))
            return coming["second"][5]

        token = next_start(act) if more and l > 0 else no_token
        act, q, st = _layer_fwd(act, _layer_params(small_w, large, l), token, after_in, after_mixers, after_out)
        qs.append(q)
        stash.append(st)
        if more:
            if l == 0:
                coming["second"] = gather_pass(coming["first"], act, str(l + 1))
            large, _ = gathered(coming["second"], LARGE, act, str(l + 1))

    loss, dx = _loss_head(act, loss_target[0])
    loss = lax.psum(loss[0, 0], ("x", "y", "c"))

    replicated = tuple(k for k in SMALL if k != "conv_w")
    small_rows = _pack_layers([w[k][:1] for k in replicated]).shape[1]
    conv_rows = _pack_layers([conv_full[:1]]).shape[1]
    group_a, group_b = ("w_gate_up", "w_down", "w_out"), ("w_in",)
    blocks = lambda k, a: a.reshape((N_DEV,) + w[k].shape[1:])
    land_a = [lax.empty((DEPTH, N_DEV) + w[k].shape[1:], BF16) for k in group_a]
    land_b = [lax.empty((DEPTH, N_DEV) + w[k].shape[1:], BF16) for k in group_b]
    land_b.append(lax.empty((DEPTH, N_DEV, small_rows + conv_rows, PACK_LANES), F32))
    pend_a = pend_b = None
    token = no_token
    for l in reversed(range(DEPTH)):
        dxm, carried, d_wgu, d_wd, d_wo = _layer_bwd_ffn(dx, qs[l], stash[l], token)
        srcs = [blocks(k, a) for k, a in zip(group_a, (d_wgu, d_wd, d_wo))]
        if pend_a is not None:
            land_a = _exchange_wait(pend_a, d_wo, f"grads_a_wait_{l + 1}", layer=l + 1)
        land_a = [_with_own_block(ld, a, l, me) for ld, a in zip(land_a, srcs)]
        pend_a = _exchange_start(["scatter"] * 3, srcs, land_a, dxm, f"grads_a_start_{l}", layer=l)
        dx, d_win, small = _layer_bwd_mix(dxm, carried, qs[l], stash[l], pend_a[5])
        packed = _pack_layers([small[k][None] for k in replicated + ("conv_w",)])[0]
        srcs = [blocks("w_in", d_win)]
        if pend_b is not None:
            land_b = _exchange_wait(pend_b, d_win, f"grads_b_wait_{l + 1}", layer=l + 1)
        land_b = [_with_own_block(land_b[0], srcs[0], l, me),
                  lax.dynamic_update_slice(land_b[1], packed[None, None], (l, me, 0, 0))]
        pend_b = _exchange_start(["scatter", "spread"], srcs + [packed], land_b, dx, f"grads_b_start_{l}",
                                 layer=l)
        token = pend_b[5]

    land_a = _exchange_wait(pend_a, token, "grads_a_wait_0", layer=0)
    res = {}
    for k, parts in zip(group_a, land_a):
        res[k] = _adamw(parts, w[k], m[k], v[k], _row_tile(w[k].shape[1]))
    land_b = _exchange_wait(pend_b, [res[k][0] for k in group_a], "grads_b_wait_0", layer=0)
    res["w_in"] = _adamw(land_b[0], w["w_in"], m["w_in"], v["w_in"], _row_tile(w["w_in"].shape[1]))
    small_parts = land_b[1]
    updated = _adamw(small_parts, *(_pack_layers([t[k] for k in replicated]) for t in (w, m, v)), small_rows)
    unpacked = [_unpack_layers(o, [w[k].shape for k in replicated]) for o in updated]
    res.update({k: [u[i] for u in unpacked] for i, k in enumerate(replicated)})
    conv_parts = small_parts[:, :, small_rows:].reshape(DEPTH, N_DEV, -1)[:, :, :CONV_K * CONV_W]
    conv_parts = lax.dynamic_slice_in_dim(conv_parts.reshape(DEPTH, N_DEV, CONV_K, CONV_W), me * conv_cols,
                                          conv_cols, axis=3)
    res["conv_w"] = _adamw(conv_parts, w["conv_w"], m["conv_w"], v["conv_w"], CONV_K)
    for k in ("w_in", "w_gate_up"):
        res[k] = [jnp.swapaxes(a, 1, 2) for a in res[k]]

    return (loss, dx[None], *[res[k][0] for k in names], *[res[k][1] for k in names],
            *[res[k][2] for k in names], *[res[k][3] for k in names])
```
